```python
import math
import jax
import jax.numpy as jnp
from jax import lax
import numpy as np

D_MODEL = 1024
BATCH = 8
SEQ = 8192
DEPTH = 1

N_MOD = 6
EPS = 1e-6
GM_CHUNK = 128
GM_HEAD = 128
GM_GROUPS = D_MODEL // GM_HEAD
GM_WIDTH = GM_GROUPS * GM_HEAD
SSM_INNER = 2 * D_MODEL
SSM_HEAD_DIM = 64
SSM_HEADS = SSM_INNER // SSM_HEAD_DIM
SSM_GROUPS = 8
SSM_STATE = 128
SSM_CONV = 4
SSM_CHUNK = 128
CONV_DIM = SSM_INNER + 2 * SSM_GROUPS * SSM_STATE
D_FF = 4 * D_MODEL
IN_SIZES = (GM_WIDTH, GM_WIDTH, SSM_INNER, CONV_DIM, SSM_HEADS, D_MODEL, D_MODEL)
IN_WIDTH = sum(IN_SIZES)

kernel_name = 'hybrid_sgu_ssd_block'


def _split_offsets(sizes):
    offs, acc = [], 0
    for s in sizes[:-1]:
        acc += s
        offs.append(acc)
    return offs


def rms_norm(x, w=None):
    xf = x.astype(jnp.float32)
    y = xf * lax.rsqrt(jnp.mean(jnp.square(xf), axis=-1, keepdims=True) + EPS)
    if w is not None:
        y = y * w.astype(jnp.float32)
    return y.astype(x.dtype)


def gated_group_rms_norm(y, z, w, groups):
    g = y.astype(jnp.float32) * jax.nn.silu(z.astype(jnp.float32))
    gs = g.reshape(*g.shape[:-1], groups, -1)
    gs = gs * lax.rsqrt(jnp.mean(jnp.square(gs), axis=-1, keepdims=True) + EPS)
    return (gs.reshape(g.shape) * w.astype(jnp.float32)).astype(y.dtype)


def sgu_branch(u, v, norm_w, ws, bs):
    b, s, _ = v.shape
    u = jax.nn.gelu(u)
    v = rms_norm(jax.nn.gelu(v), norm_w)
    vc = v.reshape(b, s // GM_CHUNK, GM_CHUNK, GM_GROUPS, GM_HEAD)
    causal = jnp.tril(jnp.ones((GM_CHUNK, GM_CHUNK), dtype=bool))
    ws_c = jnp.where(causal[None], ws, jnp.zeros_like(ws))
    sv = jnp.einsum('gij,bnjgd->bnigd', ws_c, vc) + bs.T[None, None, :, :, None]
    return u * sv.reshape(b, s, GM_WIDTH)


def ssd_scan(xs, dt, A, Bm, Cm):
    b, s, h, p = xs.shape
    g, n = Bm.shape[-2], Bm.shape[-1]
    hpg = h // g
    q = SSM_CHUNK
    nc = s // q

    def to_chunks(t):
        return jnp.swapaxes(t.reshape(b, nc, q, *t.shape[2:]), 0, 1)

    xdt = (xs.astype(jnp.float32) * dt[..., None]).reshape(b, s, g, hpg, p)
    a = (dt * A).reshape(b, s, g, hpg)
    causal = jnp.tril(jnp.ones((q, q), dtype=bool))

    def step(state, inp):
        x_c, a_c, B_c, C_c = inp
        cum = jnp.cumsum(a_c, axis=1)
        cum_t = jnp.moveaxis(cum, 1, -1)
        seg = cum_t[..., :, None] - cum_t[..., None, :]
        decay = jnp.exp(jnp.where(causal, seg, -jnp.inf))
        cb = jnp.einsum('bign,bjgn->bgij', C_c, B_c)
        y_intra = jnp.einsum('bgij,bghij,bjghp->bighp', cb, decay, x_c)
        y_inter = jnp.einsum('bign,bghpn->bighp', C_c, state) * jnp.exp(cum)[..., None]
        last = cum[:, -1]
        w_end = jnp.exp(last[:, None] - cum)
        new_state = state * jnp.exp(last)[..., None, None] + jnp.einsum(
            'bjgn,bjgh,bjghp->bghpn', B_c, w_end, x_c)
        return new_state, y_intra + y_inter

    state0 = jnp.zeros((b, g, hpg, p, n), jnp.float32)
    _, ys = lax.scan(step, state0, (to_chunks(xdt), to_chunks(a),
                                    to_chunks(Bm.astype(jnp.float32)),
                                    to_chunks(Cm.astype(jnp.float32))))
    return jnp.swapaxes(ys, 0, 1).reshape(b, s, h, p)


def ssd_branch(z, xbc, dt_raw, conv_w, conv_b, dt_bias, a_log, d_skip, norm_w):
    b, s, _ = xbc.shape
    xbc = lax.conv_general_dilated(
        xbc, conv_w[:, None, :], window_strides=(1,), padding=[(SSM_CONV - 1, 0)],
        dimension_numbers=('NWC', 'WIO', 'NWC'), feature_group_count=CONV_DIM) + conv_b
    xbc = jax.nn.silu(xbc)
    gn = SSM_GROUPS * SSM_STATE
    xs = xbc[..., :SSM_INNER].reshape(b, s, SSM_HEADS, SSM_HEAD_DIM)
    Bm = xbc[..., SSM_INNER:SSM_INNER + gn].reshape(b, s, SSM_GROUPS, SSM_STATE)
    Cm = xbc[..., SSM_INNER + gn:].reshape(b, s, SSM_GROUPS, SSM_STATE)
    dt = jax.nn.softplus(dt_raw.astype(jnp.float32) + dt_bias.astype(jnp.float32))
    A = -jnp.exp(a_log.astype(jnp.float32))
    y = ssd_scan(xs, dt, A, Bm, Cm)
    y = y + xs.astype(jnp.float32) * d_skip.astype(jnp.float32)[:, None]
    y = y.reshape(b, s, SSM_INNER).astype(z.dtype)
    return gated_group_rms_norm(y, z, norm_w, SSM_GROUPS)


def _fwd_setup_inputs(seed: int = 0) -> dict:
    key = jax.random.key(seed)
    ks = jax.random.split(key, 22)
    f = jnp.float32
    L = DEPTH

    def nrm(k, shape, fan_in):
        return jax.random.normal(k, shape, f) * (fan_in ** -0.5)

    x = jax.random.normal(ks[0], (BATCH, SEQ, D_MODEL), f)
    c = jax.random.normal(ks[1], (BATCH, D_MODEL), f)
    w_mod = 0.5 * nrm(ks[2], (L, D_MODEL, N_MOD * D_MODEL), D_MODEL)
    b_mod = 0.01 * jax.random.normal(ks[3], (L, N_MOD * D_MODEL), f)
    w_in = nrm(ks[4], (L, D_MODEL, IN_WIDTH), D_MODEL)
    gm_norm_w = 1.0 + 0.05 * jax.random.normal(ks[5], (L, GM_WIDTH), f)
    gm_ws = nrm(ks[6], (L, GM_GROUPS, GM_CHUNK, GM_CHUNK), GM_CHUNK)
    gm_bs = 1.0 + 0.1 * jax.random.normal(ks[7], (L, GM_GROUPS, GM_CHUNK), f)
    conv_w = nrm(ks[8], (L, SSM_CONV, CONV_DIM), SSM_CONV)
    conv_b = 0.01 * jax.random.normal(ks[9], (L, CONV_DIM), f)
    dt0 = jnp.exp(jax.random.uniform(ks[10], (L, SSM_HEADS), f,
                                     minval=math.log(1e-3), maxval=math.log(1e-1)))
    dt_bias = dt0 + jnp.log(-jnp.expm1(-dt0))
    a_log = jnp.log(jax.random.uniform(ks[11], (L, SSM_HEADS), f, minval=1.0, maxval=16.0))
    d_skip = 1.0 + 0.1 * jax.random.normal(ks[12], (L, SSM_HEADS), f)
    ssm_norm_w = 1.0 + 0.05 * jax.random.normal(ks[13], (L, SSM_INNER), f)
    w_branch_gm = nrm(ks[14], (L, GM_WIDTH, D_MODEL), GM_WIDTH)
    w_branch_ssm = nrm(ks[15], (L, SSM_INNER, D_MODEL), SSM_INNER)
    w_out = nrm(ks[16], (L, D_MODEL, D_MODEL), D_MODEL)
    w_ff1 = nrm(ks[17], (L, D_MODEL, D_FF), D_MODEL)
    w_ff2 = nrm(ks[18], (L, D_FF, D_MODEL), D_FF)
    final_norm_w = 1.0 + 0.05 * jax.random.normal(ks[19], (D_MODEL,), f)
    return {'x': x, 'c': c, 'w_mod': w_mod, 'b_mod': b_mod, 'w_in': w_in,
            'gm_norm_w': gm_norm_w, 'gm_ws': gm_ws, 'gm_bs': gm_bs,
            'conv_w': conv_w, 'conv_b': conv_b, 'dt_bias': dt_bias, 'a_log': a_log,
            'd_skip': d_skip, 'ssm_norm_w': ssm_norm_w, 'w_branch_gm': w_branch_gm,
            'w_branch_ssm': w_branch_ssm, 'w_out': w_out, 'w_ff1': w_ff1,
            'w_ff2': w_ff2, 'final_norm_w': final_norm_w}


def _fwd_reference(x, c, w_mod, b_mod, w_in, gm_norm_w, gm_ws, gm_bs, conv_w, conv_b,
              dt_bias, a_log, d_skip, ssm_norm_w, w_branch_gm, w_branch_ssm, w_out,
              w_ff1, w_ff2, final_norm_w):
    c_act = jax.nn.silu(c)
    offs = _split_offsets(IN_SIZES)
    for l in range(DEPTH):
        mod = (c_act @ w_mod[l] + b_mod[l])[:, None, :]
        sh1, sc1, g1, sh2, sc2, g2 = jnp.split(mod, N_MOD, axis=-1)
        h = rms_norm(x) * (1.0 + sc1) + sh1
        proj = h @ w_in[l]
        u, v, z, xbc, dt_raw, gate_a, gate_b = jnp.split(proj, offs, axis=-1)
        y_a = sgu_branch(u, v, gm_norm_w[l], gm_ws[l], gm_bs[l])
        y_b = ssd_branch(z, xbc, dt_raw, conv_w[l], conv_b[l], dt_bias[l], a_log[l],
                         d_skip[l], ssm_norm_w[l])
        mixed = (jax.nn.sigmoid(gate_a) * (y_a @ w_branch_gm[l])
                 + jax.nn.sigmoid(gate_b) * (y_b @ w_branch_ssm[l]))
        x = x + g1 * (mixed @ w_out[l])
        h2 = rms_norm(x) * (1.0 + sc2) + sh2
        x = x + g2 * (jnp.square(jax.nn.relu(h2 @ w_ff1[l])) @ w_ff2[l])
    return rms_norm(x, final_norm_w)


import jax as _jax
import jax.numpy as _jnp

TWIN_FORMAT = 'train_step'
FWD_PARAMS = ['x', 'c', 'w_mod', 'b_mod', 'w_in', 'gm_norm_w', 'gm_ws', 'gm_bs', 'conv_w', 'conv_b', 'dt_bias', 'a_log', 'd_skip', 'ssm_norm_w', 'w_branch_gm', 'w_branch_ssm', 'w_out', 'w_ff1', 'w_ff2', 'final_norm_w']
TWIN_WEIGHTS = ['w_mod', 'b_mod', 'w_in', 'gm_norm_w', 'gm_ws', 'gm_bs', 'conv_w', 'conv_b', 'dt_bias', 'a_log', 'd_skip', 'ssm_norm_w', 'w_branch_gm', 'w_branch_ssm', 'w_out', 'w_ff1', 'w_ff2', 'final_norm_w']
TWIN_DIFF_INPUT = 'x'
TWIN_INPUTS = ['x', 'c', 'w_mod', 'b_mod', 'w_in', 'gm_norm_w', 'gm_ws', 'gm_bs', 'conv_w', 'conv_b', 'dt_bias', 'a_log', 'd_skip', 'ssm_norm_w', 'w_branch_gm', 'w_branch_ssm', 'w_out', 'w_ff1', 'w_ff2', 'final_norm_w', 'loss_target', 'm_w_mod', 'm_b_mod', 'm_w_in', 'm_gm_norm_w', 'm_gm_ws', 'm_gm_bs', 'm_conv_w', 'm_conv_b', 'm_dt_bias', 'm_a_log', 'm_d_skip', 'm_ssm_norm_w', 'm_w_branch_gm', 'm_w_branch_ssm', 'm_w_out', 'm_w_ff1', 'm_w_ff2', 'm_final_norm_w', 'v_w_mod', 'v_b_mod', 'v_w_in', 'v_gm_norm_w', 'v_gm_ws', 'v_gm_bs', 'v_conv_w', 'v_conv_b', 'v_dt_bias', 'v_a_log', 'v_d_skip', 'v_ssm_norm_w', 'v_w_branch_gm', 'v_w_branch_ssm', 'v_w_out', 'v_w_ff1', 'v_w_ff2', 'v_final_norm_w']
TWIN_OUTPUTS = ['loss', 'grad_x', 'grad_w_mod', 'grad_b_mod', 'grad_w_in', 'grad_gm_norm_w', 'grad_gm_ws', 'grad_gm_bs', 'grad_conv_w', 'grad_conv_b', 'grad_dt_bias', 'grad_a_log', 'grad_d_skip', 'grad_ssm_norm_w', 'grad_w_branch_gm', 'grad_w_branch_ssm', 'grad_w_out', 'grad_w_ff1', 'grad_w_ff2', 'grad_final_norm_w', 'delta_w_mod', 'delta_b_mod', 'delta_w_in', 'delta_gm_norm_w', 'delta_gm_ws', 'delta_gm_bs', 'delta_conv_w', 'delta_conv_b', 'delta_dt_bias', 'delta_a_log', 'delta_d_skip', 'delta_ssm_norm_w', 'delta_w_branch_gm', 'delta_w_branch_ssm', 'delta_w_out', 'delta_w_ff1', 'delta_w_ff2', 'delta_final_norm_w', 'new_m_w_mod', 'new_m_b_mod', 'new_m_w_in', 'new_m_gm_norm_w', 'new_m_gm_ws', 'new_m_gm_bs', 'new_m_conv_w', 'new_m_conv_b', 'new_m_dt_bias', 'new_m_a_log', 'new_m_d_skip', 'new_m_ssm_norm_w', 'new_m_w_branch_gm', 'new_m_w_branch_ssm', 'new_m_w_out', 'new_m_w_ff1', 'new_m_w_ff2', 'new_m_final_norm_w', 'new_v_w_mod', 'new_v_b_mod', 'new_v_w_in', 'new_v_gm_norm_w', 'new_v_gm_ws', 'new_v_gm_bs', 'new_v_conv_w', 'new_v_conv_b', 'new_v_dt_bias', 'new_v_a_log', 'new_v_d_skip', 'new_v_ssm_norm_w', 'new_v_w_branch_gm', 'new_v_w_branch_ssm', 'new_v_w_out', 'new_v_w_ff1', 'new_v_w_ff2', 'new_v_final_norm_w']
TWIN_LEAF_KINDS = {'loss': 'loss', 'grad_x': 'grad_x', 'grad_w_mod': 'grad_w', 'grad_b_mod': 'grad_w', 'grad_w_in': 'grad_w', 'grad_gm_norm_w': 'grad_w', 'grad_gm_ws': 'grad_w', 'grad_gm_bs': 'grad_w', 'grad_conv_w': 'grad_w', 'grad_conv_b': 'grad_w', 'grad_dt_bias': 'grad_w', 'grad_a_log': 'grad_w', 'grad_d_skip': 'grad_w', 'grad_ssm_norm_w': 'grad_w', 'grad_w_branch_gm': 'grad_w', 'grad_w_branch_ssm': 'grad_w', 'grad_w_out': 'grad_w', 'grad_w_ff1': 'grad_w', 'grad_w_ff2': 'grad_w', 'grad_final_norm_w': 'grad_w', 'delta_w_mod': 'delta_w', 'delta_b_mod': 'delta_w', 'delta_w_in': 'delta_w', 'delta_gm_norm_w': 'delta_w', 'delta_gm_ws': 'delta_w', 'delta_gm_bs': 'delta_w', 'delta_conv_w': 'delta_w', 'delta_conv_b': 'delta_w', 'delta_dt_bias': 'delta_w', 'delta_a_log': 'delta_w', 'delta_d_skip': 'delta_w', 'delta_ssm_norm_w': 'delta_w', 'delta_w_branch_gm': 'delta_w', 'delta_w_branch_ssm': 'delta_w', 'delta_w_out': 'delta_w', 'delta_w_ff1': 'delta_w', 'delta_w_ff2': 'delta_w', 'delta_final_norm_w': 'delta_w', 'new_m_w_mod': 'new_m', 'new_m_b_mod': 'new_m', 'new_m_w_in': 'new_m', 'new_m_gm_norm_w': 'new_m', 'new_m_gm_ws': 'new_m', 'new_m_gm_bs': 'new_m', 'new_m_conv_w': 'new_m', 'new_m_conv_b': 'new_m', 'new_m_dt_bias': 'new_m', 'new_m_a_log': 'new_m', 'new_m_d_skip': 'new_m', 'new_m_ssm_norm_w': 'new_m', 'new_m_w_branch_gm': 'new_m', 'new_m_w_branch_ssm': 'new_m', 'new_m_w_out': 'new_m', 'new_m_w_ff1': 'new_m', 'new_m_w_ff2': 'new_m', 'new_m_final_norm_w': 'new_m', 'new_v_w_mod': 'new_v', 'new_v_b_mod': 'new_v', 'new_v_w_in': 'new_v', 'new_v_gm_norm_w': 'new_v', 'new_v_gm_ws': 'new_v', 'new_v_gm_bs': 'new_v', 'new_v_conv_w': 'new_v', 'new_v_conv_b': 'new_v', 'new_v_dt_bias': 'new_v', 'new_v_a_log': 'new_v', 'new_v_d_skip': 'new_v', 'new_v_ssm_norm_w': 'new_v', 'new_v_w_branch_gm': 'new_v', 'new_v_w_branch_ssm': 'new_v', 'new_v_w_out': 'new_v', 'new_v_w_ff1': 'new_v', 'new_v_w_ff2': 'new_v', 'new_v_final_norm_w': 'new_v'}


def _forward(args):
    return _fwd_reference(*[args[k] for k in FWD_PARAMS])


def _output_shape():
    out = _jax.eval_shape(lambda: _forward(_fwd_setup_inputs(0)))
    return out.shape, out.dtype

N_MICROBATCH = 1
ADAM_LR = 0.001
ADAM_B1 = 0.9
ADAM_B2 = 0.999
ADAM_EPS = 1e-08
ADAM_WD = 0.01
ADAM_STEP = 10
PER_EXAMPLE_BATCH_AXIS = {'x': 0, 'c': 0, 'loss_target': 0}
SHARED_INPUTS = []
_WEIGHT_DTYPES = {'w_mod': _jnp.float32, 'b_mod': _jnp.float32, 'w_in': _jnp.float32, 'gm_norm_w': _jnp.float32, 'gm_ws': _jnp.float32, 'gm_bs': _jnp.float32, 'conv_w': _jnp.float32, 'conv_b': _jnp.float32, 'dt_bias': _jnp.float32, 'a_log': _jnp.float32, 'd_skip': _jnp.float32, 'ssm_norm_w': _jnp.float32, 'w_branch_gm': _jnp.float32, 'w_branch_ssm': _jnp.float32, 'w_out': _jnp.float32, 'w_ff1': _jnp.float32, 'w_ff2': _jnp.float32, 'final_norm_w': _jnp.float32}
MOMENT_SCALE = {'w_mod': 2.595121e-01, 'b_mod': 5.281226e-01, 'w_in': 2.602395e-02, 'gm_norm_w': 2.243963e-02, 'gm_ws': 2.182490e-02, 'gm_bs': 3.146316e-02, 'conv_w': 2.346620e-02, 'conv_b': 3.501296e-02, 'dt_bias': 8.106647e-02, 'a_log': 2.343082e-01, 'd_skip': 1.464788e-01, 'ssm_norm_w': 3.286985e-02, 'w_branch_gm': 4.072797e-02, 'w_branch_ssm': 4.361928e-02, 'w_out': 6.027274e-02, 'w_ff1': 6.051473e-02, 'w_ff2': 1.430101e-01, 'final_norm_w': 6.444141e+01}


def _to_microbatches(a, axis):
    t = _jnp.moveaxis(a, axis, 0)
    t = t.reshape((N_MICROBATCH, t.shape[0] // N_MICROBATCH) + t.shape[1:])
    return _jnp.moveaxis(t, 1, axis + 1)


def setup_inputs(seed: int = 0) -> dict:
    inp = _fwd_setup_inputs(seed)
    key = _jax.random.fold_in(_jax.random.key(seed), 7919)
    shape, _ = _output_shape()
    out = dict(inp)
    out["loss_target"] = _jax.random.normal(_jax.random.fold_in(key, 0), shape, _jnp.float32)
    for i, name in enumerate(TWIN_WEIGHTS):
        w = inp[name].astype(_jnp.float32)
        if MOMENT_SCALE is None:
            s = _jnp.sqrt(_jnp.mean(_jnp.square(w)) + 1e-30)
        else:
            s = MOMENT_SCALE[name]
        km, kv = _jax.random.split(_jax.random.fold_in(key, i + 1))
        out[name] = w
        out["m_" + name] = s * _jax.random.normal(km, w.shape, _jnp.float32)
        out["v_" + name] = (s * s) * _jax.random.uniform(kv, w.shape, _jnp.float32, 0.5, 1.5)
    if N_MICROBATCH > 1:
        for name, axis in PER_EXAMPLE_BATCH_AXIS.items():
            out[name] = _to_microbatches(out[name], axis)
    return {'x': out['x'], 'c': out['c'], 'w_mod': out['w_mod'], 'b_mod': out['b_mod'], 'w_in': out['w_in'], 'gm_norm_w': out['gm_norm_w'], 'gm_ws': out['gm_ws'], 'gm_bs': out['gm_bs'], 'conv_w': out['conv_w'], 'conv_b': out['conv_b'], 'dt_bias': out['dt_bias'], 'a_log': out['a_log'], 'd_skip': out['d_skip'], 'ssm_norm_w': out['ssm_norm_w'], 'w_branch_gm': out['w_branch_gm'], 'w_branch_ssm': out['w_branch_ssm'], 'w_out': out['w_out'], 'w_ff1': out['w_ff1'], 'w_ff2': out['w_ff2'], 'final_norm_w': out['final_norm_w'], 'loss_target': out['loss_target'], 'm_w_mod': out['m_w_mod'], 'm_b_mod': out['m_b_mod'], 'm_w_in': out['m_w_in'], 'm_gm_norm_w': out['m_gm_norm_w'], 'm_gm_ws': out['m_gm_ws'], 'm_gm_bs': out['m_gm_bs'], 'm_conv_w': out['m_conv_w'], 'm_conv_b': out['m_conv_b'], 'm_dt_bias': out['m_dt_bias'], 'm_a_log': out['m_a_log'], 'm_d_skip': out['m_d_skip'], 'm_ssm_norm_w': out['m_ssm_norm_w'], 'm_w_branch_gm': out['m_w_branch_gm'], 'm_w_branch_ssm': out['m_w_branch_ssm'], 'm_w_out': out['m_w_out'], 'm_w_ff1': out['m_w_ff1'], 'm_w_ff2': out['m_w_ff2'], 'm_final_norm_w': out['m_final_norm_w'], 'v_w_mod': out['v_w_mod'], 'v_b_mod': out['v_b_mod'], 'v_w_in': out['v_w_in'], 'v_gm_norm_w': out['v_gm_norm_w'], 'v_gm_ws': out['v_gm_ws'], 'v_gm_bs': out['v_gm_bs'], 'v_conv_w': out['v_conv_w'], 'v_conv_b': out['v_conv_b'], 'v_dt_bias': out['v_dt_bias'], 'v_a_log': out['v_a_log'], 'v_d_skip': out['v_d_skip'], 'v_ssm_norm_w': out['v_ssm_norm_w'], 'v_w_branch_gm': out['v_w_branch_gm'], 'v_w_branch_ssm': out['v_w_branch_ssm'], 'v_w_out': out['v_w_out'], 'v_w_ff1': out['v_w_ff1'], 'v_w_ff2': out['v_w_ff2'], 'v_final_norm_w': out['v_final_norm_w']}


def _loss(weights, diff, rest, loss_target):
    with _jax.named_scope("forward"):
        args = {**rest, TWIN_DIFF_INPUT: diff, **{k: w.astype(_WEIGHT_DTYPES[k]) for k, w in weights.items()}}
        y = _forward(args)
    with _jax.named_scope("loss_head"):
        err = _jnp.square(y.astype(_jnp.float32) - loss_target)
        return 0.5 * _jnp.sum(_jnp.mean(err, axis=-1)) if err.ndim else 0.5 * err


def _adamw(w, g, m, v):
    m = ADAM_B1 * m + (1.0 - ADAM_B1) * g
    v = ADAM_B2 * v + (1.0 - ADAM_B2) * _jnp.square(g)
    m_hat = m / (1.0 - ADAM_B1 ** ADAM_STEP)
    v_hat = v / (1.0 - ADAM_B2 ** ADAM_STEP)
    delta = -ADAM_LR * (m_hat / (_jnp.sqrt(v_hat) + ADAM_EPS) + ADAM_WD * w)
    return delta, m, v


def reference(x, c, w_mod, b_mod, w_in, gm_norm_w, gm_ws, gm_bs, conv_w, conv_b, dt_bias, a_log, d_skip, ssm_norm_w, w_branch_gm, w_branch_ssm, w_out, w_ff1, w_ff2, final_norm_w, loss_target, m_w_mod, m_b_mod, m_w_in, m_gm_norm_w, m_gm_ws, m_gm_bs, m_conv_w, m_conv_b, m_dt_bias, m_a_log, m_d_skip, m_ssm_norm_w, m_w_branch_gm, m_w_branch_ssm, m_w_out, m_w_ff1, m_w_ff2, m_final_norm_w, v_w_mod, v_b_mod, v_w_in, v_gm_norm_w, v_gm_ws, v_gm_bs, v_conv_w, v_conv_b, v_dt_bias, v_a_log, v_d_skip, v_ssm_norm_w, v_w_branch_gm, v_w_branch_ssm, v_w_out, v_w_ff1, v_w_ff2, v_final_norm_w):
    given = dict(x=x, c=c, w_mod=w_mod, b_mod=b_mod, w_in=w_in, gm_norm_w=gm_norm_w, gm_ws=gm_ws, gm_bs=gm_bs, conv_w=conv_w, conv_b=conv_b, dt_bias=dt_bias, a_log=a_log, d_skip=d_skip, ssm_norm_w=ssm_norm_w, w_branch_gm=w_branch_gm, w_branch_ssm=w_branch_ssm, w_out=w_out, w_ff1=w_ff1, w_ff2=w_ff2, final_norm_w=final_norm_w, loss_target=loss_target, m_w_mod=m_w_mod, m_b_mod=m_b_mod, m_w_in=m_w_in, m_gm_norm_w=m_gm_norm_w, m_gm_ws=m_gm_ws, m_gm_bs=m_gm_bs, m_conv_w=m_conv_w, m_conv_b=m_conv_b, m_dt_bias=m_dt_bias, m_a_log=m_a_log, m_d_skip=m_d_skip, m_ssm_norm_w=m_ssm_norm_w, m_w_branch_gm=m_w_branch_gm, m_w_branch_ssm=m_w_branch_ssm, m_w_out=m_w_out, m_w_ff1=m_w_ff1, m_w_ff2=m_w_ff2, m_final_norm_w=m_final_norm_w, v_w_mod=v_w_mod, v_b_mod=v_b_mod, v_w_in=v_w_in, v_gm_norm_w=v_gm_norm_w, v_gm_ws=v_gm_ws, v_gm_bs=v_gm_bs, v_conv_w=v_conv_w, v_conv_b=v_conv_b, v_dt_bias=v_dt_bias, v_a_log=v_a_log, v_d_skip=v_d_skip, v_ssm_norm_w=v_ssm_norm_w, v_w_branch_gm=v_w_branch_gm, v_w_branch_ssm=v_w_branch_ssm, v_w_out=v_w_out, v_w_ff1=v_w_ff1, v_w_ff2=v_w_ff2, v_final_norm_w=v_final_norm_w)
    weights = {n: given[n] for n in TWIN_WEIGHTS}
    shared = {n: given[n] for n in SHARED_INPUTS}
    per_example = {n: given[n] for n in ['x', 'c']}
    grad_fn = _jax.value_and_grad(_loss, argnums=(0, 1))

    def one_microbatch(ex, loss_target):
        ex = dict(ex)
        diff = ex.pop(TWIN_DIFF_INPUT)
        return grad_fn(weights, diff, {**shared, **ex}, loss_target)

    if N_MICROBATCH == 1:
        loss, (grad_w, grad_x) = one_microbatch(per_example, given["loss_target"])
    else:
        def body(carry, xs):
            loss_sum, grad_sum = carry
            l_k, (gw_k, gx_k) = one_microbatch(xs[0], xs[1])
            with _jax.named_scope("update"):
                return (loss_sum + l_k, _jax.tree.map(_jnp.add, grad_sum, gw_k)), gx_k

        init = (_jnp.zeros((), _jnp.float32), _jax.tree.map(_jnp.zeros_like, weights))
        (loss, grad_w), grad_x = _jax.lax.scan(body, init, (per_example, given["loss_target"]))
    with _jax.named_scope("update"):
        delta_w, new_m, new_v = {}, {}, {}
        for n in TWIN_WEIGHTS:
            delta_w[n], new_m[n], new_v[n] = _adamw(weights[n], grad_w[n], given["m_" + n], given["v_" + n])
    return (loss, grad_x, *[grad_w[n] for n in TWIN_WEIGHTS], *[delta_w[n] for n in TWIN_WEIGHTS],
            *[new_m[n] for n in TWIN_WEIGHTS], *[new_v[n] for n in TWIN_WEIGHTS])
```

```python
import functools

import jax
import jax.numpy as jnp
from jax import lax
from jax.experimental import pallas as pl
from jax.experimental.pallas import tpu as pltpu

F32 = jnp.float32
BF16 = jnp.bfloat16
MESH = pl.DeviceIdType.MESH
HIGHEST = lax.Precision.HIGHEST

N_DEV = 8
D = 1024
Q = 128
GM_GROUPS = 8
SSM_INNER = 2048
SSM_GROUPS = 8
SSM_HPG = 4
SSM_P = 64
SSM_GW = SSM_HPG * SSM_P
CONV_DIM = 4096
CONV_K = 4
D_FF = 4096
N_MOD = 6
EPS = 1e-6
IN_WIDTH = 10272
OFF_DT = 8192
OFF_GA = 8224
PROJ_W = 10240
ALL_W = 10368
P_U, P_V, P_Z, P_XBC, P_GA, P_GB = 0, 1024, 2048, 4096, 8192, 9216

ADAM_LR = 0.001
ADAM_B1 = 0.9
ADAM_B2 = 0.999
ADAM_EPS = 1e-08
ADAM_WD = 0.01
ADAM_STEP = 10

VMEM_LIMIT_BYTES = 48 * 1024 * 1024
PACK_W_ROWS = 2880
SMALL_ROWS = 1296


def _params(sem=None):
    return pltpu.CompilerParams(dimension_semantics=sem, vmem_limit_bytes=VMEM_LIMIT_BYTES)


def _dg(a, b, ca, cb):
    return lax.dot_general(a.astype(BF16), b.astype(BF16), (((ca,), (cb,)), ((), ())),
                           preferred_element_type=F32)


@jax.custom_vjp
def dot_nn(a, b):
    return _dg(a, b, 1, 0)


@jax.custom_vjp
def dot_nt(a, b):
    return _dg(a, b, 1, 1)


@jax.custom_vjp
def dot_tn(a, b):
    return _dg(a, b, 0, 0)


dot_nn.defvjp(lambda a, b: (dot_nn(a, b), (a, b)),
              lambda r, g: (dot_nt(g, r[1]), dot_tn(r[0], g)))
dot_nt.defvjp(lambda a, b: (dot_nt(a, b), (a, b)),
              lambda r, g: (dot_nn(g, r[1]), dot_tn(g, r[0])))
dot_tn.defvjp(lambda a, b: (dot_tn(a, b), (a, b)),
              lambda r, g: (dot_nt(r[1], g), dot_nn(r[0], g)))


def _rms(x):
    return x * lax.rsqrt(jnp.mean(x * x, axis=-1, keepdims=True) + EPS)


def _softplus(x):
    return jnp.maximum(x, 0.0) + jnp.log1p(jnp.exp(-jnp.abs(x)))


def exchange(src, *, gather, name):
    blk_shape = src.shape if gather else src.shape[1:]

    def body(src_ref, out_ref, send_sems, recv_sems, local_sem):
        x = lax.axis_index("x")
        y = lax.axis_index("y")
        c = lax.axis_index("c")
        me = 4 * x + 2 * y + c

        def blk(j):
            return src_ref if gather else src_ref.at[j]

        local = pltpu.make_async_copy(blk(me), out_ref.at[me], local_sem)
        local.start()
        copies = []
        for k in range(1, N_DEV):
            px = lax.rem(x + ((k >> 2) & 1), 2)
            py = lax.rem(y + ((k >> 1) & 1), 2)
            pc = lax.rem(c + (k & 1), 2)
            peer = 4 * px + 2 * py + pc
            cp = pltpu.make_async_remote_copy(
                src_ref=blk(peer), dst_ref=out_ref.at[me],
                send_sem=send_sems.at[k - 1], recv_sem=recv_sems.at[k - 1],
                device_id=(px, py, pc), device_id_type=MESH)
            cp.start()
            copies.append(cp)
        for cp in copies:
            cp.wait_send()
        for cp in copies:
            cp.wait_recv()
        local.wait()

    return pl.pallas_call(
        body, name=name,
        out_shape=jax.ShapeDtypeStruct((N_DEV,) + tuple(blk_shape), src.dtype),
        in_specs=[pl.BlockSpec(memory_space=pl.ANY)],
        out_specs=pl.BlockSpec(memory_space=pl.ANY),
        scratch_shapes=[pltpu.SemaphoreType.DMA((N_DEV - 1,)),
                        pltpu.SemaphoreType.DMA((N_DEV - 1,)),
                        pltpu.SemaphoreType.DMA(())],
    )(src)


def sum_devices(g, *, tr, name):
    _, R, C = g.shape

    def body(g_ref, o_ref):
        acc = g_ref[0].astype(F32)
        for j in range(1, N_DEV):
            acc = acc + g_ref[j].astype(F32)
        o_ref[...] = acc

    return pl.pallas_call(
        body, name=name, grid=(R // tr,),
        out_shape=jax.ShapeDtypeStruct((R, C), F32),
        in_specs=[pl.BlockSpec((N_DEV, tr, C), lambda i: (0, i, 0))],
        out_specs=pl.BlockSpec((tr, C), lambda i: (i, 0)),
        compiler_params=_params(("parallel",)),
    )(g)


def matmul(a, b, mode, out_dtype, *, name, tm=1024, tn=1024, tk=1024):
    if mode == "nn":
        (M, K), (K2, N) = a.shape, b.shape
    elif mode == "nt":
        (M, K), (N, K2) = a.shape, b.shape
    else:
        (K, M), (K2, N) = a.shape, b.shape
    assert K == K2
    tm, tn, tk = min(tm, M), min(tn, N), min(tk, K)
    assert M % tm == 0 and N % tn == 0 and K % tk == 0, (name, M, N, K, tm, tn, tk)
    nk = K // tk
    if mode == "tn":
        a_spec = pl.BlockSpec((tk, tm), lambda i, j, k: (k, i))
    else:
        a_spec = pl.BlockSpec((tm, tk), lambda i, j, k: (i, k))
    if mode == "nt":
        b_spec = pl.BlockSpec((tn, tk), lambda i, j, k: (j, k))
    else:
        b_spec = pl.BlockSpec((tk, tn), lambda i, j, k: (k, j))
    dims = {"nn": (1, 0), "nt": (1, 1), "tn": (0, 0)}[mode]

    def body(a_ref, b_ref, o_ref, acc_ref):
        k = pl.program_id(2)
        part = lax.dot_general(a_ref[...], b_ref[...], (((dims[0],), (dims[1],)), ((), ())),
                               preferred_element_type=F32)

        @pl.when(k == 0)
        def _():
            acc_ref[...] = part

        @pl.when(k > 0)
        def _():
            acc_ref[...] += part

        @pl.when(k == nk - 1)
        def _():
            o_ref[...] = acc_ref[...].astype(o_ref.dtype)

    return pl.pallas_call(
        body, name=name, grid=(M // tm, N // tn, nk),
        out_shape=jax.ShapeDtypeStruct((M, N), out_dtype),
        in_specs=[a_spec, b_spec],
        out_specs=pl.BlockSpec((tm, tn), lambda i, j, k: (i, j)),
        scratch_shapes=[pltpu.VMEM((tm, tn), F32)],
        compiler_params=_params(("parallel", "parallel", "arbitrary")),
    )(a, b)


def rowwise_call(body_fn, rows, fulls, row_outs, acc_outs, *, tm, name):
    rows = [r if isinstance(r, tuple) else (r, r.shape[1], 0) for r in rows]
    T = rows[0][0].shape[0]
    tm = min(tm, T)
    assert T % tm == 0
    n_r, n_f, n_ro = len(rows), len(fulls), len(row_outs)

    def body(*refs):
        r_refs = refs[:n_r]
        f_refs = refs[n_r:n_r + n_f]
        ro_refs = refs[n_r + n_f:n_r + n_f + n_ro]
        ao_refs = refs[n_r + n_f + n_ro:]
        r_vals = [r[...].astype(F32) for r in r_refs]
        f_vals = [f[...].astype(F32) for f in f_refs]
        ro, ao = body_fn(r_vals, f_vals)
        for ref, v in zip(ro_refs, ro):
            ref[...] = v.astype(ref.dtype)
        if ao_refs:
            @pl.when(pl.program_id(0) == 0)
            def _():
                for ref in ao_refs:
                    ref[...] = jnp.zeros(ref.shape, F32)
            for ref, v in zip(ao_refs, ao):
                ref[...] += v.reshape(ref.shape)

    def full_spec(shape):
        nd = len(shape)
        return pl.BlockSpec(tuple(shape), lambda i: (0,) * nd)

    in_specs = [pl.BlockSpec((tm, w), functools.partial(lambda i, o: (i, o), o=off // w))
                for (_, w, off) in rows]
    in_specs += [full_spec(f.shape) for f in fulls]
    out_specs = [pl.BlockSpec((tm, w), lambda i: (i, 0)) for (w, _) in row_outs]
    out_specs += [full_spec(s) for s in acc_outs]
    out_shape = [jax.ShapeDtypeStruct((T, w), dt) for (w, dt) in row_outs]
    out_shape += [jax.ShapeDtypeStruct(tuple(s), F32) for s in acc_outs]
    return pl.pallas_call(
        body, name=name, grid=(T // tm,),
        out_shape=out_shape, in_specs=in_specs, out_specs=out_specs,
        compiler_params=_params(("arbitrary",)),
    )(*[r[0] for r in rows], *fulls)


def fwd_body(fn):
    return lambda r, f: (fn(*r, *f), ())


def bwd_body(fn, n_rows):
    def body(r, f):
        ins, cots = r[:n_rows], r[n_rows:]
        _, vjp = jax.vjp(fn, *ins, *f)
        g = vjp(tuple(cots))
        return g[:n_rows], g[n_rows:]
    return body


def whole_call(fn, ins, outs, *, name):
    n_in = len(ins)

    def body(*refs):
        res = fn(*[r[...] for r in refs[:n_in]])
        for ref, v in zip(refs[n_in:], res):
            ref[...] = v.astype(ref.dtype)

    return pl.pallas_call(
        body, name=name,
        out_shape=[jax.ShapeDtypeStruct(tuple(s), dt) for (s, dt) in outs],
        compiler_params=_params(),
    )(*ins)


def fn_modulate(x, sc, sh):
    return (_rms(x) * (1.0 + sc) + sh,)


def fn_sgu(u, v, nw, ws, bs):
    ug = jax.nn.gelu(u)
    vn = _rms(jax.nn.gelu(v)) * nw
    ri = lax.broadcasted_iota(jnp.int32, (Q, Q), 0)
    ci = lax.broadcasted_iota(jnp.int32, (Q, Q), 1)
    causal = ri >= ci
    chunks = []
    for n in range(u.shape[0] // Q):
        vc = vn[n * Q:(n + 1) * Q]
        cols = [dot_nn(jnp.where(causal, ws[g], 0.0), vc[:, g * Q:(g + 1) * Q]) + bs[g]
                for g in range(GM_GROUPS)]
        chunks.append(jnp.concatenate(cols, axis=1))
    sv = chunks[0] if len(chunks) == 1 else jnp.concatenate(chunks, axis=0)
    return (ug * sv,)


def fn_mix(ga, gb, pa, pb):
    return (jax.nn.sigmoid(ga) * pa + jax.nn.sigmoid(gb) * pb,)


def fn_res_modulate(x, o, g1, sc2, sh2):
    x1 = x + g1 * o
    return x1, _rms(x1) * (1.0 + sc2) + sh2


def fn_relu2(f):
    return (jnp.square(jnp.maximum(f, 0.0)),)


def final_body(r, f):
    x1, gf, tgt = r
    g2, fnw = f

    def loss_fn(x1, gf, g2, fnw):
        y = _rms(x1 + g2 * gf) * fnw
        row = 0.5 * jnp.mean(jnp.square(y - tgt), axis=-1, keepdims=True)
        return jnp.sum(row, axis=0, keepdims=True)

    l, vjp = jax.vjp(loss_fn, x1, gf, g2, fnw)
    dx1, dgf, dg2, dfnw = vjp(jnp.ones((1, 1), F32))
    return (dx1, dgf), (jnp.broadcast_to(l, (1, 128)), dg2, dfnw)


def grad_x_body(r, f):
    x, dh, dxa = r
    _, vjp = jax.vjp(fn_modulate, x, *f)
    dx, dsc, dsh = vjp((dh,))
    return (dx + dxa,), (dsc, dsh)


CONV_CW = 128
CONV_PAD = 8


def _conv_pre(xp, w_ref, b_ref, r0, R):
    acc = b_ref[...] + w_ref[0:1, :] * xp[r0 + CONV_PAD - 3:r0 + CONV_PAD - 3 + R, :]
    for k in range(1, CONV_K):
        s = r0 + CONV_PAD - 3 + k
        acc = acc + w_ref[k:k + 1, :] * xp[s:s + R, :]
    return acc


def conv_fwd(proj, conv_w, conv_b):
    T = proj.shape[0]
    R = min(512, T)

    def body(x_ref, w_ref, b_ref, o_ref, xp):
        xp[0:CONV_PAD, :] = jnp.zeros((CONV_PAD, CONV_CW), F32)
        xp[CONV_PAD:CONV_PAD + T, :] = x_ref[...].astype(F32)
        for r0 in range(0, T, R):
            pre = _conv_pre(xp, w_ref, b_ref, r0, R)
            o_ref[r0:r0 + R, :] = (pre * jax.nn.sigmoid(pre)).astype(o_ref.dtype)

    return pl.pallas_call(
        body, name="conv_fwd", grid=(CONV_DIM // CONV_CW,),
        out_shape=jax.ShapeDtypeStruct((T, CONV_DIM), BF16),
        in_specs=[pl.BlockSpec((T, CONV_CW), lambda j: (0, P_XBC // CONV_CW + j)),
                  pl.BlockSpec((CONV_K, CONV_CW), lambda j: (0, j)),
                  pl.BlockSpec((1, CONV_CW), lambda j: (0, j))],
        out_specs=pl.BlockSpec((T, CONV_CW), lambda j: (0, j)),
        scratch_shapes=[pltpu.VMEM((T + CONV_PAD, CONV_CW), F32)],
        compiler_params=_params(("parallel",)),
    )(proj, conv_w, conv_b)


def conv_bwd(proj, dact, conv_w, conv_b):
    T = proj.shape[0]
    R = min(512, T)

    def body(x_ref, d_ref, w_ref, b_ref, dx_ref, dw_ref, db_ref, xp, dp):
        xp[0:CONV_PAD, :] = jnp.zeros((CONV_PAD, CONV_CW), F32)
        xp[CONV_PAD:CONV_PAD + T, :] = x_ref[...].astype(F32)
        dp[T:T + CONV_PAD, :] = jnp.zeros((CONV_PAD, CONV_CW), F32)
        dws = [jnp.zeros((1, CONV_CW), F32) for _ in range(CONV_K)]
        db = jnp.zeros((1, CONV_CW), F32)
        for r0 in range(0, T, R):
            pre = _conv_pre(xp, w_ref, b_ref, r0, R)
            s = jax.nn.sigmoid(pre)
            dpre = d_ref[r0:r0 + R, :].astype(F32) * (s * (1.0 + pre * (1.0 - s)))
            dp[r0:r0 + R, :] = dpre
            db = db + jnp.sum(dpre, axis=0, keepdims=True)
            for k in range(CONV_K):
                st = r0 + CONV_PAD - 3 + k
                dws[k] = dws[k] + jnp.sum(dpre * xp[st:st + R, :], axis=0, keepdims=True)
        for r0 in range(0, T, R):
            acc = w_ref[0:1, :] * dp[r0 + 3:r0 + 3 + R, :]
            for k in range(1, CONV_K):
                acc = acc + w_ref[k:k + 1, :] * dp[r0 + 3 - k:r0 + 3 - k + R, :]
            dx_ref[r0:r0 + R, :] = acc.astype(dx_ref.dtype)
        for k in range(CONV_K):
            dw_ref[k:k + 1, :] = dws[k]
        db_ref[...] = db

    return pl.pallas_call(
        body, name="conv_bwd", grid=(CONV_DIM // CONV_CW,),
        out_shape=[jax.ShapeDtypeStruct((T, CONV_DIM), BF16),
                   jax.ShapeDtypeStruct((CONV_K, CONV_DIM), F32),
                   jax.ShapeDtypeStruct((1, CONV_DIM), F32)],
        in_specs=[pl.BlockSpec((T, CONV_CW), lambda j: (0, P_XBC // CONV_CW + j)),
                  pl.BlockSpec((T, CONV_CW), lambda j: (0, j)),
                  pl.BlockSpec((CONV_K, CONV_CW), lambda j: (0, j)),
                  pl.BlockSpec((1, CONV_CW), lambda j: (0, j))],
        out_specs=[pl.BlockSpec((T, CONV_CW), lambda j: (0, j)),
                   pl.BlockSpec((CONV_K, CONV_CW), lambda j: (0, j)),
                   pl.BlockSpec((1, CONV_CW), lambda j: (0, j))],
        scratch_shapes=[pltpu.VMEM((T + CONV_PAD, CONV_CW), F32),
                        pltpu.VMEM((T + CONV_PAD, CONV_CW), F32)],
        compiler_params=_params(("parallel",)),
    )(proj, dact, conv_w, conv_b)


def ssd_step(state, x, z, dtr, Bm, Cm, dtb, alog, dsk, nw):
    ri = lax.broadcasted_iota(jnp.int32, (Q, Q), 0)
    ci = lax.broadcasted_iota(jnp.int32, (Q, Q), 1)
    causal = ri >= ci
    eye = ri == ci
    tril = causal.astype(F32)
    lane4 = lax.broadcasted_iota(jnp.int32, (1, SSM_HPG), 1)
    colh = lax.shift_right_logical(lax.broadcasted_iota(jnp.int32, (1, SSM_GW), 1), 6)
    rowh = lax.shift_right_logical(lax.broadcasted_iota(jnp.int32, (SSM_GW, 1), 0), 6)
    rowq = lax.broadcasted_iota(jnp.int32, (Q, 1), 0)

    dt_f = jnp.zeros((Q, SSM_GW), F32)
    a_f = jnp.zeros((Q, SSM_GW), F32)
    dsk_f = jnp.zeros((1, SSM_GW), F32)
    for h in range(SSM_HPG):
        dtr_h = jnp.sum(jnp.where(lane4 == h, dtr, 0.0), axis=1, keepdims=True)
        dt_h = _softplus(dtr_h + dtb[h])
        a_h = dt_h * (-jnp.exp(alog[h]))
        dt_f = jnp.where(colh == h, dt_h, dt_f)
        a_f = jnp.where(colh == h, a_h, a_f)
        dsk_f = jnp.where(colh == h, dsk[h], dsk_f)

    cum_f = jnp.dot(tril, a_f, precision=HIGHEST, preferred_element_type=F32)
    last_f = jnp.sum(jnp.where(rowq == Q - 1, cum_f, 0.0), axis=0, keepdims=True)
    xdt = x * dt_f
    cb = dot_nt(Cm, Bm)

    y = jnp.zeros((Q, SSM_GW), F32)
    elast_rows = jnp.zeros((SSM_GW, 1), F32)
    for h in range(SSM_HPG):
        pick = lax.broadcasted_iota(jnp.int32, (1, SSM_GW), 1) == h * SSM_P
        ch = jnp.sum(jnp.where(pick, cum_f, 0.0), axis=1, keepdims=True)
        ch_t = jnp.sum(jnp.where(eye, ch, 0.0), axis=0, keepdims=True)
        decay = jnp.exp(jnp.where(causal, ch - ch_t, -1e30))
        y = jnp.where(colh == h, dot_nn(cb * decay, xdt), y)
        last_h = jnp.sum(jnp.where(pick, last_f, 0.0), axis=1, keepdims=True)
        elast_rows = jnp.where(rowh == h, jnp.exp(last_h), elast_rows)

    y = y + dot_nt(Cm, state) * jnp.exp(cum_f) + x * dsk_f
    new_state = state * elast_rows + dot_tn(xdt * jnp.exp(last_f - cum_f), Bm)
    gated = y * (z * jax.nn.sigmoid(z))
    return new_state, _rms(gated) * nw


def _ssd_in_specs(rev, nc):
    def n_of(n):
        return nc - 1 - n if rev else n
    return [
        pl.BlockSpec((Q, SSM_GW), lambda g, n: (n_of(n), g)),
        pl.BlockSpec((Q, 128), lambda g, n: (n_of(n), SSM_INNER // 128 + g)),
        pl.BlockSpec((Q, 128), lambda g, n: (n_of(n), SSM_INNER // 128 + SSM_GROUPS + g)),
        pl.BlockSpec((Q, SSM_GW), lambda g, n: (n_of(n), P_Z // SSM_GW + g)),
        pl.BlockSpec((1, Q, SSM_HPG), lambda g, n: (g, n_of(n), 0)),
        pl.BlockSpec((1, SSM_HPG, 1, 1), lambda g, n: (g, 0, 0, 0)),
        pl.BlockSpec((1, SSM_HPG, 1, 1), lambda g, n: (g, 0, 0, 0)),
        pl.BlockSpec((1, SSM_HPG, 1, 1), lambda g, n: (g, 0, 0, 0)),
        pl.BlockSpec((1, SSM_GW), lambda g, n: (0, g)),
    ]


def ssd_fwd(xact, proj, dtg, dtb, alog, dsk, nw):
    T = xact.shape[0]
    nc = T // Q

    def body(x_ref, b_ref, c_ref, z_ref, dt_ref, dtb_ref, al_ref, dk_ref, nw_ref,
             yb_ref, st_ref, state):
        @pl.when(pl.program_id(1) == 0)
        def _():
            state[...] = jnp.zeros(state.shape, F32)

        s = state[...]
        st_ref[0, 0] = s
        new_s, yb = ssd_step(s, x_ref[...].astype(F32), z_ref[...].astype(F32), dt_ref[0],
                             b_ref[...].astype(F32), c_ref[...].astype(F32),
                             dtb_ref[0], al_ref[0], dk_ref[0], nw_ref[...])
        state[...] = new_s
        yb_ref[...] = yb.astype(yb_ref.dtype)

    return pl.pallas_call(
        body, name="ssd_fwd", grid=(SSM_GROUPS, nc),
        out_shape=[jax.ShapeDtypeStruct((T, SSM_INNER), BF16),
                   jax.ShapeDtypeStruct((SSM_GROUPS, nc, SSM_GW, 128), F32)],
        in_specs=_ssd_in_specs(False, nc),
        out_specs=[pl.BlockSpec((Q, SSM_GW), lambda g, n: (n, g)),
                   pl.BlockSpec((1, 1, SSM_GW, 128), lambda g, n: (g, n, 0, 0))],
        scratch_shapes=[pltpu.VMEM((SSM_GW, 128), F32)],
        compiler_params=_params(("parallel", "arbitrary")),
    )(xact, xact, xact, proj, dtg, dtb, alog, dsk, nw)


def ssd_bwd(xact, proj, dtg, dtb, alog, dsk, nw, states, dyb):
    T = xact.shape[0]
    nc = T // Q

    def body(x_ref, b_ref, c_ref, z_ref, dt_ref, dtb_ref, al_ref, dk_ref, nw_ref, st_ref, dy_ref,
             dx_ref, db_ref, dc_ref, dz_ref, ddt_ref, ddtb_ref, dal_ref, ddk_ref, dnw_ref, dstate):
        @pl.when(pl.program_id(1) == 0)
        def _():
            dstate[...] = jnp.zeros(dstate.shape, F32)
            ddtb_ref[...] = jnp.zeros(ddtb_ref.shape, F32)
            dal_ref[...] = jnp.zeros(dal_ref.shape, F32)
            ddk_ref[...] = jnp.zeros(ddk_ref.shape, F32)
            dnw_ref[...] = jnp.zeros(dnw_ref.shape, F32)

        ins = (st_ref[0, 0], x_ref[...].astype(F32), z_ref[...].astype(F32), dt_ref[0],
               b_ref[...].astype(F32), c_ref[...].astype(F32),
               dtb_ref[0], al_ref[0], dk_ref[0], nw_ref[...])
        _, vjp = jax.vjp(ssd_step, *ins)
        ds, dx, dz, ddt, dbm, dcm, ddtb, dal, ddk, dnw = vjp((dstate[...], dy_ref[...].astype(F32)))
        dstate[...] = ds
        dx_ref[...] = dx.astype(dx_ref.dtype)
        db_ref[...] = dbm.astype(db_ref.dtype)
        dc_ref[...] = dcm.astype(dc_ref.dtype)
        dz_ref[...] = dz.astype(dz_ref.dtype)
        ddt_ref[0] = ddt
        ddtb_ref[0] += ddtb
        dal_ref[0] += dal
        ddk_ref[0] += ddk
        dnw_ref[...] += dnw

    rev = lambda n: nc - 1 - n
    par_shape = jax.ShapeDtypeStruct((SSM_GROUPS, SSM_HPG, 1, 1), F32)
    par_spec = pl.BlockSpec((1, SSM_HPG, 1, 1), lambda g, n: (g, 0, 0, 0))
    return pl.pallas_call(
        body, name="ssd_bwd", grid=(SSM_GROUPS, nc),
        out_shape=[jax.ShapeDtypeStruct((T, SSM_INNER), BF16),
                   jax.ShapeDtypeStruct((T, SSM_GROUPS * 128), BF16),
                   jax.ShapeDtypeStruct((T, SSM_GROUPS * 128), BF16),
                   jax.ShapeDtypeStruct((T, SSM_INNER), BF16),
                   jax.ShapeDtypeStruct((SSM_GROUPS, T, SSM_HPG), F32),
                   par_shape, par_shape, par_shape,
                   jax.ShapeDtypeStruct((1, SSM_INNER), F32)],
        in_specs=_ssd_in_specs(True, nc) + [
            pl.BlockSpec((1, 1, SSM_GW, 128), lambda g, n: (g, rev(n), 0, 0)),
            pl.BlockSpec((Q, SSM_GW), lambda g, n: (rev(n), g))],
        out_specs=[pl.BlockSpec((Q, SSM_GW), lambda g, n: (rev(n), g)),
                   pl.BlockSpec((Q, 128), lambda g, n: (rev(n), g)),
                   pl.BlockSpec((Q, 128), lambda g, n: (rev(n), g)),
                   pl.BlockSpec((Q, SSM_GW), lambda g, n: (rev(n), g)),
                   pl.BlockSpec((1, Q, SSM_HPG), lambda g, n: (g, rev(n), 0)),
                   par_spec, par_spec, par_spec,
                   pl.BlockSpec((1, SSM_GW), lambda g, n: (0, g))],
        scratch_shapes=[pltpu.VMEM((SSM_GW, 128), F32)],
        compiler_params=_params(("parallel", "arbitrary")),
    )(xact, xact, xact, proj, dtg, dtb, alog, dsk, nw, states, dyb)


def adamw(w, g, m, v, *, name):
    R, C = w.shape
    tr = R
    if R * C > 256 * 1024:
        tr = 256 if R % 256 == 0 else 128
    assert R % tr == 0

    def body(w_ref, g_ref, m_ref, v_ref, d_ref, nm_ref, nv_ref):
        g = g_ref[...]
        nm = ADAM_B1 * m_ref[...] + (1.0 - ADAM_B1) * g
        nv = ADAM_B2 * v_ref[...] + (1.0 - ADAM_B2) * jnp.square(g)
        m_hat = nm / (1.0 - ADAM_B1 ** ADAM_STEP)
        v_hat = nv / (1.0 - ADAM_B2 ** ADAM_STEP)
        d_ref[...] = -ADAM_LR * (m_hat / (jnp.sqrt(v_hat) + ADAM_EPS) + ADAM_WD * w_ref[...])
        nm_ref[...] = nm
        nv_ref[...] = nv

    spec = pl.BlockSpec((tr, C), lambda i: (i, 0))
    shp = jax.ShapeDtypeStruct((R, C), F32)
    return pl.pallas_call(
        body, name=name, grid=(R // tr,),
        out_shape=[shp, shp, shp], in_specs=[spec] * 4, out_specs=[spec] * 3,
        compiler_params=_params(("parallel",)),
    )(w, g, m, v)


def _pad_rows(a, rows):
    return jnp.pad(a, ((0, rows - a.shape[0]), (0, 0)))


def kernel(x, c, w_mod, b_mod, w_in, gm_norm_w, gm_ws, gm_bs, conv_w, conv_b, dt_bias, a_log, d_skip, ssm_norm_w, w_branch_gm, w_branch_ssm, w_out, w_ff1, w_ff2, final_norm_w, loss_target, m_w_mod, m_b_mod, m_w_in, m_gm_norm_w, m_gm_ws, m_gm_bs, m_conv_w, m_conv_b, m_dt_bias, m_a_log, m_d_skip, m_ssm_norm_w, m_w_branch_gm, m_w_branch_ssm, m_w_out, m_w_ff1, m_w_ff2, m_final_norm_w, v_w_mod, v_b_mod, v_w_in, v_gm_norm_w, v_gm_ws, v_gm_bs, v_conv_w, v_conv_b, v_dt_bias, v_a_log, v_d_skip, v_ssm_norm_w, v_w_branch_gm, v_w_branch_ssm, v_w_out, v_w_ff1, v_w_ff2, v_final_norm_w):
    T = x.shape[1]
    me = 4 * lax.axis_index("x") + 2 * lax.axis_index("y") + lax.axis_index("c")
    x2 = x[0]
    tgt = loss_target[0]
    n_in = IN_WIDTH // N_DEV
    n_mod = N_MOD * D // N_DEV
    n_cv = CONV_DIM // N_DEV

    pack1 = jnp.concatenate([c.reshape(-1), conv_w.reshape(-1)]).reshape(24, 128)
    g1 = exchange(pack1, gather=True, name="gather_c_convw").reshape(N_DEV, 3072)
    c_all = g1[:, :D]
    conv_w_full = g1[:, D:].reshape(N_DEV, CONV_K, n_cv).transpose(1, 0, 2).reshape(CONV_K, CONV_DIM)

    shards = [w_in[0], w_branch_gm[0], w_branch_ssm[0], w_out[0], w_ff1[0], w_ff2[0]]
    sizes = [s.size for s in shards]
    flat = jnp.concatenate([s.astype(BF16).reshape(-1) for s in shards])
    flat = jnp.pad(flat, (0, PACK_W_ROWS * D - flat.shape[0]))
    gw = exchange(flat.reshape(PACK_W_ROWS, D), gather=True, name="gather_weights")
    gw = gw.reshape(N_DEV, PACK_W_ROWS * D)
    offs = [0]
    for s in sizes:
        offs.append(offs[-1] + s)
    seg = [gw[:, offs[i]:offs[i + 1]] for i in range(6)]
    w_in_f = seg[0].reshape(N_DEV, D, n_in).transpose(1, 0, 2).reshape(D, IN_WIDTH)
    w_gm_f = seg[1].reshape(D, D)
    w_ssm_f = seg[2].reshape(SSM_INNER, D)
    w_out_f = seg[3].reshape(D, D)
    w_ff1_f = seg[4].reshape(N_DEV, D, D_FF // N_DEV).transpose(1, 0, 2).reshape(D, D_FF)
    w_ff2_f = seg[5].reshape(D_FF, D)
    w_proj = jnp.concatenate([w_in_f[:, :OFF_DT], w_in_f[:, OFF_GA:]], axis=1)
    w_dt = jnp.pad(w_in_f[:, OFF_DT:OFF_GA], ((0, 0), (0, 96)))
    w_all = jnp.concatenate([w_proj, w_dt], axis=1)

    c_pad = _pad_rows(c_all, 128)
    b_mine = lax.dynamic_slice(b_mod, (0, me * n_mod), (1, n_mod))

    def mod_fn(cp, w, b):
        ca = cp * jax.nn.sigmoid(cp)
        return (jnp.dot(ca, w, precision=HIGHEST, preferred_element_type=F32) + b,)

    (mod_part,) = whole_call(mod_fn, [c_pad, w_mod[0], b_mine], [((128, n_mod), F32)], name="mod_fwd")
    gmod = exchange(mod_part[:N_DEV], gather=True, name="gather_mod")
    mod = lax.dynamic_index_in_dim(gmod, me, axis=1, keepdims=False).reshape(N_MOD, D)
    sh1, sc1, gt1, sh2, sc2, gt2 = [mod[i:i + 1] for i in range(N_MOD)]

    (h,) = rowwise_call(fwd_body(fn_modulate), [x2], [sc1, sh1], [(D, BF16)], [], tm=256, name="modulate1")
    proj = matmul(h, w_proj, "nn", BF16, name="mm_proj")
    dt_raw = matmul(h, w_dt, "nn", F32, name="mm_dt")
    dtg = dt_raw[:, :32].reshape(T, SSM_GROUPS, SSM_HPG).transpose(1, 0, 2)
    ws = gm_ws[0]
    bs3 = gm_bs[0].reshape(GM_GROUPS, Q, 1)
    sgu_rows = [(proj, D, P_U), (proj, D, P_V)]
    (ya,) = rowwise_call(fwd_body(fn_sgu), sgu_rows, [gm_norm_w, ws, bs3], [(D, BF16)], [],
                         tm=256, name="sgu_fwd")
    xact = conv_fwd(proj, conv_w_full, conv_b)
    dtb4 = dt_bias.reshape(SSM_GROUPS, SSM_HPG, 1, 1)
    alog4 = a_log.reshape(SSM_GROUPS, SSM_HPG, 1, 1)
    dsk4 = d_skip.reshape(SSM_GROUPS, SSM_HPG, 1, 1)
    yb, states = ssd_fwd(xact, proj, dtg, dtb4, alog4, dsk4, ssm_norm_w)
    pa = matmul(ya, w_gm_f, "nn", F32, name="mm_branch_gm")
    pb = matmul(yb, w_ssm_f, "nn", F32, name="mm_branch_ssm")
    gate_rows = [(proj, D, P_GA), (proj, D, P_GB)]
    (mixed,) = rowwise_call(fwd_body(fn_mix), gate_rows + [pa, pb], [], [(D, BF16)], [], tm=256, name="mix_fwd")
    o = matmul(mixed, w_out_f, "nn", F32, name="mm_out")
    x1, h2 = rowwise_call(fwd_body(fn_res_modulate), [x2, o], [gt1, sc2, sh2], [(D, F32), (D, BF16)], [],
                          tm=256, name="res_modulate2")
    f = matmul(h2, w_ff1_f, "nn", BF16, name="mm_ff1")
    (act,) = rowwise_call(fwd_body(fn_relu2), [f], [], [(D_FF, BF16)], [], tm=256, name="relu2_fwd")
    gf = matmul(act, w_ff2_f, "nn", F32, name="mm_ff2")

    dx1, dgf, loss_v, dgt2, dfnw = rowwise_call(
        final_body, [x1, gf, tgt], [gt2, final_norm_w.reshape(1, D)], [(D, F32), (D, BF16)],
        [(1, 128), (1, D), (1, D)], tm=256, name="final_loss_bwd")
    dact = matmul(dgf, w_ff2_f, "nt", BF16, name="mm_ff2_dgrad")
    gw_ff2 = matmul(act, dgf, "tn", BF16, name="mm_ff2_wgrad", tk=512)
    (df,) = rowwise_call(bwd_body(fn_relu2, 1), [f, dact], [], [(D_FF, BF16)], [], tm=256, name="relu2_bwd")
    dh2 = matmul(df, w_ff1_f, "nt", F32, name="mm_ff1_dgrad")
    gw_ff1 = matmul(h2, df, "tn", BF16, name="mm_ff1_wgrad", tk=512)

    def res_mod_bwd(r, fl):
        xv, ov, dx1v, dh2v = r
        _, vjp = jax.vjp(fn_res_modulate, xv, ov, *fl)
        dxv, dov, dg1, dsc, dsh = vjp((dx1v, dh2v))
        return (dxv, dov), (dg1, dsc, dsh)

    dxa, do, dgt1, dsc2, dsh2 = rowwise_call(
        res_mod_bwd, [x2, o, dx1, dh2], [gt1, sc2, sh2], [(D, F32), (D, BF16)],
        [(1, D), (1, D), (1, D)], tm=256, name="res_modulate2_bwd")
    dmixed = matmul(do, w_out_f, "nt", F32, name="mm_out_dgrad")
    gw_out = matmul(mixed, do, "tn", BF16, name="mm_out_wgrad", tk=512)
    dga, dgb, dpa, dpb = rowwise_call(
        bwd_body(fn_mix, 4), gate_rows + [pa, pb, dmixed], [], [(D, BF16)] * 4, [], tm=256, name="mix_bwd")
    dya = matmul(dpa, w_gm_f, "nt", F32, name="mm_branch_gm_dgrad")
    gw_gm = matmul(ya, dpa, "tn", BF16, name="mm_branch_gm_wgrad", tk=512)
    dyb = matmul(dpb, w_ssm_f, "nt", BF16, name="mm_branch_ssm_dgrad")
    gw_ssm = matmul(yb, dpb, "tn", BF16, name="mm_branch_ssm_wgrad", tk=512)
    du, dv, dgnw, dws, dbs = rowwise_call(
        bwd_body(fn_sgu, 2), sgu_rows + [dya], [gm_norm_w, ws, bs3], [(D, BF16), (D, BF16)],
        [(1, D), (GM_GROUPS, Q, Q), (GM_GROUPS, Q, 1)], tm=256, name="sgu_bwd")
    dxs, dbm, dcm, dz, ddtg, ddtb, dalog, ddsk, dsnw = ssd_bwd(
        xact, proj, dtg, dtb4, alog4, dsk4, ssm_norm_w, states, dyb)
    dxact = jnp.concatenate([dxs, dbm, dcm], axis=1)
    dxbc, dconv_w, dconv_b = conv_bwd(proj, dxact, conv_w_full, conv_b)
    ddt = jnp.pad(ddtg.transpose(1, 0, 2).reshape(T, 32), ((0, 0), (0, 96))).astype(BF16)
    dproj = jnp.concatenate([du, dv, dz, dxbc, dga, dgb, ddt], axis=1)
    dh = matmul(dproj, w_all, "nt", F32, name="mm_in_dgrad", tk=1152)
    gw_all = matmul(h, dproj, "tn", BF16, name="mm_in_wgrad", tn=1152, tk=512)
    grad_x, dsc1, dsh1 = rowwise_call(grad_x_body, [x2, dh, dxa], [sc1, sh1], [(D, F32)],
                                      [(1, D), (1, D)], tm=256, name="modulate1_bwd")
    dmod = jnp.concatenate([dsh1, dsc1, dgt1, dsh2, dsc2, dgt2], axis=0)

    gw_in = jnp.concatenate([gw_all[:, :OFF_DT], gw_all[:, PROJ_W:PROJ_W + 32], gw_all[:, OFF_DT:PROJ_W]], axis=1)
    by_dst = [
        gw_in.reshape(D, N_DEV, n_in).transpose(1, 0, 2).reshape(N_DEV, -1),
        gw_gm.reshape(N_DEV, -1), gw_ssm.reshape(N_DEV, -1), gw_out.reshape(N_DEV, -1),
        gw_ff1.reshape(D, N_DEV, D_FF // N_DEV).transpose(1, 0, 2).reshape(N_DEV, -1),
        gw_ff2.reshape(N_DEV, -1)]
    gpack = jnp.concatenate(by_dst, axis=1)
    gpack = jnp.pad(gpack, ((0, 0), (0, PACK_W_ROWS * D - gpack.shape[1]))).reshape(N_DEV, PACK_W_ROWS, D)
    grecv = exchange(gpack, gather=False, name="alltoall_wgrads")
    gsum = sum_devices(grecv, tr=288, name="sum_wgrads").reshape(-1)
    gsh = [gsum[offs[i]:offs[i + 1]].reshape(shards[i].shape) for i in range(6)]

    def rows128(a, rows):
        a = a.reshape(-1)
        return jnp.pad(a, (0, rows * 128 - a.shape[0])).reshape(rows, 128)

    small = [(dgnw, 8), (dws, 1024), (dbs, 8), (dconv_w, 128), (dconv_b, 32), (ddtb, 8), (dalog, 8),
             (ddsk, 8), (dsnw, 16), (dfnw, 8), (dmod, 48)]
    spack = jnp.concatenate([rows128(a, r) for a, r in small], axis=0)
    sall = exchange(spack, gather=True, name="gather_small_grads")
    ssum = sum_devices(sall, tr=SMALL_ROWS, name="sum_small_grads")
    soff = [0]
    for _, r in small:
        soff.append(soff[-1] + r)

    def spart(i, n):
        return ssum[soff[i]:soff[i + 1]].reshape(-1)[:n]

    g_gm_norm_w = spart(0, D).reshape(1, D)
    g_gm_ws = spart(1, GM_GROUPS * Q * Q).reshape(GM_GROUPS * Q, Q)
    g_gm_bs = spart(2, GM_GROUPS * Q).reshape(GM_GROUPS, Q)
    g_conv_w_full = spart(3, CONV_K * CONV_DIM).reshape(CONV_K, CONV_DIM)
    g_conv_w = lax.dynamic_slice(g_conv_w_full, (0, me * n_cv), (CONV_K, n_cv))
    g_conv_b = spart(4, CONV_DIM).reshape(1, CONV_DIM)
    g_dt_bias = spart(5, 32).reshape(1, 32)
    g_a_log = spart(6, 32).reshape(1, 32)
    g_d_skip = spart(7, 32).reshape(1, 32)
    g_ssm_norm_w = spart(8, SSM_INNER).reshape(1, SSM_INNER)
    g_final_norm_w = spart(9, D).reshape(1, D)
    g_b_mod = spart(10, N_MOD * D).reshape(1, N_MOD * D)

    dmod_all = sall[:, soff[10]:soff[11]].reshape(N_DEV, N_MOD * D)
    dmod_mine = _pad_rows(lax.dynamic_slice(dmod_all, (0, me * n_mod), (N_DEV, n_mod)), 128)

    def wmod_grad_fn(cp, dm):
        ca = cp * jax.nn.sigmoid(cp)
        return (lax.dot_general(ca, dm, (((0,), (0,)), ((), ())), precision=HIGHEST,
                                preferred_element_type=F32),)

    (g_w_mod,) = whole_call(wmod_grad_fn, [c_pad, dmod_mine], [((D, n_mod), F32)], name="w_mod_grad")

    upd = {}

    def step(name, w, g, m, v):
        shape = w.shape
        w2 = w.reshape(-1, shape[-1])
        d, nm, nv = adamw(w2, g.reshape(w2.shape), m.reshape(w2.shape), v.reshape(w2.shape), name="adamw_" + name)
        upd[name] = (g.reshape(shape), d.reshape(shape), nm.reshape(shape), nv.reshape(shape))

    step("w_mod", w_mod, g_w_mod, m_w_mod, v_w_mod)
    step("b_mod", b_mod, g_b_mod, m_b_mod, v_b_mod)
    step("w_in", w_in, gsh[0], m_w_in, v_w_in)
    step("gm_norm_w", gm_norm_w, g_gm_norm_w, m_gm_norm_w, v_gm_norm_w)
    step("gm_ws", gm_ws, g_gm_ws, m_gm_ws, v_gm_ws)
    step("gm_bs", gm_bs, g_gm_bs, m_gm_bs, v_gm_bs)
    step("conv_w", conv_w, g_conv_w, m_conv_w, v_conv_w)
    step("conv_b", conv_b, g_conv_b, m_conv_b, v_conv_b)
    step("dt_bias", dt_bias, g_dt_bias, m_dt_bias, v_dt_bias)
    step("a_log", a_log, g_a_log, m_a_log, v_a_log)
    step("d_skip", d_skip, g_d_skip, m_d_skip, v_d_skip)
    step("ssm_norm_w", ssm_norm_w, g_ssm_norm_w, m_ssm_norm_w, v_ssm_norm_w)
    step("w_branch_gm", w_branch_gm, gsh[1], m_w_branch_gm, v_w_branch_gm)
    step("w_branch_ssm", w_branch_ssm, gsh[2], m_w_branch_ssm, v_w_branch_ssm)
    step("w_out", w_out, gsh[3], m_w_out, v_w_out)
    step("w_ff1", w_ff1, gsh[4], m_w_ff1, v_w_ff1)
    step("w_ff2", w_ff2, gsh[5], m_w_ff2, v_w_ff2)
    step("final_norm_w", final_norm_w.reshape(1, D), g_final_norm_w, m_final_norm_w.reshape(1, D),
         v_final_norm_w.reshape(1, D))
    upd["final_norm_w"] = tuple(a.reshape(D) for a in upd["final_norm_w"])

    loss = lax.psum(loss_v[0, 0], ("x", "y", "c"))
    order = ["w_mod", "b_mod", "w_in", "gm_norm_w", "gm_ws", "gm_bs", "conv_w", "conv_b", "dt_bias", "a_log",
             "d_skip", "ssm_norm_w", "w_branch_gm", "w_branch_ssm", "w_out", "w_ff1", "w_ff2", "final_norm_w"]
    return (loss, grad_x.reshape(1, T, D),
            *[upd[n][0] for n in order], *[upd[n][1] for n in order],
            *[upd[n][2] for n in order], *[upd[n][3] for n in order])
```

```python
import functools

import jax
import jax.numpy as jnp
from jax import lax
from jax.experimental import pallas as pl
from jax.experimental.pallas import tpu as pltpu

F32 = jnp.float32
BF16 = jnp.bfloat16
MESH = pl.DeviceIdType.MESH
HIGHEST = lax.Precision.HIGHEST

N_DEV = 8
D = 1024
Q = 128
GM_GROUPS = 8
SSM_INNER = 2048
SSM_GROUPS = 8
SSM_HPG = 4
SSM_P = 64
SSM_GW = SSM_HPG * SSM_P
CONV_DIM = 4096
CONV_K = 4
D_FF = 4096
N_MOD = 6
EPS = 1e-6
IN_WIDTH = 10272
OFF_DT = 8192
OFF_GA = 8224
PROJ_W = 10240
ALL_W = 10368
P_U, P_V, P_Z, P_XBC, P_GA, P_GB = 0, 1024, 2048, 4096, 8192, 9216

ADAM_LR = 0.001
ADAM_B1 = 0.9
ADAM_B2 = 0.999
ADAM_EPS = 1e-08
ADAM_WD = 0.01
ADAM_STEP = 10

VMEM_LIMIT_BYTES = 48 * 1024 * 1024
SMALL_ROWS = 1296


def _params(sem=None):
    return pltpu.CompilerParams(dimension_semantics=sem, vmem_limit_bytes=VMEM_LIMIT_BYTES)


def _dg(a, b, ca, cb):
    return lax.dot_general(a.astype(BF16), b.astype(BF16), (((ca,), (cb,)), ((), ())),
                           preferred_element_type=F32)


@jax.custom_vjp
def dot_nn(a, b):
    return _dg(a, b, 1, 0)


@jax.custom_vjp
def dot_nt(a, b):
    return _dg(a, b, 1, 1)


@jax.custom_vjp
def dot_tn(a, b):
    return _dg(a, b, 0, 0)


dot_nn.defvjp(lambda a, b: (dot_nn(a, b), (a, b)),
              lambda r, g: (dot_nt(g, r[1]), dot_tn(r[0], g)))
dot_nt.defvjp(lambda a, b: (dot_nt(a, b), (a, b)),
              lambda r, g: (dot_nn(g, r[1]), dot_tn(g, r[0])))
dot_tn.defvjp(lambda a, b: (dot_tn(a, b), (a, b)),
              lambda r, g: (dot_nt(r[1], g), dot_nn(r[0], g)))


def _rms(x):
    return x * lax.rsqrt(jnp.mean(x * x, axis=-1, keepdims=True) + EPS)


def _softplus(x):
    return jnp.maximum(x, 0.0) + jnp.log1p(jnp.exp(-jnp.abs(x)))


def _rows(n):
    return lambda ref, j: ref.at[pl.ds(pl.multiple_of(j * n, n), n)]


def _cols(n):
    return lambda ref, j: ref.at[:, pl.ds(pl.multiple_of(j * n, n), n)]


def _slot(ref, j):
    return ref.at[j]


def _whole(ref, j):
    return ref


def exchange(items, *, name):
    n = len(items)

    def body(*refs):
        src_refs, out_refs = refs[:n], refs[n:2 * n]
        send_sems, recv_sems, local_sems = refs[2 * n:]
        x = lax.axis_index("x")
        y = lax.axis_index("y")
        c = lax.axis_index("c")
        me = 4 * x + 2 * y + c
        copies = []
        for i, (_, src_win, _, dst_win) in enumerate(items):
            local = pltpu.make_async_copy(src_win(src_refs[i], me), dst_win(out_refs[i], me), local_sems.at[i])
            local.start()
            copies.append(local)
        remote = []
        for i, (_, src_win, _, dst_win) in enumerate(items):
            for k in range(1, N_DEV):
                px = lax.rem(x + ((k >> 2) & 1), 2)
                py = lax.rem(y + ((k >> 1) & 1), 2)
                pc = lax.rem(c + (k & 1), 2)
                peer = 4 * px + 2 * py + pc
                cp = pltpu.make_async_remote_copy(
                    src_ref=src_win(src_refs[i], peer), dst_ref=dst_win(out_refs[i], me),
                    send_sem=send_sems.at[i, k - 1], recv_sem=recv_sems.at[i, k - 1],
                    device_id=(px, py, pc), device_id_type=MESH)
                cp.start()
                remote.append(cp)
        for cp in remote:
            cp.wait_send()
        for cp in remote:
            cp.wait_recv()
        for cp in copies:
            cp.wait()

    return pl.pallas_call(
        body, name=name,
        out_shape=[jax.ShapeDtypeStruct(tuple(shape), src.dtype) for (src, _, shape, _) in items],
        in_specs=[pl.BlockSpec(memory_space=pl.ANY)] * n,
        out_specs=[pl.BlockSpec(memory_space=pl.ANY)] * n,
        scratch_shapes=[pltpu.SemaphoreType.DMA((n, N_DEV - 1)),
                        pltpu.SemaphoreType.DMA((n, N_DEV - 1)),
                        pltpu.SemaphoreType.DMA((n,))],
    )(*[it[0] for it in items])


def gather_blocks(src, *, name):
    return exchange([(src, _whole, (N_DEV,) + src.shape, _slot)], name=name)[0]


def sum_devices(g, *, tr, name):
    _, R, C = g.shape

    def body(g_ref, o_ref):
        acc = g_ref[0].astype(F32)
        for j in range(1, N_DEV):
            acc = acc + g_ref[j].astype(F32)
        o_ref[...] = acc

    return pl.pallas_call(
        body, name=name, grid=(R // tr,),
        out_shape=jax.ShapeDtypeStruct((R, C), F32),
        in_specs=[pl.BlockSpec((N_DEV, tr, C), lambda i: (0, i, 0))],
        out_specs=pl.BlockSpec((tr, C), lambda i: (i, 0)),
        compiler_params=_params(("parallel",)),
    )(g)


def matmul(a, b, mode, out_dtype, *, name, tm=1024, tn=1024, tk=1024, n=None):
    if mode == "nn":
        (M, K), (K2, N) = a.shape, b.shape
    elif mode == "nt":
        (M, K), (N, K2) = a.shape, b.shape
    else:
        (K, M), (K2, N) = a.shape, b.shape
    assert K == K2
    N = N if n is None else n
    tm, tn, tk = min(tm, M), min(tn, N), min(tk, K)
    assert M % tm == 0 and N % tn == 0 and K % tk == 0, (name, M, N, K, tm, tn, tk)
    nk = K // tk
    if mode == "tn":
        a_spec = pl.BlockSpec((tk, tm), lambda i, j, k: (k, i))
    else:
        a_spec = pl.BlockSpec((tm, tk), lambda i, j, k: (i, k))
    if mode == "nt":
        b_spec = pl.BlockSpec((tn, tk), lambda i, j, k: (j, k))
    else:
        b_spec = pl.BlockSpec((tk, tn), lambda i, j, k: (k, j))
    dims = {"nn": (1, 0), "nt": (1, 1), "tn": (0, 0)}[mode]

    def body(a_ref, b_ref, o_ref, acc_ref):
        k = pl.program_id(2)
        part = lax.dot_general(a_ref[...], b_ref[...], (((dims[0],), (dims[1],)), ((), ())),
                               preferred_element_type=F32)

        @pl.when(k == 0)
        def _():
            acc_ref[...] = part

        @pl.when(k > 0)
        def _():
            acc_ref[...] += part

        @pl.when(k == nk - 1)
        def _():
            o_ref[...] = acc_ref[...].astype(o_ref.dtype)

    return pl.pallas_call(
        body, name=name, grid=(M // tm, N // tn, nk),
        out_shape=jax.ShapeDtypeStruct((M, N), out_dtype),
        in_specs=[a_spec, b_spec],
        out_specs=pl.BlockSpec((tm, tn), lambda i, j, k: (i, j)),
        scratch_shapes=[pltpu.VMEM((tm, tn), F32)],
        compiler_params=_params(("parallel", "parallel", "arbitrary")),
    )(a, b)


def rowwise_call(body_fn, rows, fulls, row_outs, acc_outs, *, tm, name):
    rows = [r if isinstance(r, tuple) else (r, r.shape[1], 0) for r in rows]
    T = rows[0][0].shape[0]
    tm = min(tm, T)
    assert T % tm == 0
    n_r, n_f, n_ro = len(rows), len(fulls), len(row_outs)
    into = [(k, ro) for k, ro in enumerate(row_outs) if len(ro) == 3]
    n_b = len(into)

    def body(*refs):
        r_refs = refs[:n_r]
        f_refs = refs[n_r:n_r + n_f]
        refs = refs[n_r + n_f + n_b:]
        ro_refs = refs[:n_ro]
        ao_refs = refs[n_ro:]
        r_vals = [r[...].astype(F32) for r in r_refs]
        f_vals = [f[...].astype(F32) for f in f_refs]
        ro, ao = body_fn(r_vals, f_vals)
        for ref, v in zip(ro_refs, ro):
            ref[...] = v.astype(ref.dtype)
        if ao_refs:
            @pl.when(pl.program_id(0) == 0)
            def _():
                for ref in ao_refs:
                    ref[...] = jnp.zeros(ref.shape, F32)
            for ref, v in zip(ao_refs, ao):
                ref[...] += v.reshape(ref.shape)

    def full_spec(shape):
        nd = len(shape)
        return pl.BlockSpec(tuple(shape), lambda i: (0,) * nd)

    in_specs = [pl.BlockSpec((tm, w), functools.partial(lambda i, o: (i, o), o=off // w))
                for (_, w, off) in rows]
    in_specs += [full_spec(f.shape) for f in fulls]
    in_specs += [pl.BlockSpec(memory_space=pl.ANY)] * n_b
    out_specs, out_shape = [], []
    for ro in row_outs:
        if len(ro) == 3:
            buf, w, off = ro
            out_specs.append(pl.BlockSpec((tm, w), functools.partial(lambda i, o: (i, o), o=off // w)))
            out_shape.append(jax.ShapeDtypeStruct(buf.shape, buf.dtype))
        else:
            w, dt = ro
            out_specs.append(pl.BlockSpec((tm, w), lambda i: (i, 0)))
            out_shape.append(jax.ShapeDtypeStruct((T, w), dt))
    out_specs += [full_spec(s) for s in acc_outs]
    out_shape += [jax.ShapeDtypeStruct(tuple(s), F32) for s in acc_outs]
    aliases = {n_r + n_f + b: k for b, (k, _) in enumerate(into)}
    return pl.pallas_call(
        body, name=name, grid=(T // tm,),
        out_shape=out_shape, in_specs=in_specs, out_specs=out_specs,
        input_output_aliases=aliases,
        compiler_params=_params(("arbitrary",)),
    )(*[r[0] for r in rows], *fulls, *[ro[0] for _, ro in into])


def fwd_body(fn):
    return lambda r, f: (fn(*r, *f), ())


def bwd_body(fn, n_rows):
    def body(r, f):
        ins, cots = r[:n_rows], r[n_rows:]
        _, vjp = jax.vjp(fn, *ins, *f)
        g = vjp(tuple(cots))
        return g[:n_rows], g[n_rows:]
    return body


def whole_call(fn, ins, outs, *, name):
    n_in = len(ins)

    def body(*refs):
        res = fn(*[r[...] for r in refs[:n_in]])
        for ref, v in zip(refs[n_in:], res):
            ref[...] = v.astype(ref.dtype)

    return pl.pallas_call(
        body, name=name,
        out_shape=[jax.ShapeDtypeStruct(tuple(s), dt) for (s, dt) in outs],
        compiler_params=_params(),
    )(*ins)


def fn_modulate(x, sc, sh):
    return (_rms(x) * (1.0 + sc) + sh,)


def fn_sgu(u, v, nw, ws, bs):
    ug = jax.nn.gelu(u)
    vn = _rms(jax.nn.gelu(v)) * nw
    ri = lax.broadcasted_iota(jnp.int32, (Q, Q), 0)
    ci = lax.broadcasted_iota(jnp.int32, (Q, Q), 1)
    causal = ri >= ci
    chunks = []
    for n in range(u.shape[0] // Q):
        vc = vn[n * Q:(n + 1) * Q]
        cols = [dot_nn(jnp.where(causal, ws[g], 0.0), vc[:, g * Q:(g + 1) * Q]) + bs[g]
                for g in range(GM_GROUPS)]
        chunks.append(jnp.concatenate(cols, axis=1))
    sv = chunks[0] if len(chunks) == 1 else jnp.concatenate(chunks, axis=0)
    return (ug * sv,)


def fn_mix(ga, gb, pa, pb):
    return (jax.nn.sigmoid(ga) * pa + jax.nn.sigmoid(gb) * pb,)


def fn_res_modulate(x, o, g1, sc2, sh2):
    x1 = x + g1 * o
    return x1, _rms(x1) * (1.0 + sc2) + sh2


def fn_relu2(f):
    return (jnp.square(jnp.maximum(f, 0.0)),)


def final_body(r, f):
    x1, gf, tgt = r
    g2, fnw = f

    def loss_fn(x1, gf, g2, fnw):
        y = _rms(x1 + g2 * gf) * fnw
        row = 0.5 * jnp.mean(jnp.square(y - tgt), axis=-1, keepdims=True)
        return jnp.sum(row, axis=0, keepdims=True)

    l, vjp = jax.vjp(loss_fn, x1, gf, g2, fnw)
    dx1, dgf, dg2, dfnw = vjp(jnp.ones((1, 1), F32))
    return (dx1, dgf), (jnp.broadcast_to(l, (1, 128)), dg2, dfnw)


def grad_x_body(r, f):
    x, dh, dxa = r
    _, vjp = jax.vjp(fn_modulate, x, *f)
    dx, dsc, dsh = vjp((dh,))
    return (dx + dxa,), (dsc, dsh)


CONV_CW = 128
CONV_PAD = 8


def _conv_pre(xp, w_ref, b_ref, r0, R):
    acc = b_ref[...] + w_ref[0:1, :] * xp[r0 + CONV_PAD - 3:r0 + CONV_PAD - 3 + R, :]
    for k in range(1, CONV_K):
        s = r0 + CONV_PAD - 3 + k
        acc = acc + w_ref[k:k + 1, :] * xp[s:s + R, :]
    return acc


def conv_fwd(proj, conv_w, conv_b):
    T = proj.shape[0]
    R = min(512, T)

    def body(x_ref, w_ref, b_ref, o_ref, xp):
        xp[0:CONV_PAD, :] = jnp.zeros((CONV_PAD, CONV_CW), F32)
        xp[CONV_PAD:CONV_PAD + T, :] = x_ref[...].astype(F32)
        for r0 in range(0, T, R):
            pre = _conv_pre(xp, w_ref, b_ref, r0, R)
            o_ref[r0:r0 + R, :] = (pre * jax.nn.sigmoid(pre)).astype(o_ref.dtype)

    return pl.pallas_call(
        body, name="conv_fwd", grid=(CONV_DIM // CONV_CW,),
        out_shape=jax.ShapeDtypeStruct((T, CONV_DIM), BF16),
        in_specs=[pl.BlockSpec((T, CONV_CW), lambda j: (0, P_XBC // CONV_CW + j)),
                  pl.BlockSpec((CONV_K, CONV_CW), lambda j: (0, j)),
                  pl.BlockSpec((1, CONV_CW), lambda j: (0, j))],
        out_specs=pl.BlockSpec((T, CONV_CW), lambda j: (0, j)),
        scratch_shapes=[pltpu.VMEM((T + CONV_PAD, CONV_CW), F32)],
        compiler_params=_params(("parallel",)),
    )(proj, conv_w, conv_b)


def conv_bwd(proj, dact, col0, conv_w, conv_b, dproj, *, name):
    T = proj.shape[0]
    R = min(512, T)
    nb = dact.shape[1] // CONV_CW
    c0 = col0 // CONV_CW
    x0 = (P_XBC + col0) // CONV_CW

    def body(x_ref, d_ref, w_ref, b_ref, _, dx_ref, dw_ref, db_ref, xp, dp):
        xp[0:CONV_PAD, :] = jnp.zeros((CONV_PAD, CONV_CW), F32)
        xp[CONV_PAD:CONV_PAD + T, :] = x_ref[...].astype(F32)
        dp[T:T + CONV_PAD, :] = jnp.zeros((CONV_PAD, CONV_CW), F32)
        dws = [jnp.zeros((1, CONV_CW), F32) for _ in range(CONV_K)]
        db = jnp.zeros((1, CONV_CW), F32)
        for r0 in range(0, T, R):
            pre = _conv_pre(xp, w_ref, b_ref, r0, R)
            s = jax.nn.sigmoid(pre)
            dpre = d_ref[r0:r0 + R, :].astype(F32) * (s * (1.0 + pre * (1.0 - s)))
            dp[r0:r0 + R, :] = dpre
            db = db + jnp.sum(dpre, axis=0, keepdims=True)
            for k in range(CONV_K):
                st = r0 + CONV_PAD - 3 + k
                dws[k] = dws[k] + jnp.sum(dpre * xp[st:st + R, :], axis=0, keepdims=True)
        for r0 in range(0, T, R):
            acc = w_ref[0:1, :] * dp[r0 + 3:r0 + 3 + R, :]
            for k in range(1, CONV_K):
                acc = acc + w_ref[k:k + 1, :] * dp[r0 + 3 - k:r0 + 3 - k + R, :]
            dx_ref[r0:r0 + R, :] = acc.astype(dx_ref.dtype)
        for k in range(CONV_K):
            dw_ref[k:k + 1, :] = dws[k]
        db_ref[...] = db

    return pl.pallas_call(
        body, name=name, grid=(nb,),
        out_shape=[jax.ShapeDtypeStruct(dproj.shape, dproj.dtype),
                   jax.ShapeDtypeStruct((CONV_K, nb * CONV_CW), F32),
                   jax.ShapeDtypeStruct((1, nb * CONV_CW), F32)],
        in_specs=[pl.BlockSpec((T, CONV_CW), lambda j: (0, x0 + j)),
                  pl.BlockSpec((T, CONV_CW), lambda j: (0, j)),
                  pl.BlockSpec((CONV_K, CONV_CW), lambda j: (0, c0 + j)),
                  pl.BlockSpec((1, CONV_CW), lambda j: (0, c0 + j)),
                  pl.BlockSpec(memory_space=pl.ANY)],
        out_specs=[pl.BlockSpec((T, CONV_CW), lambda j: (0, x0 + j)),
                   pl.BlockSpec((CONV_K, CONV_CW), lambda j: (0, j)),
                   pl.BlockSpec((1, CONV_CW), lambda j: (0, j))],
        scratch_shapes=[pltpu.VMEM((T + CONV_PAD, CONV_CW), F32),
                        pltpu.VMEM((T + CONV_PAD, CONV_CW), F32)],
        input_output_aliases={4: 0},
        compiler_params=_params(("parallel",)),
    )(proj, dact, conv_w, conv_b, dproj)


def ssd_step(lane0, state, x, z, dtr, Bm, Cm, dtb, alog, dsk, nw):
    ri = lax.broadcasted_iota(jnp.int32, (Q, Q), 0)
    ci = lax.broadcasted_iota(jnp.int32, (Q, Q), 1)
    causal = ri >= ci
    eye = ri == ci
    tril = causal.astype(F32)
    lane = lax.broadcasted_iota(jnp.int32, (1, 128), 1)
    colh = lax.shift_right_logical(lax.broadcasted_iota(jnp.int32, (1, SSM_GW), 1), 6)
    rowh = lax.shift_right_logical(lax.broadcasted_iota(jnp.int32, (SSM_GW, 1), 0), 6)
    rowq = lax.broadcasted_iota(jnp.int32, (Q, 1), 0)

    dt_f = jnp.zeros((Q, SSM_GW), F32)
    a_f = jnp.zeros((Q, SSM_GW), F32)
    dsk_f = jnp.zeros((1, SSM_GW), F32)
    for h in range(SSM_HPG):
        dtr_h = jnp.sum(jnp.where(lane == lane0 + h, dtr, 0.0), axis=1, keepdims=True)
        dt_h = _softplus(dtr_h + dtb[h])
        a_h = dt_h * (-jnp.exp(alog[h]))
        dt_f = jnp.where(colh == h, dt_h, dt_f)
        a_f = jnp.where(colh == h, a_h, a_f)
        dsk_f = jnp.where(colh == h, dsk[h], dsk_f)

    cum_f = jnp.dot(tril, a_f, precision=HIGHEST, preferred_element_type=F32)
    last_f = jnp.sum(jnp.where(rowq == Q - 1, cum_f, 0.0), axis=0, keepdims=True)
    xdt = x * dt_f
    cb = dot_nt(Cm, Bm)

    y = jnp.zeros((Q, SSM_GW), F32)
    elast_rows = jnp.zeros((SSM_GW, 1), F32)
    for h in range(SSM_HPG):
        pick = lax.broadcasted_iota(jnp.int32, (1, SSM_GW), 1) == h * SSM_P
        ch = jnp.sum(jnp.where(pick, cum_f, 0.0), axis=1, keepdims=True)
        ch_t = jnp.sum(jnp.where(eye, ch, 0.0), axis=0, keepdims=True)
        decay = jnp.exp(jnp.where(causal, ch - ch_t, -1e30))
        y = jnp.where(colh == h, dot_nn(cb * decay, xdt), y)
        last_h = jnp.sum(jnp.where(pick, last_f, 0.0), axis=1, keepdims=True)
        elast_rows = jnp.where(rowh == h, jnp.exp(last_h), elast_rows)

    y = y + dot_nt(Cm, state) * jnp.exp(cum_f) + x * dsk_f
    new_state = state * elast_rows + dot_tn(xdt * jnp.exp(last_f - cum_f), Bm)
    gated = y * (z * jax.nn.sigmoid(z))
    return new_state, _rms(gated) * nw


def _ssd_in_specs(rev, nc):
    def n_of(n):
        return nc - 1 - n if rev else n
    return [
        pl.BlockSpec((Q, SSM_GW), lambda g, n: (n_of(n), g)),
        pl.BlockSpec((Q, 128), lambda g, n: (n_of(n), SSM_INNER // 128 + g)),
        pl.BlockSpec((Q, 128), lambda g, n: (n_of(n), SSM_INNER // 128 + SSM_GROUPS + g)),
        pl.BlockSpec((Q, SSM_GW), lambda g, n: (n_of(n), P_Z // SSM_GW + g)),
        pl.BlockSpec((Q, 128), lambda g, n: (n_of(n), 0)),
        pl.BlockSpec((1, SSM_HPG, 1, 1), lambda g, n: (g, 0, 0, 0)),
        pl.BlockSpec((1, SSM_HPG, 1, 1), lambda g, n: (g, 0, 0, 0)),
        pl.BlockSpec((1, SSM_HPG, 1, 1), lambda g, n: (g, 0, 0, 0)),
        pl.BlockSpec((1, SSM_GW), lambda g, n: (0, g)),
    ]


def ssd_fwd(xact, proj, dtg, dtb, alog, dsk, nw):
    T = xact.shape[0]
    nc = T // Q

    def body(x_ref, b_ref, c_ref, z_ref, dt_ref, dtb_ref, al_ref, dk_ref, nw_ref,
             yb_ref, st_ref, state):
        @pl.when(pl.program_id(1) == 0)
        def _():
            state[...] = jnp.zeros(state.shape, F32)

        s = state[...]
        st_ref[0, 0] = s
        new_s, yb = ssd_step(SSM_HPG * pl.program_id(0), s, x_ref[...].astype(F32), z_ref[...].astype(F32), dt_ref[...],
                             b_ref[...].astype(F32), c_ref[...].astype(F32),
                             dtb_ref[0], al_ref[0], dk_ref[0], nw_ref[...])
        state[...] = new_s
        yb_ref[...] = yb.astype(yb_ref.dtype)

    return pl.pallas_call(
        body, name="ssd_fwd", grid=(SSM_GROUPS, nc),
        out_shape=[jax.ShapeDtypeStruct((T, SSM_INNER), BF16),
                   jax.ShapeDtypeStruct((SSM_GROUPS, nc, SSM_GW, 128), F32)],
        in_specs=_ssd_in_specs(False, nc),
        out_specs=[pl.BlockSpec((Q, SSM_GW), lambda g, n: (n, g)),
                   pl.BlockSpec((1, 1, SSM_GW, 128), lambda g, n: (g, n, 0, 0))],
        scratch_shapes=[pltpu.VMEM((SSM_GW, 128), F32)],
        compiler_params=_params(("parallel", "arbitrary")),
    )(xact, xact, xact, proj, dtg, dtb, alog, dsk, nw)


def ssd_bwd(xact, proj, dtg, dtb, alog, dsk, nw, states, dyb, dproj):
    T = xact.shape[0]
    nc = T // Q

    def body(x_ref, b_ref, c_ref, z_ref, dt_ref, dtb_ref, al_ref, dk_ref, nw_ref, st_ref, dy_ref, _,
             dx_ref, db_ref, dc_ref, dz_ref, ddt_ref, ddtb_ref, dal_ref, ddk_ref, dnw_ref, dstate):
        @pl.when(pl.program_id(1) == 0)
        def _():
            dstate[...] = jnp.zeros(dstate.shape, F32)
            ddtb_ref[...] = jnp.zeros(ddtb_ref.shape, F32)
            dal_ref[...] = jnp.zeros(dal_ref.shape, F32)
            ddk_ref[...] = jnp.zeros(ddk_ref.shape, F32)
            dnw_ref[...] = jnp.zeros(dnw_ref.shape, F32)

        ins = (st_ref[0, 0], x_ref[...].astype(F32), z_ref[...].astype(F32), dt_ref[...],
               b_ref[...].astype(F32), c_ref[...].astype(F32),
               dtb_ref[0], al_ref[0], dk_ref[0], nw_ref[...])
        _, vjp = jax.vjp(functools.partial(ssd_step, SSM_HPG * pl.program_id(0)), *ins)
        ds, dx, dz, ddt, dbm, dcm, ddtb, dal, ddk, dnw = vjp((dstate[...], dy_ref[...].astype(F32)))
        dstate[...] = ds
        dx_ref[...] = dx.astype(dx_ref.dtype)
        db_ref[...] = dbm.astype(db_ref.dtype)
        dc_ref[...] = dcm.astype(dc_ref.dtype)
        dz_ref[...] = dz.astype(dz_ref.dtype)
        ddt_ref[0] = ddt
        ddtb_ref[0] += ddtb
        dal_ref[0] += dal
        ddk_ref[0] += ddk
        dnw_ref[...] += dnw

    rev = lambda n: nc - 1 - n
    par_shape = jax.ShapeDtypeStruct((SSM_GROUPS, SSM_HPG, 1, 1), F32)
    par_spec = pl.BlockSpec((1, SSM_HPG, 1, 1), lambda g, n: (g, 0, 0, 0))
    return pl.pallas_call(
        body, name="ssd_bwd", grid=(SSM_GROUPS, nc),
        out_shape=[jax.ShapeDtypeStruct((T, SSM_INNER), BF16),
                   jax.ShapeDtypeStruct((T, SSM_GROUPS * 128), BF16),
                   jax.ShapeDtypeStruct((T, SSM_GROUPS * 128), BF16),
                   jax.ShapeDtypeStruct(dproj.shape, dproj.dtype),
                   jax.ShapeDtypeStruct((SSM_GROUPS, T, 128), F32),
                   par_shape, par_shape, par_shape,
                   jax.ShapeDtypeStruct((1, SSM_INNER), F32)],
        in_specs=_ssd_in_specs(True, nc) + [
            pl.BlockSpec((1, 1, SSM_GW, 128), lambda g, n: (g, rev(n), 0, 0)),
            pl.BlockSpec((Q, SSM_GW), lambda g, n: (rev(n), g)),
            pl.BlockSpec(memory_space=pl.ANY)],
        out_specs=[pl.BlockSpec((Q, SSM_GW), lambda g, n: (rev(n), g)),
                   pl.BlockSpec((Q, 128), lambda g, n: (rev(n), g)),
                   pl.BlockSpec((Q, 128), lambda g, n: (rev(n), g)),
                   pl.BlockSpec((Q, SSM_GW), lambda g, n: (rev(n), P_Z // SSM_GW + g)),
                   pl.BlockSpec((1, Q, 128), lambda g, n: (g, rev(n), 0)),
                   par_spec, par_spec, par_spec,
                   pl.BlockSpec((1, SSM_GW), lambda g, n: (0, g))],
        scratch_shapes=[pltpu.VMEM((SSM_GW, 128), F32)],
        input_output_aliases={11: 3},
        compiler_params=_params(("parallel", "arbitrary")),
    )(xact, xact, xact, proj, dtg, dtb, alog, dsk, nw, states, dyb, dproj)


ADAMW_WHOLE_ELEMS = 256 * 1024


def adamw(w, g, m, v, *, name):
    shape = w.shape
    parts = g.shape != shape
    nd = len(shape)
    if w.size <= ADAMW_WHOLE_ELEMS:
        grid, tr = (1,), shape[-2]
    else:
        assert all(s == 1 for s in shape[:-2]) and shape[-2] % 256 == 0
        grid, tr = (shape[-2] // 256,), 256
    blk = tuple(shape[:-2]) + (tr, shape[-1])
    spec = pl.BlockSpec(blk, lambda i: (0,) * (nd - 2) + (i, 0))
    g_spec = pl.BlockSpec((N_DEV,) + blk[1:], lambda i: (0,) * (nd - 2) + (i, 0)) if parts else spec

    def body(w_ref, g_ref, m_ref, v_ref, go_ref, d_ref, nm_ref, nv_ref):
        if parts:
            g = g_ref[0:1].astype(F32)
            for j in range(1, N_DEV):
                g = g + g_ref[j:j + 1].astype(F32)
        else:
            g = g_ref[...]
        nm = ADAM_B1 * m_ref[...] + (1.0 - ADAM_B1) * g
        nv = ADAM_B2 * v_ref[...] + (1.0 - ADAM_B2) * jnp.square(g)
        m_hat = nm / (1.0 - ADAM_B1 ** ADAM_STEP)
        v_hat = nv / (1.0 - ADAM_B2 ** ADAM_STEP)
        go_ref[...] = g
        d_ref[...] = -ADAM_LR * (m_hat / (jnp.sqrt(v_hat) + ADAM_EPS) + ADAM_WD * w_ref[...])
        nm_ref[...] = nm
        nv_ref[...] = nv

    shp = jax.ShapeDtypeStruct(shape, F32)
    return pl.pallas_call(
        body, name=name, grid=grid,
        out_shape=[shp] * 4, in_specs=[spec, g_spec, spec, spec], out_specs=[spec] * 4,
        compiler_params=_params(("parallel",)),
    )(w, g, m, v)


def _pad_rows(a, rows):
    return jnp.pad(a, ((0, rows - a.shape[0]), (0, 0)))


WIN_W = 1408
N_IN = IN_WIDTH // N_DEV
_A6 = OFF_DT - 6 * N_IN
_C6 = 7 * N_IN - OFF_GA


def _win_offset(me):
    return jnp.where(me == 7, 124, 4 * me)


def _w_in_window(shard, me):
    rows = shard.shape[0]
    z = lambda n: jnp.zeros((rows, n), shard.dtype)
    a = lax.dynamic_update_slice(z(WIN_W), shard, (0, _win_offset(me)))
    b = jnp.concatenate([z(24), shard[:, :_A6], shard[:, _A6 + 32:], z(4), shard[:, _A6:_A6 + 32], z(96)], axis=1)
    return jnp.where(me == 6, b, a)


def _w_in_from_window(window, me):
    a = lax.dynamic_slice(window, (0, _win_offset(me)), (window.shape[0], N_IN))
    b = jnp.concatenate([window[:, 24:24 + _A6], window[:, 1280:1312], window[:, 24 + _A6:24 + _A6 + _C6]], axis=1)
    return jnp.where(me == 6, b, a)


def _w_all_from_windows(g):
    def merge_first(p, t):
        return jnp.concatenate([p[:, :128] + t, p[:, 128:]], axis=1)

    parts = [g[0][:, :1280]]
    for j in range(1, 6):
        parts.append(merge_first(g[j][:, :1280], g[j - 1][:, 1280:]))
    p6 = merge_first(g[6][:, :1280], g[5][:, 1280:])
    parts.append(jnp.concatenate([p6[:, :1152], p6[:, 1152:] + g[7][:, :128]], axis=1))
    parts.append(g[7][:, 128:])
    parts.append(g[6][:, 1280:])
    return jnp.concatenate(parts, axis=1)


def _windows_of_w_all(gw):
    wins = [gw[:, 1280 * j:1280 * j + WIN_W] for j in range(6)]
    wins.append(jnp.concatenate([gw[:, 7680:8960], gw[:, PROJ_W:]], axis=1))
    wins.append(gw[:, 8832:PROJ_W])
    return jnp.stack(wins)


def kernel(x, c, w_mod, b_mod, w_in, gm_norm_w, gm_ws, gm_bs, conv_w, conv_b, dt_bias, a_log, d_skip, ssm_norm_w, w_branch_gm, w_branch_ssm, w_out, w_ff1, w_ff2, final_norm_w, loss_target, m_w_mod, m_b_mod, m_w_in, m_gm_norm_w, m_gm_ws, m_gm_bs, m_conv_w, m_conv_b, m_dt_bias, m_a_log, m_d_skip, m_ssm_norm_w, m_w_branch_gm, m_w_branch_ssm, m_w_out, m_w_ff1, m_w_ff2, m_final_norm_w, v_w_mod, v_b_mod, v_w_in, v_gm_norm_w, v_gm_ws, v_gm_bs, v_conv_w, v_conv_b, v_dt_bias, v_a_log, v_d_skip, v_ssm_norm_w, v_w_branch_gm, v_w_branch_ssm, v_w_out, v_w_ff1, v_w_ff2, v_final_norm_w):
    T = x.shape[1]
    me = 4 * lax.axis_index("x") + 2 * lax.axis_index("y") + lax.axis_index("c")
    x2 = x[0]
    tgt = loss_target[0]
    n_in = IN_WIDTH // N_DEV
    n_mod = N_MOD * D // N_DEV
    n_cv = CONV_DIM // N_DEV

    c_all, conv_w_full = exchange(
        [(c.reshape(8, 128), _whole, (N_DEV, 8, 128), _slot),
         (conv_w[0], _whole, (N_DEV, CONV_K, n_cv), _slot)], name="gather_c_convw")
    c_all = c_all.reshape(N_DEV, D)
    conv_w_full = conv_w_full.transpose(1, 0, 2).reshape(CONV_K, CONV_DIM)

    win = _w_in_window(w_in[0].astype(BF16), me)
    gwin, w_gm_f, w_ssm_f, w_out_f, w_ff1_f, w_ff2_f = exchange(
        [(win, _whole, (N_DEV, D, WIN_W), _slot),
         (w_branch_gm[0].astype(BF16), _whole, (D, D), _rows(D // N_DEV)),
         (w_branch_ssm[0].astype(BF16), _whole, (SSM_INNER, D), _rows(SSM_INNER // N_DEV)),
         (w_out[0].astype(BF16), _whole, (D, D), _rows(D // N_DEV)),
         (w_ff1[0].astype(BF16), _whole, (D, D_FF), _cols(D_FF // N_DEV)),
         (w_ff2[0].astype(BF16), _whole, (D_FF, D), _rows(D_FF // N_DEV))], name="gather_weights")
    w_all = _w_all_from_windows(gwin)
    w_dt = w_all[:, PROJ_W:]

    c_pad = _pad_rows(c_all, 128)
    b_mine = lax.dynamic_slice(b_mod, (0, me * n_mod), (1, n_mod))

    def mod_fn(cp, w, b):
        ca = cp * jax.nn.sigmoid(cp)
        return (jnp.dot(ca, w, precision=HIGHEST, preferred_element_type=F32) + b,)

    (mod_part,) = whole_call(mod_fn, [c_pad, w_mod[0], b_mine], [((128, n_mod), F32)], name="mod_fwd")
    gmod = gather_blocks(mod_part[:N_DEV], name="gather_mod")
    mod = lax.dynamic_index_in_dim(gmod, me, axis=1, keepdims=False).reshape(N_MOD, D)
    sh1, sc1, gt1, sh2, sc2, gt2 = [mod[i:i + 1] for i in range(N_MOD)]

    (h,) = rowwise_call(fwd_body(fn_modulate), [x2], [sc1, sh1], [(D, BF16)], [], tm=256, name="modulate1")
    proj = matmul(h, w_all, "nn", BF16, name="mm_proj", n=PROJ_W)
    dtg = matmul(h, w_dt, "nn", F32, name="mm_dt")
    ws = gm_ws[0]
    bs3 = gm_bs[0].reshape(GM_GROUPS, Q, 1)
    sgu_rows = [(proj, D, P_U), (proj, D, P_V)]
    (ya,) = rowwise_call(fwd_body(fn_sgu), sgu_rows, [gm_norm_w, ws, bs3], [(D, BF16)], [],
                         tm=256, name="sgu_fwd")
    xact = conv_fwd(proj, conv_w_full, conv_b)
    dtb4 = dt_bias.reshape(SSM_GROUPS, SSM_HPG, 1, 1)
    alog4 = a_log.reshape(SSM_GROUPS, SSM_HPG, 1, 1)
    dsk4 = d_skip.reshape(SSM_GROUPS, SSM_HPG, 1, 1)
    yb, states = ssd_fwd(xact, proj, dtg, dtb4, alog4, dsk4, ssm_norm_w)
    pa = matmul(ya, w_gm_f, "nn", F32, name="mm_branch_gm")
    pb = matmul(yb, w_ssm_f, "nn", F32, name="mm_branch_ssm")
    gate_rows = [(proj, D, P_GA), (proj, D, P_GB)]
    (mixed,) = rowwise_call(fwd_body(fn_mix), gate_rows + [pa, pb], [], [(D, BF16)], [], tm=256, name="mix_fwd")
    o = matmul(mixed, w_out_f, "nn", F32, name="mm_out")
    x1, h2 = rowwise_call(fwd_body(fn_res_modulate), [x2, o], [gt1, sc2, sh2], [(D, F32), (D, BF16)], [],
                          tm=256, name="res_modulate2")
    f = matmul(h2, w_ff1_f, "nn", BF16, name="mm_ff1")
    (act,) = rowwise_call(fwd_body(fn_relu2), [f], [], [(D_FF, BF16)], [], tm=256, name="relu2_fwd")
    gf = matmul(act, w_ff2_f, "nn", F32, name="mm_ff2")

    dx1, dgf, loss_v, dgt2, dfnw = rowwise_call(
        final_body, [x1, gf, tgt], [gt2, final_norm_w.reshape(1, D)], [(D, F32), (D, BF16)],
        [(1, 128), (1, D), (1, D)], tm=256, name="final_loss_bwd")
    dact = matmul(dgf, w_ff2_f, "nt", BF16, name="mm_ff2_dgrad")
    gw_ff2 = matmul(act, dgf, "tn", BF16, name="mm_ff2_wgrad", tk=512)
    (df,) = rowwise_call(bwd_body(fn_relu2, 1), [f, dact], [], [(D_FF, BF16)], [], tm=256, name="relu2_bwd")
    dh2 = matmul(df, w_ff1_f, "nt", F32, name="mm_ff1_dgrad")
    gw_ff1 = matmul(h2, df, "tn", BF16, name="mm_ff1_wgrad", tk=512)

    def res_mod_bwd(r, fl):
        xv, ov, dx1v, dh2v = r
        _, vjp = jax.vjp(fn_res_modulate, xv, ov, *fl)
        dxv, dov, dg1, dsc, dsh = vjp((dx1v, dh2v))
        return (dxv, dov), (dg1, dsc, dsh)

    dxa, do, dgt1, dsc2, dsh2 = rowwise_call(
        res_mod_bwd, [x2, o, dx1, dh2], [gt1, sc2, sh2], [(D, F32), (D, BF16)],
        [(1, D), (1, D), (1, D)], tm=256, name="res_modulate2_bwd")
    dmixed = matmul(do, w_out_f, "nt", F32, name="mm_out_dgrad")
    gw_out = matmul(mixed, do, "tn", BF16, name="mm_out_wgrad", tk=512)
    dproj = lax.empty((T, ALL_W), BF16)

    def mix_bwd(r, fl):
        dga, dgb, dpa, dpb = bwd_body(fn_mix, 4)(r, fl)[0]
        return (jnp.concatenate([dga, dgb], axis=1), dpa, dpb), ()

    dproj, dpa, dpb = rowwise_call(
        mix_bwd, gate_rows + [pa, pb, dmixed], [], [(dproj, 2 * D, P_GA), (D, BF16), (D, BF16)], [],
        tm=256, name="mix_bwd")
    dya = matmul(dpa, w_gm_f, "nt", F32, name="mm_branch_gm_dgrad")
    gw_gm = matmul(ya, dpa, "tn", BF16, name="mm_branch_gm_wgrad", tk=512)
    dyb = matmul(dpb, w_ssm_f, "nt", BF16, name="mm_branch_ssm_dgrad")
    gw_ssm = matmul(yb, dpb, "tn", BF16, name="mm_branch_ssm_wgrad", tk=512)

    def sgu_bwd(r, fl):
        (du, dv), acc = bwd_body(fn_sgu, 2)(r, fl)
        return (jnp.concatenate([du, dv], axis=1),), acc

    dproj, dgnw, dws, dbs = rowwise_call(
        sgu_bwd, sgu_rows + [dya], [gm_norm_w, ws, bs3], [(dproj, 2 * D, P_U)],
        [(1, D), (GM_GROUPS, Q, Q), (GM_GROUPS, Q, 1)], tm=256, name="sgu_bwd")
    dxs, dbm, dcm, dproj, ddt8, ddtb, dalog, ddsk, dsnw = ssd_bwd(
        xact, proj, dtg, dtb4, alog4, dsk4, ssm_norm_w, states, dyb, dproj)
    dconv_w, dconv_b = [], []
    for nm, dact_part, col0 in (("xs", dxs, 0), ("b", dbm, SSM_INNER), ("c", dcm, SSM_INNER + SSM_GROUPS * 128)):
        dproj, dcw, dcb = conv_bwd(proj, dact_part, col0, conv_w_full, conv_b, dproj, name="conv_bwd_" + nm)
        dconv_w.append(dcw)
        dconv_b.append(dcb)
    dconv_w = jnp.concatenate(dconv_w, axis=1)
    dconv_b = jnp.concatenate(dconv_b, axis=1)
    dproj = dproj.at[:, PROJ_W:].set(jnp.sum(ddt8, axis=0).astype(BF16))
    dh = matmul(dproj, w_all, "nt", F32, name="mm_in_dgrad", tk=1152)
    gw_all = matmul(h, dproj, "tn", BF16, name="mm_in_wgrad", tn=1152, tk=512)
    grad_x, dsc1, dsh1 = rowwise_call(grad_x_body, [x2, dh, dxa], [sc1, sh1], [(D, F32)],
                                      [(1, D), (1, D)], tm=256, name="modulate1_bwd")
    dmod = jnp.concatenate([dsh1, dsc1, dgt1, dsh2, dsc2, dgt2], axis=0)

    r_in, r_gm, r_ssm, r_out, r_ff1, r_ff2 = exchange(
        [(_windows_of_w_all(gw_all), _slot, (N_DEV, D, WIN_W), _slot),
         (gw_gm, _rows(D // N_DEV), (N_DEV, D // N_DEV, D), _slot),
         (gw_ssm, _rows(SSM_INNER // N_DEV), (N_DEV, SSM_INNER // N_DEV, D), _slot),
         (gw_out, _rows(D // N_DEV), (N_DEV, D // N_DEV, D), _slot),
         (gw_ff1, _cols(D_FF // N_DEV), (N_DEV, D, D_FF // N_DEV), _slot),
         (gw_ff2, _rows(D_FF // N_DEV), (N_DEV, D_FF // N_DEV, D), _slot)], name="alltoall_wgrads")
    g_w_in = _w_in_from_window(sum_devices(r_in, tr=256, name="sum_w_in_grads"), me).reshape(1, D, n_in)

    def rows128(a, rows):
        a = a.reshape(-1)
        return jnp.pad(a, (0, rows * 128 - a.shape[0])).reshape(rows, 128)

    small = [(dgnw, 8), (dws, 1024), (dbs, 8), (dconv_w, 128), (dconv_b, 32), (ddtb, 8), (dalog, 8),
             (ddsk, 8), (dsnw, 16), (dfnw, 8), (dmod, 48)]
    spack = jnp.concatenate([rows128(a, r) for a, r in small], axis=0)
    sall = gather_blocks(spack, name="gather_small_grads")
    ssum = sum_devices(sall, tr=SMALL_ROWS, name="sum_small_grads")
    soff = [0]
    for _, r in small:
        soff.append(soff[-1] + r)

    def spart(i, n):
        return ssum[soff[i]:soff[i + 1]].reshape(-1)[:n]

    g_gm_norm_w = spart(0, D).reshape(1, D)
    g_gm_ws = spart(1, GM_GROUPS * Q * Q).reshape(GM_GROUPS * Q, Q)
    g_gm_bs = spart(2, GM_GROUPS * Q).reshape(GM_GROUPS, Q)
    g_conv_w_full = spart(3, CONV_K * CONV_DIM).reshape(CONV_K, CONV_DIM)
    g_conv_w = lax.dynamic_slice(g_conv_w_full, (0, me * n_cv), (CONV_K, n_cv))
    g_conv_b = spart(4, CONV_DIM).reshape(1, CONV_DIM)
    g_dt_bias = spart(5, 32).reshape(1, 32)
    g_a_log = spart(6, 32).reshape(1, 32)
    g_d_skip = spart(7, 32).reshape(1, 32)
    g_ssm_norm_w = spart(8, SSM_INNER).reshape(1, SSM_INNER)
    g_final_norm_w = spart(9, D).reshape(1, D)
    g_b_mod = spart(10, N_MOD * D).reshape(1, N_MOD * D)

    dmod_all = sall[:, soff[10]:soff[11]].reshape(N_DEV, N_MOD * D)
    dmod_mine = _pad_rows(lax.dynamic_slice(dmod_all, (0, me * n_mod), (N_DEV, n_mod)), 128)

    def wmod_grad_fn(cp, dm):
        ca = cp * jax.nn.sigmoid(cp)
        return (lax.dot_general(ca, dm, (((0,), (0,)), ((), ())), precision=HIGHEST,
                                preferred_element_type=F32),)

    (g_w_mod,) = whole_call(wmod_grad_fn, [c_pad, dmod_mine], [((D, n_mod), F32)], name="w_mod_grad")

    upd = {}

    def step(name, w, g, m, v, parts=False):
        upd[name] = adamw(w, g if parts else g.reshape(w.shape), m, v, name="adamw_" + name)

    step("w_mod", w_mod, g_w_mod, m_w_mod, v_w_mod)
    step("b_mod", b_mod, g_b_mod, m_b_mod, v_b_mod)
    step("w_in", w_in, g_w_in, m_w_in, v_w_in)
    step("gm_norm_w", gm_norm_w, g_gm_norm_w, m_gm_norm_w, v_gm_norm_w)
    step("gm_ws", gm_ws, g_gm_ws, m_gm_ws, v_gm_ws)
    step("gm_bs", gm_bs, g_gm_bs, m_gm_bs, v_gm_bs)
    step("conv_w", conv_w, g_conv_w, m_conv_w, v_conv_w)
    step("conv_b", conv_b, g_conv_b, m_conv_b, v_conv_b)
    step("dt_bias", dt_bias, g_dt_bias, m_dt_bias, v_dt_bias)
    step("a_log", a_log, g_a_log, m_a_log, v_a_log)
    step("d_skip", d_skip, g_d_skip, m_d_skip, v_d_skip)
    step("ssm_norm_w", ssm_norm_w, g_ssm_norm_w, m_ssm_norm_w, v_ssm_norm_w)
    step("w_branch_gm", w_branch_gm, r_gm, m_w_branch_gm, v_w_branch_gm, parts=True)
    step("w_branch_ssm", w_branch_ssm, r_ssm, m_w_branch_ssm, v_w_branch_ssm, parts=True)
    step("w_out", w_out, r_out, m_w_out, v_w_out, parts=True)
    step("w_ff1", w_ff1, r_ff1, m_w_ff1, v_w_ff1, parts=True)
    step("w_ff2", w_ff2, r_ff2, m_w_ff2, v_w_ff2, parts=True)
    step("final_norm_w", final_norm_w.reshape(1, D), g_final_norm_w, m_final_norm_w.reshape(1, D),
         v_final_norm_w.reshape(1, D))
    upd["final_norm_w"] = tuple(a.reshape(D) for a in upd["final_norm_w"])

    loss = lax.psum(loss_v[0, 0], ("x", "y", "c"))
    order = ["w_mod", "b_mod", "w_in", "gm_norm_w", "gm_ws", "gm_bs", "conv_w", "conv_b", "dt_bias", "a_log",
             "d_skip", "ssm_norm_w", "w_branch_gm", "w_branch_ssm", "w_out", "w_ff1", "w_ff2", "final_norm_w"]
    return (loss, grad_x.reshape(1, T, D),
            *[upd[n][0] for n in order], *[upd[n][1] for n in order],
            *[upd[n][2] for n in order], *[upd[n][3] for n in order])
```

```python
import functools

import jax
import jax.numpy as jnp
from jax import lax
from jax.experimental import pallas as pl
from jax.experimental.pallas import tpu as pltpu

F32 = jnp.float32
BF16 = jnp.bfloat16
MESH = pl.DeviceIdType.MESH
HIGHEST = lax.Precision.HIGHEST

N_DEV = 8
D = 1024
Q = 128
GM_GROUPS = 8
SSM_INNER = 2048
SSM_GROUPS = 8
SSM_HPG = 4
SSM_P = 64
SSM_GW = SSM_HPG * SSM_P
CONV_DIM = 4096
CONV_K = 4
D_FF = 4096
N_MOD = 6
EPS = 1e-6
IN_WIDTH = 10272
OFF_DT = 8192
OFF_GA = 8224
PROJ_W = 10240
ALL_W = 10368
P_U, P_V, P_Z, P_XBC, P_GA, P_GB = 0, 1024, 2048, 4096, 8192, 9216

ADAM_LR = 0.001
ADAM_B1 = 0.9
ADAM_B2 = 0.999
ADAM_EPS = 1e-08
ADAM_WD = 0.01
ADAM_STEP = 10

VMEM_LIMIT_BYTES = 48 * 1024 * 1024
SMALL_ROWS = 1296


def _params(sem=None):
    return pltpu.CompilerParams(dimension_semantics=sem, vmem_limit_bytes=VMEM_LIMIT_BYTES)


def _dg(a, b, ca, cb):
    return lax.dot_general(a.astype(BF16), b.astype(BF16), (((ca,), (cb,)), ((), ())),
                           preferred_element_type=F32)


@jax.custom_vjp
def dot_nn(a, b):
    return _dg(a, b, 1, 0)


@jax.custom_vjp
def dot_nt(a, b):
    return _dg(a, b, 1, 1)


@jax.custom_vjp
def dot_tn(a, b):
    return _dg(a, b, 0, 0)


def _like(ct, primal):
    return ct.astype(primal.dtype)


dot_nn.defvjp(lambda a, b: (dot_nn(a, b), (a, b)),
              lambda r, g: (_like(dot_nt(g, r[1]), r[0]), _like(dot_tn(r[0], g), r[1])))
dot_nt.defvjp(lambda a, b: (dot_nt(a, b), (a, b)),
              lambda r, g: (_like(dot_nn(g, r[1]), r[0]), _like(dot_tn(g, r[0]), r[1])))
dot_tn.defvjp(lambda a, b: (dot_tn(a, b), (a, b)),
              lambda r, g: (_like(dot_nt(r[1], g), r[0]), _like(dot_nn(r[0], g), r[1])))


def _rms(x):
    return x * lax.rsqrt(jnp.mean(x * x, axis=-1, keepdims=True) + EPS)


def _softplus(x):
    return jnp.maximum(x, 0.0) + jnp.log1p(jnp.exp(-jnp.abs(x)))


def _rows(n):
    return lambda ref, j: ref.at[pl.ds(pl.multiple_of(j * n, n), n)]


def _cols(n):
    return lambda ref, j: ref.at[:, pl.ds(pl.multiple_of(j * n, n), n)]


def _slot(ref, j):
    return ref.at[j]


def _whole(ref, j):
    return ref


def exchange(items, *, name):
    n = len(items)

    def body(*refs):
        exchange_in_body(items, refs[:n], refs[n:2 * n], refs[2 * n:], True, True)

    return pl.pallas_call(
        body, name=name,
        out_shape=exchange_out_shapes(items),
        in_specs=[pl.BlockSpec(memory_space=pl.ANY)] * n,
        out_specs=[pl.BlockSpec(memory_space=pl.ANY)] * n,
        scratch_shapes=exchange_semaphores(items),
    )(*[it[0] for it in items])


def exchange_out_shapes(items):
    return [jax.ShapeDtypeStruct(tuple(shape), src.dtype) for (src, _, shape, _) in items]


def exchange_semaphores(items):
    n = len(items)
    return [pltpu.SemaphoreType.DMA((n, N_DEV - 1)), pltpu.SemaphoreType.DMA((n, N_DEV - 1)),
            pltpu.SemaphoreType.DMA((n,))]


def _exchange_copies(items, src_refs, out_refs, sems):
    send_sems, recv_sems, local_sems = sems
    x = lax.axis_index("x")
    y = lax.axis_index("y")
    c = lax.axis_index("c")
    me = 4 * x + 2 * y + c
    local = [pltpu.make_async_copy(src_win(src_refs[i], me), dst_win(out_refs[i], me), local_sems.at[i])
             for i, (_, src_win, _, dst_win) in enumerate(items)]
    remote = []
    for i, (_, src_win, _, dst_win) in enumerate(items):
        for k in range(1, N_DEV):
            px = lax.rem(x + ((k >> 2) & 1), 2)
            py = lax.rem(y + ((k >> 1) & 1), 2)
            pc = lax.rem(c + (k & 1), 2)
            peer = 4 * px + 2 * py + pc
            remote.append(pltpu.make_async_remote_copy(
                src_ref=src_win(src_refs[i], peer), dst_ref=dst_win(out_refs[i], me),
                send_sem=send_sems.at[i, k - 1], recv_sem=recv_sems.at[i, k - 1],
                device_id=(px, py, pc), device_id_type=MESH))
    return local, remote


def _when(cond, fn):
    if cond is True:
        fn()
    else:
        pl.when(cond)(fn)


def exchange_start(items, src_refs, out_refs, sems, cond):
    def start():
        local, remote = _exchange_copies(items, src_refs, out_refs, sems)
        for cp in local + remote:
            cp.start()
    _when(cond, start)


def exchange_finish(items, src_refs, out_refs, sems, cond):
    def finish():
        local, remote = _exchange_copies(items, src_refs, out_refs, sems)
        for cp in remote:
            cp.wait_send()
        for cp in remote:
            cp.wait_recv()
        for cp in local:
            cp.wait()
    _when(cond, finish)


def exchange_in_body(items, src_refs, out_refs, sems, first, last):
    exchange_start(items, src_refs, out_refs, sems, first)
    exchange_finish(items, src_refs, out_refs, sems, last)


def gather_blocks(src, *, name):
    return exchange([(src, _whole, (N_DEV,) + src.shape, _slot)], name=name)[0]


def sum_devices(g, *, tr, name):
    _, R, C = g.shape

    def body(g_ref, o_ref):
        acc = g_ref[0].astype(F32)
        for j in range(1, N_DEV):
            acc = acc + g_ref[j].astype(F32)
        o_ref[...] = acc

    return pl.pallas_call(
        body, name=name, grid=(R // tr,),
        out_shape=jax.ShapeDtypeStruct((R, C), F32),
        in_specs=[pl.BlockSpec((N_DEV, tr, C), lambda i: (0, i, 0))],
        out_specs=pl.BlockSpec((tr, C), lambda i: (i, 0)),
        compiler_params=_params(("parallel",)),
    )(g)


def matmul(a, b, mode, out_dtype, *, name, tm=1024, tn=1024, tk=1024, n=None, comm=None):
    if mode == "nn":
        (M, K), (K2, N) = a.shape, b.shape
    elif mode == "nt":
        (M, K), (N, K2) = a.shape, b.shape
    else:
        (K, M), (K2, N) = a.shape, b.shape
    assert K == K2
    N = N if n is None else n
    tm, tn, tk = min(tm, M), min(tn, N), min(tk, K)
    assert M % tm == 0 and N % tn == 0 and K % tk == 0, (name, M, N, K, tm, tn, tk)
    nk = K // tk
    if mode == "tn":
        a_spec = pl.BlockSpec((tk, tm), lambda i, j, k: (k, i))
    else:
        a_spec = pl.BlockSpec((tm, tk), lambda i, j, k: (i, k))
    if mode == "nt":
        b_spec = pl.BlockSpec((tn, tk), lambda i, j, k: (j, k))
    else:
        b_spec = pl.BlockSpec((tk, tn), lambda i, j, k: (k, j))
    dims = {"nn": (1, 0), "nt": (1, 1), "tn": (0, 0)}[mode]
    items = list(comm) if comm else []
    nx = len(items)
    gm, gn = M // tm, N // tn
    any_spec = pl.BlockSpec(memory_space=pl.ANY)

    def body(*refs):
        a_ref, b_ref = refs[:2]
        src_refs, o_ref, out_refs = refs[2:2 + nx], refs[2 + nx], refs[3 + nx:3 + 2 * nx]
        acc_ref, sems = refs[3 + 2 * nx], refs[4 + 2 * nx:]
        i, j, k = pl.program_id(0), pl.program_id(1), pl.program_id(2)
        if items:
            exchange_start(items, src_refs, out_refs, sems, (i == 0) & (j == 0) & (k == 0))
        part = lax.dot_general(a_ref[...], b_ref[...], (((dims[0],), (dims[1],)), ((), ())),
                               preferred_element_type=F32)

        @pl.when(k == 0)
        def _():
            acc_ref[...] = part

        @pl.when(k > 0)
        def _():
            acc_ref[...] += part

        @pl.when(k == nk - 1)
        def _():
            o_ref[...] = acc_ref[...].astype(o_ref.dtype)

        if items:
            exchange_finish(items, src_refs, out_refs, sems, (i == gm - 1) & (j == gn - 1) & (k == nk - 1))

    res = pl.pallas_call(
        body, name=name, grid=(gm, gn, nk),
        out_shape=[jax.ShapeDtypeStruct((M, N), out_dtype)] + exchange_out_shapes(items),
        in_specs=[a_spec, b_spec] + [any_spec] * nx,
        out_specs=[pl.BlockSpec((tm, tn), lambda i, j, k: (i, j))] + [any_spec] * nx,
        scratch_shapes=[pltpu.VMEM((tm, tn), F32)] + (exchange_semaphores(items) if items else []),
        compiler_params=_params(("arbitrary",) * 3 if items else ("parallel", "parallel", "arbitrary")),
    )(a, b, *[it[0] for it in items])
    return res if items else res[0]


def rowwise_call(body_fn, rows, fulls, row_outs, acc_outs, *, tm, name):
    rows = [r if isinstance(r, tuple) else (r, r.shape[1], 0) for r in rows]
    T = rows[0][0].shape[0]
    tm = min(tm, T)
    assert T % tm == 0
    n_r, n_f, n_ro = len(rows), len(fulls), len(row_outs)
    into = [(k, ro) for k, ro in enumerate(row_outs) if len(ro) == 3]
    n_b = len(into)

    def body(*refs):
        r_refs = refs[:n_r]
        f_refs = refs[n_r:n_r + n_f]
        refs = refs[n_r + n_f + n_b:]
        ro_refs = refs[:n_ro]
        ao_refs = refs[n_ro:]
        r_vals = [r[...].astype(F32) for r in r_refs]
        f_vals = [f[...].astype(F32) for f in f_refs]
        ro, ao = body_fn(r_vals, f_vals)
        for ref, v in zip(ro_refs, ro):
            ref[...] = v.astype(ref.dtype)
        if ao_refs:
            @pl.when(pl.program_id(0) == 0)
            def _():
                for ref in ao_refs:
                    ref[...] = jnp.zeros(ref.shape, F32)
            for ref, v in zip(ao_refs, ao):
                ref[...] += v.reshape(ref.shape)

    def full_spec(shape):
        nd = len(shape)
        return pl.BlockSpec(tuple(shape), lambda i: (0,) * nd)

    in_specs = [pl.BlockSpec((tm, w), functools.partial(lambda i, o: (i, o), o=off // w))
                for (_, w, off) in rows]
    in_specs += [full_spec(f.shape) for f in fulls]
    in_specs += [pl.BlockSpec(memory_space=pl.ANY)] * n_b
    out_specs, out_shape = [], []
    for ro in row_outs:
        if len(ro) == 3:
            buf, w, off = ro
            out_specs.append(pl.BlockSpec((tm, w), functools.partial(lambda i, o: (i, o), o=off // w)))
            out_shape.append(jax.ShapeDtypeStruct(buf.shape, buf.dtype))
        else:
            w, dt = ro
            out_specs.append(pl.BlockSpec((tm, w), lambda i: (i, 0)))
            out_shape.append(jax.ShapeDtypeStruct((T, w), dt))
    out_specs += [full_spec(s) for s in acc_outs]
    out_shape += [jax.ShapeDtypeStruct(tuple(s), F32) for s in acc_outs]
    aliases = {n_r + n_f + b: k for b, (k, _) in enumerate(into)}
    return pl.pallas_call(
        body, name=name, grid=(T // tm,),
        out_shape=out_shape, in_specs=in_specs, out_specs=out_specs,
        input_output_aliases=aliases,
        compiler_params=_params(("arbitrary",)),
    )(*[r[0] for r in rows], *fulls, *[ro[0] for _, ro in into])


def fwd_body(fn):
    return lambda r, f: (fn(*r, *f), ())


def bwd_body(fn, n_rows):
    def body(r, f):
        ins, cots = r[:n_rows], r[n_rows:]
        _, vjp = jax.vjp(fn, *ins, *f)
        g = vjp(tuple(cots))
        return g[:n_rows], g[n_rows:]
    return body


def whole_call(fn, ins, outs, *, name):
    n_in = len(ins)

    def body(*refs):
        res = fn(*[r[...] for r in refs[:n_in]])
        for ref, v in zip(refs[n_in:], res):
            ref[...] = v.astype(ref.dtype)

    return pl.pallas_call(
        body, name=name,
        out_shape=[jax.ShapeDtypeStruct(tuple(s), dt) for (s, dt) in outs],
        compiler_params=_params(),
    )(*ins)


def fn_modulate(x, sc, sh):
    return (_rms(x) * (1.0 + sc) + sh,)


def fn_sgu(u, v, nw, ws, bs):
    ug = jax.nn.gelu(u)
    vn = _rms(jax.nn.gelu(v)) * nw
    ri = lax.broadcasted_iota(jnp.int32, (Q, Q), 0)
    ci = lax.broadcasted_iota(jnp.int32, (Q, Q), 1)
    causal = ri >= ci
    chunks = []
    for n in range(u.shape[0] // Q):
        vc = vn[n * Q:(n + 1) * Q]
        cols = [dot_nn(jnp.where(causal, ws[g], 0.0), vc[:, g * Q:(g + 1) * Q]) + bs[g]
                for g in range(GM_GROUPS)]
        chunks.append(jnp.concatenate(cols, axis=1))
    sv = chunks[0] if len(chunks) == 1 else jnp.concatenate(chunks, axis=0)
    return (ug * sv,)


def fn_mix(ga, gb, pa, pb):
    return (jax.nn.sigmoid(ga) * pa + jax.nn.sigmoid(gb) * pb,)


def fn_res_modulate(x, o, g1, sc2, sh2):
    x1 = x + g1 * o
    return x1, _rms(x1) * (1.0 + sc2) + sh2


def fn_relu2(f):
    return (jnp.square(jnp.maximum(f, 0.0)),)


def final_body(r, f):
    x1, gf, tgt = r
    g2, fnw = f

    def loss_fn(x1, gf, g2, fnw):
        y = _rms(x1 + g2 * gf) * fnw
        row = 0.5 * jnp.mean(jnp.square(y - tgt), axis=-1, keepdims=True)
        return jnp.sum(row, axis=0, keepdims=True)

    l, vjp = jax.vjp(loss_fn, x1, gf, g2, fnw)
    dx1, dgf, dg2, dfnw = vjp(jnp.ones((1, 1), F32))
    return (dx1, dgf), (jnp.broadcast_to(l, (1, 128)), dg2, dfnw)


def grad_x_body(r, f):
    x, dh, dxa = r
    _, vjp = jax.vjp(fn_modulate, x, *f)
    dx, dsc, dsh = vjp((dh,))
    return (dx + dxa,), (dsc, dsh)


CONV_CW = 128
CONV_PAD = 8


def _conv_pre(xp, w_ref, b_ref, r0, R):
    acc = b_ref[...] + w_ref[0:1, :] * xp[r0 + CONV_PAD - 3:r0 + CONV_PAD - 3 + R, :]
    for k in range(1, CONV_K):
        s = r0 + CONV_PAD - 3 + k
        acc = acc + w_ref[k:k + 1, :] * xp[s:s + R, :]
    return acc


def conv_fwd(proj, conv_w, conv_b):
    T = proj.shape[0]
    R = min(512, T)

    def body(x_ref, w_ref, b_ref, o_ref, xp):
        xp[0:CONV_PAD, :] = jnp.zeros((CONV_PAD, CONV_CW), F32)
        xp[CONV_PAD:CONV_PAD + T, :] = x_ref[...].astype(F32)
        for r0 in range(0, T, R):
            pre = _conv_pre(xp, w_ref, b_ref, r0, R)
            o_ref[r0:r0 + R, :] = (pre * jax.nn.sigmoid(pre)).astype(o_ref.dtype)

    return pl.pallas_call(
        body, name="conv_fwd", grid=(CONV_DIM // CONV_CW,),
        out_shape=jax.ShapeDtypeStruct((T, CONV_DIM), BF16),
        in_specs=[pl.BlockSpec((T, CONV_CW), lambda j: (0, P_XBC // CONV_CW + j)),
                  pl.BlockSpec((CONV_K, CONV_CW), lambda j: (0, j)),
                  pl.BlockSpec((1, CONV_CW), lambda j: (0, j))],
        out_specs=pl.BlockSpec((T, CONV_CW), lambda j: (0, j)),
        scratch_shapes=[pltpu.VMEM((T + CONV_PAD, CONV_CW), F32)],
        compiler_params=_params(("parallel",)),
    )(proj, conv_w, conv_b)


def conv_bwd(proj, dact, col0, conv_w, conv_b, dproj, *, name):
    T = proj.shape[0]
    R = min(512, T)
    nb = dact.shape[1] // CONV_CW
    c0 = col0 // CONV_CW
    x0 = (P_XBC + col0) // CONV_CW

    def body(x_ref, d_ref, w_ref, b_ref, _, dx_ref, dw_ref, db_ref, xp, dp):
        xp[0:CONV_PAD, :] = jnp.zeros((CONV_PAD, CONV_CW), F32)
        xp[CONV_PAD:CONV_PAD + T, :] = x_ref[...].astype(F32)
        dp[T:T + CONV_PAD, :] = jnp.zeros((CONV_PAD, CONV_CW), F32)
        dws = [jnp.zeros((1, CONV_CW), F32) for _ in range(CONV_K)]
        db = jnp.zeros((1, CONV_CW), F32)
        for r0 in range(0, T, R):
            pre = _conv_pre(xp, w_ref, b_ref, r0, R)
            s = jax.nn.sigmoid(pre)
            dpre = d_ref[r0:r0 + R, :].astype(F32) * (s * (1.0 + pre * (1.0 - s)))
            dp[r0:r0 + R, :] = dpre
            db = db + jnp.sum(dpre, axis=0, keepdims=True)
            for k in range(CONV_K):
                st = r0 + CONV_PAD - 3 + k
                dws[k] = dws[k] + jnp.sum(dpre * xp[st:st + R, :], axis=0, keepdims=True)
        for r0 in range(0, T, R):
            acc = w_ref[0:1, :] * dp[r0 + 3:r0 + 3 + R, :]
            for k in range(1, CONV_K):
                acc = acc + w_ref[k:k + 1, :] * dp[r0 + 3 - k:r0 + 3 - k + R, :]
            dx_ref[r0:r0 + R, :] = acc.astype(dx_ref.dtype)
        for k in range(CONV_K):
            dw_ref[k:k + 1, :] = dws[k]
        db_ref[...] = db

    return pl.pallas_call(
        body, name=name, grid=(nb,),
        out_shape=[jax.ShapeDtypeStruct(dproj.shape, dproj.dtype),
                   jax.ShapeDtypeStruct((CONV_K, nb * CONV_CW), F32),
                   jax.ShapeDtypeStruct((1, nb * CONV_CW), F32)],
        in_specs=[pl.BlockSpec((T, CONV_CW), lambda j: (0, x0 + j)),
                  pl.BlockSpec((T, CONV_CW), lambda j: (0, j)),
                  pl.BlockSpec((CONV_K, CONV_CW), lambda j: (0, c0 + j)),
                  pl.BlockSpec((1, CONV_CW), lambda j: (0, c0 + j)),
                  pl.BlockSpec(memory_space=pl.ANY)],
        out_specs=[pl.BlockSpec((T, CONV_CW), lambda j: (0, x0 + j)),
                   pl.BlockSpec((CONV_K, CONV_CW), lambda j: (0, j)),
                   pl.BlockSpec((1, CONV_CW), lambda j: (0, j))],
        scratch_shapes=[pltpu.VMEM((T + CONV_PAD, CONV_CW), F32),
                        pltpu.VMEM((T + CONV_PAD, CONV_CW), F32)],
        input_output_aliases={4: 0},
        compiler_params=_params(("parallel",)),
    )(proj, dact, conv_w, conv_b, dproj)


def _split3(a):
    hi = a.astype(BF16)
    r = a - hi.astype(F32)
    mid = r.astype(BF16)
    return hi, mid, (r - mid.astype(F32)).astype(BF16)


def _dg3(a, m, ca, cm, a_first):
    dims = (((ca,), (cm,)), ((), ())) if a_first else (((cm,), (ca,)), ((), ()))
    out = None
    for p in _split3(a):
        t = lax.dot_general(p, m, dims, preferred_element_type=F32) if a_first else \
            lax.dot_general(m, p, dims, preferred_element_type=F32)
        out = t if out is None else out + t
    return out


@jax.custom_vjp
def exact_right(a, m):
    return _dg3(a, m, 1, 0, True)


@jax.custom_vjp
def exact_left(m, a):
    return _dg3(a, m, 0, 1, False)


exact_right.defvjp(lambda a, m: (exact_right(a, m), m),
                   lambda m, g: (dot_nt(g, m), jnp.zeros_like(m)))
exact_left.defvjp(lambda m, a: (exact_left(m, a), m),
                  lambda m, g: (jnp.zeros_like(m), _dg3(g, m, 0, 0, False)))


def ssd_step(lane0, state, x, z, dtr, Bm, Cm, dtb, alog, dsk, nw):
    def iota(shape, dim):
        return lax.broadcasted_iota(jnp.int32, shape, dim)

    def one_hot(mask):
        return mask.astype(F32).astype(BF16)

    causal = iota((Q, Q), 0) >= iota((Q, Q), 1)
    eye = iota((Q, Q), 0) == iota((Q, Q), 1)
    lane = iota((1, 128), 1)
    colh = lax.shift_right_logical(iota((1, SSM_GW), 1), 6)
    to_cols = one_hot(iota((128, SSM_GW), 0) == lane0 + colh)

    dt_all = _softplus(dtr + dtb)
    a_all = dt_all * (-jnp.exp(alog))
    cum_all = exact_left(one_hot(causal), a_all)
    both = exact_right(jnp.concatenate([dt_all, cum_all], axis=0), to_cols)
    dt_f, cum_f = both[:Q], both[Q:]
    last_f = jnp.sum(jnp.where(iota((Q, 1), 0) == Q - 1, cum_f, 0.0), axis=0, keepdims=True)
    dsk_f = jnp.zeros((1, SSM_GW), F32)
    for h in range(SSM_HPG):
        dsk_f = jnp.where(colh == h, dsk[h], dsk_f)

    xdt = x * dt_f
    cb = dot_nt(Cm, Bm)
    ms, rhs = [], []
    for h in range(SSM_HPG):
        ch = jnp.sum(jnp.where(lane == lane0 + h, cum_all, 0.0), axis=1, keepdims=True)
        ch_t = jnp.sum(jnp.where(eye, ch, 0.0), axis=0, keepdims=True)
        ms.append(cb * jnp.exp(jnp.where(causal, ch - ch_t, -1e30)))
        rhs.append(jnp.where(colh == h, xdt, 0.0))
    y = dot_nn(jnp.concatenate(ms, axis=1), jnp.concatenate(rhs, axis=0))
    y = y + dot_nn(Cm, state) * jnp.exp(cum_f) + x * dsk_f
    new_state = state * jnp.exp(last_f) + dot_tn(Bm, xdt * jnp.exp(last_f - cum_f))
    gated = y * (z * jax.nn.sigmoid(z))
    return new_state, _rms(gated) * nw


SSD_GPS = 4
_XW = SSD_GPS * SSM_GW
_BW = SSD_GPS * 128


def _ssd_in_specs(rev, nc):
    def n_of(n):
        return nc - 1 - n if rev else n
    return [
        pl.BlockSpec((Q, _XW), lambda g, n: (n_of(n), g)),
        pl.BlockSpec((Q, _BW), lambda g, n: (n_of(n), SSM_INNER // _BW + g)),
        pl.BlockSpec((Q, _BW), lambda g, n: (n_of(n), (SSM_INNER + SSM_GROUPS * 128) // _BW + g)),
        pl.BlockSpec((Q, _XW), lambda g, n: (n_of(n), P_Z // _XW + g)),
        pl.BlockSpec((Q, 128), lambda g, n: (n_of(n), 0)),
        pl.BlockSpec((1, 128), lambda g, n: (0, 0)),
        pl.BlockSpec((1, 128), lambda g, n: (0, 0)),
        pl.BlockSpec((SSD_GPS, SSM_HPG, 1, 1), lambda g, n: (g, 0, 0, 0)),
        pl.BlockSpec((1, _XW), lambda g, n: (0, g)),
    ]


def _ssd_group_inputs(gi, x_ref, b_ref, c_ref, z_ref, dt_ref, dtb_ref, al_ref, dk_ref, nw_ref):
    xs = slice(gi * SSM_GW, (gi + 1) * SSM_GW)
    bs = slice(gi * 128, (gi + 1) * 128)
    return (x_ref[:, xs].astype(F32), z_ref[:, xs].astype(F32), dt_ref[...],
            b_ref[:, bs], c_ref[:, bs],
            dtb_ref[...], al_ref[...], dk_ref[gi], nw_ref[:, xs])


def ssd_fwd(xact, proj, dtg, dtb, alog, dsk, nw, comm):
    T = xact.shape[0]
    nc = T // Q
    nx = len(comm)
    ng = SSM_GROUPS // SSD_GPS
    any_spec = pl.BlockSpec(memory_space=pl.ANY)

    def body(*refs):
        in_refs, src_refs = refs[:9], refs[9:9 + nx]
        yb_ref, st_ref = refs[9 + nx:11 + nx]
        out_refs, state, sems = refs[11 + nx:11 + 2 * nx], refs[11 + 2 * nx], refs[12 + 2 * nx:]
        g, n = pl.program_id(0), pl.program_id(1)
        exchange_start(comm, src_refs, out_refs, sems, (g == 0) & (n == 0))

        @pl.when(n == 0)
        def _():
            state[...] = jnp.zeros(state.shape, F32)

        for gi in range(SSD_GPS):
            lane0 = SSM_HPG * (SSD_GPS * pl.program_id(0) + gi)
            s = state[gi]
            st_ref[gi, 0] = s
            new_s, yb = ssd_step(lane0, s, *_ssd_group_inputs(gi, *in_refs))
            state[gi] = new_s
            yb_ref[:, gi * SSM_GW:(gi + 1) * SSM_GW] = yb.astype(yb_ref.dtype)

        exchange_finish(comm, src_refs, out_refs, sems, (g == ng - 1) & (n == nc - 1))

    return pl.pallas_call(
        body, name="ssd_fwd", grid=(ng, nc),
        out_shape=[jax.ShapeDtypeStruct((T, SSM_INNER), BF16),
                   jax.ShapeDtypeStruct((SSM_GROUPS, nc, 128, SSM_GW), F32)] + exchange_out_shapes(comm),
        in_specs=_ssd_in_specs(False, nc) + [any_spec] * nx,
        out_specs=[pl.BlockSpec((Q, _XW), lambda g, n: (n, g)),
                   pl.BlockSpec((SSD_GPS, 1, 128, SSM_GW), lambda g, n: (g, n, 0, 0))] + [any_spec] * nx,
        scratch_shapes=[pltpu.VMEM((SSD_GPS, 128, SSM_GW), F32)] + exchange_semaphores(comm),
        compiler_params=_params(("arbitrary", "arbitrary")),
    )(xact, xact, xact, proj, dtg, dtb, alog, dsk, nw, *[it[0] for it in comm])


def ssd_bwd(xact, proj, dtg, dtb, alog, dsk, nw, states, dyb, dproj, comm):
    T = xact.shape[0]
    nc = T // Q

    nx = len(comm)
    ng = SSM_GROUPS // SSD_GPS
    any_spec = pl.BlockSpec(memory_space=pl.ANY)

    def body(*refs):
        in_refs, (st_ref, dy_ref, _) = refs[:9], refs[9:12]
        src_refs, refs = refs[12:12 + nx], refs[12 + nx:]
        dx_ref, db_ref, dc_ref, dz_ref, ddt_ref, ddtb_ref, dal_ref, ddk_ref, dnw_ref = refs[:9]
        out_refs, dstate, sems = refs[9:9 + nx], refs[9 + nx], refs[10 + nx:]
        exchange_start(comm, src_refs, out_refs, sems, (pl.program_id(0) == 0) & (pl.program_id(1) == 0))

        @pl.when(pl.program_id(1) == 0)
        def _():
            dstate[...] = jnp.zeros(dstate.shape, F32)
            ddtb_ref[...] = jnp.zeros(ddtb_ref.shape, F32)
            dal_ref[...] = jnp.zeros(dal_ref.shape, F32)
            ddk_ref[...] = jnp.zeros(ddk_ref.shape, F32)
            dnw_ref[...] = jnp.zeros(dnw_ref.shape, F32)

        for gi in range(SSD_GPS):
            xs = slice(gi * SSM_GW, (gi + 1) * SSM_GW)
            bs = slice(gi * 128, (gi + 1) * 128)
            lane0 = SSM_HPG * (SSD_GPS * pl.program_id(0) + gi)
            ins = (st_ref[gi, 0],) + _ssd_group_inputs(gi, *in_refs)
            _, vjp = jax.vjp(functools.partial(ssd_step, lane0), *ins)
            ds, dx, dz, ddt, dbm, dcm, ddtb, dal, ddk, dnw = vjp((dstate[gi], dy_ref[:, xs].astype(F32)))
            dstate[gi] = ds
            dx_ref[:, xs] = dx.astype(dx_ref.dtype)
            db_ref[:, bs] = dbm.astype(db_ref.dtype)
            dc_ref[:, bs] = dcm.astype(dc_ref.dtype)
            dz_ref[:, xs] = dz.astype(dz_ref.dtype)
            ddt_ref[gi] = ddt
            ddtb_ref[gi] += ddtb
            dal_ref[gi] += dal
            ddk_ref[gi] += ddk
            dnw_ref[:, xs] += dnw

        exchange_finish(comm, src_refs, out_refs, sems,
                        (pl.program_id(0) == ng - 1) & (pl.program_id(1) == nc - 1))

    rev = lambda n: nc - 1 - n
    row_shape = jax.ShapeDtypeStruct((SSM_GROUPS, 1, 128), F32)
    row_spec = pl.BlockSpec((SSD_GPS, 1, 128), lambda g, n: (g, 0, 0))
    return pl.pallas_call(
        body, name="ssd_bwd", grid=(ng, nc),
        out_shape=[jax.ShapeDtypeStruct((T, SSM_INNER), BF16),
                   jax.ShapeDtypeStruct((T, SSM_GROUPS * 128), BF16),
                   jax.ShapeDtypeStruct((T, SSM_GROUPS * 128), BF16),
                   jax.ShapeDtypeStruct(dproj.shape, dproj.dtype),
                   jax.ShapeDtypeStruct((SSM_GROUPS, T, 128), F32),
                   row_shape, row_shape,
                   jax.ShapeDtypeStruct((SSM_GROUPS, SSM_HPG, 1, 1), F32),
                   jax.ShapeDtypeStruct((1, SSM_INNER), F32)] + exchange_out_shapes(comm),
        in_specs=_ssd_in_specs(True, nc) + [
            pl.BlockSpec((SSD_GPS, 1, 128, SSM_GW), lambda g, n: (g, rev(n), 0, 0)),
            pl.BlockSpec((Q, _XW), lambda g, n: (rev(n), g)),
            any_spec] + [any_spec] * nx,
        out_specs=[pl.BlockSpec((Q, _XW), lambda g, n: (rev(n), g)),
                   pl.BlockSpec((Q, _BW), lambda g, n: (rev(n), g)),
                   pl.BlockSpec((Q, _BW), lambda g, n: (rev(n), g)),
                   pl.BlockSpec((Q, _XW), lambda g, n: (rev(n), P_Z // _XW + g)),
                   pl.BlockSpec((SSD_GPS, Q, 128), lambda g, n: (g, rev(n), 0)),
                   row_spec, row_spec,
                   pl.BlockSpec((SSD_GPS, SSM_HPG, 1, 1), lambda g, n: (g, 0, 0, 0)),
                   pl.BlockSpec((1, _XW), lambda g, n: (0, g))] + [any_spec] * nx,
        scratch_shapes=[pltpu.VMEM((SSD_GPS, 128, SSM_GW), F32)] + exchange_semaphores(comm),
        input_output_aliases={11: 3},
        compiler_params=_params(("arbitrary", "arbitrary")),
    )(xact, xact, xact, proj, dtg, dtb, alog, dsk, nw, states, dyb, dproj, *[it[0] for it in comm])


ADAMW_WHOLE_ELEMS = 256 * 1024


def adamw(w, g, m, v, *, name):
    shape = w.shape
    parts = g.shape != shape
    nd = len(shape)
    if w.size <= ADAMW_WHOLE_ELEMS:
        grid, tr = (1,), shape[-2]
    else:
        assert all(s == 1 for s in shape[:-2]) and shape[-2] % 256 == 0
        grid, tr = (shape[-2] // 256,), 256
    blk = tuple(shape[:-2]) + (tr, shape[-1])
    spec = pl.BlockSpec(blk, lambda i: (0,) * (nd - 2) + (i, 0))
    g_spec = pl.BlockSpec((N_DEV,) + blk[1:], lambda i: (0,) * (nd - 2) + (i, 0)) if parts else spec

    def body(w_ref, g_ref, m_ref, v_ref, go_ref, d_ref, nm_ref, nv_ref):
        if parts:
            g = g_ref[0:1].astype(F32)
            for j in range(1, N_DEV):
                g = g + g_ref[j:j + 1].astype(F32)
        else:
            g = g_ref[...]
        nm = ADAM_B1 * m_ref[...] + (1.0 - ADAM_B1) * g
        nv = ADAM_B2 * v_ref[...] + (1.0 - ADAM_B2) * jnp.square(g)
        m_hat = nm / (1.0 - ADAM_B1 ** ADAM_STEP)
        v_hat = nv / (1.0 - ADAM_B2 ** ADAM_STEP)
        go_ref[...] = g
        d_ref[...] = -ADAM_LR * (m_hat / (jnp.sqrt(v_hat) + ADAM_EPS) + ADAM_WD * w_ref[...])
        nm_ref[...] = nm
        nv_ref[...] = nv

    shp = jax.ShapeDtypeStruct(shape, F32)
    return pl.pallas_call(
        body, name=name, grid=grid,
        out_shape=[shp] * 4, in_specs=[spec, g_spec, spec, spec], out_specs=[spec] * 4,
        compiler_params=_params(("parallel",)),
    )(w, g, m, v)


def _pad_rows(a, rows):
    return jnp.pad(a, ((0, rows - a.shape[0]), (0, 0)))


WIN_W = 1408
N_IN = IN_WIDTH // N_DEV
_A6 = OFF_DT - 6 * N_IN
_C6 = 7 * N_IN - OFF_GA


def _win_offset(me):
    return jnp.where(me == 7, 124, 4 * me)


def _w_in_window(shard, me):
    rows = shard.shape[0]
    z = lambda n: jnp.zeros((rows, n), shard.dtype)
    a = lax.dynamic_update_slice(z(WIN_W), shard, (0, _win_offset(me)))
    b = jnp.concatenate([z(24), shard[:, :_A6], shard[:, _A6 + 32:], z(4), shard[:, _A6:_A6 + 32], z(96)], axis=1)
    return jnp.where(me == 6, b, a)


def _w_in_from_window(window, me):
    a = lax.dynamic_slice(window, (0, _win_offset(me)), (window.shape[0], N_IN))
    b = jnp.concatenate([window[:, 24:24 + _A6], window[:, 1280:1312], window[:, 24 + _A6:24 + _A6 + _C6]], axis=1)
    return jnp.where(me == 6, b, a)


def _w_all_from_windows(g):
    def merge_first(p, t):
        return jnp.concatenate([p[:, :128] + t, p[:, 128:]], axis=1)

    parts = [g[0][:, :1280]]
    for j in range(1, 6):
        parts.append(merge_first(g[j][:, :1280], g[j - 1][:, 1280:]))
    p6 = merge_first(g[6][:, :1280], g[5][:, 1280:])
    parts.append(jnp.concatenate([p6[:, :1152], p6[:, 1152:] + g[7][:, :128]], axis=1))
    parts.append(g[7][:, 128:])
    parts.append(g[6][:, 1280:])
    return jnp.concatenate(parts, axis=1)


def _windows_of_w_all(gw):
    wins = [gw[:, 1280 * j:1280 * j + WIN_W] for j in range(6)]
    wins.append(jnp.concatenate([gw[:, 7680:8960], gw[:, PROJ_W:]], axis=1))
    wins.append(gw[:, 8832:PROJ_W])
    return jnp.stack(wins)


def kernel(x, c, w_mod, b_mod, w_in, gm_norm_w, gm_ws, gm_bs, conv_w, conv_b, dt_bias, a_log, d_skip, ssm_norm_w, w_branch_gm, w_branch_ssm, w_out, w_ff1, w_ff2, final_norm_w, loss_target, m_w_mod, m_b_mod, m_w_in, m_gm_norm_w, m_gm_ws, m_gm_bs, m_conv_w, m_conv_b, m_dt_bias, m_a_log, m_d_skip, m_ssm_norm_w, m_w_branch_gm, m_w_branch_ssm, m_w_out, m_w_ff1, m_w_ff2, m_final_norm_w, v_w_mod, v_b_mod, v_w_in, v_gm_norm_w, v_gm_ws, v_gm_bs, v_conv_w, v_conv_b, v_dt_bias, v_a_log, v_d_skip, v_ssm_norm_w, v_w_branch_gm, v_w_branch_ssm, v_w_out, v_w_ff1, v_w_ff2, v_final_norm_w):
    T = x.shape[1]
    me = 4 * lax.axis_index("x") + 2 * lax.axis_index("y") + lax.axis_index("c")
    x2 = x[0]
    tgt = loss_target[0]
    n_in = IN_WIDTH // N_DEV
    n_mod = N_MOD * D // N_DEV
    n_cv = CONV_DIM // N_DEV

    c_all, conv_w_full = exchange(
        [(c.reshape(8, 128), _whole, (N_DEV, 8, 128), _slot),
         (conv_w[0], _whole, (N_DEV, CONV_K, n_cv), _slot)], name="gather_c_convw")
    c_all = c_all.reshape(N_DEV, D)
    conv_w_full = conv_w_full.transpose(1, 0, 2).reshape(CONV_K, CONV_DIM)

    win = _w_in_window(w_in[0].astype(BF16), me)
    gwin = gather_blocks(win, name="gather_w_in")
    late_weights = [
        (w_branch_gm[0].astype(BF16), _whole, (D, D), _rows(D // N_DEV)),
        (w_branch_ssm[0].astype(BF16), _whole, (SSM_INNER, D), _rows(SSM_INNER // N_DEV)),
        (w_out[0].astype(BF16), _whole, (D, D), _rows(D // N_DEV)),
        (w_ff1[0].astype(BF16), _whole, (D, D_FF), _cols(D_FF // N_DEV)),
        (w_ff2[0].astype(BF16), _whole, (D_FF, D), _rows(D_FF // N_DEV))]
    w_all = _w_all_from_windows(gwin)
    w_dt = w_all[:, PROJ_W:]

    c_pad = _pad_rows(c_all, 128)
    b_mine = lax.dynamic_slice(b_mod, (0, me * n_mod), (1, n_mod))

    def mod_fn(cp, w, b):
        ca = cp * jax.nn.sigmoid(cp)
        return (jnp.dot(ca, w, precision=HIGHEST, preferred_element_type=F32) + b,)

    (mod_part,) = whole_call(mod_fn, [c_pad, w_mod[0], b_mine], [((128, n_mod), F32)], name="mod_fwd")
    gmod = gather_blocks(mod_part[:N_DEV], name="gather_mod")
    mod = lax.dynamic_index_in_dim(gmod, me, axis=1, keepdims=False).reshape(N_MOD, D)
    sh1, sc1, gt1, sh2, sc2, gt2 = [mod[i:i + 1] for i in range(N_MOD)]

    (h,) = rowwise_call(fwd_body(fn_modulate), [x2], [sc1, sh1], [(D, BF16)], [], tm=256, name="modulate1")
    proj = matmul(h, w_all, "nn", BF16, name="mm_proj", n=PROJ_W)
    dtg = matmul(h, w_dt, "nn", F32, name="mm_dt")
    ws = gm_ws[0]
    bs3 = gm_bs[0].reshape(GM_GROUPS, Q, 1)
    sgu_rows = [(proj, D, P_U), (proj, D, P_V)]
    (ya,) = rowwise_call(fwd_body(fn_sgu), sgu_rows, [gm_norm_w, ws, bs3], [(D, BF16)], [],
                         tm=256, name="sgu_fwd")
    xact = conv_fwd(proj, conv_w_full, conv_b)
    dtb4 = jnp.pad(dt_bias, ((0, 0), (0, 96)))
    alog4 = jnp.pad(a_log, ((0, 0), (0, 96)))
    dsk4 = d_skip.reshape(SSM_GROUPS, SSM_HPG, 1, 1)
    yb, states, w_gm_f, w_ssm_f, w_out_f, w_ff1_f, w_ff2_f = ssd_fwd(
        xact, proj, dtg, dtb4, alog4, dsk4, ssm_norm_w, late_weights)
    pa = matmul(ya, w_gm_f, "nn", F32, name="mm_branch_gm")
    pb = matmul(yb, w_ssm_f, "nn", F32, name="mm_branch_ssm")
    gate_rows = [(proj, D, P_GA), (proj, D, P_GB)]
    (mixed,) = rowwise_call(fwd_body(fn_mix), gate_rows + [pa, pb], [], [(D, BF16)], [], tm=256, name="mix_fwd")
    o = matmul(mixed, w_out_f, "nn", F32, name="mm_out")
    x1, h2 = rowwise_call(fwd_body(fn_res_modulate), [x2, o], [gt1, sc2, sh2], [(D, F32), (D, BF16)], [],
                          tm=256, name="res_modulate2")
    f = matmul(h2, w_ff1_f, "nn", BF16, name="mm_ff1")
    (act,) = rowwise_call(fwd_body(fn_relu2), [f], [], [(D_FF, BF16)], [], tm=256, name="relu2_fwd")
    gf = matmul(act, w_ff2_f, "nn", F32, name="mm_ff2")

    dx1, dgf, loss_v, dgt2, dfnw = rowwise_call(
        final_body, [x1, gf, tgt], [gt2, final_norm_w.reshape(1, D)], [(D, F32), (D, BF16)],
        [(1, 128), (1, D), (1, D)], tm=256, name="final_loss_bwd")
    dact = matmul(dgf, w_ff2_f, "nt", BF16, name="mm_ff2_dgrad")
    gw_ff2 = matmul(act, dgf, "tn", BF16, name="mm_ff2_wgrad", tk=512)
    (df,) = rowwise_call(bwd_body(fn_relu2, 1), [f, dact], [], [(D_FF, BF16)], [], tm=256, name="relu2_bwd")
    dh2 = matmul(df, w_ff1_f, "nt", F32, name="mm_ff1_dgrad")
    gw_ff1 = matmul(h2, df, "tn", BF16, name="mm_ff1_wgrad", tk=512)

    def res_mod_bwd(r, fl):
        xv, ov, dx1v, dh2v = r
        _, vjp = jax.vjp(fn_res_modulate, xv, ov, *fl)
        dxv, dov, dg1, dsc, dsh = vjp((dx1v, dh2v))
        return (dxv, dov), (dg1, dsc, dsh)

    dxa, do, dgt1, dsc2, dsh2 = rowwise_call(
        res_mod_bwd, [x2, o, dx1, dh2], [gt1, sc2, sh2], [(D, F32), (D, BF16)],
        [(1, D), (1, D), (1, D)], tm=256, name="res_modulate2_bwd")
    dmixed = matmul(do, w_out_f, "nt", F32, name="mm_out_dgrad")
    gw_out = matmul(mixed, do, "tn", BF16, name="mm_out_wgrad", tk=512)
    dproj = lax.empty((T, ALL_W), BF16)

    def mix_bwd(r, fl):
        dga, dgb, dpa, dpb = bwd_body(fn_mix, 4)(r, fl)[0]
        return (jnp.concatenate([dga, dgb], axis=1), dpa, dpb), ()

    dproj, dpa, dpb = rowwise_call(
        mix_bwd, gate_rows + [pa, pb, dmixed], [], [(dproj, 2 * D, P_GA), (D, BF16), (D, BF16)], [],
        tm=256, name="mix_bwd")
    dya = matmul(dpa, w_gm_f, "nt", F32, name="mm_branch_gm_dgrad")
    gw_gm = matmul(ya, dpa, "tn", BF16, name="mm_branch_gm_wgrad", tk=512)
    dyb = matmul(dpb, w_ssm_f, "nt", BF16, name="mm_branch_ssm_dgrad")
    gw_ssm = matmul(yb, dpb, "tn", BF16, name="mm_branch_ssm_wgrad", tk=512)

    def sgu_bwd(r, fl):
        (du, dv), acc = bwd_body(fn_sgu, 2)(r, fl)
        return (jnp.concatenate([du, dv], axis=1),), acc

    dproj, dgnw, dws, dbs = rowwise_call(
        sgu_bwd, sgu_rows + [dya], [gm_norm_w, ws, bs3], [(dproj, 2 * D, P_U)],
        [(1, D), (GM_GROUPS, Q, Q), (GM_GROUPS, Q, 1)], tm=256, name="sgu_bwd")
    early_grads = [
        (gw_gm, _rows(D // N_DEV), (N_DEV, D // N_DEV, D), _slot),
        (gw_ssm, _rows(SSM_INNER // N_DEV), (N_DEV, SSM_INNER // N_DEV, D), _slot),
        (gw_out, _rows(D // N_DEV), (N_DEV, D // N_DEV, D), _slot),
        (gw_ff1, _cols(D_FF // N_DEV), (N_DEV, D, D_FF // N_DEV), _slot),
        (gw_ff2, _rows(D_FF // N_DEV), (N_DEV, D_FF // N_DEV, D), _slot)]
    dxs, dbm, dcm, dproj, ddt8, ddtb, dalog, ddsk, dsnw, r_gm, r_ssm, r_out, r_ff1, r_ff2 = ssd_bwd(
        xact, proj, dtg, dtb4, alog4, dsk4, ssm_norm_w, states, dyb, dproj, early_grads)
    dconv_w, dconv_b = [], []
    for nm, dact_part, col0 in (("xs", dxs, 0), ("b", dbm, SSM_INNER), ("c", dcm, SSM_INNER + SSM_GROUPS * 128)):
        dproj, dcw, dcb = conv_bwd(proj, dact_part, col0, conv_w_full, conv_b, dproj, name="conv_bwd_" + nm)
        dconv_w.append(dcw)
        dconv_b.append(dcb)
    dconv_w = jnp.concatenate(dconv_w, axis=1)
    dconv_b = jnp.concatenate(dconv_b, axis=1)
    dproj = dproj.at[:, PROJ_W:].set(jnp.sum(ddt8, axis=0).astype(BF16))
    gw_all = matmul(h, dproj, "tn", BF16, name="mm_in_wgrad", tn=1152, tk=512)
    dh, r_in = matmul(dproj, w_all, "nt", F32, name="mm_in_dgrad", tk=1152,
                      comm=[(_windows_of_w_all(gw_all), _slot, (N_DEV, D, WIN_W), _slot)])
    grad_x, dsc1, dsh1 = rowwise_call(grad_x_body, [x2, dh, dxa], [sc1, sh1], [(D, F32)],
                                      [(1, D), (1, D)], tm=256, name="modulate1_bwd")
    dmod = jnp.concatenate([dsh1, dsc1, dgt1, dsh2, dsc2, dgt2], axis=0)

    g_w_in = _w_in_from_window(sum_devices(r_in, tr=256, name="sum_w_in_grads"), me).reshape(1, D, n_in)

    def rows128(a, rows):
        a = a.reshape(-1)
        return jnp.pad(a, (0, rows * 128 - a.shape[0])).reshape(rows, 128)

    ddtb = jnp.sum(ddtb, axis=0)
    dalog = jnp.sum(dalog, axis=0)
    small = [(dgnw, 8), (dws, 1024), (dbs, 8), (dconv_w, 128), (dconv_b, 32), (ddtb, 8), (dalog, 8),
             (ddsk, 8), (dsnw, 16), (dfnw, 8), (dmod, 48)]
    spack = jnp.concatenate([rows128(a, r) for a, r in small], axis=0)
    sall = gather_blocks(spack, name="gather_small_grads")
    ssum = sum_devices(sall, tr=SMALL_ROWS, name="sum_small_grads")
    soff = [0]
    for _, r in small:
        soff.append(soff[-1] + r)

    def spart(i, n):
        return ssum[soff[i]:soff[i + 1]].reshape(-1)[:n]

    g_gm_norm_w = spart(0, D).reshape(1, D)
    g_gm_ws = spart(1, GM_GROUPS * Q * Q).reshape(GM_GROUPS * Q, Q)
    g_gm_bs = spart(2, GM_GROUPS * Q).reshape(GM_GROUPS, Q)
    g_conv_w_full = spart(3, CONV_K * CONV_DIM).reshape(CONV_K, CONV_DIM)
    g_conv_w = lax.dynamic_slice(g_conv_w_full, (0, me * n_cv), (CONV_K, n_cv))
    g_conv_b = spart(4, CONV_DIM).reshape(1, CONV_DIM)
    g_dt_bias = spart(5, 32).reshape(1, 32)
    g_a_log = spart(6, 32).reshape(1, 32)
    g_d_skip = spart(7, 32).reshape(1, 32)
    g_ssm_norm_w = spart(8, SSM_INNER).reshape(1, SSM_INNER)
    g_final_norm_w = spart(9, D).reshape(1, D)
    g_b_mod = spart(10, N_MOD * D).reshape(1, N_MOD * D)

    dmod_all = sall[:, soff[10]:soff[11]].reshape(N_DEV, N_MOD * D)
    dmod_mine = _pad_rows(lax.dynamic_slice(dmod_all, (0, me * n_mod), (N_DEV, n_mod)), 128)

    def wmod_grad_fn(cp, dm):
        ca = cp * jax.nn.sigmoid(cp)
        return (lax.dot_general(ca, dm, (((0,), (0,)), ((), ())), precision=HIGHEST,
                                preferred_element_type=F32),)

    (g_w_mod,) = whole_call(wmod_grad_fn, [c_pad, dmod_mine], [((D, n_mod), F32)], name="w_mod_grad")

    upd = {}

    def step(name, w, g, m, v, parts=False):
        upd[name] = adamw(w, g if parts else g.reshape(w.shape), m, v, name="adamw_" + name)

    step("w_mod", w_mod, g_w_mod, m_w_mod, v_w_mod)
    step("b_mod", b_mod, g_b_mod, m_b_mod, v_b_mod)
    step("w_in", w_in, g_w_in, m_w_in, v_w_in)
    step("gm_norm_w", gm_norm_w, g_gm_norm_w, m_gm_norm_w, v_gm_norm_w)
    step("gm_ws", gm_ws, g_gm_ws, m_gm_ws, v_gm_ws)
    step("gm_bs", gm_bs, g_gm_bs, m_gm_bs, v_gm_bs)
    step("conv_w", conv_w, g_conv_w, m_conv_w, v_conv_w)
    step("conv_b", conv_b, g_conv_b, m_conv_b, v_conv_b)
    step("dt_bias", dt_bias, g_dt_bias, m_dt_bias, v_dt_bias)
    step("a_log", a_log, g_a_log, m_a_log, v_a_log)
    step("d_skip", d_skip, g_d_skip, m_d_skip, v_d_skip)
    step("ssm_norm_w", ssm_norm_w, g_ssm_norm_w, m_ssm_norm_w, v_ssm_norm_w)
    step("w_branch_gm", w_branch_gm, r_gm, m_w_branch_gm, v_w_branch_gm, parts=True)
    step("w_branch_ssm", w_branch_ssm, r_ssm, m_w_branch_ssm, v_w_branch_ssm, parts=True)
    step("w_out", w_out, r_out, m_w_out, v_w_out, parts=True)
    step("w_ff1", w_ff1, r_ff1, m_w_ff1, v_w_ff1, parts=True)
    step("w_ff2", w_ff2, r_ff2, m_w_ff2, v_w_ff2, parts=True)
    step("final_norm_w", final_norm_w.reshape(1, D), g_final_norm_w, m_final_norm_w.reshape(1, D),
         v_final_norm_w.reshape(1, D))
    upd["final_norm_w"] = tuple(a.reshape(D) for a in upd["final_norm_w"])

    loss = lax.psum(loss_v[0, 0], ("x", "y", "c"))
    order = ["w_mod", "b_mod", "w_in", "gm_norm_w", "gm_ws", "gm_bs", "conv_w", "conv_b", "dt_bias", "a_log",
             "d_skip", "ssm_norm_w", "w_branch_gm", "w_branch_ssm", "w_out", "w_ff1", "w_ff2", "final_norm_w"]
    return (loss, grad_x.reshape(1, T, D),
            *[upd[n][0] for n in order], *[upd[n][1] for n in order],
            *[upd[n][2] for n in order], *[upd[n][3] for n in order])
```

```python
import functools

import jax
import jax.numpy as jnp
from jax import lax
from jax.experimental import pallas as pl
from jax.experimental.pallas import tpu as pltpu

F32 = jnp.float32
BF16 = jnp.bfloat16
MESH = pl.DeviceIdType.MESH
HIGHEST = lax.Precision.HIGHEST

N_DEV = 8
D = 1024
Q = 128
GM_GROUPS = 8
SSM_INNER = 2048
SSM_GROUPS = 8
SSM_HPG = 4
SSM_P = 64
SSM_GW = SSM_HPG * SSM_P
CONV_DIM = 4096
CONV_K = 4
D_FF = 4096
N_MOD = 6
EPS = 1e-6
IN_WIDTH = 10272
OFF_DT = 8192
OFF_GA = 8224
PROJ_W = 10240
ALL_W = 10368
P_U, P_V, P_Z, P_XBC, P_GA, P_GB = 0, 1024, 2048, 4096, 8192, 9216

ADAM_LR = 0.001
ADAM_B1 = 0.9
ADAM_B2 = 0.999
ADAM_EPS = 1e-08
ADAM_WD = 0.01
ADAM_STEP = 10

VMEM_LIMIT_BYTES = 48 * 1024 * 1024
EARLY_ROWS = (8, 1024, 8)
MID_ROWS = (128, 32, 8, 8, 8, 16, 8, 32)
LATE_ROWS = (8, 8)


def _pack_rows(arrs, rows):
    def rows128(a, r):
        a = a.reshape(-1)
        return jnp.pad(a, (0, r * 128 - a.shape[0])).reshape(r, 128)
    return jnp.concatenate([rows128(a, r) for a, r in zip(arrs, rows)], axis=0)


def _unpack_rows(s, rows):
    out, o = [], 0
    for r in rows:
        out.append(s[o:o + r].reshape(-1))
        o += r
    return out


def _params(sem=None):
    return pltpu.CompilerParams(dimension_semantics=sem, vmem_limit_bytes=VMEM_LIMIT_BYTES)


def _dg(a, b, ca, cb):
    return lax.dot_general(a.astype(BF16), b.astype(BF16), (((ca,), (cb,)), ((), ())),
                           preferred_element_type=F32)


@jax.custom_vjp
def dot_nn(a, b):
    return _dg(a, b, 1, 0)


@jax.custom_vjp
def dot_nt(a, b):
    return _dg(a, b, 1, 1)


@jax.custom_vjp
def dot_tn(a, b):
    return _dg(a, b, 0, 0)


def _like(ct, primal):
    return ct.astype(primal.dtype)


dot_nn.defvjp(lambda a, b: (dot_nn(a, b), (a, b)),
              lambda r, g: (_like(dot_nt(g, r[1]), r[0]), _like(dot_tn(r[0], g), r[1])))
dot_nt.defvjp(lambda a, b: (dot_nt(a, b), (a, b)),
              lambda r, g: (_like(dot_nn(g, r[1]), r[0]), _like(dot_tn(g, r[0]), r[1])))
dot_tn.defvjp(lambda a, b: (dot_tn(a, b), (a, b)),
              lambda r, g: (_like(dot_nt(r[1], g), r[0]), _like(dot_nn(r[0], g), r[1])))


def _rms(x):
    return x * lax.rsqrt(jnp.mean(x * x, axis=-1, keepdims=True) + EPS)


def _softplus(x):
    return jnp.maximum(x, 0.0) + jnp.log1p(jnp.exp(-jnp.abs(x)))


def _rows(n):
    return lambda ref, j: ref.at[pl.ds(pl.multiple_of(j * n, n), n)]


def _cols(n):
    return lambda ref, j: ref.at[:, pl.ds(pl.multiple_of(j * n, n), n)]


def _slot(ref, j):
    return ref.at[j]


def _whole(ref, j):
    return ref


def exchange(items, *, name):
    n = len(items)

    def body(*refs):
        exchange_in_body(items, refs[:n], refs[n:2 * n], refs[2 * n:], True, True)

    return pl.pallas_call(
        body, name=name,
        out_shape=exchange_out_shapes(items),
        in_specs=[pl.BlockSpec(memory_space=pl.ANY)] * n,
        out_specs=[pl.BlockSpec(memory_space=pl.ANY)] * n,
        scratch_shapes=exchange_semaphores(items),
    )(*[it[0] for it in items])


def exchange_out_shapes(items):
    return [jax.ShapeDtypeStruct(tuple(shape), src.dtype) for (src, _, shape, _) in items]


def exchange_semaphores(items):
    n = len(items)
    return [pltpu.SemaphoreType.DMA((n, N_DEV - 1)), pltpu.SemaphoreType.DMA((n, N_DEV - 1)),
            pltpu.SemaphoreType.DMA((n,))]


def _exchange_copies(items, src_refs, out_refs, sems):
    send_sems, recv_sems, local_sems = sems
    x = lax.axis_index("x")
    y = lax.axis_index("y")
    c = lax.axis_index("c")
    me = 4 * x + 2 * y + c
    local = [pltpu.make_async_copy(src_win(src_refs[i], me), dst_win(out_refs[i], me), local_sems.at[i])
             for i, (_, src_win, _, dst_win) in enumerate(items)]
    remote = []
    for i, (_, src_win, _, dst_win) in enumerate(items):
        for k in range(1, N_DEV):
            px = lax.rem(x + ((k >> 2) & 1), 2)
            py = lax.rem(y + ((k >> 1) & 1), 2)
            pc = lax.rem(c + (k & 1), 2)
            peer = 4 * px + 2 * py + pc
            remote.append(pltpu.make_async_remote_copy(
                src_ref=src_win(src_refs[i], peer), dst_ref=dst_win(out_refs[i], me),
                send_sem=send_sems.at[i, k - 1], recv_sem=recv_sems.at[i, k - 1],
                device_id=(px, py, pc), device_id_type=MESH))
    return local, remote


def _when(cond, fn):
    if cond is True:
        fn()
    else:
        pl.when(cond)(fn)


def exchange_start(items, src_refs, out_refs, sems, cond):
    def start():
        local, remote = _exchange_copies(items, src_refs, out_refs, sems)
        for cp in local + remote:
            cp.start()
    _when(cond, start)


def exchange_finish(items, src_refs, out_refs, sems, cond):
    def finish():
        local, remote = _exchange_copies(items, src_refs, out_refs, sems)
        for cp in remote:
            cp.wait_send()
        for cp in remote:
            cp.wait_recv()
        for cp in local:
            cp.wait()
    _when(cond, finish)


def exchange_in_body(items, src_refs, out_refs, sems, first, last):
    exchange_start(items, src_refs, out_refs, sems, first)
    exchange_finish(items, src_refs, out_refs, sems, last)


def gather_blocks(src, *, name):
    return exchange([(src, _whole, (N_DEV,) + src.shape, _slot)], name=name)[0]


def gather_blocks_two_level(src, *, name):
    def body(src_ref, out_ref, send_sems, recv_sems, local_sem):
        x = lax.axis_index("x")
        y = lax.axis_index("y")
        c = lax.axis_index("c")
        me, sibling = (x, y, c), (x, y, 1 - c)
        chips = [(1 - x, y), (x, 1 - y), (1 - x, 1 - y)]

        def slot(px, py, pc):
            return out_ref.at[4 * px + 2 * py + pc]

        def copy(k, block, to, src=None):
            return pltpu.make_async_remote_copy(
                src_ref=slot(*block) if src is None else src, dst_ref=slot(*block),
                send_sem=send_sems.at[k], recv_sem=recv_sems.at[k], device_id=to, device_id_type=MESH)

        mine = pltpu.make_async_copy(src_ref, slot(*me), local_sem)
        mine.start()
        first = [copy(0, me, sibling, src=src_ref)]
        first += [copy(1 + j, me, (*chip, c), src=src_ref) for j, chip in enumerate(chips)]
        for cp in first:
            cp.start()
        passed = [copy(4 + j, (*chip, c), sibling) for j, chip in enumerate(chips)]
        for j, chip in enumerate(chips):
            copy(1 + j, (*chip, c), me).wait_recv()
            passed[j].start()
        copy(0, sibling, me).wait_recv()
        for j, chip in enumerate(chips):
            copy(4 + j, (*chip, 1 - c), me).wait_recv()
        for cp in first + passed:
            cp.wait_send()
        mine.wait()

    return pl.pallas_call(
        body, name=name,
        out_shape=jax.ShapeDtypeStruct((N_DEV,) + src.shape, src.dtype),
        in_specs=[pl.BlockSpec(memory_space=pl.ANY)],
        out_specs=pl.BlockSpec(memory_space=pl.ANY),
        scratch_shapes=[pltpu.SemaphoreType.DMA((N_DEV - 1,)), pltpu.SemaphoreType.DMA((N_DEV - 1,)),
                        pltpu.SemaphoreType.DMA(())],
    )(src)


def sum_devices(g, *, tr, name):
    _, R, C = g.shape

    def body(g_ref, o_ref):
        acc = g_ref[0].astype(F32)
        for j in range(1, N_DEV):
            acc = acc + g_ref[j].astype(F32)
        o_ref[...] = acc

    return pl.pallas_call(
        body, name=name, grid=(R // tr,),
        out_shape=jax.ShapeDtypeStruct((R, C), F32),
        in_specs=[pl.BlockSpec((N_DEV, tr, C), lambda i: (0, i, 0))],
        out_specs=pl.BlockSpec((tr, C), lambda i: (i, 0)),
        compiler_params=_params(("parallel",)),
    )(g)


def matmul(a, b, mode, out_dtype, *, name, tm=1024, tn=1024, tk=1024, n=None, comm=None,
           a_pro=None, epi=None, epi_ins=()):
    if mode == "nn":
        (M, K), (K2, N) = a.shape, b.shape
    elif mode == "nt":
        (M, K), (N, K2) = a.shape, b.shape
    else:
        (K, M), (K2, N) = a.shape, b.shape
    assert K == K2
    N = N if n is None else n
    tm, tn, tk = min(tm, M), min(tn, N), min(tk, K)
    assert M % tm == 0 and N % tn == 0 and K % tk == 0, (name, M, N, K, tm, tn, tk)
    nk = K // tk
    if mode == "tn":
        a_spec = pl.BlockSpec((tk, tm), lambda i, j, k: (k, i))
    else:
        a_spec = pl.BlockSpec((tm, tk), lambda i, j, k: (i, k))
    if mode == "nt":
        b_spec = pl.BlockSpec((tn, tk), lambda i, j, k: (j, k))
    else:
        b_spec = pl.BlockSpec((tk, tn), lambda i, j, k: (k, j))
    dims = {"nn": (1, 0), "nt": (1, 1), "tn": (0, 0)}[mode]
    items = list(comm) if comm else []
    nx = len(items)
    ne = len(epi_ins)
    gm, gn = M // tm, N // tn
    any_spec = pl.BlockSpec(memory_space=pl.ANY)
    o_spec = pl.BlockSpec((tm, tn), lambda i, j, k: (i, j))

    def body(*refs):
        a_ref, b_ref, e_refs = refs[0], refs[1], refs[2:2 + ne]
        refs = refs[2 + ne:]
        src_refs, o_ref, out_refs = refs[:nx], refs[nx], refs[1 + nx:1 + 2 * nx]
        acc_ref, sems = refs[1 + 2 * nx], refs[2 + 2 * nx:]
        i, j, k = pl.program_id(0), pl.program_id(1), pl.program_id(2)
        if items:
            exchange_start(items, src_refs, out_refs, sems, (i == 0) & (j == 0) & (k == 0))
        a_tile = a_ref[...] if a_pro is None else a_pro(a_ref[...])
        part = lax.dot_general(a_tile, b_ref[...], (((dims[0],), (dims[1],)), ((), ())),
                               preferred_element_type=F32)

        @pl.when(k == 0)
        def _():
            acc_ref[...] = part

        @pl.when(k > 0)
        def _():
            acc_ref[...] += part

        @pl.when(k == nk - 1)
        def _():
            acc = acc_ref[...]
            if epi is not None:
                acc = epi(acc, *[e[...] for e in e_refs])
            o_ref[...] = acc.astype(o_ref.dtype)

        if items:
            exchange_finish(items, src_refs, out_refs, sems, (i == gm - 1) & (j == gn - 1) & (k == nk - 1))

    res = pl.pallas_call(
        body, name=name, grid=(gm, gn, nk),
        out_shape=[jax.ShapeDtypeStruct((M, N), out_dtype)] + exchange_out_shapes(items),
        in_specs=[a_spec, b_spec] + [o_spec] * ne + [any_spec] * nx,
        out_specs=[o_spec] + [any_spec] * nx,
        scratch_shapes=[pltpu.VMEM((tm, tn), F32)] + (exchange_semaphores(items) if items else []),
        compiler_params=_params(("arbitrary",) * 3 if items else ("parallel", "parallel", "arbitrary")),
    )(a, b, *epi_ins, *[it[0] for it in items])
    return res if items else res[0]


def rowwise_call(body_fn, rows, fulls, row_outs, acc_outs, *, tm, name):
    rows = [r if isinstance(r, tuple) else (r, r.shape[1], 0) for r in rows]
    T = rows[0][0].shape[0]
    tm = min(tm, T)
    assert T % tm == 0
    n_r, n_f, n_ro = len(rows), len(fulls), len(row_outs)
    into = [(k, ro) for k, ro in enumerate(row_outs) if len(ro) == 3]
    n_b = len(into)

    def body(*refs):
        r_refs = refs[:n_r]
        f_refs = refs[n_r:n_r + n_f]
        refs = refs[n_r + n_f + n_b:]
        ro_refs = refs[:n_ro]
        ao_refs = refs[n_ro:]
        r_vals = [r[...].astype(F32) for r in r_refs]
        f_vals = [f[...].astype(F32) for f in f_refs]
        ro, ao = body_fn(r_vals, f_vals)
        for ref, v in zip(ro_refs, ro):
            ref[...] = v.astype(ref.dtype)
        if ao_refs:
            @pl.when(pl.program_id(0) == 0)
            def _():
                for ref in ao_refs:
                    ref[...] = jnp.zeros(ref.shape, F32)
            for ref, v in zip(ao_refs, ao):
                ref[...] += v.reshape(ref.shape)

    def full_spec(shape):
        nd = len(shape)
        return pl.BlockSpec(tuple(shape), lambda i: (0,) * nd)

    in_specs = [pl.BlockSpec((tm, w), functools.partial(lambda i, o: (i, o), o=off // w))
                for (_, w, off) in rows]
    in_specs += [full_spec(f.shape) for f in fulls]
    in_specs += [pl.BlockSpec(memory_space=pl.ANY)] * n_b
    out_specs, out_shape = [], []
    for ro in row_outs:
        if len(ro) == 3:
            buf, w, off = ro
            out_specs.append(pl.BlockSpec((tm, w), functools.partial(lambda i, o: (i, o), o=off // w)))
            out_shape.append(jax.ShapeDtypeStruct(buf.shape, buf.dtype))
        else:
            w, dt = ro
            out_specs.append(pl.BlockSpec((tm, w), lambda i: (i, 0)))
            out_shape.append(jax.ShapeDtypeStruct((T, w), dt))
    out_specs += [full_spec(s) for s in acc_outs]
    out_shape += [jax.ShapeDtypeStruct(tuple(s), F32) for s in acc_outs]
    aliases = {n_r + n_f + b: k for b, (k, _) in enumerate(into)}
    return pl.pallas_call(
        body, name=name, grid=(T // tm,),
        out_shape=out_shape, in_specs=in_specs, out_specs=out_specs,
        input_output_aliases=aliases,
        compiler_params=_params(("arbitrary",)),
    )(*[r[0] for r in rows], *fulls, *[ro[0] for _, ro in into])


def fwd_body(fn):
    return lambda r, f: (fn(*r, *f), ())


def bwd_body(fn, n_rows):
    def body(r, f):
        ins, cots = r[:n_rows], r[n_rows:]
        _, vjp = jax.vjp(fn, *ins, *f)
        g = vjp(tuple(cots))
        return g[:n_rows], g[n_rows:]
    return body


def whole_call(fn, ins, outs, *, name):
    n_in = len(ins)

    def body(*refs):
        res = fn(*[r[...] for r in refs[:n_in]])
        for ref, v in zip(refs[n_in:], res):
            ref[...] = v.astype(ref.dtype)

    return pl.pallas_call(
        body, name=name,
        out_shape=[jax.ShapeDtypeStruct(tuple(s), dt) for (s, dt) in outs],
        compiler_params=_params(),
    )(*ins)


def fn_modulate(x, sc, sh):
    return (_rms(x) * (1.0 + sc) + sh,)


def fn_sgu(u, v, nw, ws, bs):
    ug = jax.nn.gelu(u)
    vn = _rms(jax.nn.gelu(v)) * nw
    ri = lax.broadcasted_iota(jnp.int32, (Q, Q), 0)
    ci = lax.broadcasted_iota(jnp.int32, (Q, Q), 1)
    causal = ri >= ci
    chunks = []
    for n in range(u.shape[0] // Q):
        vc = vn[n * Q:(n + 1) * Q]
        cols = [dot_nn(jnp.where(causal, ws[g], 0.0), vc[:, g * Q:(g + 1) * Q]) + bs[g]
                for g in range(GM_GROUPS)]
        chunks.append(jnp.concatenate(cols, axis=1))
    sv = chunks[0] if len(chunks) == 1 else jnp.concatenate(chunks, axis=0)
    return (ug * sv,)


def fn_mix(ga, gb, pa, pb):
    return (jax.nn.sigmoid(ga) * pa + jax.nn.sigmoid(gb) * pb,)


def fn_res_modulate(x, o, g1, sc2, sh2):
    x1 = x + g1 * o
    return x1, _rms(x1) * (1.0 + sc2) + sh2


def relu2_tile(f):
    return jnp.square(jnp.maximum(f.astype(F32), 0.0)).astype(BF16)


def relu2_grad_tile(dact, f):
    return dact * (2.0 * jnp.maximum(f.astype(F32), 0.0))


def final_body(r, f):
    x1, gf, tgt = r
    g2, fnw = f

    def loss_fn(x1, gf, g2, fnw):
        y = _rms(x1 + g2 * gf) * fnw
        row = 0.5 * jnp.mean(jnp.square(y - tgt), axis=-1, keepdims=True)
        return jnp.sum(row, axis=0, keepdims=True)

    l, vjp = jax.vjp(loss_fn, x1, gf, g2, fnw)
    dx1, dgf, dg2, dfnw = vjp(jnp.ones((1, 1), F32))
    return (dx1, dgf), (jnp.broadcast_to(l, (1, 128)), dg2, dfnw)


def grad_x_body(r, f):
    x, dh, dxa = r
    _, vjp = jax.vjp(fn_modulate, x, *f)
    dx, dsc, dsh = vjp((dh,))
    return (dx + dxa,), (dsc, dsh)


CONV_CW = 128
CONV_PAD = 8


def _conv_pre(xp, w_ref, b_ref, r0, R):
    acc = b_ref[...] + w_ref[0:1, :] * xp[r0 + CONV_PAD - 3:r0 + CONV_PAD - 3 + R, :]
    for k in range(1, CONV_K):
        s = r0 + CONV_PAD - 3 + k
        acc = acc + w_ref[k:k + 1, :] * xp[s:s + R, :]
    return acc


def conv_fwd(proj, conv_w, conv_b):
    T = proj.shape[0]
    R = min(512, T)

    def body(x_ref, w_ref, b_ref, o_ref, xp):
        xp[0:CONV_PAD, :] = jnp.zeros((CONV_PAD, CONV_CW), F32)
        xp[CONV_PAD:CONV_PAD + T, :] = x_ref[...].astype(F32)
        for r0 in range(0, T, R):
            pre = _conv_pre(xp, w_ref, b_ref, r0, R)
            o_ref[r0:r0 + R, :] = (pre * jax.nn.sigmoid(pre)).astype(o_ref.dtype)

    return pl.pallas_call(
        body, name="conv_fwd", grid=(CONV_DIM // CONV_CW,),
        out_shape=jax.ShapeDtypeStruct((T, CONV_DIM), BF16),
        in_specs=[pl.BlockSpec((T, CONV_CW), lambda j: (0, P_XBC // CONV_CW + j)),
                  pl.BlockSpec((CONV_K, CONV_CW), lambda j: (0, j)),
                  pl.BlockSpec((1, CONV_CW), lambda j: (0, j))],
        out_specs=pl.BlockSpec((T, CONV_CW), lambda j: (0, j)),
        scratch_shapes=[pltpu.VMEM((T + CONV_PAD, CONV_CW), F32)],
        compiler_params=_params(("parallel",)),
    )(proj, conv_w, conv_b)


def conv_bwd(proj, dact, col0, conv_w, conv_b, dproj, *, name):
    T = proj.shape[0]
    R = min(512, T)
    nb = dact.shape[1] // CONV_CW
    c0 = col0 // CONV_CW
    x0 = (P_XBC + col0) // CONV_CW

    def body(x_ref, d_ref, w_ref, b_ref, _, dx_ref, dw_ref, db_ref, xp, dp):
        xp[0:CONV_PAD, :] = jnp.zeros((CONV_PAD, CONV_CW), F32)
        xp[CONV_PAD:CONV_PAD + T, :] = x_ref[...].astype(F32)
        dp[T:T + CONV_PAD, :] = jnp.zeros((CONV_PAD, CONV_CW), F32)
        dws = [jnp.zeros((1, CONV_CW), F32) for _ in range(CONV_K)]
        db = jnp.zeros((1, CONV_CW), F32)
        for r0 in range(0, T, R):
            pre = _conv_pre(xp, w_ref, b_ref, r0, R)
            s = jax.nn.sigmoid(pre)
            dpre = d_ref[r0:r0 + R, :].astype(F32) * (s * (1.0 + pre * (1.0 - s)))
            dp[r0:r0 + R, :] = dpre
            db = db + jnp.sum(dpre, axis=0, keepdims=True)
            for k in range(CONV_K):
                st = r0 + CONV_PAD - 3 + k
                dws[k] = dws[k] + jnp.sum(dpre * xp[st:st + R, :], axis=0, keepdims=True)
        for r0 in range(0, T, R):
            acc = w_ref[0:1, :] * dp[r0 + 3:r0 + 3 + R, :]
            for k in range(1, CONV_K):
                acc = acc + w_ref[k:k + 1, :] * dp[r0 + 3 - k:r0 + 3 - k + R, :]
            dx_ref[r0:r0 + R, :] = acc.astype(dx_ref.dtype)
        for k in range(CONV_K):
            dw_ref[k:k + 1, :] = dws[k]
        db_ref[...] = db

    return pl.pallas_call(
        body, name=name, grid=(nb,),
        out_shape=[jax.ShapeDtypeStruct(dproj.shape, dproj.dtype),
                   jax.ShapeDtypeStruct((CONV_K, nb * CONV_CW), F32),
                   jax.ShapeDtypeStruct((1, nb * CONV_CW), F32)],
        in_specs=[pl.BlockSpec((T, CONV_CW), lambda j: (0, x0 + j)),
                  pl.BlockSpec((T, CONV_CW), lambda j: (0, j)),
                  pl.BlockSpec((CONV_K, CONV_CW), lambda j: (0, c0 + j)),
                  pl.BlockSpec((1, CONV_CW), lambda j: (0, c0 + j)),
                  pl.BlockSpec(memory_space=pl.ANY)],
        out_specs=[pl.BlockSpec((T, CONV_CW), lambda j: (0, x0 + j)),
                   pl.BlockSpec((CONV_K, CONV_CW), lambda j: (0, j)),
                   pl.BlockSpec((1, CONV_CW), lambda j: (0, j))],
        scratch_shapes=[pltpu.VMEM((T + CONV_PAD, CONV_CW), F32),
                        pltpu.VMEM((T + CONV_PAD, CONV_CW), F32)],
        input_output_aliases={4: 0},
        compiler_params=_params(("parallel",)),
    )(proj, dact, conv_w, conv_b, dproj)


def _split3(a):
    hi = a.astype(BF16)
    r = a - hi.astype(F32)
    mid = r.astype(BF16)
    return hi, mid, (r - mid.astype(F32)).astype(BF16)


def _dg3(a, m, ca, cm, a_first):
    dims = (((ca,), (cm,)), ((), ())) if a_first else (((cm,), (ca,)), ((), ()))
    out = None
    for p in _split3(a):
        t = lax.dot_general(p, m, dims, preferred_element_type=F32) if a_first else \
            lax.dot_general(m, p, dims, preferred_element_type=F32)
        out = t if out is None else out + t
    return out


@jax.custom_vjp
def exact_right(a, m):
    return _dg3(a, m, 1, 0, True)


@jax.custom_vjp
def exact_left(m, a):
    return _dg3(a, m, 0, 1, False)


exact_right.defvjp(lambda a, m: (exact_right(a, m), m),
                   lambda m, g: (dot_nt(g, m), jnp.zeros_like(m)))
exact_left.defvjp(lambda m, a: (exact_left(m, a), m),
                  lambda m, g: (jnp.zeros_like(m), _dg3(g, m, 0, 0, False)))


def ssd_step(lane0, state, x, z, dtr, Bm, Cm, dtb, alog, dsk, nw):
    def iota(shape, dim):
        return lax.broadcasted_iota(jnp.int32, shape, dim)

    def one_hot(mask):
        return mask.astype(F32).astype(BF16)

    causal = iota((Q, Q), 0) >= iota((Q, Q), 1)
    eye = iota((Q, Q), 0) == iota((Q, Q), 1)
    lane = iota((1, 128), 1)
    colh = lax.shift_right_logical(iota((1, SSM_GW), 1), 6)
    to_cols = one_hot(iota((128, SSM_GW), 0) == lane0 + colh)

    dt_all = _softplus(dtr + dtb)
    a_all = dt_all * (-jnp.exp(alog))
    cum_all = exact_left(one_hot(causal), a_all)
    both = exact_right(jnp.concatenate([dt_all, cum_all], axis=0), to_cols)
    dt_f, cum_f = both[:Q], both[Q:]
    last_f = jnp.sum(jnp.where(iota((Q, 1), 0) == Q - 1, cum_f, 0.0), axis=0, keepdims=True)
    dsk_f = jnp.zeros((1, SSM_GW), F32)
    for h in range(SSM_HPG):
        dsk_f = jnp.where(colh == h, dsk[h], dsk_f)

    xdt = x * dt_f
    cb = dot_nt(Cm, Bm)
    ms, rhs = [], []
    for h in range(SSM_HPG):
        ch = jnp.sum(jnp.where(lane == lane0 + h, cum_all, 0.0), axis=1, keepdims=True)
        ch_t = jnp.sum(jnp.where(eye, ch, 0.0), axis=0, keepdims=True)
        ms.append(cb * jnp.exp(jnp.where(causal, ch - ch_t, -1e30)))
        rhs.append(jnp.where(colh == h, xdt, 0.0))
    y = dot_nn(jnp.concatenate(ms, axis=1), jnp.concatenate(rhs, axis=0))
    y = y + dot_nn(Cm, state) * jnp.exp(cum_f) + x * dsk_f
    new_state = state * jnp.exp(last_f) + dot_tn(Bm, xdt * jnp.exp(last_f - cum_f))
    gated = y * (z * jax.nn.sigmoid(z))
    return new_state, _rms(gated) * nw


SSD_GPS = 4
_XW = SSD_GPS * SSM_GW
_BW = SSD_GPS * 128


def _ssd_in_specs(rev, nc):
    def n_of(n):
        return nc - 1 - n if rev else n
    return [
        pl.BlockSpec((Q, _XW), lambda g, n: (n_of(n), g)),
        pl.BlockSpec((Q, _BW), lambda g, n: (n_of(n), SSM_INNER // _BW + g)),
        pl.BlockSpec((Q, _BW), lambda g, n: (n_of(n), (SSM_INNER + SSM_GROUPS * 128) // _BW + g)),
        pl.BlockSpec((Q, _XW), lambda g, n: (n_of(n), P_Z // _XW + g)),
        pl.BlockSpec((Q, 128), lambda g, n: (n_of(n), 0)),
        pl.BlockSpec((1, 128), lambda g, n: (0, 0)),
        pl.BlockSpec((1, 128), lambda g, n: (0, 0)),
        pl.BlockSpec((SSD_GPS, SSM_HPG, 1, 1), lambda g, n: (g, 0, 0, 0)),
        pl.BlockSpec((1, _XW), lambda g, n: (0, g)),
    ]


def _ssd_group_inputs(gi, x_ref, b_ref, c_ref, z_ref, dt_ref, dtb_ref, al_ref, dk_ref, nw_ref):
    xs = slice(gi * SSM_GW, (gi + 1) * SSM_GW)
    bs = slice(gi * 128, (gi + 1) * 128)
    return (x_ref[:, xs].astype(F32), z_ref[:, xs].astype(F32), dt_ref[...],
            b_ref[:, bs], c_ref[:, bs],
            dtb_ref[...], al_ref[...], dk_ref[gi], nw_ref[:, xs])


def ssd_fwd(xact, proj, dtg, dtb, alog, dsk, nw, comm):
    T = xact.shape[0]
    nc = T // Q
    nx = len(comm)
    ng = SSM_GROUPS // SSD_GPS
    any_spec = pl.BlockSpec(memory_space=pl.ANY)

    def body(*refs):
        in_refs, src_refs = refs[:9], refs[9:9 + nx]
        yb_ref, st_ref = refs[9 + nx:11 + nx]
        out_refs, state, sems = refs[11 + nx:11 + 2 * nx], refs[11 + 2 * nx], refs[12 + 2 * nx:]
        g, n = pl.program_id(0), pl.program_id(1)
        exchange_start(comm, src_refs, out_refs, sems, (g == 0) & (n == 0))

        @pl.when(n == 0)
        def _():
            state[...] = jnp.zeros(state.shape, F32)

        for gi in range(SSD_GPS):
            lane0 = SSM_HPG * (SSD_GPS * pl.program_id(0) + gi)
            s = state[gi]
            st_ref[gi, 0] = s
            new_s, yb = ssd_step(lane0, s, *_ssd_group_inputs(gi, *in_refs))
            state[gi] = new_s
            yb_ref[:, gi * SSM_GW:(gi + 1) * SSM_GW] = yb.astype(yb_ref.dtype)

        exchange_finish(comm, src_refs, out_refs, sems, (g == ng - 1) & (n == nc - 1))

    return pl.pallas_call(
        body, name="ssd_fwd", grid=(ng, nc),
        out_shape=[jax.ShapeDtypeStruct((T, SSM_INNER), BF16),
                   jax.ShapeDtypeStruct((SSM_GROUPS, nc, 128, SSM_GW), F32)] + exchange_out_shapes(comm),
        in_specs=_ssd_in_specs(False, nc) + [any_spec] * nx,
        out_specs=[pl.BlockSpec((Q, _XW), lambda g, n: (n, g)),
                   pl.BlockSpec((SSD_GPS, 1, 128, SSM_GW), lambda g, n: (g, n, 0, 0))] + [any_spec] * nx,
        scratch_shapes=[pltpu.VMEM((SSD_GPS, 128, SSM_GW), F32)] + exchange_semaphores(comm),
        compiler_params=_params(("arbitrary", "arbitrary")),
    )(xact, xact, xact, proj, dtg, dtb, alog, dsk, nw, *[it[0] for it in comm])


def ssd_bwd(xact, proj, dtg, dtb, alog, dsk, nw, states, dyb, dproj, comm):
    T = xact.shape[0]
    nc = T // Q

    nx = len(comm)
    ng = SSM_GROUPS // SSD_GPS
    any_spec = pl.BlockSpec(memory_space=pl.ANY)

    def body(*refs):
        in_refs, (st_ref, dy_ref, _) = refs[:9], refs[9:12]
        src_refs, refs = refs[12:12 + nx], refs[12 + nx:]
        dx_ref, db_ref, dc_ref, dz_ref, ddt_ref, ddtb_ref, dal_ref, ddk_ref, dnw_ref = refs[:9]
        out_refs, dstate, sems = refs[9:9 + nx], refs[9 + nx], refs[10 + nx:]
        exchange_start(comm, src_refs, out_refs, sems, (pl.program_id(0) == 0) & (pl.program_id(1) == 0))

        @pl.when(pl.program_id(1) == 0)
        def _():
            dstate[...] = jnp.zeros(dstate.shape, F32)
            ddtb_ref[...] = jnp.zeros(ddtb_ref.shape, F32)
            dal_ref[...] = jnp.zeros(dal_ref.shape, F32)
            ddk_ref[...] = jnp.zeros(ddk_ref.shape, F32)
            dnw_ref[...] = jnp.zeros(dnw_ref.shape, F32)

        for gi in range(SSD_GPS):
            xs = slice(gi * SSM_GW, (gi + 1) * SSM_GW)
            bs = slice(gi * 128, (gi + 1) * 128)
            lane0 = SSM_HPG * (SSD_GPS * pl.program_id(0) + gi)
            ins = (st_ref[gi, 0],) + _ssd_group_inputs(gi, *in_refs)
            _, vjp = jax.vjp(functools.partial(ssd_step, lane0), *ins)
            ds, dx, dz, ddt, dbm, dcm, ddtb, dal, ddk, dnw = vjp((dstate[gi], dy_ref[:, xs].astype(F32)))
            dstate[gi] = ds
            dx_ref[:, xs] = dx.astype(dx_ref.dtype)
            db_ref[:, bs] = dbm.astype(db_ref.dtype)
            dc_ref[:, bs] = dcm.astype(dc_ref.dtype)
            dz_ref[:, xs] = dz.astype(dz_ref.dtype)
            ddt_ref[gi] = ddt
            ddtb_ref[gi] += ddtb
            dal_ref[gi] += dal
            ddk_ref[gi] += ddk
            dnw_ref[:, xs] += dnw

        exchange_finish(comm, src_refs, out_refs, sems,
                        (pl.program_id(0) == ng - 1) & (pl.program_id(1) == nc - 1))

    rev = lambda n: nc - 1 - n
    row_shape = jax.ShapeDtypeStruct((SSM_GROUPS, 1, 128), F32)
    row_spec = pl.BlockSpec((SSD_GPS, 1, 128), lambda g, n: (g, 0, 0))
    return pl.pallas_call(
        body, name="ssd_bwd", grid=(ng, nc),
        out_shape=[jax.ShapeDtypeStruct((T, SSM_INNER), BF16),
                   jax.ShapeDtypeStruct((T, SSM_GROUPS * 128), BF16),
                   jax.ShapeDtypeStruct((T, SSM_GROUPS * 128), BF16),
                   jax.ShapeDtypeStruct(dproj.shape, dproj.dtype),
                   jax.ShapeDtypeStruct((SSM_GROUPS, T, 128), F32),
                   row_shape, row_shape,
                   jax.ShapeDtypeStruct((SSM_GROUPS, SSM_HPG, 1, 1), F32),
                   jax.ShapeDtypeStruct((1, SSM_INNER), F32)] + exchange_out_shapes(comm),
        in_specs=_ssd_in_specs(True, nc) + [
            pl.BlockSpec((SSD_GPS, 1, 128, SSM_GW), lambda g, n: (g, rev(n), 0, 0)),
            pl.BlockSpec((Q, _XW), lambda g, n: (rev(n), g)),
            any_spec] + [any_spec] * nx,
        out_specs=[pl.BlockSpec((Q, _XW), lambda g, n: (rev(n), g)),
                   pl.BlockSpec((Q, _BW), lambda g, n: (rev(n), g)),
                   pl.BlockSpec((Q, _BW), lambda g, n: (rev(n), g)),
                   pl.BlockSpec((Q, _XW), lambda g, n: (rev(n), P_Z // _XW + g)),
                   pl.BlockSpec((SSD_GPS, Q, 128), lambda g, n: (g, rev(n), 0)),
                   row_spec, row_spec,
                   pl.BlockSpec((SSD_GPS, SSM_HPG, 1, 1), lambda g, n: (g, 0, 0, 0)),
                   pl.BlockSpec((1, _XW), lambda g, n: (0, g))] + [any_spec] * nx,
        scratch_shapes=[pltpu.VMEM((SSD_GPS, 128, SSM_GW), F32)] + exchange_semaphores(comm),
        input_output_aliases={11: 3},
        compiler_params=_params(("arbitrary", "arbitrary")),
    )(xact, xact, xact, proj, dtg, dtb, alog, dsk, nw, states, dyb, dproj, *[it[0] for it in comm])


ADAMW_WHOLE_ELEMS = 256 * 1024


def adamw(w, g, m, v, *, name):
    shape = w.shape
    parts = g.shape != shape
    nd = len(shape)
    if w.size <= ADAMW_WHOLE_ELEMS:
        grid, tr = (1,), shape[-2]
    else:
        assert all(s == 1 for s in shape[:-2]) and shape[-2] % 256 == 0
        grid, tr = (shape[-2] // 256,), 256
    blk = tuple(shape[:-2]) + (tr, shape[-1])
    spec = pl.BlockSpec(blk, lambda i: (0,) * (nd - 2) + (i, 0))
    g_spec = pl.BlockSpec((N_DEV,) + blk[1:], lambda i: (0,) * (nd - 2) + (i, 0)) if parts else spec

    def body(w_ref, g_ref, m_ref, v_ref, go_ref, d_ref, nm_ref, nv_ref):
        if parts:
            g = g_ref[0:1].astype(F32)
            for j in range(1, N_DEV):
                g = g + g_ref[j:j + 1].astype(F32)
        else:
            g = g_ref[...]
        nm = ADAM_B1 * m_ref[...] + (1.0 - ADAM_B1) * g
        nv = ADAM_B2 * v_ref[...] + (1.0 - ADAM_B2) * jnp.square(g)
        m_hat = nm / (1.0 - ADAM_B1 ** ADAM_STEP)
        v_hat = nv / (1.0 - ADAM_B2 ** ADAM_STEP)
        go_ref[...] = g
        d_ref[...] = -ADAM_LR * (m_hat / (jnp.sqrt(v_hat) + ADAM_EPS) + ADAM_WD * w_ref[...])
        nm_ref[...] = nm
        nv_ref[...] = nv

    shp = jax.ShapeDtypeStruct(shape, F32)
    return pl.pallas_call(
        body, name=name, grid=grid,
        out_shape=[shp] * 4, in_specs=[spec, g_spec, spec, spec], out_specs=[spec] * 4,
        compiler_params=_params(("parallel",)),
    )(w, g, m, v)


def _pad_rows(a, rows):
    return jnp.pad(a, ((0, rows - a.shape[0]), (0, 0)))


WIN_W = 1408
N_IN = IN_WIDTH // N_DEV
_A6 = OFF_DT - 6 * N_IN
_C6 = 7 * N_IN - OFF_GA


def _win_offset(me):
    return jnp.where(me == 7, 124, 4 * me)


def _w_in_window(shard, me):
    rows = shard.shape[0]
    z = lambda n: jnp.zeros((rows, n), shard.dtype)
    a = lax.dynamic_update_slice(z(WIN_W), shard, (0, _win_offset(me)))
    b = jnp.concatenate([z(24), shard[:, :_A6], shard[:, _A6 + 32:], z(4), shard[:, _A6:_A6 + 32], z(96)], axis=1)
    return jnp.where(me == 6, b, a)


def _w_in_from_window(window, me):
    a = lax.dynamic_slice(window, (0, _win_offset(me)), (window.shape[0], N_IN))
    b = jnp.concatenate([window[:, 24:24 + _A6], window[:, 1280:1312], window[:, 24 + _A6:24 + _A6 + _C6]], axis=1)
    return jnp.where(me == 6, b, a)


def _w_all_from_windows(g):
    def merge_first(p, t):
        return jnp.concatenate([p[:, :128] + t, p[:, 128:]], axis=1)

    parts = [g[0][:, :1280]]
    for j in range(1, 6):
        parts.append(merge_first(g[j][:, :1280], g[j - 1][:, 1280:]))
    p6 = merge_first(g[6][:, :1280], g[5][:, 1280:])
    parts.append(jnp.concatenate([p6[:, :1152], p6[:, 1152:] + g[7][:, :128]], axis=1))
    parts.append(g[7][:, 128:])
    parts.append(g[6][:, 1280:])
    return jnp.concatenate(parts, axis=1)


def _windows_of_w_all(gw):
    wins = [gw[:, 1280 * j:1280 * j + WIN_W] for j in range(6)]
    wins.append(jnp.concatenate([gw[:, 7680:8960], gw[:, PROJ_W:]], axis=1))
    wins.append(gw[:, 8832:PROJ_W])
    return jnp.stack(wins)


def kernel(x, c, w_mod, b_mod, w_in, gm_norm_w, gm_ws, gm_bs, conv_w, conv_b, dt_bias, a_log, d_skip, ssm_norm_w, w_branch_gm, w_branch_ssm, w_out, w_ff1, w_ff2, final_norm_w, loss_target, m_w_mod, m_b_mod, m_w_in, m_gm_norm_w, m_gm_ws, m_gm_bs, m_conv_w, m_conv_b, m_dt_bias, m_a_log, m_d_skip, m_ssm_norm_w, m_w_branch_gm, m_w_branch_ssm, m_w_out, m_w_ff1, m_w_ff2, m_final_norm_w, v_w_mod, v_b_mod, v_w_in, v_gm_norm_w, v_gm_ws, v_gm_bs, v_conv_w, v_conv_b, v_dt_bias, v_a_log, v_d_skip, v_ssm_norm_w, v_w_branch_gm, v_w_branch_ssm, v_w_out, v_w_ff1, v_w_ff2, v_final_norm_w):
    T = x.shape[1]
    me = 4 * lax.axis_index("x") + 2 * lax.axis_index("y") + lax.axis_index("c")
    x2 = x[0]
    tgt = loss_target[0]
    n_in = IN_WIDTH // N_DEV
    n_mod = N_MOD * D // N_DEV
    n_cv = CONV_DIM // N_DEV

    c_all, conv_w_full = exchange(
        [(c.reshape(8, 128), _whole, (N_DEV, 8, 128), _slot),
         (conv_w[0], _whole, (N_DEV, CONV_K, n_cv), _slot)], name="gather_c_convw")
    c_all = c_all.reshape(N_DEV, D)
    conv_w_full = conv_w_full.transpose(1, 0, 2).reshape(CONV_K, CONV_DIM)

    win = _w_in_window(w_in[0].astype(BF16), me)
    gwin = gather_blocks_two_level(win, name="gather_w_in")
    late_weights = [
        (w_branch_gm[0].astype(BF16), _whole, (D, D), _rows(D // N_DEV)),
        (w_branch_ssm[0].astype(BF16), _whole, (SSM_INNER, D), _rows(SSM_INNER // N_DEV)),
        (w_out[0].astype(BF16), _whole, (D, D), _rows(D // N_DEV)),
        (w_ff1[0].astype(BF16), _whole, (D, D_FF), _cols(D_FF // N_DEV)),
        (w_ff2[0].astype(BF16), _whole, (D_FF, D), _rows(D_FF // N_DEV))]
    w_all = _w_all_from_windows(gwin)
    w_dt = w_all[:, PROJ_W:]

    c_pad = _pad_rows(c_all, 128)
    b_mine = lax.dynamic_slice(b_mod, (0, me * n_mod), (1, n_mod))

    def mod_fn(cp, w, b):
        ca = cp * jax.nn.sigmoid(cp)
        return (jnp.dot(ca, w, precision=HIGHEST, preferred_element_type=F32) + b,)

    (mod_part,) = whole_call(mod_fn, [c_pad, w_mod[0], b_mine], [((128, n_mod), F32)], name="mod_fwd")
    gmod = gather_blocks(mod_part[:N_DEV], name="gather_mod")
    mod = lax.dynamic_index_in_dim(gmod, me, axis=1, keepdims=False).reshape(N_MOD, D)
    sh1, sc1, gt1, sh2, sc2, gt2 = [mod[i:i + 1] for i in range(N_MOD)]

    (h,) = rowwise_call(fwd_body(fn_modulate), [x2], [sc1, sh1], [(D, BF16)], [], tm=256, name="modulate1")
    proj = matmul(h, w_all, "nn", BF16, name="mm_proj", n=PROJ_W)
    dtg = matmul(h, w_dt, "nn", F32, name="mm_dt")
    ws = gm_ws[0]
    bs3 = gm_bs[0].reshape(GM_GROUPS, Q, 1)
    sgu_rows = [(proj, D, P_U), (proj, D, P_V)]
    (ya,) = rowwise_call(fwd_body(fn_sgu), sgu_rows, [gm_norm_w, ws, bs3], [(D, BF16)], [],
                         tm=256, name="sgu_fwd")
    xact = conv_fwd(proj, conv_w_full, conv_b)
    dtb4 = jnp.pad(dt_bias, ((0, 0), (0, 96)))
    alog4 = jnp.pad(a_log, ((0, 0), (0, 96)))
    dsk4 = d_skip.reshape(SSM_GROUPS, SSM_HPG, 1, 1)
    yb, states, w_gm_f, w_ssm_f, w_out_f, w_ff1_f, w_ff2_f = ssd_fwd(
        xact, proj, dtg, dtb4, alog4, dsk4, ssm_norm_w, late_weights)
    pa = matmul(ya, w_gm_f, "nn", F32, name="mm_branch_gm")
    pb = matmul(yb, w_ssm_f, "nn", F32, name="mm_branch_ssm")
    gate_rows = [(proj, D, P_GA), (proj, D, P_GB)]
    (mixed,) = rowwise_call(fwd_body(fn_mix), gate_rows + [pa, pb], [], [(D, BF16)], [], tm=256, name="mix_fwd")
    o = matmul(mixed, w_out_f, "nn", F32, name="mm_out")
    x1, h2 = rowwise_call(fwd_body(fn_res_modulate), [x2, o], [gt1, sc2, sh2], [(D, F32), (D, BF16)], [],
                          tm=256, name="res_modulate2")
    f = matmul(h2, w_ff1_f, "nn", BF16, name="mm_ff1")
    gf = matmul(f, w_ff2_f, "nn", F32, name="mm_ff2", a_pro=relu2_tile)

    dx1, dgf, loss_v, dgt2, dfnw = rowwise_call(
        final_body, [x1, gf, tgt], [gt2, final_norm_w.reshape(1, D)], [(D, F32), (D, BF16)],
        [(1, 128), (1, D), (1, D)], tm=256, name="final_loss_bwd")
    df = matmul(dgf, w_ff2_f, "nt", BF16, name="mm_ff2_dgrad", epi=relu2_grad_tile, epi_ins=(f,))
    gw_ff2 = matmul(f, dgf, "tn", BF16, name="mm_ff2_wgrad", tk=512, a_pro=relu2_tile)
    dh2 = matmul(df, w_ff1_f, "nt", F32, name="mm_ff1_dgrad")
    gw_ff1 = matmul(h2, df, "tn", BF16, name="mm_ff1_wgrad", tk=512)

    def res_mod_bwd(r, fl):
        xv, ov, dx1v, dh2v = r
        _, vjp = jax.vjp(fn_res_modulate, xv, ov, *fl)
        dxv, dov, dg1, dsc, dsh = vjp((dx1v, dh2v))
        return (dxv, dov), (dg1, dsc, dsh)

    dxa, do, dgt1, dsc2, dsh2 = rowwise_call(
        res_mod_bwd, [x2, o, dx1, dh2], [gt1, sc2, sh2], [(D, F32), (D, BF16)],
        [(1, D), (1, D), (1, D)], tm=256, name="res_modulate2_bwd")
    dmixed = matmul(do, w_out_f, "nt", F32, name="mm_out_dgrad")
    gw_out = matmul(mixed, do, "tn", BF16, name="mm_out_wgrad", tk=512)
    dproj = lax.empty((T, ALL_W), BF16)

    def mix_bwd(r, fl):
        dga, dgb, dpa, dpb = bwd_body(fn_mix, 4)(r, fl)[0]
        return (jnp.concatenate([dga, dgb], axis=1), dpa, dpb), ()

    dproj, dpa, dpb = rowwise_call(
        mix_bwd, gate_rows + [pa, pb, dmixed], [], [(dproj, 2 * D, P_GA), (D, BF16), (D, BF16)], [],
        tm=256, name="mix_bwd")
    dya = matmul(dpa, w_gm_f, "nt", F32, name="mm_branch_gm_dgrad")
    gw_gm = matmul(ya, dpa, "tn", BF16, name="mm_branch_gm_wgrad", tk=512)
    dyb = matmul(dpb, w_ssm_f, "nt", BF16, name="mm_branch_ssm_dgrad")
    gw_ssm = matmul(yb, dpb, "tn", BF16, name="mm_branch_ssm_wgrad", tk=512)

    def sgu_bwd(r, fl):
        (du, dv), acc = bwd_body(fn_sgu, 2)(r, fl)
        return (jnp.concatenate([du, dv], axis=1),), acc

    dproj, dgnw, dws, dbs = rowwise_call(
        sgu_bwd, sgu_rows + [dya], [gm_norm_w, ws, bs3], [(dproj, 2 * D, P_U)],
        [(1, D), (GM_GROUPS, Q, Q), (GM_GROUPS, Q, 1)], tm=256, name="sgu_bwd")
    early_grads = [
        (gw_gm, _rows(D // N_DEV), (N_DEV, D // N_DEV, D), _slot),
        (gw_ssm, _rows(SSM_INNER // N_DEV), (N_DEV, SSM_INNER // N_DEV, D), _slot),
        (gw_out, _rows(D // N_DEV), (N_DEV, D // N_DEV, D), _slot),
        (gw_ff1, _cols(D_FF // N_DEV), (N_DEV, D, D_FF // N_DEV), _slot),
        (gw_ff2, _rows(D_FF // N_DEV), (N_DEV, D_FF // N_DEV, D), _slot),
        (_pack_rows([dgnw, dws, dbs], EARLY_ROWS), _whole, (N_DEV, sum(EARLY_ROWS), 128), _slot)]
    (dxs, dbm, dcm, dproj, ddt8, ddtb, dalog, ddsk, dsnw,
     r_gm, r_ssm, r_out, r_ff1, r_ff2, early_all) = ssd_bwd(
        xact, proj, dtg, dtb4, alog4, dsk4, ssm_norm_w, states, dyb, dproj, early_grads)
    dconv_w, dconv_b = [], []
    for nm, dact_part, col0 in (("xs", dxs, 0), ("b", dbm, SSM_INNER), ("c", dcm, SSM_INNER + SSM_GROUPS * 128)):
        dproj, dcw, dcb = conv_bwd(proj, dact_part, col0, conv_w_full, conv_b, dproj, name="conv_bwd_" + nm)
        dconv_w.append(dcw)
        dconv_b.append(dcb)
    dconv_w = jnp.concatenate(dconv_w, axis=1)
    dconv_b = jnp.concatenate(dconv_b, axis=1)
    dproj = dproj.at[:, PROJ_W:].set(jnp.sum(ddt8, axis=0).astype(BF16))
    gw_all = matmul(h, dproj, "tn", BF16, name="mm_in_wgrad", tn=1152, tk=512)
    mid_pack = _pack_rows([dconv_w, dconv_b, jnp.sum(ddtb, axis=0), jnp.sum(dalog, axis=0), ddsk, dsnw, dfnw,
                           jnp.concatenate([dgt1, dsh2, dsc2, dgt2], axis=0)], MID_ROWS)
    dh, r_in, mid_all = matmul(
        dproj, w_all, "nt", F32, name="mm_in_dgrad", tk=1152,
        comm=[(_windows_of_w_all(gw_all), _slot, (N_DEV, D, WIN_W), _slot),
              (mid_pack, _whole, (N_DEV, sum(MID_ROWS), 128), _slot)])
    grad_x, dsc1, dsh1 = rowwise_call(grad_x_body, [x2, dh, dxa], [sc1, sh1], [(D, F32)],
                                      [(1, D), (1, D)], tm=256, name="modulate1_bwd")

    g_w_in = _w_in_from_window(sum_devices(r_in, tr=256, name="sum_w_in_grads"), me).reshape(1, D, n_in)

    late_all = gather_blocks(_pack_rows([dsh1, dsc1], LATE_ROWS), name="gather_dmod1")
    s_early = _unpack_rows(sum_devices(early_all, tr=early_all.shape[1], name="sum_small_early"), EARLY_ROWS)
    s_mid = _unpack_rows(sum_devices(mid_all, tr=mid_all.shape[1], name="sum_small_mid"), MID_ROWS)
    s_late = _unpack_rows(sum_devices(late_all, tr=late_all.shape[1], name="sum_small_late"), LATE_ROWS)
    g_gm_norm_w = s_early[0][:D].reshape(1, D)
    g_gm_ws = s_early[1].reshape(GM_GROUPS * Q, Q)
    g_gm_bs = s_early[2][:GM_GROUPS * Q].reshape(GM_GROUPS, Q)
    g_conv_w_full = s_mid[0].reshape(CONV_K, CONV_DIM)
    g_conv_w = lax.dynamic_slice(g_conv_w_full, (0, me * n_cv), (CONV_K, n_cv))
    g_conv_b = s_mid[1].reshape(1, CONV_DIM)
    g_dt_bias = s_mid[2][:32].reshape(1, 32)
    g_a_log = s_mid[3][:32].reshape(1, 32)
    g_d_skip = s_mid[4][:32].reshape(1, 32)
    g_ssm_norm_w = s_mid[5].reshape(1, SSM_INNER)
    g_final_norm_w = s_mid[6][:D].reshape(1, D)
    g_b_mod = jnp.concatenate([s_late[0][:D], s_late[1][:D], s_mid[7]]).reshape(1, N_MOD * D)

    dmod_all = jnp.concatenate(
        [late_all.reshape(N_DEV, -1)[:, :2 * D],
         mid_all[:, sum(MID_ROWS[:7]):].reshape(N_DEV, 4 * D)], axis=1)
    dmod_mine = _pad_rows(lax.dynamic_slice(dmod_all, (0, me * n_mod), (N_DEV, n_mod)), 128)

    def wmod_grad_fn(cp, dm):
        ca = cp * jax.nn.sigmoid(cp)
        return (lax.dot_general(ca, dm, (((0,), (0,)), ((), ())), precision=HIGHEST,
                                preferred_element_type=F32),)

    (g_w_mod,) = whole_call(wmod_grad_fn, [c_pad, dmod_mine], [((D, n_mod), F32)], name="w_mod_grad")

    upd = {}

    def step(name, w, g, m, v, parts=False):
        upd[name] = adamw(w, g if parts else g.reshape(w.shape), m, v, name="adamw_" + name)

    step("w_mod", w_mod, g_w_mod, m_w_mod, v_w_mod)
    step("b_mod", b_mod, g_b_mod, m_b_mod, v_b_mod)
    step("w_in", w_in, g_w_in, m_w_in, v_w_in)
    step("gm_norm_w", gm_norm_w, g_gm_norm_w, m_gm_norm_w, v_gm_norm_w)
    step("gm_ws", gm_ws, g_gm_ws, m_gm_ws, v_gm_ws)
    step("gm_bs", gm_bs, g_gm_bs, m_gm_bs, v_gm_bs)
    step("conv_w", conv_w, g_conv_w, m_conv_w, v_conv_w)
    step("conv_b", conv_b, g_conv_b, m_conv_b, v_conv_b)
    step("dt_bias", dt_bias, g_dt_bias, m_dt_bias, v_dt_bias)
    step("a_log", a_log, g_a_log, m_a_log, v_a_log)
    step("d_skip", d_skip, g_d_skip, m_d_skip, v_d_skip)
    step("ssm_norm_w", ssm_norm_w, g_ssm_norm_w, m_ssm_norm_w, v_ssm_norm_w)
    step("w_branch_gm", w_branch_gm, r_gm, m_w_branch_gm, v_w_branch_gm, parts=True)
    step("w_branch_ssm", w_branch_ssm, r_ssm, m_w_branch_ssm, v_w_branch_ssm, parts=True)
    step("w_out", w_out, r_out, m_w_out, v_w_out, parts=True)
    step("w_ff1", w_ff1, r_ff1, m_w_ff1, v_w_ff1, parts=True)
    step("w_ff2", w_ff2, r_ff2, m_w_ff2, v_w_ff2, parts=True)
    step("final_norm_w", final_norm_w.reshape(1, D), g_final_norm_w, m_final_norm_w.reshape(1, D),
         v_final_norm_w.reshape(1, D))
    upd["final_norm_w"] = tuple(a.reshape(D) for a in upd["final_norm_w"])

    loss = lax.psum(loss_v[0, 0], ("x", "y", "c"))
    order = ["w_mod", "b_mod", "w_in", "gm_norm_w", "gm_ws", "gm_bs", "conv_w", "conv_b", "dt_bias", "a_log",
             "d_skip", "ssm_norm_w", "w_branch_gm", "w_branch_ssm", "w_out", "w_ff1", "w_ff2", "final_norm_w"]
    return (loss, grad_x.reshape(1, T, D),
            *[upd[n][0] for n in order], *[upd[n][1] for n in order],
            *[upd[n][2] for n in order], *[upd[n][3] for n in order])
```

```python
import functools

import jax
import jax.numpy as jnp
from jax import lax
from jax.experimental import pallas as pl
from jax.experimental.pallas import tpu as pltpu

F32 = jnp.float32
BF16 = jnp.bfloat16
MESH = pl.DeviceIdType.MESH
HIGHEST = lax.Precision.HIGHEST

N_DEV = 8
D = 1024
Q = 128
GM_GROUPS = 8
SSM_INNER = 2048
SSM_GROUPS = 8
SSM_HPG = 4
SSM_P = 64
SSM_GW = SSM_HPG * SSM_P
CONV_DIM = 4096
CONV_K = 4
D_FF = 4096
N_MOD = 6
EPS = 1e-6
IN_WIDTH = 10272
OFF_DT = 8192
OFF_GA = 8224
PROJ_W = 10240
ALL_W = 10368
P_U, P_V, P_Z, P_XBC, P_GA, P_GB = 0, 1024, 2048, 4096, 8192, 9216

ADAM_LR = 0.001
ADAM_B1 = 0.9
ADAM_B2 = 0.999
ADAM_EPS = 1e-08
ADAM_WD = 0.01
ADAM_STEP = 10

VMEM_LIMIT_BYTES = 48 * 1024 * 1024
FUSED_TM = 256
WGRAD_TK = 2048
EARLY_ROWS = (8, 1024, 8)
MID_ROWS = (128, 32, 8, 8, 8, 16, 8, 32)
LATE_ROWS = (8, 8)


def _pack_rows(arrs, rows):
    def rows128(a, r):
        a = a.reshape(-1)
        return jnp.pad(a, (0, r * 128 - a.shape[0])).reshape(r, 128)
    return jnp.concatenate([rows128(a, r) for a, r in zip(arrs, rows)], axis=0)


def _unpack_rows(s, rows):
    out, o = [], 0
    for r in rows:
        out.append(s[o:o + r].reshape(-1))
        o += r
    return out


def _params(sem=None):
    return pltpu.CompilerParams(dimension_semantics=sem, vmem_limit_bytes=VMEM_LIMIT_BYTES)


def _dg(a, b, ca, cb):
    return lax.dot_general(a.astype(BF16), b.astype(BF16), (((ca,), (cb,)), ((), ())),
                           preferred_element_type=F32)


@jax.custom_vjp
def dot_nn(a, b):
    return _dg(a, b, 1, 0)


@jax.custom_vjp
def dot_nt(a, b):
    return _dg(a, b, 1, 1)


@jax.custom_vjp
def dot_tn(a, b):
    return _dg(a, b, 0, 0)


def _like(ct, primal):
    return ct.astype(primal.dtype)


dot_nn.defvjp(lambda a, b: (dot_nn(a, b), (a, b)),
              lambda r, g: (_like(dot_nt(g, r[1]), r[0]), _like(dot_tn(r[0], g), r[1])))
dot_nt.defvjp(lambda a, b: (dot_nt(a, b), (a, b)),
              lambda r, g: (_like(dot_nn(g, r[1]), r[0]), _like(dot_tn(g, r[0]), r[1])))
dot_tn.defvjp(lambda a, b: (dot_tn(a, b), (a, b)),
              lambda r, g: (_like(dot_nt(r[1], g), r[0]), _like(dot_nn(r[0], g), r[1])))


def _rms(x):
    return x * lax.rsqrt(jnp.mean(x * x, axis=-1, keepdims=True) + EPS)


def _softplus(x):
    return jnp.maximum(x, 0.0) + jnp.log1p(jnp.exp(-jnp.abs(x)))


def _rows(n):
    return lambda ref, j: ref.at[pl.ds(pl.multiple_of(j * n, n), n)]


def _cols(n):
    return lambda ref, j: ref.at[:, pl.ds(pl.multiple_of(j * n, n), n)]


def _slot(ref, j):
    return ref.at[j]


def _whole(ref, j):
    return ref


def exchange(items, *, name):
    n = len(items)

    def body(*refs):
        exchange_in_body(items, refs[:n], refs[n:2 * n], refs[2 * n:], True, True)

    return pl.pallas_call(
        body, name=name,
        out_shape=exchange_out_shapes(items),
        in_specs=[pl.BlockSpec(memory_space=pl.ANY)] * n,
        out_specs=[pl.BlockSpec(memory_space=pl.ANY)] * n,
        scratch_shapes=exchange_semaphores(items),
    )(*[it[0] for it in items])


def exchange_out_shapes(items):
    return [jax.ShapeDtypeStruct(tuple(shape), src.dtype) for (src, _, shape, _) in items]


def exchange_semaphores(items):
    n = len(items)
    return [pltpu.SemaphoreType.DMA((n, N_DEV - 1)), pltpu.SemaphoreType.DMA((n, N_DEV - 1)),
            pltpu.SemaphoreType.DMA((n,))]


def _exchange_copies(items, src_refs, out_refs, sems):
    send_sems, recv_sems, local_sems = sems
    x = lax.axis_index("x")
    y = lax.axis_index("y")
    c = lax.axis_index("c")
    me = 4 * x + 2 * y + c
    local = [pltpu.make_async_copy(src_win(src_refs[i], me), dst_win(out_refs[i], me), local_sems.at[i])
             for i, (_, src_win, _, dst_win) in enumerate(items)]
    remote = []
    for i, (_, src_win, _, dst_win) in enumerate(items):
        for k in range(1, N_DEV):
            px = lax.rem(x + ((k >> 2) & 1), 2)
            py = lax.rem(y + ((k >> 1) & 1), 2)
            pc = lax.rem(c + (k & 1), 2)
            peer = 4 * px + 2 * py + pc
            remote.append(pltpu.make_async_remote_copy(
                src_ref=src_win(src_refs[i], peer), dst_ref=dst_win(out_refs[i], me),
                send_sem=send_sems.at[i, k - 1], recv_sem=recv_sems.at[i, k - 1],
                device_id=(px, py, pc), device_id_type=MESH))
    return local, remote


def _when(cond, fn):
    if cond is True:
        fn()
    else:
        pl.when(cond)(fn)


def exchange_start(items, src_refs, out_refs, sems, cond):
    def start():
        local, remote = _exchange_copies(items, src_refs, out_refs, sems)
        for cp in local + remote:
            cp.start()
    _when(cond, start)


def exchange_finish(items, src_refs, out_refs, sems, cond):
    def finish():
        local, remote = _exchange_copies(items, src_refs, out_refs, sems)
        for cp in remote:
            cp.wait_send()
        for cp in remote:
            cp.wait_recv()
        for cp in local:
            cp.wait()
    _when(cond, finish)


def exchange_in_body(items, src_refs, out_refs, sems, first, last):
    exchange_start(items, src_refs, out_refs, sems, first)
    exchange_finish(items, src_refs, out_refs, sems, last)


def gather_blocks(src, *, name):
    return exchange([(src, _whole, (N_DEV,) + src.shape, _slot)], name=name)[0]


def gather_blocks_two_level(src, *, name):
    def body(src_ref, out_ref, send_sems, recv_sems, local_sem):
        x = lax.axis_index("x")
        y = lax.axis_index("y")
        c = lax.axis_index("c")
        me, sibling = (x, y, c), (x, y, 1 - c)
        chips = [(1 - x, y), (x, 1 - y), (1 - x, 1 - y)]

        def slot(px, py, pc):
            return out_ref.at[4 * px + 2 * py + pc]

        def copy(k, block, to, src=None):
            return pltpu.make_async_remote_copy(
                src_ref=slot(*block) if src is None else src, dst_ref=slot(*block),
                send_sem=send_sems.at[k], recv_sem=recv_sems.at[k], device_id=to, device_id_type=MESH)

        mine = pltpu.make_async_copy(src_ref, slot(*me), local_sem)
        mine.start()
        first = [copy(0, me, sibling, src=src_ref)]
        first += [copy(1 + j, me, (*chip, c), src=src_ref) for j, chip in enumerate(chips)]
        for cp in first:
            cp.start()
        passed = [copy(4 + j, (*chip, c), sibling) for j, chip in enumerate(chips)]
        for j, chip in enumerate(chips):
            copy(1 + j, (*chip, c), me).wait_recv()
            passed[j].start()
        copy(0, sibling, me).wait_recv()
        for j, chip in enumerate(chips):
            copy(4 + j, (*chip, 1 - c), me).wait_recv()
        for cp in first + passed:
            cp.wait_send()
        mine.wait()

    return pl.pallas_call(
        body, name=name,
        out_shape=jax.ShapeDtypeStruct((N_DEV,) + src.shape, src.dtype),
        in_specs=[pl.BlockSpec(memory_space=pl.ANY)],
        out_specs=pl.BlockSpec(memory_space=pl.ANY),
        scratch_shapes=[pltpu.SemaphoreType.DMA((N_DEV - 1,)), pltpu.SemaphoreType.DMA((N_DEV - 1,)),
                        pltpu.SemaphoreType.DMA(())],
    )(src)


def sum_devices(g, *, tr, name):
    _, R, C = g.shape

    def body(g_ref, o_ref):
        acc = g_ref[0].astype(F32)
        for j in range(1, N_DEV):
            acc = acc + g_ref[j].astype(F32)
        o_ref[...] = acc

    return pl.pallas_call(
        body, name=name, grid=(R // tr,),
        out_shape=jax.ShapeDtypeStruct((R, C), F32),
        in_specs=[pl.BlockSpec((N_DEV, tr, C), lambda i: (0, i, 0))],
        out_specs=pl.BlockSpec((tr, C), lambda i: (i, 0)),
        compiler_params=_params(("parallel",)),
    )(g)


def matmul(a, b, mode, out_dtype, *, name, tm=1024, tn=1024, tk=1024, n=None, comm=None,
           a_pro=None, epi=None, epi_ins=()):
    if mode == "nn":
        (M, K), (K2, N) = a.shape, b.shape
    elif mode == "nt":
        (M, K), (N, K2) = a.shape, b.shape
    else:
        (K, M), (K2, N) = a.shape, b.shape
    assert K == K2
    N = N if n is None else n
    tm, tn, tk = min(tm, M), min(tn, N), min(tk, K)
    assert M % tm == 0 and N % tn == 0 and K % tk == 0, (name, M, N, K, tm, tn, tk)
    nk = K // tk
    if mode == "tn":
        a_spec = pl.BlockSpec((tk, tm), lambda i, j, k: (k, i))
    else:
        a_spec = pl.BlockSpec((tm, tk), lambda i, j, k: (i, k))
    if mode == "nt":
        b_spec = pl.BlockSpec((tn, tk), lambda i, j, k: (j, k))
    else:
        b_spec = pl.BlockSpec((tk, tn), lambda i, j, k: (k, j))
    dims = {"nn": (1, 0), "nt": (1, 1), "tn": (0, 0)}[mode]
    items = list(comm) if comm else []
    nx = len(items)
    ne = len(epi_ins)
    gm, gn = M // tm, N // tn
    any_spec = pl.BlockSpec(memory_space=pl.ANY)
    o_spec = pl.BlockSpec((tm, tn), lambda i, j, k: (i, j))

    def body(*refs):
        a_ref, b_ref, e_refs = refs[0], refs[1], refs[2:2 + ne]
        refs = refs[2 + ne:]
        src_refs, o_ref, out_refs = refs[:nx], refs[nx], refs[1 + nx:1 + 2 * nx]
        acc_ref, sems = refs[1 + 2 * nx], refs[2 + 2 * nx:]
        i, j, k = pl.program_id(0), pl.program_id(1), pl.program_id(2)
        if items:
            exchange_start(items, src_refs, out_refs, sems, (i == 0) & (j == 0) & (k == 0))
        a_tile = a_ref[...] if a_pro is None else a_pro(a_ref[...])
        part = lax.dot_general(a_tile, b_ref[...], (((dims[0],), (dims[1],)), ((), ())),
                               preferred_element_type=F32)

        def finish(acc):
            if epi is not None:
                acc = epi(acc, *[e[...] for e in e_refs])
            o_ref[...] = acc.astype(o_ref.dtype)

        if nk == 1:
            finish(part)
        else:
            @pl.when(k == 0)
            def _():
                acc_ref[...] = part

            @pl.when((k > 0) & (k < nk - 1))
            def _():
                acc_ref[...] += part

            @pl.when(k == nk - 1)
            def _():
                finish(acc_ref[...] + part)

        if items:
            exchange_finish(items, src_refs, out_refs, sems, (i == gm - 1) & (j == gn - 1) & (k == nk - 1))

    res = pl.pallas_call(
        body, name=name, grid=(gm, gn, nk),
        out_shape=[jax.ShapeDtypeStruct((M, N), out_dtype)] + exchange_out_shapes(items),
        in_specs=[a_spec, b_spec] + [o_spec] * ne + [any_spec] * nx,
        out_specs=[o_spec] + [any_spec] * nx,
        scratch_shapes=[pltpu.VMEM((tm, tn) if nk > 1 else (8, 128), F32)]
        + (exchange_semaphores(items) if items else []),
        compiler_params=_params(("arbitrary",) * 3 if items else ("parallel", "parallel", "arbitrary")),
    )(a, b, *epi_ins, *[it[0] for it in items])
    return res if items else res[0]


def rowwise_call(body_fn, rows, fulls, row_outs, acc_outs, *, tm, name, mm=None, tk=1024):
    rows = [r if isinstance(r, tuple) else (r, r.shape[1], 0) for r in rows]
    T = rows[0][0].shape[0]
    tm = min(tm, T)
    assert T % tm == 0
    n_r, n_f, n_ro = len(rows), len(fulls), len(row_outs)
    into = [(k, ro) for k, ro in enumerate(row_outs) if len(ro) == 3]
    n_b = len(into)
    n_mm, nk = 0, 1
    if mm is not None:
        a, b, mode, pos, a_pro = mm
        n_mm = 2
        K = a.shape[1]
        N = b.shape[1] if mode == "nn" else b.shape[0]
        tk = min(tk, K)
        assert K % tk == 0 and a.shape[0] == T
        nk = K // tk
        b_contract = 0 if mode == "nn" else 1

    def row_body(refs, product):
        r_refs = refs[:n_r]
        f_refs = refs[n_r:n_r + n_f]
        refs = refs[n_r + n_f + n_b:]
        ro_refs = refs[:n_ro]
        ao_refs = refs[n_ro:n_ro + len(acc_outs)]
        r_vals = [r[...].astype(F32) for r in r_refs]
        if product is not None:
            r_vals.insert(pos, product)
        f_vals = [f[...].astype(F32) for f in f_refs]
        ro, ao = body_fn(r_vals, f_vals)
        for ref, v in zip(ro_refs, ro):
            ref[...] = v.astype(ref.dtype)
        if ao_refs:
            @pl.when(pl.program_id(0) == 0)
            def _():
                for ref in ao_refs:
                    ref[...] = jnp.zeros(ref.shape, F32)
            for ref, v in zip(ao_refs, ao):
                ref[...] += v.reshape(ref.shape)

    def body(*refs):
        if mm is None:
            return row_body(refs, None)
        a_ref, b_ref, rest, acc_ref = refs[0], refs[1], refs[2:-1], refs[-1]
        k = pl.program_id(1)
        a_tile = a_ref[...] if a_pro is None else a_pro(a_ref[...])
        part = lax.dot_general(a_tile, b_ref[...], (((1,), (b_contract,)), ((), ())),
                               preferred_element_type=F32)
        if nk == 1:
            return row_body(rest, part)

        @pl.when(k == 0)
        def _():
            acc_ref[...] = part

        @pl.when((k > 0) & (k < nk - 1))
        def _():
            acc_ref[...] += part

        @pl.when(k == nk - 1)
        def _():
            row_body(rest, acc_ref[...] + part)

    def full_spec(shape):
        nd = len(shape)
        return pl.BlockSpec(tuple(shape), lambda i, *_: (0,) * nd)

    def row_spec(w, off):
        return pl.BlockSpec((tm, w), functools.partial(lambda i, *_, o: (i, o), o=off // w))

    in_specs = []
    if mm is not None:
        in_specs.append(pl.BlockSpec((tm, tk), lambda i, k: (i, k)))
        in_specs.append(pl.BlockSpec((tk, N), lambda i, k: (k, 0)) if mode == "nn" else
                        pl.BlockSpec((N, tk), lambda i, k: (0, k)))
    in_specs += [row_spec(w, off) for (_, w, off) in rows]
    in_specs += [full_spec(f.shape) for f in fulls]
    in_specs += [pl.BlockSpec(memory_space=pl.ANY)] * n_b
    out_specs, out_shape = [], []
    for ro in row_outs:
        if len(ro) == 3:
            buf, w, off = ro
            out_specs.append(row_spec(w, off))
            out_shape.append(jax.ShapeDtypeStruct(buf.shape, buf.dtype))
        else:
            w, dt = ro
            out_specs.append(row_spec(w, 0))
            out_shape.append(jax.ShapeDtypeStruct((T, w), dt))
    out_specs += [full_spec(s) for s in acc_outs]
    out_shape += [jax.ShapeDtypeStruct(tuple(s), F32) for s in acc_outs]
    aliases = {n_mm + n_r + n_f + b: k for b, (k, _) in enumerate(into)}
    return pl.pallas_call(
        body, name=name, grid=(T // tm,) if mm is None else (T // tm, nk),
        out_shape=out_shape, in_specs=in_specs, out_specs=out_specs,
        scratch_shapes=[] if mm is None else [pltpu.VMEM((tm, N) if nk > 1 else (8, 128), F32)],
        input_output_aliases=aliases,
        compiler_params=_params(("arbitrary",) if mm is None else ("arbitrary", "arbitrary")),
    )(*([] if mm is None else [a, b]), *[r[0] for r in rows], *fulls, *[ro[0] for _, ro in into])


def fwd_body(fn):
    return lambda r, f: (fn(*r, *f), ())


def bwd_body(fn, n_rows):
    def body(r, f):
        ins, cots = r[:n_rows], r[n_rows:]
        _, vjp = jax.vjp(fn, *ins, *f)
        g = vjp(tuple(cots))
        return g[:n_rows], g[n_rows:]
    return body


def whole_call(fn, ins, outs, *, name):
    n_in = len(ins)

    def body(*refs):
        res = fn(*[r[...] for r in refs[:n_in]])
        for ref, v in zip(refs[n_in:], res):
            ref[...] = v.astype(ref.dtype)

    return pl.pallas_call(
        body, name=name,
        out_shape=[jax.ShapeDtypeStruct(tuple(s), dt) for (s, dt) in outs],
        compiler_params=_params(),
    )(*ins)


def fn_modulate(x, sc, sh):
    return (_rms(x) * (1.0 + sc) + sh,)


def fn_sgu(u, v, nw, ws, bs):
    ug = jax.nn.gelu(u)
    vn = _rms(jax.nn.gelu(v)) * nw
    ri = lax.broadcasted_iota(jnp.int32, (Q, Q), 0)
    ci = lax.broadcasted_iota(jnp.int32, (Q, Q), 1)
    causal = ri >= ci
    chunks = []
    for n in range(u.shape[0] // Q):
        vc = vn[n * Q:(n + 1) * Q]
        cols = [dot_nn(jnp.where(causal, ws[g], 0.0), vc[:, g * Q:(g + 1) * Q]) + bs[g]
                for g in range(GM_GROUPS)]
        chunks.append(jnp.concatenate(cols, axis=1))
    sv = chunks[0] if len(chunks) == 1 else jnp.concatenate(chunks, axis=0)
    return (ug * sv,)


def fn_mix(ga, gb, pa, pb):
    return (jax.nn.sigmoid(ga) * pa + jax.nn.sigmoid(gb) * pb,)


def fn_res_modulate(x, o, g1, sc2, sh2):
    x1 = x + g1 * o
    return x1, _rms(x1) * (1.0 + sc2) + sh2


def relu2_tile(f):
    return jnp.square(jnp.maximum(f.astype(F32), 0.0)).astype(BF16)


def relu2_grad_tile(dact, f):
    return dact * (2.0 * jnp.maximum(f.astype(F32), 0.0))


def final_body(r, f):
    x1, gf, tgt = r
    g2, fnw = f

    def loss_fn(x1, gf, g2, fnw):
        y = _rms(x1 + g2 * gf) * fnw
        row = 0.5 * jnp.mean(jnp.square(y - tgt), axis=-1, keepdims=True)
        return jnp.sum(row, axis=0, keepdims=True)

    l, vjp = jax.vjp(loss_fn, x1, gf, g2, fnw)
    dx1, dgf, dg2, dfnw = vjp(jnp.ones((1, 1), F32))
    return (dx1, dgf), (jnp.broadcast_to(l, (1, 128)), dg2, dfnw)


def grad_x_body(r, f):
    x, dh, dxa = r
    _, vjp = jax.vjp(fn_modulate, x, *f)
    dx, dsc, dsh = vjp((dh,))
    return (dx + dxa,), (dsc, dsh)


CONV_CW = 128
CONV_PAD = 8


def _conv_pre(xp, w_ref, b_ref, r0, R):
    acc = b_ref[...] + w_ref[0:1, :] * xp[r0 + CONV_PAD - 3:r0 + CONV_PAD - 3 + R, :]
    for k in range(1, CONV_K):
        s = r0 + CONV_PAD - 3 + k
        acc = acc + w_ref[k:k + 1, :] * xp[s:s + R, :]
    return acc


def conv_fwd(proj, conv_w, conv_b):
    T = proj.shape[0]
    R = min(512, T)

    def body(x_ref, w_ref, b_ref, o_ref, xp):
        xp[0:CONV_PAD, :] = jnp.zeros((CONV_PAD, CONV_CW), F32)
        xp[CONV_PAD:CONV_PAD + T, :] = x_ref[...].astype(F32)
        for r0 in range(0, T, R):
            pre = _conv_pre(xp, w_ref, b_ref, r0, R)
            o_ref[r0:r0 + R, :] = (pre * jax.nn.sigmoid(pre)).astype(o_ref.dtype)

    return pl.pallas_call(
        body, name="conv_fwd", grid=(CONV_DIM // CONV_CW,),
        out_shape=jax.ShapeDtypeStruct((T, CONV_DIM), BF16),
        in_specs=[pl.BlockSpec((T, CONV_CW), lambda j: (0, P_XBC // CONV_CW + j)),
                  pl.BlockSpec((CONV_K, CONV_CW), lambda j: (0, j)),
                  pl.BlockSpec((1, CONV_CW), lambda j: (0, j))],
        out_specs=pl.BlockSpec((T, CONV_CW), lambda j: (0, j)),
        scratch_shapes=[pltpu.VMEM((T + CONV_PAD, CONV_CW), F32)],
        compiler_params=_params(("parallel",)),
    )(proj, conv_w, conv_b)


def conv_bwd(proj, dact, col0, conv_w, conv_b, dproj, *, name):
    T = proj.shape[0]
    R = min(512, T)
    nb = dact.shape[1] // CONV_CW
    c0 = col0 // CONV_CW
    x0 = (P_XBC + col0) // CONV_CW

    def body(x_ref, d_ref, w_ref, b_ref, _, dx_ref, dw_ref, db_ref, xp, dp):
        xp[0:CONV_PAD, :] = jnp.zeros((CONV_PAD, CONV_CW), F32)
        xp[CONV_PAD:CONV_PAD + T, :] = x_ref[...].astype(F32)
        dp[T:T + CONV_PAD, :] = jnp.zeros((CONV_PAD, CONV_CW), F32)
        dws = [jnp.zeros((1, CONV_CW), F32) for _ in range(CONV_K)]
        db = jnp.zeros((1, CONV_CW), F32)
        for r0 in range(0, T, R):
            pre = _conv_pre(xp, w_ref, b_ref, r0, R)
            s = jax.nn.sigmoid(pre)
            dpre = d_ref[r0:r0 + R, :].astype(F32) * (s * (1.0 + pre * (1.0 - s)))
            dp[r0:r0 + R, :] = dpre
            db = db + jnp.sum(dpre, axis=0, keepdims=True)
            for k in range(CONV_K):
                st = r0 + CONV_PAD - 3 + k
                dws[k] = dws[k] + jnp.sum(dpre * xp[st:st + R, :], axis=0, keepdims=True)
        for r0 in range(0, T, R):
            acc = w_ref[0:1, :] * dp[r0 + 3:r0 + 3 + R, :]
            for k in range(1, CONV_K):
                acc = acc + w_ref[k:k + 1, :] * dp[r0 + 3 - k:r0 + 3 - k + R, :]
            dx_ref[r0:r0 + R, :] = acc.astype(dx_ref.dtype)
        for k in range(CONV_K):
            dw_ref[k:k + 1, :] = dws[k]
        db_ref[...] = db

    return pl.pallas_call(
        body, name=name, grid=(nb,),
        out_shape=[jax.ShapeDtypeStruct(dproj.shape, dproj.dtype),
                   jax.ShapeDtypeStruct((CONV_K, nb * CONV_CW), F32),
                   jax.ShapeDtypeStruct((1, nb * CONV_CW), F32)],
        in_specs=[pl.BlockSpec((T, CONV_CW), lambda j: (0, x0 + j)),
                  pl.BlockSpec((T, CONV_CW), lambda j: (0, j)),
                  pl.BlockSpec((CONV_K, CONV_CW), lambda j: (0, c0 + j)),
                  pl.BlockSpec((1, CONV_CW), lambda j: (0, c0 + j)),
                  pl.BlockSpec(memory_space=pl.ANY)],
        out_specs=[pl.BlockSpec((T, CONV_CW), lambda j: (0, x0 + j)),
                   pl.BlockSpec((CONV_K, CONV_CW), lambda j: (0, j)),
                   pl.BlockSpec((1, CONV_CW), lambda j: (0, j))],
        scratch_shapes=[pltpu.VMEM((T + CONV_PAD, CONV_CW), F32),
                        pltpu.VMEM((T + CONV_PAD, CONV_CW), F32)],
        input_output_aliases={4: 0},
        compiler_params=_params(("parallel",)),
    )(proj, dact, conv_w, conv_b, dproj)


def _split3(a):
    hi = a.astype(BF16)
    r = a - hi.astype(F32)
    mid = r.astype(BF16)
    return hi, mid, (r - mid.astype(F32)).astype(BF16)


def _dg3(a, m, ca, cm, a_first):
    dims = (((ca,), (cm,)), ((), ())) if a_first else (((cm,), (ca,)), ((), ()))
    out = None
    for p in _split3(a):
        t = lax.dot_general(p, m, dims, preferred_element_type=F32) if a_first else \
            lax.dot_general(m, p, dims, preferred_element_type=F32)
        out = t if out is None else out + t
    return out


@jax.custom_vjp
def exact_right(a, m):
    return _dg3(a, m, 1, 0, True)


@jax.custom_vjp
def exact_left(m, a):
    return _dg3(a, m, 0, 1, False)


exact_right.defvjp(lambda a, m: (exact_right(a, m), m),
                   lambda m, g: (dot_nt(g, m), jnp.zeros_like(m)))
exact_left.defvjp(lambda m, a: (exact_left(m, a), m),
                  lambda m, g: (jnp.zeros_like(m), _dg3(g, m, 0, 0, False)))


def ssd_step(lane0, state, x, z, dtr, Bm, Cm, dtb, alog, dsk, nw):
    def iota(shape, dim):
        return lax.broadcasted_iota(jnp.int32, shape, dim)

    def one_hot(mask):
        return mask.astype(F32).astype(BF16)

    causal = iota((Q, Q), 0) >= iota((Q, Q), 1)
    eye = iota((Q, Q), 0) == iota((Q, Q), 1)
    lane = iota((1, 128), 1)
    colh = lax.shift_right_logical(iota((1, SSM_GW), 1), 6)
    to_cols = one_hot(iota((128, SSM_GW), 0) == lane0 + colh)

    dt_all = _softplus(dtr + dtb)
    a_all = dt_all * (-jnp.exp(alog))
    cum_all = exact_left(one_hot(causal), a_all)
    both = exact_right(jnp.concatenate([dt_all, cum_all], axis=0), to_cols)
    dt_f, cum_f = both[:Q], both[Q:]
    last_f = jnp.sum(jnp.where(iota((Q, 1), 0) == Q - 1, cum_f, 0.0), axis=0, keepdims=True)
    dsk_f = jnp.zeros((1, SSM_GW), F32)
    for h in range(SSM_HPG):
        dsk_f = jnp.where(colh == h, dsk[h], dsk_f)

    xdt = x * dt_f
    cb = dot_nt(Cm, Bm)
    ms, rhs = [], []
    for h in range(SSM_HPG):
        ch = jnp.sum(jnp.where(lane == lane0 + h, cum_all, 0.0), axis=1, keepdims=True)
        ch_t = jnp.sum(jnp.where(eye, ch, 0.0), axis=0, keepdims=True)
        ms.append(cb * jnp.exp(jnp.where(causal, ch - ch_t, -1e30)))
        rhs.append(jnp.where(colh == h, xdt, 0.0))
    y = dot_nn(jnp.concatenate(ms, axis=1), jnp.concatenate(rhs, axis=0))
    y = y + dot_nn(Cm, state) * jnp.exp(cum_f) + x * dsk_f
    new_state = state * jnp.exp(last_f) + dot_tn(Bm, xdt * jnp.exp(last_f - cum_f))
    gated = y * (z * jax.nn.sigmoid(z))
    return new_state, _rms(gated) * nw


SSD_GPS = 4
_XW = SSD_GPS * SSM_GW
_BW = SSD_GPS * 128


def _ssd_in_specs(rev, nc):
    def n_of(n):
        return nc - 1 - n if rev else n
    return [
        pl.BlockSpec((Q, _XW), lambda g, n: (n_of(n), g)),
        pl.BlockSpec((Q, _BW), lambda g, n: (n_of(n), SSM_INNER // _BW + g)),
        pl.BlockSpec((Q, _BW), lambda g, n: (n_of(n), (SSM_INNER + SSM_GROUPS * 128) // _BW + g)),
        pl.BlockSpec((Q, _XW), lambda g, n: (n_of(n), P_Z // _XW + g)),
        pl.BlockSpec((Q, 128), lambda g, n: (n_of(n), 0)),
        pl.BlockSpec((1, 128), lambda g, n: (0, 0)),
        pl.BlockSpec((1, 128), lambda g, n: (0, 0)),
        pl.BlockSpec((SSD_GPS, SSM_HPG, 1, 1), lambda g, n: (g, 0, 0, 0)),
        pl.BlockSpec((1, _XW), lambda g, n: (0, g)),
    ]


def _ssd_group_inputs(gi, x_ref, b_ref, c_ref, z_ref, dt_ref, dtb_ref, al_ref, dk_ref, nw_ref):
    xs = slice(gi * SSM_GW, (gi + 1) * SSM_GW)
    bs = slice(gi * 128, (gi + 1) * 128)
    return (x_ref[:, xs].astype(F32), z_ref[:, xs].astype(F32), dt_ref[...],
            b_ref[:, bs], c_ref[:, bs],
            dtb_ref[...], al_ref[...], dk_ref[gi], nw_ref[:, xs])


def ssd_fwd(xact, proj, dtg, dtb, alog, dsk, nw, comm):
    T = xact.shape[0]
    nc = T // Q
    nx = len(comm)
    ng = SSM_GROUPS // SSD_GPS
    any_spec = pl.BlockSpec(memory_space=pl.ANY)

    def body(*refs):
        in_refs, src_refs = refs[:9], refs[9:9 + nx]
        yb_ref, st_ref = refs[9 + nx:11 + nx]
        out_refs, state, sems = refs[11 + nx:11 + 2 * nx], refs[11 + 2 * nx], refs[12 + 2 * nx:]
        g, n = pl.program_id(0), pl.program_id(1)
        exchange_start(comm, src_refs, out_refs, sems, (g == 0) & (n == 0))

        @pl.when(n == 0)
        def _():
            state[...] = jnp.zeros(state.shape, F32)

        for gi in range(SSD_GPS):
            lane0 = SSM_HPG * (SSD_GPS * pl.program_id(0) + gi)
            s = state[gi]
            st_ref[gi, 0] = s
            new_s, yb = ssd_step(lane0, s, *_ssd_group_inputs(gi, *in_refs))
            state[gi] = new_s
            yb_ref[:, gi * SSM_GW:(gi + 1) * SSM_GW] = yb.astype(yb_ref.dtype)

        exchange_finish(comm, src_refs, out_refs, sems, (g == ng - 1) & (n == nc - 1))

    return pl.pallas_call(
        body, name="ssd_fwd", grid=(ng, nc),
        out_shape=[jax.ShapeDtypeStruct((T, SSM_INNER), BF16),
                   jax.ShapeDtypeStruct((SSM_GROUPS, nc, 128, SSM_GW), F32)] + exchange_out_shapes(comm),
        in_specs=_ssd_in_specs(False, nc) + [any_spec] * nx,
        out_specs=[pl.BlockSpec((Q, _XW), lambda g, n: (n, g)),
                   pl.BlockSpec((SSD_GPS, 1, 128, SSM_GW), lambda g, n: (g, n, 0, 0))] + [any_spec] * nx,
        scratch_shapes=[pltpu.VMEM((SSD_GPS, 128, SSM_GW), F32)] + exchange_semaphores(comm),
        compiler_params=_params(("arbitrary", "arbitrary")),
    )(xact, xact, xact, proj, dtg, dtb, alog, dsk, nw, *[it[0] for it in comm])


def ssd_bwd(xact, proj, dtg, dtb, alog, dsk, nw, states, dyb, dproj, comm):
    T = xact.shape[0]
    nc = T // Q

    nx = len(comm)
    ng = SSM_GROUPS // SSD_GPS
    any_spec = pl.BlockSpec(memory_space=pl.ANY)

    def body(*refs):
        in_refs, (st_ref, dy_ref, _) = refs[:9], refs[9:12]
        src_refs, refs = refs[12:12 + nx], refs[12 + nx:]
        dx_ref, db_ref, dc_ref, dz_ref, ddt_ref, ddtb_ref, dal_ref, ddk_ref, dnw_ref = refs[:9]
        out_refs, dstate, sems = refs[9:9 + nx], refs[9 + nx], refs[10 + nx:]
        exchange_start(comm, src_refs, out_refs, sems, (pl.program_id(0) == 0) & (pl.program_id(1) == 0))

        @pl.when(pl.program_id(1) == 0)
        def _():
            dstate[...] = jnp.zeros(dstate.shape, F32)
            ddtb_ref[...] = jnp.zeros(ddtb_ref.shape, F32)
            dal_ref[...] = jnp.zeros(dal_ref.shape, F32)
            ddk_ref[...] = jnp.zeros(ddk_ref.shape, F32)
            dnw_ref[...] = jnp.zeros(dnw_ref.shape, F32)

        for gi in range(SSD_GPS):
            xs = slice(gi * SSM_GW, (gi + 1) * SSM_GW)
            bs = slice(gi * 128, (gi + 1) * 128)
            lane0 = SSM_HPG * (SSD_GPS * pl.program_id(0) + gi)
            ins = (st_ref[gi, 0],) + _ssd_group_inputs(gi, *in_refs)
            _, vjp = jax.vjp(functools.partial(ssd_step, lane0), *ins)
            ds, dx, dz, ddt, dbm, dcm, ddtb, dal, ddk, dnw = vjp((dstate[gi], dy_ref[:, xs].astype(F32)))
            dstate[gi] = ds
            dx_ref[:, xs] = dx.astype(dx_ref.dtype)
            db_ref[:, bs] = dbm.astype(db_ref.dtype)
            dc_ref[:, bs] = dcm.astype(dc_ref.dtype)
            dz_ref[:, xs] = dz.astype(dz_ref.dtype)
            ddt_ref[gi] = ddt
            ddtb_ref[gi] += ddtb
            dal_ref[gi] += dal
            ddk_ref[gi] += ddk
            dnw_ref[:, xs] += dnw

        exchange_finish(comm, src_refs, out_refs, sems,
                        (pl.program_id(0) == ng - 1) & (pl.program_id(1) == nc - 1))

    rev = lambda n: nc - 1 - n
    row_shape = jax.ShapeDtypeStruct((SSM_GROUPS, 1, 128), F32)
    row_spec = pl.BlockSpec((SSD_GPS, 1, 128), lambda g, n: (g, 0, 0))
    return pl.pallas_call(
        body, name="ssd_bwd", grid=(ng, nc),
        out_shape=[jax.ShapeDtypeStruct((T, SSM_INNER), BF16),
                   jax.ShapeDtypeStruct((T, SSM_GROUPS * 128), BF16),
                   jax.ShapeDtypeStruct((T, SSM_GROUPS * 128), BF16),
                   jax.ShapeDtypeStruct(dproj.shape, dproj.dtype),
                   jax.ShapeDtypeStruct((SSM_GROUPS, T, 128), F32),
                   row_shape, row_shape,
                   jax.ShapeDtypeStruct((SSM_GROUPS, SSM_HPG, 1, 1), F32),
                   jax.ShapeDtypeStruct((1, SSM_INNER), F32)] + exchange_out_shapes(comm),
        in_specs=_ssd_in_specs(True, nc) + [
            pl.BlockSpec((SSD_GPS, 1, 128, SSM_GW), lambda g, n: (g, rev(n), 0, 0)),
            pl.BlockSpec((Q, _XW), lambda g, n: (rev(n), g)),
            any_spec] + [any_spec] * nx,
        out_specs=[pl.BlockSpec((Q, _XW), lambda g, n: (rev(n), g)),
                   pl.BlockSpec((Q, _BW), lambda g, n: (rev(n), g)),
                   pl.BlockSpec((Q, _BW), lambda g, n: (rev(n), g)),
                   pl.BlockSpec((Q, _XW), lambda g, n: (rev(n), P_Z // _XW + g)),
                   pl.BlockSpec((SSD_GPS, Q, 128), lambda g, n: (g, rev(n), 0)),
                   row_spec, row_spec,
                   pl.BlockSpec((SSD_GPS, SSM_HPG, 1, 1), lambda g, n: (g, 0, 0, 0)),
                   pl.BlockSpec((1, _XW), lambda g, n: (0, g))] + [any_spec] * nx,
        scratch_shapes=[pltpu.VMEM((SSD_GPS, 128, SSM_GW), F32)] + exchange_semaphores(comm),
        input_output_aliases={11: 3},
        compiler_params=_params(("arbitrary", "arbitrary")),
    )(xact, xact, xact, proj, dtg, dtb, alog, dsk, nw, states, dyb, dproj, *[it[0] for it in comm])


ADAMW_WHOLE_ELEMS = 256 * 1024


def adamw(w, g, m, v, *, name):
    shape = w.shape
    parts = g.shape != shape
    nd = len(shape)
    if w.size <= ADAMW_WHOLE_ELEMS:
        grid, tr = (1,), shape[-2]
    else:
        assert all(s == 1 for s in shape[:-2]) and shape[-2] % 256 == 0
        grid, tr = (shape[-2] // 256,), 256
    blk = tuple(shape[:-2]) + (tr, shape[-1])
    spec = pl.BlockSpec(blk, lambda i: (0,) * (nd - 2) + (i, 0))
    g_spec = pl.BlockSpec((N_DEV,) + blk[1:], lambda i: (0,) * (nd - 2) + (i, 0)) if parts else spec

    def body(w_ref, g_ref, m_ref, v_ref, go_ref, d_ref, nm_ref, nv_ref):
        if parts:
            g = g_ref[0:1].astype(F32)
            for j in range(1, N_DEV):
                g = g + g_ref[j:j + 1].astype(F32)
        else:
            g = g_ref[...]
        nm = ADAM_B1 * m_ref[...] + (1.0 - ADAM_B1) * g
        nv = ADAM_B2 * v_ref[...] + (1.0 - ADAM_B2) * jnp.square(g)
        m_hat = nm / (1.0 - ADAM_B1 ** ADAM_STEP)
        v_hat = nv / (1.0 - ADAM_B2 ** ADAM_STEP)
        go_ref[...] = g
        d_ref[...] = -ADAM_LR * (m_hat / (jnp.sqrt(v_hat) + ADAM_EPS) + ADAM_WD * w_ref[...])
        nm_ref[...] = nm
        nv_ref[...] = nv

    shp = jax.ShapeDtypeStruct(shape, F32)
    return pl.pallas_call(
        body, name=name, grid=grid,
        out_shape=[shp] * 4, in_specs=[spec, g_spec, spec, spec], out_specs=[spec] * 4,
        compiler_params=_params(("parallel",)),
    )(w, g, m, v)


def _pad_rows(a, rows):
    return jnp.pad(a, ((0, rows - a.shape[0]), (0, 0)))


WIN_W = 1408
N_IN = IN_WIDTH // N_DEV
_A6 = OFF_DT - 6 * N_IN
_C6 = 7 * N_IN - OFF_GA


def _win_offset(me):
    return jnp.where(me == 7, 124, 4 * me)


def _w_in_window(shard, me):
    rows = shard.shape[0]
    z = lambda n: jnp.zeros((rows, n), shard.dtype)
    a = lax.dynamic_update_slice(z(WIN_W), shard, (0, _win_offset(me)))
    b = jnp.concatenate([z(24), shard[:, :_A6], shard[:, _A6 + 32:], z(4), shard[:, _A6:_A6 + 32], z(96)], axis=1)
    return jnp.where(me == 6, b, a)


def _w_in_from_window(window, me):
    a = lax.dynamic_slice(window, (0, _win_offset(me)), (window.shape[0], N_IN))
    b = jnp.concatenate([window[:, 24:24 + _A6], window[:, 1280:1312], window[:, 24 + _A6:24 + _A6 + _C6]], axis=1)
    return jnp.where(me == 6, b, a)


def _w_all_from_windows(g):
    def merge_first(p, t):
        return jnp.concatenate([p[:, :128] + t, p[:, 128:]], axis=1)

    parts = [g[0][:, :1280]]
    for j in range(1, 6):
        parts.append(merge_first(g[j][:, :1280], g[j - 1][:, 1280:]))
    p6 = merge_first(g[6][:, :1280], g[5][:, 1280:])
    parts.append(jnp.concatenate([p6[:, :1152], p6[:, 1152:] + g[7][:, :128]], axis=1))
    parts.append(g[7][:, 128:])
    parts.append(g[6][:, 1280:])
    return jnp.concatenate(parts, axis=1)


def _windows_of_w_all(gw):
    wins = [gw[:, 1280 * j:1280 * j + WIN_W] for j in range(6)]
    wins.append(jnp.concatenate([gw[:, 7680:8960], gw[:, PROJ_W:]], axis=1))
    wins.append(gw[:, 8832:PROJ_W])
    return jnp.stack(wins)


def kernel(x, c, w_mod, b_mod, w_in, gm_norm_w, gm_ws, gm_bs, conv_w, conv_b, dt_bias, a_log, d_skip, ssm_norm_w, w_branch_gm, w_branch_ssm, w_out, w_ff1, w_ff2, final_norm_w, loss_target, m_w_mod, m_b_mod, m_w_in, m_gm_norm_w, m_gm_ws, m_gm_bs, m_conv_w, m_conv_b, m_dt_bias, m_a_log, m_d_skip, m_ssm_norm_w, m_w_branch_gm, m_w_branch_ssm, m_w_out, m_w_ff1, m_w_ff2, m_final_norm_w, v_w_mod, v_b_mod, v_w_in, v_gm_norm_w, v_gm_ws, v_gm_bs, v_conv_w, v_conv_b, v_dt_bias, v_a_log, v_d_skip, v_ssm_norm_w, v_w_branch_gm, v_w_branch_ssm, v_w_out, v_w_ff1, v_w_ff2, v_final_norm_w):
    T = x.shape[1]
    me = 4 * lax.axis_index("x") + 2 * lax.axis_index("y") + lax.axis_index("c")
    x2 = x[0]
    tgt = loss_target[0]
    n_in = IN_WIDTH // N_DEV
    n_mod = N_MOD * D // N_DEV
    n_cv = CONV_DIM // N_DEV

    c_all, conv_w_full = exchange(
        [(c.reshape(8, 128), _whole, (N_DEV, 8, 128), _slot),
         (conv_w[0], _whole, (N_DEV, CONV_K, n_cv), _slot)], name="gather_c_convw")
    c_all = c_all.reshape(N_DEV, D)
    conv_w_full = conv_w_full.transpose(1, 0, 2).reshape(CONV_K, CONV_DIM)

    win = _w_in_window(w_in[0].astype(BF16), me)
    gwin = gather_blocks_two_level(win, name="gather_w_in")
    late_weights = [
        (w_branch_gm[0].astype(BF16), _whole, (D, D), _rows(D // N_DEV)),
        (w_branch_ssm[0].astype(BF16), _whole, (SSM_INNER, D), _rows(SSM_INNER // N_DEV)),
        (w_out[0].astype(BF16), _whole, (D, D), _rows(D // N_DEV)),
        (w_ff1[0].astype(BF16), _whole, (D, D_FF), _cols(D_FF // N_DEV)),
        (w_ff2[0].astype(BF16), _whole, (D_FF, D), _rows(D_FF // N_DEV))]
    w_all = _w_all_from_windows(gwin)
    w_dt = w_all[:, PROJ_W:]

    c_pad = _pad_rows(c_all, 128)
    b_mine = lax.dynamic_slice(b_mod, (0, me * n_mod), (1, n_mod))

    def mod_fn(cp, w, b):
        ca = cp * jax.nn.sigmoid(cp)
        return (jnp.dot(ca, w, precision=HIGHEST, preferred_element_type=F32) + b,)

    (mod_part,) = whole_call(mod_fn, [c_pad, w_mod[0], b_mine], [((128, n_mod), F32)], name="mod_fwd")
    gmod = gather_blocks(mod_part[:N_DEV], name="gather_mod")
    mod = lax.dynamic_index_in_dim(gmod, me, axis=1, keepdims=False).reshape(N_MOD, D)
    sh1, sc1, gt1, sh2, sc2, gt2 = [mod[i:i + 1] for i in range(N_MOD)]

    (h,) = rowwise_call(fwd_body(fn_modulate), [x2], [sc1, sh1], [(D, BF16)], [], tm=256, name="modulate1")
    proj = matmul(h, w_all, "nn", BF16, name="mm_proj", n=PROJ_W)
    dtg = matmul(h, w_dt, "nn", F32, name="mm_dt")
    ws = gm_ws[0]
    bs3 = gm_bs[0].reshape(GM_GROUPS, Q, 1)
    sgu_rows = [(proj, D, P_U), (proj, D, P_V)]
    (ya,) = rowwise_call(fwd_body(fn_sgu), sgu_rows, [gm_norm_w, ws, bs3], [(D, BF16)], [],
                         tm=256, name="sgu_fwd")
    xact = conv_fwd(proj, conv_w_full, conv_b)
    dtb4 = jnp.pad(dt_bias, ((0, 0), (0, 96)))
    alog4 = jnp.pad(a_log, ((0, 0), (0, 96)))
    dsk4 = d_skip.reshape(SSM_GROUPS, SSM_HPG, 1, 1)
    yb, states, w_gm_f, w_ssm_f, w_out_f, w_ff1_f, w_ff2_f = ssd_fwd(
        xact, proj, dtg, dtb4, alog4, dsk4, ssm_norm_w, late_weights)
    pa = matmul(ya, w_gm_f, "nn", F32, name="mm_branch_gm")
    gate_rows = [(proj, D, P_GA), (proj, D, P_GB)]
    mixed, pb = rowwise_call(
        lambda r, fl: (fn_mix(*r) + (r[3],), ()), gate_rows + [pa], [], [(D, BF16), (D, F32)], [],
        tm=FUSED_TM, name="branch_ssm_mix", mm=(yb, w_ssm_f, "nn", 3, None))
    x1, h2, o = rowwise_call(
        lambda r, fl: (fn_res_modulate(*r, *fl) + (r[1],), ()), [x2], [gt1, sc2, sh2],
        [(D, F32), (D, BF16), (D, F32)], [], tm=FUSED_TM, name="out_res_modulate2",
        mm=(mixed, w_out_f, "nn", 1, None))
    f = matmul(h2, w_ff1_f, "nn", BF16, name="mm_ff1")

    dx1, dgf, loss_v, dgt2, dfnw = rowwise_call(
        final_body, [x1, tgt], [gt2, final_norm_w.reshape(1, D)], [(D, F32), (D, BF16)],
        [(1, 128), (1, D), (1, D)], tm=FUSED_TM, name="ff2_loss_bwd", mm=(f, w_ff2_f, "nn", 1, relu2_tile))
    df = matmul(dgf, w_ff2_f, "nt", BF16, name="mm_ff2_dgrad", epi=relu2_grad_tile, epi_ins=(f,))
    gw_ff2 = matmul(f, dgf, "tn", BF16, name="mm_ff2_wgrad", tk=WGRAD_TK, a_pro=relu2_tile)
    gw_ff1 = matmul(h2, df, "tn", BF16, name="mm_ff1_wgrad", tk=WGRAD_TK)

    def res_mod_bwd(r, fl):
        xv, ov, dx1v, dh2v = r
        _, vjp = jax.vjp(fn_res_modulate, xv, ov, *fl)
        dxv, dov, dg1, dsc, dsh = vjp((dx1v, dh2v))
        return (dxv, dov), (dg1, dsc, dsh)

    dxa, do, dgt1, dsc2, dsh2 = rowwise_call(
        res_mod_bwd, [x2, o, dx1], [gt1, sc2, sh2], [(D, F32), (D, BF16)],
        [(1, D), (1, D), (1, D)], tm=FUSED_TM, name="ff1_dgrad_res_modulate2_bwd",
        mm=(df, w_ff1_f, "nt", 3, None))
    gw_out = matmul(mixed, do, "tn", BF16, name="mm_out_wgrad", tk=WGRAD_TK)
    dproj = lax.empty((T, ALL_W), BF16)

    def mix_bwd(r, fl):
        dga, dgb, dpa, dpb = bwd_body(fn_mix, 4)(r, fl)[0]
        return (jnp.concatenate([dga, dgb], axis=1), dpa, dpb), ()

    dproj, dpa, dpb = rowwise_call(
        mix_bwd, gate_rows + [pa, pb], [], [(dproj, 2 * D, P_GA), (D, BF16), (D, BF16)], [],
        tm=FUSED_TM, name="out_dgrad_mix_bwd", mm=(do, w_out_f, "nt", 4, None))
    gw_gm = matmul(ya, dpa, "tn", BF16, name="mm_branch_gm_wgrad", tk=WGRAD_TK)
    dyb = matmul(dpb, w_ssm_f, "nt", BF16, name="mm_branch_ssm_dgrad")
    gw_ssm = matmul(yb, dpb, "tn", BF16, name="mm_branch_ssm_wgrad", tk=WGRAD_TK)

    def sgu_bwd(r, fl):
        (du, dv), acc = bwd_body(fn_sgu, 2)(r, fl)
        return (jnp.concatenate([du, dv], axis=1),), acc

    dproj, dgnw, dws, dbs = rowwise_call(
        sgu_bwd, sgu_rows, [gm_norm_w, ws, bs3], [(dproj, 2 * D, P_U)],
        [(1, D), (GM_GROUPS, Q, Q), (GM_GROUPS, Q, 1)], tm=FUSED_TM, name="branch_gm_dgrad_sgu_bwd",
        mm=(dpa, w_gm_f, "nt", 2, None))
    early_grads = [
        (gw_gm, _rows(D // N_DEV), (N_DEV, D // N_DEV, D), _slot),
        (gw_ssm, _rows(SSM_INNER // N_DEV), (N_DEV, SSM_INNER // N_DEV, D), _slot),
        (gw_out, _rows(D // N_DEV), (N_DEV, D // N_DEV, D), _slot),
        (gw_ff1, _cols(D_FF // N_DEV), (N_DEV, D, D_FF // N_DEV), _slot),
        (gw_ff2, _rows(D_FF // N_DEV), (N_DEV, D_FF // N_DEV, D), _slot),
        (_pack_rows([dgnw, dws, dbs], EARLY_ROWS), _whole, (N_DEV, sum(EARLY_ROWS), 128), _slot)]
    (dxs, dbm, dcm, dproj, ddt8, ddtb, dalog, ddsk, dsnw,
     r_gm, r_ssm, r_out, r_ff1, r_ff2, early_all) = ssd_bwd(
        xact, proj, dtg, dtb4, alog4, dsk4, ssm_norm_w, states, dyb, dproj, early_grads)
    dconv_w, dconv_b = [], []
    for nm, dact_part, col0 in (("xs", dxs, 0), ("b", dbm, SSM_INNER), ("c", dcm, SSM_INNER + SSM_GROUPS * 128)):
        dproj, dcw, dcb = conv_bwd(proj, dact_part, col0, conv_w_full, conv_b, dproj, name="conv_bwd_" + nm)
        dconv_w.append(dcw)
        dconv_b.append(dcb)
    dconv_w = jnp.concatenate(dconv_w, axis=1)
    dconv_b = jnp.concatenate(dconv_b, axis=1)
    dproj = dproj.at[:, PROJ_W:].set(jnp.sum(ddt8, axis=0).astype(BF16))
    gw_all = matmul(h, dproj, "tn", BF16, name="mm_in_wgrad", tn=1152, tk=WGRAD_TK)
    mid_pack = _pack_rows([dconv_w, dconv_b, jnp.sum(ddtb, axis=0), jnp.sum(dalog, axis=0), ddsk, dsnw, dfnw,
                           jnp.concatenate([dgt1, dsh2, dsc2, dgt2], axis=0)], MID_ROWS)
    dh, r_in, mid_all = matmul(
        dproj, w_all, "nt", F32, name="mm_in_dgrad", tk=1152,
        comm=[(_windows_of_w_all(gw_all), _slot, (N_DEV, D, WIN_W), _slot),
              (mid_pack, _whole, (N_DEV, sum(MID_ROWS), 128), _slot)])
    grad_x, dsc1, dsh1 = rowwise_call(grad_x_body, [x2, dh, dxa], [sc1, sh1], [(D, F32)],
                                      [(1, D), (1, D)], tm=256, name="modulate1_bwd")

    g_w_in = _w_in_from_window(sum_devices(r_in, tr=256, name="sum_w_in_grads"), me).reshape(1, D, n_in)

    late_all = gather_blocks(_pack_rows([dsh1, dsc1], LATE_ROWS), name="gather_dmod1")
    s_early = _unpack_rows(sum_devices(early_all, tr=early_all.shape[1], name="sum_small_early"), EARLY_ROWS)
    s_mid = _unpack_rows(sum_devices(mid_all, tr=mid_all.shape[1], name="sum_small_mid"), MID_ROWS)
    s_late = _unpack_rows(sum_devices(late_all, tr=late_all.shape[1], name="sum_small_late"), LATE_ROWS)
    g_gm_norm_w = s_early[0][:D].reshape(1, D)
    g_gm_ws = s_early[1].reshape(GM_GROUPS * Q, Q)
    g_gm_bs = s_early[2][:GM_GROUPS * Q].reshape(GM_GROUPS, Q)
    g_conv_w_full = s_mid[0].reshape(CONV_K, CONV_DIM)
    g_conv_w = lax.dynamic_slice(g_conv_w_full, (0, me * n_cv), (CONV_K, n_cv))
    g_conv_b = s_mid[1].reshape(1, CONV_DIM)
    g_dt_bias = s_mid[2][:32].reshape(1, 32)
    g_a_log = s_mid[3][:32].reshape(1, 32)
    g_d_skip = s_mid[4][:32].reshape(1, 32)
    g_ssm_norm_w = s_mid[5].reshape(1, SSM_INNER)
    g_final_norm_w = s_mid[6][:D].reshape(1, D)
    g_b_mod = jnp.concatenate([s_late[0][:D], s_late[1][:D], s_mid[7]]).reshape(1, N_MOD * D)

    dmod_all = jnp.concatenate(
        [late_all.reshape(N_DEV, -1)[:, :2 * D],
         mid_all[:, sum(MID_ROWS[:7]):].reshape(N_DEV, 4 * D)], axis=1)
    dmod_mine = _pad_rows(lax.dynamic_slice(dmod_all, (0, me * n_mod), (N_DEV, n_mod)), 128)

    def wmod_grad_fn(cp, dm):
        ca = cp * jax.nn.sigmoid(cp)
        return (lax.dot_general(ca, dm, (((0,), (0,)), ((), ())), precision=HIGHEST,
                                preferred_element_type=F32),)

    (g_w_mod,) = whole_call(wmod_grad_fn, [c_pad, dmod_mine], [((D, n_mod), F32)], name="w_mod_grad")

    upd = {}

    def step(name, w, g, m, v, parts=False):
        upd[name] = adamw(w, g if parts else g.reshape(w.shape), m, v, name="adamw_" + name)

    step("w_mod", w_mod, g_w_mod, m_w_mod, v_w_mod)
    step("b_mod", b_mod, g_b_mod, m_b_mod, v_b_mod)
    step("w_in", w_in, g_w_in, m_w_in, v_w_in)
    step("gm_norm_w", gm_norm_w, g_gm_norm_w, m_gm_norm_w, v_gm_norm_w)
    step("gm_ws", gm_ws, g_gm_ws, m_gm_ws, v_gm_ws)
    step("gm_bs", gm_bs, g_gm_bs, m_gm_bs, v_gm_bs)
    step("conv_w", conv_w, g_conv_w, m_conv_w, v_conv_w)
    step("conv_b", conv_b, g_conv_b, m_conv_b, v_conv_b)
    step("dt_bias", dt_bias, g_dt_bias, m_dt_bias, v_dt_bias)
    step("a_log", a_log, g_a_log, m_a_log, v_a_log)
    step("d_skip", d_skip, g_d_skip, m_d_skip, v_d_skip)
    step("ssm_norm_w", ssm_norm_w, g_ssm_norm_w, m_ssm_norm_w, v_ssm_norm_w)
    step("w_branch_gm", w_branch_gm, r_gm, m_w_branch_gm, v_w_branch_gm, parts=True)
    step("w_branch_ssm", w_branch_ssm, r_ssm, m_w_branch_ssm, v_w_branch_ssm, parts=True)
    step("w_out", w_out, r_out, m_w_out, v_w_out, parts=True)
    step("w_ff1", w_ff1, r_ff1, m_w_ff1, v_w_ff1, parts=True)
    step("w_ff2", w_ff2, r_ff2, m_w_ff2, v_w_ff2, parts=True)
    step("final_norm_w", final_norm_w.reshape(1, D), g_final_norm_w, m_final_norm_w.reshape(1, D),
         v_final_norm_w.reshape(1, D))
    upd["final_norm_w"] = tuple(a.reshape(D) for a in upd["final_norm_w"])

    loss = lax.psum(loss_v[0, 0], ("x", "y", "c"))
    order = ["w_mod", "b_mod", "w_in", "gm_norm_w", "gm_ws", "gm_bs", "conv_w", "conv_b", "dt_bias", "a_log",
             "d_skip", "ssm_norm_w", "w_branch_gm", "w_branch_ssm", "w_out", "w_ff1", "w_ff2", "final_norm_w"]
    return (loss, grad_x.reshape(1, T, D),
            *[upd[n][0] for n in order], *[upd[n][1] for n in order],
            *[upd[n][2] for n in order], *[upd[n][3] for n in order])
```

```python
import functools

import jax
import jax.numpy as jnp
from jax import lax
from jax.experimental import pallas as pl
from jax.experimental.pallas import tpu as pltpu

F32 = jnp.float32
BF16 = jnp.bfloat16
MESH = pl.DeviceIdType.MESH
HIGHEST = lax.Precision.HIGHEST

N_DEV = 8
D = 1024
Q = 128
GM_GROUPS = 8
SSM_INNER = 2048
SSM_GROUPS = 8
SSM_HPG = 4
SSM_P = 64
SSM_GW = SSM_HPG * SSM_P
CONV_DIM = 4096
CONV_K = 4
D_FF = 4096
N_MOD = 6
EPS = 1e-6
IN_WIDTH = 10272
OFF_DT = 8192
OFF_GA = 8224
PROJ_W = 10240
ALL_W = 10368
P_U, P_V, P_Z, P_XBC, P_GA, P_GB = 0, 1024, 2048, 4096, 8192, 9216

ADAM_LR = 0.001
ADAM_B1 = 0.9
ADAM_B2 = 0.999
ADAM_EPS = 1e-08
ADAM_WD = 0.01
ADAM_STEP = 10

VMEM_LIMIT_BYTES = 48 * 1024 * 1024
FUSED_TM = 256
WGRAD_TK = 2048
EARLY_ROWS = (8, 1024, 8)
MID_ROWS = (128, 32, 8, 8, 8, 16, 8, 32)
LATE_ROWS = (8, 8)


def _pack_rows(arrs, rows):
    def rows128(a, r):
        a = a.reshape(-1)
        return jnp.pad(a, (0, r * 128 - a.shape[0])).reshape(r, 128)
    return jnp.concatenate([rows128(a, r) for a, r in zip(arrs, rows)], axis=0)


def _unpack_rows(s, rows):
    out, o = [], 0
    for r in rows:
        out.append(s[o:o + r].reshape(-1))
        o += r
    return out


def _params(sem=None):
    return pltpu.CompilerParams(dimension_semantics=sem, vmem_limit_bytes=VMEM_LIMIT_BYTES)


def _dg(a, b, ca, cb):
    return lax.dot_general(a.astype(BF16), b.astype(BF16), (((ca,), (cb,)), ((), ())),
                           preferred_element_type=F32)


@jax.custom_vjp
def dot_nn(a, b):
    return _dg(a, b, 1, 0)


@jax.custom_vjp
def dot_nt(a, b):
    return _dg(a, b, 1, 1)


@jax.custom_vjp
def dot_tn(a, b):
    return _dg(a, b, 0, 0)


def _like(ct, primal):
    return ct.astype(primal.dtype)


dot_nn.defvjp(lambda a, b: (dot_nn(a, b), (a, b)),
              lambda r, g: (_like(dot_nt(g, r[1]), r[0]), _like(dot_tn(r[0], g), r[1])))
dot_nt.defvjp(lambda a, b: (dot_nt(a, b), (a, b)),
              lambda r, g: (_like(dot_nn(g, r[1]), r[0]), _like(dot_tn(g, r[0]), r[1])))
dot_tn.defvjp(lambda a, b: (dot_tn(a, b), (a, b)),
              lambda r, g: (_like(dot_nt(r[1], g), r[0]), _like(dot_nn(r[0], g), r[1])))


def _rms(x):
    return x * lax.rsqrt(jnp.mean(x * x, axis=-1, keepdims=True) + EPS)


def _softplus(x):
    return jnp.maximum(x, 0.0) + jnp.log1p(jnp.exp(-jnp.abs(x)))


def _rows(n):
    return lambda ref, j: ref.at[pl.ds(pl.multiple_of(j * n, n), n)]


def _cols(n):
    return lambda ref, j: ref.at[:, pl.ds(pl.multiple_of(j * n, n), n)]


def _slot(ref, j):
    return ref.at[j]


def _whole(ref, j):
    return ref


def exchange(items, *, name):
    n = len(items)

    def body(*refs):
        exchange_in_body(items, refs[:n], refs[n:2 * n], refs[2 * n:], True, True)

    return pl.pallas_call(
        body, name=name,
        out_shape=exchange_out_shapes(items),
        in_specs=[pl.BlockSpec(memory_space=pl.ANY)] * n,
        out_specs=[pl.BlockSpec(memory_space=pl.ANY)] * n,
        scratch_shapes=exchange_semaphores(items),
    )(*[it[0] for it in items])


def exchange_out_shapes(items):
    return [jax.ShapeDtypeStruct(tuple(shape), src.dtype) for (src, _, shape, _) in items]


def exchange_semaphores(items):
    n = len(items)
    return [pltpu.SemaphoreType.DMA((n, N_DEV - 1)), pltpu.SemaphoreType.DMA((n, N_DEV - 1)),
            pltpu.SemaphoreType.DMA((n,))]


def _exchange_copies(items, src_refs, out_refs, sems):
    send_sems, recv_sems, local_sems = sems
    x = lax.axis_index("x")
    y = lax.axis_index("y")
    c = lax.axis_index("c")
    me = 4 * x + 2 * y + c
    local = [pltpu.make_async_copy(src_win(src_refs[i], me), dst_win(out_refs[i], me), local_sems.at[i])
             for i, (_, src_win, _, dst_win) in enumerate(items)]
    remote = []
    for i, (_, src_win, _, dst_win) in enumerate(items):
        for k in range(1, N_DEV):
            px = lax.rem(x + ((k >> 2) & 1), 2)
            py = lax.rem(y + ((k >> 1) & 1), 2)
            pc = lax.rem(c + (k & 1), 2)
            peer = 4 * px + 2 * py + pc
            remote.append(pltpu.make_async_remote_copy(
                src_ref=src_win(src_refs[i], peer), dst_ref=dst_win(out_refs[i], me),
                send_sem=send_sems.at[i, k - 1], recv_sem=recv_sems.at[i, k - 1],
                device_id=(px, py, pc), device_id_type=MESH))
    return local, remote


def _when(cond, fn):
    if cond is True:
        fn()
    else:
        pl.when(cond)(fn)


def exchange_start(items, src_refs, out_refs, sems, cond):
    def start():
        local, remote = _exchange_copies(items, src_refs, out_refs, sems)
        for cp in local + remote:
            cp.start()
    _when(cond, start)


def exchange_finish(items, src_refs, out_refs, sems, cond):
    def finish():
        local, remote = _exchange_copies(items, src_refs, out_refs, sems)
        for cp in remote:
            cp.wait_send()
        for cp in remote:
            cp.wait_recv()
        for cp in local:
            cp.wait()
    _when(cond, finish)


def exchange_in_body(items, src_refs, out_refs, sems, first, last):
    exchange_start(items, src_refs, out_refs, sems, first)
    exchange_finish(items, src_refs, out_refs, sems, last)


def gather_blocks(src, *, name):
    return exchange([(src, _whole, (N_DEV,) + src.shape, _slot)], name=name)[0]


def gather_blocks_two_level(src, *, name):
    def body(src_ref, out_ref, send_sems, recv_sems, local_sem):
        x = lax.axis_index("x")
        y = lax.axis_index("y")
        c = lax.axis_index("c")
        me, sibling = (x, y, c), (x, y, 1 - c)
        chips = [(1 - x, y), (x, 1 - y), (1 - x, 1 - y)]

        def slot(px, py, pc):
            return out_ref.at[4 * px + 2 * py + pc]

        def copy(k, block, to, src=None):
            return pltpu.make_async_remote_copy(
                src_ref=slot(*block) if src is None else src, dst_ref=slot(*block),
                send_sem=send_sems.at[k], recv_sem=recv_sems.at[k], device_id=to, device_id_type=MESH)

        mine = pltpu.make_async_copy(src_ref, slot(*me), local_sem)
        mine.start()
        first = [copy(0, me, sibling, src=src_ref)]
        first += [copy(1 + j, me, (*chip, c), src=src_ref) for j, chip in enumerate(chips)]
        for cp in first:
            cp.start()
        passed = [copy(4 + j, (*chip, c), sibling) for j, chip in enumerate(chips)]
        for j, chip in enumerate(chips):
            copy(1 + j, (*chip, c), me).wait_recv()
            passed[j].start()
        copy(0, sibling, me).wait_recv()
        for j, chip in enumerate(chips):
            copy(4 + j, (*chip, 1 - c), me).wait_recv()
        for cp in first + passed:
            cp.wait_send()
        mine.wait()

    return pl.pallas_call(
        body, name=name,
        out_shape=jax.ShapeDtypeStruct((N_DEV,) + src.shape, src.dtype),
        in_specs=[pl.BlockSpec(memory_space=pl.ANY)],
        out_specs=pl.BlockSpec(memory_space=pl.ANY),
        scratch_shapes=[pltpu.SemaphoreType.DMA((N_DEV - 1,)), pltpu.SemaphoreType.DMA((N_DEV - 1,)),
                        pltpu.SemaphoreType.DMA(())],
    )(src)


def sum_devices(g, *, tr, name):
    _, R, C = g.shape

    def body(g_ref, o_ref):
        acc = g_ref[0].astype(F32)
        for j in range(1, N_DEV):
            acc = acc + g_ref[j].astype(F32)
        o_ref[...] = acc

    return pl.pallas_call(
        body, name=name, grid=(R // tr,),
        out_shape=jax.ShapeDtypeStruct((R, C), F32),
        in_specs=[pl.BlockSpec((N_DEV, tr, C), lambda i: (0, i, 0))],
        out_specs=pl.BlockSpec((tr, C), lambda i: (i, 0)),
        compiler_params=_params(("parallel",)),
    )(g)


def matmul(a, b, mode, out_dtype, *, name, tm=1024, tn=1024, tk=1024, n=None, comm=None,
           a_pro=None, epi=None, epi_ins=()):
    if mode == "nn":
        (M, K), (K2, N) = a.shape, b.shape
    elif mode == "nt":
        (M, K), (N, K2) = a.shape, b.shape
    else:
        (K, M), (K2, N) = a.shape, b.shape
    assert K == K2
    N = N if n is None else n
    tm, tn, tk = min(tm, M), min(tn, N), min(tk, K)
    assert M % tm == 0 and N % tn == 0 and K % tk == 0, (name, M, N, K, tm, tn, tk)
    nk = K // tk
    if mode == "tn":
        a_spec = pl.BlockSpec((tk, tm), lambda i, j, k: (k, i))
    else:
        a_spec = pl.BlockSpec((tm, tk), lambda i, j, k: (i, k))
    if mode == "nt":
        b_spec = pl.BlockSpec((tn, tk), lambda i, j, k: (j, k))
    else:
        b_spec = pl.BlockSpec((tk, tn), lambda i, j, k: (k, j))
    dims = {"nn": (1, 0), "nt": (1, 1), "tn": (0, 0)}[mode]
    items = list(comm) if comm else []
    nx = len(items)
    ne = len(epi_ins)
    gm, gn = M // tm, N // tn
    any_spec = pl.BlockSpec(memory_space=pl.ANY)
    o_spec = pl.BlockSpec((tm, tn), lambda i, j, k: (i, j))

    def body(*refs):
        a_ref, b_ref, e_refs = refs[0], refs[1], refs[2:2 + ne]
        refs = refs[2 + ne:]
        src_refs, o_ref, out_refs = refs[:nx], refs[nx], refs[1 + nx:1 + 2 * nx]
        acc_ref, sems = refs[1 + 2 * nx], refs[2 + 2 * nx:]
        i, j, k = pl.program_id(0), pl.program_id(1), pl.program_id(2)
        if items:
            exchange_start(items, src_refs, out_refs, sems, (i == 0) & (j == 0) & (k == 0))
        a_tile = a_ref[...] if a_pro is None else a_pro(a_ref[...])
        part = lax.dot_general(a_tile, b_ref[...], (((dims[0],), (dims[1],)), ((), ())),
                               preferred_element_type=F32)

        def finish(acc):
            if epi is not None:
                acc = epi(acc, *[e[...] for e in e_refs])
            o_ref[...] = acc.astype(o_ref.dtype)

        if nk == 1:
            finish(part)
        else:
            @pl.when(k == 0)
            def _():
                acc_ref[...] = part

            @pl.when((k > 0) & (k < nk - 1))
            def _():
                acc_ref[...] += part

            @pl.when(k == nk - 1)
            def _():
                finish(acc_ref[...] + part)

        if items:
            exchange_finish(items, src_refs, out_refs, sems, (i == gm - 1) & (j == gn - 1) & (k == nk - 1))

    res = pl.pallas_call(
        body, name=name, grid=(gm, gn, nk),
        out_shape=[jax.ShapeDtypeStruct((M, N), out_dtype)] + exchange_out_shapes(items),
        in_specs=[a_spec, b_spec] + [o_spec] * ne + [any_spec] * nx,
        out_specs=[o_spec] + [any_spec] * nx,
        scratch_shapes=[pltpu.VMEM((tm, tn) if nk > 1 else (8, 128), F32)]
        + (exchange_semaphores(items) if items else []),
        compiler_params=_params(("arbitrary",) * 3 if items else ("parallel", "parallel", "arbitrary")),
    )(a, b, *epi_ins, *[it[0] for it in items])
    return res if items else res[0]


def rowwise_call(body_fn, rows, fulls, row_outs, acc_outs, *, tm, name, mm=None, tk=1024):
    rows = [r if isinstance(r, tuple) else (r, r.shape[1], 0) for r in rows]
    T = rows[0][0].shape[0]
    tm = min(tm, T)
    assert T % tm == 0
    n_r, n_f, n_ro = len(rows), len(fulls), len(row_outs)
    into = [(k, ro) for k, ro in enumerate(row_outs) if len(ro) == 3]
    n_b = len(into)
    n_mm, nk = 0, 1
    if mm is not None:
        a, b, mode, pos, a_pro = mm
        n_mm = 2
        K = a.shape[1]
        N = b.shape[1] if mode == "nn" else b.shape[0]
        tk = min(tk, K)
        assert K % tk == 0 and a.shape[0] == T
        nk = K // tk
        b_contract = 0 if mode == "nn" else 1

    def row_body(refs, product):
        r_refs = refs[:n_r]
        f_refs = refs[n_r:n_r + n_f]
        refs = refs[n_r + n_f + n_b:]
        ro_refs = refs[:n_ro]
        ao_refs = refs[n_ro:n_ro + len(acc_outs)]
        r_vals = [r[...].astype(F32) for r in r_refs]
        if product is not None:
            r_vals.insert(pos, product)
        f_vals = [f[...].astype(F32) for f in f_refs]
        ro, ao = body_fn(r_vals, f_vals)
        for ref, v in zip(ro_refs, ro):
            ref[...] = v.astype(ref.dtype)
        if ao_refs:
            @pl.when(pl.program_id(0) == 0)
            def _():
                for ref in ao_refs:
                    ref[...] = jnp.zeros(ref.shape, F32)
            for ref, v in zip(ao_refs, ao):
                ref[...] += v.reshape(ref.shape)

    def body(*refs):
        if mm is None:
            return row_body(refs, None)
        a_ref, b_ref, rest, acc_ref = refs[0], refs[1], refs[2:-1], refs[-1]
        k = pl.program_id(1)
        a_tile = a_ref[...] if a_pro is None else a_pro(a_ref[...])
        part = lax.dot_general(a_tile, b_ref[...], (((1,), (b_contract,)), ((), ())),
                               preferred_element_type=F32)
        if nk == 1:
            return row_body(rest, part)

        @pl.when(k == 0)
        def _():
            acc_ref[...] = part

        @pl.when((k > 0) & (k < nk - 1))
        def _():
            acc_ref[...] += part

        @pl.when(k == nk - 1)
        def _():
            row_body(rest, acc_ref[...] + part)

    def full_spec(shape):
        nd = len(shape)
        return pl.BlockSpec(tuple(shape), lambda i, *_: (0,) * nd)

    def row_spec(w, off):
        return pl.BlockSpec((tm, w), functools.partial(lambda i, *_, o: (i, o), o=off // w))

    in_specs = []
    if mm is not None:
        in_specs.append(pl.BlockSpec((tm, tk), lambda i, k: (i, k)))
        in_specs.append(pl.BlockSpec((tk, N), lambda i, k: (k, 0)) if mode == "nn" else
                        pl.BlockSpec((N, tk), lambda i, k: (0, k)))
    in_specs += [row_spec(w, off) for (_, w, off) in rows]
    in_specs += [full_spec(f.shape) for f in fulls]
    in_specs += [pl.BlockSpec(memory_space=pl.ANY)] * n_b
    out_specs, out_shape = [], []
    for ro in row_outs:
        if len(ro) == 3:
            buf, w, off = ro
            out_specs.append(row_spec(w, off))
            out_shape.append(jax.ShapeDtypeStruct(buf.shape, buf.dtype))
        else:
            w, dt = ro
            out_specs.append(row_spec(w, 0))
            out_shape.append(jax.ShapeDtypeStruct((T, w), dt))
    out_specs += [full_spec(s) for s in acc_outs]
    out_shape += [jax.ShapeDtypeStruct(tuple(s), F32) for s in acc_outs]
    aliases = {n_mm + n_r + n_f + b: k for b, (k, _) in enumerate(into)}
    return pl.pallas_call(
        body, name=name, grid=(T // tm,) if mm is None else (T // tm, nk),
        out_shape=out_shape, in_specs=in_specs, out_specs=out_specs,
        scratch_shapes=[] if mm is None else [pltpu.VMEM((tm, N) if nk > 1 else (8, 128), F32)],
        input_output_aliases=aliases,
        compiler_params=_params(("arbitrary",) if mm is None else ("arbitrary", "arbitrary")),
    )(*([] if mm is None else [a, b]), *[r[0] for r in rows], *fulls, *[ro[0] for _, ro in into])


def fwd_body(fn):
    return lambda r, f: (fn(*r, *f), ())


def bwd_body(fn, n_rows):
    def body(r, f):
        ins, cots = r[:n_rows], r[n_rows:]
        _, vjp = jax.vjp(fn, *ins, *f)
        g = vjp(tuple(cots))
        return g[:n_rows], g[n_rows:]
    return body


def whole_call(fn, ins, outs, *, name):
    n_in = len(ins)

    def body(*refs):
        res = fn(*[r[...] for r in refs[:n_in]])
        for ref, v in zip(refs[n_in:], res):
            ref[...] = v.astype(ref.dtype)

    return pl.pallas_call(
        body, name=name,
        out_shape=[jax.ShapeDtypeStruct(tuple(s), dt) for (s, dt) in outs],
        compiler_params=_params(),
    )(*ins)


def fn_modulate(x, sc, sh):
    return (_rms(x) * (1.0 + sc) + sh,)


def fn_sgu(u, v, nw, ws, bs):
    ug = jax.nn.gelu(u)
    vn = _rms(jax.nn.gelu(v)) * nw
    ri = lax.broadcasted_iota(jnp.int32, (Q, Q), 0)
    ci = lax.broadcasted_iota(jnp.int32, (Q, Q), 1)
    causal = ri >= ci
    chunks = []
    for n in range(u.shape[0] // Q):
        vc = vn[n * Q:(n + 1) * Q]
        cols = [dot_nn(jnp.where(causal, ws[g], 0.0), vc[:, g * Q:(g + 1) * Q]) + bs[g]
                for g in range(GM_GROUPS)]
        chunks.append(jnp.concatenate(cols, axis=1))
    sv = chunks[0] if len(chunks) == 1 else jnp.concatenate(chunks, axis=0)
    return (ug * sv,)


def fn_mix(ga, gb, pa, pb):
    return (jax.nn.sigmoid(ga) * pa + jax.nn.sigmoid(gb) * pb,)


def fn_res_modulate(x, o, g1, sc2, sh2):
    x1 = x + g1 * o
    return x1, _rms(x1) * (1.0 + sc2) + sh2


def relu2_tile(f):
    return jnp.square(jnp.maximum(f.astype(F32), 0.0)).astype(BF16)


def relu2_grad_tile(dact, f):
    return dact * (2.0 * jnp.maximum(f.astype(F32), 0.0))


def final_body(r, f):
    x1, gf, tgt = r
    g2, fnw = f

    def loss_fn(x1, gf, g2, fnw):
        y = _rms(x1 + g2 * gf) * fnw
        row = 0.5 * jnp.mean(jnp.square(y - tgt), axis=-1, keepdims=True)
        return jnp.sum(row, axis=0, keepdims=True)

    l, vjp = jax.vjp(loss_fn, x1, gf, g2, fnw)
    dx1, dgf, dg2, dfnw = vjp(jnp.ones((1, 1), F32))
    return (dx1, dgf), (jnp.broadcast_to(l, (1, 128)), dg2, dfnw)


def grad_x_body(r, f):
    x, dh, dxa = r
    _, vjp = jax.vjp(fn_modulate, x, *f)
    dx, dsc, dsh = vjp((dh,))
    return (dx + dxa,), (dsc, dsh)


CONV_CW = 128
CONV_PAD = 8


def _conv_pre(xp, w_ref, b_ref, r0, R):
    acc = b_ref[...] + w_ref[0:1, :] * xp[r0 + CONV_PAD - 3:r0 + CONV_PAD - 3 + R, :]
    for k in range(1, CONV_K):
        s = r0 + CONV_PAD - 3 + k
        acc = acc + w_ref[k:k + 1, :] * xp[s:s + R, :]
    return acc


def conv_fwd(proj, conv_w, conv_b):
    T = proj.shape[0]
    R = min(512, T)

    def body(x_ref, w_ref, b_ref, o_ref, xp):
        xp[0:CONV_PAD, :] = jnp.zeros((CONV_PAD, CONV_CW), F32)
        xp[CONV_PAD:CONV_PAD + T, :] = x_ref[...].astype(F32)
        for r0 in range(0, T, R):
            pre = _conv_pre(xp, w_ref, b_ref, r0, R)
            o_ref[r0:r0 + R, :] = (pre * jax.nn.sigmoid(pre)).astype(o_ref.dtype)

    return pl.pallas_call(
        body, name="conv_fwd", grid=(CONV_DIM // CONV_CW,),
        out_shape=jax.ShapeDtypeStruct((T, CONV_DIM), BF16),
        in_specs=[pl.BlockSpec((T, CONV_CW), lambda j: (0, P_XBC // CONV_CW + j)),
                  pl.BlockSpec((CONV_K, CONV_CW), lambda j: (0, j)),
                  pl.BlockSpec((1, CONV_CW), lambda j: (0, j))],
        out_specs=pl.BlockSpec((T, CONV_CW), lambda j: (0, j)),
        scratch_shapes=[pltpu.VMEM((T + CONV_PAD, CONV_CW), F32)],
        compiler_params=_params(("parallel",)),
    )(proj, conv_w, conv_b)


def conv_bwd(proj, dact, col0, conv_w, conv_b, dproj, *, name):
    T = proj.shape[0]
    R = min(512, T)
    nb = dact.shape[1] // CONV_CW
    c0 = col0 // CONV_CW
    x0 = (P_XBC + col0) // CONV_CW

    def body(x_ref, d_ref, w_ref, b_ref, _, dx_ref, dw_ref, db_ref, xp, dp):
        xp[0:CONV_PAD, :] = jnp.zeros((CONV_PAD, CONV_CW), F32)
        xp[CONV_PAD:CONV_PAD + T, :] = x_ref[...].astype(F32)
        dp[T:T + CONV_PAD, :] = jnp.zeros((CONV_PAD, CONV_CW), F32)
        dws = [jnp.zeros((1, CONV_CW), F32) for _ in range(CONV_K)]
        db = jnp.zeros((1, CONV_CW), F32)
        for r0 in range(0, T, R):
            pre = _conv_pre(xp, w_ref, b_ref, r0, R)
            s = jax.nn.sigmoid(pre)
            dpre = d_ref[r0:r0 + R, :].astype(F32) * (s * (1.0 + pre * (1.0 - s)))
            dp[r0:r0 + R, :] = dpre
            db = db + jnp.sum(dpre, axis=0, keepdims=True)
            for k in range(CONV_K):
                st = r0 + CONV_PAD - 3 + k
                dws[k] = dws[k] + jnp.sum(dpre * xp[st:st + R, :], axis=0, keepdims=True)
        for r0 in range(0, T, R):
            acc = w_ref[0:1, :] * dp[r0 + 3:r0 + 3 + R, :]
            for k in range(1, CONV_K):
                acc = acc + w_ref[k:k + 1, :] * dp[r0 + 3 - k:r0 + 3 - k + R, :]
            dx_ref[r0:r0 + R, :] = acc.astype(dx_ref.dtype)
        for k in range(CONV_K):
            dw_ref[k:k + 1, :] = dws[k]
        db_ref[...] = db

    return pl.pallas_call(
        body, name=name, grid=(nb,),
        out_shape=[jax.ShapeDtypeStruct(dproj.shape, dproj.dtype),
                   jax.ShapeDtypeStruct((CONV_K, nb * CONV_CW), F32),
                   jax.ShapeDtypeStruct((1, nb * CONV_CW), F32)],
        in_specs=[pl.BlockSpec((T, CONV_CW), lambda j: (0, x0 + j)),
                  pl.BlockSpec((T, CONV_CW), lambda j: (0, j)),
                  pl.BlockSpec((CONV_K, CONV_CW), lambda j: (0, c0 + j)),
                  pl.BlockSpec((1, CONV_CW), lambda j: (0, c0 + j)),
                  pl.BlockSpec(memory_space=pl.ANY)],
        out_specs=[pl.BlockSpec((T, CONV_CW), lambda j: (0, x0 + j)),
                   pl.BlockSpec((CONV_K, CONV_CW), lambda j: (0, j)),
                   pl.BlockSpec((1, CONV_CW), lambda j: (0, j))],
        scratch_shapes=[pltpu.VMEM((T + CONV_PAD, CONV_CW), F32),
                        pltpu.VMEM((T + CONV_PAD, CONV_CW), F32)],
        input_output_aliases={4: 0},
        compiler_params=_params(("parallel",)),
    )(proj, dact, conv_w, conv_b, dproj)


def _split3(a):
    hi = a.astype(BF16)
    r = a - hi.astype(F32)
    mid = r.astype(BF16)
    return hi, mid, (r - mid.astype(F32)).astype(BF16)


def _dg3(a, m, ca, cm, a_first):
    dims = (((ca,), (cm,)), ((), ())) if a_first else (((cm,), (ca,)), ((), ()))
    out = None
    for p in _split3(a):
        t = lax.dot_general(p, m, dims, preferred_element_type=F32) if a_first else \
            lax.dot_general(m, p, dims, preferred_element_type=F32)
        out = t if out is None else out + t
    return out


@jax.custom_vjp
def exact_right(a, m):
    return _dg3(a, m, 1, 0, True)


@jax.custom_vjp
def exact_left(m, a):
    return _dg3(a, m, 0, 1, False)


exact_right.defvjp(lambda a, m: (exact_right(a, m), m),
                   lambda m, g: (dot_nt(g, m), jnp.zeros_like(m)))
exact_left.defvjp(lambda m, a: (exact_left(m, a), m),
                  lambda m, g: (jnp.zeros_like(m), _dg3(g, m, 0, 0, False)))


def ssd_step(lane0, state, x, z, dtr, Bm, Cm, dtb, alog, dsk, nw):
    def iota(shape, dim):
        return lax.broadcasted_iota(jnp.int32, shape, dim)

    def one_hot(mask):
        return mask.astype(F32).astype(BF16)

    causal = iota((Q, Q), 0) >= iota((Q, Q), 1)
    eye = iota((Q, Q), 0) == iota((Q, Q), 1)
    lane = iota((1, 128), 1)
    colh = lax.shift_right_logical(iota((1, SSM_GW), 1), 6)
    to_cols = one_hot(iota((128, SSM_GW), 0) == lane0 + colh)

    dt_all = _softplus(dtr + dtb)
    a_all = dt_all * (-jnp.exp(alog))
    cum_all = exact_left(one_hot(causal), a_all)
    both = exact_right(jnp.concatenate([dt_all, cum_all], axis=0), to_cols)
    dt_f, cum_f = both[:Q], both[Q:]
    last_f = jnp.sum(jnp.where(iota((Q, 1), 0) == Q - 1, cum_f, 0.0), axis=0, keepdims=True)
    dsk_f = jnp.zeros((1, SSM_GW), F32)
    for h in range(SSM_HPG):
        dsk_f = jnp.where(colh == h, dsk[h], dsk_f)

    xdt = x * dt_f
    cb = dot_nt(Cm, Bm)
    ms, rhs = [], []
    for h in range(SSM_HPG):
        ch = jnp.sum(jnp.where(lane == lane0 + h, cum_all, 0.0), axis=1, keepdims=True)
        ch_t = jnp.sum(jnp.where(eye, ch, 0.0), axis=0, keepdims=True)
        ms.append(cb * jnp.exp(jnp.where(causal, ch - ch_t, -1e30)))
        rhs.append(jnp.where(colh == h, xdt, 0.0))
    y = dot_nn(jnp.concatenate(ms, axis=1), jnp.concatenate(rhs, axis=0))
    y = y + dot_nn(Cm, state) * jnp.exp(cum_f) + x * dsk_f
    new_state = state * jnp.exp(last_f) + dot_tn(Bm, xdt * jnp.exp(last_f - cum_f))
    gated = y * (z * jax.nn.sigmoid(z))
    return new_state, _rms(gated) * nw


SSD_GPS = 4
_XW = SSD_GPS * SSM_GW
_BW = SSD_GPS * 128


def _ssd_in_specs(rev, nc):
    def n_of(n):
        return nc - 1 - n if rev else n
    return [
        pl.BlockSpec((Q, _XW), lambda g, n: (n_of(n), g)),
        pl.BlockSpec((Q, _BW), lambda g, n: (n_of(n), SSM_INNER // _BW + g)),
        pl.BlockSpec((Q, _BW), lambda g, n: (n_of(n), (SSM_INNER + SSM_GROUPS * 128) // _BW + g)),
        pl.BlockSpec((Q, _XW), lambda g, n: (n_of(n), P_Z // _XW + g)),
        pl.BlockSpec((Q, 128), lambda g, n: (n_of(n), 0)),
        pl.BlockSpec((1, 128), lambda g, n: (0, 0)),
        pl.BlockSpec((1, 128), lambda g, n: (0, 0)),
        pl.BlockSpec((SSD_GPS, SSM_HPG, 1, 1), lambda g, n: (g, 0, 0, 0)),
        pl.BlockSpec((1, _XW), lambda g, n: (0, g)),
    ]


def _ssd_group_inputs(gi, x_ref, b_ref, c_ref, z_ref, dt_ref, dtb_ref, al_ref, dk_ref, nw_ref):
    xs = slice(gi * SSM_GW, (gi + 1) * SSM_GW)
    bs = slice(gi * 128, (gi + 1) * 128)
    return (x_ref[:, xs].astype(F32), z_ref[:, xs].astype(F32), dt_ref[...],
            b_ref[:, bs], c_ref[:, bs],
            dtb_ref[...], al_ref[...], dk_ref[gi], nw_ref[:, xs])


def ssd_fwd(xact, proj, dtg, dtb, alog, dsk, nw, comm):
    T = xact.shape[0]
    nc = T // Q
    nx = len(comm)
    ng = SSM_GROUPS // SSD_GPS
    any_spec = pl.BlockSpec(memory_space=pl.ANY)

    def body(*refs):
        in_refs, src_refs = refs[:9], refs[9:9 + nx]
        yb_ref, st_ref = refs[9 + nx:11 + nx]
        out_refs, state, sems = refs[11 + nx:11 + 2 * nx], refs[11 + 2 * nx], refs[12 + 2 * nx:]
        g, n = pl.program_id(0), pl.program_id(1)
        exchange_start(comm, src_refs, out_refs, sems, (g == 0) & (n == 0))

        @pl.when(n == 0)
        def _():
            state[...] = jnp.zeros(state.shape, F32)

        for gi in range(SSD_GPS):
            lane0 = SSM_HPG * (SSD_GPS * pl.program_id(0) + gi)
            s = state[gi]
            st_ref[gi, 0] = s
            new_s, yb = ssd_step(lane0, s, *_ssd_group_inputs(gi, *in_refs))
            state[gi] = new_s
            yb_ref[:, gi * SSM_GW:(gi + 1) * SSM_GW] = yb.astype(yb_ref.dtype)

        exchange_finish(comm, src_refs, out_refs, sems, (g == ng - 1) & (n == nc - 1))

    return pl.pallas_call(
        body, name="ssd_fwd", grid=(ng, nc),
        out_shape=[jax.ShapeDtypeStruct((T, SSM_INNER), BF16),
                   jax.ShapeDtypeStruct((SSM_GROUPS, nc, 128, SSM_GW), F32)] + exchange_out_shapes(comm),
        in_specs=_ssd_in_specs(False, nc) + [any_spec] * nx,
        out_specs=[pl.BlockSpec((Q, _XW), lambda g, n: (n, g)),
                   pl.BlockSpec((SSD_GPS, 1, 128, SSM_GW), lambda g, n: (g, n, 0, 0))] + [any_spec] * nx,
        scratch_shapes=[pltpu.VMEM((SSD_GPS, 128, SSM_GW), F32)] + exchange_semaphores(comm),
        compiler_params=_params(("arbitrary", "arbitrary")),
    )(xact, xact, xact, proj, dtg, dtb, alog, dsk, nw, *[it[0] for it in comm])


def ssd_bwd(xact, proj, dtg, dtb, alog, dsk, nw, states, dyb, dproj, comm):
    T = xact.shape[0]
    nc = T // Q

    nx = len(comm)
    ng = SSM_GROUPS // SSD_GPS
    any_spec = pl.BlockSpec(memory_space=pl.ANY)

    def body(*refs):
        in_refs, (st_ref, dy_ref, _) = refs[:9], refs[9:12]
        src_refs, refs = refs[12:12 + nx], refs[12 + nx:]
        dx_ref, db_ref, dc_ref, dz_ref, ddt_ref, ddtb_ref, dal_ref, ddk_ref, dnw_ref = refs[:9]
        out_refs, dstate, sems = refs[9:9 + nx], refs[9 + nx], refs[10 + nx:]
        exchange_start(comm, src_refs, out_refs, sems, (pl.program_id(0) == 0) & (pl.program_id(1) == 0))

        @pl.when(pl.program_id(1) == 0)
        def _():
            dstate[...] = jnp.zeros(dstate.shape, F32)
            ddtb_ref[...] = jnp.zeros(ddtb_ref.shape, F32)
            dal_ref[...] = jnp.zeros(dal_ref.shape, F32)
            ddk_ref[...] = jnp.zeros(ddk_ref.shape, F32)
            dnw_ref[...] = jnp.zeros(dnw_ref.shape, F32)

        for gi in range(SSD_GPS):
            xs = slice(gi * SSM_GW, (gi + 1) * SSM_GW)
            bs = slice(gi * 128, (gi + 1) * 128)
            lane0 = SSM_HPG * (SSD_GPS * pl.program_id(0) + gi)
            ins = (st_ref[gi, 0],) + _ssd_group_inputs(gi, *in_refs)
            _, vjp = jax.vjp(functools.partial(ssd_step, lane0), *ins)
            ds, dx, dz, ddt, dbm, dcm, ddtb, dal, ddk, dnw = vjp((dstate[gi], dy_ref[:, xs].astype(F32)))
            dstate[gi] = ds
            dx_ref[:, xs] = dx.astype(dx_ref.dtype)
            db_ref[:, bs] = dbm.astype(db_ref.dtype)
            dc_ref[:, bs] = dcm.astype(dc_ref.dtype)
            dz_ref[:, xs] = dz.astype(dz_ref.dtype)
            ddt_ref[gi] = ddt
            ddtb_ref[gi] += ddtb
            dal_ref[gi] += dal
            ddk_ref[gi] += ddk
            dnw_ref[:, xs] += dnw

        exchange_finish(comm, src_refs, out_refs, sems,
                        (pl.program_id(0) == ng - 1) & (pl.program_id(1) == nc - 1))

    rev = lambda n: nc - 1 - n
    row_shape = jax.ShapeDtypeStruct((SSM_GROUPS, 1, 128), F32)
    row_spec = pl.BlockSpec((SSD_GPS, 1, 128), lambda g, n: (g, 0, 0))
    return pl.pallas_call(
        body, name="ssd_bwd", grid=(ng, nc),
        out_shape=[jax.ShapeDtypeStruct((T, SSM_INNER), BF16),
                   jax.ShapeDtypeStruct((T, SSM_GROUPS * 128), BF16),
                   jax.ShapeDtypeStruct((T, SSM_GROUPS * 128), BF16),
                   jax.ShapeDtypeStruct(dproj.shape, dproj.dtype),
                   jax.ShapeDtypeStruct((SSM_GROUPS, T, 128), F32),
                   row_shape, row_shape,
                   jax.ShapeDtypeStruct((SSM_GROUPS, SSM_HPG, 1, 1), F32),
                   jax.ShapeDtypeStruct((1, SSM_INNER), F32)] + exchange_out_shapes(comm),
        in_specs=_ssd_in_specs(True, nc) + [
            pl.BlockSpec((SSD_GPS, 1, 128, SSM_GW), lambda g, n: (g, rev(n), 0, 0)),
            pl.BlockSpec((Q, _XW), lambda g, n: (rev(n), g)),
            any_spec] + [any_spec] * nx,
        out_specs=[pl.BlockSpec((Q, _XW), lambda g, n: (rev(n), g)),
                   pl.BlockSpec((Q, _BW), lambda g, n: (rev(n), g)),
                   pl.BlockSpec((Q, _BW), lambda g, n: (rev(n), g)),
                   pl.BlockSpec((Q, _XW), lambda g, n: (rev(n), P_Z // _XW + g)),
                   pl.BlockSpec((SSD_GPS, Q, 128), lambda g, n: (g, rev(n), 0)),
                   row_spec, row_spec,
                   pl.BlockSpec((SSD_GPS, SSM_HPG, 1, 1), lambda g, n: (g, 0, 0, 0)),
                   pl.BlockSpec((1, _XW), lambda g, n: (0, g))] + [any_spec] * nx,
        scratch_shapes=[pltpu.VMEM((SSD_GPS, 128, SSM_GW), F32)] + exchange_semaphores(comm),
        input_output_aliases={11: 3},
        compiler_params=_params(("arbitrary", "arbitrary")),
    )(xact, xact, xact, proj, dtg, dtb, alog, dsk, nw, states, dyb, dproj, *[it[0] for it in comm])


ADAMW_WHOLE_ELEMS = 256 * 1024


def adamw(w, g, m, v, *, name):
    shape = w.shape
    parts = g.shape != shape
    nd = len(shape)
    if w.size <= ADAMW_WHOLE_ELEMS:
        grid, tr = (1,), shape[-2]
    else:
        assert all(s == 1 for s in shape[:-2]) and shape[-2] % 256 == 0
        grid, tr = (shape[-2] // 256,), 256
    blk = tuple(shape[:-2]) + (tr, shape[-1])
    spec = pl.BlockSpec(blk, lambda i: (0,) * (nd - 2) + (i, 0))
    g_spec = pl.BlockSpec((N_DEV,) + blk[1:], lambda i: (0,) * (nd - 2) + (i, 0)) if parts else spec

    def body(w_ref, g_ref, m_ref, v_ref, go_ref, d_ref, nm_ref, nv_ref):
        if parts:
            g = g_ref[0:1].astype(F32)
            for j in range(1, N_DEV):
                g = g + g_ref[j:j + 1].astype(F32)
        else:
            g = g_ref[...]
        nm = ADAM_B1 * m_ref[...] + (1.0 - ADAM_B1) * g
        nv = ADAM_B2 * v_ref[...] + (1.0 - ADAM_B2) * jnp.square(g)
        m_hat = nm / (1.0 - ADAM_B1 ** ADAM_STEP)
        v_hat = nv / (1.0 - ADAM_B2 ** ADAM_STEP)
        go_ref[...] = g
        d_ref[...] = -ADAM_LR * (m_hat / (jnp.sqrt(v_hat) + ADAM_EPS) + ADAM_WD * w_ref[...])
        nm_ref[...] = nm
        nv_ref[...] = nv

    shp = jax.ShapeDtypeStruct(shape, F32)
    return pl.pallas_call(
        body, name=name, grid=grid,
        out_shape=[shp] * 4, in_specs=[spec, g_spec, spec, spec], out_specs=[spec] * 4,
        compiler_params=_params(("parallel",)),
    )(w, g, m, v)


def _pad_rows(a, rows):
    return jnp.pad(a, ((0, rows - a.shape[0]), (0, 0)))


WIN_W = 1408
N_IN = IN_WIDTH // N_DEV
_A6 = OFF_DT - 6 * N_IN
_C6 = 7 * N_IN - OFF_GA


def _win_offset(me):
    return jnp.where(me == 7, 124, 4 * me)


def _w_in_window(shard, me):
    rows = shard.shape[0]
    z = lambda n: jnp.zeros((rows, n), shard.dtype)
    a = lax.dynamic_update_slice(z(WIN_W), shard, (0, _win_offset(me)))
    b = jnp.concatenate([z(24), shard[:, :_A6], shard[:, _A6 + 32:], z(4), shard[:, _A6:_A6 + 32], z(96)], axis=1)
    return jnp.where(me == 6, b, a)


def _w_in_from_window(window, me):
    a = lax.dynamic_slice(window, (0, _win_offset(me)), (window.shape[0], N_IN))
    b = jnp.concatenate([window[:, 24:24 + _A6], window[:, 1280:1312], window[:, 24 + _A6:24 + _A6 + _C6]], axis=1)
    return jnp.where(me == 6, b, a)


def _w_all_from_windows(g):
    def merge_first(p, t):
        return jnp.concatenate([p[:, :128] + t, p[:, 128:]], axis=1)

    parts = [g[0][:, :1280]]
    for j in range(1, 6):
        parts.append(merge_first(g[j][:, :1280], g[j - 1][:, 1280:]))
    p6 = merge_first(g[6][:, :1280], g[5][:, 1280:])
    parts.append(jnp.concatenate([p6[:, :1152], p6[:, 1152:] + g[7][:, :128]], axis=1))
    parts.append(g[7][:, 128:])
    parts.append(g[6][:, 1280:])
    return jnp.concatenate(parts, axis=1)


def _windows_of_w_all(gw):
    wins = [gw[:, 1280 * j:1280 * j + WIN_W] for j in range(6)]
    wins.append(jnp.concatenate([gw[:, 7680:8960], gw[:, PROJ_W:]], axis=1))
    wins.append(gw[:, 8832:PROJ_W])
    return jnp.stack(wins)


def kernel(x, c, w_mod, b_mod, w_in, gm_norm_w, gm_ws, gm_bs, conv_w, conv_b, dt_bias, a_log, d_skip, ssm_norm_w, w_branch_gm, w_branch_ssm, w_out, w_ff1, w_ff2, final_norm_w, loss_target, m_w_mod, m_b_mod, m_w_in, m_gm_norm_w, m_gm_ws, m_gm_bs, m_conv_w, m_conv_b, m_dt_bias, m_a_log, m_d_skip, m_ssm_norm_w, m_w_branch_gm, m_w_branch_ssm, m_w_out, m_w_ff1, m_w_ff2, m_final_norm_w, v_w_mod, v_b_mod, v_w_in, v_gm_norm_w, v_gm_ws, v_gm_bs, v_conv_w, v_conv_b, v_dt_bias, v_a_log, v_d_skip, v_ssm_norm_w, v_w_branch_gm, v_w_branch_ssm, v_w_out, v_w_ff1, v_w_ff2, v_final_norm_w):
    T = x.shape[1]
    me = 4 * lax.axis_index("x") + 2 * lax.axis_index("y") + lax.axis_index("c")
    x2 = x[0]
    tgt = loss_target[0]
    n_in = IN_WIDTH // N_DEV
    n_mod = N_MOD * D // N_DEV
    n_cv = CONV_DIM // N_DEV

    c_all, conv_w_full = exchange(
        [(c.reshape(8, 128), _whole, (N_DEV, 8, 128), _slot),
         (conv_w[0], _whole, (N_DEV, CONV_K, n_cv), _slot)], name="gather_c_convw")
    c_all = c_all.reshape(N_DEV, D)
    conv_w_full = conv_w_full.transpose(1, 0, 2).reshape(CONV_K, CONV_DIM)

    win = _w_in_window(w_in[0].astype(BF16), me)
    gwin = gather_blocks_two_level(win, name="gather_w_in")
    late_weights = [
        (w_branch_gm[0].astype(BF16), _whole, (D, D), _rows(D // N_DEV)),
        (w_branch_ssm[0].astype(BF16), _whole, (SSM_INNER, D), _rows(SSM_INNER // N_DEV)),
        (w_out[0].astype(BF16), _whole, (D, D), _rows(D // N_DEV)),
        (w_ff1[0].astype(BF16), _whole, (D, D_FF), _cols(D_FF // N_DEV)),
        (w_ff2[0].astype(BF16), _whole, (D_FF, D), _rows(D_FF // N_DEV))]
    w_all = _w_all_from_windows(gwin)
    w_dt = w_all[:, PROJ_W:]

    c_pad = _pad_rows(c_all, 128)
    b_mine = lax.dynamic_slice(b_mod, (0, me * n_mod), (1, n_mod))

    def mod_fn(cp, w, b):
        ca = cp * jax.nn.sigmoid(cp)
        return (jnp.dot(ca, w, precision=HIGHEST, preferred_element_type=F32) + b,)

    (mod_part,) = whole_call(mod_fn, [c_pad, w_mod[0], b_mine], [((128, n_mod), F32)], name="mod_fwd")
    gmod = gather_blocks(mod_part[:N_DEV], name="gather_mod")
    mod = lax.dynamic_index_in_dim(gmod, me, axis=1, keepdims=False).reshape(N_MOD, D)
    sh1, sc1, gt1, sh2, sc2, gt2 = [mod[i:i + 1] for i in range(N_MOD)]

    (h,) = rowwise_call(fwd_body(fn_modulate), [x2], [sc1, sh1], [(D, BF16)], [], tm=256, name="modulate1")
    proj = matmul(h, w_all, "nn", BF16, name="mm_proj", n=PROJ_W)
    dtg = matmul(h, w_dt, "nn", F32, name="mm_dt")
    ws = gm_ws[0]
    bs3 = gm_bs[0].reshape(GM_GROUPS, Q, 1)
    sgu_rows = [(proj, D, P_U), (proj, D, P_V)]
    (ya,) = rowwise_call(fwd_body(fn_sgu), sgu_rows, [gm_norm_w, ws, bs3], [(D, BF16)], [],
                         tm=256, name="sgu_fwd")
    xact = conv_fwd(proj, conv_w_full, conv_b)
    dtb4 = jnp.pad(dt_bias, ((0, 0), (0, 96)))
    alog4 = jnp.pad(a_log, ((0, 0), (0, 96)))
    dsk4 = d_skip.reshape(SSM_GROUPS, SSM_HPG, 1, 1)
    yb, states, w_gm_f, w_ssm_f, w_out_f, w_ff1_f, w_ff2_f = ssd_fwd(
        xact, proj, dtg, dtb4, alog4, dsk4, ssm_norm_w, late_weights)
    pa = matmul(ya, w_gm_f, "nn", F32, name="mm_branch_gm")
    gate_rows = [(proj, D, P_GA), (proj, D, P_GB)]
    mixed, pb = rowwise_call(
        lambda r, fl: (fn_mix(*r) + (r[3],), ()), gate_rows + [pa], [], [(D, BF16), (D, F32)], [],
        tm=FUSED_TM, name="branch_ssm_mix", mm=(yb, w_ssm_f, "nn", 3, None), tk=SSM_INNER)
    x1, h2, o = rowwise_call(
        lambda r, fl: (fn_res_modulate(*r, *fl) + (r[1],), ()), [x2], [gt1, sc2, sh2],
        [(D, F32), (D, BF16), (D, F32)], [], tm=FUSED_TM, name="out_res_modulate2",
        mm=(mixed, w_out_f, "nn", 1, None))
    f = matmul(h2, w_ff1_f, "nn", BF16, name="mm_ff1")

    dx1, dgf, loss_v, dgt2, dfnw = rowwise_call(
        final_body, [x1, tgt], [gt2, final_norm_w.reshape(1, D)], [(D, F32), (D, BF16)],
        [(1, 128), (1, D), (1, D)], tm=FUSED_TM, name="ff2_loss_bwd", mm=(f, w_ff2_f, "nn", 1, relu2_tile),
        tk=D_FF)
    df = matmul(dgf, w_ff2_f, "nt", BF16, name="mm_ff2_dgrad", epi=relu2_grad_tile, epi_ins=(f,))
    gw_ff2 = matmul(f, dgf, "tn", BF16, name="mm_ff2_wgrad", tk=WGRAD_TK, a_pro=relu2_tile)
    gw_ff1 = matmul(h2, df, "tn", BF16, name="mm_ff1_wgrad", tk=WGRAD_TK)

    def res_mod_bwd(r, fl):
        xv, ov, dx1v, dh2v = r
        _, vjp = jax.vjp(fn_res_modulate, xv, ov, *fl)
        dxv, dov, dg1, dsc, dsh = vjp((dx1v, dh2v))
        return (dxv, dov), (dg1, dsc, dsh)

    dxa, do, dgt1, dsc2, dsh2 = rowwise_call(
        res_mod_bwd, [x2, o, dx1], [gt1, sc2, sh2], [(D, F32), (D, BF16)],
        [(1, D), (1, D), (1, D)], tm=FUSED_TM, name="ff1_dgrad_res_modulate2_bwd",
        mm=(df, w_ff1_f, "nt", 3, None), tk=D_FF)
    gw_out = matmul(mixed, do, "tn", BF16, name="mm_out_wgrad", tk=WGRAD_TK)
    dproj = lax.empty((T, ALL_W), BF16)

    def mix_bwd(r, fl):
        dga, dgb, dpa, dpb = bwd_body(fn_mix, 4)(r, fl)[0]
        return (jnp.concatenate([dga, dgb], axis=1), dpa, dpb), ()

    dproj, dpa, dpb = rowwise_call(
        mix_bwd, gate_rows + [pa, pb], [], [(dproj, 2 * D, P_GA), (D, BF16), (D, BF16)], [],
        tm=FUSED_TM, name="out_dgrad_mix_bwd", mm=(do, w_out_f, "nt", 4, None))
    gw_gm = matmul(ya, dpa, "tn", BF16, name="mm_branch_gm_wgrad", tk=WGRAD_TK)
    dyb = matmul(dpb, w_ssm_f, "nt", BF16, name="mm_branch_ssm_dgrad")
    gw_ssm = matmul(yb, dpb, "tn", BF16, name="mm_branch_ssm_wgrad", tk=WGRAD_TK)

    def sgu_bwd(r, fl):
        (du, dv), acc = bwd_body(fn_sgu, 2)(r, fl)
        return (jnp.concatenate([du, dv], axis=1),), acc

    dproj, dgnw, dws, dbs = rowwise_call(
        sgu_bwd, sgu_rows, [gm_norm_w, ws, bs3], [(dproj, 2 * D, P_U)],
        [(1, D), (GM_GROUPS, Q, Q), (GM_GROUPS, Q, 1)], tm=FUSED_TM, name="branch_gm_dgrad_sgu_bwd",
        mm=(dpa, w_gm_f, "nt", 2, None))
    early_grads = [
        (gw_gm, _rows(D // N_DEV), (N_DEV, D // N_DEV, D), _slot),
        (gw_ssm, _rows(SSM_INNER // N_DEV), (N_DEV, SSM_INNER // N_DEV, D), _slot),
        (gw_out, _rows(D // N_DEV), (N_DEV, D // N_DEV, D), _slot),
        (gw_ff1, _cols(D_FF // N_DEV), (N_DEV, D, D_FF // N_DEV), _slot),
        (gw_ff2, _rows(D_FF // N_DEV), (N_DEV, D_FF // N_DEV, D), _slot),
        (_pack_rows([dgnw, dws, dbs], EARLY_ROWS), _whole, (N_DEV, sum(EARLY_ROWS), 128), _slot)]
    (dxs, dbm, dcm, dproj, ddt8, ddtb, dalog, ddsk, dsnw,
     r_gm, r_ssm, r_out, r_ff1, r_ff2, early_all) = ssd_bwd(
        xact, proj, dtg, dtb4, alog4, dsk4, ssm_norm_w, states, dyb, dproj, early_grads)
    dconv_w, dconv_b = [], []
    for nm, dact_part, col0 in (("xs", dxs, 0), ("b", dbm, SSM_INNER), ("c", dcm, SSM_INNER + SSM_GROUPS * 128)):
        dproj, dcw, dcb = conv_bwd(proj, dact_part, col0, conv_w_full, conv_b, dproj, name="conv_bwd_" + nm)
        dconv_w.append(dcw)
        dconv_b.append(dcb)
    dconv_w = jnp.concatenate(dconv_w, axis=1)
    dconv_b = jnp.concatenate(dconv_b, axis=1)
    dproj = dproj.at[:, PROJ_W:].set(jnp.sum(ddt8, axis=0).astype(BF16))
    gw_all = matmul(h, dproj, "tn", BF16, name="mm_in_wgrad", tn=1152, tk=WGRAD_TK)
    mid_pack = _pack_rows([dconv_w, dconv_b, jnp.sum(ddtb, axis=0), jnp.sum(dalog, axis=0), ddsk, dsnw, dfnw,
                           jnp.concatenate([dgt1, dsh2, dsc2, dgt2], axis=0)], MID_ROWS)
    dh, r_in, mid_all = matmul(
        dproj, w_all, "nt", F32, name="mm_in_dgrad", tk=3456,
        comm=[(_windows_of_w_all(gw_all), _slot, (N_DEV, D, WIN_W), _slot),
              (mid_pack, _whole, (N_DEV, sum(MID_ROWS), 128), _slot)])
    grad_x, dsc1, dsh1 = rowwise_call(grad_x_body, [x2, dh, dxa], [sc1, sh1], [(D, F32)],
                                      [(1, D), (1, D)], tm=256, name="modulate1_bwd")

    g_w_in = _w_in_from_window(sum_devices(r_in, tr=256, name="sum_w_in_grads"), me).reshape(1, D, n_in)

    late_all = gather_blocks(_pack_rows([dsh1, dsc1], LATE_ROWS), name="gather_dmod1")
    s_early = _unpack_rows(sum_devices(early_all, tr=early_all.shape[1], name="sum_small_early"), EARLY_ROWS)
    s_mid = _unpack_rows(sum_devices(mid_all, tr=mid_all.shape[1], name="sum_small_mid"), MID_ROWS)
    s_late = _unpack_rows(sum_devices(late_all, tr=late_all.shape[1], name="sum_small_late"), LATE_ROWS)
    g_gm_norm_w = s_early[0][:D].reshape(1, D)
    g_gm_ws = s_early[1].reshape(GM_GROUPS * Q, Q)
    g_gm_bs = s_early[2][:GM_GROUPS * Q].reshape(GM_GROUPS, Q)
    g_conv_w_full = s_mid[0].reshape(CONV_K, CONV_DIM)
    g_conv_w = lax.dynamic_slice(g_conv_w_full, (0, me * n_cv), (CONV_K, n_cv))
    g_conv_b = s_mid[1].reshape(1, CONV_DIM)
    g_dt_bias = s_mid[2][:32].reshape(1, 32)
    g_a_log = s_mid[3][:32].reshape(1, 32)
    g_d_skip = s_mid[4][:32].reshape(1, 32)
    g_ssm_norm_w = s_mid[5].reshape(1, SSM_INNER)
    g_final_norm_w = s_mid[6][:D].reshape(1, D)
    g_b_mod = jnp.concatenate([s_late[0][:D], s_late[1][:D], s_mid[7]]).reshape(1, N_MOD * D)

    dmod_all = jnp.concatenate(
        [late_all.reshape(N_DEV, -1)[:, :2 * D],
         mid_all[:, sum(MID_ROWS[:7]):].reshape(N_DEV, 4 * D)], axis=1)
    dmod_mine = _pad_rows(lax.dynamic_slice(dmod_all, (0, me * n_mod), (N_DEV, n_mod)), 128)

    def wmod_grad_fn(cp, dm):
        ca = cp * jax.nn.sigmoid(cp)
        return (lax.dot_general(ca, dm, (((0,), (0,)), ((), ())), precision=HIGHEST,
                                preferred_element_type=F32),)

    (g_w_mod,) = whole_call(wmod_grad_fn, [c_pad, dmod_mine], [((D, n_mod), F32)], name="w_mod_grad")

    upd = {}

    def step(name, w, g, m, v, parts=False):
        upd[name] = adamw(w, g if parts else g.reshape(w.shape), m, v, name="adamw_" + name)

    step("w_mod", w_mod, g_w_mod, m_w_mod, v_w_mod)
    step("b_mod", b_mod, g_b_mod, m_b_mod, v_b_mod)
    step("w_in", w_in, g_w_in, m_w_in, v_w_in)
    step("gm_norm_w", gm_norm_w, g_gm_norm_w, m_gm_norm_w, v_gm_norm_w)
    step("gm_ws", gm_ws, g_gm_ws, m_gm_ws, v_gm_ws)
    step("gm_bs", gm_bs, g_gm_bs, m_gm_bs, v_gm_bs)
    step("conv_w", conv_w, g_conv_w, m_conv_w, v_conv_w)
    step("conv_b", conv_b, g_conv_b, m_conv_b, v_conv_b)
    step("dt_bias", dt_bias, g_dt_bias, m_dt_bias, v_dt_bias)
    step("a_log", a_log, g_a_log, m_a_log, v_a_log)
    step("d_skip", d_skip, g_d_skip, m_d_skip, v_d_skip)
    step("ssm_norm_w", ssm_norm_w, g_ssm_norm_w, m_ssm_norm_w, v_ssm_norm_w)
    step("w_branch_gm", w_branch_gm, r_gm, m_w_branch_gm, v_w_branch_gm, parts=True)
    step("w_branch_ssm", w_branch_ssm, r_ssm, m_w_branch_ssm, v_w_branch_ssm, parts=True)
    step("w_out", w_out, r_out, m_w_out, v_w_out, parts=True)
    step("w_ff1", w_ff1, r_ff1, m_w_ff1, v_w_ff1, parts=True)
    step("w_ff2", w_ff2, r_ff2, m_w_ff2, v_w_ff2, parts=True)
    step("final_norm_w", final_norm_w.reshape(1, D), g_final_norm_w, m_final_norm_w.reshape(1, D),
         v_final_norm_w.reshape(1, D))
    upd["final_norm_w"] = tuple(a.reshape(D) for a in upd["final_norm_w"])

    loss = lax.psum(loss_v[0, 0], ("x", "y", "c"))
    order = ["w_mod", "b_mod", "w_in", "gm_norm_w", "gm_ws", "gm_bs", "conv_w", "conv_b", "dt_bias", "a_log",
             "d_skip", "ssm_norm_w", "w_branch_gm", "w_branch_ssm", "w_out", "w_ff1", "w_ff2", "final_norm_w"]
    return (loss, grad_x.reshape(1, T, D),
            *[upd[n][0] for n in order], *[upd[n][1] for n in order],
            *[upd[n][2] for n in order], *[upd[n][3] for n in order])
```

```python
import functools

import jax
import jax.numpy as jnp
from jax import lax
from jax.experimental import pallas as pl
from jax.experimental.pallas import tpu as pltpu

F32 = jnp.float32
BF16 = jnp.bfloat16
MESH = pl.DeviceIdType.MESH
HIGHEST = lax.Precision.HIGHEST

N_DEV = 8
D = 1024
Q = 128
GM_GROUPS = 8
SSM_INNER = 2048
SSM_GROUPS = 8
SSM_HPG = 4
SSM_P = 64
SSM_GW = SSM_HPG * SSM_P
CONV_DIM = 4096
CONV_K = 4
D_FF = 4096
N_MOD = 6
EPS = 1e-6
IN_WIDTH = 10272
OFF_DT = 8192
OFF_GA = 8224
PROJ_W = 10240
ALL_W = 10368
P_U, P_V, P_Z, P_XBC, P_GA, P_GB = 0, 1024, 2048, 4096, 8192, 9216

ADAM_LR = 0.001
ADAM_B1 = 0.9
ADAM_B2 = 0.999
ADAM_EPS = 1e-08
ADAM_WD = 0.01
ADAM_STEP = 10

VMEM_LIMIT_BYTES = 48 * 1024 * 1024
FUSED_TM = 256
WGRAD_TK = 2048
EARLY_ROWS = (8, 1024, 8)
MID_ROWS = (128, 32, 8, 8, 8, 16, 8, 32)
LATE_ROWS = (8, 8)


def _pack_rows(arrs, rows):
    def rows128(a, r):
        a = a.reshape(-1)
        return jnp.pad(a, (0, r * 128 - a.shape[0])).reshape(r, 128)
    return jnp.concatenate([rows128(a, r) for a, r in zip(arrs, rows)], axis=0)


def _unpack_rows(s, rows):
    out, o = [], 0
    for r in rows:
        out.append(s[o:o + r].reshape(-1))
        o += r
    return out


def _params(sem=None):
    return pltpu.CompilerParams(dimension_semantics=sem, vmem_limit_bytes=VMEM_LIMIT_BYTES)


def _dg(a, b, ca, cb):
    return lax.dot_general(a.astype(BF16), b.astype(BF16), (((ca,), (cb,)), ((), ())),
                           preferred_element_type=F32)


@jax.custom_vjp
def dot_nn(a, b):
    return _dg(a, b, 1, 0)


@jax.custom_vjp
def dot_nt(a, b):
    return _dg(a, b, 1, 1)


@jax.custom_vjp
def dot_tn(a, b):
    return _dg(a, b, 0, 0)


def _like(ct, primal):
    return ct.astype(primal.dtype)


dot_nn.defvjp(lambda a, b: (dot_nn(a, b), (a, b)),
              lambda r, g: (_like(dot_nt(g, r[1]), r[0]), _like(dot_tn(r[0], g), r[1])))
dot_nt.defvjp(lambda a, b: (dot_nt(a, b), (a, b)),
              lambda r, g: (_like(dot_nn(g, r[1]), r[0]), _like(dot_tn(g, r[0]), r[1])))
dot_tn.defvjp(lambda a, b: (dot_tn(a, b), (a, b)),
              lambda r, g: (_like(dot_nt(r[1], g), r[0]), _like(dot_nn(r[0], g), r[1])))


def _rms(x):
    return x * lax.rsqrt(jnp.mean(x * x, axis=-1, keepdims=True) + EPS)


def _softplus(x):
    return jnp.maximum(x, 0.0) + jnp.log1p(jnp.exp(-jnp.abs(x)))


def _rows(n):
    return lambda ref, j: ref.at[pl.ds(pl.multiple_of(j * n, n), n)]


def _cols(n):
    return lambda ref, j: ref.at[:, pl.ds(pl.multiple_of(j * n, n), n)]


def _slot(ref, j):
    return ref.at[j]


def _whole(ref, j):
    return ref


def exchange(items, *, name):
    n = len(items)

    def body(*refs):
        exchange_in_body(items, refs[:n], refs[n:2 * n], refs[2 * n:], True, True)

    return pl.pallas_call(
        body, name=name,
        out_shape=exchange_out_shapes(items),
        in_specs=[pl.BlockSpec(memory_space=pl.ANY)] * n,
        out_specs=[pl.BlockSpec(memory_space=pl.ANY)] * n,
        scratch_shapes=exchange_semaphores(items),
    )(*[it[0] for it in items])


def exchange_out_shapes(items):
    return [jax.ShapeDtypeStruct(tuple(shape), src.dtype) for (src, _, shape, _) in items]


def exchange_semaphores(items):
    n = len(items)
    return [pltpu.SemaphoreType.DMA((n, N_DEV - 1)), pltpu.SemaphoreType.DMA((n, N_DEV - 1)),
            pltpu.SemaphoreType.DMA((n,))]


def _exchange_copies(items, src_refs, out_refs, sems):
    send_sems, recv_sems, local_sems = sems
    x = lax.axis_index("x")
    y = lax.axis_index("y")
    c = lax.axis_index("c")
    me = 4 * x + 2 * y + c
    local = [pltpu.make_async_copy(src_win(src_refs[i], me), dst_win(out_refs[i], me), local_sems.at[i])
             for i, (_, src_win, _, dst_win) in enumerate(items)]
    remote = []
    for i, (_, src_win, _, dst_win) in enumerate(items):
        for k in range(1, N_DEV):
            px = lax.rem(x + ((k >> 2) & 1), 2)
            py = lax.rem(y + ((k >> 1) & 1), 2)
            pc = lax.rem(c + (k & 1), 2)
            peer = 4 * px + 2 * py + pc
            remote.append(pltpu.make_async_remote_copy(
                src_ref=src_win(src_refs[i], peer), dst_ref=dst_win(out_refs[i], me),
                send_sem=send_sems.at[i, k - 1], recv_sem=recv_sems.at[i, k - 1],
                device_id=(px, py, pc), device_id_type=MESH))
    return local, remote


def _when(cond, fn):
    if cond is True:
        fn()
    else:
        pl.when(cond)(fn)


def exchange_start(items, src_refs, out_refs, sems, cond):
    def start():
        local, remote = _exchange_copies(items, src_refs, out_refs, sems)
        for cp in local + remote:
            cp.start()
    _when(cond, start)


def exchange_finish(items, src_refs, out_refs, sems, cond):
    def finish():
        local, remote = _exchange_copies(items, src_refs, out_refs, sems)
        for cp in remote:
            cp.wait_send()
        for cp in remote:
            cp.wait_recv()
        for cp in local:
            cp.wait()
    _when(cond, finish)


def exchange_in_body(items, src_refs, out_refs, sems, first, last):
    exchange_start(items, src_refs, out_refs, sems, first)
    exchange_finish(items, src_refs, out_refs, sems, last)


def gather_blocks(src, *, name):
    return exchange([(src, _whole, (N_DEV,) + src.shape, _slot)], name=name)[0]


def gather_blocks_two_level(src, *, name):
    def body(src_ref, out_ref, send_sems, recv_sems, local_sem):
        x = lax.axis_index("x")
        y = lax.axis_index("y")
        c = lax.axis_index("c")
        me, sibling = (x, y, c), (x, y, 1 - c)
        chips = [(1 - x, y), (x, 1 - y), (1 - x, 1 - y)]

        def slot(px, py, pc):
            return out_ref.at[4 * px + 2 * py + pc]

        def copy(k, block, to, src=None):
            return pltpu.make_async_remote_copy(
                src_ref=slot(*block) if src is None else src, dst_ref=slot(*block),
                send_sem=send_sems.at[k], recv_sem=recv_sems.at[k], device_id=to, device_id_type=MESH)

        mine = pltpu.make_async_copy(src_ref, slot(*me), local_sem)
        mine.start()
        first = [copy(0, me, sibling, src=src_ref)]
        first += [copy(1 + j, me, (*chip, c), src=src_ref) for j, chip in enumerate(chips)]
        for cp in first:
            cp.start()
        passed = [copy(4 + j, (*chip, c), sibling) for j, chip in enumerate(chips)]
        for j, chip in enumerate(chips):
            copy(1 + j, (*chip, c), me).wait_recv()
            passed[j].start()
        copy(0, sibling, me).wait_recv()
        for j, chip in enumerate(chips):
            copy(4 + j, (*chip, 1 - c), me).wait_recv()
        for cp in first + passed:
            cp.wait_send()
        mine.wait()

    return pl.pallas_call(
        body, name=name,
        out_shape=jax.ShapeDtypeStruct((N_DEV,) + src.shape, src.dtype),
        in_specs=[pl.BlockSpec(memory_space=pl.ANY)],
        out_specs=pl.BlockSpec(memory_space=pl.ANY),
        scratch_shapes=[pltpu.SemaphoreType.DMA((N_DEV - 1,)), pltpu.SemaphoreType.DMA((N_DEV - 1,)),
                        pltpu.SemaphoreType.DMA(())],
    )(src)


def sum_devices(g, *, tr, name):
    _, R, C = g.shape

    def body(g_ref, o_ref):
        acc = g_ref[0].astype(F32)
        for j in range(1, N_DEV):
            acc = acc + g_ref[j].astype(F32)
        o_ref[...] = acc

    return pl.pallas_call(
        body, name=name, grid=(R // tr,),
        out_shape=jax.ShapeDtypeStruct((R, C), F32),
        in_specs=[pl.BlockSpec((N_DEV, tr, C), lambda i: (0, i, 0))],
        out_specs=pl.BlockSpec((tr, C), lambda i: (i, 0)),
        compiler_params=_params(("parallel",)),
    )(g)


def matmul(a, b, mode, out_dtype, *, name, tm=1024, tn=1024, tk=1024, n=None, comm=None,
           a_pro=None, epi=None, epi_ins=()):
    if mode == "nn":
        (M, K), (K2, N) = a.shape, b.shape
    elif mode == "nt":
        (M, K), (N, K2) = a.shape, b.shape
    else:
        (K, M), (K2, N) = a.shape, b.shape
    assert K == K2
    N = N if n is None else n
    tm, tn, tk = min(tm, M), min(tn, N), min(tk, K)
    assert M % tm == 0 and N % tn == 0 and K % tk == 0, (name, M, N, K, tm, tn, tk)
    nk = K // tk
    if mode == "tn":
        a_spec = pl.BlockSpec((tk, tm), lambda i, j, k: (k, i))
    else:
        a_spec = pl.BlockSpec((tm, tk), lambda i, j, k: (i, k))
    if mode == "nt":
        b_spec = pl.BlockSpec((tn, tk), lambda i, j, k: (j, k))
    else:
        b_spec = pl.BlockSpec((tk, tn), lambda i, j, k: (k, j))
    dims = {"nn": (1, 0), "nt": (1, 1), "tn": (0, 0)}[mode]
    items = list(comm) if comm else []
    nx = len(items)
    ne = len(epi_ins)
    gm, gn = M // tm, N // tn
    any_spec = pl.BlockSpec(memory_space=pl.ANY)
    o_spec = pl.BlockSpec((tm, tn), lambda i, j, k: (i, j))

    def body(*refs):
        a_ref, b_ref, e_refs = refs[0], refs[1], refs[2:2 + ne]
        refs = refs[2 + ne:]
        src_refs, o_ref, out_refs = refs[:nx], refs[nx], refs[1 + nx:1 + 2 * nx]
        acc_ref, sems = refs[1 + 2 * nx], refs[2 + 2 * nx:]
        i, j, k = pl.program_id(0), pl.program_id(1), pl.program_id(2)
        if items:
            exchange_start(items, src_refs, out_refs, sems, (i == 0) & (j == 0) & (k == 0))
        a_tile = a_ref[...] if a_pro is None else a_pro(a_ref[...])
        part = lax.dot_general(a_tile, b_ref[...], (((dims[0],), (dims[1],)), ((), ())),
                               preferred_element_type=F32)

        def finish(acc):
            if epi is not None:
                acc = epi(acc, *[e[...] for e in e_refs])
            o_ref[...] = acc.astype(o_ref.dtype)

        if nk == 1:
            finish(part)
        else:
            @pl.when(k == 0)
            def _():
                acc_ref[...] = part

            @pl.when((k > 0) & (k < nk - 1))
            def _():
                acc_ref[...] += part

            @pl.when(k == nk - 1)
            def _():
                finish(acc_ref[...] + part)

        if items:
            exchange_finish(items, src_refs, out_refs, sems, (i == gm - 1) & (j == gn - 1) & (k == nk - 1))

    res = pl.pallas_call(
        body, name=name, grid=(gm, gn, nk),
        out_shape=[jax.ShapeDtypeStruct((M, N), out_dtype)] + exchange_out_shapes(items),
        in_specs=[a_spec, b_spec] + [o_spec] * ne + [any_spec] * nx,
        out_specs=[o_spec] + [any_spec] * nx,
        scratch_shapes=[pltpu.VMEM((tm, tn) if nk > 1 else (8, 128), F32)]
        + (exchange_semaphores(items) if items else []),
        compiler_params=_params(("arbitrary",) * 3 if items else ("parallel", "parallel", "arbitrary")),
    )(a, b, *epi_ins, *[it[0] for it in items])
    return res if items else res[0]


def rowwise_call(body_fn, rows, fulls, row_outs, acc_outs, *, tm, name, mm=None, tk=1024):
    rows = [r if isinstance(r, tuple) else (r, r.shape[1], 0) for r in rows]
    T = rows[0][0].shape[0]
    tm = min(tm, T)
    assert T % tm == 0
    n_r, n_f, n_ro = len(rows), len(fulls), len(row_outs)
    into = [(k, ro) for k, ro in enumerate(row_outs) if len(ro) == 3]
    n_b = len(into)
    n_mm, nk = 0, 1
    if mm is not None:
        a, b, mode, pos, a_pro = mm
        n_mm = 2
        K = a.shape[1]
        N = b.shape[1] if mode == "nn" else b.shape[0]
        tk = min(tk, K)
        assert K % tk == 0 and a.shape[0] == T
        nk = K // tk
        b_contract = 0 if mode == "nn" else 1

    def row_body(refs, product):
        r_refs = refs[:n_r]
        f_refs = refs[n_r:n_r + n_f]
        refs = refs[n_r + n_f + n_b:]
        ro_refs = refs[:n_ro]
        ao_refs = refs[n_ro:n_ro + len(acc_outs)]
        r_vals = [r[...].astype(F32) for r in r_refs]
        if product is not None:
            r_vals.insert(pos, product)
        f_vals = [f[...].astype(F32) for f in f_refs]
        ro, ao = body_fn(r_vals, f_vals)
        for ref, v in zip(ro_refs, ro):
            ref[...] = v.astype(ref.dtype)
        if ao_refs:
            @pl.when(pl.program_id(0) == 0)
            def _():
                for ref in ao_refs:
                    ref[...] = jnp.zeros(ref.shape, F32)
            for ref, v in zip(ao_refs, ao):
                ref[...] += v.reshape(ref.shape)

    def body(*refs):
        if mm is None:
            return row_body(refs, None)
        a_ref, b_ref, rest, acc_ref = refs[0], refs[1], refs[2:-1], refs[-1]
        k = pl.program_id(1)
        a_tile = a_ref[...] if a_pro is None else a_pro(a_ref[...])
        part = lax.dot_general(a_tile, b_ref[...], (((1,), (b_contract,)), ((), ())),
                               preferred_element_type=F32)
        if nk == 1:
            return row_body(rest, part)

        @pl.when(k == 0)
        def _():
            acc_ref[...] = part

        @pl.when((k > 0) & (k < nk - 1))
        def _():
            acc_ref[...] += part

        @pl.when(k == nk - 1)
        def _():
            row_body(rest, acc_ref[...] + part)

    def full_spec(shape):
        nd = len(shape)
        return pl.BlockSpec(tuple(shape), lambda i, *_: (0,) * nd)

    def row_spec(w, off):
        return pl.BlockSpec((tm, w), functools.partial(lambda i, *_, o: (i, o), o=off // w))

    in_specs = []
    if mm is not None:
        in_specs.append(pl.BlockSpec((tm, tk), lambda i, k: (i, k)))
        in_specs.append(pl.BlockSpec((tk, N), lambda i, k: (k, 0)) if mode == "nn" else
                        pl.BlockSpec((N, tk), lambda i, k: (0, k)))
    in_specs += [row_spec(w, off) for (_, w, off) in rows]
    in_specs += [full_spec(f.shape) for f in fulls]
    in_specs += [pl.BlockSpec(memory_space=pl.ANY)] * n_b
    out_specs, out_shape = [], []
    for ro in row_outs:
        if len(ro) == 3:
            buf, w, off = ro
            out_specs.append(row_spec(w, off))
            out_shape.append(jax.ShapeDtypeStruct(buf.shape, buf.dtype))
        else:
            w, dt = ro
            out_specs.append(row_spec(w, 0))
            out_shape.append(jax.ShapeDtypeStruct((T, w), dt))
    out_specs += [full_spec(s) for s in acc_outs]
    out_shape += [jax.ShapeDtypeStruct(tuple(s), F32) for s in acc_outs]
    aliases = {n_mm + n_r + n_f + b: k for b, (k, _) in enumerate(into)}
    return pl.pallas_call(
        body, name=name, grid=(T // tm,) if mm is None else (T // tm, nk),
        out_shape=out_shape, in_specs=in_specs, out_specs=out_specs,
        scratch_shapes=[] if mm is None else [pltpu.VMEM((tm, N) if nk > 1 else (8, 128), F32)],
        input_output_aliases=aliases,
        compiler_params=_params(("arbitrary",) if mm is None else ("arbitrary", "arbitrary")),
    )(*([] if mm is None else [a, b]), *[r[0] for r in rows], *fulls, *[ro[0] for _, ro in into])


def fwd_body(fn):
    return lambda r, f: (fn(*r, *f), ())


def bwd_body(fn, n_rows):
    def body(r, f):
        ins, cots = r[:n_rows], r[n_rows:]
        _, vjp = jax.vjp(fn, *ins, *f)
        g = vjp(tuple(cots))
        return g[:n_rows], g[n_rows:]
    return body


def whole_call(fn, ins, outs, *, name):
    n_in = len(ins)

    def body(*refs):
        res = fn(*[r[...] for r in refs[:n_in]])
        for ref, v in zip(refs[n_in:], res):
            ref[...] = v.astype(ref.dtype)

    return pl.pallas_call(
        body, name=name,
        out_shape=[jax.ShapeDtypeStruct(tuple(s), dt) for (s, dt) in outs],
        compiler_params=_params(),
    )(*ins)


def fn_modulate(x, sc, sh):
    return (_rms(x) * (1.0 + sc) + sh,)


def fn_sgu(u, v, nw, ws, bs):
    ug = jax.nn.gelu(u)
    vn = _rms(jax.nn.gelu(v)) * nw
    ri = lax.broadcasted_iota(jnp.int32, (Q, Q), 0)
    ci = lax.broadcasted_iota(jnp.int32, (Q, Q), 1)
    causal = ri >= ci
    chunks = []
    for n in range(u.shape[0] // Q):
        vc = vn[n * Q:(n + 1) * Q]
        cols = [dot_nn(jnp.where(causal, ws[g], 0.0), vc[:, g * Q:(g + 1) * Q]) + bs[g]
                for g in range(GM_GROUPS)]
        chunks.append(jnp.concatenate(cols, axis=1))
    sv = chunks[0] if len(chunks) == 1 else jnp.concatenate(chunks, axis=0)
    return (ug * sv,)


def fn_mix(ga, gb, pa, pb):
    return (jax.nn.sigmoid(ga) * pa + jax.nn.sigmoid(gb) * pb,)


def fn_res_modulate(x, o, g1, sc2, sh2):
    x1 = x + g1 * o
    return x1, _rms(x1) * (1.0 + sc2) + sh2


def relu2_tile(f):
    return jnp.square(jnp.maximum(f.astype(F32), 0.0)).astype(BF16)


def relu2_grad_tile(dact, f):
    return dact * (2.0 * jnp.maximum(f.astype(F32), 0.0))


def final_body(r, f):
    x1, gf, tgt = r
    g2, fnw = f

    def loss_fn(x1, gf, g2, fnw):
        y = _rms(x1 + g2 * gf) * fnw
        row = 0.5 * jnp.mean(jnp.square(y - tgt), axis=-1, keepdims=True)
        return jnp.sum(row, axis=0, keepdims=True)

    l, vjp = jax.vjp(loss_fn, x1, gf, g2, fnw)
    dx1, dgf, dg2, dfnw = vjp(jnp.ones((1, 1), F32))
    return (dx1, dgf), (jnp.broadcast_to(l, (1, 128)), dg2, dfnw)


def grad_x_body(r, f):
    x, dh, dxa = r
    _, vjp = jax.vjp(fn_modulate, x, *f)
    dx, dsc, dsh = vjp((dh,))
    return (dx + dxa,), (dsc, dsh)


CONV_CW = 128
CONV_PAD = 8
CONV_ROWS = 128


def _conv_pre(xp, w_ref, b_ref, r0, R):
    acc = b_ref[...] + w_ref[0:1, :] * xp[r0 + CONV_PAD - 3:r0 + CONV_PAD - 3 + R, :]
    for k in range(1, CONV_K):
        s = r0 + CONV_PAD - 3 + k
        acc = acc + w_ref[k:k + 1, :] * xp[s:s + R, :]
    return acc


def conv_fwd(proj, conv_w, conv_b):
    T = proj.shape[0]
    R = min(CONV_ROWS, T)

    def body(x_ref, w_ref, b_ref, o_ref, xp):
        xp[0:CONV_PAD, :] = jnp.zeros((CONV_PAD, CONV_CW), F32)
        xp[CONV_PAD:CONV_PAD + T, :] = x_ref[...].astype(F32)
        for r0 in range(0, T, R):
            pre = _conv_pre(xp, w_ref, b_ref, r0, R)
            o_ref[r0:r0 + R, :] = (pre * jax.nn.sigmoid(pre)).astype(o_ref.dtype)

    return pl.pallas_call(
        body, name="conv_fwd", grid=(CONV_DIM // CONV_CW,),
        out_shape=jax.ShapeDtypeStruct((T, CONV_DIM), BF16),
        in_specs=[pl.BlockSpec((T, CONV_CW), lambda j: (0, P_XBC // CONV_CW + j)),
                  pl.BlockSpec((CONV_K, CONV_CW), lambda j: (0, j)),
                  pl.BlockSpec((1, CONV_CW), lambda j: (0, j))],
        out_specs=pl.BlockSpec((T, CONV_CW), lambda j: (0, j)),
        scratch_shapes=[pltpu.VMEM((T + CONV_PAD, CONV_CW), F32)],
        compiler_params=_params(("parallel",)),
    )(proj, conv_w, conv_b)


def conv_bwd(proj, dact, col0, conv_w, conv_b, dproj, *, name):
    T = proj.shape[0]
    R = min(CONV_ROWS, T)
    nb = dact.shape[1] // CONV_CW
    c0 = col0 // CONV_CW
    x0 = (P_XBC + col0) // CONV_CW

    def body(x_ref, d_ref, w_ref, b_ref, _, dx_ref, dw_ref, db_ref, xp, dp):
        xp[0:CONV_PAD, :] = jnp.zeros((CONV_PAD, CONV_CW), F32)
        xp[CONV_PAD:CONV_PAD + T, :] = x_ref[...].astype(F32)
        dp[T:T + CONV_PAD, :] = jnp.zeros((CONV_PAD, CONV_CW), F32)
        dws = [jnp.zeros((1, CONV_CW), F32) for _ in range(CONV_K)]
        db = jnp.zeros((1, CONV_CW), F32)
        for r0 in range(0, T, R):
            pre = _conv_pre(xp, w_ref, b_ref, r0, R)
            s = jax.nn.sigmoid(pre)
            dpre = d_ref[r0:r0 + R, :].astype(F32) * (s * (1.0 + pre * (1.0 - s)))
            dp[r0:r0 + R, :] = dpre
            db = db + jnp.sum(dpre, axis=0, keepdims=True)
            for k in range(CONV_K):
                st = r0 + CONV_PAD - 3 + k
                dws[k] = dws[k] + jnp.sum(dpre * xp[st:st + R, :], axis=0, keepdims=True)
        for r0 in range(0, T, R):
            acc = w_ref[0:1, :] * dp[r0 + 3:r0 + 3 + R, :]
            for k in range(1, CONV_K):
                acc = acc + w_ref[k:k + 1, :] * dp[r0 + 3 - k:r0 + 3 - k + R, :]
            dx_ref[r0:r0 + R, :] = acc.astype(dx_ref.dtype)
        for k in range(CONV_K):
            dw_ref[k:k + 1, :] = dws[k]
        db_ref[...] = db

    return pl.pallas_call(
        body, name=name, grid=(nb,),
        out_shape=[jax.ShapeDtypeStruct(dproj.shape, dproj.dtype),
                   jax.ShapeDtypeStruct((CONV_K, nb * CONV_CW), F32),
                   jax.ShapeDtypeStruct((1, nb * CONV_CW), F32)],
        in_specs=[pl.BlockSpec((T, CONV_CW), lambda j: (0, x0 + j)),
                  pl.BlockSpec((T, CONV_CW), lambda j: (0, j)),
                  pl.BlockSpec((CONV_K, CONV_CW), lambda j: (0, c0 + j)),
                  pl.BlockSpec((1, CONV_CW), lambda j: (0, c0 + j)),
                  pl.BlockSpec(memory_space=pl.ANY)],
        out_specs=[pl.BlockSpec((T, CONV_CW), lambda j: (0, x0 + j)),
                   pl.BlockSpec((CONV_K, CONV_CW), lambda j: (0, j)),
                   pl.BlockSpec((1, CONV_CW), lambda j: (0, j))],
        scratch_shapes=[pltpu.VMEM((T + CONV_PAD, CONV_CW), F32),
                        pltpu.VMEM((T + CONV_PAD, CONV_CW), F32)],
        input_output_aliases={4: 0},
        compiler_params=_params(("parallel",)),
    )(proj, dact, conv_w, conv_b, dproj)


def _split3(a):
    hi = a.astype(BF16)
    r = a - hi.astype(F32)
    mid = r.astype(BF16)
    return hi, mid, (r - mid.astype(F32)).astype(BF16)


def _dg3(a, m, ca, cm, a_first):
    dims = (((ca,), (cm,)), ((), ())) if a_first else (((cm,), (ca,)), ((), ()))
    out = None
    for p in _split3(a):
        t = lax.dot_general(p, m, dims, preferred_element_type=F32) if a_first else \
            lax.dot_general(m, p, dims, preferred_element_type=F32)
        out = t if out is None else out + t
    return out


@jax.custom_vjp
def exact_right(a, m):
    return _dg3(a, m, 1, 0, True)


@jax.custom_vjp
def exact_left(m, a):
    return _dg3(a, m, 0, 1, False)


exact_right.defvjp(lambda a, m: (exact_right(a, m), m),
                   lambda m, g: (dot_nt(g, m), jnp.zeros_like(m)))
exact_left.defvjp(lambda m, a: (exact_left(m, a), m),
                  lambda m, g: (jnp.zeros_like(m), _dg3(g, m, 0, 0, False)))


def ssd_step(lane0, state, x, z, dtr, Bm, Cm, dtb, alog, dsk, nw):
    def iota(shape, dim):
        return lax.broadcasted_iota(jnp.int32, shape, dim)

    def one_hot(mask):
        return mask.astype(F32).astype(BF16)

    causal = iota((Q, Q), 0) >= iota((Q, Q), 1)
    eye = iota((Q, Q), 0) == iota((Q, Q), 1)
    lane = iota((1, 128), 1)
    colh = lax.shift_right_logical(iota((1, SSM_GW), 1), 6)
    to_cols = one_hot(iota((128, SSM_GW), 0) == lane0 + colh)

    dt_all = _softplus(dtr + dtb)
    a_all = dt_all * (-jnp.exp(alog))
    cum_all = exact_left(one_hot(causal), a_all)
    both = exact_right(jnp.concatenate([dt_all, cum_all], axis=0), to_cols)
    dt_f, cum_f = both[:Q], both[Q:]
    last_f = jnp.sum(jnp.where(iota((Q, 1), 0) == Q - 1, cum_f, 0.0), axis=0, keepdims=True)
    dsk_f = jnp.zeros((1, SSM_GW), F32)
    for h in range(SSM_HPG):
        dsk_f = jnp.where(colh == h, dsk[h], dsk_f)

    xdt = x * dt_f
    cb = dot_nt(Cm, Bm)
    ms, rhs = [], []
    for h in range(SSM_HPG):
        ch = jnp.sum(jnp.where(lane == lane0 + h, cum_all, 0.0), axis=1, keepdims=True)
        ch_t = jnp.sum(jnp.where(eye, ch, 0.0), axis=0, keepdims=True)
        ms.append(cb * jnp.exp(jnp.where(causal, ch - ch_t, -1e30)))
        rhs.append(jnp.where(colh == h, xdt, 0.0))
    y = dot_nn(jnp.concatenate(ms, axis=1), jnp.concatenate(rhs, axis=0))
    y = y + dot_nn(Cm, state) * jnp.exp(cum_f) + x * dsk_f
    new_state = state * jnp.exp(last_f) + dot_tn(Bm, xdt * jnp.exp(last_f - cum_f))
    gated = y * (z * jax.nn.sigmoid(z))
    return new_state, _rms(gated) * nw


SSD_GPS = 4
_XW = SSD_GPS * SSM_GW
_BW = SSD_GPS * 128


def _ssd_in_specs(rev, nc):
    def n_of(n):
        return nc - 1 - n if rev else n
    return [
        pl.BlockSpec((Q, _XW), lambda g, n: (n_of(n), g)),
        pl.BlockSpec((Q, _BW), lambda g, n: (n_of(n), SSM_INNER // _BW + g)),
        pl.BlockSpec((Q, _BW), lambda g, n: (n_of(n), (SSM_INNER + SSM_GROUPS * 128) // _BW + g)),
        pl.BlockSpec((Q, _XW), lambda g, n: (n_of(n), P_Z // _XW + g)),
        pl.BlockSpec((Q, 128), lambda g, n: (n_of(n), 0)),
        pl.BlockSpec((1, 128), lambda g, n: (0, 0)),
        pl.BlockSpec((1, 128), lambda g, n: (0, 0)),
        pl.BlockSpec((SSD_GPS, SSM_HPG, 1, 1), lambda g, n: (g, 0, 0, 0)),
        pl.BlockSpec((1, _XW), lambda g, n: (0, g)),
    ]


def _ssd_group_inputs(gi, x_ref, b_ref, c_ref, z_ref, dt_ref, dtb_ref, al_ref, dk_ref, nw_ref):
    xs = slice(gi * SSM_GW, (gi + 1) * SSM_GW)
    bs = slice(gi * 128, (gi + 1) * 128)
    return (x_ref[:, xs].astype(F32), z_ref[:, xs].astype(F32), dt_ref[...],
            b_ref[:, bs], c_ref[:, bs],
            dtb_ref[...], al_ref[...], dk_ref[gi], nw_ref[:, xs])


def ssd_fwd(xact, proj, dtg, dtb, alog, dsk, nw, comm):
    T = xact.shape[0]
    nc = T // Q
    nx = len(comm)
    ng = SSM_GROUPS // SSD_GPS
    any_spec = pl.BlockSpec(memory_space=pl.ANY)

    def body(*refs):
        in_refs, src_refs = refs[:9], refs[9:9 + nx]
        yb_ref, st_ref = refs[9 + nx:11 + nx]
        out_refs, state, sems = refs[11 + nx:11 + 2 * nx], refs[11 + 2 * nx], refs[12 + 2 * nx:]
        g, n = pl.program_id(0), pl.program_id(1)
        exchange_start(comm, src_refs, out_refs, sems, (g == 0) & (n == 0))

        @pl.when(n == 0)
        def _():
            state[...] = jnp.zeros(state.shape, F32)

        for gi in range(SSD_GPS):
            lane0 = SSM_HPG * (SSD_GPS * pl.program_id(0) + gi)
            s = state[gi]
            st_ref[gi, 0] = s
            new_s, yb = ssd_step(lane0, s, *_ssd_group_inputs(gi, *in_refs))
            state[gi] = new_s
            yb_ref[:, gi * SSM_GW:(gi + 1) * SSM_GW] = yb.astype(yb_ref.dtype)

        exchange_finish(comm, src_refs, out_refs, sems, (g == ng - 1) & (n == nc - 1))

    return pl.pallas_call(
        body, name="ssd_fwd", grid=(ng, nc),
        out_shape=[jax.ShapeDtypeStruct((T, SSM_INNER), BF16),
                   jax.ShapeDtypeStruct((SSM_GROUPS, nc, 128, SSM_GW), F32)] + exchange_out_shapes(comm),
        in_specs=_ssd_in_specs(False, nc) + [any_spec] * nx,
        out_specs=[pl.BlockSpec((Q, _XW), lambda g, n: (n, g)),
                   pl.BlockSpec((SSD_GPS, 1, 128, SSM_GW), lambda g, n: (g, n, 0, 0))] + [any_spec] * nx,
        scratch_shapes=[pltpu.VMEM((SSD_GPS, 128, SSM_GW), F32)] + exchange_semaphores(comm),
        compiler_params=_params(("arbitrary", "arbitrary")),
    )(xact, xact, xact, proj, dtg, dtb, alog, dsk, nw, *[it[0] for it in comm])


def ssd_bwd(xact, proj, dtg, dtb, alog, dsk, nw, states, dyb, dproj, comm):
    T = xact.shape[0]
    nc = T // Q

    nx = len(comm)
    ng = SSM_GROUPS // SSD_GPS
    any_spec = pl.BlockSpec(memory_space=pl.ANY)

    def body(*refs):
        in_refs, (st_ref, dy_ref, _) = refs[:9], refs[9:12]
        src_refs, refs = refs[12:12 + nx], refs[12 + nx:]
        dx_ref, db_ref, dc_ref, dz_ref, ddt_ref, ddtb_ref, dal_ref, ddk_ref, dnw_ref = refs[:9]
        out_refs, dstate, sems = refs[9:9 + nx], refs[9 + nx], refs[10 + nx:]
        exchange_start(comm, src_refs, out_refs, sems, (pl.program_id(0) == 0) & (pl.program_id(1) == 0))

        @pl.when(pl.program_id(1) == 0)
        def _():
            dstate[...] = jnp.zeros(dstate.shape, F32)
            ddtb_ref[...] = jnp.zeros(ddtb_ref.shape, F32)
            dal_ref[...] = jnp.zeros(dal_ref.shape, F32)
            ddk_ref[...] = jnp.zeros(ddk_ref.shape, F32)
            dnw_ref[...] = jnp.zeros(dnw_ref.shape, F32)

        for gi in range(SSD_GPS):
            xs = slice(gi * SSM_GW, (gi + 1) * SSM_GW)
            bs = slice(gi * 128, (gi + 1) * 128)
            lane0 = SSM_HPG * (SSD_GPS * pl.program_id(0) + gi)
            ins = (st_ref[gi, 0],) + _ssd_group_inputs(gi, *in_refs)
            _, vjp = jax.vjp(functools.partial(ssd_step, lane0), *ins)
            ds, dx, dz, ddt, dbm, dcm, ddtb, dal, ddk, dnw = vjp((dstate[gi], dy_ref[:, xs].astype(F32)))
            dstate[gi] = ds
            dx_ref[:, xs] = dx.astype(dx_ref.dtype)
            db_ref[:, bs] = dbm.astype(db_ref.dtype)
            dc_ref[:, bs] = dcm.astype(dc_ref.dtype)
            dz_ref[:, xs] = dz.astype(dz_ref.dtype)
            ddt_ref[gi] = ddt
            ddtb_ref[gi] += ddtb
            dal_ref[gi] += dal
            ddk_ref[gi] += ddk
            dnw_ref[:, xs] += dnw

        exchange_finish(comm, src_refs, out_refs, sems,
                        (pl.program_id(0) == ng - 1) & (pl.program_id(1) == nc - 1))

    rev = lambda n: nc - 1 - n
    row_shape = jax.ShapeDtypeStruct((SSM_GROUPS, 1, 128), F32)
    row_spec = pl.BlockSpec((SSD_GPS, 1, 128), lambda g, n: (g, 0, 0))
    return pl.pallas_call(
        body, name="ssd_bwd", grid=(ng, nc),
        out_shape=[jax.ShapeDtypeStruct((T, SSM_INNER), BF16),
                   jax.ShapeDtypeStruct((T, SSM_GROUPS * 128), BF16),
                   jax.ShapeDtypeStruct((T, SSM_GROUPS * 128), BF16),
                   jax.ShapeDtypeStruct(dproj.shape, dproj.dtype),
                   jax.ShapeDtypeStruct((SSM_GROUPS, T, 128), F32),
                   row_shape, row_shape,
                   jax.ShapeDtypeStruct((SSM_GROUPS, SSM_HPG, 1, 1), F32),
                   jax.ShapeDtypeStruct((1, SSM_INNER), F32)] + exchange_out_shapes(comm),
        in_specs=_ssd_in_specs(True, nc) + [
            pl.BlockSpec((SSD_GPS, 1, 128, SSM_GW), lambda g, n: (g, rev(n), 0, 0)),
            pl.BlockSpec((Q, _XW), lambda g, n: (rev(n), g)),
            any_spec] + [any_spec] * nx,
        out_specs=[pl.BlockSpec((Q, _XW), lambda g, n: (rev(n), g)),
                   pl.BlockSpec((Q, _BW), lambda g, n: (rev(n), g)),
                   pl.BlockSpec((Q, _BW), lambda g, n: (rev(n), g)),
                   pl.BlockSpec((Q, _XW), lambda g, n: (rev(n), P_Z // _XW + g)),
                   pl.BlockSpec((SSD_GPS, Q, 128), lambda g, n: (g, rev(n), 0)),
                   row_spec, row_spec,
                   pl.BlockSpec((SSD_GPS, SSM_HPG, 1, 1), lambda g, n: (g, 0, 0, 0)),
                   pl.BlockSpec((1, _XW), lambda g, n: (0, g))] + [any_spec] * nx,
        scratch_shapes=[pltpu.VMEM((SSD_GPS, 128, SSM_GW), F32)] + exchange_semaphores(comm),
        input_output_aliases={11: 3},
        compiler_params=_params(("arbitrary", "arbitrary")),
    )(xact, xact, xact, proj, dtg, dtb, alog, dsk, nw, states, dyb, dproj, *[it[0] for it in comm])


ADAMW_WHOLE_ELEMS = 256 * 1024


def adamw(w, g, m, v, *, name):
    shape = w.shape
    parts = g.shape != shape
    nd = len(shape)
    if nd == 3 and shape[1] == 1 and w.size > ADAMW_WHOLE_ELEMS:
        assert not parts and shape[0] % 4 == 0
        grid = (4,)
        spec = g_spec = pl.BlockSpec((shape[0] // 4, 1, shape[2]), lambda i: (i, 0, 0))
    else:
        if w.size <= ADAMW_WHOLE_ELEMS:
            grid, tr = (1,), shape[-2]
        else:
            assert all(s == 1 for s in shape[:-2]) and shape[-2] % 256 == 0
            grid, tr = (shape[-2] // 256,), 256
        blk = tuple(shape[:-2]) + (tr, shape[-1])
        spec = pl.BlockSpec(blk, lambda i: (0,) * (nd - 2) + (i, 0))
        g_spec = pl.BlockSpec((N_DEV,) + blk[1:], lambda i: (0,) * (nd - 2) + (i, 0)) if parts else spec

    def body(w_ref, g_ref, m_ref, v_ref, go_ref, d_ref, nm_ref, nv_ref):
        if parts:
            g = g_ref[0:1].astype(F32)
            for j in range(1, N_DEV):
                g = g + g_ref[j:j + 1].astype(F32)
        else:
            g = g_ref[...]
        nm = ADAM_B1 * m_ref[...] + (1.0 - ADAM_B1) * g
        nv = ADAM_B2 * v_ref[...] + (1.0 - ADAM_B2) * jnp.square(g)
        m_hat = nm / (1.0 - ADAM_B1 ** ADAM_STEP)
        v_hat = nv / (1.0 - ADAM_B2 ** ADAM_STEP)
        go_ref[...] = g
        d_ref[...] = -ADAM_LR * (m_hat / (jnp.sqrt(v_hat) + ADAM_EPS) + ADAM_WD * w_ref[...])
        nm_ref[...] = nm
        nv_ref[...] = nv

    shp = jax.ShapeDtypeStruct(shape, F32)
    return pl.pallas_call(
        body, name=name, grid=grid,
        out_shape=[shp] * 4, in_specs=[spec, g_spec, spec, spec], out_specs=[spec] * 4,
        compiler_params=_params(("parallel",)),
    )(w, g, m, v)


def _pad_rows(a, rows):
    return jnp.pad(a, ((0, rows - a.shape[0]), (0, 0)))


WIN_W = 1408
N_IN = IN_WIDTH // N_DEV
_A6 = OFF_DT - 6 * N_IN
_C6 = 7 * N_IN - OFF_GA


def _win_offset(me):
    return jnp.where(me == 7, 124, 4 * me)


def _w_in_window(shard, me):
    rows = shard.shape[0]
    z = lambda n: jnp.zeros((rows, n), shard.dtype)
    a = lax.dynamic_update_slice(z(WIN_W), shard, (0, _win_offset(me)))
    b = jnp.concatenate([z(24), shard[:, :_A6], shard[:, _A6 + 32:], z(4), shard[:, _A6:_A6 + 32], z(96)], axis=1)
    return jnp.where(me == 6, b, a)


def _w_in_from_window(window, me):
    a = lax.dynamic_slice(window, (0, _win_offset(me)), (window.shape[0], N_IN))
    b = jnp.concatenate([window[:, 24:24 + _A6], window[:, 1280:1312], window[:, 24 + _A6:24 + _A6 + _C6]], axis=1)
    return jnp.where(me == 6, b, a)


def _w_all_from_windows(g):
    def merge_first(p, t):
        return jnp.concatenate([p[:, :128] + t, p[:, 128:]], axis=1)

    parts = [g[0][:, :1280]]
    for j in range(1, 6):
        parts.append(merge_first(g[j][:, :1280], g[j - 1][:, 1280:]))
    p6 = merge_first(g[6][:, :1280], g[5][:, 1280:])
    parts.append(jnp.concatenate([p6[:, :1152], p6[:, 1152:] + g[7][:, :128]], axis=1))
    parts.append(g[7][:, 128:])
    parts.append(g[6][:, 1280:])
    return jnp.concatenate(parts, axis=1)


def _windows_of_w_all(gw):
    wins = [gw[:, 1280 * j:1280 * j + WIN_W] for j in range(6)]
    wins.append(jnp.concatenate([gw[:, 7680:8960], gw[:, PROJ_W:]], axis=1))
    wins.append(gw[:, 8832:PROJ_W])
    return jnp.stack(wins)


def kernel(x, c, w_mod, b_mod, w_in, gm_norm_w, gm_ws, gm_bs, conv_w, conv_b, dt_bias, a_log, d_skip, ssm_norm_w, w_branch_gm, w_branch_ssm, w_out, w_ff1, w_ff2, final_norm_w, loss_target, m_w_mod, m_b_mod, m_w_in, m_gm_norm_w, m_gm_ws, m_gm_bs, m_conv_w, m_conv_b, m_dt_bias, m_a_log, m_d_skip, m_ssm_norm_w, m_w_branch_gm, m_w_branch_ssm, m_w_out, m_w_ff1, m_w_ff2, m_final_norm_w, v_w_mod, v_b_mod, v_w_in, v_gm_norm_w, v_gm_ws, v_gm_bs, v_conv_w, v_conv_b, v_dt_bias, v_a_log, v_d_skip, v_ssm_norm_w, v_w_branch_gm, v_w_branch_ssm, v_w_out, v_w_ff1, v_w_ff2, v_final_norm_w):
    T = x.shape[1]
    me = 4 * lax.axis_index("x") + 2 * lax.axis_index("y") + lax.axis_index("c")
    x2 = x[0]
    tgt = loss_target[0]
    n_in = IN_WIDTH // N_DEV
    n_mod = N_MOD * D // N_DEV
    n_cv = CONV_DIM // N_DEV

    c_all, conv_w_full = exchange(
        [(c.reshape(8, 128), _whole, (N_DEV, 8, 128), _slot),
         (conv_w[0], _whole, (N_DEV, CONV_K, n_cv), _slot)], name="gather_c_convw")
    c_all = c_all.reshape(N_DEV, D)
    conv_w_full = conv_w_full.transpose(1, 0, 2).reshape(CONV_K, CONV_DIM)

    win = _w_in_window(w_in[0].astype(BF16), me)
    gwin = gather_blocks_two_level(win, name="gather_w_in")
    late_weights = [
        (w_branch_gm[0].astype(BF16), _whole, (D, D), _rows(D // N_DEV)),
        (w_branch_ssm[0].astype(BF16), _whole, (SSM_INNER, D), _rows(SSM_INNER // N_DEV)),
        (w_out[0].astype(BF16), _whole, (D, D), _rows(D // N_DEV)),
        (w_ff1[0].astype(BF16), _whole, (D, D_FF), _cols(D_FF // N_DEV)),
        (w_ff2[0].astype(BF16), _whole, (D_FF, D), _rows(D_FF // N_DEV))]
    w_all = _w_all_from_windows(gwin)
    w_dt = w_all[:, PROJ_W:]

    c_pad = _pad_rows(c_all, 128)
    b_mine = lax.dynamic_slice(b_mod, (0, me * n_mod), (1, n_mod))

    def mod_fn(cp, w, b):
        ca = cp * jax.nn.sigmoid(cp)
        return (jnp.dot(ca, w, precision=HIGHEST, preferred_element_type=F32) + b,)

    (mod_part,) = whole_call(mod_fn, [c_pad, w_mod[0], b_mine], [((128, n_mod), F32)], name="mod_fwd")
    gmod = gather_blocks(mod_part[:N_DEV], name="gather_mod")
    mod = lax.dynamic_index_in_dim(gmod, me, axis=1, keepdims=False).reshape(N_MOD, D)
    sh1, sc1, gt1, sh2, sc2, gt2 = [mod[i:i + 1] for i in range(N_MOD)]

    (h,) = rowwise_call(fwd_body(fn_modulate), [x2], [sc1, sh1], [(D, BF16)], [], tm=256, name="modulate1")
    proj = matmul(h, w_all, "nn", BF16, name="mm_proj", n=PROJ_W)
    dtg = matmul(h, w_dt, "nn", F32, name="mm_dt")
    ws = gm_ws[0]
    bs3 = gm_bs[0].reshape(GM_GROUPS, Q, 1)
    sgu_rows = [(proj, D, P_U), (proj, D, P_V)]
    (ya,) = rowwise_call(fwd_body(fn_sgu), sgu_rows, [gm_norm_w, ws, bs3], [(D, BF16)], [],
                         tm=256, name="sgu_fwd")
    xact = conv_fwd(proj, conv_w_full, conv_b)
    dtb4 = jnp.pad(dt_bias, ((0, 0), (0, 96)))
    alog4 = jnp.pad(a_log, ((0, 0), (0, 96)))
    dsk4 = d_skip.reshape(SSM_GROUPS, SSM_HPG, 1, 1)
    yb, states, w_gm_f, w_ssm_f, w_out_f, w_ff1_f, w_ff2_f = ssd_fwd(
        xact, proj, dtg, dtb4, alog4, dsk4, ssm_norm_w, late_weights)
    pa = matmul(ya, w_gm_f, "nn", F32, name="mm_branch_gm")
    gate_rows = [(proj, D, P_GA), (proj, D, P_GB)]
    mixed, pb = rowwise_call(
        lambda r, fl: (fn_mix(*r) + (r[3],), ()), gate_rows + [pa], [], [(D, BF16), (D, F32)], [],
        tm=FUSED_TM, name="branch_ssm_mix", mm=(yb, w_ssm_f, "nn", 3, None), tk=SSM_INNER)
    x1, h2, o = rowwise_call(
        lambda r, fl: (fn_res_modulate(*r, *fl) + (r[1],), ()), [x2], [gt1, sc2, sh2],
        [(D, F32), (D, BF16), (D, F32)], [], tm=FUSED_TM, name="out_res_modulate2",
        mm=(mixed, w_out_f, "nn", 1, None))
    f = matmul(h2, w_ff1_f, "nn", BF16, name="mm_ff1")

    dx1, dgf, loss_v, dgt2, dfnw = rowwise_call(
        final_body, [x1, tgt], [gt2, final_norm_w.reshape(1, D)], [(D, F32), (D, BF16)],
        [(1, 128), (1, D), (1, D)], tm=FUSED_TM, name="ff2_loss_bwd", mm=(f, w_ff2_f, "nn", 1, relu2_tile),
        tk=D_FF)
    df = matmul(dgf, w_ff2_f, "nt", BF16, name="mm_ff2_dgrad", epi=relu2_grad_tile, epi_ins=(f,))
    gw_ff2 = matmul(f, dgf, "tn", BF16, name="mm_ff2_wgrad", tk=WGRAD_TK, a_pro=relu2_tile)
    gw_ff1 = matmul(h2, df, "tn", BF16, name="mm_ff1_wgrad", tk=WGRAD_TK)

    def res_mod_bwd(r, fl):
        xv, ov, dx1v, dh2v = r
        _, vjp = jax.vjp(fn_res_modulate, xv, ov, *fl)
        dxv, dov, dg1, dsc, dsh = vjp((dx1v, dh2v))
        return (dxv, dov), (dg1, dsc, dsh)

    dxa, do, dgt1, dsc2, dsh2 = rowwise_call(
        res_mod_bwd, [x2, o, dx1], [gt1, sc2, sh2], [(D, F32), (D, BF16)],
        [(1, D), (1, D), (1, D)], tm=FUSED_TM, name="ff1_dgrad_res_modulate2_bwd",
        mm=(df, w_ff1_f, "nt", 3, None), tk=D_FF)
    gw_out = matmul(mixed, do, "tn", BF16, name="mm_out_wgrad", tk=WGRAD_TK)
    dproj = lax.empty((T, ALL_W), BF16)

    def mix_bwd(r, fl):
        dga, dgb, dpa, dpb = bwd_body(fn_mix, 4)(r, fl)[0]
        return (jnp.concatenate([dga, dgb], axis=1), dpa, dpb), ()

    dproj, dpa, dpb = rowwise_call(
        mix_bwd, gate_rows + [pa, pb], [], [(dproj, 2 * D, P_GA), (D, BF16), (D, BF16)], [],
        tm=FUSED_TM, name="out_dgrad_mix_bwd", mm=(do, w_out_f, "nt", 4, None))
    gw_gm = matmul(ya, dpa, "tn", BF16, name="mm_branch_gm_wgrad", tk=WGRAD_TK)
    dyb = matmul(dpb, w_ssm_f, "nt", BF16, name="mm_branch_ssm_dgrad")
    gw_ssm = matmul(yb, dpb, "tn", BF16, name="mm_branch_ssm_wgrad", tk=WGRAD_TK)

    def sgu_bwd(r, fl):
        (du, dv), acc = bwd_body(fn_sgu, 2)(r, fl)
        return (jnp.concatenate([du, dv], axis=1),), acc

    dproj, dgnw, dws, dbs = rowwise_call(
        sgu_bwd, sgu_rows, [gm_norm_w, ws, bs3], [(dproj, 2 * D, P_U)],
        [(1, D), (GM_GROUPS, Q, Q), (GM_GROUPS, Q, 1)], tm=FUSED_TM, name="branch_gm_dgrad_sgu_bwd",
        mm=(dpa, w_gm_f, "nt", 2, None))
    early_grads = [
        (gw_gm, _rows(D // N_DEV), (N_DEV, D // N_DEV, D), _slot),
        (gw_ssm, _rows(SSM_INNER // N_DEV), (N_DEV, SSM_INNER // N_DEV, D), _slot),
        (gw_out, _rows(D // N_DEV), (N_DEV, D // N_DEV, D), _slot),
        (gw_ff1, _cols(D_FF // N_DEV), (N_DEV, D, D_FF // N_DEV), _slot),
        (gw_ff2, _rows(D_FF // N_DEV), (N_DEV, D_FF // N_DEV, D), _slot),
        (_pack_rows([dgnw, dws, dbs], EARLY_ROWS), _whole, (N_DEV, sum(EARLY_ROWS), 128), _slot)]
    (dxs, dbm, dcm, dproj, ddt8, ddtb, dalog, ddsk, dsnw,
     r_gm, r_ssm, r_out, r_ff1, r_ff2, early_all) = ssd_bwd(
        xact, proj, dtg, dtb4, alog4, dsk4, ssm_norm_w, states, dyb, dproj, early_grads)
    dconv_w, dconv_b = [], []
    for nm, dact_part, col0 in (("xs", dxs, 0), ("b", dbm, SSM_INNER), ("c", dcm, SSM_INNER + SSM_GROUPS * 128)):
        dproj, dcw, dcb = conv_bwd(proj, dact_part, col0, conv_w_full, conv_b, dproj, name="conv_bwd_" + nm)
        dconv_w.append(dcw)
        dconv_b.append(dcb)
    dconv_w = jnp.concatenate(dconv_w, axis=1)
    dconv_b = jnp.concatenate(dconv_b, axis=1)
    dproj = dproj.at[:, PROJ_W:].set(jnp.sum(ddt8, axis=0).astype(BF16))
    gw_all = matmul(h, dproj, "tn", BF16, name="mm_in_wgrad", tn=1152, tk=WGRAD_TK)
    mid_pack = _pack_rows([dconv_w, dconv_b, jnp.sum(ddtb, axis=0), jnp.sum(dalog, axis=0), ddsk, dsnw, dfnw,
                           jnp.concatenate([dgt1, dsh2, dsc2, dgt2], axis=0)], MID_ROWS)
    dh, r_in, mid_all = matmul(
        dproj, w_all, "nt", F32, name="mm_in_dgrad", tk=3456,
        comm=[(_windows_of_w_all(gw_all), _slot, (N_DEV, D, WIN_W), _slot),
              (mid_pack, _whole, (N_DEV, sum(MID_ROWS), 128), _slot)])
    grad_x, dsc1, dsh1 = rowwise_call(grad_x_body, [x2, dh, dxa], [sc1, sh1], [(D, F32)],
                                      [(1, D), (1, D)], tm=256, name="modulate1_bwd")

    g_w_in = _w_in_from_window(sum_devices(r_in, tr=256, name="sum_w_in_grads"), me).reshape(1, D, n_in)

    late_all = gather_blocks(_pack_rows([dsh1, dsc1], LATE_ROWS), name="gather_dmod1")
    s_early = _unpack_rows(sum_devices(early_all, tr=early_all.shape[1], name="sum_small_early"), EARLY_ROWS)
    s_mid = _unpack_rows(sum_devices(mid_all, tr=mid_all.shape[1], name="sum_small_mid"), MID_ROWS)
    s_late = _unpack_rows(sum_devices(late_all, tr=late_all.shape[1], name="sum_small_late"), LATE_ROWS)
    g_gm_norm_w = s_early[0][:D].reshape(1, D)
    g_gm_ws = s_early[1].reshape(GM_GROUPS * Q, Q)
    g_gm_bs = s_early[2][:GM_GROUPS * Q].reshape(GM_GROUPS, Q)
    g_conv_w_full = s_mid[0].reshape(CONV_K, CONV_DIM)
    g_conv_w = lax.dynamic_slice(g_conv_w_full, (0, me * n_cv), (CONV_K, n_cv))
    g_conv_b = s_mid[1].reshape(1, CONV_DIM)
    g_dt_bias = s_mid[2][:32].reshape(1, 32)
    g_a_log = s_mid[3][:32].reshape(1, 32)
    g_d_skip = s_mid[4][:32].reshape(1, 32)
    g_ssm_norm_w = s_mid[5].reshape(1, SSM_INNER)
    g_final_norm_w = s_mid[6][:D].reshape(1, D)
    g_b_mod = jnp.concatenate([s_late[0][:D], s_late[1][:D], s_mid[7]]).reshape(1, N_MOD * D)

    dmod_all = jnp.concatenate(
        [late_all.reshape(N_DEV, -1)[:, :2 * D],
         mid_all[:, sum(MID_ROWS[:7]):].reshape(N_DEV, 4 * D)], axis=1)
    dmod_mine = _pad_rows(lax.dynamic_slice(dmod_all, (0, me * n_mod), (N_DEV, n_mod)), 128)

    def wmod_grad_fn(cp, dm):
        ca = cp * jax.nn.sigmoid(cp)
        return (lax.dot_general(ca, dm, (((0,), (0,)), ((), ())), precision=HIGHEST,
                                preferred_element_type=F32),)

    (g_w_mod,) = whole_call(wmod_grad_fn, [c_pad, dmod_mine], [((D, n_mod), F32)], name="w_mod_grad")

    upd = {}

    def step(name, w, g, m, v, parts=False):
        upd[name] = adamw(w, g if parts else g.reshape(w.shape), m, v, name="adamw_" + name)

    step("w_mod", w_mod, g_w_mod, m_w_mod, v_w_mod)
    step("b_mod", b_mod, g_b_mod, m_b_mod, v_b_mod)
    col_major = lambda a: jnp.transpose(a, (2, 0, 1))
    upd["w_in"] = tuple(jnp.transpose(o, (1, 2, 0)) for o in adamw(
        col_major(w_in), col_major(g_w_in), col_major(m_w_in), col_major(v_w_in), name="adamw_w_in"))
    step("gm_norm_w", gm_norm_w, g_gm_norm_w, m_gm_norm_w, v_gm_norm_w)
    step("gm_ws", gm_ws, g_gm_ws, m_gm_ws, v_gm_ws)
    step("gm_bs", gm_bs, g_gm_bs, m_gm_bs, v_gm_bs)
    step("conv_w", conv_w, g_conv_w, m_conv_w, v_conv_w)
    step("conv_b", conv_b, g_conv_b, m_conv_b, v_conv_b)
    step("dt_bias", dt_bias, g_dt_bias, m_dt_bias, v_dt_bias)
    step("a_log", a_log, g_a_log, m_a_log, v_a_log)
    step("d_skip", d_skip, g_d_skip, m_d_skip, v_d_skip)
    step("ssm_norm_w", ssm_norm_w, g_ssm_norm_w, m_ssm_norm_w, v_ssm_norm_w)
    step("w_branch_gm", w_branch_gm, r_gm, m_w_branch_gm, v_w_branch_gm, parts=True)
    step("w_branch_ssm", w_branch_ssm, r_ssm, m_w_branch_ssm, v_w_branch_ssm, parts=True)
    step("w_out", w_out, r_out, m_w_out, v_w_out, parts=True)
    step("w_ff1", w_ff1, r_ff1, m_w_ff1, v_w_ff1, parts=True)
    step("w_ff2", w_ff2, r_ff2, m_w_ff2, v_w_ff2, parts=True)
    step("final_norm_w", final_norm_w.reshape(1, D), g_final_norm_w, m_final_norm_w.reshape(1, D),
         v_final_norm_w.reshape(1, D))
    upd["final_norm_w"] = tuple(a.reshape(D) for a in upd["final_norm_w"])

    loss = lax.psum(loss_v[0, 0], ("x", "y", "c"))
    order = ["w_mod", "b_mod", "w_in", "gm_norm_w", "gm_ws", "gm_bs", "conv_w", "conv_b", "dt_bias", "a_log",
             "d_skip", "ssm_norm_w", "w_branch_gm", "w_branch_ssm", "w_out", "w_ff1", "w_ff2", "final_norm_w"]
    return (loss, grad_x.reshape(1, T, D),
            *[upd[n][0] for n in order], *[upd[n][1] for n in order],
            *[upd[n][2] for n in order], *[upd[n][3] for n in order])
```

```python
import functools

import jax
import jax.numpy as jnp
from jax import lax
from jax.experimental import pallas as pl
from jax.experimental.pallas import tpu as pltpu

F32 = jnp.float32
BF16 = jnp.bfloat16
MESH = pl.DeviceIdType.MESH
HIGHEST = lax.Precision.HIGHEST

N_DEV = 8
D = 1024
Q = 128
GM_GROUPS = 8
SSM_INNER = 2048
SSM_GROUPS = 8
SSM_HPG = 4
SSM_P = 64
SSM_GW = SSM_HPG * SSM_P
CONV_DIM = 4096
CONV_K = 4
D_FF = 4096
N_MOD = 6
EPS = 1e-6
IN_WIDTH = 10272
OFF_DT = 8192
OFF_GA = 8224
PROJ_W = 10240
ALL_W = 10368
P_U, P_V, P_Z, P_XBC, P_GA, P_GB = 0, 1024, 2048, 4096, 8192, 9216

ADAM_LR = 0.001
ADAM_B1 = 0.9
ADAM_B2 = 0.999
ADAM_EPS = 1e-08
ADAM_WD = 0.01
ADAM_STEP = 10

VMEM_LIMIT_BYTES = 48 * 1024 * 1024
FUSED_TM = 256
WGRAD_TK = 2048
EARLY_ROWS = (8, 1024, 8)
MID_ROWS = (128, 32, 8, 8, 8, 16, 8, 32)
LATE_ROWS = (8, 8)


def _pack_rows(arrs, rows):
    def rows128(a, r):
        a = a.reshape(-1)
        return jnp.pad(a, (0, r * 128 - a.shape[0])).reshape(r, 128)
    return jnp.concatenate([rows128(a, r) for a, r in zip(arrs, rows)], axis=0)


def _unpack_rows(s, rows):
    out, o = [], 0
    for r in rows:
        out.append(s[o:o + r].reshape(-1))
        o += r
    return out


def _params(sem=None):
    return pltpu.CompilerParams(dimension_semantics=sem, vmem_limit_bytes=VMEM_LIMIT_BYTES)


def _dg(a, b, ca, cb):
    return lax.dot_general(a.astype(BF16), b.astype(BF16), (((ca,), (cb,)), ((), ())),
                           preferred_element_type=F32)


@jax.custom_vjp
def dot_nn(a, b):
    return _dg(a, b, 1, 0)


@jax.custom_vjp
def dot_nt(a, b):
    return _dg(a, b, 1, 1)


@jax.custom_vjp
def dot_tn(a, b):
    return _dg(a, b, 0, 0)


def _like(ct, primal):
    return ct.astype(primal.dtype)


dot_nn.defvjp(lambda a, b: (dot_nn(a, b), (a, b)),
              lambda r, g: (_like(dot_nt(g, r[1]), r[0]), _like(dot_tn(r[0], g), r[1])))
dot_nt.defvjp(lambda a, b: (dot_nt(a, b), (a, b)),
              lambda r, g: (_like(dot_nn(g, r[1]), r[0]), _like(dot_tn(g, r[0]), r[1])))
dot_tn.defvjp(lambda a, b: (dot_tn(a, b), (a, b)),
              lambda r, g: (_like(dot_nt(r[1], g), r[0]), _like(dot_nn(r[0], g), r[1])))


def _rms(x):
    return x * lax.rsqrt(jnp.mean(x * x, axis=-1, keepdims=True) + EPS)


def _softplus(x):
    return jnp.maximum(x, 0.0) + jnp.log1p(jnp.exp(-jnp.abs(x)))


def _rows(n):
    return lambda ref, j: ref.at[pl.ds(pl.multiple_of(j * n, n), n)]


def _cols(n):
    return lambda ref, j: ref.at[:, pl.ds(pl.multiple_of(j * n, n), n)]


def _slot(ref, j):
    return ref.at[j]


def _whole(ref, j):
    return ref


def exchange(items, *, name):
    n = len(items)

    def body(*refs):
        exchange_in_body(items, refs[:n], refs[n:2 * n], refs[2 * n:], True, True)

    return pl.pallas_call(
        body, name=name,
        out_shape=exchange_out_shapes(items),
        in_specs=[pl.BlockSpec(memory_space=pl.ANY)] * n,
        out_specs=[pl.BlockSpec(memory_space=pl.ANY)] * n,
        scratch_shapes=exchange_semaphores(items),
    )(*[it[0] for it in items])


def exchange_out_shapes(items):
    return [jax.ShapeDtypeStruct(tuple(shape), src.dtype) for (src, _, shape, _) in items]


def exchange_semaphores(items):
    n = len(items)
    return [pltpu.SemaphoreType.DMA((n, N_DEV - 1)), pltpu.SemaphoreType.DMA((n, N_DEV - 1)),
            pltpu.SemaphoreType.DMA((n,))]


def _exchange_copies(items, src_refs, out_refs, sems):
    send_sems, recv_sems, local_sems = sems
    x = lax.axis_index("x")
    y = lax.axis_index("y")
    c = lax.axis_index("c")
    me = 4 * x + 2 * y + c
    local = [pltpu.make_async_copy(src_win(src_refs[i], me), dst_win(out_refs[i], me), local_sems.at[i])
             for i, (_, src_win, _, dst_win) in enumerate(items)]
    remote = []
    for i, (_, src_win, _, dst_win) in enumerate(items):
        for k in range(1, N_DEV):
            px = lax.rem(x + ((k >> 2) & 1), 2)
            py = lax.rem(y + ((k >> 1) & 1), 2)
            pc = lax.rem(c + (k & 1), 2)
            peer = 4 * px + 2 * py + pc
            remote.append(pltpu.make_async_remote_copy(
                src_ref=src_win(src_refs[i], peer), dst_ref=dst_win(out_refs[i], me),
                send_sem=send_sems.at[i, k - 1], recv_sem=recv_sems.at[i, k - 1],
                device_id=(px, py, pc), device_id_type=MESH))
    return local, remote


def _when(cond, fn):
    if cond is True:
        fn()
    else:
        pl.when(cond)(fn)


def exchange_start(items, src_refs, out_refs, sems, cond):
    def start():
        local, remote = _exchange_copies(items, src_refs, out_refs, sems)
        for cp in local + remote:
            cp.start()
    _when(cond, start)


def exchange_finish(items, src_refs, out_refs, sems, cond):
    def finish():
        local, remote = _exchange_copies(items, src_refs, out_refs, sems)
        for cp in remote:
            cp.wait_send()
        for cp in remote:
            cp.wait_recv()
        for cp in local:
            cp.wait()
    _when(cond, finish)


def exchange_in_body(items, src_refs, out_refs, sems, first, last):
    exchange_start(items, src_refs, out_refs, sems, first)
    exchange_finish(items, src_refs, out_refs, sems, last)


def gather_blocks(src, *, name):
    return exchange([(src, _whole, (N_DEV,) + src.shape, _slot)], name=name)[0]


def gather_blocks_two_level(src, *, name):
    def body(src_ref, out_ref, send_sems, recv_sems, local_sem):
        x = lax.axis_index("x")
        y = lax.axis_index("y")
        c = lax.axis_index("c")
        me, sibling = (x, y, c), (x, y, 1 - c)
        chips = [(1 - x, y), (x, 1 - y), (1 - x, 1 - y)]

        def slot(px, py, pc):
            return out_ref.at[4 * px + 2 * py + pc]

        def copy(k, block, to, src=None):
            return pltpu.make_async_remote_copy(
                src_ref=slot(*block) if src is None else src, dst_ref=slot(*block),
                send_sem=send_sems.at[k], recv_sem=recv_sems.at[k], device_id=to, device_id_type=MESH)

        mine = pltpu.make_async_copy(src_ref, slot(*me), local_sem)
        mine.start()
        first = [copy(0, me, sibling, src=src_ref)]
        first += [copy(1 + j, me, (*chip, c), src=src_ref) for j, chip in enumerate(chips)]
        for cp in first:
            cp.start()
        passed = [copy(4 + j, (*chip, c), sibling) for j, chip in enumerate(chips)]
        for j, chip in enumerate(chips):
            copy(1 + j, (*chip, c), me).wait_recv()
            passed[j].start()
        copy(0, sibling, me).wait_recv()
        for j, chip in enumerate(chips):
            copy(4 + j, (*chip, 1 - c), me).wait_recv()
        for cp in first + passed:
            cp.wait_send()
        mine.wait()

    return pl.pallas_call(
        body, name=name,
        out_shape=jax.ShapeDtypeStruct((N_DEV,) + src.shape, src.dtype),
        in_specs=[pl.BlockSpec(memory_space=pl.ANY)],
        out_specs=pl.BlockSpec(memory_space=pl.ANY),
        scratch_shapes=[pltpu.SemaphoreType.DMA((N_DEV - 1,)), pltpu.SemaphoreType.DMA((N_DEV - 1,)),
                        pltpu.SemaphoreType.DMA(())],
    )(src)


def sum_devices(g, *, tr, name):
    _, R, C = g.shape

    def body(g_ref, o_ref):
        acc = g_ref[0].astype(F32)
        for j in range(1, N_DEV):
            acc = acc + g_ref[j].astype(F32)
        o_ref[...] = acc

    return pl.pallas_call(
        body, name=name, grid=(R // tr,),
        out_shape=jax.ShapeDtypeStruct((R, C), F32),
        in_specs=[pl.BlockSpec((N_DEV, tr, C), lambda i: (0, i, 0))],
        out_specs=pl.BlockSpec((tr, C), lambda i: (i, 0)),
        compiler_params=_params(("parallel",)),
    )(g)


def matmul(a, b, mode, out_dtype, *, name, tm=1024, tn=1024, tk=1024, n=None, comm=None,
           a_pro=None, epi=None, epi_ins=()):
    if mode == "nn":
        (M, K), (K2, N) = a.shape, b.shape
    elif mode == "nt":
        (M, K), (N, K2) = a.shape, b.shape
    else:
        (K, M), (K2, N) = a.shape, b.shape
    assert K == K2
    N = N if n is None else n
    tm, tn, tk = min(tm, M), min(tn, N), min(tk, K)
    assert M % tm == 0 and N % tn == 0 and K % tk == 0, (name, M, N, K, tm, tn, tk)
    nk = K // tk
    if mode == "tn":
        a_spec = pl.BlockSpec((tk, tm), lambda i, j, k: (k, i))
    else:
        a_spec = pl.BlockSpec((tm, tk), lambda i, j, k: (i, k))
    if mode == "nt":
        b_spec = pl.BlockSpec((tn, tk), lambda i, j, k: (j, k))
    else:
        b_spec = pl.BlockSpec((tk, tn), lambda i, j, k: (k, j))
    dims = {"nn": (1, 0), "nt": (1, 1), "tn": (0, 0)}[mode]
    items = list(comm) if comm else []
    nx = len(items)
    ne = len(epi_ins)
    gm, gn = M // tm, N // tn
    any_spec = pl.BlockSpec(memory_space=pl.ANY)
    o_spec = pl.BlockSpec((tm, tn), lambda i, j, k: (i, j))

    def body(*refs):
        a_ref, b_ref, e_refs = refs[0], refs[1], refs[2:2 + ne]
        refs = refs[2 + ne:]
        src_refs, o_ref, out_refs = refs[:nx], refs[nx], refs[1 + nx:1 + 2 * nx]
        acc_ref, sems = refs[1 + 2 * nx], refs[2 + 2 * nx:]
        i, j, k = pl.program_id(0), pl.program_id(1), pl.program_id(2)
        if items:
            exchange_start(items, src_refs, out_refs, sems, (i == 0) & (j == 0) & (k == 0))
        a_tile = a_ref[...] if a_pro is None else a_pro(a_ref[...])
        part = lax.dot_general(a_tile, b_ref[...], (((dims[0],), (dims[1],)), ((), ())),
                               preferred_element_type=F32)

        def finish(acc):
            if epi is not None:
                acc = epi(acc, *[e[...] for e in e_refs])
            o_ref[...] = acc.astype(o_ref.dtype)

        if nk == 1:
            finish(part)
        else:
            @pl.when(k == 0)
            def _():
                acc_ref[...] = part

            @pl.when((k > 0) & (k < nk - 1))
            def _():
                acc_ref[...] += part

            @pl.when(k == nk - 1)
            def _():
                finish(acc_ref[...] + part)

        if items:
            exchange_finish(items, src_refs, out_refs, sems, (i == gm - 1) & (j == gn - 1) & (k == nk - 1))

    res = pl.pallas_call(
        body, name=name, grid=(gm, gn, nk),
        out_shape=[jax.ShapeDtypeStruct((M, N), out_dtype)] + exchange_out_shapes(items),
        in_specs=[a_spec, b_spec] + [o_spec] * ne + [any_spec] * nx,
        out_specs=[o_spec] + [any_spec] * nx,
        scratch_shapes=[pltpu.VMEM((tm, tn) if nk > 1 else (8, 128), F32)]
        + (exchange_semaphores(items) if items else []),
        compiler_params=_params(("arbitrary",) * 3 if items else ("parallel", "parallel", "arbitrary")),
    )(a, b, *epi_ins, *[it[0] for it in items])
    return res if items else res[0]


def rowwise_call(body_fn, rows, fulls, row_outs, acc_outs, *, tm, name, mm=None, tk=1024):
    rows = [r if isinstance(r, tuple) else (r, r.shape[1], 0) for r in rows]
    T = rows[0][0].shape[0]
    tm = min(tm, T)
    assert T % tm == 0
    n_r, n_f, n_ro = len(rows), len(fulls), len(row_outs)
    into = [(k, ro) for k, ro in enumerate(row_outs) if len(ro) == 3]
    n_b = len(into)
    n_mm, nk = 0, 1
    if mm is not None:
        a, b, mode, pos, a_pro = mm
        n_mm = 2
        K = a.shape[1]
        N = b.shape[1] if mode == "nn" else b.shape[0]
        tk = min(tk, K)
        assert K % tk == 0 and a.shape[0] == T
        nk = K // tk
        b_contract = 0 if mode == "nn" else 1

    def row_body(refs, product):
        r_refs = refs[:n_r]
        f_refs = refs[n_r:n_r + n_f]
        refs = refs[n_r + n_f + n_b:]
        ro_refs = refs[:n_ro]
        ao_refs = refs[n_ro:n_ro + len(acc_outs)]
        r_vals = [r[...].astype(F32) for r in r_refs]
        if product is not None:
            r_vals.insert(pos, product)
        f_vals = [f[...].astype(F32) for f in f_refs]
        ro, ao = body_fn(r_vals, f_vals)
        for ref, v in zip(ro_refs, ro):
            ref[...] = v.astype(ref.dtype)
        if ao_refs:
            @pl.when(pl.program_id(0) == 0)
            def _():
                for ref in ao_refs:
                    ref[...] = jnp.zeros(ref.shape, F32)
            for ref, v in zip(ao_refs, ao):
                ref[...] += v.reshape(ref.shape)

    def body(*refs):
        if mm is None:
            return row_body(refs, None)
        a_ref, b_ref, rest, acc_ref = refs[0], refs[1], refs[2:-1], refs[-1]
        k = pl.program_id(1)
        a_tile = a_ref[...] if a_pro is None else a_pro(a_ref[...])
        part = lax.dot_general(a_tile, b_ref[...], (((1,), (b_contract,)), ((), ())),
                               preferred_element_type=F32)
        if nk == 1:
            return row_body(rest, part)

        @pl.when(k == 0)
        def _():
            acc_ref[...] = part

        @pl.when((k > 0) & (k < nk - 1))
        def _():
            acc_ref[...] += part

        @pl.when(k == nk - 1)
        def _():
            row_body(rest, acc_ref[...] + part)

    def full_spec(shape):
        nd = len(shape)
        return pl.BlockSpec(tuple(shape), lambda i, *_: (0,) * nd)

    def row_spec(w, off):
        return pl.BlockSpec((tm, w), functools.partial(lambda i, *_, o: (i, o), o=off // w))

    in_specs = []
    if mm is not None:
        in_specs.append(pl.BlockSpec((tm, tk), lambda i, k: (i, k)))
        in_specs.append(pl.BlockSpec((tk, N), lambda i, k: (k, 0)) if mode == "nn" else
                        pl.BlockSpec((N, tk), lambda i, k: (0, k)))
    in_specs += [row_spec(w, off) for (_, w, off) in rows]
    in_specs += [full_spec(f.shape) for f in fulls]
    in_specs += [pl.BlockSpec(memory_space=pl.ANY)] * n_b
    out_specs, out_shape = [], []
    for ro in row_outs:
        if len(ro) == 3:
            buf, w, off = ro
            out_specs.append(row_spec(w, off))
            out_shape.append(jax.ShapeDtypeStruct(buf.shape, buf.dtype))
        else:
            w, dt = ro
            out_specs.append(row_spec(w, 0))
            out_shape.append(jax.ShapeDtypeStruct((T, w), dt))
    out_specs += [full_spec(s) for s in acc_outs]
    out_shape += [jax.ShapeDtypeStruct(tuple(s), F32) for s in acc_outs]
    aliases = {n_mm + n_r + n_f + b: k for b, (k, _) in enumerate(into)}
    return pl.pallas_call(
        body, name=name, grid=(T // tm,) if mm is None else (T // tm, nk),
        out_shape=out_shape, in_specs=in_specs, out_specs=out_specs,
        scratch_shapes=[] if mm is None else [pltpu.VMEM((tm, N) if nk > 1 else (8, 128), F32)],
        input_output_aliases=aliases,
        compiler_params=_params(("arbitrary",) if mm is None else ("arbitrary", "arbitrary")),
    )(*([] if mm is None else [a, b]), *[r[0] for r in rows], *fulls, *[ro[0] for _, ro in into])


def fwd_body(fn):
    return lambda r, f: (fn(*r, *f), ())


def bwd_body(fn, n_rows):
    def body(r, f):
        ins, cots = r[:n_rows], r[n_rows:]
        _, vjp = jax.vjp(fn, *ins, *f)
        g = vjp(tuple(cots))
        return g[:n_rows], g[n_rows:]
    return body


def whole_call(fn, ins, outs, *, name):
    n_in = len(ins)

    def body(*refs):
        res = fn(*[r[...] for r in refs[:n_in]])
        for ref, v in zip(refs[n_in:], res):
            ref[...] = v.astype(ref.dtype)

    return pl.pallas_call(
        body, name=name,
        out_shape=[jax.ShapeDtypeStruct(tuple(s), dt) for (s, dt) in outs],
        compiler_params=_params(),
    )(*ins)


def fn_modulate(x, sc, sh):
    return (_rms(x) * (1.0 + sc) + sh,)


def fn_sgu(u, v, nw, ws, bs):
    ug = jax.nn.gelu(u)
    vn = _rms(jax.nn.gelu(v)) * nw
    ri = lax.broadcasted_iota(jnp.int32, (Q, Q), 0)
    ci = lax.broadcasted_iota(jnp.int32, (Q, Q), 1)
    causal = ri >= ci
    chunks = []
    for n in range(u.shape[0] // Q):
        vc = vn[n * Q:(n + 1) * Q]
        cols = [dot_nn(jnp.where(causal, ws[g], 0.0), vc[:, g * Q:(g + 1) * Q]) + bs[g]
                for g in range(GM_GROUPS)]
        chunks.append(jnp.concatenate(cols, axis=1))
    sv = chunks[0] if len(chunks) == 1 else jnp.concatenate(chunks, axis=0)
    return (ug * sv,)


def fn_mix(ga, gb, pa, pb):
    return (jax.nn.sigmoid(ga) * pa + jax.nn.sigmoid(gb) * pb,)


def fn_res_modulate(x, o, g1, sc2, sh2):
    x1 = x + g1 * o
    return x1, _rms(x1) * (1.0 + sc2) + sh2


def relu2_tile(f):
    return jnp.square(jnp.maximum(f.astype(F32), 0.0)).astype(BF16)


def relu2_grad_tile(dact, f):
    return dact * (2.0 * jnp.maximum(f.astype(F32), 0.0))


def final_body(r, f):
    x1, gf, tgt = r
    g2, fnw = f

    def loss_fn(x1, gf, g2, fnw):
        y = _rms(x1 + g2 * gf) * fnw
        row = 0.5 * jnp.mean(jnp.square(y - tgt), axis=-1, keepdims=True)
        return jnp.sum(row, axis=0, keepdims=True)

    l, vjp = jax.vjp(loss_fn, x1, gf, g2, fnw)
    dx1, dgf, dg2, dfnw = vjp(jnp.ones((1, 1), F32))
    return (dx1, dgf), (jnp.broadcast_to(l, (1, 128)), dg2, dfnw)


def grad_x_body(r, f):
    x, dh, dxa = r
    _, vjp = jax.vjp(fn_modulate, x, *f)
    dx, dsc, dsh = vjp((dh,))
    return (dx + dxa,), (dsc, dsh)


CONV_CW = 128
CONV_PAD = 8
CONV_ROWS = 128


def _conv_pre(xp, w_ref, b_ref, r0, R):
    acc = b_ref[...] + w_ref[0:1, :] * xp[r0 + CONV_PAD - 3:r0 + CONV_PAD - 3 + R, :]
    for k in range(1, CONV_K):
        s = r0 + CONV_PAD - 3 + k
        acc = acc + w_ref[k:k + 1, :] * xp[s:s + R, :]
    return acc


def conv_fwd(proj, conv_w, conv_b):
    T = proj.shape[0]
    R = min(CONV_ROWS, T)

    def body(x_ref, w_ref, b_ref, o_ref, xp):
        xp[0:CONV_PAD, :] = jnp.zeros((CONV_PAD, CONV_CW), F32)
        xp[CONV_PAD:CONV_PAD + T, :] = x_ref[...].astype(F32)
        for r0 in range(0, T, R):
            pre = _conv_pre(xp, w_ref, b_ref, r0, R)
            o_ref[r0:r0 + R, :] = (pre * jax.nn.sigmoid(pre)).astype(o_ref.dtype)

    return pl.pallas_call(
        body, name="conv_fwd", grid=(CONV_DIM // CONV_CW,),
        out_shape=jax.ShapeDtypeStruct((T, CONV_DIM), BF16),
        in_specs=[pl.BlockSpec((T, CONV_CW), lambda j: (0, P_XBC // CONV_CW + j)),
                  pl.BlockSpec((CONV_K, CONV_CW), lambda j: (0, j)),
                  pl.BlockSpec((1, CONV_CW), lambda j: (0, j))],
        out_specs=pl.BlockSpec((T, CONV_CW), lambda j: (0, j)),
        scratch_shapes=[pltpu.VMEM((T + CONV_PAD, CONV_CW), F32)],
        compiler_params=_params(("parallel",)),
    )(proj, conv_w, conv_b)


def conv_bwd(proj, dact, col0, conv_w, conv_b, dproj, *, name):
    T = proj.shape[0]
    R = min(CONV_ROWS, T)
    nb = dact.shape[1] // CONV_CW
    c0 = col0 // CONV_CW
    x0 = (P_XBC + col0) // CONV_CW

    def body(x_ref, d_ref, w_ref, b_ref, _, dx_ref, dw_ref, db_ref, xp, dp):
        xp[0:CONV_PAD, :] = jnp.zeros((CONV_PAD, CONV_CW), F32)
        xp[CONV_PAD:CONV_PAD + T, :] = x_ref[...].astype(F32)
        dp[T:T + CONV_PAD, :] = jnp.zeros((CONV_PAD, CONV_CW), F32)
        dws = [jnp.zeros((1, CONV_CW), F32) for _ in range(CONV_K)]
        db = jnp.zeros((1, CONV_CW), F32)
        for r0 in range(0, T, R):
            pre = _conv_pre(xp, w_ref, b_ref, r0, R)
            s = jax.nn.sigmoid(pre)
            dpre = d_ref[r0:r0 + R, :].astype(F32) * (s * (1.0 + pre * (1.0 - s)))
            dp[r0:r0 + R, :] = dpre
            db = db + jnp.sum(dpre, axis=0, keepdims=True)
            for k in range(CONV_K):
                st = r0 + CONV_PAD - 3 + k
                dws[k] = dws[k] + jnp.sum(dpre * xp[st:st + R, :], axis=0, keepdims=True)
        for r0 in range(0, T, R):
            acc = w_ref[0:1, :] * dp[r0 + 3:r0 + 3 + R, :]
            for k in range(1, CONV_K):
                acc = acc + w_ref[k:k + 1, :] * dp[r0 + 3 - k:r0 + 3 - k + R, :]
            dx_ref[r0:r0 + R, :] = acc.astype(dx_ref.dtype)
        for k in range(CONV_K):
            dw_ref[k:k + 1, :] = dws[k]
        db_ref[...] = db

    return pl.pallas_call(
        body, name=name, grid=(nb,),
        out_shape=[jax.ShapeDtypeStruct(dproj.shape, dproj.dtype),
                   jax.ShapeDtypeStruct((CONV_K, nb * CONV_CW), F32),
                   jax.ShapeDtypeStruct((1, nb * CONV_CW), F32)],
        in_specs=[pl.BlockSpec((T, CONV_CW), lambda j: (0, x0 + j)),
                  pl.BlockSpec((T, CONV_CW), lambda j: (0, j)),
                  pl.BlockSpec((CONV_K, CONV_CW), lambda j: (0, c0 + j)),
                  pl.BlockSpec((1, CONV_CW), lambda j: (0, c0 + j)),
                  pl.BlockSpec(memory_space=pl.ANY)],
        out_specs=[pl.BlockSpec((T, CONV_CW), lambda j: (0, x0 + j)),
                   pl.BlockSpec((CONV_K, CONV_CW), lambda j: (0, j)),
                   pl.BlockSpec((1, CONV_CW), lambda j: (0, j))],
        scratch_shapes=[pltpu.VMEM((T + CONV_PAD, CONV_CW), F32),
                        pltpu.VMEM((T + CONV_PAD, CONV_CW), F32)],
        input_output_aliases={4: 0},
        compiler_params=_params(("parallel",)),
    )(proj, dact, conv_w, conv_b, dproj)


def _split3(a):
    hi = a.astype(BF16)
    r = a - hi.astype(F32)
    mid = r.astype(BF16)
    return hi, mid, (r - mid.astype(F32)).astype(BF16)


def _dg3(a, m, ca, cm, a_first):
    dims = (((ca,), (cm,)), ((), ())) if a_first else (((cm,), (ca,)), ((), ()))
    out = None
    for p in _split3(a):
        t = lax.dot_general(p, m, dims, preferred_element_type=F32) if a_first else \
            lax.dot_general(m, p, dims, preferred_element_type=F32)
        out = t if out is None else out + t
    return out


@jax.custom_vjp
def exact_right(a, m):
    return _dg3(a, m, 1, 0, True)


@jax.custom_vjp
def exact_left(m, a):
    return _dg3(a, m, 0, 1, False)


exact_right.defvjp(lambda a, m: (exact_right(a, m), m),
                   lambda m, g: (dot_nt(g, m), jnp.zeros_like(m)))
exact_left.defvjp(lambda m, a: (exact_left(m, a), m),
                  lambda m, g: (jnp.zeros_like(m), _dg3(g, m, 0, 0, False)))


def ssd_step(lane0, state, x, z, dtr, Bm, Cm, dtb, alog, dsk, nw):
    def iota(shape, dim):
        return lax.broadcasted_iota(jnp.int32, shape, dim)

    def one_hot(mask):
        return mask.astype(F32).astype(BF16)

    causal = iota((Q, Q), 0) >= iota((Q, Q), 1)
    eye = iota((Q, Q), 0) == iota((Q, Q), 1)
    lane = iota((1, 128), 1)
    colh = lax.shift_right_logical(iota((1, SSM_GW), 1), 6)
    to_cols = one_hot(iota((128, SSM_GW), 0) == lane0 + colh)

    dt_all = _softplus(dtr + dtb)
    a_all = dt_all * (-jnp.exp(alog))
    cum_all = exact_left(one_hot(causal), a_all)
    both = exact_right(jnp.concatenate([dt_all, cum_all], axis=0), to_cols)
    dt_f, cum_f = both[:Q], both[Q:]
    last_f = jnp.sum(jnp.where(iota((Q, 1), 0) == Q - 1, cum_f, 0.0), axis=0, keepdims=True)
    dsk_f = jnp.zeros((1, SSM_GW), F32)
    for h in range(SSM_HPG):
        dsk_f = jnp.where(colh == h, dsk[h], dsk_f)

    xdt = x * dt_f
    cb = dot_nt(Cm, Bm)
    ms, rhs = [], []
    for h in range(SSM_HPG):
        ch = jnp.sum(jnp.where(lane == lane0 + h, cum_all, 0.0), axis=1, keepdims=True)
        ch_t = jnp.sum(jnp.where(eye, ch, 0.0), axis=0, keepdims=True)
        ms.append(cb * jnp.exp(jnp.where(causal, ch - ch_t, -1e30)))
        rhs.append(jnp.where(colh == h, xdt, 0.0))
    y = dot_nn(jnp.concatenate(ms, axis=1), jnp.concatenate(rhs, axis=0))
    y = y + dot_nn(Cm, state) * jnp.exp(cum_f) + x * dsk_f
    new_state = state * jnp.exp(last_f) + dot_tn(Bm, xdt * jnp.exp(last_f - cum_f))
    gated = y * (z * jax.nn.sigmoid(z))
    return new_state, _rms(gated) * nw


SSD_GPS = 4
_XW = SSD_GPS * SSM_GW
_BW = SSD_GPS * 128


def _ssd_in_specs(rev, nc):
    def n_of(n):
        return nc - 1 - n if rev else n
    return [
        pl.BlockSpec((Q, _XW), lambda g, n: (n_of(n), g)),
        pl.BlockSpec((Q, _BW), lambda g, n: (n_of(n), SSM_INNER // _BW + g)),
        pl.BlockSpec((Q, _BW), lambda g, n: (n_of(n), (SSM_INNER + SSM_GROUPS * 128) // _BW + g)),
        pl.BlockSpec((Q, _XW), lambda g, n: (n_of(n), P_Z // _XW + g)),
        pl.BlockSpec((Q, 128), lambda g, n: (n_of(n), 0)),
        pl.BlockSpec((1, 128), lambda g, n: (0, 0)),
        pl.BlockSpec((1, 128), lambda g, n: (0, 0)),
        pl.BlockSpec((SSD_GPS, SSM_HPG, 1, 1), lambda g, n: (g, 0, 0, 0)),
        pl.BlockSpec((1, _XW), lambda g, n: (0, g)),
    ]


def _ssd_group_inputs(gi, x_ref, b_ref, c_ref, z_ref, dt_ref, dtb_ref, al_ref, dk_ref, nw_ref):
    xs = slice(gi * SSM_GW, (gi + 1) * SSM_GW)
    bs = slice(gi * 128, (gi + 1) * 128)
    return (x_ref[:, xs].astype(F32), z_ref[:, xs].astype(F32), dt_ref[...],
            b_ref[:, bs], c_ref[:, bs],
            dtb_ref[...], al_ref[...], dk_ref[gi], nw_ref[:, xs])


def ssd_fwd(xact, proj, dtg, dtb, alog, dsk, nw, comm):
    T = xact.shape[0]
    nc = T // Q
    nx = len(comm)
    ng = SSM_GROUPS // SSD_GPS
    any_spec = pl.BlockSpec(memory_space=pl.ANY)

    def body(*refs):
        in_refs, src_refs = refs[:9], refs[9:9 + nx]
        yb_ref, st_ref = refs[9 + nx:11 + nx]
        out_refs, state, sems = refs[11 + nx:11 + 2 * nx], refs[11 + 2 * nx], refs[12 + 2 * nx:]
        g, n = pl.program_id(0), pl.program_id(1)
        exchange_start(comm, src_refs, out_refs, sems, (g == 0) & (n == 0))

        @pl.when(n == 0)
        def _():
            state[...] = jnp.zeros(state.shape, F32)

        for gi in range(SSD_GPS):
            lane0 = SSM_HPG * (SSD_GPS * pl.program_id(0) + gi)
            s = state[gi]
            st_ref[gi, 0] = s
            new_s, yb = ssd_step(lane0, s, *_ssd_group_inputs(gi, *in_refs))
            state[gi] = new_s
            yb_ref[:, gi * SSM_GW:(gi + 1) * SSM_GW] = yb.astype(yb_ref.dtype)

        exchange_finish(comm, src_refs, out_refs, sems, (g == ng - 1) & (n == nc - 1))

    return pl.pallas_call(
        body, name="ssd_fwd", grid=(ng, nc),
        out_shape=[jax.ShapeDtypeStruct((T, SSM_INNER), BF16),
                   jax.ShapeDtypeStruct((SSM_GROUPS, nc, 128, SSM_GW), F32)] + exchange_out_shapes(comm),
        in_specs=_ssd_in_specs(False, nc) + [any_spec] * nx,
        out_specs=[pl.BlockSpec((Q, _XW), lambda g, n: (n, g)),
                   pl.BlockSpec((SSD_GPS, 1, 128, SSM_GW), lambda g, n: (g, n, 0, 0))] + [any_spec] * nx,
        scratch_shapes=[pltpu.VMEM((SSD_GPS, 128, SSM_GW), F32)] + exchange_semaphores(comm),
        compiler_params=_params(("arbitrary", "arbitrary")),
    )(xact, xact, xact, proj, dtg, dtb, alog, dsk, nw, *[it[0] for it in comm])


def ssd_bwd(xact, proj, dtg, dtb, alog, dsk, nw, states, dyb, dproj, comm):
    T = xact.shape[0]
    nc = T // Q

    nx = len(comm)
    ng = SSM_GROUPS // SSD_GPS
    any_spec = pl.BlockSpec(memory_space=pl.ANY)

    def body(*refs):
        in_refs, (st_ref, dy_ref, _) = refs[:9], refs[9:12]
        src_refs, refs = refs[12:12 + nx], refs[12 + nx:]
        dx_ref, db_ref, dc_ref, dz_ref, ddt_ref, ddtb_ref, dal_ref, ddk_ref, dnw_ref = refs[:9]
        out_refs, dstate, sems = refs[9:9 + nx], refs[9 + nx], refs[10 + nx:]
        exchange_start(comm, src_refs, out_refs, sems, (pl.program_id(0) == 0) & (pl.program_id(1) == 0))

        @pl.when(pl.program_id(1) == 0)
        def _():
            dstate[...] = jnp.zeros(dstate.shape, F32)
            ddtb_ref[...] = jnp.zeros(ddtb_ref.shape, F32)
            dal_ref[...] = jnp.zeros(dal_ref.shape, F32)
            ddk_ref[...] = jnp.zeros(ddk_ref.shape, F32)
            dnw_ref[...] = jnp.zeros(dnw_ref.shape, F32)

        for gi in range(SSD_GPS):
            xs = slice(gi * SSM_GW, (gi + 1) * SSM_GW)
            bs = slice(gi * 128, (gi + 1) * 128)
            lane0 = SSM_HPG * (SSD_GPS * pl.program_id(0) + gi)
            ins = (st_ref[gi, 0],) + _ssd_group_inputs(gi, *in_refs)
            _, vjp = jax.vjp(functools.partial(ssd_step, lane0), *ins)
            ds, dx, dz, ddt, dbm, dcm, ddtb, dal, ddk, dnw = vjp((dstate[gi], dy_ref[:, xs].astype(F32)))
            dstate[gi] = ds
            dx_ref[:, xs] = dx.astype(dx_ref.dtype)
            db_ref[:, bs] = dbm.astype(db_ref.dtype)
            dc_ref[:, bs] = dcm.astype(dc_ref.dtype)
            dz_ref[:, xs] = dz.astype(dz_ref.dtype)
            ddt_ref[:, bs] = ddt
            ddtb_ref[gi] += ddtb
            dal_ref[gi] += dal
            ddk_ref[gi] += ddk
            dnw_ref[:, xs] += dnw

        exchange_finish(comm, src_refs, out_refs, sems,
                        (pl.program_id(0) == ng - 1) & (pl.program_id(1) == nc - 1))

    rev = lambda n: nc - 1 - n
    row_shape = jax.ShapeDtypeStruct((SSM_GROUPS, 1, 128), F32)
    row_spec = pl.BlockSpec((SSD_GPS, 1, 128), lambda g, n: (g, 0, 0))
    return pl.pallas_call(
        body, name="ssd_bwd", grid=(ng, nc),
        out_shape=[jax.ShapeDtypeStruct((T, SSM_INNER), BF16),
                   jax.ShapeDtypeStruct((T, SSM_GROUPS * 128), BF16),
                   jax.ShapeDtypeStruct((T, SSM_GROUPS * 128), BF16),
                   jax.ShapeDtypeStruct(dproj.shape, dproj.dtype),
                   jax.ShapeDtypeStruct((T, SSM_GROUPS * 128), F32),
                   row_shape, row_shape,
                   jax.ShapeDtypeStruct((SSM_GROUPS, SSM_HPG, 1, 1), F32),
                   jax.ShapeDtypeStruct((1, SSM_INNER), F32)] + exchange_out_shapes(comm),
        in_specs=_ssd_in_specs(True, nc) + [
            pl.BlockSpec((SSD_GPS, 1, 128, SSM_GW), lambda g, n: (g, rev(n), 0, 0)),
            pl.BlockSpec((Q, _XW), lambda g, n: (rev(n), g)),
            any_spec] + [any_spec] * nx,
        out_specs=[pl.BlockSpec((Q, _XW), lambda g, n: (rev(n), g)),
                   pl.BlockSpec((Q, _BW), lambda g, n: (rev(n), g)),
                   pl.BlockSpec((Q, _BW), lambda g, n: (rev(n), g)),
                   pl.BlockSpec((Q, _XW), lambda g, n: (rev(n), P_Z // _XW + g)),
                   pl.BlockSpec((Q, _BW), lambda g, n: (rev(n), g)),
                   row_spec, row_spec,
                   pl.BlockSpec((SSD_GPS, SSM_HPG, 1, 1), lambda g, n: (g, 0, 0, 0)),
                   pl.BlockSpec((1, _XW), lambda g, n: (0, g))] + [any_spec] * nx,
        scratch_shapes=[pltpu.VMEM((SSD_GPS, 128, SSM_GW), F32)] + exchange_semaphores(comm),
        input_output_aliases={11: 3},
        compiler_params=_params(("arbitrary", "arbitrary")),
    )(xact, xact, xact, proj, dtg, dtb, alog, dsk, nw, states, dyb, dproj, *[it[0] for it in comm])


ADAMW_WHOLE_ELEMS = 256 * 1024


def adamw(w, g, m, v, *, name):
    shape = w.shape
    parts = g.shape != shape
    nd = len(shape)
    if nd == 3 and shape[1] == 1 and w.size > ADAMW_WHOLE_ELEMS:
        assert not parts and shape[0] % 4 == 0
        grid = (4,)
        spec = g_spec = pl.BlockSpec((shape[0] // 4, 1, shape[2]), lambda i: (i, 0, 0))
    else:
        if w.size <= ADAMW_WHOLE_ELEMS:
            grid, tr = (1,), shape[-2]
        else:
            assert all(s == 1 for s in shape[:-2]) and shape[-2] % 256 == 0
            grid, tr = (shape[-2] // 256,), 256
        blk = tuple(shape[:-2]) + (tr, shape[-1])
        spec = pl.BlockSpec(blk, lambda i: (0,) * (nd - 2) + (i, 0))
        g_spec = pl.BlockSpec((N_DEV,) + blk[1:], lambda i: (0,) * (nd - 2) + (i, 0)) if parts else spec

    def body(w_ref, g_ref, m_ref, v_ref, go_ref, d_ref, nm_ref, nv_ref):
        if parts:
            g = g_ref[0:1].astype(F32)
            for j in range(1, N_DEV):
                g = g + g_ref[j:j + 1].astype(F32)
        else:
            g = g_ref[...]
        nm = ADAM_B1 * m_ref[...] + (1.0 - ADAM_B1) * g
        nv = ADAM_B2 * v_ref[...] + (1.0 - ADAM_B2) * jnp.square(g)
        m_hat = nm / (1.0 - ADAM_B1 ** ADAM_STEP)
        v_hat = nv / (1.0 - ADAM_B2 ** ADAM_STEP)
        go_ref[...] = g
        d_ref[...] = -ADAM_LR * (m_hat / (jnp.sqrt(v_hat) + ADAM_EPS) + ADAM_WD * w_ref[...])
        nm_ref[...] = nm
        nv_ref[...] = nv

    shp = jax.ShapeDtypeStruct(shape, F32)
    return pl.pallas_call(
        body, name=name, grid=grid,
        out_shape=[shp] * 4, in_specs=[spec, g_spec, spec, spec], out_specs=[spec] * 4,
        compiler_params=_params(("parallel",)),
    )(w, g, m, v)


def _pad_rows(a, rows):
    return jnp.pad(a, ((0, rows - a.shape[0]), (0, 0)))


WIN_W = 1408
N_IN = IN_WIDTH // N_DEV
_A6 = OFF_DT - 6 * N_IN
_C6 = 7 * N_IN - OFF_GA


def _win_offset(me):
    return jnp.where(me == 7, 124, 4 * me)


def _w_in_window(shard, me):
    rows = shard.shape[0]
    z = lambda n: jnp.zeros((rows, n), shard.dtype)
    a = lax.dynamic_update_slice(z(WIN_W), shard, (0, _win_offset(me)))
    b = jnp.concatenate([z(24), shard[:, :_A6], shard[:, _A6 + 32:], z(4), shard[:, _A6:_A6 + 32], z(96)], axis=1)
    return jnp.where(me == 6, b, a)


def _w_in_from_window(window, me):
    a = lax.dynamic_slice(window, (0, _win_offset(me)), (window.shape[0], N_IN))
    b = jnp.concatenate([window[:, 24:24 + _A6], window[:, 1280:1312], window[:, 24 + _A6:24 + _A6 + _C6]], axis=1)
    return jnp.where(me == 6, b, a)


def _w_all_from_windows(g):
    def merge_first(p, t):
        return jnp.concatenate([p[:, :128] + t, p[:, 128:]], axis=1)

    parts = [g[0][:, :1280]]
    for j in range(1, 6):
        parts.append(merge_first(g[j][:, :1280], g[j - 1][:, 1280:]))
    p6 = merge_first(g[6][:, :1280], g[5][:, 1280:])
    parts.append(jnp.concatenate([p6[:, :1152], p6[:, 1152:] + g[7][:, :128]], axis=1))
    parts.append(g[7][:, 128:])
    parts.append(g[6][:, 1280:])
    return jnp.concatenate(parts, axis=1)


def _windows_of_w_all(gw):
    wins = [gw[:, 1280 * j:1280 * j + WIN_W] for j in range(6)]
    wins.append(jnp.concatenate([gw[:, 7680:8960], gw[:, PROJ_W:]], axis=1))
    wins.append(gw[:, 8832:PROJ_W])
    return jnp.stack(wins)


def kernel(x, c, w_mod, b_mod, w_in, gm_norm_w, gm_ws, gm_bs, conv_w, conv_b, dt_bias, a_log, d_skip, ssm_norm_w, w_branch_gm, w_branch_ssm, w_out, w_ff1, w_ff2, final_norm_w, loss_target, m_w_mod, m_b_mod, m_w_in, m_gm_norm_w, m_gm_ws, m_gm_bs, m_conv_w, m_conv_b, m_dt_bias, m_a_log, m_d_skip, m_ssm_norm_w, m_w_branch_gm, m_w_branch_ssm, m_w_out, m_w_ff1, m_w_ff2, m_final_norm_w, v_w_mod, v_b_mod, v_w_in, v_gm_norm_w, v_gm_ws, v_gm_bs, v_conv_w, v_conv_b, v_dt_bias, v_a_log, v_d_skip, v_ssm_norm_w, v_w_branch_gm, v_w_branch_ssm, v_w_out, v_w_ff1, v_w_ff2, v_final_norm_w):
    T = x.shape[1]
    me = 4 * lax.axis_index("x") + 2 * lax.axis_index("y") + lax.axis_index("c")
    x2 = x[0]
    tgt = loss_target[0]
    n_in = IN_WIDTH // N_DEV
    n_mod = N_MOD * D // N_DEV
    n_cv = CONV_DIM // N_DEV

    c_all, conv_w_full = exchange(
        [(c.reshape(8, 128), _whole, (N_DEV, 8, 128), _slot),
         (conv_w[0], _whole, (N_DEV, CONV_K, n_cv), _slot)], name="gather_c_convw")
    c_all = c_all.reshape(N_DEV, D)
    conv_w_full = conv_w_full.transpose(1, 0, 2).reshape(CONV_K, CONV_DIM)

    win = _w_in_window(w_in[0].astype(BF16), me)
    gwin = gather_blocks_two_level(win, name="gather_w_in")
    late_weights = [
        (w_branch_gm[0].astype(BF16), _whole, (D, D), _rows(D // N_DEV)),
        (w_branch_ssm[0].astype(BF16), _whole, (SSM_INNER, D), _rows(SSM_INNER // N_DEV)),
        (w_out[0].astype(BF16), _whole, (D, D), _rows(D // N_DEV)),
        (w_ff1[0].astype(BF16), _whole, (D, D_FF), _cols(D_FF // N_DEV)),
        (w_ff2[0].astype(BF16), _whole, (D_FF, D), _rows(D_FF // N_DEV))]
    w_all = _w_all_from_windows(gwin)
    w_dt = w_all[:, PROJ_W:]

    c_pad = _pad_rows(c_all, 128)
    b_mine = lax.dynamic_slice(b_mod, (0, me * n_mod), (1, n_mod))

    def mod_fn(cp, w, b):
        ca = cp * jax.nn.sigmoid(cp)
        return (jnp.dot(ca, w, precision=HIGHEST, preferred_element_type=F32) + b,)

    (mod_part,) = whole_call(mod_fn, [c_pad, w_mod[0], b_mine], [((128, n_mod), F32)], name="mod_fwd")
    gmod = gather_blocks(mod_part[:N_DEV], name="gather_mod")
    mod = lax.dynamic_index_in_dim(gmod, me, axis=1, keepdims=False).reshape(N_MOD, D)
    sh1, sc1, gt1, sh2, sc2, gt2 = [mod[i:i + 1] for i in range(N_MOD)]

    (h,) = rowwise_call(fwd_body(fn_modulate), [x2], [sc1, sh1], [(D, BF16)], [], tm=256, name="modulate1")
    proj = matmul(h, w_all, "nn", BF16, name="mm_proj", n=PROJ_W)
    dtg = matmul(h, w_dt, "nn", F32, name="mm_dt")
    ws = gm_ws[0]
    bs3 = gm_bs[0].reshape(GM_GROUPS, Q, 1)
    sgu_rows = [(proj, D, P_U), (proj, D, P_V)]
    (ya,) = rowwise_call(fwd_body(fn_sgu), sgu_rows, [gm_norm_w, ws, bs3], [(D, BF16)], [],
                         tm=256, name="sgu_fwd")
    xact = conv_fwd(proj, conv_w_full, conv_b)
    dtb4 = jnp.pad(dt_bias, ((0, 0), (0, 96)))
    alog4 = jnp.pad(a_log, ((0, 0), (0, 96)))
    dsk4 = d_skip.reshape(SSM_GROUPS, SSM_HPG, 1, 1)
    yb, states, w_gm_f, w_ssm_f, w_out_f, w_ff1_f, w_ff2_f = ssd_fwd(
        xact, proj, dtg, dtb4, alog4, dsk4, ssm_norm_w, late_weights)
    pa = matmul(ya, w_gm_f, "nn", F32, name="mm_branch_gm")
    gate_rows = [(proj, D, P_GA), (proj, D, P_GB)]
    mixed, pb = rowwise_call(
        lambda r, fl: (fn_mix(*r) + (r[3],), ()), gate_rows + [pa], [], [(D, BF16), (D, F32)], [],
        tm=FUSED_TM, name="branch_ssm_mix", mm=(yb, w_ssm_f, "nn", 3, None), tk=SSM_INNER)
    x1, h2, o = rowwise_call(
        lambda r, fl: (fn_res_modulate(*r, *fl) + (r[1],), ()), [x2], [gt1, sc2, sh2],
        [(D, F32), (D, BF16), (D, F32)], [], tm=FUSED_TM, name="out_res_modulate2",
        mm=(mixed, w_out_f, "nn", 1, None))
    f = matmul(h2, w_ff1_f, "nn", BF16, name="mm_ff1")

    dx1, dgf, loss_v, dgt2, dfnw = rowwise_call(
        final_body, [x1, tgt], [gt2, final_norm_w.reshape(1, D)], [(D, F32), (D, BF16)],
        [(1, 128), (1, D), (1, D)], tm=FUSED_TM, name="ff2_loss_bwd", mm=(f, w_ff2_f, "nn", 1, relu2_tile),
        tk=D_FF)
    df = matmul(dgf, w_ff2_f, "nt", BF16, name="mm_ff2_dgrad", epi=relu2_grad_tile, epi_ins=(f,))
    gw_ff2 = matmul(f, dgf, "tn", BF16, name="mm_ff2_wgrad", tk=WGRAD_TK, a_pro=relu2_tile)
    gw_ff1 = matmul(h2, df, "tn", BF16, name="mm_ff1_wgrad", tk=WGRAD_TK)

    def res_mod_bwd(r, fl):
        xv, ov, dx1v, dh2v = r
        _, vjp = jax.vjp(fn_res_modulate, xv, ov, *fl)
        dxv, dov, dg1, dsc, dsh = vjp((dx1v, dh2v))
        return (dxv, dov), (dg1, dsc, dsh)

    dxa, do, dgt1, dsc2, dsh2 = rowwise_call(
        res_mod_bwd, [x2, o, dx1], [gt1, sc2, sh2], [(D, F32), (D, BF16)],
        [(1, D), (1, D), (1, D)], tm=FUSED_TM, name="ff1_dgrad_res_modulate2_bwd",
        mm=(df, w_ff1_f, "nt", 3, None), tk=D_FF)
    gw_out = matmul(mixed, do, "tn", BF16, name="mm_out_wgrad", tk=WGRAD_TK)
    dproj = lax.empty((T, ALL_W), BF16)

    def mix_bwd(r, fl):
        dga, dgb, dpa, dpb = bwd_body(fn_mix, 4)(r, fl)[0]
        return (jnp.concatenate([dga, dgb], axis=1), dpa, dpb), ()

    dproj, dpa, dpb = rowwise_call(
        mix_bwd, gate_rows + [pa, pb], [], [(dproj, 2 * D, P_GA), (D, BF16), (D, BF16)], [],
        tm=FUSED_TM, name="out_dgrad_mix_bwd", mm=(do, w_out_f, "nt", 4, None))
    gw_gm = matmul(ya, dpa, "tn", BF16, name="mm_branch_gm_wgrad", tk=WGRAD_TK)
    dyb = matmul(dpb, w_ssm_f, "nt", BF16, name="mm_branch_ssm_dgrad")
    gw_ssm = matmul(yb, dpb, "tn", BF16, name="mm_branch_ssm_wgrad", tk=WGRAD_TK)

    def sgu_bwd(r, fl):
        (du, dv), acc = bwd_body(fn_sgu, 2)(r, fl)
        return (jnp.concatenate([du, dv], axis=1),), acc

    dproj, dgnw, dws, dbs = rowwise_call(
        sgu_bwd, sgu_rows, [gm_norm_w, ws, bs3], [(dproj, 2 * D, P_U)],
        [(1, D), (GM_GROUPS, Q, Q), (GM_GROUPS, Q, 1)], tm=FUSED_TM, name="branch_gm_dgrad_sgu_bwd",
        mm=(dpa, w_gm_f, "nt", 2, None))
    early_grads = [
        (gw_gm, _rows(D // N_DEV), (N_DEV, D // N_DEV, D), _slot),
        (gw_ssm, _rows(SSM_INNER // N_DEV), (N_DEV, SSM_INNER // N_DEV, D), _slot),
        (gw_out, _rows(D // N_DEV), (N_DEV, D // N_DEV, D), _slot),
        (gw_ff1, _cols(D_FF // N_DEV), (N_DEV, D, D_FF // N_DEV), _slot),
        (gw_ff2, _rows(D_FF // N_DEV), (N_DEV, D_FF // N_DEV, D), _slot),
        (_pack_rows([dgnw, dws, dbs], EARLY_ROWS), _whole, (N_DEV, sum(EARLY_ROWS), 128), _slot)]
    (dxs, dbm, dcm, dproj, ddt8, ddtb, dalog, ddsk, dsnw,
     r_gm, r_ssm, r_out, r_ff1, r_ff2, early_all) = ssd_bwd(
        xact, proj, dtg, dtb4, alog4, dsk4, ssm_norm_w, states, dyb, dproj, early_grads)
    dconv_w, dconv_b = [], []
    for nm, dact_part, col0 in (("xs", dxs, 0), ("b", dbm, SSM_INNER), ("c", dcm, SSM_INNER + SSM_GROUPS * 128)):
        dproj, dcw, dcb = conv_bwd(proj, dact_part, col0, conv_w_full, conv_b, dproj, name="conv_bwd_" + nm)
        dconv_w.append(dcw)
        dconv_b.append(dcb)
    dconv_w = jnp.concatenate(dconv_w, axis=1)
    dconv_b = jnp.concatenate(dconv_b, axis=1)
    (dproj,) = rowwise_call(
        lambda r, fl: ((functools.reduce(jnp.add, r),), ()),
        [(ddt8, 128, 128 * g) for g in range(SSM_GROUPS)], [], [(dproj, 128, PROJ_W)], [],
        tm=1024, name="ddt_into_dproj")
    gw_all = matmul(h, dproj, "tn", BF16, name="mm_in_wgrad", tn=1152, tk=WGRAD_TK)
    mid_pack = _pack_rows([dconv_w, dconv_b, jnp.sum(ddtb, axis=0), jnp.sum(dalog, axis=0), ddsk, dsnw, dfnw,
                           jnp.concatenate([dgt1, dsh2, dsc2, dgt2], axis=0)], MID_ROWS)
    dh, r_in, mid_all = matmul(
        dproj, w_all.T, "nn", F32, name="mm_in_dgrad", tk=3456,
        comm=[(_windows_of_w_all(gw_all), _slot, (N_DEV, D, WIN_W), _slot),
              (mid_pack, _whole, (N_DEV, sum(MID_ROWS), 128), _slot)])
    grad_x, dsc1, dsh1 = rowwise_call(grad_x_body, [x2, dh, dxa], [sc1, sh1], [(D, F32)],
                                      [(1, D), (1, D)], tm=256, name="modulate1_bwd")

    g_w_in = _w_in_from_window(sum_devices(r_in, tr=256, name="sum_w_in_grads"), me).reshape(1, D, n_in)

    late_all = gather_blocks(_pack_rows([dsh1, dsc1], LATE_ROWS), name="gather_dmod1")
    s_early = _unpack_rows(sum_devices(early_all, tr=early_all.shape[1], name="sum_small_early"), EARLY_ROWS)
    s_mid = _unpack_rows(sum_devices(mid_all, tr=mid_all.shape[1], name="sum_small_mid"), MID_ROWS)
    s_late = _unpack_rows(sum_devices(late_all, tr=late_all.shape[1], name="sum_small_late"), LATE_ROWS)
    g_gm_norm_w = s_early[0][:D].reshape(1, D)
    g_gm_ws = s_early[1].reshape(GM_GROUPS * Q, Q)
    g_gm_bs = s_early[2][:GM_GROUPS * Q].reshape(GM_GROUPS, Q)
    g_conv_w_full = s_mid[0].reshape(CONV_K, CONV_DIM)
    g_conv_w = lax.dynamic_slice(g_conv_w_full, (0, me * n_cv), (CONV_K, n_cv))
    g_conv_b = s_mid[1].reshape(1, CONV_DIM)
    g_dt_bias = s_mid[2][:32].reshape(1, 32)
    g_a_log = s_mid[3][:32].reshape(1, 32)
    g_d_skip = s_mid[4][:32].reshape(1, 32)
    g_ssm_norm_w = s_mid[5].reshape(1, SSM_INNER)
    g_final_norm_w = s_mid[6][:D].reshape(1, D)
    g_b_mod = jnp.concatenate([s_late[0][:D], s_late[1][:D], s_mid[7]]).reshape(1, N_MOD * D)

    dmod_all = jnp.concatenate(
        [late_all.reshape(N_DEV, -1)[:, :2 * D],
         mid_all[:, sum(MID_ROWS[:7]):].reshape(N_DEV, 4 * D)], axis=1)
    dmod_mine = _pad_rows(lax.dynamic_slice(dmod_all, (0, me * n_mod), (N_DEV, n_mod)), 128)

    def wmod_grad_fn(cp, dm):
        ca = cp * jax.nn.sigmoid(cp)
        return (lax.dot_general(ca, dm, (((0,), (0,)), ((), ())), precision=HIGHEST,
                                preferred_element_type=F32),)

    (g_w_mod,) = whole_call(wmod_grad_fn, [c_pad, dmod_mine], [((D, n_mod), F32)], name="w_mod_grad")

    upd = {}

    def step(name, w, g, m, v, parts=False):
        upd[name] = adamw(w, g if parts else g.reshape(w.shape), m, v, name="adamw_" + name)

    step("w_mod", w_mod, g_w_mod, m_w_mod, v_w_mod)
    step("b_mod", b_mod, g_b_mod, m_b_mod, v_b_mod)
    col_major = lambda a: jnp.transpose(a, (2, 0, 1))
    upd["w_in"] = tuple(jnp.transpose(o, (1, 2, 0)) for o in adamw(
        col_major(w_in), col_major(g_w_in), col_major(m_w_in), col_major(v_w_in), name="adamw_w_in"))
    step("gm_norm_w", gm_norm_w, g_gm_norm_w, m_gm_norm_w, v_gm_norm_w)
    step("gm_ws", gm_ws, g_gm_ws, m_gm_ws, v_gm_ws)
    step("gm_bs", gm_bs, g_gm_bs, m_gm_bs, v_gm_bs)
    step("conv_w", conv_w, g_conv_w, m_conv_w, v_conv_w)
    step("conv_b", conv_b, g_conv_b, m_conv_b, v_conv_b)
    step("dt_bias", dt_bias, g_dt_bias, m_dt_bias, v_dt_bias)
    step("a_log", a_log, g_a_log, m_a_log, v_a_log)
    step("d_skip", d_skip, g_d_skip, m_d_skip, v_d_skip)
    step("ssm_norm_w", ssm_norm_w, g_ssm_norm_w, m_ssm_norm_w, v_ssm_norm_w)
    step("w_branch_gm", w_branch_gm, r_gm, m_w_branch_gm, v_w_branch_gm, parts=True)
    step("w_branch_ssm", w_branch_ssm, r_ssm, m_w_branch_ssm, v_w_branch_ssm, parts=True)
    step("w_out", w_out, r_out, m_w_out, v_w_out, parts=True)
    step("w_ff1", w_ff1, r_ff1, m_w_ff1, v_w_ff1, parts=True)
    step("w_ff2", w_ff2, r_ff2, m_w_ff2, v_w_ff2, parts=True)
    step("final_norm_w", final_norm_w.reshape(1, D), g_final_norm_w, m_final_norm_w.reshape(1, D),
         v_final_norm_w.reshape(1, D))
    upd["final_norm_w"] = tuple(a.reshape(D) for a in upd["final_norm_w"])

    loss = lax.psum(loss_v[0, 0], ("x", "y", "c"))
    order = ["w_mod", "b_mod", "w_in", "gm_norm_w", "gm_ws", "gm_bs", "conv_w", "conv_b", "dt_bias", "a_log",
             "d_skip", "ssm_norm_w", "w_branch_gm", "w_branch_ssm", "w_out", "w_ff1", "w_ff2", "final_norm_w"]
    return (loss, grad_x.reshape(1, T, D),
            *[upd[n][0] for n in order], *[upd[n][1] for n in order],
            *[upd[n][2] for n in order], *[upd[n][3] for n in order])
```

```python
import functools

import jax
import jax.numpy as jnp
from jax import lax
from jax.experimental import pallas as pl
from jax.experimental.pallas import tpu as pltpu

F32 = jnp.float32
BF16 = jnp.bfloat16
MESH = pl.DeviceIdType.MESH
HIGHEST = lax.Precision.HIGHEST

N_DEV = 8
D = 1024
Q = 128
GM_GROUPS = 8
SSM_INNER = 2048
SSM_GROUPS = 8
SSM_HPG = 4
SSM_P = 64
SSM_GW = SSM_HPG * SSM_P
CONV_DIM = 4096
CONV_K = 4
D_FF = 4096
N_MOD = 6
EPS = 1e-6
IN_WIDTH = 10272
OFF_DT = 8192
OFF_GA = 8224
PROJ_W = 10240
ALL_W = 10368
P_U, P_V, P_Z, P_XBC, P_GA, P_GB = 0, 1024, 2048, 4096, 8192, 9216

ADAM_LR = 0.001
ADAM_B1 = 0.9
ADAM_B2 = 0.999
ADAM_EPS = 1e-08
ADAM_WD = 0.01
ADAM_STEP = 10

VMEM_LIMIT_BYTES = 48 * 1024 * 1024
FUSED_TM = 256
WGRAD_TK = 2048
EARLY_ROWS = (8, 1024, 8)
MID_ROWS = (128, 32, 8, 8, 8, 16, 8, 32)
LATE_ROWS = (8, 8)


def _pack_rows(arrs, rows):
    def rows128(a, r):
        a = a.reshape(-1)
        return jnp.pad(a, (0, r * 128 - a.shape[0])).reshape(r, 128)
    return jnp.concatenate([rows128(a, r) for a, r in zip(arrs, rows)], axis=0)


def _unpack_rows(s, rows):
    out, o = [], 0
    for r in rows:
        out.append(s[o:o + r].reshape(-1))
        o += r
    return out


def _params(sem=None):
    return pltpu.CompilerParams(dimension_semantics=sem, vmem_limit_bytes=VMEM_LIMIT_BYTES)


def _dg(a, b, ca, cb):
    return lax.dot_general(a.astype(BF16), b.astype(BF16), (((ca,), (cb,)), ((), ())),
                           preferred_element_type=F32)


@jax.custom_vjp
def dot_nn(a, b):
    return _dg(a, b, 1, 0)


@jax.custom_vjp
def dot_nt(a, b):
    return _dg(a, b, 1, 1)


@jax.custom_vjp
def dot_tn(a, b):
    return _dg(a, b, 0, 0)


def _like(ct, primal):
    return ct.astype(primal.dtype)


dot_nn.defvjp(lambda a, b: (dot_nn(a, b), (a, b)),
              lambda r, g: (_like(dot_nt(g, r[1]), r[0]), _like(dot_tn(r[0], g), r[1])))
dot_nt.defvjp(lambda a, b: (dot_nt(a, b), (a, b)),
              lambda r, g: (_like(dot_nn(g, r[1]), r[0]), _like(dot_tn(g, r[0]), r[1])))
dot_tn.defvjp(lambda a, b: (dot_tn(a, b), (a, b)),
              lambda r, g: (_like(dot_nt(r[1], g), r[0]), _like(dot_nn(r[0], g), r[1])))


def _rms(x):
    return x * lax.rsqrt(jnp.mean(x * x, axis=-1, keepdims=True) + EPS)


def _softplus(x):
    return jnp.maximum(x, 0.0) + jnp.log1p(jnp.exp(-jnp.abs(x)))


def _rows(n):
    return lambda ref, j: ref.at[pl.ds(pl.multiple_of(j * n, n), n)]


def _cols(n):
    return lambda ref, j: ref.at[:, pl.ds(pl.multiple_of(j * n, n), n)]


def _slot(ref, j):
    return ref.at[j]


def _whole(ref, j):
    return ref


def exchange(items, *, name):
    n = len(items)

    def body(*refs):
        exchange_in_body(items, refs[:n], refs[n:2 * n], refs[2 * n:], True, True)

    return pl.pallas_call(
        body, name=name,
        out_shape=exchange_out_shapes(items),
        in_specs=[pl.BlockSpec(memory_space=pl.ANY)] * n,
        out_specs=[pl.BlockSpec(memory_space=pl.ANY)] * n,
        scratch_shapes=exchange_semaphores(items),
    )(*[it[0] for it in items])


def exchange_out_shapes(items):
    return [jax.ShapeDtypeStruct(tuple(shape), src.dtype) for (src, _, shape, _) in items]


def exchange_semaphores(items):
    n = len(items)
    return [pltpu.SemaphoreType.DMA((n, N_DEV - 1)), pltpu.SemaphoreType.DMA((n, N_DEV - 1)),
            pltpu.SemaphoreType.DMA((n,))]


def _exchange_copies(items, src_refs, out_refs, sems):
    send_sems, recv_sems, local_sems = sems
    x = lax.axis_index("x")
    y = lax.axis_index("y")
    c = lax.axis_index("c")
    me = 4 * x + 2 * y + c
    local = [pltpu.make_async_copy(src_win(src_refs[i], me), dst_win(out_refs[i], me), local_sems.at[i])
             for i, (_, src_win, _, dst_win) in enumerate(items)]
    remote = []
    for i, (_, src_win, _, dst_win) in enumerate(items):
        for k in range(1, N_DEV):
            px = lax.rem(x + ((k >> 2) & 1), 2)
            py = lax.rem(y + ((k >> 1) & 1), 2)
            pc = lax.rem(c + (k & 1), 2)
            peer = 4 * px + 2 * py + pc
            remote.append(pltpu.make_async_remote_copy(
                src_ref=src_win(src_refs[i], peer), dst_ref=dst_win(out_refs[i], me),
                send_sem=send_sems.at[i, k - 1], recv_sem=recv_sems.at[i, k - 1],
                device_id=(px, py, pc), device_id_type=MESH))
    return local, remote


def _when(cond, fn):
    if cond is True:
        fn()
    else:
        pl.when(cond)(fn)


def exchange_start(items, src_refs, out_refs, sems, cond):
    def start():
        local, remote = _exchange_copies(items, src_refs, out_refs, sems)
        for cp in local + remote:
            cp.start()
    _when(cond, start)


def exchange_finish(items, src_refs, out_refs, sems, cond):
    def finish():
        local, remote = _exchange_copies(items, src_refs, out_refs, sems)
        for cp in remote:
            cp.wait_send()
        for cp in remote:
            cp.wait_recv()
        for cp in local:
            cp.wait()
    _when(cond, finish)


def exchange_in_body(items, src_refs, out_refs, sems, first, last):
    exchange_start(items, src_refs, out_refs, sems, first)
    exchange_finish(items, src_refs, out_refs, sems, last)


def gather_blocks(src, *, name):
    return exchange([(src, _whole, (N_DEV,) + src.shape, _slot)], name=name)[0]


def gather_blocks_two_level(src, *, name):
    def body(src_ref, out_ref, send_sems, recv_sems, local_sem):
        x = lax.axis_index("x")
        y = lax.axis_index("y")
        c = lax.axis_index("c")
        me, sibling = (x, y, c), (x, y, 1 - c)
        chips = [(1 - x, y), (x, 1 - y), (1 - x, 1 - y)]

        def slot(px, py, pc):
            return out_ref.at[4 * px + 2 * py + pc]

        def copy(k, block, to, src=None):
            return pltpu.make_async_remote_copy(
                src_ref=slot(*block) if src is None else src, dst_ref=slot(*block),
                send_sem=send_sems.at[k], recv_sem=recv_sems.at[k], device_id=to, device_id_type=MESH)

        mine = pltpu.make_async_copy(src_ref, slot(*me), local_sem)
        mine.start()
        first = [copy(0, me, sibling, src=src_ref)]
        first += [copy(1 + j, me, (*chip, c), src=src_ref) for j, chip in enumerate(chips)]
        for cp in first:
            cp.start()
        passed = [copy(4 + j, (*chip, c), sibling) for j, chip in enumerate(chips)]
        for j, chip in enumerate(chips):
            copy(1 + j, (*chip, c), me).wait_recv()
            passed[j].start()
        copy(0, sibling, me).wait_recv()
        for j, chip in enumerate(chips):
            copy(4 + j, (*chip, 1 - c), me).wait_recv()
        for cp in first + passed:
            cp.wait_send()
        mine.wait()

    return pl.pallas_call(
        body, name=name,
        out_shape=jax.ShapeDtypeStruct((N_DEV,) + src.shape, src.dtype),
        in_specs=[pl.BlockSpec(memory_space=pl.ANY)],
        out_specs=pl.BlockSpec(memory_space=pl.ANY),
        scratch_shapes=[pltpu.SemaphoreType.DMA((N_DEV - 1,)), pltpu.SemaphoreType.DMA((N_DEV - 1,)),
                        pltpu.SemaphoreType.DMA(())],
    )(src)


def sum_devices(g, *, tr, name):
    _, R, C = g.shape

    def body(g_ref, o_ref):
        acc = g_ref[0].astype(F32)
        for j in range(1, N_DEV):
            acc = acc + g_ref[j].astype(F32)
        o_ref[...] = acc

    return pl.pallas_call(
        body, name=name, grid=(R // tr,),
        out_shape=jax.ShapeDtypeStruct((R, C), F32),
        in_specs=[pl.BlockSpec((N_DEV, tr, C), lambda i: (0, i, 0))],
        out_specs=pl.BlockSpec((tr, C), lambda i: (i, 0)),
        compiler_params=_params(("parallel",)),
    )(g)


def matmul(a, b, mode, out_dtype, *, name, tm=1024, tn=1024, tk=1024, n=None, comm=None,
           a_pro=None, epi=None, epi_ins=()):
    if mode == "nn":
        (M, K), (K2, N) = a.shape, b.shape
    elif mode == "nt":
        (M, K), (N, K2) = a.shape, b.shape
    else:
        (K, M), (K2, N) = a.shape, b.shape
    assert K == K2
    N = N if n is None else n
    tm, tn, tk = min(tm, M), min(tn, N), min(tk, K)
    assert M % tm == 0 and N % tn == 0 and K % tk == 0, (name, M, N, K, tm, tn, tk)
    nk = K // tk
    if mode == "tn":
        a_spec = pl.BlockSpec((tk, tm), lambda i, j, k: (k, i))
    else:
        a_spec = pl.BlockSpec((tm, tk), lambda i, j, k: (i, k))
    if mode == "nt":
        b_spec = pl.BlockSpec((tn, tk), lambda i, j, k: (j, k))
    else:
        b_spec = pl.BlockSpec((tk, tn), lambda i, j, k: (k, j))
    dims = {"nn": (1, 0), "nt": (1, 1), "tn": (0, 0)}[mode]
    items = list(comm) if comm else []
    nx = len(items)
    ne = len(epi_ins)
    gm, gn = M // tm, N // tn
    any_spec = pl.BlockSpec(memory_space=pl.ANY)
    o_spec = pl.BlockSpec((tm, tn), lambda i, j, k: (i, j))

    def body(*refs):
        a_ref, b_ref, e_refs = refs[0], refs[1], refs[2:2 + ne]
        refs = refs[2 + ne:]
        src_refs, o_ref, out_refs = refs[:nx], refs[nx], refs[1 + nx:1 + 2 * nx]
        acc_ref, sems = refs[1 + 2 * nx], refs[2 + 2 * nx:]
        i, j, k = pl.program_id(0), pl.program_id(1), pl.program_id(2)
        if items:
            exchange_start(items, src_refs, out_refs, sems, (i == 0) & (j == 0) & (k == 0))
        a_tile = a_ref[...] if a_pro is None else a_pro(a_ref[...])
        part = lax.dot_general(a_tile, b_ref[...], (((dims[0],), (dims[1],)), ((), ())),
                               preferred_element_type=F32)

        def finish(acc):
            if epi is not None:
                acc = epi(acc, *[e[...] for e in e_refs])
            o_ref[...] = acc.astype(o_ref.dtype)

        if nk == 1:
            finish(part)
        else:
            @pl.when(k == 0)
            def _():
                acc_ref[...] = part

            @pl.when((k > 0) & (k < nk - 1))
            def _():
                acc_ref[...] += part

            @pl.when(k == nk - 1)
            def _():
                finish(acc_ref[...] + part)

        if items:
            exchange_finish(items, src_refs, out_refs, sems, (i == gm - 1) & (j == gn - 1) & (k == nk - 1))

    res = pl.pallas_call(
        body, name=name, grid=(gm, gn, nk),
        out_shape=[jax.ShapeDtypeStruct((M, N), out_dtype)] + exchange_out_shapes(items),
        in_specs=[a_spec, b_spec] + [o_spec] * ne + [any_spec] * nx,
        out_specs=[o_spec] + [any_spec] * nx,
        scratch_shapes=[pltpu.VMEM((tm, tn) if nk > 1 else (8, 128), F32)]
        + (exchange_semaphores(items) if items else []),
        compiler_params=_params(("arbitrary",) * 3 if items else ("parallel", "parallel", "arbitrary")),
    )(a, b, *epi_ins, *[it[0] for it in items])
    return res if items else res[0]


def rowwise_call(body_fn, rows, fulls, row_outs, acc_outs, *, tm, name, mm=None, tk=1024):
    rows = [r if isinstance(r, tuple) else (r, r.shape[1], 0) for r in rows]
    T = rows[0][0].shape[0]
    tm = min(tm, T)
    assert T % tm == 0
    n_r, n_f, n_ro = len(rows), len(fulls), len(row_outs)
    into = [(k, ro) for k, ro in enumerate(row_outs) if len(ro) == 3]
    n_b = len(into)
    n_mm, nk = 0, 1
    if mm is not None:
        a, b, mode, pos, a_pro = mm
        n_mm = 2
        K = a.shape[1]
        N = b.shape[1] if mode == "nn" else b.shape[0]
        tk = min(tk, K)
        assert K % tk == 0 and a.shape[0] == T
        nk = K // tk
        b_contract = 0 if mode == "nn" else 1

    def row_body(refs, product):
        r_refs = refs[:n_r]
        f_refs = refs[n_r:n_r + n_f]
        refs = refs[n_r + n_f + n_b:]
        ro_refs = refs[:n_ro]
        ao_refs = refs[n_ro:n_ro + len(acc_outs)]
        r_vals = [r[...].astype(F32) for r in r_refs]
        if product is not None:
            r_vals.insert(pos, product)
        f_vals = [f[...].astype(F32) for f in f_refs]
        ro, ao = body_fn(r_vals, f_vals)
        for ref, v in zip(ro_refs, ro):
            ref[...] = v.astype(ref.dtype)
        if ao_refs:
            @pl.when(pl.program_id(0) == 0)
            def _():
                for ref in ao_refs:
                    ref[...] = jnp.zeros(ref.shape, F32)
            for ref, v in zip(ao_refs, ao):
                ref[...] += v.reshape(ref.shape)

    def body(*refs):
        if mm is None:
            return row_body(refs, None)
        a_ref, b_ref, rest, acc_ref = refs[0], refs[1], refs[2:-1], refs[-1]
        k = pl.program_id(1)
        a_tile = a_ref[...] if a_pro is None else a_pro(a_ref[...])
        part = lax.dot_general(a_tile, b_ref[...], (((1,), (b_contract,)), ((), ())),
                               preferred_element_type=F32)
        if nk == 1:
            return row_body(rest, part)

        @pl.when(k == 0)
        def _():
            acc_ref[...] = part

        @pl.when((k > 0) & (k < nk - 1))
        def _():
            acc_ref[...] += part

        @pl.when(k == nk - 1)
        def _():
            row_body(rest, acc_ref[...] + part)

    def full_spec(shape):
        nd = len(shape)
        return pl.BlockSpec(tuple(shape), lambda i, *_: (0,) * nd)

    def row_spec(w, off):
        return pl.BlockSpec((tm, w), functools.partial(lambda i, *_, o: (i, o), o=off // w))

    in_specs = []
    if mm is not None:
        in_specs.append(pl.BlockSpec((tm, tk), lambda i, k: (i, k)))
        in_specs.append(pl.BlockSpec((tk, N), lambda i, k: (k, 0)) if mode == "nn" else
                        pl.BlockSpec((N, tk), lambda i, k: (0, k)))
    in_specs += [row_spec(w, off) for (_, w, off) in rows]
    in_specs += [full_spec(f.shape) for f in fulls]
    in_specs += [pl.BlockSpec(memory_space=pl.ANY)] * n_b
    out_specs, out_shape = [], []
    for ro in row_outs:
        if len(ro) == 3:
            buf, w, off = ro
            out_specs.append(row_spec(w, off))
            out_shape.append(jax.ShapeDtypeStruct(buf.shape, buf.dtype))
        else:
            w, dt = ro
            out_specs.append(row_spec(w, 0))
            out_shape.append(jax.ShapeDtypeStruct((T, w), dt))
    out_specs += [full_spec(s) for s in acc_outs]
    out_shape += [jax.ShapeDtypeStruct(tuple(s), F32) for s in acc_outs]
    aliases = {n_mm + n_r + n_f + b: k for b, (k, _) in enumerate(into)}
    return pl.pallas_call(
        body, name=name, grid=(T // tm,) if mm is None else (T // tm, nk),
        out_shape=out_shape, in_specs=in_specs, out_specs=out_specs,
        scratch_shapes=[] if mm is None else [pltpu.VMEM((tm, N) if nk > 1 else (8, 128), F32)],
        input_output_aliases=aliases,
        compiler_params=_params(("arbitrary",) if mm is None else ("arbitrary", "arbitrary")),
    )(*([] if mm is None else [a, b]), *[r[0] for r in rows], *fulls, *[ro[0] for _, ro in into])


def fwd_body(fn):
    return lambda r, f: (fn(*r, *f), ())


def bwd_body(fn, n_rows):
    def body(r, f):
        ins, cots = r[:n_rows], r[n_rows:]
        _, vjp = jax.vjp(fn, *ins, *f)
        g = vjp(tuple(cots))
        return g[:n_rows], g[n_rows:]
    return body


def whole_call(fn, ins, outs, *, name):
    n_in = len(ins)

    def body(*refs):
        res = fn(*[r[...] for r in refs[:n_in]])
        for ref, v in zip(refs[n_in:], res):
            ref[...] = v.astype(ref.dtype)

    return pl.pallas_call(
        body, name=name,
        out_shape=[jax.ShapeDtypeStruct(tuple(s), dt) for (s, dt) in outs],
        compiler_params=_params(),
    )(*ins)


def fn_modulate(x, sc, sh):
    return (_rms(x) * (1.0 + sc) + sh,)


def fn_sgu(u, v, nw, ws, bs):
    ug = jax.nn.gelu(u)
    vn = _rms(jax.nn.gelu(v)) * nw
    ri = lax.broadcasted_iota(jnp.int32, (Q, Q), 0)
    ci = lax.broadcasted_iota(jnp.int32, (Q, Q), 1)
    causal = ri >= ci
    chunks = []
    for n in range(u.shape[0] // Q):
        vc = vn[n * Q:(n + 1) * Q]
        cols = [dot_nn(jnp.where(causal, ws[g], 0.0), vc[:, g * Q:(g + 1) * Q]) + bs[g]
                for g in range(GM_GROUPS)]
        chunks.append(jnp.concatenate(cols, axis=1))
    sv = chunks[0] if len(chunks) == 1 else jnp.concatenate(chunks, axis=0)
    return (ug * sv,)


def fn_mix(ga, gb, pa, pb):
    return (jax.nn.sigmoid(ga) * pa + jax.nn.sigmoid(gb) * pb,)


def fn_res_modulate(x, o, g1, sc2, sh2):
    x1 = x + g1 * o
    return x1, _rms(x1) * (1.0 + sc2) + sh2


def relu2_tile(f):
    return jnp.square(jnp.maximum(f.astype(F32), 0.0)).astype(BF16)


def relu2_grad_tile(dact, f):
    return dact * (2.0 * jnp.maximum(f.astype(F32), 0.0))


def final_body(r, f):
    x1, gf, tgt = r
    g2, fnw = f

    def loss_fn(x1, gf, g2, fnw):
        y = _rms(x1 + g2 * gf) * fnw
        row = 0.5 * jnp.mean(jnp.square(y - tgt), axis=-1, keepdims=True)
        return jnp.sum(row, axis=0, keepdims=True)

    l, vjp = jax.vjp(loss_fn, x1, gf, g2, fnw)
    dx1, dgf, dg2, dfnw = vjp(jnp.ones((1, 1), F32))
    return (dx1, dgf), (jnp.broadcast_to(l, (1, 128)), dg2, dfnw)


def grad_x_body(r, f):
    x, dh, dxa = r
    _, vjp = jax.vjp(fn_modulate, x, *f)
    dx, dsc, dsh = vjp((dh,))
    return (dx + dxa,), (dsc, dsh)


CONV_CW = 128
CONV_PAD = 8
CONV_ROWS = 128


def _conv_pre(xp, w_ref, b_ref, r0, R):
    acc = b_ref[...] + w_ref[0:1, :] * xp[r0 + CONV_PAD - 3:r0 + CONV_PAD - 3 + R, :]
    for k in range(1, CONV_K):
        s = r0 + CONV_PAD - 3 + k
        acc = acc + w_ref[k:k + 1, :] * xp[s:s + R, :]
    return acc


def conv_fwd(proj, conv_w, conv_b):
    T = proj.shape[0]
    R = min(CONV_ROWS, T)

    def body(x_ref, w_ref, b_ref, o_ref, xp):
        xp[0:CONV_PAD, :] = jnp.zeros((CONV_PAD, CONV_CW), F32)
        xp[CONV_PAD:CONV_PAD + T, :] = x_ref[...].astype(F32)
        for r0 in range(0, T, R):
            pre = _conv_pre(xp, w_ref, b_ref, r0, R)
            o_ref[r0:r0 + R, :] = (pre * jax.nn.sigmoid(pre)).astype(o_ref.dtype)

    return pl.pallas_call(
        body, name="conv_fwd", grid=(CONV_DIM // CONV_CW,),
        out_shape=jax.ShapeDtypeStruct((T, CONV_DIM), BF16),
        in_specs=[pl.BlockSpec((T, CONV_CW), lambda j: (0, P_XBC // CONV_CW + j)),
                  pl.BlockSpec((CONV_K, CONV_CW), lambda j: (0, j)),
                  pl.BlockSpec((1, CONV_CW), lambda j: (0, j))],
        out_specs=pl.BlockSpec((T, CONV_CW), lambda j: (0, j)),
        scratch_shapes=[pltpu.VMEM((T + CONV_PAD, CONV_CW), F32)],
        compiler_params=_params(("parallel",)),
    )(proj, conv_w, conv_b)


def conv_bwd(proj, dact, col0, conv_w, conv_b, dproj, *, name):
    T = proj.shape[0]
    R = min(CONV_ROWS, T)
    nb = dact.shape[1] // CONV_CW
    c0 = col0 // CONV_CW
    x0 = (P_XBC + col0) // CONV_CW

    def body(x_ref, d_ref, w_ref, b_ref, _, dx_ref, dw_ref, db_ref, xp, dp):
        xp[0:CONV_PAD, :] = jnp.zeros((CONV_PAD, CONV_CW), F32)
        xp[CONV_PAD:CONV_PAD + T, :] = x_ref[...].astype(F32)
        dp[T:T + CONV_PAD, :] = jnp.zeros((CONV_PAD, CONV_CW), F32)
        dws = [jnp.zeros((1, CONV_CW), F32) for _ in range(CONV_K)]
        db = jnp.zeros((1, CONV_CW), F32)
        for r0 in range(0, T, R):
            pre = _conv_pre(xp, w_ref, b_ref, r0, R)
            s = jax.nn.sigmoid(pre)
            dpre = d_ref[r0:r0 + R, :].astype(F32) * (s * (1.0 + pre * (1.0 - s)))
            dp[r0:r0 + R, :] = dpre
            db = db + jnp.sum(dpre, axis=0, keepdims=True)
            for k in range(CONV_K):
                st = r0 + CONV_PAD - 3 + k
                dws[k] = dws[k] + jnp.sum(dpre * xp[st:st + R, :], axis=0, keepdims=True)
        for r0 in range(0, T, R):
            acc = w_ref[0:1, :] * dp[r0 + 3:r0 + 3 + R, :]
            for k in range(1, CONV_K):
                acc = acc + w_ref[k:k + 1, :] * dp[r0 + 3 - k:r0 + 3 - k + R, :]
            dx_ref[r0:r0 + R, :] = acc.astype(dx_ref.dtype)
        for k in range(CONV_K):
            dw_ref[k:k + 1, :] = dws[k]
        db_ref[...] = db

    return pl.pallas_call(
        body, name=name, grid=(nb,),
        out_shape=[jax.ShapeDtypeStruct(dproj.shape, dproj.dtype),
                   jax.ShapeDtypeStruct((CONV_K, nb * CONV_CW), F32),
                   jax.ShapeDtypeStruct((1, nb * CONV_CW), F32)],
        in_specs=[pl.BlockSpec((T, CONV_CW), lambda j: (0, x0 + j)),
                  pl.BlockSpec((T, CONV_CW), lambda j: (0, j)),
                  pl.BlockSpec((CONV_K, CONV_CW), lambda j: (0, c0 + j)),
                  pl.BlockSpec((1, CONV_CW), lambda j: (0, c0 + j)),
                  pl.BlockSpec(memory_space=pl.ANY)],
        out_specs=[pl.BlockSpec((T, CONV_CW), lambda j: (0, x0 + j)),
                   pl.BlockSpec((CONV_K, CONV_CW), lambda j: (0, j)),
                   pl.BlockSpec((1, CONV_CW), lambda j: (0, j))],
        scratch_shapes=[pltpu.VMEM((T + CONV_PAD, CONV_CW), F32),
                        pltpu.VMEM((T + CONV_PAD, CONV_CW), F32)],
        input_output_aliases={4: 0},
        compiler_params=_params(("parallel",)),
    )(proj, dact, conv_w, conv_b, dproj)


def _split3(a):
    hi = a.astype(BF16)
    r = a - hi.astype(F32)
    mid = r.astype(BF16)
    return hi, mid, (r - mid.astype(F32)).astype(BF16)


def _dg3(a, m, ca, cm, a_first):
    dims = (((ca,), (cm,)), ((), ())) if a_first else (((cm,), (ca,)), ((), ()))
    out = None
    for p in _split3(a):
        t = lax.dot_general(p, m, dims, preferred_element_type=F32) if a_first else \
            lax.dot_general(m, p, dims, preferred_element_type=F32)
        out = t if out is None else out + t
    return out


@jax.custom_vjp
def exact_right(a, m):
    return _dg3(a, m, 1, 0, True)


@jax.custom_vjp
def exact_left(m, a):
    return _dg3(a, m, 0, 1, False)


def _expand_bwd(m, g):
    hi = g.astype(BF16)
    lo = (g - hi.astype(F32)).astype(BF16)
    out = lax.dot_general(jnp.concatenate([hi, lo], axis=1), jnp.concatenate([m, m], axis=1),
                          (((1,), (1,)), ((), ())), preferred_element_type=F32)
    return out, jnp.zeros_like(m)


exact_right.defvjp(lambda a, m: (exact_right(a, m), m), _expand_bwd)
exact_left.defvjp(lambda m, a: (exact_left(m, a), m),
                  lambda m, g: (jnp.zeros_like(m), _dg3(g, m, 0, 0, False)))


def ssd_step(lane0, state, x, z, dtr, Bm, Cm, dtb, alog, dsk, nw):
    def iota(shape, dim):
        return lax.broadcasted_iota(jnp.int32, shape, dim)

    def one_hot(mask):
        return mask.astype(F32).astype(BF16)

    causal = iota((Q, Q), 0) >= iota((Q, Q), 1)
    eye = iota((Q, Q), 0) == iota((Q, Q), 1)
    lane = iota((1, 128), 1)
    colh = lax.shift_right_logical(iota((1, SSM_GW), 1), 6)
    to_cols = one_hot(iota((128, SSM_GW), 0) == lane0 + colh)

    dt_all = _softplus(dtr + dtb)
    a_all = dt_all * (-jnp.exp(alog))
    cum_all = exact_left(one_hot(causal), a_all)
    both = exact_right(jnp.concatenate([dt_all, cum_all], axis=0), to_cols)
    dt_f, cum_f = both[:Q], both[Q:]
    last_f = jnp.sum(jnp.where(iota((Q, 1), 0) == Q - 1, cum_f, 0.0), axis=0, keepdims=True)
    dsk_f = jnp.zeros((1, SSM_GW), F32)
    for h in range(SSM_HPG):
        dsk_f = jnp.where(colh == h, dsk[h], dsk_f)

    xdt = x * dt_f
    cb = dot_nt(Cm, Bm)
    ms, rhs = [], []
    for h in range(SSM_HPG):
        ch = jnp.sum(jnp.where(lane == lane0 + h, cum_all, 0.0), axis=1, keepdims=True)
        ch_t = jnp.sum(jnp.where(eye, ch, 0.0), axis=0, keepdims=True)
        ms.append(cb * jnp.exp(jnp.where(causal, ch - ch_t, -1e30)))
        rhs.append(jnp.where(colh == h, xdt, 0.0))
    y = dot_nn(jnp.concatenate(ms, axis=1), jnp.concatenate(rhs, axis=0))
    y = y + dot_nn(Cm, state) * jnp.exp(cum_f) + x * dsk_f
    new_state = state * jnp.exp(last_f) + dot_tn(Bm, xdt * jnp.exp(last_f - cum_f))
    gated = y * (z * jax.nn.sigmoid(z))
    return new_state, _rms(gated) * nw


SSD_GPS = 4
_XW = SSD_GPS * SSM_GW
_BW = SSD_GPS * 128


def _ssd_in_specs(rev, nc):
    def n_of(n):
        return nc - 1 - n if rev else n
    return [
        pl.BlockSpec((Q, _XW), lambda g, n: (n_of(n), g)),
        pl.BlockSpec((Q, _BW), lambda g, n: (n_of(n), SSM_INNER // _BW + g)),
        pl.BlockSpec((Q, _BW), lambda g, n: (n_of(n), (SSM_INNER + SSM_GROUPS * 128) // _BW + g)),
        pl.BlockSpec((Q, _XW), lambda g, n: (n_of(n), P_Z // _XW + g)),
        pl.BlockSpec((Q, 128), lambda g, n: (n_of(n), 0)),
        pl.BlockSpec((1, 128), lambda g, n: (0, 0)),
        pl.BlockSpec((1, 128), lambda g, n: (0, 0)),
        pl.BlockSpec((SSD_GPS, SSM_HPG, 1, 1), lambda g, n: (g, 0, 0, 0)),
        pl.BlockSpec((1, _XW), lambda g, n: (0, g)),
    ]


def _ssd_group_inputs(gi, x_ref, b_ref, c_ref, z_ref, dt_ref, dtb_ref, al_ref, dk_ref, nw_ref):
    xs = slice(gi * SSM_GW, (gi + 1) * SSM_GW)
    bs = slice(gi * 128, (gi + 1) * 128)
    return (x_ref[:, xs].astype(F32), z_ref[:, xs].astype(F32), dt_ref[...],
            b_ref[:, bs], c_ref[:, bs],
            dtb_ref[...], al_ref[...], dk_ref[gi], nw_ref[:, xs])


def ssd_fwd(xact, proj, dtg, dtb, alog, dsk, nw, comm):
    T = xact.shape[0]
    nc = T // Q
    nx = len(comm)
    ng = SSM_GROUPS // SSD_GPS
    any_spec = pl.BlockSpec(memory_space=pl.ANY)

    def body(*refs):
        in_refs, src_refs = refs[:9], refs[9:9 + nx]
        yb_ref, st_ref = refs[9 + nx:11 + nx]
        out_refs, state, sems = refs[11 + nx:11 + 2 * nx], refs[11 + 2 * nx], refs[12 + 2 * nx:]
        g, n = pl.program_id(0), pl.program_id(1)
        exchange_start(comm, src_refs, out_refs, sems, (g == 0) & (n == 0))

        @pl.when(n == 0)
        def _():
            state[...] = jnp.zeros(state.shape, F32)

        for gi in range(SSD_GPS):
            lane0 = SSM_HPG * (SSD_GPS * pl.program_id(0) + gi)
            s = state[gi]
            st_ref[gi, 0] = s
            new_s, yb = ssd_step(lane0, s, *_ssd_group_inputs(gi, *in_refs))
            state[gi] = new_s
            yb_ref[:, gi * SSM_GW:(gi + 1) * SSM_GW] = yb.astype(yb_ref.dtype)

        exchange_finish(comm, src_refs, out_refs, sems, (g == ng - 1) & (n == nc - 1))

    return pl.pallas_call(
        body, name="ssd_fwd", grid=(ng, nc),
        out_shape=[jax.ShapeDtypeStruct((T, SSM_INNER), BF16),
                   jax.ShapeDtypeStruct((SSM_GROUPS, nc, 128, SSM_GW), F32)] + exchange_out_shapes(comm),
        in_specs=_ssd_in_specs(False, nc) + [any_spec] * nx,
        out_specs=[pl.BlockSpec((Q, _XW), lambda g, n: (n, g)),
                   pl.BlockSpec((SSD_GPS, 1, 128, SSM_GW), lambda g, n: (g, n, 0, 0))] + [any_spec] * nx,
        scratch_shapes=[pltpu.VMEM((SSD_GPS, 128, SSM_GW), F32)] + exchange_semaphores(comm),
        compiler_params=_params(("arbitrary", "arbitrary")),
    )(xact, xact, xact, proj, dtg, dtb, alog, dsk, nw, *[it[0] for it in comm])


def ssd_bwd(xact, proj, dtg, dtb, alog, dsk, nw, states, dyb, dproj, comm):
    T = xact.shape[0]
    nc = T // Q

    nx = len(comm)
    ng = SSM_GROUPS // SSD_GPS
    any_spec = pl.BlockSpec(memory_space=pl.ANY)

    def body(*refs):
        in_refs, (st_ref, dy_ref, _) = refs[:9], refs[9:12]
        src_refs, refs = refs[12:12 + nx], refs[12 + nx:]
        dx_ref, db_ref, dc_ref, dz_ref, ddt_ref, ddtb_ref, dal_ref, ddk_ref, dnw_ref = refs[:9]
        out_refs, dstate, sems = refs[9:9 + nx], refs[9 + nx], refs[10 + nx:]
        exchange_start(comm, src_refs, out_refs, sems, (pl.program_id(0) == 0) & (pl.program_id(1) == 0))

        @pl.when(pl.program_id(1) == 0)
        def _():
            dstate[...] = jnp.zeros(dstate.shape, F32)
            ddtb_ref[...] = jnp.zeros(ddtb_ref.shape, F32)
            dal_ref[...] = jnp.zeros(dal_ref.shape, F32)
            ddk_ref[...] = jnp.zeros(ddk_ref.shape, F32)
            dnw_ref[...] = jnp.zeros(dnw_ref.shape, F32)

        for gi in range(SSD_GPS):
            xs = slice(gi * SSM_GW, (gi + 1) * SSM_GW)
            bs = slice(gi * 128, (gi + 1) * 128)
            lane0 = SSM_HPG * (SSD_GPS * pl.program_id(0) + gi)
            ins = (st_ref[gi, 0],) + _ssd_group_inputs(gi, *in_refs)
            _, vjp = jax.vjp(functools.partial(ssd_step, lane0), *ins)
            ds, dx, dz, ddt, dbm, dcm, ddtb, dal, ddk, dnw = vjp((dstate[gi], dy_ref[:, xs].astype(F32)))
            dstate[gi] = ds
            dx_ref[:, xs] = dx.astype(dx_ref.dtype)
            db_ref[:, bs] = dbm.astype(db_ref.dtype)
            dc_ref[:, bs] = dcm.astype(dc_ref.dtype)
            dz_ref[:, xs] = dz.astype(dz_ref.dtype)
            ddt_ref[:, bs] = ddt
            ddtb_ref[gi] += ddtb
            dal_ref[gi] += dal
            ddk_ref[gi] += ddk
            dnw_ref[:, xs] += dnw

        exchange_finish(comm, src_refs, out_refs, sems,
                        (pl.program_id(0) == ng - 1) & (pl.program_id(1) == nc - 1))

    rev = lambda n: nc - 1 - n
    row_shape = jax.ShapeDtypeStruct((SSM_GROUPS, 1, 128), F32)
    row_spec = pl.BlockSpec((SSD_GPS, 1, 128), lambda g, n: (g, 0, 0))
    return pl.pallas_call(
        body, name="ssd_bwd", grid=(ng, nc),
        out_shape=[jax.ShapeDtypeStruct((T, SSM_INNER), BF16),
                   jax.ShapeDtypeStruct((T, SSM_GROUPS * 128), BF16),
                   jax.ShapeDtypeStruct((T, SSM_GROUPS * 128), BF16),
                   jax.ShapeDtypeStruct(dproj.shape, dproj.dtype),
                   jax.ShapeDtypeStruct((T, SSM_GROUPS * 128), F32),
                   row_shape, row_shape,
                   jax.ShapeDtypeStruct((SSM_GROUPS, SSM_HPG, 1, 1), F32),
                   jax.ShapeDtypeStruct((1, SSM_INNER), F32)] + exchange_out_shapes(comm),
        in_specs=_ssd_in_specs(True, nc) + [
            pl.BlockSpec((SSD_GPS, 1, 128, SSM_GW), lambda g, n: (g, rev(n), 0, 0)),
            pl.BlockSpec((Q, _XW), lambda g, n: (rev(n), g)),
            any_spec] + [any_spec] * nx,
        out_specs=[pl.BlockSpec((Q, _XW), lambda g, n: (rev(n), g)),
                   pl.BlockSpec((Q, _BW), lambda g, n: (rev(n), g)),
                   pl.BlockSpec((Q, _BW), lambda g, n: (rev(n), g)),
                   pl.BlockSpec((Q, _XW), lambda g, n: (rev(n), P_Z // _XW + g)),
                   pl.BlockSpec((Q, _BW), lambda g, n: (rev(n), g)),
                   row_spec, row_spec,
                   pl.BlockSpec((SSD_GPS, SSM_HPG, 1, 1), lambda g, n: (g, 0, 0, 0)),
                   pl.BlockSpec((1, _XW), lambda g, n: (0, g))] + [any_spec] * nx,
        scratch_shapes=[pltpu.VMEM((SSD_GPS, 128, SSM_GW), F32)] + exchange_semaphores(comm),
        input_output_aliases={11: 3},
        compiler_params=_params(("arbitrary", "arbitrary")),
    )(xact, xact, xact, proj, dtg, dtb, alog, dsk, nw, states, dyb, dproj, *[it[0] for it in comm])


ADAMW_WHOLE_ELEMS = 256 * 1024


def adamw(w, g, m, v, *, name):
    shape = w.shape
    parts = g.shape != shape
    nd = len(shape)
    if nd == 3 and shape[1] == 1 and w.size > ADAMW_WHOLE_ELEMS:
        assert not parts and shape[0] % 4 == 0
        grid = (4,)
        spec = g_spec = pl.BlockSpec((shape[0] // 4, 1, shape[2]), lambda i: (i, 0, 0))
    else:
        if w.size <= ADAMW_WHOLE_ELEMS:
            grid, tr = (1,), shape[-2]
        else:
            assert all(s == 1 for s in shape[:-2]) and shape[-2] % 256 == 0
            grid, tr = (shape[-2] // 256,), 256
        blk = tuple(shape[:-2]) + (tr, shape[-1])
        spec = pl.BlockSpec(blk, lambda i: (0,) * (nd - 2) + (i, 0))
        g_spec = pl.BlockSpec((N_DEV,) + blk[1:], lambda i: (0,) * (nd - 2) + (i, 0)) if parts else spec

    def body(w_ref, g_ref, m_ref, v_ref, go_ref, d_ref, nm_ref, nv_ref):
        if parts:
            g = g_ref[0:1].astype(F32)
            for j in range(1, N_DEV):
                g = g + g_ref[j:j + 1].astype(F32)
        else:
            g = g_ref[...]
        nm = ADAM_B1 * m_ref[...] + (1.0 - ADAM_B1) * g
        nv = ADAM_B2 * v_ref[...] + (1.0 - ADAM_B2) * jnp.square(g)
        m_hat = nm / (1.0 - ADAM_B1 ** ADAM_STEP)
        v_hat = nv / (1.0 - ADAM_B2 ** ADAM_STEP)
        go_ref[...] = g
        d_ref[...] = -ADAM_LR * (m_hat / (jnp.sqrt(v_hat) + ADAM_EPS) + ADAM_WD * w_ref[...])
        nm_ref[...] = nm
        nv_ref[...] = nv

    shp = jax.ShapeDtypeStruct(shape, F32)
    return pl.pallas_call(
        body, name=name, grid=grid,
        out_shape=[shp] * 4, in_specs=[spec, g_spec, spec, spec], out_specs=[spec] * 4,
        compiler_params=_params(("parallel",)),
    )(w, g, m, v)


def _pad_rows(a, rows):
    return jnp.pad(a, ((0, rows - a.shape[0]), (0, 0)))


WIN_W = 1408
N_IN = IN_WIDTH // N_DEV
_A6 = OFF_DT - 6 * N_IN
_C6 = 7 * N_IN - OFF_GA


def _win_offset(me):
    return jnp.where(me == 7, 124, 4 * me)


def _w_in_window(shard, me):
    rows = shard.shape[0]
    z = lambda n: jnp.zeros((rows, n), shard.dtype)
    a = lax.dynamic_update_slice(z(WIN_W), shard, (0, _win_offset(me)))
    b = jnp.concatenate([z(24), shard[:, :_A6], shard[:, _A6 + 32:], z(4), shard[:, _A6:_A6 + 32], z(96)], axis=1)
    return jnp.where(me == 6, b, a)


def _w_in_from_window(window, me):
    a = lax.dynamic_slice(window, (0, _win_offset(me)), (window.shape[0], N_IN))
    b = jnp.concatenate([window[:, 24:24 + _A6], window[:, 1280:1312], window[:, 24 + _A6:24 + _A6 + _C6]], axis=1)
    return jnp.where(me == 6, b, a)


def _w_all_from_windows(g):
    def merge_first(p, t):
        return jnp.concatenate([p[:, :128] + t, p[:, 128:]], axis=1)

    parts = [g[0][:, :1280]]
    for j in range(1, 6):
        parts.append(merge_first(g[j][:, :1280], g[j - 1][:, 1280:]))
    p6 = merge_first(g[6][:, :1280], g[5][:, 1280:])
    parts.append(jnp.concatenate([p6[:, :1152], p6[:, 1152:] + g[7][:, :128]], axis=1))
    parts.append(g[7][:, 128:])
    parts.append(g[6][:, 1280:])
    return jnp.concatenate(parts, axis=1)


def _windows_of_w_all(gw):
    wins = [gw[:, 1280 * j:1280 * j + WIN_W] for j in range(6)]
    wins.append(jnp.concatenate([gw[:, 7680:8960], gw[:, PROJ_W:]], axis=1))
    wins.append(gw[:, 8832:PROJ_W])
    return jnp.stack(wins)


def kernel(x, c, w_mod, b_mod, w_in, gm_norm_w, gm_ws, gm_bs, conv_w, conv_b, dt_bias, a_log, d_skip, ssm_norm_w, w_branch_gm, w_branch_ssm, w_out, w_ff1, w_ff2, final_norm_w, loss_target, m_w_mod, m_b_mod, m_w_in, m_gm_norm_w, m_gm_ws, m_gm_bs, m_conv_w, m_conv_b, m_dt_bias, m_a_log, m_d_skip, m_ssm_norm_w, m_w_branch_gm, m_w_branch_ssm, m_w_out, m_w_ff1, m_w_ff2, m_final_norm_w, v_w_mod, v_b_mod, v_w_in, v_gm_norm_w, v_gm_ws, v_gm_bs, v_conv_w, v_conv_b, v_dt_bias, v_a_log, v_d_skip, v_ssm_norm_w, v_w_branch_gm, v_w_branch_ssm, v_w_out, v_w_ff1, v_w_ff2, v_final_norm_w):
    T = x.shape[1]
    me = 4 * lax.axis_index("x") + 2 * lax.axis_index("y") + lax.axis_index("c")
    x2 = x[0]
    tgt = loss_target[0]
    n_in = IN_WIDTH // N_DEV
    n_mod = N_MOD * D // N_DEV
    n_cv = CONV_DIM // N_DEV

    c_all, conv_w_full = exchange(
        [(c.reshape(8, 128), _whole, (N_DEV, 8, 128), _slot),
         (conv_w[0], _whole, (N_DEV, CONV_K, n_cv), _slot)], name="gather_c_convw")
    c_all = c_all.reshape(N_DEV, D)
    conv_w_full = conv_w_full.transpose(1, 0, 2).reshape(CONV_K, CONV_DIM)

    win = _w_in_window(w_in[0].astype(BF16), me)
    gwin = gather_blocks_two_level(win, name="gather_w_in")
    late_weights = [
        (w_branch_gm[0].astype(BF16), _whole, (D, D), _rows(D // N_DEV)),
        (w_branch_ssm[0].astype(BF16), _whole, (SSM_INNER, D), _rows(SSM_INNER // N_DEV)),
        (w_out[0].astype(BF16), _whole, (D, D), _rows(D // N_DEV)),
        (w_ff1[0].astype(BF16), _whole, (D, D_FF), _cols(D_FF // N_DEV)),
        (w_ff2[0].astype(BF16), _whole, (D_FF, D), _rows(D_FF // N_DEV))]
    w_all = _w_all_from_windows(gwin)
    w_dt = w_all[:, PROJ_W:]

    c_pad = _pad_rows(c_all, 128)
    b_mine = lax.dynamic_slice(b_mod, (0, me * n_mod), (1, n_mod))

    def mod_fn(cp, w, b):
        ca = cp * jax.nn.sigmoid(cp)
        return (jnp.dot(ca, w, precision=HIGHEST, preferred_element_type=F32) + b,)

    (mod_part,) = whole_call(mod_fn, [c_pad, w_mod[0], b_mine], [((128, n_mod), F32)], name="mod_fwd")
    gmod = gather_blocks(mod_part[:N_DEV], name="gather_mod")
    mod = lax.dynamic_index_in_dim(gmod, me, axis=1, keepdims=False).reshape(N_MOD, D)
    sh1, sc1, gt1, sh2, sc2, gt2 = [mod[i:i + 1] for i in range(N_MOD)]

    (h,) = rowwise_call(fwd_body(fn_modulate), [x2], [sc1, sh1], [(D, BF16)], [], tm=256, name="modulate1")
    proj = matmul(h, w_all, "nn", BF16, name="mm_proj", n=PROJ_W)
    dtg = matmul(h, w_dt, "nn", F32, name="mm_dt")
    ws = gm_ws[0]
    bs3 = gm_bs[0].reshape(GM_GROUPS, Q, 1)
    sgu_rows = [(proj, D, P_U), (proj, D, P_V)]
    (ya,) = rowwise_call(fwd_body(fn_sgu), sgu_rows, [gm_norm_w, ws, bs3], [(D, BF16)], [],
                         tm=256, name="sgu_fwd")
    xact = conv_fwd(proj, conv_w_full, conv_b)
    dtb4 = jnp.pad(dt_bias, ((0, 0), (0, 96)))
    alog4 = jnp.pad(a_log, ((0, 0), (0, 96)))
    dsk4 = d_skip.reshape(SSM_GROUPS, SSM_HPG, 1, 1)
    yb, states, w_gm_f, w_ssm_f, w_out_f, w_ff1_f, w_ff2_f = ssd_fwd(
        xact, proj, dtg, dtb4, alog4, dsk4, ssm_norm_w, late_weights)
    pa = matmul(ya, w_gm_f, "nn", F32, name="mm_branch_gm")
    gate_rows = [(proj, D, P_GA), (proj, D, P_GB)]
    mixed, pb = rowwise_call(
        lambda r, fl: (fn_mix(*r) + (r[3],), ()), gate_rows + [pa], [], [(D, BF16), (D, F32)], [],
        tm=FUSED_TM, name="branch_ssm_mix", mm=(yb, w_ssm_f, "nn", 3, None), tk=SSM_INNER)
    x1, h2, o = rowwise_call(
        lambda r, fl: (fn_res_modulate(*r, *fl) + (r[1],), ()), [x2], [gt1, sc2, sh2],
        [(D, F32), (D, BF16), (D, F32)], [], tm=FUSED_TM, name="out_res_modulate2",
        mm=(mixed, w_out_f, "nn", 1, None))
    f = matmul(h2, w_ff1_f, "nn", BF16, name="mm_ff1")

    dx1, dgf, loss_v, dgt2, dfnw = rowwise_call(
        final_body, [x1, tgt], [gt2, final_norm_w.reshape(1, D)], [(D, F32), (D, BF16)],
        [(1, 128), (1, D), (1, D)], tm=FUSED_TM, name="ff2_loss_bwd", mm=(f, w_ff2_f, "nn", 1, relu2_tile),
        tk=D_FF)
    df = matmul(dgf, w_ff2_f, "nt", BF16, name="mm_ff2_dgrad", epi=relu2_grad_tile, epi_ins=(f,))
    gw_ff2 = matmul(f, dgf, "tn", BF16, name="mm_ff2_wgrad", tk=WGRAD_TK, a_pro=relu2_tile)
    gw_ff1 = matmul(h2, df, "tn", BF16, name="mm_ff1_wgrad", tk=WGRAD_TK)

    def res_mod_bwd(r, fl):
        xv, ov, dx1v, dh2v = r
        _, vjp = jax.vjp(fn_res_modulate, xv, ov, *fl)
        dxv, dov, dg1, dsc, dsh = vjp((dx1v, dh2v))
        return (dxv, dov), (dg1, dsc, dsh)

    dxa, do, dgt1, dsc2, dsh2 = rowwise_call(
        res_mod_bwd, [x2, o, dx1], [gt1, sc2, sh2], [(D, F32), (D, BF16)],
        [(1, D), (1, D), (1, D)], tm=FUSED_TM, name="ff1_dgrad_res_modulate2_bwd",
        mm=(df, w_ff1_f, "nt", 3, None), tk=D_FF)
    gw_out = matmul(mixed, do, "tn", BF16, name="mm_out_wgrad", tk=WGRAD_TK)
    dproj = lax.empty((T, ALL_W), BF16)

    def mix_bwd(r, fl):
        dga, dgb, dpa, dpb = bwd_body(fn_mix, 4)(r, fl)[0]
        return (jnp.concatenate([dga, dgb], axis=1), dpa, dpb), ()

    dproj, dpa, dpb = rowwise_call(
        mix_bwd, gate_rows + [pa, pb], [], [(dproj, 2 * D, P_GA), (D, BF16), (D, BF16)], [],
        tm=FUSED_TM, name="out_dgrad_mix_bwd", mm=(do, w_out_f, "nt", 4, None))
    gw_gm = matmul(ya, dpa, "tn", BF16, name="mm_branch_gm_wgrad", tk=WGRAD_TK)
    dyb = matmul(dpb, w_ssm_f, "nt", BF16, name="mm_branch_ssm_dgrad")
    gw_ssm = matmul(yb, dpb, "tn", BF16, name="mm_branch_ssm_wgrad", tk=WGRAD_TK)

    def sgu_bwd(r, fl):
        (du, dv), acc = bwd_body(fn_sgu, 2)(r, fl)
        return (jnp.concatenate([du, dv], axis=1),), acc

    dproj, dgnw, dws, dbs = rowwise_call(
        sgu_bwd, sgu_rows, [gm_norm_w, ws, bs3], [(dproj, 2 * D, P_U)],
        [(1, D), (GM_GROUPS, Q, Q), (GM_GROUPS, Q, 1)], tm=FUSED_TM, name="branch_gm_dgrad_sgu_bwd",
        mm=(dpa, w_gm_f, "nt", 2, None))
    early_grads = [
        (gw_gm, _rows(D // N_DEV), (N_DEV, D // N_DEV, D), _slot),
        (gw_ssm, _rows(SSM_INNER // N_DEV), (N_DEV, SSM_INNER // N_DEV, D), _slot),
        (gw_out, _rows(D // N_DEV), (N_DEV, D // N_DEV, D), _slot),
        (gw_ff1, _cols(D_FF // N_DEV), (N_DEV, D, D_FF // N_DEV), _slot),
        (gw_ff2, _rows(D_FF // N_DEV), (N_DEV, D_FF // N_DEV, D), _slot),
        (_pack_rows([dgnw, dws, dbs], EARLY_ROWS), _whole, (N_DEV, sum(EARLY_ROWS), 128), _slot)]
    (dxs, dbm, dcm, dproj, ddt8, ddtb, dalog, ddsk, dsnw,
     r_gm, r_ssm, r_out, r_ff1, r_ff2, early_all) = ssd_bwd(
        xact, proj, dtg, dtb4, alog4, dsk4, ssm_norm_w, states, dyb, dproj, early_grads)
    dconv_w, dconv_b = [], []
    for nm, dact_part, col0 in (("xs", dxs, 0), ("b", dbm, SSM_INNER), ("c", dcm, SSM_INNER + SSM_GROUPS * 128)):
        dproj, dcw, dcb = conv_bwd(proj, dact_part, col0, conv_w_full, conv_b, dproj, name="conv_bwd_" + nm)
        dconv_w.append(dcw)
        dconv_b.append(dcb)
    dconv_w = jnp.concatenate(dconv_w, axis=1)
    dconv_b = jnp.concatenate(dconv_b, axis=1)
    (dproj,) = rowwise_call(
        lambda r, fl: ((functools.reduce(jnp.add, r),), ()),
        [(ddt8, 128, 128 * g) for g in range(SSM_GROUPS)], [], [(dproj, 128, PROJ_W)], [],
        tm=1024, name="ddt_into_dproj")
    gw_all = matmul(h, dproj, "tn", BF16, name="mm_in_wgrad", tn=1152, tk=WGRAD_TK)
    mid_pack = _pack_rows([dconv_w, dconv_b, jnp.sum(ddtb, axis=0), jnp.sum(dalog, axis=0), ddsk, dsnw, dfnw,
                           jnp.concatenate([dgt1, dsh2, dsc2, dgt2], axis=0)], MID_ROWS)
    dh, r_in, mid_all = matmul(
        dproj, w_all, "nt", F32, name="mm_in_dgrad", tk=3456,
        comm=[(_windows_of_w_all(gw_all), _slot, (N_DEV, D, WIN_W), _slot),
              (mid_pack, _whole, (N_DEV, sum(MID_ROWS), 128), _slot)])
    grad_x, dsc1, dsh1 = rowwise_call(grad_x_body, [x2, dh, dxa], [sc1, sh1], [(D, F32)],
                                      [(1, D), (1, D)], tm=256, name="modulate1_bwd")

    g_w_in = _w_in_from_window(sum_devices(r_in, tr=256, name="sum_w_in_grads"), me).reshape(1, D, n_in)

    late_all = gather_blocks(_pack_rows([dsh1, dsc1], LATE_ROWS), name="gather_dmod1")
    s_early = _unpack_rows(sum_devices(early_all, tr=early_all.shape[1], name="sum_small_early"), EARLY_ROWS)
    s_mid = _unpack_rows(sum_devices(mid_all, tr=mid_all.shape[1], name="sum_small_mid"), MID_ROWS)
    s_late = _unpack_rows(sum_devices(late_all, tr=late_all.shape[1], name="sum_small_late"), LATE_ROWS)
    g_gm_norm_w = s_early[0][:D].reshape(1, D)
    g_gm_ws = s_early[1].reshape(GM_GROUPS * Q, Q)
    g_gm_bs = s_early[2][:GM_GROUPS * Q].reshape(GM_GROUPS, Q)
    g_conv_w_full = s_mid[0].reshape(CONV_K, CONV_DIM)
    g_conv_w = lax.dynamic_slice(g_conv_w_full, (0, me * n_cv), (CONV_K, n_cv))
    g_conv_b = s_mid[1].reshape(1, CONV_DIM)
    g_dt_bias = s_mid[2][:32].reshape(1, 32)
    g_a_log = s_mid[3][:32].reshape(1, 32)
    g_d_skip = s_mid[4][:32].reshape(1, 32)
    g_ssm_norm_w = s_mid[5].reshape(1, SSM_INNER)
    g_final_norm_w = s_mid[6][:D].reshape(1, D)
    g_b_mod = jnp.concatenate([s_late[0][:D], s_late[1][:D], s_mid[7]]).reshape(1, N_MOD * D)

    dmod_all = jnp.concatenate(
        [late_all.reshape(N_DEV, -1)[:, :2 * D],
         mid_all[:, sum(MID_ROWS[:7]):].reshape(N_DEV, 4 * D)], axis=1)
    dmod_mine = _pad_rows(lax.dynamic_slice(dmod_all, (0, me * n_mod), (N_DEV, n_mod)), 128)

    def wmod_grad_fn(cp, dm):
        ca = cp * jax.nn.sigmoid(cp)
        return (lax.dot_general(ca, dm, (((0,), (0,)), ((), ())), precision=HIGHEST,
                                preferred_element_type=F32),)

    (g_w_mod,) = whole_call(wmod_grad_fn, [c_pad, dmod_mine], [((D, n_mod), F32)], name="w_mod_grad")

    upd = {}

    def step(name, w, g, m, v, parts=False):
        upd[name] = adamw(w, g if parts else g.reshape(w.shape), m, v, name="adamw_" + name)

    step("w_mod", w_mod, g_w_mod, m_w_mod, v_w_mod)
    step("b_mod", b_mod, g_b_mod, m_b_mod, v_b_mod)
    col_major = lambda a: jnp.transpose(a, (2, 0, 1))
    upd["w_in"] = tuple(jnp.transpose(o, (1, 2, 0)) for o in adamw(
        col_major(w_in), col_major(g_w_in), col_major(m_w_in), col_major(v_w_in), name="adamw_w_in"))
    step("gm_norm_w", gm_norm_w, g_gm_norm_w, m_gm_norm_w, v_gm_norm_w)
    step("gm_ws", gm_ws, g_gm_ws, m_gm_ws, v_gm_ws)
    step("gm_bs", gm_bs, g_gm_bs, m_gm_bs, v_gm_bs)
    step("conv_w", conv_w, g_conv_w, m_conv_w, v_conv_w)
    step("conv_b", conv_b, g_conv_b, m_conv_b, v_conv_b)
    step("dt_bias", dt_bias, g_dt_bias, m_dt_bias, v_dt_bias)
    step("a_log", a_log, g_a_log, m_a_log, v_a_log)
    step("d_skip", d_skip, g_d_skip, m_d_skip, v_d_skip)
    step("ssm_norm_w", ssm_norm_w, g_ssm_norm_w, m_ssm_norm_w, v_ssm_norm_w)
    step("w_branch_gm", w_branch_gm, r_gm, m_w_branch_gm, v_w_branch_gm, parts=True)
    step("w_branch_ssm", w_branch_ssm, r_ssm, m_w_branch_ssm, v_w_branch_ssm, parts=True)
    step("w_out", w_out, r_out, m_w_out, v_w_out, parts=True)
    step("w_ff1", w_ff1, r_ff1, m_w_ff1, v_w_ff1, parts=True)
    step("w_ff2", w_ff2, r_ff2, m_w_ff2, v_w_ff2, parts=True)
    step("final_norm_w", final_norm_w.reshape(1, D), g_final_norm_w, m_final_norm_w.reshape(1, D),
         v_final_norm_w.reshape(1, D))
    upd["final_norm_w"] = tuple(a.reshape(D) for a in upd["final_norm_w"])

    loss = lax.psum(loss_v[0, 0], ("x", "y", "c"))
    order = ["w_mod", "b_mod", "w_in", "gm_norm_w", "gm_ws", "gm_bs", "conv_w", "conv_b", "dt_bias", "a_log",
             "d_skip", "ssm_norm_w", "w_branch_gm", "w_branch_ssm", "w_out", "w_ff1", "w_ff2", "final_norm_w"]
    return (loss, grad_x.reshape(1, T, D),
            *[upd[n][0] for n in order], *[upd[n][1] for n in order],
            *[upd[n][2] for n in order], *[upd[n][3] for n in order])
```

```python
import functools

import jax
import jax.numpy as jnp
from jax import lax
from jax.experimental import pallas as pl
from jax.experimental.pallas import tpu as pltpu

F32 = jnp.float32
BF16 = jnp.bfloat16
MESH = pl.DeviceIdType.MESH
HIGHEST = lax.Precision.HIGHEST

N_DEV = 8
D = 1024
Q = 128
GM_GROUPS = 8
SSM_INNER = 2048
SSM_GROUPS = 8
SSM_HPG = 4
SSM_P = 64
SSM_GW = SSM_HPG * SSM_P
CONV_DIM = 4096
CONV_K = 4
D_FF = 4096
N_MOD = 6
EPS = 1e-6
IN_WIDTH = 10272
OFF_DT = 8192
OFF_GA = 8224
PROJ_W = 10240
ALL_W = 10368
P_U, P_V, P_Z, P_XBC, P_GA, P_GB = 0, 1024, 2048, 4096, 8192, 9216

ADAM_LR = 0.001
ADAM_B1 = 0.9
ADAM_B2 = 0.999
ADAM_EPS = 1e-08
ADAM_WD = 0.01
ADAM_STEP = 10

VMEM_LIMIT_BYTES = 48 * 1024 * 1024
K1_TM = 2048
FUSED_TM = 256
WGRAD_TK = 2048
EARLY_ROWS = (8, 1024, 8)
MID_ROWS = (128, 32, 8, 8, 8, 16, 8, 32)
LATE_ROWS = (8, 8)


def _pack_rows(arrs, rows):
    def rows128(a, r):
        a = a.reshape(-1)
        return jnp.pad(a, (0, r * 128 - a.shape[0])).reshape(r, 128)
    return jnp.concatenate([rows128(a, r) for a, r in zip(arrs, rows)], axis=0)


def _unpack_rows(s, rows):
    out, o = [], 0
    for r in rows:
        out.append(s[o:o + r].reshape(-1))
        o += r
    return out


def _params(sem=None):
    return pltpu.CompilerParams(dimension_semantics=sem, vmem_limit_bytes=VMEM_LIMIT_BYTES)


def _dg(a, b, ca, cb):
    return lax.dot_general(a.astype(BF16), b.astype(BF16), (((ca,), (cb,)), ((), ())),
                           preferred_element_type=F32)


@jax.custom_vjp
def dot_nn(a, b):
    return _dg(a, b, 1, 0)


@jax.custom_vjp
def dot_nt(a, b):
    return _dg(a, b, 1, 1)


@jax.custom_vjp
def dot_tn(a, b):
    return _dg(a, b, 0, 0)


def _like(ct, primal):
    return ct.astype(primal.dtype)


dot_nn.defvjp(lambda a, b: (dot_nn(a, b), (a, b)),
              lambda r, g: (_like(dot_nt(g, r[1]), r[0]), _like(dot_tn(r[0], g), r[1])))
dot_nt.defvjp(lambda a, b: (dot_nt(a, b), (a, b)),
              lambda r, g: (_like(dot_nn(g, r[1]), r[0]), _like(dot_tn(g, r[0]), r[1])))
dot_tn.defvjp(lambda a, b: (dot_tn(a, b), (a, b)),
              lambda r, g: (_like(dot_nt(r[1], g), r[0]), _like(dot_nn(r[0], g), r[1])))


def _rms(x):
    return x * lax.rsqrt(jnp.mean(x * x, axis=-1, keepdims=True) + EPS)


def _softplus(x):
    return jnp.maximum(x, 0.0) + jnp.log1p(jnp.exp(-jnp.abs(x)))


def _rows(n):
    return lambda ref, j: ref.at[pl.ds(pl.multiple_of(j * n, n), n)]


def _cols(n):
    return lambda ref, j: ref.at[:, pl.ds(pl.multiple_of(j * n, n), n)]


def _slot(ref, j):
    return ref.at[j]


def _whole(ref, j):
    return ref


def exchange(items, *, name):
    n = len(items)

    def body(*refs):
        exchange_in_body(items, refs[:n], refs[n:2 * n], refs[2 * n:], True, True)

    return pl.pallas_call(
        body, name=name,
        out_shape=exchange_out_shapes(items),
        in_specs=[pl.BlockSpec(memory_space=pl.ANY)] * n,
        out_specs=[pl.BlockSpec(memory_space=pl.ANY)] * n,
        scratch_shapes=exchange_semaphores(items),
    )(*[it[0] for it in items])


def exchange_out_shapes(items):
    return [jax.ShapeDtypeStruct(tuple(shape), src.dtype) for (src, _, shape, _) in items]


def exchange_semaphores(items):
    n = len(items)
    return [pltpu.SemaphoreType.DMA((n, N_DEV - 1)), pltpu.SemaphoreType.DMA((n, N_DEV - 1)),
            pltpu.SemaphoreType.DMA((n,))]


def _exchange_copies(items, src_refs, out_refs, sems):
    send_sems, recv_sems, local_sems = sems
    x = lax.axis_index("x")
    y = lax.axis_index("y")
    c = lax.axis_index("c")
    me = 4 * x + 2 * y + c
    local = [pltpu.make_async_copy(src_win(src_refs[i], me), dst_win(out_refs[i], me), local_sems.at[i])
             for i, (_, src_win, _, dst_win) in enumerate(items)]
    remote = []
    for i, (_, src_win, _, dst_win) in enumerate(items):
        for k in range(1, N_DEV):
            px = lax.rem(x + ((k >> 2) & 1), 2)
            py = lax.rem(y + ((k >> 1) & 1), 2)
            pc = lax.rem(c + (k & 1), 2)
            peer = 4 * px + 2 * py + pc
            remote.append(pltpu.make_async_remote_copy(
                src_ref=src_win(src_refs[i], peer), dst_ref=dst_win(out_refs[i], me),
                send_sem=send_sems.at[i, k - 1], recv_sem=recv_sems.at[i, k - 1],
                device_id=(px, py, pc), device_id_type=MESH))
    return local, remote


def _when(cond, fn):
    if cond is True:
        fn()
    else:
        pl.when(cond)(fn)


def exchange_start(items, src_refs, out_refs, sems, cond):
    def start():
        local, remote = _exchange_copies(items, src_refs, out_refs, sems)
        for cp in local + remote:
            cp.start()
    _when(cond, start)


def exchange_finish(items, src_refs, out_refs, sems, cond):
    def finish():
        local, remote = _exchange_copies(items, src_refs, out_refs, sems)
        for cp in remote:
            cp.wait_send()
        for cp in remote:
            cp.wait_recv()
        for cp in local:
            cp.wait()
    _when(cond, finish)


def exchange_in_body(items, src_refs, out_refs, sems, first, last):
    exchange_start(items, src_refs, out_refs, sems, first)
    exchange_finish(items, src_refs, out_refs, sems, last)


def gather_blocks(src, *, name):
    return exchange([(src, _whole, (N_DEV,) + src.shape, _slot)], name=name)[0]


def gather_blocks_two_level(src, *, name):
    def body(src_ref, out_ref, send_sems, recv_sems, local_sem):
        x = lax.axis_index("x")
        y = lax.axis_index("y")
        c = lax.axis_index("c")
        me, sibling = (x, y, c), (x, y, 1 - c)
        chips = [(1 - x, y), (x, 1 - y), (1 - x, 1 - y)]

        def slot(px, py, pc):
            return out_ref.at[4 * px + 2 * py + pc]

        def copy(k, block, to, src=None):
            return pltpu.make_async_remote_copy(
                src_ref=slot(*block) if src is None else src, dst_ref=slot(*block),
                send_sem=send_sems.at[k], recv_sem=recv_sems.at[k], device_id=to, device_id_type=MESH)

        mine = pltpu.make_async_copy(src_ref, slot(*me), local_sem)
        mine.start()
        first = [copy(0, me, sibling, src=src_ref)]
        first += [copy(1 + j, me, (*chip, c), src=src_ref) for j, chip in enumerate(chips)]
        for cp in first:
            cp.start()
        passed = [copy(4 + j, (*chip, c), sibling) for j, chip in enumerate(chips)]
        for j, chip in enumerate(chips):
            copy(1 + j, (*chip, c), me).wait_recv()
            passed[j].start()
        copy(0, sibling, me).wait_recv()
        for j, chip in enumerate(chips):
            copy(4 + j, (*chip, 1 - c), me).wait_recv()
        for cp in first + passed:
            cp.wait_send()
        mine.wait()

    return pl.pallas_call(
        body, name=name,
        out_shape=jax.ShapeDtypeStruct((N_DEV,) + src.shape, src.dtype),
        in_specs=[pl.BlockSpec(memory_space=pl.ANY)],
        out_specs=pl.BlockSpec(memory_space=pl.ANY),
        scratch_shapes=[pltpu.SemaphoreType.DMA((N_DEV - 1,)), pltpu.SemaphoreType.DMA((N_DEV - 1,)),
                        pltpu.SemaphoreType.DMA(())],
    )(src)


def sum_devices(g, *, tr, name):
    _, R, C = g.shape

    def body(g_ref, o_ref):
        acc = g_ref[0].astype(F32)
        for j in range(1, N_DEV):
            acc = acc + g_ref[j].astype(F32)
        o_ref[...] = acc

    return pl.pallas_call(
        body, name=name, grid=(R // tr,),
        out_shape=jax.ShapeDtypeStruct((R, C), F32),
        in_specs=[pl.BlockSpec((N_DEV, tr, C), lambda i: (0, i, 0))],
        out_specs=pl.BlockSpec((tr, C), lambda i: (i, 0)),
        compiler_params=_params(("parallel",)),
    )(g)


def matmul(a, b, mode, out_dtype, *, name, tm=1024, tn=1024, tk=1024, n=None, comm=None,
           a_pro=None, epi=None, epi_ins=()):
    if mode == "nn":
        (M, K), (K2, N) = a.shape, b.shape
    elif mode == "nt":
        (M, K), (N, K2) = a.shape, b.shape
    else:
        (K, M), (K2, N) = a.shape, b.shape
    assert K == K2
    N = N if n is None else n
    tm, tn, tk = min(tm, M), min(tn, N), min(tk, K)
    assert M % tm == 0 and N % tn == 0 and K % tk == 0, (name, M, N, K, tm, tn, tk)
    nk = K // tk
    if mode == "tn":
        a_spec = pl.BlockSpec((tk, tm), lambda i, j, k: (k, i))
    else:
        a_spec = pl.BlockSpec((tm, tk), lambda i, j, k: (i, k))
    if mode == "nt":
        b_spec = pl.BlockSpec((tn, tk), lambda i, j, k: (j, k))
    else:
        b_spec = pl.BlockSpec((tk, tn), lambda i, j, k: (k, j))
    dims = {"nn": (1, 0), "nt": (1, 1), "tn": (0, 0)}[mode]
    items = list(comm) if comm else []
    nx = len(items)
    ne = len(epi_ins)
    gm, gn = M // tm, N // tn
    any_spec = pl.BlockSpec(memory_space=pl.ANY)
    o_spec = pl.BlockSpec((tm, tn), lambda i, j, k: (i, j))

    def body(*refs):
        a_ref, b_ref, e_refs = refs[0], refs[1], refs[2:2 + ne]
        refs = refs[2 + ne:]
        src_refs, o_ref, out_refs = refs[:nx], refs[nx], refs[1 + nx:1 + 2 * nx]
        acc_ref, sems = refs[1 + 2 * nx], refs[2 + 2 * nx:]
        i, j, k = pl.program_id(0), pl.program_id(1), pl.program_id(2)
        if items:
            exchange_start(items, src_refs, out_refs, sems, (i == 0) & (j == 0) & (k == 0))
        a_tile = a_ref[...] if a_pro is None else a_pro(a_ref[...])
        part = lax.dot_general(a_tile, b_ref[...], (((dims[0],), (dims[1],)), ((), ())),
                               preferred_element_type=F32)

        def finish(acc):
            if epi is not None:
                acc = epi(acc, *[e[...] for e in e_refs])
            o_ref[...] = acc.astype(o_ref.dtype)

        if nk == 1:
            finish(part)
        else:
            @pl.when(k == 0)
            def _():
                acc_ref[...] = part

            @pl.when((k > 0) & (k < nk - 1))
            def _():
                acc_ref[...] += part

            @pl.when(k == nk - 1)
            def _():
                finish(acc_ref[...] + part)

        if items:
            exchange_finish(items, src_refs, out_refs, sems, (i == gm - 1) & (j == gn - 1) & (k == nk - 1))

    res = pl.pallas_call(
        body, name=name, grid=(gm, gn, nk),
        out_shape=[jax.ShapeDtypeStruct((M, N), out_dtype)] + exchange_out_shapes(items),
        in_specs=[a_spec, b_spec] + [o_spec] * ne + [any_spec] * nx,
        out_specs=[o_spec] + [any_spec] * nx,
        scratch_shapes=[pltpu.VMEM((tm, tn) if nk > 1 else (8, 128), F32)]
        + (exchange_semaphores(items) if items else []),
        compiler_params=_params(("arbitrary",) * 3 if items else ("parallel", "parallel", "arbitrary")),
    )(a, b, *epi_ins, *[it[0] for it in items])
    return res if items else res[0]


def rowwise_call(body_fn, rows, fulls, row_outs, acc_outs, *, tm, name, mm=None, tk=1024):
    rows = [r if isinstance(r, tuple) else (r, r.shape[1], 0) for r in rows]
    T = rows[0][0].shape[0]
    tm = min(tm, T)
    assert T % tm == 0
    n_r, n_f, n_ro = len(rows), len(fulls), len(row_outs)
    into = [(k, ro) for k, ro in enumerate(row_outs) if len(ro) == 3]
    n_b = len(into)
    n_mm, nk = 0, 1
    if mm is not None:
        a, b, mode, pos, a_pro = mm
        n_mm = 2
        K = a.shape[1]
        N = b.shape[1] if mode == "nn" else b.shape[0]
        tk = min(tk, K)
        assert K % tk == 0 and a.shape[0] == T
        nk = K // tk
        b_contract = 0 if mode == "nn" else 1

    def row_body(refs, product):
        r_refs = refs[:n_r]
        f_refs = refs[n_r:n_r + n_f]
        refs = refs[n_r + n_f + n_b:]
        ro_refs = refs[:n_ro]
        ao_refs = refs[n_ro:n_ro + len(acc_outs)]
        r_vals = [r[...].astype(F32) for r in r_refs]
        if product is not None:
            r_vals.insert(pos, product)
        f_vals = [f[...].astype(F32) for f in f_refs]
        ro, ao = body_fn(r_vals, f_vals)
        for ref, v in zip(ro_refs, ro):
            ref[...] = v.astype(ref.dtype)
        if ao_refs:
            @pl.when(pl.program_id(0) == 0)
            def _():
                for ref in ao_refs:
                    ref[...] = jnp.zeros(ref.shape, F32)
            for ref, v in zip(ao_refs, ao):
                ref[...] += v.reshape(ref.shape)

    def body(*refs):
        if mm is None:
            return row_body(refs, None)
        a_ref, b_ref, rest, acc_ref = refs[0], refs[1], refs[2:-1], refs[-1]
        k = pl.program_id(1)
        a_tile = a_ref[...] if a_pro is None else a_pro(a_ref[...])
        part = lax.dot_general(a_tile, b_ref[...], (((1,), (b_contract,)), ((), ())),
                               preferred_element_type=F32)
        if nk == 1:
            return row_body(rest, part)

        @pl.when(k == 0)
        def _():
            acc_ref[...] = part

        @pl.when((k > 0) & (k < nk - 1))
        def _():
            acc_ref[...] += part

        @pl.when(k == nk - 1)
        def _():
            row_body(rest, acc_ref[...] + part)

    def full_spec(shape):
        nd = len(shape)
        return pl.BlockSpec(tuple(shape), lambda i, *_: (0,) * nd)

    def row_spec(w, off):
        return pl.BlockSpec((tm, w), functools.partial(lambda i, *_, o: (i, o), o=off // w))

    in_specs = []
    if mm is not None:
        in_specs.append(pl.BlockSpec((tm, tk), lambda i, k: (i, k)))
        in_specs.append(pl.BlockSpec((tk, N), lambda i, k: (k, 0)) if mode == "nn" else
                        pl.BlockSpec((N, tk), lambda i, k: (0, k)))
    in_specs += [row_spec(w, off) for (_, w, off) in rows]
    in_specs += [full_spec(f.shape) for f in fulls]
    in_specs += [pl.BlockSpec(memory_space=pl.ANY)] * n_b
    out_specs, out_shape = [], []
    for ro in row_outs:
        if len(ro) == 3:
            buf, w, off = ro
            out_specs.append(row_spec(w, off))
            out_shape.append(jax.ShapeDtypeStruct(buf.shape, buf.dtype))
        else:
            w, dt = ro
            out_specs.append(row_spec(w, 0))
            out_shape.append(jax.ShapeDtypeStruct((T, w), dt))
    out_specs += [full_spec(s) for s in acc_outs]
    out_shape += [jax.ShapeDtypeStruct(tuple(s), F32) for s in acc_outs]
    aliases = {n_mm + n_r + n_f + b: k for b, (k, _) in enumerate(into)}
    return pl.pallas_call(
        body, name=name, grid=(T // tm,) if mm is None else (T // tm, nk),
        out_shape=out_shape, in_specs=in_specs, out_specs=out_specs,
        scratch_shapes=[] if mm is None else [pltpu.VMEM((tm, N) if nk > 1 else (8, 128), F32)],
        input_output_aliases=aliases,
        compiler_params=_params(("arbitrary",) if mm is None else ("arbitrary", "arbitrary")),
    )(*([] if mm is None else [a, b]), *[r[0] for r in rows], *fulls, *[ro[0] for _, ro in into])


def fwd_body(fn):
    return lambda r, f: (fn(*r, *f), ())


def bwd_body(fn, n_rows):
    def body(r, f):
        ins, cots = r[:n_rows], r[n_rows:]
        _, vjp = jax.vjp(fn, *ins, *f)
        g = vjp(tuple(cots))
        return g[:n_rows], g[n_rows:]
    return body


def whole_call(fn, ins, outs, *, name):
    n_in = len(ins)

    def body(*refs):
        res = fn(*[r[...] for r in refs[:n_in]])
        for ref, v in zip(refs[n_in:], res):
            ref[...] = v.astype(ref.dtype)

    return pl.pallas_call(
        body, name=name,
        out_shape=[jax.ShapeDtypeStruct(tuple(s), dt) for (s, dt) in outs],
        compiler_params=_params(),
    )(*ins)


def fn_modulate(x, sc, sh):
    return (_rms(x) * (1.0 + sc) + sh,)


def fn_sgu(u, v, nw, ws, bs):
    ug = jax.nn.gelu(u)
    vn = _rms(jax.nn.gelu(v)) * nw
    ri = lax.broadcasted_iota(jnp.int32, (Q, Q), 0)
    ci = lax.broadcasted_iota(jnp.int32, (Q, Q), 1)
    causal = ri >= ci
    chunks = []
    for n in range(u.shape[0] // Q):
        vc = vn[n * Q:(n + 1) * Q]
        cols = [dot_nn(jnp.where(causal, ws[g], 0.0), vc[:, g * Q:(g + 1) * Q]) + bs[g]
                for g in range(GM_GROUPS)]
        chunks.append(jnp.concatenate(cols, axis=1))
    sv = chunks[0] if len(chunks) == 1 else jnp.concatenate(chunks, axis=0)
    return (ug * sv,)


def fn_mix(ga, gb, pa, pb):
    return (jax.nn.sigmoid(ga) * pa + jax.nn.sigmoid(gb) * pb,)


def fn_res_modulate(x, o, g1, sc2, sh2):
    x1 = x + g1 * o
    return x1, _rms(x1) * (1.0 + sc2) + sh2


def relu2_tile(f):
    return jnp.square(jnp.maximum(f.astype(F32), 0.0)).astype(BF16)


def relu2_grad_tile(dact, f):
    return dact * (2.0 * jnp.maximum(f.astype(F32), 0.0))


def final_body(r, f):
    x1, gf, tgt = r
    g2, fnw = f

    def loss_fn(x1, gf, g2, fnw):
        y = _rms(x1 + g2 * gf) * fnw
        row = 0.5 * jnp.mean(jnp.square(y - tgt), axis=-1, keepdims=True)
        return jnp.sum(row, axis=0, keepdims=True)

    l, vjp = jax.vjp(loss_fn, x1, gf, g2, fnw)
    dx1, dgf, dg2, dfnw = vjp(jnp.ones((1, 1), F32))
    return (dx1, dgf), (jnp.broadcast_to(l, (1, 128)), dg2, dfnw)


def grad_x_body(r, f):
    x, dh, dxa = r
    _, vjp = jax.vjp(fn_modulate, x, *f)
    dx, dsc, dsh = vjp((dh,))
    return (dx + dxa,), (dsc, dsh)


CONV_CW = 128
CONV_PAD = 8
CONV_ROWS = 128


def _conv_pre(xp, w_ref, b_ref, r0, R):
    acc = b_ref[...] + w_ref[0:1, :] * xp[r0 + CONV_PAD - 3:r0 + CONV_PAD - 3 + R, :]
    for k in range(1, CONV_K):
        s = r0 + CONV_PAD - 3 + k
        acc = acc + w_ref[k:k + 1, :] * xp[s:s + R, :]
    return acc


def conv_fwd(proj, conv_w, conv_b):
    T = proj.shape[0]
    R = min(CONV_ROWS, T)

    def body(x_ref, w_ref, b_ref, o_ref, xp):
        xp[0:CONV_PAD, :] = jnp.zeros((CONV_PAD, CONV_CW), F32)
        xp[CONV_PAD:CONV_PAD + T, :] = x_ref[...].astype(F32)
        for r0 in range(0, T, R):
            pre = _conv_pre(xp, w_ref, b_ref, r0, R)
            o_ref[r0:r0 + R, :] = (pre * jax.nn.sigmoid(pre)).astype(o_ref.dtype)

    return pl.pallas_call(
        body, name="conv_fwd", grid=(CONV_DIM // CONV_CW,),
        out_shape=jax.ShapeDtypeStruct((T, CONV_DIM), BF16),
        in_specs=[pl.BlockSpec((T, CONV_CW), lambda j: (0, P_XBC // CONV_CW + j)),
                  pl.BlockSpec((CONV_K, CONV_CW), lambda j: (0, j)),
                  pl.BlockSpec((1, CONV_CW), lambda j: (0, j))],
        out_specs=pl.BlockSpec((T, CONV_CW), lambda j: (0, j)),
        scratch_shapes=[pltpu.VMEM((T + CONV_PAD, CONV_CW), F32)],
        compiler_params=_params(("parallel",)),
    )(proj, conv_w, conv_b)


def conv_bwd(proj, dact, col0, conv_w, conv_b, dproj, *, name):
    T = proj.shape[0]
    R = min(CONV_ROWS, T)
    nb = dact.shape[1] // CONV_CW
    c0 = col0 // CONV_CW
    x0 = (P_XBC + col0) // CONV_CW

    def body(x_ref, d_ref, w_ref, b_ref, _, dx_ref, dw_ref, db_ref, xp, dp):
        xp[0:CONV_PAD, :] = jnp.zeros((CONV_PAD, CONV_CW), F32)
        xp[CONV_PAD:CONV_PAD + T, :] = x_ref[...].astype(F32)
        dp[T:T + CONV_PAD, :] = jnp.zeros((CONV_PAD, CONV_CW), F32)
        dws = [jnp.zeros((1, CONV_CW), F32) for _ in range(CONV_K)]
        db = jnp.zeros((1, CONV_CW), F32)
        for r0 in range(0, T, R):
            pre = _conv_pre(xp, w_ref, b_ref, r0, R)
            s = jax.nn.sigmoid(pre)
            dpre = d_ref[r0:r0 + R, :].astype(F32) * (s * (1.0 + pre * (1.0 - s)))
            dp[r0:r0 + R, :] = dpre
            db = db + jnp.sum(dpre, axis=0, keepdims=True)
            for k in range(CONV_K):
                st = r0 + CONV_PAD - 3 + k
                dws[k] = dws[k] + jnp.sum(dpre * xp[st:st + R, :], axis=0, keepdims=True)
        for r0 in range(0, T, R):
            acc = w_ref[0:1, :] * dp[r0 + 3:r0 + 3 + R, :]
            for k in range(1, CONV_K):
                acc = acc + w_ref[k:k + 1, :] * dp[r0 + 3 - k:r0 + 3 - k + R, :]
            dx_ref[r0:r0 + R, :] = acc.astype(dx_ref.dtype)
        for k in range(CONV_K):
            dw_ref[k:k + 1, :] = dws[k]
        db_ref[...] = db

    return pl.pallas_call(
        body, name=name, grid=(nb,),
        out_shape=[jax.ShapeDtypeStruct(dproj.shape, dproj.dtype),
                   jax.ShapeDtypeStruct((CONV_K, nb * CONV_CW), F32),
                   jax.ShapeDtypeStruct((1, nb * CONV_CW), F32)],
        in_specs=[pl.BlockSpec((T, CONV_CW), lambda j: (0, x0 + j)),
                  pl.BlockSpec((T, CONV_CW), lambda j: (0, j)),
                  pl.BlockSpec((CONV_K, CONV_CW), lambda j: (0, c0 + j)),
                  pl.BlockSpec((1, CONV_CW), lambda j: (0, c0 + j)),
                  pl.BlockSpec(memory_space=pl.ANY)],
        out_specs=[pl.BlockSpec((T, CONV_CW), lambda j: (0, x0 + j)),
                   pl.BlockSpec((CONV_K, CONV_CW), lambda j: (0, j)),
                   pl.BlockSpec((1, CONV_CW), lambda j: (0, j))],
        scratch_shapes=[pltpu.VMEM((T + CONV_PAD, CONV_CW), F32),
                        pltpu.VMEM((T + CONV_PAD, CONV_CW), F32)],
        input_output_aliases={4: 0},
        compiler_params=_params(("parallel",)),
    )(proj, dact, conv_w, conv_b, dproj)


def _split3(a):
    hi = a.astype(BF16)
    r = a - hi.astype(F32)
    mid = r.astype(BF16)
    return hi, mid, (r - mid.astype(F32)).astype(BF16)


def _dg3(a, m, ca, cm, a_first):
    dims = (((ca,), (cm,)), ((), ())) if a_first else (((cm,), (ca,)), ((), ()))
    out = None
    for p in _split3(a):
        t = lax.dot_general(p, m, dims, preferred_element_type=F32) if a_first else \
            lax.dot_general(m, p, dims, preferred_element_type=F32)
        out = t if out is None else out + t
    return out


@jax.custom_vjp
def exact_right(a, m):
    return _dg3(a, m, 1, 0, True)


@jax.custom_vjp
def exact_left(m, a):
    return _dg3(a, m, 0, 1, False)


def _expand_bwd(m, g):
    hi = g.astype(BF16)
    lo = (g - hi.astype(F32)).astype(BF16)
    out = lax.dot_general(jnp.concatenate([hi, lo], axis=1), jnp.concatenate([m, m], axis=1),
                          (((1,), (1,)), ((), ())), preferred_element_type=F32)
    return out, jnp.zeros_like(m)


exact_right.defvjp(lambda a, m: (exact_right(a, m), m), _expand_bwd)
exact_left.defvjp(lambda m, a: (exact_left(m, a), m),
                  lambda m, g: (jnp.zeros_like(m), _dg3(g, m, 0, 0, False)))


def ssd_step(lane0, state, x, z, dtr, Bm, Cm, dtb, alog, dsk, nw):
    def iota(shape, dim):
        return lax.broadcasted_iota(jnp.int32, shape, dim)

    def one_hot(mask):
        return mask.astype(F32).astype(BF16)

    causal = iota((Q, Q), 0) >= iota((Q, Q), 1)
    eye = iota((Q, Q), 0) == iota((Q, Q), 1)
    lane = iota((1, 128), 1)
    colh = lax.shift_right_logical(iota((1, SSM_GW), 1), 6)
    to_cols = one_hot(iota((128, SSM_GW), 0) == lane0 + colh)

    dt_all = _softplus(dtr + dtb)
    a_all = dt_all * (-jnp.exp(alog))
    cum_all = exact_left(one_hot(causal), a_all)
    both = exact_right(jnp.concatenate([dt_all, cum_all], axis=0), to_cols)
    dt_f, cum_f = both[:Q], both[Q:]
    last_f = jnp.sum(jnp.where(iota((Q, 1), 0) == Q - 1, cum_f, 0.0), axis=0, keepdims=True)
    dsk_f = jnp.zeros((1, SSM_GW), F32)
    for h in range(SSM_HPG):
        dsk_f = jnp.where(colh == h, dsk[h], dsk_f)

    xdt = x * dt_f
    cb = dot_nt(Cm, Bm)
    ms = []
    for h in range(SSM_HPG):
        ch = jnp.sum(jnp.where(lane == lane0 + h, cum_all, 0.0), axis=1, keepdims=True)
        ch_t = jnp.sum(jnp.where(eye, ch, 0.0), axis=0, keepdims=True)
        ms.append(cb * jnp.exp(jnp.where(causal, ch - ch_t, -1e30)))
    first_half = lane < SSM_P
    blocks = []
    for b in range(SSM_HPG // 2):
        xb = xdt[:, b * 128:(b + 1) * 128]
        rhs = jnp.concatenate([jnp.where(first_half, xb, 0.0), jnp.where(first_half, 0.0, xb)], axis=0)
        blocks.append(dot_nn(jnp.concatenate(ms[2 * b:2 * b + 2], axis=1), rhs))
    y = jnp.concatenate(blocks, axis=1)
    y = y + dot_nn(Cm, state) * jnp.exp(cum_f) + x * dsk_f
    new_state = state * jnp.exp(last_f) + dot_tn(Bm, xdt * jnp.exp(last_f - cum_f))
    gated = y * (z * jax.nn.sigmoid(z))
    return new_state, _rms(gated) * nw


SSD_GPS = 4
_XW = SSD_GPS * SSM_GW
_BW = SSD_GPS * 128


def _ssd_in_specs(rev, nc):
    def n_of(n):
        return nc - 1 - n if rev else n
    return [
        pl.BlockSpec((Q, _XW), lambda g, n: (n_of(n), g)),
        pl.BlockSpec((Q, _BW), lambda g, n: (n_of(n), SSM_INNER // _BW + g)),
        pl.BlockSpec((Q, _BW), lambda g, n: (n_of(n), (SSM_INNER + SSM_GROUPS * 128) // _BW + g)),
        pl.BlockSpec((Q, _XW), lambda g, n: (n_of(n), P_Z // _XW + g)),
        pl.BlockSpec((Q, 128), lambda g, n: (n_of(n), 0)),
        pl.BlockSpec((1, 128), lambda g, n: (0, 0)),
        pl.BlockSpec((1, 128), lambda g, n: (0, 0)),
        pl.BlockSpec((SSD_GPS, SSM_HPG, 1, 1), lambda g, n: (g, 0, 0, 0)),
        pl.BlockSpec((1, _XW), lambda g, n: (0, g)),
    ]


def _ssd_group_inputs(gi, x_ref, b_ref, c_ref, z_ref, dt_ref, dtb_ref, al_ref, dk_ref, nw_ref):
    xs = slice(gi * SSM_GW, (gi + 1) * SSM_GW)
    bs = slice(gi * 128, (gi + 1) * 128)
    return (x_ref[:, xs].astype(F32), z_ref[:, xs].astype(F32), dt_ref[...],
            b_ref[:, bs], c_ref[:, bs],
            dtb_ref[...], al_ref[...], dk_ref[gi], nw_ref[:, xs])


def ssd_fwd(xact, proj, dtg, dtb, alog, dsk, nw, comm):
    T = xact.shape[0]
    nc = T // Q
    nx = len(comm)
    ng = SSM_GROUPS // SSD_GPS
    any_spec = pl.BlockSpec(memory_space=pl.ANY)

    def body(*refs):
        in_refs, src_refs = refs[:9], refs[9:9 + nx]
        yb_ref, st_ref = refs[9 + nx:11 + nx]
        out_refs, state, sems = refs[11 + nx:11 + 2 * nx], refs[11 + 2 * nx], refs[12 + 2 * nx:]
        g, n = pl.program_id(0), pl.program_id(1)
        exchange_start(comm, src_refs, out_refs, sems, (g == 0) & (n == 0))

        @pl.when(n == 0)
        def _():
            state[...] = jnp.zeros(state.shape, F32)

        for gi in range(SSD_GPS):
            lane0 = SSM_HPG * (SSD_GPS * pl.program_id(0) + gi)
            s = state[gi]
            st_ref[gi, 0] = s
            new_s, yb = ssd_step(lane0, s, *_ssd_group_inputs(gi, *in_refs))
            state[gi] = new_s
            yb_ref[:, gi * SSM_GW:(gi + 1) * SSM_GW] = yb.astype(yb_ref.dtype)

        exchange_finish(comm, src_refs, out_refs, sems, (g == ng - 1) & (n == nc - 1))

    return pl.pallas_call(
        body, name="ssd_fwd", grid=(ng, nc),
        out_shape=[jax.ShapeDtypeStruct((T, SSM_INNER), BF16),
                   jax.ShapeDtypeStruct((SSM_GROUPS, nc, 128, SSM_GW), F32)] + exchange_out_shapes(comm),
        in_specs=_ssd_in_specs(False, nc) + [any_spec] * nx,
        out_specs=[pl.BlockSpec((Q, _XW), lambda g, n: (n, g)),
                   pl.BlockSpec((SSD_GPS, 1, 128, SSM_GW), lambda g, n: (g, n, 0, 0))] + [any_spec] * nx,
        scratch_shapes=[pltpu.VMEM((SSD_GPS, 128, SSM_GW), F32)] + exchange_semaphores(comm),
        compiler_params=_params(("arbitrary", "arbitrary")),
    )(xact, xact, xact, proj, dtg, dtb, alog, dsk, nw, *[it[0] for it in comm])


def ssd_bwd(xact, proj, dtg, dtb, alog, dsk, nw, states, dyb, dproj, comm):
    T = xact.shape[0]
    nc = T // Q

    nx = len(comm)
    ng = SSM_GROUPS // SSD_GPS
    any_spec = pl.BlockSpec(memory_space=pl.ANY)

    def body(*refs):
        in_refs, (st_ref, dy_ref, _) = refs[:9], refs[9:12]
        src_refs, refs = refs[12:12 + nx], refs[12 + nx:]
        dx_ref, db_ref, dc_ref, dz_ref, ddt_ref, ddtb_ref, dal_ref, ddk_ref, dnw_ref = refs[:9]
        out_refs, dstate, sems = refs[9:9 + nx], refs[9 + nx], refs[10 + nx:]
        exchange_start(comm, src_refs, out_refs, sems, (pl.program_id(0) == 0) & (pl.program_id(1) == 0))

        @pl.when(pl.program_id(1) == 0)
        def _():
            dstate[...] = jnp.zeros(dstate.shape, F32)
            ddtb_ref[...] = jnp.zeros(ddtb_ref.shape, F32)
            dal_ref[...] = jnp.zeros(dal_ref.shape, F32)
            ddk_ref[...] = jnp.zeros(ddk_ref.shape, F32)
            dnw_ref[...] = jnp.zeros(dnw_ref.shape, F32)

        for gi in range(SSD_GPS):
            xs = slice(gi * SSM_GW, (gi + 1) * SSM_GW)
            bs = slice(gi * 128, (gi + 1) * 128)
            lane0 = SSM_HPG * (SSD_GPS * pl.program_id(0) + gi)
            ins = (st_ref[gi, 0],) + _ssd_group_inputs(gi, *in_refs)
            _, vjp = jax.vjp(functools.partial(ssd_step, lane0), *ins)
            ds, dx, dz, ddt, dbm, dcm, ddtb, dal, ddk, dnw = vjp((dstate[gi], dy_ref[:, xs].astype(F32)))
            dstate[gi] = ds
            dx_ref[:, xs] = dx.astype(dx_ref.dtype)
            db_ref[:, bs] = dbm.astype(db_ref.dtype)
            dc_ref[:, bs] = dcm.astype(dc_ref.dtype)
            dz_ref[:, xs] = dz.astype(dz_ref.dtype)
            ddt_ref[:, bs] = ddt
            ddtb_ref[gi] += ddtb
            dal_ref[gi] += dal
            ddk_ref[gi] += ddk
            dnw_ref[:, xs] += dnw

        exchange_finish(comm, src_refs, out_refs, sems,
                        (pl.program_id(0) == ng - 1) & (pl.program_id(1) == nc - 1))

    rev = lambda n: nc - 1 - n
    row_shape = jax.ShapeDtypeStruct((SSM_GROUPS, 1, 128), F32)
    row_spec = pl.BlockSpec((SSD_GPS, 1, 128), lambda g, n: (g, 0, 0))
    return pl.pallas_call(
        body, name="ssd_bwd", grid=(ng, nc),
        out_shape=[jax.ShapeDtypeStruct((T, SSM_INNER), BF16),
                   jax.ShapeDtypeStruct((T, SSM_GROUPS * 128), BF16),
                   jax.ShapeDtypeStruct((T, SSM_GROUPS * 128), BF16),
                   jax.ShapeDtypeStruct(dproj.shape, dproj.dtype),
                   jax.ShapeDtypeStruct((T, SSM_GROUPS * 128), F32),
                   row_shape, row_shape,
                   jax.ShapeDtypeStruct((SSM_GROUPS, SSM_HPG, 1, 1), F32),
                   jax.ShapeDtypeStruct((1, SSM_INNER), F32)] + exchange_out_shapes(comm),
        in_specs=_ssd_in_specs(True, nc) + [
            pl.BlockSpec((SSD_GPS, 1, 128, SSM_GW), lambda g, n: (g, rev(n), 0, 0)),
            pl.BlockSpec((Q, _XW), lambda g, n: (rev(n), g)),
            any_spec] + [any_spec] * nx,
        out_specs=[pl.BlockSpec((Q, _XW), lambda g, n: (rev(n), g)),
                   pl.BlockSpec((Q, _BW), lambda g, n: (rev(n), g)),
                   pl.BlockSpec((Q, _BW), lambda g, n: (rev(n), g)),
                   pl.BlockSpec((Q, _XW), lambda g, n: (rev(n), P_Z // _XW + g)),
                   pl.BlockSpec((Q, _BW), lambda g, n: (rev(n), g)),
                   row_spec, row_spec,
                   pl.BlockSpec((SSD_GPS, SSM_HPG, 1, 1), lambda g, n: (g, 0, 0, 0)),
                   pl.BlockSpec((1, _XW), lambda g, n: (0, g))] + [any_spec] * nx,
        scratch_shapes=[pltpu.VMEM((SSD_GPS, 128, SSM_GW), F32)] + exchange_semaphores(comm),
        input_output_aliases={11: 3},
        compiler_params=_params(("arbitrary", "arbitrary")),
    )(xact, xact, xact, proj, dtg, dtb, alog, dsk, nw, states, dyb, dproj, *[it[0] for it in comm])


ADAMW_WHOLE_ELEMS = 256 * 1024


def adamw(w, g, m, v, *, name):
    shape = w.shape
    parts = g.shape != shape
    nd = len(shape)
    if nd == 3 and shape[1] == 1 and w.size > ADAMW_WHOLE_ELEMS:
        assert not parts and shape[0] % 4 == 0
        grid = (4,)
        spec = g_spec = pl.BlockSpec((shape[0] // 4, 1, shape[2]), lambda i: (i, 0, 0))
    else:
        if w.size <= ADAMW_WHOLE_ELEMS:
            grid, tr = (1,), shape[-2]
        else:
            assert all(s == 1 for s in shape[:-2]) and shape[-2] % 256 == 0
            grid, tr = (shape[-2] // 256,), 256
        blk = tuple(shape[:-2]) + (tr, shape[-1])
        spec = pl.BlockSpec(blk, lambda i: (0,) * (nd - 2) + (i, 0))
        g_spec = pl.BlockSpec((N_DEV,) + blk[1:], lambda i: (0,) * (nd - 2) + (i, 0)) if parts else spec

    def body(w_ref, g_ref, m_ref, v_ref, go_ref, d_ref, nm_ref, nv_ref):
        if parts:
            g = g_ref[0:1].astype(F32)
            for j in range(1, N_DEV):
                g = g + g_ref[j:j + 1].astype(F32)
        else:
            g = g_ref[...]
        nm = ADAM_B1 * m_ref[...] + (1.0 - ADAM_B1) * g
        nv = ADAM_B2 * v_ref[...] + (1.0 - ADAM_B2) * jnp.square(g)
        m_hat = nm / (1.0 - ADAM_B1 ** ADAM_STEP)
        v_hat = nv / (1.0 - ADAM_B2 ** ADAM_STEP)
        go_ref[...] = g
        d_ref[...] = -ADAM_LR * (m_hat / (jnp.sqrt(v_hat) + ADAM_EPS) + ADAM_WD * w_ref[...])
        nm_ref[...] = nm
        nv_ref[...] = nv

    shp = jax.ShapeDtypeStruct(shape, F32)
    return pl.pallas_call(
        body, name=name, grid=grid,
        out_shape=[shp] * 4, in_specs=[spec, g_spec, spec, spec], out_specs=[spec] * 4,
        compiler_params=_params(("parallel",)),
    )(w, g, m, v)


def _pad_rows(a, rows):
    return jnp.pad(a, ((0, rows - a.shape[0]), (0, 0)))


WIN_W = 1408
N_IN = IN_WIDTH // N_DEV
_A6 = OFF_DT - 6 * N_IN
_C6 = 7 * N_IN - OFF_GA


def _win_offset(me):
    return jnp.where(me == 7, 124, 4 * me)


def _w_in_window(shard, me):
    rows = shard.shape[0]
    z = lambda n: jnp.zeros((rows, n), shard.dtype)
    a = lax.dynamic_update_slice(z(WIN_W), shard, (0, _win_offset(me)))
    b = jnp.concatenate([z(24), shard[:, :_A6], shard[:, _A6 + 32:], z(4), shard[:, _A6:_A6 + 32], z(96)], axis=1)
    return jnp.where(me == 6, b, a)


def _w_in_from_window(window, me):
    a = lax.dynamic_slice(window, (0, _win_offset(me)), (window.shape[0], N_IN))
    b = jnp.concatenate([window[:, 24:24 + _A6], window[:, 1280:1312], window[:, 24 + _A6:24 + _A6 + _C6]], axis=1)
    return jnp.where(me == 6, b, a)


def _w_all_from_windows(g):
    def merge_first(p, t):
        return jnp.concatenate([p[:, :128] + t, p[:, 128:]], axis=1)

    parts = [g[0][:, :1280]]
    for j in range(1, 6):
        parts.append(merge_first(g[j][:, :1280], g[j - 1][:, 1280:]))
    p6 = merge_first(g[6][:, :1280], g[5][:, 1280:])
    parts.append(jnp.concatenate([p6[:, :1152], p6[:, 1152:] + g[7][:, :128]], axis=1))
    parts.append(g[7][:, 128:])
    parts.append(g[6][:, 1280:])
    return jnp.concatenate(parts, axis=1)


def _windows_of_w_all(gw):
    wins = [gw[:, 1280 * j:1280 * j + WIN_W] for j in range(6)]
    wins.append(jnp.concatenate([gw[:, 7680:8960], gw[:, PROJ_W:]], axis=1))
    wins.append(gw[:, 8832:PROJ_W])
    return jnp.stack(wins)


def kernel(x, c, w_mod, b_mod, w_in, gm_norm_w, gm_ws, gm_bs, conv_w, conv_b, dt_bias, a_log, d_skip, ssm_norm_w, w_branch_gm, w_branch_ssm, w_out, w_ff1, w_ff2, final_norm_w, loss_target, m_w_mod, m_b_mod, m_w_in, m_gm_norm_w, m_gm_ws, m_gm_bs, m_conv_w, m_conv_b, m_dt_bias, m_a_log, m_d_skip, m_ssm_norm_w, m_w_branch_gm, m_w_branch_ssm, m_w_out, m_w_ff1, m_w_ff2, m_final_norm_w, v_w_mod, v_b_mod, v_w_in, v_gm_norm_w, v_gm_ws, v_gm_bs, v_conv_w, v_conv_b, v_dt_bias, v_a_log, v_d_skip, v_ssm_norm_w, v_w_branch_gm, v_w_branch_ssm, v_w_out, v_w_ff1, v_w_ff2, v_final_norm_w):
    T = x.shape[1]
    me = 4 * lax.axis_index("x") + 2 * lax.axis_index("y") + lax.axis_index("c")
    x2 = x[0]
    tgt = loss_target[0]
    n_in = IN_WIDTH // N_DEV
    n_mod = N_MOD * D // N_DEV
    n_cv = CONV_DIM // N_DEV

    c_all, conv_w_full = exchange(
        [(c.reshape(8, 128), _whole, (N_DEV, 8, 128), _slot),
         (conv_w[0], _whole, (N_DEV, CONV_K, n_cv), _slot)], name="gather_c_convw")
    c_all = c_all.reshape(N_DEV, D)
    conv_w_full = conv_w_full.transpose(1, 0, 2).reshape(CONV_K, CONV_DIM)

    win = _w_in_window(w_in[0].astype(BF16), me)
    gwin = gather_blocks_two_level(win, name="gather_w_in")
    late_weights = [
        (w_branch_gm[0].astype(BF16), _whole, (D, D), _rows(D // N_DEV)),
        (w_branch_ssm[0].astype(BF16), _whole, (SSM_INNER, D), _rows(SSM_INNER // N_DEV)),
        (w_out[0].astype(BF16), _whole, (D, D), _rows(D // N_DEV)),
        (w_ff1[0].astype(BF16), _whole, (D, D_FF), _cols(D_FF // N_DEV)),
        (w_ff2[0].astype(BF16), _whole, (D_FF, D), _rows(D_FF // N_DEV))]
    w_all = _w_all_from_windows(gwin)
    w_dt = w_all[:, PROJ_W:]

    c_pad = _pad_rows(c_all, 128)
    b_mine = lax.dynamic_slice(b_mod, (0, me * n_mod), (1, n_mod))

    def mod_fn(cp, w, b):
        ca = cp * jax.nn.sigmoid(cp)
        return (jnp.dot(ca, w, precision=HIGHEST, preferred_element_type=F32) + b,)

    (mod_part,) = whole_call(mod_fn, [c_pad, w_mod[0], b_mine], [((128, n_mod), F32)], name="mod_fwd")
    gmod = gather_blocks(mod_part[:N_DEV], name="gather_mod")
    mod = lax.dynamic_index_in_dim(gmod, me, axis=1, keepdims=False).reshape(N_MOD, D)
    sh1, sc1, gt1, sh2, sc2, gt2 = [mod[i:i + 1] for i in range(N_MOD)]

    (h,) = rowwise_call(fwd_body(fn_modulate), [x2], [sc1, sh1], [(D, BF16)], [], tm=256, name="modulate1")
    proj = matmul(h, w_all, "nn", BF16, name="mm_proj", n=PROJ_W, tm=K1_TM)
    dtg = matmul(h, w_dt, "nn", F32, name="mm_dt")
    ws = gm_ws[0]
    bs3 = gm_bs[0].reshape(GM_GROUPS, Q, 1)
    sgu_rows = [(proj, D, P_U), (proj, D, P_V)]
    (ya,) = rowwise_call(fwd_body(fn_sgu), sgu_rows, [gm_norm_w, ws, bs3], [(D, BF16)], [],
                         tm=256, name="sgu_fwd")
    xact = conv_fwd(proj, conv_w_full, conv_b)
    dtb4 = jnp.pad(dt_bias, ((0, 0), (0, 96)))
    alog4 = jnp.pad(a_log, ((0, 0), (0, 96)))
    dsk4 = d_skip.reshape(SSM_GROUPS, SSM_HPG, 1, 1)
    yb, states, w_gm_f, w_ssm_f, w_out_f, w_ff1_f, w_ff2_f = ssd_fwd(
        xact, proj, dtg, dtb4, alog4, dsk4, ssm_norm_w, late_weights)
    pa = matmul(ya, w_gm_f, "nn", F32, name="mm_branch_gm")
    gate_rows = [(proj, D, P_GA), (proj, D, P_GB)]
    mixed, pb = rowwise_call(
        lambda r, fl: (fn_mix(*r) + (r[3],), ()), gate_rows + [pa], [], [(D, BF16), (D, F32)], [],
        tm=FUSED_TM, name="branch_ssm_mix", mm=(yb, w_ssm_f, "nn", 3, None), tk=SSM_INNER)
    x1, h2, o = rowwise_call(
        lambda r, fl: (fn_res_modulate(*r, *fl) + (r[1],), ()), [x2], [gt1, sc2, sh2],
        [(D, F32), (D, BF16), (D, F32)], [], tm=FUSED_TM, name="out_res_modulate2",
        mm=(mixed, w_out_f, "nn", 1, None))
    f = matmul(h2, w_ff1_f, "nn", BF16, name="mm_ff1", tm=K1_TM)

    dx1, dgf, loss_v, dgt2, dfnw = rowwise_call(
        final_body, [x1, tgt], [gt2, final_norm_w.reshape(1, D)], [(D, F32), (D, BF16)],
        [(1, 128), (1, D), (1, D)], tm=FUSED_TM, name="ff2_loss_bwd", mm=(f, w_ff2_f, "nn", 1, relu2_tile),
        tk=D_FF)
    df = matmul(dgf, w_ff2_f, "nt", BF16, name="mm_ff2_dgrad", epi=relu2_grad_tile, epi_ins=(f,), tm=K1_TM)
    gw_ff2 = matmul(f, dgf, "tn", BF16, name="mm_ff2_wgrad", tk=WGRAD_TK, a_pro=relu2_tile)
    gw_ff1 = matmul(h2, df, "tn", BF16, name="mm_ff1_wgrad", tk=WGRAD_TK)

    def res_mod_bwd(r, fl):
        xv, ov, dx1v, dh2v = r
        _, vjp = jax.vjp(fn_res_modulate, xv, ov, *fl)
        dxv, dov, dg1, dsc, dsh = vjp((dx1v, dh2v))
        return (dxv, dov), (dg1, dsc, dsh)

    dxa, do, dgt1, dsc2, dsh2 = rowwise_call(
        res_mod_bwd, [x2, o, dx1], [gt1, sc2, sh2], [(D, F32), (D, BF16)],
        [(1, D), (1, D), (1, D)], tm=FUSED_TM, name="ff1_dgrad_res_modulate2_bwd",
        mm=(df, w_ff1_f, "nt", 3, None), tk=D_FF)
    gw_out = matmul(mixed, do, "tn", BF16, name="mm_out_wgrad", tk=WGRAD_TK)
    dproj = lax.empty((T, ALL_W), BF16)

    def mix_bwd(r, fl):
        dga, dgb, dpa, dpb = bwd_body(fn_mix, 4)(r, fl)[0]
        return (jnp.concatenate([dga, dgb], axis=1), dpa, dpb), ()

    dproj, dpa, dpb = rowwise_call(
        mix_bwd, gate_rows + [pa, pb], [], [(dproj, 2 * D, P_GA), (D, BF16), (D, BF16)], [],
        tm=FUSED_TM, name="out_dgrad_mix_bwd", mm=(do, w_out_f, "nt", 4, None))
    gw_gm = matmul(ya, dpa, "tn", BF16, name="mm_branch_gm_wgrad", tk=WGRAD_TK)
    dyb = matmul(dpb, w_ssm_f, "nt", BF16, name="mm_branch_ssm_dgrad")
    gw_ssm = matmul(yb, dpb, "tn", BF16, name="mm_branch_ssm_wgrad", tk=WGRAD_TK)

    def sgu_bwd(r, fl):
        (du, dv), acc = bwd_body(fn_sgu, 2)(r, fl)
        return (jnp.concatenate([du, dv], axis=1),), acc

    dproj, dgnw, dws, dbs = rowwise_call(
        sgu_bwd, sgu_rows, [gm_norm_w, ws, bs3], [(dproj, 2 * D, P_U)],
        [(1, D), (GM_GROUPS, Q, Q), (GM_GROUPS, Q, 1)], tm=FUSED_TM, name="branch_gm_dgrad_sgu_bwd",
        mm=(dpa, w_gm_f, "nt", 2, None))
    early_grads = [
        (gw_gm, _rows(D // N_DEV), (N_DEV, D // N_DEV, D), _slot),
        (gw_ssm, _rows(SSM_INNER // N_DEV), (N_DEV, SSM_INNER // N_DEV, D), _slot),
        (gw_out, _rows(D // N_DEV), (N_DEV, D // N_DEV, D), _slot),
        (gw_ff1, _cols(D_FF // N_DEV), (N_DEV, D, D_FF // N_DEV), _slot),
        (gw_ff2, _rows(D_FF // N_DEV), (N_DEV, D_FF // N_DEV, D), _slot),
        (_pack_rows([dgnw, dws, dbs], EARLY_ROWS), _whole, (N_DEV, sum(EARLY_ROWS), 128), _slot)]
    (dxs, dbm, dcm, dproj, ddt8, ddtb, dalog, ddsk, dsnw,
     r_gm, r_ssm, r_out, r_ff1, r_ff2, early_all) = ssd_bwd(
        xact, proj, dtg, dtb4, alog4, dsk4, ssm_norm_w, states, dyb, dproj, early_grads)
    dconv_w, dconv_b = [], []
    for nm, dact_part, col0 in (("xs", dxs, 0), ("b", dbm, SSM_INNER), ("c", dcm, SSM_INNER + SSM_GROUPS * 128)):
        dproj, dcw, dcb = conv_bwd(proj, dact_part, col0, conv_w_full, conv_b, dproj, name="conv_bwd_" + nm)
        dconv_w.append(dcw)
        dconv_b.append(dcb)
    dconv_w = jnp.concatenate(dconv_w, axis=1)
    dconv_b = jnp.concatenate(dconv_b, axis=1)
    (dproj,) = rowwise_call(
        lambda r, fl: ((functools.reduce(jnp.add, r),), ()),
        [(ddt8, 128, 128 * g) for g in range(SSM_GROUPS)], [], [(dproj, 128, PROJ_W)], [],
        tm=1024, name="ddt_into_dproj")
    gw_all = matmul(h, dproj, "tn", BF16, name="mm_in_wgrad", tn=1152, tk=WGRAD_TK)
    mid_pack = _pack_rows([dconv_w, dconv_b, jnp.sum(ddtb, axis=0), jnp.sum(dalog, axis=0), ddsk, dsnw, dfnw,
                           jnp.concatenate([dgt1, dsh2, dsc2, dgt2], axis=0)], MID_ROWS)
    dh, r_in, mid_all = matmul(
        dproj, w_all, "nt", F32, name="mm_in_dgrad", tk=3456,
        comm=[(_windows_of_w_all(gw_all), _slot, (N_DEV, D, WIN_W), _slot),
              (mid_pack, _whole, (N_DEV, sum(MID_ROWS), 128), _slot)])
    grad_x, dsc1, dsh1 = rowwise_call(grad_x_body, [x2, dh, dxa], [sc1, sh1], [(D, F32)],
                                      [(1, D), (1, D)], tm=256, name="modulate1_bwd")

    g_w_in = _w_in_from_window(sum_devices(r_in, tr=256, name="sum_w_in_grads"), me).reshape(1, D, n_in)

    late_all = gather_blocks(_pack_rows([dsh1, dsc1], LATE_ROWS), name="gather_dmod1")
    s_early = _unpack_rows(sum_devices(early_all, tr=early_all.shape[1], name="sum_small_early"), EARLY_ROWS)
    s_mid = _unpack_rows(sum_devices(mid_all, tr=mid_all.shape[1], name="sum_small_mid"), MID_ROWS)
    s_late = _unpack_rows(sum_devices(late_all, tr=late_all.shape[1], name="sum_small_late"), LATE_ROWS)
    g_gm_norm_w = s_early[0][:D].reshape(1, D)
    g_gm_ws = s_early[1].reshape(GM_GROUPS * Q, Q)
    g_gm_bs = s_early[2][:GM_GROUPS * Q].reshape(GM_GROUPS, Q)
    g_conv_w_full = s_mid[0].reshape(CONV_K, CONV_DIM)
    g_conv_w = lax.dynamic_slice(g_conv_w_full, (0, me * n_cv), (CONV_K, n_cv))
    g_conv_b = s_mid[1].reshape(1, CONV_DIM)
    g_dt_bias = s_mid[2][:32].reshape(1, 32)
    g_a_log = s_mid[3][:32].reshape(1, 32)
    g_d_skip = s_mid[4][:32].reshape(1, 32)
    g_ssm_norm_w = s_mid[5].reshape(1, SSM_INNER)
    g_final_norm_w = s_mid[6][:D].reshape(1, D)
    g_b_mod = jnp.concatenate([s_late[0][:D], s_late[1][:D], s_mid[7]]).reshape(1, N_MOD * D)

    dmod_all = jnp.concatenate(
        [late_all.reshape(N_DEV, -1)[:, :2 * D],
         mid_all[:, sum(MID_ROWS[:7]):].reshape(N_DEV, 4 * D)], axis=1)
    dmod_mine = _pad_rows(lax.dynamic_slice(dmod_all, (0, me * n_mod), (N_DEV, n_mod)), 128)

    def wmod_grad_fn(cp, dm):
        ca = cp * jax.nn.sigmoid(cp)
        return (lax.dot_general(ca, dm, (((0,), (0,)), ((), ())), precision=HIGHEST,
                                preferred_element_type=F32),)

    (g_w_mod,) = whole_call(wmod_grad_fn, [c_pad, dmod_mine], [((D, n_mod), F32)], name="w_mod_grad")

    upd = {}

    def step(name, w, g, m, v, parts=False):
        upd[name] = adamw(w, g if parts else g.reshape(w.shape), m, v, name="adamw_" + name)

    step("w_mod", w_mod, g_w_mod, m_w_mod, v_w_mod)
    step("b_mod", b_mod, g_b_mod, m_b_mod, v_b_mod)
    col_major = lambda a: jnp.transpose(a, (2, 0, 1))
    upd["w_in"] = tuple(jnp.transpose(o, (1, 2, 0)) for o in adamw(
        col_major(w_in), col_major(g_w_in), col_major(m_w_in), col_major(v_w_in), name="adamw_w_in"))
    step("gm_norm_w", gm_norm_w, g_gm_norm_w, m_gm_norm_w, v_gm_norm_w)
    step("gm_ws", gm_ws, g_gm_ws, m_gm_ws, v_gm_ws)
    step("gm_bs", gm_bs, g_gm_bs, m_gm_bs, v_gm_bs)
    step("conv_w", conv_w, g_conv_w, m_conv_w, v_conv_w)
    step("conv_b", conv_b, g_conv_b, m_conv_b, v_conv_b)
    step("dt_bias", dt_bias, g_dt_bias, m_dt_bias, v_dt_bias)
    step("a_log", a_log, g_a_log, m_a_log, v_a_log)
    step("d_skip", d_skip, g_d_skip, m_d_skip, v_d_skip)
    step("ssm_norm_w", ssm_norm_w, g_ssm_norm_w, m_ssm_norm_w, v_ssm_norm_w)
    step("w_branch_gm", w_branch_gm, r_gm, m_w_branch_gm, v_w_branch_gm, parts=True)
    step("w_branch_ssm", w_branch_ssm, r_ssm, m_w_branch_ssm, v_w_branch_ssm, parts=True)
    step("w_out", w_out, r_out, m_w_out, v_w_out, parts=True)
    step("w_ff1", w_ff1, r_ff1, m_w_ff1, v_w_ff1, parts=True)
    step("w_ff2", w_ff2, r_ff2, m_w_ff2, v_w_ff2, parts=True)
    step("final_norm_w", final_norm_w.reshape(1, D), g_final_norm_w, m_final_norm_w.reshape(1, D),
         v_final_norm_w.reshape(1, D))
    upd["final_norm_w"] = tuple(a.reshape(D) for a in upd["final_norm_w"])

    loss = lax.psum(loss_v[0, 0], ("x", "y", "c"))
    order = ["w_mod", "b_mod", "w_in", "gm_norm_w", "gm_ws", "gm_bs", "conv_w", "conv_b", "dt_bias", "a_log",
             "d_skip", "ssm_norm_w", "w_branch_gm", "w_branch_ssm", "w_out", "w_ff1", "w_ff2", "final_norm_w"]
    return (loss, grad_x.reshape(1, T, D),
            *[upd[n][0] for n in order], *[upd[n][1] for n in order],
            *[upd[n][2] for n in order], *[upd[n][3] for n in order])
```

```python
import functools

import jax
import jax.numpy as jnp
from jax import lax
from jax.experimental import pallas as pl
from jax.experimental.pallas import tpu as pltpu

F32 = jnp.float32
BF16 = jnp.bfloat16
MESH = pl.DeviceIdType.MESH
HIGHEST = lax.Precision.HIGHEST

N_DEV = 8
D = 1024
Q = 128
GM_GROUPS = 8
SSM_INNER = 2048
SSM_GROUPS = 8
SSM_HPG = 4
SSM_P = 64
SSM_GW = SSM_HPG * SSM_P
CONV_DIM = 4096
CONV_K = 4
D_FF = 4096
N_MOD = 6
EPS = 1e-6
IN_WIDTH = 10272
OFF_DT = 8192
OFF_GA = 8224
PROJ_W = 10240
ALL_W = 10368
P_U, P_V, P_Z, P_XBC, P_GA, P_GB = 0, 1024, 2048, 4096, 8192, 9216

ADAM_LR = 0.001
ADAM_B1 = 0.9
ADAM_B2 = 0.999
ADAM_EPS = 1e-08
ADAM_WD = 0.01
ADAM_STEP = 10

VMEM_LIMIT_BYTES = 48 * 1024 * 1024
K1_TM = 2048
FUSED_TM = 512
WGRAD_TK = 2048
EARLY_ROWS = (8, 1024, 8)
MID_ROWS = (128, 32, 8, 8, 8, 16, 8, 32)
LATE_ROWS = (8, 8)


def _pack_rows(arrs, rows):
    def rows128(a, r):
        a = a.reshape(-1)
        return jnp.pad(a, (0, r * 128 - a.shape[0])).reshape(r, 128)
    return jnp.concatenate([rows128(a, r) for a, r in zip(arrs, rows)], axis=0)


def _unpack_rows(s, rows):
    out, o = [], 0
    for r in rows:
        out.append(s[o:o + r].reshape(-1))
        o += r
    return out


def _params(sem=None):
    return pltpu.CompilerParams(dimension_semantics=sem, vmem_limit_bytes=VMEM_LIMIT_BYTES)


def _dg(a, b, ca, cb):
    return lax.dot_general(a.astype(BF16), b.astype(BF16), (((ca,), (cb,)), ((), ())),
                           preferred_element_type=F32)


@jax.custom_vjp
def dot_nn(a, b):
    return _dg(a, b, 1, 0)


@jax.custom_vjp
def dot_nt(a, b):
    return _dg(a, b, 1, 1)


@jax.custom_vjp
def dot_tn(a, b):
    return _dg(a, b, 0, 0)


def _like(ct, primal):
    return ct.astype(primal.dtype)


dot_nn.defvjp(lambda a, b: (dot_nn(a, b), (a, b)),
              lambda r, g: (_like(dot_nt(g, r[1]), r[0]), _like(dot_tn(r[0], g), r[1])))
dot_nt.defvjp(lambda a, b: (dot_nt(a, b), (a, b)),
              lambda r, g: (_like(dot_nn(g, r[1]), r[0]), _like(dot_tn(g, r[0]), r[1])))
dot_tn.defvjp(lambda a, b: (dot_tn(a, b), (a, b)),
              lambda r, g: (_like(dot_nt(r[1], g), r[0]), _like(dot_nn(r[0], g), r[1])))


def _rms(x):
    return x * lax.rsqrt(jnp.mean(x * x, axis=-1, keepdims=True) + EPS)


def _softplus(x):
    return jnp.maximum(x, 0.0) + jnp.log1p(jnp.exp(-jnp.abs(x)))


def _rows(n):
    return lambda ref, j: ref.at[pl.ds(pl.multiple_of(j * n, n), n)]


def _cols(n):
    return lambda ref, j: ref.at[:, pl.ds(pl.multiple_of(j * n, n), n)]


def _slot(ref, j):
    return ref.at[j]


def _whole(ref, j):
    return ref


def exchange(items, *, name):
    n = len(items)

    def body(*refs):
        exchange_in_body(items, refs[:n], refs[n:2 * n], refs[2 * n:], True, True)

    return pl.pallas_call(
        body, name=name,
        out_shape=exchange_out_shapes(items),
        in_specs=[pl.BlockSpec(memory_space=pl.ANY)] * n,
        out_specs=[pl.BlockSpec(memory_space=pl.ANY)] * n,
        scratch_shapes=exchange_semaphores(items),
    )(*[it[0] for it in items])


def exchange_out_shapes(items):
    return [jax.ShapeDtypeStruct(tuple(shape), src.dtype) for (src, _, shape, _) in items]


def exchange_semaphores(items):
    n = len(items)
    return [pltpu.SemaphoreType.DMA((n, N_DEV - 1)), pltpu.SemaphoreType.DMA((n, N_DEV - 1)),
            pltpu.SemaphoreType.DMA((n,))]


def _exchange_copies(items, src_refs, out_refs, sems):
    send_sems, recv_sems, local_sems = sems
    x = lax.axis_index("x")
    y = lax.axis_index("y")
    c = lax.axis_index("c")
    me = 4 * x + 2 * y + c
    local = [pltpu.make_async_copy(src_win(src_refs[i], me), dst_win(out_refs[i], me), local_sems.at[i])
             for i, (_, src_win, _, dst_win) in enumerate(items)]
    remote = []
    for i, (_, src_win, _, dst_win) in enumerate(items):
        for k in range(1, N_DEV):
            px = lax.rem(x + ((k >> 2) & 1), 2)
            py = lax.rem(y + ((k >> 1) & 1), 2)
            pc = lax.rem(c + (k & 1), 2)
            peer = 4 * px + 2 * py + pc
            remote.append(pltpu.make_async_remote_copy(
                src_ref=src_win(src_refs[i], peer), dst_ref=dst_win(out_refs[i], me),
                send_sem=send_sems.at[i, k - 1], recv_sem=recv_sems.at[i, k - 1],
                device_id=(px, py, pc), device_id_type=MESH))
    return local, remote


def _when(cond, fn):
    if cond is True:
        fn()
    else:
        pl.when(cond)(fn)


def exchange_start(items, src_refs, out_refs, sems, cond):
    def start():
        local, remote = _exchange_copies(items, src_refs, out_refs, sems)
        for cp in local + remote:
            cp.start()
    _when(cond, start)


def exchange_finish(items, src_refs, out_refs, sems, cond):
    def finish():
        local, remote = _exchange_copies(items, src_refs, out_refs, sems)
        for cp in remote:
            cp.wait_send()
        for cp in remote:
            cp.wait_recv()
        for cp in local:
            cp.wait()
    _when(cond, finish)


def exchange_in_body(items, src_refs, out_refs, sems, first, last):
    exchange_start(items, src_refs, out_refs, sems, first)
    exchange_finish(items, src_refs, out_refs, sems, last)


def gather_blocks(src, *, name):
    return exchange([(src, _whole, (N_DEV,) + src.shape, _slot)], name=name)[0]


def gather_blocks_two_level(src, *, name):
    def body(src_ref, out_ref, send_sems, recv_sems, local_sem):
        x = lax.axis_index("x")
        y = lax.axis_index("y")
        c = lax.axis_index("c")
        me, sibling = (x, y, c), (x, y, 1 - c)
        chips = [(1 - x, y), (x, 1 - y), (1 - x, 1 - y)]

        def slot(px, py, pc):
            return out_ref.at[4 * px + 2 * py + pc]

        def copy(k, block, to, src=None):
            return pltpu.make_async_remote_copy(
                src_ref=slot(*block) if src is None else src, dst_ref=slot(*block),
                send_sem=send_sems.at[k], recv_sem=recv_sems.at[k], device_id=to, device_id_type=MESH)

        mine = pltpu.make_async_copy(src_ref, slot(*me), local_sem)
        mine.start()
        first = [copy(0, me, sibling, src=src_ref)]
        first += [copy(1 + j, me, (*chip, c), src=src_ref) for j, chip in enumerate(chips)]
        for cp in first:
            cp.start()
        passed = [copy(4 + j, (*chip, c), sibling) for j, chip in enumerate(chips)]
        for j, chip in enumerate(chips):
            copy(1 + j, (*chip, c), me).wait_recv()
            passed[j].start()
        copy(0, sibling, me).wait_recv()
        for j, chip in enumerate(chips):
            copy(4 + j, (*chip, 1 - c), me).wait_recv()
        for cp in first + passed:
            cp.wait_send()
        mine.wait()

    return pl.pallas_call(
        body, name=name,
        out_shape=jax.ShapeDtypeStruct((N_DEV,) + src.shape, src.dtype),
        in_specs=[pl.BlockSpec(memory_space=pl.ANY)],
        out_specs=pl.BlockSpec(memory_space=pl.ANY),
        scratch_shapes=[pltpu.SemaphoreType.DMA((N_DEV - 1,)), pltpu.SemaphoreType.DMA((N_DEV - 1,)),
                        pltpu.SemaphoreType.DMA(())],
    )(src)


def sum_devices(g, *, tr, name):
    _, R, C = g.shape

    def body(g_ref, o_ref):
        acc = g_ref[0].astype(F32)
        for j in range(1, N_DEV):
            acc = acc + g_ref[j].astype(F32)
        o_ref[...] = acc

    return pl.pallas_call(
        body, name=name, grid=(R // tr,),
        out_shape=jax.ShapeDtypeStruct((R, C), F32),
        in_specs=[pl.BlockSpec((N_DEV, tr, C), lambda i: (0, i, 0))],
        out_specs=pl.BlockSpec((tr, C), lambda i: (i, 0)),
        compiler_params=_params(("parallel",)),
    )(g)


def matmul(a, b, mode, out_dtype, *, name, tm=1024, tn=1024, tk=1024, n=None, comm=None,
           a_pro=None, epi=None, epi_ins=()):
    if mode == "nn":
        (M, K), (K2, N) = a.shape, b.shape
    elif mode == "nt":
        (M, K), (N, K2) = a.shape, b.shape
    else:
        (K, M), (K2, N) = a.shape, b.shape
    assert K == K2
    N = N if n is None else n
    tm, tn, tk = min(tm, M), min(tn, N), min(tk, K)
    assert M % tm == 0 and N % tn == 0 and K % tk == 0, (name, M, N, K, tm, tn, tk)
    nk = K // tk
    if mode == "tn":
        a_spec = pl.BlockSpec((tk, tm), lambda i, j, k: (k, i))
    else:
        a_spec = pl.BlockSpec((tm, tk), lambda i, j, k: (i, k))
    if mode == "nt":
        b_spec = pl.BlockSpec((tn, tk), lambda i, j, k: (j, k))
    else:
        b_spec = pl.BlockSpec((tk, tn), lambda i, j, k: (k, j))
    dims = {"nn": (1, 0), "nt": (1, 1), "tn": (0, 0)}[mode]
    items = list(comm) if comm else []
    nx = len(items)
    ne = len(epi_ins)
    gm, gn = M // tm, N // tn
    any_spec = pl.BlockSpec(memory_space=pl.ANY)
    o_spec = pl.BlockSpec((tm, tn), lambda i, j, k: (i, j))

    def body(*refs):
        a_ref, b_ref, e_refs = refs[0], refs[1], refs[2:2 + ne]
        refs = refs[2 + ne:]
        src_refs, o_ref, out_refs = refs[:nx], refs[nx], refs[1 + nx:1 + 2 * nx]
        acc_ref, sems = refs[1 + 2 * nx], refs[2 + 2 * nx:]
        i, j, k = pl.program_id(0), pl.program_id(1), pl.program_id(2)
        if items:
            exchange_start(items, src_refs, out_refs, sems, (i == 0) & (j == 0) & (k == 0))
        a_tile = a_ref[...] if a_pro is None else a_pro(a_ref[...])
        part = lax.dot_general(a_tile, b_ref[...], (((dims[0],), (dims[1],)), ((), ())),
                               preferred_element_type=F32)

        def finish(acc):
            if epi is not None:
                acc = epi(acc, *[e[...] for e in e_refs])
            o_ref[...] = acc.astype(o_ref.dtype)

        if nk == 1:
            finish(part)
        else:
            @pl.when(k == 0)
            def _():
                acc_ref[...] = part

            @pl.when((k > 0) & (k < nk - 1))
            def _():
                acc_ref[...] += part

            @pl.when(k == nk - 1)
            def _():
                finish(acc_ref[...] + part)

        if items:
            exchange_finish(items, src_refs, out_refs, sems, (i == gm - 1) & (j == gn - 1) & (k == nk - 1))

    res = pl.pallas_call(
        body, name=name, grid=(gm, gn, nk),
        out_shape=[jax.ShapeDtypeStruct((M, N), out_dtype)] + exchange_out_shapes(items),
        in_specs=[a_spec, b_spec] + [o_spec] * ne + [any_spec] * nx,
        out_specs=[o_spec] + [any_spec] * nx,
        scratch_shapes=[pltpu.VMEM((tm, tn) if nk > 1 else (8, 128), F32)]
        + (exchange_semaphores(items) if items else []),
        compiler_params=_params(("arbitrary",) * 3 if items else ("parallel", "parallel", "arbitrary")),
    )(a, b, *epi_ins, *[it[0] for it in items])
    return res if items else res[0]


def rowwise_call(body_fn, rows, fulls, row_outs, acc_outs, *, tm, name, mm=None, tk=1024):
    rows = [r if isinstance(r, tuple) else (r, r.shape[1], 0) for r in rows]
    T = rows[0][0].shape[0]
    tm = min(tm, T)
    assert T % tm == 0
    n_r, n_f, n_ro = len(rows), len(fulls), len(row_outs)
    into = [(k, ro) for k, ro in enumerate(row_outs) if len(ro) == 3]
    n_b = len(into)
    n_mm, nk = 0, 1
    if mm is not None:
        a, b, mode, pos, a_pro = mm
        n_mm = 2
        K = a.shape[1]
        N = b.shape[1] if mode == "nn" else b.shape[0]
        tk = min(tk, K)
        assert K % tk == 0 and a.shape[0] == T
        nk = K // tk
        b_contract = 0 if mode == "nn" else 1

    def row_body(refs, product):
        r_refs = refs[:n_r]
        f_refs = refs[n_r:n_r + n_f]
        refs = refs[n_r + n_f + n_b:]
        ro_refs = refs[:n_ro]
        ao_refs = refs[n_ro:n_ro + len(acc_outs)]
        r_vals = [r[...].astype(F32) for r in r_refs]
        if product is not None:
            r_vals.insert(pos, product)
        f_vals = [f[...].astype(F32) for f in f_refs]
        ro, ao = body_fn(r_vals, f_vals)
        for ref, v in zip(ro_refs, ro):
            ref[...] = v.astype(ref.dtype)
        if ao_refs:
            @pl.when(pl.program_id(0) == 0)
            def _():
                for ref in ao_refs:
                    ref[...] = jnp.zeros(ref.shape, F32)
            for ref, v in zip(ao_refs, ao):
                ref[...] += v.reshape(ref.shape)

    def body(*refs):
        if mm is None:
            return row_body(refs, None)
        a_ref, b_ref, rest, acc_ref = refs[0], refs[1], refs[2:-1], refs[-1]
        k = pl.program_id(1)
        a_tile = a_ref[...] if a_pro is None else a_pro(a_ref[...])
        part = lax.dot_general(a_tile, b_ref[...], (((1,), (b_contract,)), ((), ())),
                               preferred_element_type=F32)
        if nk == 1:
            return row_body(rest, part)

        @pl.when(k == 0)
        def _():
            acc_ref[...] = part

        @pl.when((k > 0) & (k < nk - 1))
        def _():
            acc_ref[...] += part

        @pl.when(k == nk - 1)
        def _():
            row_body(rest, acc_ref[...] + part)

    def full_spec(shape):
        nd = len(shape)
        return pl.BlockSpec(tuple(shape), lambda i, *_: (0,) * nd)

    def row_spec(w, off):
        return pl.BlockSpec((tm, w), functools.partial(lambda i, *_, o: (i, o), o=off // w))

    in_specs = []
    if mm is not None:
        in_specs.append(pl.BlockSpec((tm, tk), lambda i, k: (i, k)))
        in_specs.append(pl.BlockSpec((tk, N), lambda i, k: (k, 0)) if mode == "nn" else
                        pl.BlockSpec((N, tk), lambda i, k: (0, k)))
    in_specs += [row_spec(w, off) for (_, w, off) in rows]
    in_specs += [full_spec(f.shape) for f in fulls]
    in_specs += [pl.BlockSpec(memory_space=pl.ANY)] * n_b
    out_specs, out_shape = [], []
    for ro in row_outs:
        if len(ro) == 3:
            buf, w, off = ro
            out_specs.append(row_spec(w, off))
            out_shape.append(jax.ShapeDtypeStruct(buf.shape, buf.dtype))
        else:
            w, dt = ro
            out_specs.append(row_spec(w, 0))
            out_shape.append(jax.ShapeDtypeStruct((T, w), dt))
    out_specs += [full_spec(s) for s in acc_outs]
    out_shape += [jax.ShapeDtypeStruct(tuple(s), F32) for s in acc_outs]
    aliases = {n_mm + n_r + n_f + b: k for b, (k, _) in enumerate(into)}
    return pl.pallas_call(
        body, name=name, grid=(T // tm,) if mm is None else (T // tm, nk),
        out_shape=out_shape, in_specs=in_specs, out_specs=out_specs,
        scratch_shapes=[] if mm is None else [pltpu.VMEM((tm, N) if nk > 1 else (8, 128), F32)],
        input_output_aliases=aliases,
        compiler_params=_params(("arbitrary",) if mm is None else ("arbitrary", "arbitrary")),
    )(*([] if mm is None else [a, b]), *[r[0] for r in rows], *fulls, *[ro[0] for _, ro in into])


def fwd_body(fn):
    return lambda r, f: (fn(*r, *f), ())


def bwd_body(fn, n_rows):
    def body(r, f):
        ins, cots = r[:n_rows], r[n_rows:]
        _, vjp = jax.vjp(fn, *ins, *f)
        g = vjp(tuple(cots))
        return g[:n_rows], g[n_rows:]
    return body


def whole_call(fn, ins, outs, *, name):
    n_in = len(ins)

    def body(*refs):
        res = fn(*[r[...] for r in refs[:n_in]])
        for ref, v in zip(refs[n_in:], res):
            ref[...] = v.astype(ref.dtype)

    return pl.pallas_call(
        body, name=name,
        out_shape=[jax.ShapeDtypeStruct(tuple(s), dt) for (s, dt) in outs],
        compiler_params=_params(),
    )(*ins)


def fn_modulate(x, sc, sh):
    return (_rms(x) * (1.0 + sc) + sh,)


def fn_sgu(u, v, nw, ws, bs):
    ug = jax.nn.gelu(u)
    vn = _rms(jax.nn.gelu(v)) * nw
    ri = lax.broadcasted_iota(jnp.int32, (Q, Q), 0)
    ci = lax.broadcasted_iota(jnp.int32, (Q, Q), 1)
    causal = ri >= ci
    chunks = []
    for n in range(u.shape[0] // Q):
        vc = vn[n * Q:(n + 1) * Q]
        cols = [dot_nn(jnp.where(causal, ws[g], 0.0), vc[:, g * Q:(g + 1) * Q]) + bs[g]
                for g in range(GM_GROUPS)]
        chunks.append(jnp.concatenate(cols, axis=1))
    sv = chunks[0] if len(chunks) == 1 else jnp.concatenate(chunks, axis=0)
    return (ug * sv,)


def fn_mix(ga, gb, pa, pb):
    return (jax.nn.sigmoid(ga) * pa + jax.nn.sigmoid(gb) * pb,)


def fn_res_modulate(x, o, g1, sc2, sh2):
    x1 = x + g1 * o
    return x1, _rms(x1) * (1.0 + sc2) + sh2


def relu2_tile(f):
    return jnp.square(jnp.maximum(f.astype(F32), 0.0)).astype(BF16)


def relu2_grad_tile(dact, f):
    return dact * (2.0 * jnp.maximum(f.astype(F32), 0.0))


def final_body(r, f):
    x1, gf, tgt = r
    g2, fnw = f

    def loss_fn(x1, gf, g2, fnw):
        y = _rms(x1 + g2 * gf) * fnw
        row = 0.5 * jnp.mean(jnp.square(y - tgt), axis=-1, keepdims=True)
        return jnp.sum(row, axis=0, keepdims=True)

    l, vjp = jax.vjp(loss_fn, x1, gf, g2, fnw)
    dx1, dgf, dg2, dfnw = vjp(jnp.ones((1, 1), F32))
    return (dx1, dgf), (jnp.broadcast_to(l, (1, 128)), dg2, dfnw)


def grad_x_body(r, f):
    x, dh, dxa = r
    _, vjp = jax.vjp(fn_modulate, x, *f)
    dx, dsc, dsh = vjp((dh,))
    return (dx + dxa,), (dsc, dsh)


CONV_CW = 128
CONV_PAD = 8
CONV_ROWS = 128


def _conv_pre(xp, w_ref, b_ref, r0, R):
    acc = b_ref[...] + w_ref[0:1, :] * xp[r0 + CONV_PAD - 3:r0 + CONV_PAD - 3 + R, :]
    for k in range(1, CONV_K):
        s = r0 + CONV_PAD - 3 + k
        acc = acc + w_ref[k:k + 1, :] * xp[s:s + R, :]
    return acc


def conv_fwd(proj, conv_w, conv_b):
    T = proj.shape[0]
    R = min(CONV_ROWS, T)

    def body(x_ref, w_ref, b_ref, o_ref, xp):
        xp[0:CONV_PAD, :] = jnp.zeros((CONV_PAD, CONV_CW), F32)
        xp[CONV_PAD:CONV_PAD + T, :] = x_ref[...].astype(F32)
        for r0 in range(0, T, R):
            pre = _conv_pre(xp, w_ref, b_ref, r0, R)
            o_ref[r0:r0 + R, :] = (pre * jax.nn.sigmoid(pre)).astype(o_ref.dtype)

    return pl.pallas_call(
        body, name="conv_fwd", grid=(CONV_DIM // CONV_CW,),
        out_shape=jax.ShapeDtypeStruct((T, CONV_DIM), BF16),
        in_specs=[pl.BlockSpec((T, CONV_CW), lambda j: (0, P_XBC // CONV_CW + j)),
                  pl.BlockSpec((CONV_K, CONV_CW), lambda j: (0, j)),
                  pl.BlockSpec((1, CONV_CW), lambda j: (0, j))],
        out_specs=pl.BlockSpec((T, CONV_CW), lambda j: (0, j)),
        scratch_shapes=[pltpu.VMEM((T + CONV_PAD, CONV_CW), F32)],
        compiler_params=_params(("parallel",)),
    )(proj, conv_w, conv_b)


def conv_bwd(proj, dact, col0, conv_w, conv_b, dproj, *, name):
    T = proj.shape[0]
    R = min(CONV_ROWS, T)
    nb = dact.shape[1] // CONV_CW
    c0 = col0 // CONV_CW
    x0 = (P_XBC + col0) // CONV_CW

    def body(x_ref, d_ref, w_ref, b_ref, _, dx_ref, dw_ref, db_ref, xp, dp):
        xp[0:CONV_PAD, :] = jnp.zeros((CONV_PAD, CONV_CW), F32)
        xp[CONV_PAD:CONV_PAD + T, :] = x_ref[...].astype(F32)
        dp[T:T + CONV_PAD, :] = jnp.zeros((CONV_PAD, CONV_CW), F32)
        dws = [jnp.zeros((1, CONV_CW), F32) for _ in range(CONV_K)]
        db = jnp.zeros((1, CONV_CW), F32)
        for r0 in range(0, T, R):
            pre = _conv_pre(xp, w_ref, b_ref, r0, R)
            s = jax.nn.sigmoid(pre)
            dpre = d_ref[r0:r0 + R, :].astype(F32) * (s * (1.0 + pre * (1.0 - s)))
            dp[r0:r0 + R, :] = dpre
            db = db + jnp.sum(dpre, axis=0, keepdims=True)
            for k in range(CONV_K):
                st = r0 + CONV_PAD - 3 + k
                dws[k] = dws[k] + jnp.sum(dpre * xp[st:st + R, :], axis=0, keepdims=True)
        for r0 in range(0, T, R):
            acc = w_ref[0:1, :] * dp[r0 + 3:r0 + 3 + R, :]
            for k in range(1, CONV_K):
                acc = acc + w_ref[k:k + 1, :] * dp[r0 + 3 - k:r0 + 3 - k + R, :]
            dx_ref[r0:r0 + R, :] = acc.astype(dx_ref.dtype)
        for k in range(CONV_K):
            dw_ref[k:k + 1, :] = dws[k]
        db_ref[...] = db

    return pl.pallas_call(
        body, name=name, grid=(nb,),
        out_shape=[jax.ShapeDtypeStruct(dproj.shape, dproj.dtype),
                   jax.ShapeDtypeStruct((CONV_K, nb * CONV_CW), F32),
                   jax.ShapeDtypeStruct((1, nb * CONV_CW), F32)],
        in_specs=[pl.BlockSpec((T, CONV_CW), lambda j: (0, x0 + j)),
                  pl.BlockSpec((T, CONV_CW), lambda j: (0, j)),
                  pl.BlockSpec((CONV_K, CONV_CW), lambda j: (0, c0 + j)),
                  pl.BlockSpec((1, CONV_CW), lambda j: (0, c0 + j)),
                  pl.BlockSpec(memory_space=pl.ANY)],
        out_specs=[pl.BlockSpec((T, CONV_CW), lambda j: (0, x0 + j)),
                   pl.BlockSpec((CONV_K, CONV_CW), lambda j: (0, j)),
                   pl.BlockSpec((1, CONV_CW), lambda j: (0, j))],
        scratch_shapes=[pltpu.VMEM((T + CONV_PAD, CONV_CW), F32),
                        pltpu.VMEM((T + CONV_PAD, CONV_CW), F32)],
        input_output_aliases={4: 0},
        compiler_params=_params(("parallel",)),
    )(proj, dact, conv_w, conv_b, dproj)


def _split3(a):
    hi = a.astype(BF16)
    r = a - hi.astype(F32)
    mid = r.astype(BF16)
    return hi, mid, (r - mid.astype(F32)).astype(BF16)


def _dg3(a, m, ca, cm, a_first):
    dims = (((ca,), (cm,)), ((), ())) if a_first else (((cm,), (ca,)), ((), ()))
    out = None
    for p in _split3(a):
        t = lax.dot_general(p, m, dims, preferred_element_type=F32) if a_first else \
            lax.dot_general(m, p, dims, preferred_element_type=F32)
        out = t if out is None else out + t
    return out


@jax.custom_vjp
def exact_right(a, m):
    return _dg3(a, m, 1, 0, True)


@jax.custom_vjp
def exact_left(m, a):
    return _dg3(a, m, 0, 1, False)


def _expand_bwd(m, g):
    hi = g.astype(BF16)
    lo = (g - hi.astype(F32)).astype(BF16)
    out = lax.dot_general(jnp.concatenate([hi, lo], axis=1), jnp.concatenate([m, m], axis=1),
                          (((1,), (1,)), ((), ())), preferred_element_type=F32)
    return out, jnp.zeros_like(m)


exact_right.defvjp(lambda a, m: (exact_right(a, m), m), _expand_bwd)
exact_left.defvjp(lambda m, a: (exact_left(m, a), m),
                  lambda m, g: (jnp.zeros_like(m), _dg3(g, m, 0, 0, False)))


def ssd_step(lane0, state, x, z, dtr, Bm, Cm, dtb, alog, dsk, nw):
    def iota(shape, dim):
        return lax.broadcasted_iota(jnp.int32, shape, dim)

    def one_hot(mask):
        return mask.astype(F32).astype(BF16)

    causal = iota((Q, Q), 0) >= iota((Q, Q), 1)
    eye = iota((Q, Q), 0) == iota((Q, Q), 1)
    lane = iota((1, 128), 1)
    colh = lax.shift_right_logical(iota((1, SSM_GW), 1), 6)
    to_cols = one_hot(iota((128, SSM_GW), 0) == lane0 + colh)

    dt_all = _softplus(dtr + dtb)
    a_all = dt_all * (-jnp.exp(alog))
    cum_all = exact_left(one_hot(causal), a_all)
    both = exact_right(jnp.concatenate([dt_all, cum_all], axis=0), to_cols)
    dt_f, cum_f = both[:Q], both[Q:]
    last_f = jnp.sum(jnp.where(iota((Q, 1), 0) == Q - 1, cum_f, 0.0), axis=0, keepdims=True)
    dsk_f = jnp.zeros((1, SSM_GW), F32)
    for h in range(SSM_HPG):
        dsk_f = jnp.where(colh == h, dsk[h], dsk_f)

    xdt = x * dt_f
    cb = dot_nt(Cm, Bm)
    ms = []
    for h in range(SSM_HPG):
        ch = jnp.sum(jnp.where(lane == lane0 + h, cum_all, 0.0), axis=1, keepdims=True)
        ch_t = jnp.sum(jnp.where(eye, ch, 0.0), axis=0, keepdims=True)
        ms.append(cb * jnp.exp(jnp.where(causal, ch - ch_t, -1e30)))
    first_half = lane < SSM_P
    blocks = []
    for b in range(SSM_HPG // 2):
        xb = xdt[:, b * 128:(b + 1) * 128]
        rhs = jnp.concatenate([jnp.where(first_half, xb, 0.0), jnp.where(first_half, 0.0, xb)], axis=0)
        blocks.append(dot_nn(jnp.concatenate(ms[2 * b:2 * b + 2], axis=1), rhs))
    y = jnp.concatenate(blocks, axis=1)
    y = y + dot_nn(Cm, state) * jnp.exp(cum_f) + x * dsk_f
    new_state = state * jnp.exp(last_f) + dot_tn(Bm, xdt * jnp.exp(last_f - cum_f))
    gated = y * (z * jax.nn.sigmoid(z))
    return new_state, _rms(gated) * nw


SSD_GPS = 4
_XW = SSD_GPS * SSM_GW
_BW = SSD_GPS * 128


def _ssd_in_specs(rev, nc):
    def n_of(n):
        return nc - 1 - n if rev else n
    return [
        pl.BlockSpec((Q, _XW), lambda g, n: (n_of(n), g)),
        pl.BlockSpec((Q, _BW), lambda g, n: (n_of(n), SSM_INNER // _BW + g)),
        pl.BlockSpec((Q, _BW), lambda g, n: (n_of(n), (SSM_INNER + SSM_GROUPS * 128) // _BW + g)),
        pl.BlockSpec((Q, _XW), lambda g, n: (n_of(n), P_Z // _XW + g)),
        pl.BlockSpec((Q, 128), lambda g, n: (n_of(n), 0)),
        pl.BlockSpec((1, 128), lambda g, n: (0, 0)),
        pl.BlockSpec((1, 128), lambda g, n: (0, 0)),
        pl.BlockSpec((SSD_GPS, SSM_HPG, 1, 1), lambda g, n: (g, 0, 0, 0)),
        pl.BlockSpec((1, _XW), lambda g, n: (0, g)),
    ]


def _ssd_group_inputs(gi, x_ref, b_ref, c_ref, z_ref, dt_ref, dtb_ref, al_ref, dk_ref, nw_ref):
    xs = slice(gi * SSM_GW, (gi + 1) * SSM_GW)
    bs = slice(gi * 128, (gi + 1) * 128)
    return (x_ref[:, xs].astype(F32), z_ref[:, xs].astype(F32), dt_ref[...],
            b_ref[:, bs], c_ref[:, bs],
            dtb_ref[...], al_ref[...], dk_ref[gi], nw_ref[:, xs])


def ssd_fwd(xact, proj, dtg, dtb, alog, dsk, nw, comm):
    T = xact.shape[0]
    nc = T // Q
    nx = len(comm)
    ng = SSM_GROUPS // SSD_GPS
    any_spec = pl.BlockSpec(memory_space=pl.ANY)

    def body(*refs):
        in_refs, src_refs = refs[:9], refs[9:9 + nx]
        yb_ref, st_ref = refs[9 + nx:11 + nx]
        out_refs, state, sems = refs[11 + nx:11 + 2 * nx], refs[11 + 2 * nx], refs[12 + 2 * nx:]
        g, n = pl.program_id(0), pl.program_id(1)
        exchange_start(comm, src_refs, out_refs, sems, (g == 0) & (n == 0))

        @pl.when(n == 0)
        def _():
            state[...] = jnp.zeros(state.shape, F32)

        for gi in range(SSD_GPS):
            lane0 = SSM_HPG * (SSD_GPS * pl.program_id(0) + gi)
            s = state[gi]
            st_ref[gi, 0] = s
            new_s, yb = ssd_step(lane0, s, *_ssd_group_inputs(gi, *in_refs))
            state[gi] = new_s
            yb_ref[:, gi * SSM_GW:(gi + 1) * SSM_GW] = yb.astype(yb_ref.dtype)

        exchange_finish(comm, src_refs, out_refs, sems, (g == ng - 1) & (n == nc - 1))

    return pl.pallas_call(
        body, name="ssd_fwd", grid=(ng, nc),
        out_shape=[jax.ShapeDtypeStruct((T, SSM_INNER), BF16),
                   jax.ShapeDtypeStruct((SSM_GROUPS, nc, 128, SSM_GW), F32)] + exchange_out_shapes(comm),
        in_specs=_ssd_in_specs(False, nc) + [any_spec] * nx,
        out_specs=[pl.BlockSpec((Q, _XW), lambda g, n: (n, g)),
                   pl.BlockSpec((SSD_GPS, 1, 128, SSM_GW), lambda g, n: (g, n, 0, 0))] + [any_spec] * nx,
        scratch_shapes=[pltpu.VMEM((SSD_GPS, 128, SSM_GW), F32)] + exchange_semaphores(comm),
        compiler_params=_params(("arbitrary", "arbitrary")),
    )(xact, xact, xact, proj, dtg, dtb, alog, dsk, nw, *[it[0] for it in comm])


def ssd_bwd(xact, proj, dtg, dtb, alog, dsk, nw, states, dyb, dproj, comm):
    T = xact.shape[0]
    nc = T // Q

    nx = len(comm)
    ng = SSM_GROUPS // SSD_GPS
    any_spec = pl.BlockSpec(memory_space=pl.ANY)

    def body(*refs):
        in_refs, (st_ref, dy_ref, _) = refs[:9], refs[9:12]
        src_refs, refs = refs[12:12 + nx], refs[12 + nx:]
        dx_ref, db_ref, dc_ref, dz_ref, ddt_ref, ddtb_ref, dal_ref, ddk_ref, dnw_ref = refs[:9]
        out_refs, dstate, sems = refs[9:9 + nx], refs[9 + nx], refs[10 + nx:]
        exchange_start(comm, src_refs, out_refs, sems, (pl.program_id(0) == 0) & (pl.program_id(1) == 0))

        @pl.when(pl.program_id(1) == 0)
        def _():
            dstate[...] = jnp.zeros(dstate.shape, F32)
            ddtb_ref[...] = jnp.zeros(ddtb_ref.shape, F32)
            dal_ref[...] = jnp.zeros(dal_ref.shape, F32)
            ddk_ref[...] = jnp.zeros(ddk_ref.shape, F32)
            dnw_ref[...] = jnp.zeros(dnw_ref.shape, F32)

        for gi in range(SSD_GPS):
            xs = slice(gi * SSM_GW, (gi + 1) * SSM_GW)
            bs = slice(gi * 128, (gi + 1) * 128)
            lane0 = SSM_HPG * (SSD_GPS * pl.program_id(0) + gi)
            ins = (st_ref[gi, 0],) + _ssd_group_inputs(gi, *in_refs)
            _, vjp = jax.vjp(functools.partial(ssd_step, lane0), *ins)
            ds, dx, dz, ddt, dbm, dcm, ddtb, dal, ddk, dnw = vjp((dstate[gi], dy_ref[:, xs].astype(F32)))
            dstate[gi] = ds
            dx_ref[:, xs] = dx.astype(dx_ref.dtype)
            db_ref[:, bs] = dbm.astype(db_ref.dtype)
            dc_ref[:, bs] = dcm.astype(dc_ref.dtype)
            dz_ref[:, xs] = dz.astype(dz_ref.dtype)
            ddt_ref[:, bs] = ddt
            ddtb_ref[gi] += ddtb
            dal_ref[gi] += dal
            ddk_ref[gi] += ddk
            dnw_ref[:, xs] += dnw

        exchange_finish(comm, src_refs, out_refs, sems,
                        (pl.program_id(0) == ng - 1) & (pl.program_id(1) == nc - 1))

    rev = lambda n: nc - 1 - n
    row_shape = jax.ShapeDtypeStruct((SSM_GROUPS, 1, 128), F32)
    row_spec = pl.BlockSpec((SSD_GPS, 1, 128), lambda g, n: (g, 0, 0))
    return pl.pallas_call(
        body, name="ssd_bwd", grid=(ng, nc),
        out_shape=[jax.ShapeDtypeStruct((T, SSM_INNER), BF16),
                   jax.ShapeDtypeStruct((T, SSM_GROUPS * 128), BF16),
                   jax.ShapeDtypeStruct((T, SSM_GROUPS * 128), BF16),
                   jax.ShapeDtypeStruct(dproj.shape, dproj.dtype),
                   jax.ShapeDtypeStruct((T, SSM_GROUPS * 128), F32),
                   row_shape, row_shape,
                   jax.ShapeDtypeStruct((SSM_GROUPS, SSM_HPG, 1, 1), F32),
                   jax.ShapeDtypeStruct((1, SSM_INNER), F32)] + exchange_out_shapes(comm),
        in_specs=_ssd_in_specs(True, nc) + [
            pl.BlockSpec((SSD_GPS, 1, 128, SSM_GW), lambda g, n: (g, rev(n), 0, 0)),
            pl.BlockSpec((Q, _XW), lambda g, n: (rev(n), g)),
            any_spec] + [any_spec] * nx,
        out_specs=[pl.BlockSpec((Q, _XW), lambda g, n: (rev(n), g)),
                   pl.BlockSpec((Q, _BW), lambda g, n: (rev(n), g)),
                   pl.BlockSpec((Q, _BW), lambda g, n: (rev(n), g)),
                   pl.BlockSpec((Q, _XW), lambda g, n: (rev(n), P_Z // _XW + g)),
                   pl.BlockSpec((Q, _BW), lambda g, n: (rev(n), g)),
                   row_spec, row_spec,
                   pl.BlockSpec((SSD_GPS, SSM_HPG, 1, 1), lambda g, n: (g, 0, 0, 0)),
                   pl.BlockSpec((1, _XW), lambda g, n: (0, g))] + [any_spec] * nx,
        scratch_shapes=[pltpu.VMEM((SSD_GPS, 128, SSM_GW), F32)] + exchange_semaphores(comm),
        input_output_aliases={11: 3},
        compiler_params=_params(("arbitrary", "arbitrary")),
    )(xact, xact, xact, proj, dtg, dtb, alog, dsk, nw, states, dyb, dproj, *[it[0] for it in comm])


ADAMW_WHOLE_ELEMS = 256 * 1024


def adamw(w, g, m, v, *, name):
    shape = w.shape
    parts = g.shape != shape
    nd = len(shape)
    if nd == 3 and shape[1] == 1 and w.size > ADAMW_WHOLE_ELEMS:
        assert not parts and shape[0] % 4 == 0
        grid = (4,)
        spec = g_spec = pl.BlockSpec((shape[0] // 4, 1, shape[2]), lambda i: (i, 0, 0))
    else:
        if w.size <= ADAMW_WHOLE_ELEMS:
            grid, tr = (1,), shape[-2]
        else:
            assert all(s == 1 for s in shape[:-2]) and shape[-2] % 256 == 0
            grid, tr = (shape[-2] // 256,), 256
        blk = tuple(shape[:-2]) + (tr, shape[-1])
        spec = pl.BlockSpec(blk, lambda i: (0,) * (nd - 2) + (i, 0))
        g_spec = pl.BlockSpec((N_DEV,) + blk[1:], lambda i: (0,) * (nd - 2) + (i, 0)) if parts else spec

    def body(w_ref, g_ref, m_ref, v_ref, go_ref, d_ref, nm_ref, nv_ref):
        if parts:
            g = g_ref[0:1].astype(F32)
            for j in range(1, N_DEV):
                g = g + g_ref[j:j + 1].astype(F32)
        else:
            g = g_ref[...]
        nm = ADAM_B1 * m_ref[...] + (1.0 - ADAM_B1) * g
        nv = ADAM_B2 * v_ref[...] + (1.0 - ADAM_B2) * jnp.square(g)
        m_hat = nm / (1.0 - ADAM_B1 ** ADAM_STEP)
        v_hat = nv / (1.0 - ADAM_B2 ** ADAM_STEP)
        go_ref[...] = g
        d_ref[...] = -ADAM_LR * (m_hat / (jnp.sqrt(v_hat) + ADAM_EPS) + ADAM_WD * w_ref[...])
        nm_ref[...] = nm
        nv_ref[...] = nv

    shp = jax.ShapeDtypeStruct(shape, F32)
    return pl.pallas_call(
        body, name=name, grid=grid,
        out_shape=[shp] * 4, in_specs=[spec, g_spec, spec, spec], out_specs=[spec] * 4,
        compiler_params=_params(("parallel",)),
    )(w, g, m, v)


def _pad_rows(a, rows):
    return jnp.pad(a, ((0, rows - a.shape[0]), (0, 0)))


WIN_W = 1408
N_IN = IN_WIDTH // N_DEV
_A6 = OFF_DT - 6 * N_IN
_C6 = 7 * N_IN - OFF_GA


def _win_offset(me):
    return jnp.where(me == 7, 124, 4 * me)


def _w_in_window(shard, me):
    rows = shard.shape[0]
    z = lambda n: jnp.zeros((rows, n), shard.dtype)
    a = lax.dynamic_update_slice(z(WIN_W), shard, (0, _win_offset(me)))
    b = jnp.concatenate([z(24), shard[:, :_A6], shard[:, _A6 + 32:], z(4), shard[:, _A6:_A6 + 32], z(96)], axis=1)
    return jnp.where(me == 6, b, a)


def _w_in_from_window(window, me):
    a = lax.dynamic_slice(window, (0, _win_offset(me)), (window.shape[0], N_IN))
    b = jnp.concatenate([window[:, 24:24 + _A6], window[:, 1280:1312], window[:, 24 + _A6:24 + _A6 + _C6]], axis=1)
    return jnp.where(me == 6, b, a)


def _w_all_from_windows(g):
    def merge_first(p, t):
        return jnp.concatenate([p[:, :128] + t, p[:, 128:]], axis=1)

    parts = [g[0][:, :1280]]
    for j in range(1, 6):
        parts.append(merge_first(g[j][:, :1280], g[j - 1][:, 1280:]))
    p6 = merge_first(g[6][:, :1280], g[5][:, 1280:])
    parts.append(jnp.concatenate([p6[:, :1152], p6[:, 1152:] + g[7][:, :128]], axis=1))
    parts.append(g[7][:, 128:])
    parts.append(g[6][:, 1280:])
    return jnp.concatenate(parts, axis=1)


def _windows_of_w_all(gw):
    wins = [gw[:, 1280 * j:1280 * j + WIN_W] for j in range(6)]
    wins.append(jnp.concatenate([gw[:, 7680:8960], gw[:, PROJ_W:]], axis=1))
    wins.append(gw[:, 8832:PROJ_W])
    return jnp.stack(wins)


def kernel(x, c, w_mod, b_mod, w_in, gm_norm_w, gm_ws, gm_bs, conv_w, conv_b, dt_bias, a_log, d_skip, ssm_norm_w, w_branch_gm, w_branch_ssm, w_out, w_ff1, w_ff2, final_norm_w, loss_target, m_w_mod, m_b_mod, m_w_in, m_gm_norm_w, m_gm_ws, m_gm_bs, m_conv_w, m_conv_b, m_dt_bias, m_a_log, m_d_skip, m_ssm_norm_w, m_w_branch_gm, m_w_branch_ssm, m_w_out, m_w_ff1, m_w_ff2, m_final_norm_w, v_w_mod, v_b_mod, v_w_in, v_gm_norm_w, v_gm_ws, v_gm_bs, v_conv_w, v_conv_b, v_dt_bias, v_a_log, v_d_skip, v_ssm_norm_w, v_w_branch_gm, v_w_branch_ssm, v_w_out, v_w_ff1, v_w_ff2, v_final_norm_w):
    T = x.shape[1]
    me = 4 * lax.axis_index("x") + 2 * lax.axis_index("y") + lax.axis_index("c")
    x2 = x[0]
    tgt = loss_target[0]
    n_in = IN_WIDTH // N_DEV
    n_mod = N_MOD * D // N_DEV
    n_cv = CONV_DIM // N_DEV

    c_all, conv_w_full = exchange(
        [(c.reshape(8, 128), _whole, (N_DEV, 8, 128), _slot),
         (conv_w[0], _whole, (N_DEV, CONV_K, n_cv), _slot)], name="gather_c_convw")
    c_all = c_all.reshape(N_DEV, D)
    conv_w_full = conv_w_full.transpose(1, 0, 2).reshape(CONV_K, CONV_DIM)

    win = _w_in_window(w_in[0].astype(BF16), me)
    gwin = gather_blocks_two_level(win, name="gather_w_in")
    late_weights = [
        (w_branch_gm[0].astype(BF16), _whole, (D, D), _rows(D // N_DEV)),
        (w_branch_ssm[0].astype(BF16), _whole, (SSM_INNER, D), _rows(SSM_INNER // N_DEV)),
        (w_out[0].astype(BF16), _whole, (D, D), _rows(D // N_DEV)),
        (w_ff1[0].astype(BF16), _whole, (D, D_FF), _cols(D_FF // N_DEV)),
        (w_ff2[0].astype(BF16), _whole, (D_FF, D), _rows(D_FF // N_DEV))]
    w_all = _w_all_from_windows(gwin)
    w_dt = w_all[:, PROJ_W:]

    c_pad = _pad_rows(c_all, 128)
    b_mine = lax.dynamic_slice(b_mod, (0, me * n_mod), (1, n_mod))

    def mod_fn(cp, w, b):
        ca = cp * jax.nn.sigmoid(cp)
        return (jnp.dot(ca, w, precision=HIGHEST, preferred_element_type=F32) + b,)

    (mod_part,) = whole_call(mod_fn, [c_pad, w_mod[0], b_mine], [((128, n_mod), F32)], name="mod_fwd")
    gmod = gather_blocks(mod_part[:N_DEV], name="gather_mod")
    mod = lax.dynamic_index_in_dim(gmod, me, axis=1, keepdims=False).reshape(N_MOD, D)
    sh1, sc1, gt1, sh2, sc2, gt2 = [mod[i:i + 1] for i in range(N_MOD)]

    (h,) = rowwise_call(fwd_body(fn_modulate), [x2], [sc1, sh1], [(D, BF16)], [], tm=256, name="modulate1")
    proj = matmul(h, w_all, "nn", BF16, name="mm_proj", n=PROJ_W, tm=K1_TM)
    dtg = matmul(h, w_dt, "nn", F32, name="mm_dt")
    ws = gm_ws[0]
    bs3 = gm_bs[0].reshape(GM_GROUPS, Q, 1)
    sgu_rows = [(proj, D, P_U), (proj, D, P_V)]
    (ya,) = rowwise_call(fwd_body(fn_sgu), sgu_rows, [gm_norm_w, ws, bs3], [(D, BF16)], [],
                         tm=256, name="sgu_fwd")
    xact = conv_fwd(proj, conv_w_full, conv_b)
    dtb4 = jnp.pad(dt_bias, ((0, 0), (0, 96)))
    alog4 = jnp.pad(a_log, ((0, 0), (0, 96)))
    dsk4 = d_skip.reshape(SSM_GROUPS, SSM_HPG, 1, 1)
    yb, states, w_gm_f, w_ssm_f, w_out_f, w_ff1_f, w_ff2_f = ssd_fwd(
        xact, proj, dtg, dtb4, alog4, dsk4, ssm_norm_w, late_weights)
    pa = matmul(ya, w_gm_f, "nn", F32, name="mm_branch_gm", tm=K1_TM)
    gate_rows = [(proj, D, P_GA), (proj, D, P_GB)]
    mixed, pb = rowwise_call(
        lambda r, fl: (fn_mix(*r) + (r[3],), ()), gate_rows + [pa], [], [(D, BF16), (D, F32)], [],
        tm=FUSED_TM, name="branch_ssm_mix", mm=(yb, w_ssm_f, "nn", 3, None), tk=SSM_INNER)
    x1, h2, o = rowwise_call(
        lambda r, fl: (fn_res_modulate(*r, *fl) + (r[1],), ()), [x2], [gt1, sc2, sh2],
        [(D, F32), (D, BF16), (D, F32)], [], tm=FUSED_TM, name="out_res_modulate2",
        mm=(mixed, w_out_f, "nn", 1, None))
    f = matmul(h2, w_ff1_f, "nn", BF16, name="mm_ff1", tm=K1_TM)

    dx1, dgf, loss_v, dgt2, dfnw = rowwise_call(
        final_body, [x1, tgt], [gt2, final_norm_w.reshape(1, D)], [(D, F32), (D, BF16)],
        [(1, 128), (1, D), (1, D)], tm=FUSED_TM, name="ff2_loss_bwd", mm=(f, w_ff2_f, "nn", 1, relu2_tile),
        tk=D_FF)
    df = matmul(dgf, w_ff2_f, "nt", BF16, name="mm_ff2_dgrad", epi=relu2_grad_tile, epi_ins=(f,), tm=K1_TM)
    gw_ff2 = matmul(f, dgf, "tn", BF16, name="mm_ff2_wgrad", tk=WGRAD_TK, a_pro=relu2_tile)
    gw_ff1 = matmul(h2, df, "tn", BF16, name="mm_ff1_wgrad", tk=WGRAD_TK)

    def res_mod_bwd(r, fl):
        xv, ov, dx1v, dh2v = r
        _, vjp = jax.vjp(fn_res_modulate, xv, ov, *fl)
        dxv, dov, dg1, dsc, dsh = vjp((dx1v, dh2v))
        return (dxv, dov), (dg1, dsc, dsh)

    dxa, do, dgt1, dsc2, dsh2 = rowwise_call(
        res_mod_bwd, [x2, o, dx1], [gt1, sc2, sh2], [(D, F32), (D, BF16)],
        [(1, D), (1, D), (1, D)], tm=FUSED_TM, name="ff1_dgrad_res_modulate2_bwd",
        mm=(df, w_ff1_f, "nt", 3, None), tk=D_FF)
    gw_out = matmul(mixed, do, "tn", BF16, name="mm_out_wgrad", tk=WGRAD_TK)
    dproj = lax.empty((T, ALL_W), BF16)

    def mix_bwd(r, fl):
        dga, dgb, dpa, dpb = bwd_body(fn_mix, 4)(r, fl)[0]
        return (jnp.concatenate([dga, dgb], axis=1), dpa, dpb), ()

    dproj, dpa, dpb = rowwise_call(
        mix_bwd, gate_rows + [pa, pb], [], [(dproj, 2 * D, P_GA), (D, BF16), (D, BF16)], [],
        tm=FUSED_TM, name="out_dgrad_mix_bwd", mm=(do, w_out_f, "nt", 4, None))
    gw_gm = matmul(ya, dpa, "tn", BF16, name="mm_branch_gm_wgrad", tk=WGRAD_TK)
    dyb = matmul(dpb, w_ssm_f, "nt", BF16, name="mm_branch_ssm_dgrad", tm=K1_TM)
    gw_ssm = matmul(yb, dpb, "tn", BF16, name="mm_branch_ssm_wgrad", tk=WGRAD_TK)

    def sgu_bwd(r, fl):
        (du, dv), acc = bwd_body(fn_sgu, 2)(r, fl)
        return (jnp.concatenate([du, dv], axis=1),), acc

    dproj, dgnw, dws, dbs = rowwise_call(
        sgu_bwd, sgu_rows, [gm_norm_w, ws, bs3], [(dproj, 2 * D, P_U)],
        [(1, D), (GM_GROUPS, Q, Q), (GM_GROUPS, Q, 1)], tm=FUSED_TM, name="branch_gm_dgrad_sgu_bwd",
        mm=(dpa, w_gm_f, "nt", 2, None))
    early_grads = [
        (gw_gm, _rows(D // N_DEV), (N_DEV, D // N_DEV, D), _slot),
        (gw_ssm, _rows(SSM_INNER // N_DEV), (N_DEV, SSM_INNER // N_DEV, D), _slot),
        (gw_out, _rows(D // N_DEV), (N_DEV, D // N_DEV, D), _slot),
        (gw_ff1, _cols(D_FF // N_DEV), (N_DEV, D, D_FF // N_DEV), _slot),
        (gw_ff2, _rows(D_FF // N_DEV), (N_DEV, D_FF // N_DEV, D), _slot),
        (_pack_rows([dgnw, dws, dbs], EARLY_ROWS), _whole, (N_DEV, sum(EARLY_ROWS), 128), _slot)]
    (dxs, dbm, dcm, dproj, ddt8, ddtb, dalog, ddsk, dsnw,
     r_gm, r_ssm, r_out, r_ff1, r_ff2, early_all) = ssd_bwd(
        xact, proj, dtg, dtb4, alog4, dsk4, ssm_norm_w, states, dyb, dproj, early_grads)
    dconv_w, dconv_b = [], []
    for nm, dact_part, col0 in (("xs", dxs, 0), ("b", dbm, SSM_INNER), ("c", dcm, SSM_INNER + SSM_GROUPS * 128)):
        dproj, dcw, dcb = conv_bwd(proj, dact_part, col0, conv_w_full, conv_b, dproj, name="conv_bwd_" + nm)
        dconv_w.append(dcw)
        dconv_b.append(dcb)
    dconv_w = jnp.concatenate(dconv_w, axis=1)
    dconv_b = jnp.concatenate(dconv_b, axis=1)
    (dproj,) = rowwise_call(
        lambda r, fl: ((functools.reduce(jnp.add, r),), ()),
        [(ddt8, 128, 128 * g) for g in range(SSM_GROUPS)], [], [(dproj, 128, PROJ_W)], [],
        tm=1024, name="ddt_into_dproj")
    gw_all = matmul(h, dproj, "tn", BF16, name="mm_in_wgrad", tn=1152, tk=WGRAD_TK)
    mid_pack = _pack_rows([dconv_w, dconv_b, jnp.sum(ddtb, axis=0), jnp.sum(dalog, axis=0), ddsk, dsnw, dfnw,
                           jnp.concatenate([dgt1, dsh2, dsc2, dgt2], axis=0)], MID_ROWS)
    dh, r_in, mid_all = matmul(
        dproj, w_all, "nt", F32, name="mm_in_dgrad", tk=3456,
        comm=[(_windows_of_w_all(gw_all), _slot, (N_DEV, D, WIN_W), _slot),
              (mid_pack, _whole, (N_DEV, sum(MID_ROWS), 128), _slot)])
    grad_x, dsc1, dsh1 = rowwise_call(grad_x_body, [x2, dh, dxa], [sc1, sh1], [(D, F32)],
                                      [(1, D), (1, D)], tm=256, name="modulate1_bwd")

    g_w_in = _w_in_from_window(sum_devices(r_in, tr=256, name="sum_w_in_grads"), me).reshape(1, D, n_in)

    late_all = gather_blocks(_pack_rows([dsh1, dsc1], LATE_ROWS), name="gather_dmod1")
    s_early = _unpack_rows(sum_devices(early_all, tr=early_all.shape[1], name="sum_small_early"), EARLY_ROWS)
    s_mid = _unpack_rows(sum_devices(mid_all, tr=mid_all.shape[1], name="sum_small_mid"), MID_ROWS)
    s_late = _unpack_rows(sum_devices(late_all, tr=late_all.shape[1], name="sum_small_late"), LATE_ROWS)
    g_gm_norm_w = s_early[0][:D].reshape(1, D)
    g_gm_ws = s_early[1].reshape(GM_GROUPS * Q, Q)
    g_gm_bs = s_early[2][:GM_GROUPS * Q].reshape(GM_GROUPS, Q)
    g_conv_w_full = s_mid[0].reshape(CONV_K, CONV_DIM)
    g_conv_w = lax.dynamic_slice(g_conv_w_full, (0, me * n_cv), (CONV_K, n_cv))
    g_conv_b = s_mid[1].reshape(1, CONV_DIM)
    g_dt_bias = s_mid[2][:32].reshape(1, 32)
    g_a_log = s_mid[3][:32].reshape(1, 32)
    g_d_skip = s_mid[4][:32].reshape(1, 32)
    g_ssm_norm_w = s_mid[5].reshape(1, SSM_INNER)
    g_final_norm_w = s_mid[6][:D].reshape(1, D)
    g_b_mod = jnp.concatenate([s_late[0][:D], s_late[1][:D], s_mid[7]]).reshape(1, N_MOD * D)

    dmod_all = jnp.concatenate(
        [late_all.reshape(N_DEV, -1)[:, :2 * D],
         mid_all[:, sum(MID_ROWS[:7]):].reshape(N_DEV, 4 * D)], axis=1)
    dmod_mine = _pad_rows(lax.dynamic_slice(dmod_all, (0, me * n_mod), (N_DEV, n_mod)), 128)

    def wmod_grad_fn(cp, dm):
        ca = cp * jax.nn.sigmoid(cp)
        return (lax.dot_general(ca, dm, (((0,), (0,)), ((), ())), precision=HIGHEST,
                                preferred_element_type=F32),)

    (g_w_mod,) = whole_call(wmod_grad_fn, [c_pad, dmod_mine], [((D, n_mod), F32)], name="w_mod_grad")

    upd = {}

    def step(name, w, g, m, v, parts=False):
        upd[name] = adamw(w, g if parts else g.reshape(w.shape), m, v, name="adamw_" + name)

    step("w_mod", w_mod, g_w_mod, m_w_mod, v_w_mod)
    step("b_mod", b_mod, g_b_mod, m_b_mod, v_b_mod)
    col_major = lambda a: jnp.transpose(a, (2, 0, 1))
    upd["w_in"] = tuple(jnp.transpose(o, (1, 2, 0)) for o in adamw(
        col_major(w_in), col_major(g_w_in), col_major(m_w_in), col_major(v_w_in), name="adamw_w_in"))
    step("gm_norm_w", gm_norm_w, g_gm_norm_w, m_gm_norm_w, v_gm_norm_w)
    step("gm_ws", gm_ws, g_gm_ws, m_gm_ws, v_gm_ws)
    step("gm_bs", gm_bs, g_gm_bs, m_gm_bs, v_gm_bs)
    step("conv_w", conv_w, g_conv_w, m_conv_w, v_conv_w)
    step("conv_b", conv_b, g_conv_b, m_conv_b, v_conv_b)
    step("dt_bias", dt_bias, g_dt_bias, m_dt_bias, v_dt_bias)
    step("a_log", a_log, g_a_log, m_a_log, v_a_log)
    step("d_skip", d_skip, g_d_skip, m_d_skip, v_d_skip)
    step("ssm_norm_w", ssm_norm_w, g_ssm_norm_w, m_ssm_norm_w, v_ssm_norm_w)
    step("w_branch_gm", w_branch_gm, r_gm, m_w_branch_gm, v_w_branch_gm, parts=True)
    step("w_branch_ssm", w_branch_ssm, r_ssm, m_w_branch_ssm, v_w_branch_ssm, parts=True)
    step("w_out", w_out, r_out, m_w_out, v_w_out, parts=True)
    step("w_ff1", w_ff1, r_ff1, m_w_ff1, v_w_ff1, parts=True)
    step("w_ff2", w_ff2, r_ff2, m_w_ff2, v_w_ff2, parts=True)
    step("final_norm_w", final_norm_w.reshape(1, D), g_final_norm_w, m_final_norm_w.reshape(1, D),
         v_final_norm_w.reshape(1, D))
    upd["final_norm_w"] = tuple(a.reshape(D) for a in upd["final_norm_w"])

    loss = lax.psum(loss_v[0, 0], ("x", "y", "c"))
    order = ["w_mod", "b_mod", "w_in", "gm_norm_w", "gm_ws", "gm_bs", "conv_w", "conv_b", "dt_bias", "a_log",
             "d_skip", "ssm_norm_w", "w_branch_gm", "w_branch_ssm", "w_out", "w_ff1", "w_ff2", "final_norm_w"]
    return (loss, grad_x.reshape(1, T, D),
            *[upd[n][0] for n in order], *[upd[n][1] for n in order],
            *[upd[n][2] for n in order], *[upd[n][3] for n in order])
```

```python
import functools

import jax
import jax.numpy as jnp
from jax import lax
from jax.experimental import pallas as pl
from jax.experimental.pallas import tpu as pltpu

F32 = jnp.float32
BF16 = jnp.bfloat16
MESH = pl.DeviceIdType.MESH
HIGHEST = lax.Precision.HIGHEST

N_DEV = 8
D = 1024
Q = 128
GM_GROUPS = 8
SSM_INNER = 2048
SSM_GROUPS = 8
SSM_HPG = 4
SSM_P = 64
SSM_GW = SSM_HPG * SSM_P
CONV_DIM = 4096
CONV_K = 4
D_FF = 4096
N_MOD = 6
EPS = 1e-6
IN_WIDTH = 10272
OFF_DT = 8192
OFF_GA = 8224
PROJ_W = 10240
ALL_W = 10368
P_U, P_V, P_Z, P_XBC, P_GA, P_GB = 0, 1024, 2048, 4096, 8192, 9216

ADAM_LR = 0.001
ADAM_B1 = 0.9
ADAM_B2 = 0.999
ADAM_EPS = 1e-08
ADAM_WD = 0.01
ADAM_STEP = 10

VMEM_LIMIT_BYTES = 48 * 1024 * 1024
K1_TM = 2048
FUSED_TM = 512
WGRAD_TK = 2048
EARLY_ROWS = (8, 1024, 8)
MID_ROWS = (128, 32, 8, 8, 8, 16, 8, 32)
LATE_ROWS = (8, 8)


def _pack_rows(arrs, rows):
    def rows128(a, r):
        a = a.reshape(-1)
        return jnp.pad(a, (0, r * 128 - a.shape[0])).reshape(r, 128)
    return jnp.concatenate([rows128(a, r) for a, r in zip(arrs, rows)], axis=0)


def _unpack_rows(s, rows):
    out, o = [], 0
    for r in rows:
        out.append(s[o:o + r].reshape(-1))
        o += r
    return out


def _params(sem=None):
    return pltpu.CompilerParams(dimension_semantics=sem, vmem_limit_bytes=VMEM_LIMIT_BYTES)


def _dg(a, b, ca, cb):
    return lax.dot_general(a.astype(BF16), b.astype(BF16), (((ca,), (cb,)), ((), ())),
                           preferred_element_type=F32)


@jax.custom_vjp
def dot_nn(a, b):
    return _dg(a, b, 1, 0)


@jax.custom_vjp
def dot_nt(a, b):
    return _dg(a, b, 1, 1)


@jax.custom_vjp
def dot_tn(a, b):
    return _dg(a, b, 0, 0)


def _like(ct, primal):
    return ct.astype(primal.dtype)


dot_nn.defvjp(lambda a, b: (dot_nn(a, b), (a, b)),
              lambda r, g: (_like(dot_nt(g, r[1]), r[0]), _like(dot_tn(r[0], g), r[1])))
dot_nt.defvjp(lambda a, b: (dot_nt(a, b), (a, b)),
              lambda r, g: (_like(dot_nn(g, r[1]), r[0]), _like(dot_tn(g, r[0]), r[1])))
dot_tn.defvjp(lambda a, b: (dot_tn(a, b), (a, b)),
              lambda r, g: (_like(dot_nt(r[1], g), r[0]), _like(dot_nn(r[0], g), r[1])))


def _rms(x):
    return x * lax.rsqrt(jnp.mean(x * x, axis=-1, keepdims=True) + EPS)


def _softplus(x):
    return jnp.maximum(x, 0.0) + jnp.log1p(jnp.exp(-jnp.abs(x)))


def _rows(n):
    return lambda ref, j: ref.at[pl.ds(pl.multiple_of(j * n, n), n)]


def _cols(n):
    return lambda ref, j: ref.at[:, pl.ds(pl.multiple_of(j * n, n), n)]


def _slot(ref, j):
    return ref.at[j]


def _whole(ref, j):
    return ref


def exchange(items, *, name):
    n = len(items)

    def body(*refs):
        exchange_in_body(items, refs[:n], refs[n:2 * n], refs[2 * n:], True, True)

    return pl.pallas_call(
        body, name=name,
        out_shape=exchange_out_shapes(items),
        in_specs=[pl.BlockSpec(memory_space=pl.ANY)] * n,
        out_specs=[pl.BlockSpec(memory_space=pl.ANY)] * n,
        scratch_shapes=exchange_semaphores(items),
    )(*[it[0] for it in items])


def exchange_out_shapes(items):
    return [jax.ShapeDtypeStruct(tuple(shape), src.dtype) for (src, _, shape, _) in items]


def exchange_semaphores(items):
    n = len(items)
    return [pltpu.SemaphoreType.DMA((n, N_DEV - 1)), pltpu.SemaphoreType.DMA((n, N_DEV - 1)),
            pltpu.SemaphoreType.DMA((n,))]


def _exchange_copies(items, src_refs, out_refs, sems):
    send_sems, recv_sems, local_sems = sems
    x = lax.axis_index("x")
    y = lax.axis_index("y")
    c = lax.axis_index("c")
    me = 4 * x + 2 * y + c
    local = [pltpu.make_async_copy(src_win(src_refs[i], me), dst_win(out_refs[i], me), local_sems.at[i])
             for i, (_, src_win, _, dst_win) in enumerate(items)]
    remote = []
    for i, (_, src_win, _, dst_win) in enumerate(items):
        for k in range(1, N_DEV):
            px = lax.rem(x + ((k >> 2) & 1), 2)
            py = lax.rem(y + ((k >> 1) & 1), 2)
            pc = lax.rem(c + (k & 1), 2)
            peer = 4 * px + 2 * py + pc
            remote.append(pltpu.make_async_remote_copy(
                src_ref=src_win(src_refs[i], peer), dst_ref=dst_win(out_refs[i], me),
                send_sem=send_sems.at[i, k - 1], recv_sem=recv_sems.at[i, k - 1],
                device_id=(px, py, pc), device_id_type=MESH))
    return local, remote


def _when(cond, fn):
    if cond is True:
        fn()
    else:
        pl.when(cond)(fn)


def exchange_start(items, src_refs, out_refs, sems, cond):
    def start():
        local, remote = _exchange_copies(items, src_refs, out_refs, sems)
        for cp in local + remote:
            cp.start()
    _when(cond, start)


def exchange_finish(items, src_refs, out_refs, sems, cond):
    def finish():
        local, remote = _exchange_copies(items, src_refs, out_refs, sems)
        for cp in remote:
            cp.wait_send()
        for cp in remote:
            cp.wait_recv()
        for cp in local:
            cp.wait()
    _when(cond, finish)


def exchange_in_body(items, src_refs, out_refs, sems, first, last):
    exchange_start(items, src_refs, out_refs, sems, first)
    exchange_finish(items, src_refs, out_refs, sems, last)


def gather_blocks(src, *, name):
    return exchange([(src, _whole, (N_DEV,) + src.shape, _slot)], name=name)[0]


def gather_blocks_two_level(src, *, name):
    def body(src_ref, out_ref, send_sems, recv_sems, local_sem):
        x = lax.axis_index("x")
        y = lax.axis_index("y")
        c = lax.axis_index("c")
        me, sibling = (x, y, c), (x, y, 1 - c)
        chips = [(1 - x, y), (x, 1 - y), (1 - x, 1 - y)]

        def slot(px, py, pc):
            return out_ref.at[4 * px + 2 * py + pc]

        def copy(k, block, to, src=None):
            return pltpu.make_async_remote_copy(
                src_ref=slot(*block) if src is None else src, dst_ref=slot(*block),
                send_sem=send_sems.at[k], recv_sem=recv_sems.at[k], device_id=to, device_id_type=MESH)

        mine = pltpu.make_async_copy(src_ref, slot(*me), local_sem)
        mine.start()
        first = [copy(0, me, sibling, src=src_ref)]
        first += [copy(1 + j, me, (*chip, c), src=src_ref) for j, chip in enumerate(chips)]
        for cp in first:
            cp.start()
        passed = [copy(4 + j, (*chip, c), sibling) for j, chip in enumerate(chips)]
        for j, chip in enumerate(chips):
            copy(1 + j, (*chip, c), me).wait_recv()
            passed[j].start()
        copy(0, sibling, me).wait_recv()
        for j, chip in enumerate(chips):
            copy(4 + j, (*chip, 1 - c), me).wait_recv()
        for cp in first + passed:
            cp.wait_send()
        mine.wait()

    return pl.pallas_call(
        body, name=name,
        out_shape=jax.ShapeDtypeStruct((N_DEV,) + src.shape, src.dtype),
        in_specs=[pl.BlockSpec(memory_space=pl.ANY)],
        out_specs=pl.BlockSpec(memory_space=pl.ANY),
        scratch_shapes=[pltpu.SemaphoreType.DMA((N_DEV - 1,)), pltpu.SemaphoreType.DMA((N_DEV - 1,)),
                        pltpu.SemaphoreType.DMA(())],
    )(src)


def sum_devices(g, *, tr, name):
    _, R, C = g.shape

    def body(g_ref, o_ref):
        acc = g_ref[0].astype(F32)
        for j in range(1, N_DEV):
            acc = acc + g_ref[j].astype(F32)
        o_ref[...] = acc

    return pl.pallas_call(
        body, name=name, grid=(R // tr,),
        out_shape=jax.ShapeDtypeStruct((R, C), F32),
        in_specs=[pl.BlockSpec((N_DEV, tr, C), lambda i: (0, i, 0))],
        out_specs=pl.BlockSpec((tr, C), lambda i: (i, 0)),
        compiler_params=_params(("parallel",)),
    )(g)


def matmul(a, b, mode, out_dtype, *, name, tm=1024, tn=1024, tk=1024, n=None, comm=None,
           a_pro=None, epi=None, epi_ins=()):
    if mode == "nn":
        (M, K), (K2, N) = a.shape, b.shape
    elif mode == "nt":
        (M, K), (N, K2) = a.shape, b.shape
    else:
        (K, M), (K2, N) = a.shape, b.shape
    assert K == K2
    N = N if n is None else n
    tm, tn, tk = min(tm, M), min(tn, N), min(tk, K)
    assert M % tm == 0 and N % tn == 0 and K % tk == 0, (name, M, N, K, tm, tn, tk)
    nk = K // tk
    if mode == "tn":
        a_spec = pl.BlockSpec((tk, tm), lambda i, j, k: (k, i))
    else:
        a_spec = pl.BlockSpec((tm, tk), lambda i, j, k: (i, k))
    if mode == "nt":
        b_spec = pl.BlockSpec((tn, tk), lambda i, j, k: (j, k))
    else:
        b_spec = pl.BlockSpec((tk, tn), lambda i, j, k: (k, j))
    dims = {"nn": (1, 0), "nt": (1, 1), "tn": (0, 0)}[mode]
    items = list(comm) if comm else []
    nx = len(items)
    ne = len(epi_ins)
    gm, gn = M // tm, N // tn
    any_spec = pl.BlockSpec(memory_space=pl.ANY)
    o_spec = pl.BlockSpec((tm, tn), lambda i, j, k: (i, j))

    def body(*refs):
        a_ref, b_ref, e_refs = refs[0], refs[1], refs[2:2 + ne]
        refs = refs[2 + ne:]
        src_refs, o_ref, out_refs = refs[:nx], refs[nx], refs[1 + nx:1 + 2 * nx]
        acc_ref, sems = refs[1 + 2 * nx], refs[2 + 2 * nx:]
        i, j, k = pl.program_id(0), pl.program_id(1), pl.program_id(2)
        if items:
            exchange_start(items, src_refs, out_refs, sems, (i == 0) & (j == 0) & (k == 0))
        a_tile = a_ref[...] if a_pro is None else a_pro(a_ref[...])
        part = lax.dot_general(a_tile, b_ref[...], (((dims[0],), (dims[1],)), ((), ())),
                               preferred_element_type=F32)

        def finish(acc):
            if epi is not None:
                acc = epi(acc, *[e[...] for e in e_refs])
            o_ref[...] = acc.astype(o_ref.dtype)

        if nk == 1:
            finish(part)
        else:
            @pl.when(k == 0)
            def _():
                acc_ref[...] = part

            @pl.when((k > 0) & (k < nk - 1))
            def _():
                acc_ref[...] += part

            @pl.when(k == nk - 1)
            def _():
                finish(acc_ref[...] + part)

        if items:
            exchange_finish(items, src_refs, out_refs, sems, (i == gm - 1) & (j == gn - 1) & (k == nk - 1))

    res = pl.pallas_call(
        body, name=name, grid=(gm, gn, nk),
        out_shape=[jax.ShapeDtypeStruct((M, N), out_dtype)] + exchange_out_shapes(items),
        in_specs=[a_spec, b_spec] + [o_spec] * ne + [any_spec] * nx,
        out_specs=[o_spec] + [any_spec] * nx,
        scratch_shapes=[pltpu.VMEM((tm, tn) if nk > 1 else (8, 128), F32)]
        + (exchange_semaphores(items) if items else []),
        compiler_params=_params(("arbitrary",) * 3 if items else ("parallel", "parallel", "arbitrary")),
    )(a, b, *epi_ins, *[it[0] for it in items])
    return res if items else res[0]


def rowwise_call(body_fn, rows, fulls, row_outs, acc_outs, *, tm, name, mm=None, tk=1024):
    rows = [r if isinstance(r, tuple) else (r, r.shape[1], 0) for r in rows]
    T = rows[0][0].shape[0]
    tm = min(tm, T)
    assert T % tm == 0
    n_r, n_f, n_ro = len(rows), len(fulls), len(row_outs)
    into = [(k, ro) for k, ro in enumerate(row_outs) if len(ro) == 3]
    n_b = len(into)
    n_mm, nk = 0, 1
    if mm is not None:
        a, b, mode, pos, a_pro = mm
        n_mm = 2
        K = a.shape[1]
        N = b.shape[1] if mode == "nn" else b.shape[0]
        tk = min(tk, K)
        assert K % tk == 0 and a.shape[0] == T
        nk = K // tk
        b_contract = 0 if mode == "nn" else 1

    def row_body(refs, product):
        r_refs = refs[:n_r]
        f_refs = refs[n_r:n_r + n_f]
        refs = refs[n_r + n_f + n_b:]
        ro_refs = refs[:n_ro]
        ao_refs = refs[n_ro:n_ro + len(acc_outs)]
        r_vals = [r[...].astype(F32) for r in r_refs]
        if product is not None:
            r_vals.insert(pos, product)
        f_vals = [f[...].astype(F32) for f in f_refs]
        ro, ao = body_fn(r_vals, f_vals)
        for ref, v in zip(ro_refs, ro):
            ref[...] = v.astype(ref.dtype)
        if ao_refs:
            @pl.when(pl.program_id(0) == 0)
            def _():
                for ref in ao_refs:
                    ref[...] = jnp.zeros(ref.shape, F32)
            for ref, v in zip(ao_refs, ao):
                ref[...] += v.reshape(ref.shape)

    def body(*refs):
        if mm is None:
            return row_body(refs, None)
        a_ref, b_ref, rest, acc_ref = refs[0], refs[1], refs[2:-1], refs[-1]
        k = pl.program_id(1)
        a_tile = a_ref[...] if a_pro is None else a_pro(a_ref[...])
        part = lax.dot_general(a_tile, b_ref[...], (((1,), (b_contract,)), ((), ())),
                               preferred_element_type=F32)
        if nk == 1:
            return row_body(rest, part)

        @pl.when(k == 0)
        def _():
            acc_ref[...] = part

        @pl.when((k > 0) & (k < nk - 1))
        def _():
            acc_ref[...] += part

        @pl.when(k == nk - 1)
        def _():
            row_body(rest, acc_ref[...] + part)

    def full_spec(shape):
        nd = len(shape)
        return pl.BlockSpec(tuple(shape), lambda i, *_: (0,) * nd)

    def row_spec(w, off):
        return pl.BlockSpec((tm, w), functools.partial(lambda i, *_, o: (i, o), o=off // w))

    in_specs = []
    if mm is not None:
        in_specs.append(pl.BlockSpec((tm, tk), lambda i, k: (i, k)))
        in_specs.append(pl.BlockSpec((tk, N), lambda i, k: (k, 0)) if mode == "nn" else
                        pl.BlockSpec((N, tk), lambda i, k: (0, k)))
    in_specs += [row_spec(w, off) for (_, w, off) in rows]
    in_specs += [full_spec(f.shape) for f in fulls]
    in_specs += [pl.BlockSpec(memory_space=pl.ANY)] * n_b
    out_specs, out_shape = [], []
    for ro in row_outs:
        if len(ro) == 3:
            buf, w, off = ro
            out_specs.append(row_spec(w, off))
            out_shape.append(jax.ShapeDtypeStruct(buf.shape, buf.dtype))
        else:
            w, dt = ro
            out_specs.append(row_spec(w, 0))
            out_shape.append(jax.ShapeDtypeStruct((T, w), dt))
    out_specs += [full_spec(s) for s in acc_outs]
    out_shape += [jax.ShapeDtypeStruct(tuple(s), F32) for s in acc_outs]
    aliases = {n_mm + n_r + n_f + b: k for b, (k, _) in enumerate(into)}
    return pl.pallas_call(
        body, name=name, grid=(T // tm,) if mm is None else (T // tm, nk),
        out_shape=out_shape, in_specs=in_specs, out_specs=out_specs,
        scratch_shapes=[] if mm is None else [pltpu.VMEM((tm, N) if nk > 1 else (8, 128), F32)],
        input_output_aliases=aliases,
        compiler_params=_params(("arbitrary",) if mm is None else ("arbitrary", "arbitrary")),
    )(*([] if mm is None else [a, b]), *[r[0] for r in rows], *fulls, *[ro[0] for _, ro in into])


def fwd_body(fn):
    return lambda r, f: (fn(*r, *f), ())


def bwd_body(fn, n_rows):
    def body(r, f):
        ins, cots = r[:n_rows], r[n_rows:]
        _, vjp = jax.vjp(fn, *ins, *f)
        g = vjp(tuple(cots))
        return g[:n_rows], g[n_rows:]
    return body


def whole_call(fn, ins, outs, *, name):
    n_in = len(ins)

    def body(*refs):
        res = fn(*[r[...] for r in refs[:n_in]])
        for ref, v in zip(refs[n_in:], res):
            ref[...] = v.astype(ref.dtype)

    return pl.pallas_call(
        body, name=name,
        out_shape=[jax.ShapeDtypeStruct(tuple(s), dt) for (s, dt) in outs],
        compiler_params=_params(),
    )(*ins)


def fn_modulate(x, sc, sh):
    return (_rms(x) * (1.0 + sc) + sh,)


def fn_sgu(u, v, nw, ws, bs):
    ug = jax.nn.gelu(u)
    vn = _rms(jax.nn.gelu(v)) * nw
    ri = lax.broadcasted_iota(jnp.int32, (Q, Q), 0)
    ci = lax.broadcasted_iota(jnp.int32, (Q, Q), 1)
    causal = ri >= ci
    chunks = []
    for n in range(u.shape[0] // Q):
        vc = vn[n * Q:(n + 1) * Q]
        cols = [dot_nn(jnp.where(causal, ws[g], 0.0), vc[:, g * Q:(g + 1) * Q]) + bs[g]
                for g in range(GM_GROUPS)]
        chunks.append(jnp.concatenate(cols, axis=1))
    sv = chunks[0] if len(chunks) == 1 else jnp.concatenate(chunks, axis=0)
    return (ug * sv,)


def fn_mix(ga, gb, pa, pb):
    return (jax.nn.sigmoid(ga) * pa + jax.nn.sigmoid(gb) * pb,)


def fn_res_modulate(x, o, g1, sc2, sh2):
    x1 = x + g1 * o
    return x1, _rms(x1) * (1.0 + sc2) + sh2


def relu2_tile(f):
    return jnp.square(jnp.maximum(f.astype(F32), 0.0)).astype(BF16)


def relu2_grad_tile(dact, f):
    return dact * (2.0 * jnp.maximum(f.astype(F32), 0.0))


def final_body(r, f):
    x1, gf, tgt = r
    g2, fnw = f

    def loss_fn(x1, gf, g2, fnw):
        y = _rms(x1 + g2 * gf) * fnw
        row = 0.5 * jnp.mean(jnp.square(y - tgt), axis=-1, keepdims=True)
        return jnp.sum(row, axis=0, keepdims=True)

    l, vjp = jax.vjp(loss_fn, x1, gf, g2, fnw)
    dx1, dgf, dg2, dfnw = vjp(jnp.ones((1, 1), F32))
    return (dx1, dgf), (jnp.broadcast_to(l, (1, 128)), dg2, dfnw)


def grad_x_body(r, f):
    x, dh, dxa = r
    _, vjp = jax.vjp(fn_modulate, x, *f)
    dx, dsc, dsh = vjp((dh,))
    return (dx + dxa,), (dsc, dsh)


CONV_CW = 128
CONV_PAD = 8
CONV_ROWS = 128


def _conv_pre(xp, w_ref, b_ref, r0, R):
    acc = b_ref[...] + w_ref[0:1, :] * xp[r0 + CONV_PAD - 3:r0 + CONV_PAD - 3 + R, :]
    for k in range(1, CONV_K):
        s = r0 + CONV_PAD - 3 + k
        acc = acc + w_ref[k:k + 1, :] * xp[s:s + R, :]
    return acc


def conv_fwd(proj, conv_w, conv_b):
    T = proj.shape[0]
    R = min(CONV_ROWS, T)

    def body(x_ref, w_ref, b_ref, o_ref, xp):
        xp[0:CONV_PAD, :] = jnp.zeros((CONV_PAD, CONV_CW), F32)
        xp[CONV_PAD:CONV_PAD + T, :] = x_ref[...].astype(F32)
        for r0 in range(0, T, R):
            pre = _conv_pre(xp, w_ref, b_ref, r0, R)
            o_ref[r0:r0 + R, :] = (pre * jax.nn.sigmoid(pre)).astype(o_ref.dtype)

    return pl.pallas_call(
        body, name="conv_fwd", grid=(CONV_DIM // CONV_CW,),
        out_shape=jax.ShapeDtypeStruct((T, CONV_DIM), BF16),
        in_specs=[pl.BlockSpec((T, CONV_CW), lambda j: (0, P_XBC // CONV_CW + j)),
                  pl.BlockSpec((CONV_K, CONV_CW), lambda j: (0, j)),
                  pl.BlockSpec((1, CONV_CW), lambda j: (0, j))],
        out_specs=pl.BlockSpec((T, CONV_CW), lambda j: (0, j)),
        scratch_shapes=[pltpu.VMEM((T + CONV_PAD, CONV_CW), F32)],
        compiler_params=_params(("parallel",)),
    )(proj, conv_w, conv_b)


def conv_bwd(proj, dact, col0, conv_w, conv_b, dproj, *, name):
    T = proj.shape[0]
    R = min(CONV_ROWS, T)
    nb = dact.shape[1] // CONV_CW
    c0 = col0 // CONV_CW
    x0 = (P_XBC + col0) // CONV_CW

    def body(x_ref, d_ref, w_ref, b_ref, _, dx_ref, dw_ref, db_ref, xp, dp):
        xp[0:CONV_PAD, :] = jnp.zeros((CONV_PAD, CONV_CW), F32)
        xp[CONV_PAD:CONV_PAD + T, :] = x_ref[...].astype(F32)
        dp[T:T + CONV_PAD, :] = jnp.zeros((CONV_PAD, CONV_CW), F32)
        dws = [jnp.zeros((1, CONV_CW), F32) for _ in range(CONV_K)]
        db = jnp.zeros((1, CONV_CW), F32)
        for r0 in range(0, T, R):
            pre = _conv_pre(xp, w_ref, b_ref, r0, R)
            s = jax.nn.sigmoid(pre)
            dpre = d_ref[r0:r0 + R, :].astype(F32) * (s * (1.0 + pre * (1.0 - s)))
            dp[r0:r0 + R, :] = dpre
            db = db + jnp.sum(dpre, axis=0, keepdims=True)
            for k in range(CONV_K):
                st = r0 + CONV_PAD - 3 + k
                dws[k] = dws[k] + jnp.sum(dpre * xp[st:st + R, :], axis=0, keepdims=True)
        for r0 in range(0, T, R):
            acc = w_ref[0:1, :] * dp[r0 + 3:r0 + 3 + R, :]
            for k in range(1, CONV_K):
                acc = acc + w_ref[k:k + 1, :] * dp[r0 + 3 - k:r0 + 3 - k + R, :]
            dx_ref[r0:r0 + R, :] = acc.astype(dx_ref.dtype)
        for k in range(CONV_K):
            dw_ref[k:k + 1, :] = dws[k]
        db_ref[...] = db

    return pl.pallas_call(
        body, name=name, grid=(nb,),
        out_shape=[jax.ShapeDtypeStruct(dproj.shape, dproj.dtype),
                   jax.ShapeDtypeStruct((CONV_K, nb * CONV_CW), F32),
                   jax.ShapeDtypeStruct((1, nb * CONV_CW), F32)],
        in_specs=[pl.BlockSpec((T, CONV_CW), lambda j: (0, x0 + j)),
                  pl.BlockSpec((T, CONV_CW), lambda j: (0, j)),
                  pl.BlockSpec((CONV_K, CONV_CW), lambda j: (0, c0 + j)),
                  pl.BlockSpec((1, CONV_CW), lambda j: (0, c0 + j)),
                  pl.BlockSpec(memory_space=pl.ANY)],
        out_specs=[pl.BlockSpec((T, CONV_CW), lambda j: (0, x0 + j)),
                   pl.BlockSpec((CONV_K, CONV_CW), lambda j: (0, j)),
                   pl.BlockSpec((1, CONV_CW), lambda j: (0, j))],
        scratch_shapes=[pltpu.VMEM((T + CONV_PAD, CONV_CW), F32),
                        pltpu.VMEM((T + CONV_PAD, CONV_CW), F32)],
        input_output_aliases={4: 0},
        compiler_params=_params(("parallel",)),
    )(proj, dact, conv_w, conv_b, dproj)


def _split3(a):
    hi = a.astype(BF16)
    r = a - hi.astype(F32)
    mid = r.astype(BF16)
    return hi, mid, (r - mid.astype(F32)).astype(BF16)


def _dg3(a, m, ca, cm, a_first):
    dims = (((ca,), (cm,)), ((), ())) if a_first else (((cm,), (ca,)), ((), ()))
    out = None
    for p in _split3(a):
        t = lax.dot_general(p, m, dims, preferred_element_type=F32) if a_first else \
            lax.dot_general(m, p, dims, preferred_element_type=F32)
        out = t if out is None else out + t
    return out


@jax.custom_vjp
def exact_right(a, m):
    return _dg3(a, m, 1, 0, True)


@jax.custom_vjp
def exact_left(m, a):
    return _dg3(a, m, 0, 1, False)


def _expand_bwd(m, g):
    hi = g.astype(BF16)
    lo = (g - hi.astype(F32)).astype(BF16)
    out = lax.dot_general(jnp.concatenate([hi, lo], axis=1), jnp.concatenate([m, m], axis=1),
                          (((1,), (1,)), ((), ())), preferred_element_type=F32)
    return out, jnp.zeros_like(m)


exact_right.defvjp(lambda a, m: (exact_right(a, m), m), _expand_bwd)
exact_left.defvjp(lambda m, a: (exact_left(m, a), m),
                  lambda m, g: (jnp.zeros_like(m), _dg3(g, m, 0, 0, False)))


def ssd_step(lane0, state, x, z, dtr, Bm, Cm, dtb, alog, dsk, nw):
    def iota(shape, dim):
        return lax.broadcasted_iota(jnp.int32, shape, dim)

    def one_hot(mask):
        return mask.astype(F32).astype(BF16)

    causal = iota((Q, Q), 0) >= iota((Q, Q), 1)
    eye = iota((Q, Q), 0) == iota((Q, Q), 1)
    lane = iota((1, 128), 1)
    colh = lax.shift_right_logical(iota((1, SSM_GW), 1), 6)
    to_cols = one_hot(iota((128, SSM_GW), 0) == lane0 + colh)

    dt_all = _softplus(dtr + dtb)
    a_all = dt_all * (-jnp.exp(alog))
    cum_all = exact_left(one_hot(causal), a_all)
    both = exact_right(jnp.concatenate([dt_all, cum_all], axis=0), to_cols)
    dt_f, cum_f = both[:Q], both[Q:]
    last_f = jnp.sum(jnp.where(iota((Q, 1), 0) == Q - 1, cum_f, 0.0), axis=0, keepdims=True)
    dsk_f = jnp.zeros((1, SSM_GW), F32)
    for h in range(SSM_HPG):
        dsk_f = jnp.where(colh == h, dsk[h], dsk_f)

    xdt = x * dt_f
    cb = dot_nt(Cm, Bm)
    ms = []
    for h in range(SSM_HPG):
        ch = jnp.sum(jnp.where(lane == lane0 + h, cum_all, 0.0), axis=1, keepdims=True)
        ch_t = jnp.sum(jnp.where(eye, ch, 0.0), axis=0, keepdims=True)
        ms.append(cb * jnp.exp(jnp.where(causal, ch - ch_t, -1e30)))
    first_half = lane < SSM_P
    blocks = []
    for b in range(SSM_HPG // 2):
        xb = xdt[:, b * 128:(b + 1) * 128]
        rhs = jnp.concatenate([jnp.where(first_half, xb, 0.0), jnp.where(first_half, 0.0, xb)], axis=0)
        blocks.append(dot_nn(jnp.concatenate(ms[2 * b:2 * b + 2], axis=1), rhs))
    y = jnp.concatenate(blocks, axis=1)
    y = y + dot_nn(Cm, state) * jnp.exp(cum_f) + x * dsk_f
    new_state = state * jnp.exp(last_f) + dot_tn(Bm, xdt * jnp.exp(last_f - cum_f))
    gated = y * (z * jax.nn.sigmoid(z))
    return new_state, _rms(gated) * nw


SSD_GPS = 4
_XW = SSD_GPS * SSM_GW
_BW = SSD_GPS * 128


def _ssd_in_specs(rev, nc):
    def n_of(n):
        return nc - 1 - n if rev else n
    return [
        pl.BlockSpec((Q, _XW), lambda g, n: (n_of(n), g)),
        pl.BlockSpec((Q, _BW), lambda g, n: (n_of(n), SSM_INNER // _BW + g)),
        pl.BlockSpec((Q, _BW), lambda g, n: (n_of(n), (SSM_INNER + SSM_GROUPS * 128) // _BW + g)),
        pl.BlockSpec((Q, _XW), lambda g, n: (n_of(n), P_Z // _XW + g)),
        pl.BlockSpec((Q, 128), lambda g, n: (n_of(n), 0)),
        pl.BlockSpec((1, 128), lambda g, n: (0, 0)),
        pl.BlockSpec((1, 128), lambda g, n: (0, 0)),
        pl.BlockSpec((SSD_GPS, SSM_HPG, 1, 1), lambda g, n: (g, 0, 0, 0)),
        pl.BlockSpec((1, _XW), lambda g, n: (0, g)),
    ]


def _ssd_group_inputs(gi, x_ref, b_ref, c_ref, z_ref, dt_ref, dtb_ref, al_ref, dk_ref, nw_ref):
    xs = slice(gi * SSM_GW, (gi + 1) * SSM_GW)
    bs = slice(gi * 128, (gi + 1) * 128)
    return (x_ref[:, xs].astype(F32), z_ref[:, xs].astype(F32), dt_ref[...],
            b_ref[:, bs], c_ref[:, bs],
            dtb_ref[...], al_ref[...], dk_ref[gi], nw_ref[:, xs])


def ssd_fwd(xact, proj, dtg, dtb, alog, dsk, nw, comm):
    T = xact.shape[0]
    nc = T // Q
    nx = len(comm)
    ng = SSM_GROUPS // SSD_GPS
    any_spec = pl.BlockSpec(memory_space=pl.ANY)

    def body(*refs):
        in_refs, src_refs = refs[:9], refs[9:9 + nx]
        yb_ref, st_ref = refs[9 + nx:11 + nx]
        out_refs, state, sems = refs[11 + nx:11 + 2 * nx], refs[11 + 2 * nx], refs[12 + 2 * nx:]
        g, n = pl.program_id(0), pl.program_id(1)
        exchange_start(comm, src_refs, out_refs, sems, (g == 0) & (n == 0))

        @pl.when(n == 0)
        def _():
            state[...] = jnp.zeros(state.shape, F32)

        for gi in range(SSD_GPS):
            lane0 = SSM_HPG * (SSD_GPS * pl.program_id(0) + gi)
            s = state[gi]
            st_ref[gi, 0] = s
            new_s, yb = ssd_step(lane0, s, *_ssd_group_inputs(gi, *in_refs))
            state[gi] = new_s
            yb_ref[:, gi * SSM_GW:(gi + 1) * SSM_GW] = yb.astype(yb_ref.dtype)

        exchange_finish(comm, src_refs, out_refs, sems, (g == ng - 1) & (n == nc - 1))

    return pl.pallas_call(
        body, name="ssd_fwd", grid=(ng, nc),
        out_shape=[jax.ShapeDtypeStruct((T, SSM_INNER), BF16),
                   jax.ShapeDtypeStruct((SSM_GROUPS, nc, 128, SSM_GW), F32)] + exchange_out_shapes(comm),
        in_specs=_ssd_in_specs(False, nc) + [any_spec] * nx,
        out_specs=[pl.BlockSpec((Q, _XW), lambda g, n: (n, g)),
                   pl.BlockSpec((SSD_GPS, 1, 128, SSM_GW), lambda g, n: (g, n, 0, 0))] + [any_spec] * nx,
        scratch_shapes=[pltpu.VMEM((SSD_GPS, 128, SSM_GW), F32)] + exchange_semaphores(comm),
        compiler_params=_params(("arbitrary", "arbitrary")),
    )(xact, xact, xact, proj, dtg, dtb, alog, dsk, nw, *[it[0] for it in comm])


def ssd_bwd(xact, proj, dtg, dtb, alog, dsk, nw, states, dyb, dproj, comm):
    T = xact.shape[0]
    nc = T // Q

    nx = len(comm)
    ng = SSM_GROUPS // SSD_GPS
    any_spec = pl.BlockSpec(memory_space=pl.ANY)

    def body(*refs):
        in_refs, (st_ref, dy_ref, _) = refs[:9], refs[9:12]
        src_refs, refs = refs[12:12 + nx], refs[12 + nx:]
        dx_ref, db_ref, dc_ref, dz_ref, ddt_ref, ddtb_ref, dal_ref, ddk_ref, dnw_ref = refs[:9]
        out_refs, dstate, sems = refs[9:9 + nx], refs[9 + nx], refs[10 + nx:]
        exchange_start(comm, src_refs, out_refs, sems, (pl.program_id(0) == 0) & (pl.program_id(1) == 0))

        @pl.when(pl.program_id(1) == 0)
        def _():
            dstate[...] = jnp.zeros(dstate.shape, F32)
            ddtb_ref[...] = jnp.zeros(ddtb_ref.shape, F32)
            dal_ref[...] = jnp.zeros(dal_ref.shape, F32)
            ddk_ref[...] = jnp.zeros(ddk_ref.shape, F32)
            dnw_ref[...] = jnp.zeros(dnw_ref.shape, F32)

        for gi in range(SSD_GPS):
            xs = slice(gi * SSM_GW, (gi + 1) * SSM_GW)
            bs = slice(gi * 128, (gi + 1) * 128)
            lane0 = SSM_HPG * (SSD_GPS * pl.program_id(0) + gi)
            ins = (st_ref[gi, 0],) + _ssd_group_inputs(gi, *in_refs)
            _, vjp = jax.vjp(functools.partial(ssd_step, lane0), *ins)
            ds, dx, dz, ddt, dbm, dcm, ddtb, dal, ddk, dnw = vjp((dstate[gi], dy_ref[:, xs].astype(F32)))
            dstate[gi] = ds
            dx_ref[:, xs] = dx.astype(dx_ref.dtype)
            db_ref[:, bs] = dbm.astype(db_ref.dtype)
            dc_ref[:, bs] = dcm.astype(dc_ref.dtype)
            dz_ref[:, xs] = dz.astype(dz_ref.dtype)
            ddt_ref[:, bs] = ddt
            ddtb_ref[gi] += ddtb
            dal_ref[gi] += dal
            ddk_ref[gi] += ddk
            dnw_ref[:, xs] += dnw

        exchange_finish(comm, src_refs, out_refs, sems,
                        (pl.program_id(0) == ng - 1) & (pl.program_id(1) == nc - 1))

    rev = lambda n: nc - 1 - n
    row_shape = jax.ShapeDtypeStruct((SSM_GROUPS, 1, 128), F32)
    row_spec = pl.BlockSpec((SSD_GPS, 1, 128), lambda g, n: (g, 0, 0))
    return pl.pallas_call(
        body, name="ssd_bwd", grid=(ng, nc),
        out_shape=[jax.ShapeDtypeStruct((T, SSM_INNER), BF16),
                   jax.ShapeDtypeStruct((T, SSM_GROUPS * 128), BF16),
                   jax.ShapeDtypeStruct((T, SSM_GROUPS * 128), BF16),
                   jax.ShapeDtypeStruct(dproj.shape, dproj.dtype),
                   jax.ShapeDtypeStruct((T, SSM_GROUPS * 128), F32),
                   row_shape, row_shape,
                   jax.ShapeDtypeStruct((SSM_GROUPS, SSM_HPG, 1, 1), F32),
                   jax.ShapeDtypeStruct((1, SSM_INNER), F32)] + exchange_out_shapes(comm),
        in_specs=_ssd_in_specs(True, nc) + [
            pl.BlockSpec((SSD_GPS, 1, 128, SSM_GW), lambda g, n: (g, rev(n), 0, 0)),
            pl.BlockSpec((Q, _XW), lambda g, n: (rev(n), g)),
            any_spec] + [any_spec] * nx,
        out_specs=[pl.BlockSpec((Q, _XW), lambda g, n: (rev(n), g)),
                   pl.BlockSpec((Q, _BW), lambda g, n: (rev(n), g)),
                   pl.BlockSpec((Q, _BW), lambda g, n: (rev(n), g)),
                   pl.BlockSpec((Q, _XW), lambda g, n: (rev(n), P_Z // _XW + g)),
                   pl.BlockSpec((Q, _BW), lambda g, n: (rev(n), g)),
                   row_spec, row_spec,
                   pl.BlockSpec((SSD_GPS, SSM_HPG, 1, 1), lambda g, n: (g, 0, 0, 0)),
                   pl.BlockSpec((1, _XW), lambda g, n: (0, g))] + [any_spec] * nx,
        scratch_shapes=[pltpu.VMEM((SSD_GPS, 128, SSM_GW), F32)] + exchange_semaphores(comm),
        input_output_aliases={11: 3},
        compiler_params=_params(("arbitrary", "arbitrary")),
    )(xact, xact, xact, proj, dtg, dtb, alog, dsk, nw, states, dyb, dproj, *[it[0] for it in comm])


ADAMW_WHOLE_ELEMS = 256 * 1024


def adamw(w, g, m, v, *, name):
    shape = w.shape
    parts = g.shape != shape
    nd = len(shape)
    if nd == 3 and shape[1] == 1 and w.size > ADAMW_WHOLE_ELEMS:
        assert not parts and shape[0] % 4 == 0
        grid = (4,)
        spec = g_spec = pl.BlockSpec((shape[0] // 4, 1, shape[2]), lambda i: (i, 0, 0))
    else:
        if w.size <= ADAMW_WHOLE_ELEMS:
            grid, tr = (1,), shape[-2]
        else:
            assert all(s == 1 for s in shape[:-2]) and shape[-2] % 256 == 0
            grid, tr = (shape[-2] // 256,), 256
        blk = tuple(shape[:-2]) + (tr, shape[-1])
        spec = pl.BlockSpec(blk, lambda i: (0,) * (nd - 2) + (i, 0))
        g_spec = pl.BlockSpec((N_DEV,) + blk[1:], lambda i: (0,) * (nd - 2) + (i, 0)) if parts else spec

    def body(w_ref, g_ref, m_ref, v_ref, go_ref, d_ref, nm_ref, nv_ref):
        if parts:
            g = g_ref[0:1].astype(F32)
            for j in range(1, N_DEV):
                g = g + g_ref[j:j + 1].astype(F32)
        else:
            g = g_ref[...]
        nm = ADAM_B1 * m_ref[...] + (1.0 - ADAM_B1) * g
        nv = ADAM_B2 * v_ref[...] + (1.0 - ADAM_B2) * jnp.square(g)
        m_hat = nm / (1.0 - ADAM_B1 ** ADAM_STEP)
        v_hat = nv / (1.0 - ADAM_B2 ** ADAM_STEP)
        go_ref[...] = g
        d_ref[...] = -ADAM_LR * (m_hat / (jnp.sqrt(v_hat) + ADAM_EPS) + ADAM_WD * w_ref[...])
        nm_ref[...] = nm
        nv_ref[...] = nv

    shp = jax.ShapeDtypeStruct(shape, F32)
    return pl.pallas_call(
        body, name=name, grid=grid,
        out_shape=[shp] * 4, in_specs=[spec, g_spec, spec, spec], out_specs=[spec] * 4,
        compiler_params=_params(("parallel",)),
    )(w, g, m, v)


def _pad_rows(a, rows):
    return jnp.pad(a, ((0, rows - a.shape[0]), (0, 0)))


WIN_W = 1408
N_IN = IN_WIDTH // N_DEV
_A6 = OFF_DT - 6 * N_IN
_C6 = 7 * N_IN - OFF_GA


_WIN_OFFSETS = (0, 4, 8, 12, 16, 20, None, 124)


def _w_in_window(shard, me):
    rows = shard.shape[0]
    z = lambda n: jnp.zeros((rows, n), shard.dtype)

    def plain(off):
        return lambda s: jnp.pad(s, ((0, 0), (off, WIN_W - N_IN - off)))

    def split(s):
        return jnp.concatenate([z(24), s[:, :_A6], s[:, _A6 + 32:], z(4), s[:, _A6:_A6 + 32], z(96)], axis=1)

    return lax.switch(me, [split if off is None else plain(off) for off in _WIN_OFFSETS], shard)


def _w_in_from_window(window, me):
    def plain(off):
        return lambda w: w[:, off:off + N_IN]

    def split(w):
        return jnp.concatenate([w[:, 24:24 + _A6], w[:, 1280:1312], w[:, 24 + _A6:24 + _A6 + _C6]], axis=1)

    return lax.switch(me, [split if off is None else plain(off) for off in _WIN_OFFSETS], window)


def _w_all_from_windows(g):
    def merge_first(p, t):
        return jnp.concatenate([p[:, :128] + t, p[:, 128:]], axis=1)

    parts = [g[0][:, :1280]]
    for j in range(1, 6):
        parts.append(merge_first(g[j][:, :1280], g[j - 1][:, 1280:]))
    p6 = merge_first(g[6][:, :1280], g[5][:, 1280:])
    parts.append(jnp.concatenate([p6[:, :1152], p6[:, 1152:] + g[7][:, :128]], axis=1))
    parts.append(g[7][:, 128:])
    parts.append(g[6][:, 1280:])
    return jnp.concatenate(parts, axis=1)


def _windows_of_w_all(gw):
    wins = [gw[:, 1280 * j:1280 * j + WIN_W] for j in range(6)]
    wins.append(jnp.concatenate([gw[:, 7680:8960], gw[:, PROJ_W:]], axis=1))
    wins.append(gw[:, 8832:PROJ_W])
    return jnp.stack(wins)


def kernel(x, c, w_mod, b_mod, w_in, gm_norm_w, gm_ws, gm_bs, conv_w, conv_b, dt_bias, a_log, d_skip, ssm_norm_w, w_branch_gm, w_branch_ssm, w_out, w_ff1, w_ff2, final_norm_w, loss_target, m_w_mod, m_b_mod, m_w_in, m_gm_norm_w, m_gm_ws, m_gm_bs, m_conv_w, m_conv_b, m_dt_bias, m_a_log, m_d_skip, m_ssm_norm_w, m_w_branch_gm, m_w_branch_ssm, m_w_out, m_w_ff1, m_w_ff2, m_final_norm_w, v_w_mod, v_b_mod, v_w_in, v_gm_norm_w, v_gm_ws, v_gm_bs, v_conv_w, v_conv_b, v_dt_bias, v_a_log, v_d_skip, v_ssm_norm_w, v_w_branch_gm, v_w_branch_ssm, v_w_out, v_w_ff1, v_w_ff2, v_final_norm_w):
    T = x.shape[1]
    me = 4 * lax.axis_index("x") + 2 * lax.axis_index("y") + lax.axis_index("c")
    x2 = x[0]
    tgt = loss_target[0]
    n_in = IN_WIDTH // N_DEV
    n_mod = N_MOD * D // N_DEV
    n_cv = CONV_DIM // N_DEV

    c_all, conv_w_full = exchange(
        [(c.reshape(8, 128), _whole, (N_DEV, 8, 128), _slot),
         (conv_w[0], _whole, (N_DEV, CONV_K, n_cv), _slot)], name="gather_c_convw")
    c_all = c_all.reshape(N_DEV, D)
    conv_w_full = conv_w_full.transpose(1, 0, 2).reshape(CONV_K, CONV_DIM)

    win = _w_in_window(w_in[0].astype(BF16), me)
    gwin = gather_blocks_two_level(win, name="gather_w_in")
    late_weights = [
        (w_branch_gm[0].astype(BF16), _whole, (D, D), _rows(D // N_DEV)),
        (w_branch_ssm[0].astype(BF16), _whole, (SSM_INNER, D), _rows(SSM_INNER // N_DEV)),
        (w_out[0].astype(BF16), _whole, (D, D), _rows(D // N_DEV)),
        (w_ff1[0].astype(BF16), _whole, (D, D_FF), _cols(D_FF // N_DEV)),
        (w_ff2[0].astype(BF16), _whole, (D_FF, D), _rows(D_FF // N_DEV))]
    w_all = _w_all_from_windows(gwin)
    w_dt = w_all[:, PROJ_W:]

    c_pad = _pad_rows(c_all, 128)
    b_mine = lax.dynamic_slice(b_mod, (0, me * n_mod), (1, n_mod))

    def mod_fn(cp, w, b):
        ca = cp * jax.nn.sigmoid(cp)
        return (jnp.dot(ca, w, precision=HIGHEST, preferred_element_type=F32) + b,)

    (mod_part,) = whole_call(mod_fn, [c_pad, w_mod[0], b_mine], [((128, n_mod), F32)], name="mod_fwd")
    gmod = gather_blocks(mod_part[:N_DEV], name="gather_mod")
    mod = lax.dynamic_index_in_dim(gmod, me, axis=1, keepdims=False).reshape(N_MOD, D)
    sh1, sc1, gt1, sh2, sc2, gt2 = [mod[i:i + 1] for i in range(N_MOD)]

    (h,) = rowwise_call(fwd_body(fn_modulate), [x2], [sc1, sh1], [(D, BF16)], [], tm=256, name="modulate1")
    proj = matmul(h, w_all, "nn", BF16, name="mm_proj", n=PROJ_W, tm=K1_TM)
    dtg = matmul(h, w_dt, "nn", F32, name="mm_dt")
    ws = gm_ws[0]
    bs3 = gm_bs[0].reshape(GM_GROUPS, Q, 1)
    sgu_rows = [(proj, D, P_U), (proj, D, P_V)]
    (ya,) = rowwise_call(fwd_body(fn_sgu), sgu_rows, [gm_norm_w, ws, bs3], [(D, BF16)], [],
                         tm=256, name="sgu_fwd")
    xact = conv_fwd(proj, conv_w_full, conv_b)
    dtb4 = jnp.pad(dt_bias, ((0, 0), (0, 96)))
    alog4 = jnp.pad(a_log, ((0, 0), (0, 96)))
    dsk4 = d_skip.reshape(SSM_GROUPS, SSM_HPG, 1, 1)
    yb, states, w_gm_f, w_ssm_f, w_out_f, w_ff1_f, w_ff2_f = ssd_fwd(
        xact, proj, dtg, dtb4, alog4, dsk4, ssm_norm_w, late_weights)
    pa = matmul(ya, w_gm_f, "nn", F32, name="mm_branch_gm", tm=K1_TM)
    gate_rows = [(proj, D, P_GA), (proj, D, P_GB)]
    mixed, pb = rowwise_call(
        lambda r, fl: (fn_mix(*r) + (r[3],), ()), gate_rows + [pa], [], [(D, BF16), (D, F32)], [],
        tm=FUSED_TM, name="branch_ssm_mix", mm=(yb, w_ssm_f, "nn", 3, None), tk=SSM_INNER)
    x1, h2, o = rowwise_call(
        lambda r, fl: (fn_res_modulate(*r, *fl) + (r[1],), ()), [x2], [gt1, sc2, sh2],
        [(D, F32), (D, BF16), (D, F32)], [], tm=FUSED_TM, name="out_res_modulate2",
        mm=(mixed, w_out_f, "nn", 1, None))
    f = matmul(h2, w_ff1_f, "nn", BF16, name="mm_ff1", tm=K1_TM)

    dx1, dgf, loss_v, dgt2, dfnw = rowwise_call(
        final_body, [x1, tgt], [gt2, final_norm_w.reshape(1, D)], [(D, F32), (D, BF16)],
        [(1, 128), (1, D), (1, D)], tm=FUSED_TM, name="ff2_loss_bwd", mm=(f, w_ff2_f, "nn", 1, relu2_tile),
        tk=D_FF)
    df = matmul(dgf, w_ff2_f, "nt", BF16, name="mm_ff2_dgrad", epi=relu2_grad_tile, epi_ins=(f,), tm=K1_TM)
    gw_ff2 = matmul(f, dgf, "tn", BF16, name="mm_ff2_wgrad", tk=WGRAD_TK, a_pro=relu2_tile)
    gw_ff1 = matmul(h2, df, "tn", BF16, name="mm_ff1_wgrad", tk=WGRAD_TK)

    def res_mod_bwd(r, fl):
        xv, ov, dx1v, dh2v = r
        _, vjp = jax.vjp(fn_res_modulate, xv, ov, *fl)
        dxv, dov, dg1, dsc, dsh = vjp((dx1v, dh2v))
        return (dxv, dov), (dg1, dsc, dsh)

    dxa, do, dgt1, dsc2, dsh2 = rowwise_call(
        res_mod_bwd, [x2, o, dx1], [gt1, sc2, sh2], [(D, F32), (D, BF16)],
        [(1, D), (1, D), (1, D)], tm=FUSED_TM, name="ff1_dgrad_res_modulate2_bwd",
        mm=(df, w_ff1_f, "nt", 3, None), tk=D_FF)
    gw_out = matmul(mixed, do, "tn", BF16, name="mm_out_wgrad", tk=WGRAD_TK)
    dproj = lax.empty((T, ALL_W), BF16)

    def mix_bwd(r, fl):
        dga, dgb, dpa, dpb = bwd_body(fn_mix, 4)(r, fl)[0]
        return (jnp.concatenate([dga, dgb], axis=1), dpa, dpb), ()

    dproj, dpa, dpb = rowwise_call(
        mix_bwd, gate_rows + [pa, pb], [], [(dproj, 2 * D, P_GA), (D, BF16), (D, BF16)], [],
        tm=FUSED_TM, name="out_dgrad_mix_bwd", mm=(do, w_out_f, "nt", 4, None))
    gw_gm = matmul(ya, dpa, "tn", BF16, name="mm_branch_gm_wgrad", tk=WGRAD_TK)
    dyb = matmul(dpb, w_ssm_f, "nt", BF16, name="mm_branch_ssm_dgrad", tm=K1_TM)
    gw_ssm = matmul(yb, dpb, "tn", BF16, name="mm_branch_ssm_wgrad", tk=WGRAD_TK)

    def sgu_bwd(r, fl):
        (du, dv), acc = bwd_body(fn_sgu, 2)(r, fl)
        return (jnp.concatenate([du, dv], axis=1),), acc

    dproj, dgnw, dws, dbs = rowwise_call(
        sgu_bwd, sgu_rows, [gm_norm_w, ws, bs3], [(dproj, 2 * D, P_U)],
        [(1, D), (GM_GROUPS, Q, Q), (GM_GROUPS, Q, 1)], tm=FUSED_TM, name="branch_gm_dgrad_sgu_bwd",
        mm=(dpa, w_gm_f, "nt", 2, None))
    early_grads = [
        (gw_gm, _rows(D // N_DEV), (N_DEV, D // N_DEV, D), _slot),
        (gw_ssm, _rows(SSM_INNER // N_DEV), (N_DEV, SSM_INNER // N_DEV, D), _slot),
        (gw_out, _rows(D // N_DEV), (N_DEV, D // N_DEV, D), _slot),
        (gw_ff1, _cols(D_FF // N_DEV), (N_DEV, D, D_FF // N_DEV), _slot),
        (gw_ff2, _rows(D_FF // N_DEV), (N_DEV, D_FF // N_DEV, D), _slot),
        (_pack_rows([dgnw, dws, dbs], EARLY_ROWS), _whole, (N_DEV, sum(EARLY_ROWS), 128), _slot)]
    (dxs, dbm, dcm, dproj, ddt8, ddtb, dalog, ddsk, dsnw,
     r_gm, r_ssm, r_out, r_ff1, r_ff2, early_all) = ssd_bwd(
        xact, proj, dtg, dtb4, alog4, dsk4, ssm_norm_w, states, dyb, dproj, early_grads)
    dconv_w, dconv_b = [], []
    for nm, dact_part, col0 in (("xs", dxs, 0), ("b", dbm, SSM_INNER), ("c", dcm, SSM_INNER + SSM_GROUPS * 128)):
        dproj, dcw, dcb = conv_bwd(proj, dact_part, col0, conv_w_full, conv_b, dproj, name="conv_bwd_" + nm)
        dconv_w.append(dcw)
        dconv_b.append(dcb)
    dconv_w = jnp.concatenate(dconv_w, axis=1)
    dconv_b = jnp.concatenate(dconv_b, axis=1)
    (dproj,) = rowwise_call(
        lambda r, fl: ((functools.reduce(jnp.add, r),), ()),
        [(ddt8, 128, 128 * g) for g in range(SSM_GROUPS)], [], [(dproj, 128, PROJ_W)], [],
        tm=1024, name="ddt_into_dproj")
    gw_all = matmul(h, dproj, "tn", BF16, name="mm_in_wgrad", tn=1152, tk=WGRAD_TK)
    mid_pack = _pack_rows([dconv_w, dconv_b, jnp.sum(ddtb, axis=0), jnp.sum(dalog, axis=0), ddsk, dsnw, dfnw,
                           jnp.concatenate([dgt1, dsh2, dsc2, dgt2], axis=0)], MID_ROWS)
    dh, r_in, mid_all = matmul(
        dproj, w_all, "nt", F32, name="mm_in_dgrad", tm=2048, tk=1152,
        comm=[(_windows_of_w_all(gw_all), _slot, (N_DEV, D, WIN_W), _slot),
              (mid_pack, _whole, (N_DEV, sum(MID_ROWS), 128), _slot)])
    grad_x, dsc1, dsh1 = rowwise_call(grad_x_body, [x2, dh, dxa], [sc1, sh1], [(D, F32)],
                                      [(1, D), (1, D)], tm=256, name="modulate1_bwd")

    g_w_in = _w_in_from_window(sum_devices(r_in, tr=256, name="sum_w_in_grads"), me).reshape(1, D, n_in)

    late_all = gather_blocks(_pack_rows([dsh1, dsc1], LATE_ROWS), name="gather_dmod1")
    s_early = _unpack_rows(sum_devices(early_all, tr=early_all.shape[1], name="sum_small_early"), EARLY_ROWS)
    s_mid = _unpack_rows(sum_devices(mid_all, tr=mid_all.shape[1], name="sum_small_mid"), MID_ROWS)
    s_late = _unpack_rows(sum_devices(late_all, tr=late_all.shape[1], name="sum_small_late"), LATE_ROWS)
    g_gm_norm_w = s_early[0][:D].reshape(1, D)
    g_gm_ws = s_early[1].reshape(GM_GROUPS * Q, Q)
    g_gm_bs = s_early[2][:GM_GROUPS * Q].reshape(GM_GROUPS, Q)
    g_conv_w_full = s_mid[0].reshape(CONV_K, CONV_DIM)
    g_conv_w = lax.dynamic_slice(g_conv_w_full, (0, me * n_cv), (CONV_K, n_cv))
    g_conv_b = s_mid[1].reshape(1, CONV_DIM)
    g_dt_bias = s_mid[2][:32].reshape(1, 32)
    g_a_log = s_mid[3][:32].reshape(1, 32)
    g_d_skip = s_mid[4][:32].reshape(1, 32)
    g_ssm_norm_w = s_mid[5].reshape(1, SSM_INNER)
    g_final_norm_w = s_mid[6][:D].reshape(1, D)
    g_b_mod = jnp.concatenate([s_late[0][:D], s_late[1][:D], s_mid[7]]).reshape(1, N_MOD * D)

    dmod_all = jnp.concatenate(
        [late_all.reshape(N_DEV, -1)[:, :2 * D],
         mid_all[:, sum(MID_ROWS[:7]):].reshape(N_DEV, 4 * D)], axis=1)
    dmod_mine = _pad_rows(lax.dynamic_slice(dmod_all, (0, me * n_mod), (N_DEV, n_mod)), 128)

    def wmod_grad_fn(cp, dm):
        ca = cp * jax.nn.sigmoid(cp)
        return (lax.dot_general(ca, dm, (((0,), (0,)), ((), ())), precision=HIGHEST,
                                preferred_element_type=F32),)

    (g_w_mod,) = whole_call(wmod_grad_fn, [c_pad, dmod_mine], [((D, n_mod), F32)], name="w_mod_grad")

    upd = {}

    def step(name, w, g, m, v, parts=False):
        upd[name] = adamw(w, g if parts else g.reshape(w.shape), m, v, name="adamw_" + name)

    step("w_mod", w_mod, g_w_mod, m_w_mod, v_w_mod)
    step("b_mod", b_mod, g_b_mod, m_b_mod, v_b_mod)
    col_major = lambda a: jnp.transpose(a, (2, 0, 1))
    upd["w_in"] = tuple(jnp.transpose(o, (1, 2, 0)) for o in adamw(
        col_major(w_in), col_major(g_w_in), col_major(m_w_in), col_major(v_w_in), name="adamw_w_in"))
    step("gm_norm_w", gm_norm_w, g_gm_norm_w, m_gm_norm_w, v_gm_norm_w)
    step("gm_ws", gm_ws, g_gm_ws, m_gm_ws, v_gm_ws)
    step("gm_bs", gm_bs, g_gm_bs, m_gm_bs, v_gm_bs)
    step("conv_w", conv_w, g_conv_w, m_conv_w, v_conv_w)
    step("conv_b", conv_b, g_conv_b, m_conv_b, v_conv_b)
    step("dt_bias", dt_bias, g_dt_bias, m_dt_bias, v_dt_bias)
    step("a_log", a_log, g_a_log, m_a_log, v_a_log)
    step("d_skip", d_skip, g_d_skip, m_d_skip, v_d_skip)
    step("ssm_norm_w", ssm_norm_w, g_ssm_norm_w, m_ssm_norm_w, v_ssm_norm_w)
    step("w_branch_gm", w_branch_gm, r_gm, m_w_branch_gm, v_w_branch_gm, parts=True)
    step("w_branch_ssm", w_branch_ssm, r_ssm, m_w_branch_ssm, v_w_branch_ssm, parts=True)
    step("w_out", w_out, r_out, m_w_out, v_w_out, parts=True)
    step("w_ff1", w_ff1, r_ff1, m_w_ff1, v_w_ff1, parts=True)
    step("w_ff2", w_ff2, r_ff2, m_w_ff2, v_w_ff2, parts=True)
    step("final_norm_w", final_norm_w.reshape(1, D), g_final_norm_w, m_final_norm_w.reshape(1, D),
         v_final_norm_w.reshape(1, D))
    upd["final_norm_w"] = tuple(a.reshape(D) for a in upd["final_norm_w"])

    loss = lax.psum(loss_v[0, 0], ("x", "y", "c"))
    order = ["w_mod", "b_mod", "w_in", "gm_norm_w", "gm_ws", "gm_bs", "conv_w", "conv_b", "dt_bias", "a_log",
             "d_skip", "ssm_norm_w", "w_branch_gm", "w_branch_ssm", "w_out", "w_ff1", "w_ff2", "final_norm_w"]
    return (loss, grad_x.reshape(1, T, D),
            *[upd[n][0] for n in order], *[upd[n][1] for n in order],
            *[upd[n][2] for n in order], *[upd[n][3] for n in order])
```

```python
import functools

import jax
import jax.numpy as jnp
from jax import lax
from jax.experimental import pallas as pl
from jax.experimental.pallas import tpu as pltpu

F32 = jnp.float32
BF16 = jnp.bfloat16
MESH = pl.DeviceIdType.MESH
HIGHEST = lax.Precision.HIGHEST

N_DEV = 8
D = 1024
Q = 128
GM_GROUPS = 8
SSM_INNER = 2048
SSM_GROUPS = 8
SSM_HPG = 4
SSM_P = 64
SSM_GW = SSM_HPG * SSM_P
CONV_DIM = 4096
CONV_K = 4
D_FF = 4096
N_MOD = 6
EPS = 1e-6
IN_WIDTH = 10272
OFF_DT = 8192
OFF_GA = 8224
PROJ_W = 10240
ALL_W = 10368
P_U, P_V, P_Z, P_XBC, P_GA, P_GB = 0, 1024, 2048, 4096, 8192, 9216

ADAM_LR = 0.001
ADAM_B1 = 0.9
ADAM_B2 = 0.999
ADAM_EPS = 1e-08
ADAM_WD = 0.01
ADAM_STEP = 10

VMEM_LIMIT_BYTES = 48 * 1024 * 1024
K1_TM = 2048
FUSED_TM = 512
WGRAD_TK = 2048
EARLY_ROWS = (8, 1024, 8)
MID_ROWS = (128, 32, 8, 8, 8, 16, 8, 32)
LATE_ROWS = (8, 8, 8)


def _pack_rows(arrs, rows):
    def rows128(a, r):
        a = a.reshape(-1)
        return jnp.pad(a, (0, r * 128 - a.shape[0])).reshape(r, 128)
    return jnp.concatenate([rows128(a, r) for a, r in zip(arrs, rows)], axis=0)


def _unpack_rows(s, rows):
    out, o = [], 0
    for r in rows:
        out.append(s[o:o + r].reshape(-1))
        o += r
    return out


def _params(sem=None):
    return pltpu.CompilerParams(dimension_semantics=sem, vmem_limit_bytes=VMEM_LIMIT_BYTES)


def _dg(a, b, ca, cb):
    return lax.dot_general(a.astype(BF16), b.astype(BF16), (((ca,), (cb,)), ((), ())),
                           preferred_element_type=F32)


@jax.custom_vjp
def dot_nn(a, b):
    return _dg(a, b, 1, 0)


@jax.custom_vjp
def dot_nt(a, b):
    return _dg(a, b, 1, 1)


@jax.custom_vjp
def dot_tn(a, b):
    return _dg(a, b, 0, 0)


def _like(ct, primal):
    return ct.astype(primal.dtype)


dot_nn.defvjp(lambda a, b: (dot_nn(a, b), (a, b)),
              lambda r, g: (_like(dot_nt(g, r[1]), r[0]), _like(dot_tn(r[0], g), r[1])))
dot_nt.defvjp(lambda a, b: (dot_nt(a, b), (a, b)),
              lambda r, g: (_like(dot_nn(g, r[1]), r[0]), _like(dot_tn(g, r[0]), r[1])))
dot_tn.defvjp(lambda a, b: (dot_tn(a, b), (a, b)),
              lambda r, g: (_like(dot_nt(r[1], g), r[0]), _like(dot_nn(r[0], g), r[1])))


def _rms(x):
    return x * lax.rsqrt(jnp.mean(x * x, axis=-1, keepdims=True) + EPS)


def _softplus(x):
    return jnp.maximum(x, 0.0) + jnp.log1p(jnp.exp(-jnp.abs(x)))


def _rows(n):
    return lambda ref, j: ref.at[pl.ds(pl.multiple_of(j * n, n), n)]


def _cols(n):
    return lambda ref, j: ref.at[:, pl.ds(pl.multiple_of(j * n, n), n)]


def _slot(ref, j):
    return ref.at[j]


def _whole(ref, j):
    return ref


def exchange(items, *, name):
    n = len(items)

    def body(*refs):
        exchange_in_body(items, refs[:n], refs[n:2 * n], refs[2 * n:], True, True)

    return pl.pallas_call(
        body, name=name,
        out_shape=exchange_out_shapes(items),
        in_specs=[pl.BlockSpec(memory_space=pl.ANY)] * n,
        out_specs=[pl.BlockSpec(memory_space=pl.ANY)] * n,
        scratch_shapes=exchange_semaphores(items),
    )(*[it[0] for it in items])


def exchange_out_shapes(items):
    return [jax.ShapeDtypeStruct(tuple(shape), src.dtype) for (src, _, shape, _) in items]


def exchange_semaphores(items):
    n = len(items)
    return [pltpu.SemaphoreType.DMA((n, N_DEV - 1)), pltpu.SemaphoreType.DMA((n, N_DEV - 1)),
            pltpu.SemaphoreType.DMA((n,))]


def _exchange_copies(items, src_refs, out_refs, sems):
    send_sems, recv_sems, local_sems = sems
    x = lax.axis_index("x")
    y = lax.axis_index("y")
    c = lax.axis_index("c")
    me = 4 * x + 2 * y + c
    local = [pltpu.make_async_copy(src_win(src_refs[i], me), dst_win(out_refs[i], me), local_sems.at[i])
             for i, (_, src_win, _, dst_win) in enumerate(items)]
    remote = []
    for i, (_, src_win, _, dst_win) in enumerate(items):
        for k in range(1, N_DEV):
            px = lax.rem(x + ((k >> 2) & 1), 2)
            py = lax.rem(y + ((k >> 1) & 1), 2)
            pc = lax.rem(c + (k & 1), 2)
            peer = 4 * px + 2 * py + pc
            remote.append(pltpu.make_async_remote_copy(
                src_ref=src_win(src_refs[i], peer), dst_ref=dst_win(out_refs[i], me),
                send_sem=send_sems.at[i, k - 1], recv_sem=recv_sems.at[i, k - 1],
                device_id=(px, py, pc), device_id_type=MESH))
    return local, remote


def _when(cond, fn):
    if cond is True:
        fn()
    else:
        pl.when(cond)(fn)


def exchange_start(items, src_refs, out_refs, sems, cond):
    def start():
        local, remote = _exchange_copies(items, src_refs, out_refs, sems)
        for cp in local + remote:
            cp.start()
    _when(cond, start)


def exchange_finish(items, src_refs, out_refs, sems, cond):
    def finish():
        local, remote = _exchange_copies(items, src_refs, out_refs, sems)
        for cp in remote:
            cp.wait_send()
        for cp in remote:
            cp.wait_recv()
        for cp in local:
            cp.wait()
    _when(cond, finish)


def exchange_in_body(items, src_refs, out_refs, sems, first, last):
    exchange_start(items, src_refs, out_refs, sems, first)
    exchange_finish(items, src_refs, out_refs, sems, last)


def gather_blocks(src, *, name):
    return exchange([(src, _whole, (N_DEV,) + src.shape, _slot)], name=name)[0]


def _two_level_gather(src_ref, out_ref, send_sems, recv_sems, local_sem):
    x = lax.axis_index("x")
    y = lax.axis_index("y")
    c = lax.axis_index("c")
    me, sibling = (x, y, c), (x, y, 1 - c)
    chips = [(1 - x, y), (x, 1 - y), (1 - x, 1 - y)]

    def slot(px, py, pc):
        return out_ref.at[4 * px + 2 * py + pc]

    def copy(k, block, to, src=None):
        return pltpu.make_async_remote_copy(
            src_ref=slot(*block) if src is None else src, dst_ref=slot(*block),
            send_sem=send_sems.at[k], recv_sem=recv_sems.at[k], device_id=to, device_id_type=MESH)

    def own_copies():
        return ([copy(0, me, sibling, src=src_ref)]
                + [copy(1 + j, me, (*chip, c), src=src_ref) for j, chip in enumerate(chips)])

    def start():
        pltpu.make_async_copy(src_ref, slot(*me), local_sem).start()
        for cp in own_copies():
            cp.start()

    def finish():
        passed = [copy(4 + j, (*chip, c), sibling) for j, chip in enumerate(chips)]
        for j, chip in enumerate(chips):
            copy(1 + j, (*chip, c), me).wait_recv()
            passed[j].start()
        copy(0, sibling, me).wait_recv()
        for j, chip in enumerate(chips):
            copy(4 + j, (*chip, 1 - c), me).wait_recv()
        for cp in own_copies() + passed:
            cp.wait_send()
        pltpu.make_async_copy(src_ref, slot(*me), local_sem).wait()

    return start, finish


_TWO_LEVEL_SEMS = [pltpu.SemaphoreType.DMA((N_DEV - 1,)), pltpu.SemaphoreType.DMA((N_DEV - 1,)),
                   pltpu.SemaphoreType.DMA(())]


def modulate_with_gather(x2, sc, sh, src, *, tm, name):
    T, width = x2.shape
    n = T // tm

    def body(x_ref, sc_ref, sh_ref, src_ref, h_ref, out_ref, *sems):
        start, finish = _two_level_gather(src_ref, out_ref, *sems)
        pl.when(pl.program_id(0) == 0)(start)
        (h,) = fn_modulate(x_ref[...], sc_ref[...], sh_ref[...])
        h_ref[...] = h.astype(h_ref.dtype)
        pl.when(pl.program_id(0) == n - 1)(finish)

    row = pl.BlockSpec((tm, width), lambda i: (i, 0))
    full = pl.BlockSpec((1, width), lambda i: (0, 0))
    any_spec = pl.BlockSpec(memory_space=pl.ANY)
    return pl.pallas_call(
        body, name=name, grid=(n,),
        out_shape=[jax.ShapeDtypeStruct((T, width), BF16),
                   jax.ShapeDtypeStruct((N_DEV,) + src.shape, src.dtype)],
        in_specs=[row, full, full, any_spec], out_specs=[row, any_spec],
        scratch_shapes=_TWO_LEVEL_SEMS,
        compiler_params=_params(("arbitrary",)),
    )(x2, sc, sh, src)


def sum_devices(g, *, tr, name):
    _, R, C = g.shape

    def body(g_ref, o_ref):
        acc = g_ref[0].astype(F32)
        for j in range(1, N_DEV):
            acc = acc + g_ref[j].astype(F32)
        o_ref[...] = acc

    return pl.pallas_call(
        body, name=name, grid=(R // tr,),
        out_shape=jax.ShapeDtypeStruct((R, C), F32),
        in_specs=[pl.BlockSpec((N_DEV, tr, C), lambda i: (0, i, 0))],
        out_specs=pl.BlockSpec((tr, C), lambda i: (i, 0)),
        compiler_params=_params(("parallel",)),
    )(g)


def matmul(a, b, mode, out_dtype, *, name, tm=1024, tn=1024, tk=1024, n=None, comm=None,
           a_pro=None, epi=None, epi_ins=()):
    if mode == "nn":
        (M, K), (K2, N) = a.shape, b.shape
    elif mode == "nt":
        (M, K), (N, K2) = a.shape, b.shape
    else:
        (K, M), (K2, N) = a.shape, b.shape
    assert K == K2
    N = N if n is None else n
    tm, tn, tk = min(tm, M), min(tn, N), min(tk, K)
    assert M % tm == 0 and N % tn == 0 and K % tk == 0, (name, M, N, K, tm, tn, tk)
    nk = K // tk
    if mode == "tn":
        a_spec = pl.BlockSpec((tk, tm), lambda i, j, k: (k, i))
    else:
        a_spec = pl.BlockSpec((tm, tk), lambda i, j, k: (i, k))
    if mode == "nt":
        b_spec = pl.BlockSpec((tn, tk), lambda i, j, k: (j, k))
    else:
        b_spec = pl.BlockSpec((tk, tn), lambda i, j, k: (k, j))
    dims = {"nn": (1, 0), "nt": (1, 1), "tn": (0, 0)}[mode]
    items = list(comm) if comm else []
    nx = len(items)
    ne = len(epi_ins)
    gm, gn = M // tm, N // tn
    any_spec = pl.BlockSpec(memory_space=pl.ANY)
    o_spec = pl.BlockSpec((tm, tn), lambda i, j, k: (i, j))

    def body(*refs):
        a_ref, b_ref, e_refs = refs[0], refs[1], refs[2:2 + ne]
        refs = refs[2 + ne:]
        src_refs, o_ref, out_refs = refs[:nx], refs[nx], refs[1 + nx:1 + 2 * nx]
        acc_ref, sems = refs[1 + 2 * nx], refs[2 + 2 * nx:]
        i, j, k = pl.program_id(0), pl.program_id(1), pl.program_id(2)
        if items:
            exchange_start(items, src_refs, out_refs, sems, (i == 0) & (j == 0) & (k == 0))
        a_tile = a_ref[...] if a_pro is None else a_pro(a_ref[...])
        part = lax.dot_general(a_tile, b_ref[...], (((dims[0],), (dims[1],)), ((), ())),
                               preferred_element_type=F32)

        def finish(acc):
            if epi is not None:
                acc = epi(acc, *[e[...] for e in e_refs])
            o_ref[...] = acc.astype(o_ref.dtype)

        if nk == 1:
            finish(part)
        else:
            @pl.when(k == 0)
            def _():
                acc_ref[...] = part

            @pl.when((k > 0) & (k < nk - 1))
            def _():
                acc_ref[...] += part

            @pl.when(k == nk - 1)
            def _():
                finish(acc_ref[...] + part)

        if items:
            exchange_finish(items, src_refs, out_refs, sems, (i == gm - 1) & (j == gn - 1) & (k == nk - 1))

    res = pl.pallas_call(
        body, name=name, grid=(gm, gn, nk),
        out_shape=[jax.ShapeDtypeStruct((M, N), out_dtype)] + exchange_out_shapes(items),
        in_specs=[a_spec, b_spec] + [o_spec] * ne + [any_spec] * nx,
        out_specs=[o_spec] + [any_spec] * nx,
        scratch_shapes=[pltpu.VMEM((tm, tn) if nk > 1 else (8, 128), F32)]
        + (exchange_semaphores(items) if items else []),
        compiler_params=_params(("arbitrary",) * 3 if items else ("parallel", "parallel", "arbitrary")),
    )(a, b, *epi_ins, *[it[0] for it in items])
    return res if items else res[0]


def rowwise_call(body_fn, rows, fulls, row_outs, acc_outs, *, tm, name, mm=None, tk=1024):
    rows = [r if isinstance(r, tuple) else (r, r.shape[1], 0) for r in rows]
    T = rows[0][0].shape[0]
    tm = min(tm, T)
    assert T % tm == 0
    n_r, n_f, n_ro = len(rows), len(fulls), len(row_outs)
    into = [(k, ro) for k, ro in enumerate(row_outs) if len(ro) == 3]
    n_b = len(into)
    n_mm, nk = 0, 1
    if mm is not None:
        a, b, mode, pos, a_pro = mm
        n_mm = 2
        K = a.shape[1]
        N = b.shape[1] if mode == "nn" else b.shape[0]
        tk = min(tk, K)
        assert K % tk == 0 and a.shape[0] == T
        nk = K // tk
        b_contract = 0 if mode == "nn" else 1

    def row_body(refs, product):
        r_refs = refs[:n_r]
        f_refs = refs[n_r:n_r + n_f]
        refs = refs[n_r + n_f + n_b:]
        ro_refs = refs[:n_ro]
        ao_refs = refs[n_ro:n_ro + len(acc_outs)]
        r_vals = [r[...].astype(F32) for r in r_refs]
        if product is not None:
            r_vals.insert(pos, product)
        f_vals = [f[...].astype(F32) for f in f_refs]
        ro, ao = body_fn(r_vals, f_vals)
        for ref, v in zip(ro_refs, ro):
            ref[...] = v.astype(ref.dtype)
        if ao_refs:
            @pl.when(pl.program_id(0) == 0)
            def _():
                for ref in ao_refs:
                    ref[...] = jnp.zeros(ref.shape, F32)
            for ref, v in zip(ao_refs, ao):
                ref[...] += v.reshape(ref.shape)

    def body(*refs):
        if mm is None:
            return row_body(refs, None)
        a_ref, b_ref, rest, acc_ref = refs[0], refs[1], refs[2:-1], refs[-1]
        k = pl.program_id(1)
        a_tile = a_ref[...] if a_pro is None else a_pro(a_ref[...])
        part = lax.dot_general(a_tile, b_ref[...], (((1,), (b_contract,)), ((), ())),
                               preferred_element_type=F32)
        if nk == 1:
            return row_body(rest, part)

        @pl.when(k == 0)
        def _():
            acc_ref[...] = part

        @pl.when((k > 0) & (k < nk - 1))
        def _():
            acc_ref[...] += part

        @pl.when(k == nk - 1)
        def _():
            row_body(rest, acc_ref[...] + part)

    def full_spec(shape):
        nd = len(shape)
        return pl.BlockSpec(tuple(shape), lambda i, *_: (0,) * nd)

    def row_spec(w, off):
        return pl.BlockSpec((tm, w), functools.partial(lambda i, *_, o: (i, o), o=off // w))

    in_specs = []
    if mm is not None:
        in_specs.append(pl.BlockSpec((tm, tk), lambda i, k: (i, k)))
        in_specs.append(pl.BlockSpec((tk, N), lambda i, k: (k, 0)) if mode == "nn" else
                        pl.BlockSpec((N, tk), lambda i, k: (0, k)))
    in_specs += [row_spec(w, off) for (_, w, off) in rows]
    in_specs += [full_spec(f.shape) for f in fulls]
    in_specs += [pl.BlockSpec(memory_space=pl.ANY)] * n_b
    out_specs, out_shape = [], []
    for ro in row_outs:
        if len(ro) == 3:
            buf, w, off = ro
            out_specs.append(row_spec(w, off))
            out_shape.append(jax.ShapeDtypeStruct(buf.shape, buf.dtype))
        else:
            w, dt = ro
            out_specs.append(row_spec(w, 0))
            out_shape.append(jax.ShapeDtypeStruct((T, w), dt))
    out_specs += [full_spec(s) for s in acc_outs]
    out_shape += [jax.ShapeDtypeStruct(tuple(s), F32) for s in acc_outs]
    aliases = {n_mm + n_r + n_f + b: k for b, (k, _) in enumerate(into)}
    return pl.pallas_call(
        body, name=name, grid=(T // tm,) if mm is None else (T // tm, nk),
        out_shape=out_shape, in_specs=in_specs, out_specs=out_specs,
        scratch_shapes=[] if mm is None else [pltpu.VMEM((tm, N) if nk > 1 else (8, 128), F32)],
        input_output_aliases=aliases,
        compiler_params=_params(("arbitrary",) if mm is None else ("arbitrary", "arbitrary")),
    )(*([] if mm is None else [a, b]), *[r[0] for r in rows], *fulls, *[ro[0] for _, ro in into])


def fwd_body(fn):
    return lambda r, f: (fn(*r, *f), ())


def bwd_body(fn, n_rows):
    def body(r, f):
        ins, cots = r[:n_rows], r[n_rows:]
        _, vjp = jax.vjp(fn, *ins, *f)
        g = vjp(tuple(cots))
        return g[:n_rows], g[n_rows:]
    return body


def whole_call(fn, ins, outs, *, name):
    n_in = len(ins)

    def body(*refs):
        res = fn(*[r[...] for r in refs[:n_in]])
        for ref, v in zip(refs[n_in:], res):
            ref[...] = v.astype(ref.dtype)

    return pl.pallas_call(
        body, name=name,
        out_shape=[jax.ShapeDtypeStruct(tuple(s), dt) for (s, dt) in outs],
        compiler_params=_params(),
    )(*ins)


def fn_modulate(x, sc, sh):
    return (_rms(x) * (1.0 + sc) + sh,)


def fn_sgu(u, v, nw, ws, bs):
    ug = jax.nn.gelu(u)
    vn = _rms(jax.nn.gelu(v)) * nw
    ri = lax.broadcasted_iota(jnp.int32, (Q, Q), 0)
    ci = lax.broadcasted_iota(jnp.int32, (Q, Q), 1)
    causal = ri >= ci
    chunks = []
    for n in range(u.shape[0] // Q):
        vc = vn[n * Q:(n + 1) * Q]
        cols = [dot_nn(jnp.where(causal, ws[g], 0.0), vc[:, g * Q:(g + 1) * Q]) + bs[g]
                for g in range(GM_GROUPS)]
        chunks.append(jnp.concatenate(cols, axis=1))
    sv = chunks[0] if len(chunks) == 1 else jnp.concatenate(chunks, axis=0)
    return (ug * sv,)


def fn_mix(ga, gb, pa, pb):
    return (jax.nn.sigmoid(ga) * pa + jax.nn.sigmoid(gb) * pb,)


def fn_res_modulate(x, o, g1, sc2, sh2):
    x1 = x + g1 * o
    return x1, _rms(x1) * (1.0 + sc2) + sh2


def relu2_tile(f):
    return jnp.square(jnp.maximum(f.astype(F32), 0.0)).astype(BF16)


def relu2_grad_tile(dact, f):
    return dact * (2.0 * jnp.maximum(f.astype(F32), 0.0))


def final_body(r, f):
    x1, gf, tgt = r
    g2, fnw = f

    def loss_fn(x1, gf, g2, fnw):
        y = _rms(x1 + g2 * gf) * fnw
        row = 0.5 * jnp.mean(jnp.square(y - tgt), axis=-1, keepdims=True)
        return jnp.sum(row, axis=0, keepdims=True)

    l, vjp = jax.vjp(loss_fn, x1, gf, g2, fnw)
    dx1, dgf, dg2, dfnw = vjp(jnp.ones((1, 1), F32))
    return (dx1, dgf), (jnp.broadcast_to(l, (1, 128)), dg2, dfnw)


def grad_x_body(r, f):
    x, dh, dxa = r
    _, vjp = jax.vjp(fn_modulate, x, *f)
    dx, dsc, dsh = vjp((dh,))
    return (dx + dxa,), (dsc, dsh)


CONV_CW = 128
CONV_PAD = 8
CONV_ROWS = 128


def _conv_pre(xp, w_ref, b_ref, r0, R):
    acc = b_ref[...] + w_ref[0:1, :] * xp[r0 + CONV_PAD - 3:r0 + CONV_PAD - 3 + R, :]
    for k in range(1, CONV_K):
        s = r0 + CONV_PAD - 3 + k
        acc = acc + w_ref[k:k + 1, :] * xp[s:s + R, :]
    return acc


def conv_fwd(proj, conv_w, conv_b):
    T = proj.shape[0]
    R = min(CONV_ROWS, T)

    def body(x_ref, w_ref, b_ref, o_ref, xp):
        xp[0:CONV_PAD, :] = jnp.zeros((CONV_PAD, CONV_CW), F32)
        xp[CONV_PAD:CONV_PAD + T, :] = x_ref[...].astype(F32)
        for r0 in range(0, T, R):
            pre = _conv_pre(xp, w_ref, b_ref, r0, R)
            o_ref[r0:r0 + R, :] = (pre * jax.nn.sigmoid(pre)).astype(o_ref.dtype)

    return pl.pallas_call(
        body, name="conv_fwd", grid=(CONV_DIM // CONV_CW,),
        out_shape=jax.ShapeDtypeStruct((T, CONV_DIM), BF16),
        in_specs=[pl.BlockSpec((T, CONV_CW), lambda j: (0, P_XBC // CONV_CW + j)),
                  pl.BlockSpec((CONV_K, CONV_CW), lambda j: (0, j)),
                  pl.BlockSpec((1, CONV_CW), lambda j: (0, j))],
        out_specs=pl.BlockSpec((T, CONV_CW), lambda j: (0, j)),
        scratch_shapes=[pltpu.VMEM((T + CONV_PAD, CONV_CW), F32)],
        compiler_params=_params(("parallel",)),
    )(proj, conv_w, conv_b)


def conv_bwd(proj, dact, col0, conv_w, conv_b, dproj, *, name):
    T = proj.shape[0]
    R = min(CONV_ROWS, T)
    nb = dact.shape[1] // CONV_CW
    c0 = col0 // CONV_CW
    x0 = (P_XBC + col0) // CONV_CW

    def body(x_ref, d_ref, w_ref, b_ref, _, dx_ref, dw_ref, db_ref, xp, dp):
        xp[0:CONV_PAD, :] = jnp.zeros((CONV_PAD, CONV_CW), F32)
        xp[CONV_PAD:CONV_PAD + T, :] = x_ref[...].astype(F32)
        dp[T:T + CONV_PAD, :] = jnp.zeros((CONV_PAD, CONV_CW), F32)
        dws = [jnp.zeros((1, CONV_CW), F32) for _ in range(CONV_K)]
        db = jnp.zeros((1, CONV_CW), F32)
        for r0 in range(0, T, R):
            pre = _conv_pre(xp, w_ref, b_ref, r0, R)
            s = jax.nn.sigmoid(pre)
            dpre = d_ref[r0:r0 + R, :].astype(F32) * (s * (1.0 + pre * (1.0 - s)))
            dp[r0:r0 + R, :] = dpre
            db = db + jnp.sum(dpre, axis=0, keepdims=True)
            for k in range(CONV_K):
                st = r0 + CONV_PAD - 3 + k
                dws[k] = dws[k] + jnp.sum(dpre * xp[st:st + R, :], axis=0, keepdims=True)
        for r0 in range(0, T, R):
            acc = w_ref[0:1, :] * dp[r0 + 3:r0 + 3 + R, :]
            for k in range(1, CONV_K):
                acc = acc + w_ref[k:k + 1, :] * dp[r0 + 3 - k:r0 + 3 - k + R, :]
            dx_ref[r0:r0 + R, :] = acc.astype(dx_ref.dtype)
        for k in range(CONV_K):
            dw_ref[k:k + 1, :] = dws[k]
        db_ref[...] = db

    return pl.pallas_call(
        body, name=name, grid=(nb,),
        out_shape=[jax.ShapeDtypeStruct(dproj.shape, dproj.dtype),
                   jax.ShapeDtypeStruct((CONV_K, nb * CONV_CW), F32),
                   jax.ShapeDtypeStruct((1, nb * CONV_CW), F32)],
        in_specs=[pl.BlockSpec((T, CONV_CW), lambda j: (0, x0 + j)),
                  pl.BlockSpec((T, CONV_CW), lambda j: (0, j)),
                  pl.BlockSpec((CONV_K, CONV_CW), lambda j: (0, c0 + j)),
                  pl.BlockSpec((1, CONV_CW), lambda j: (0, c0 + j)),
                  pl.BlockSpec(memory_space=pl.ANY)],
        out_specs=[pl.BlockSpec((T, CONV_CW), lambda j: (0, x0 + j)),
                   pl.BlockSpec((CONV_K, CONV_CW), lambda j: (0, j)),
                   pl.BlockSpec((1, CONV_CW), lambda j: (0, j))],
        scratch_shapes=[pltpu.VMEM((T + CONV_PAD, CONV_CW), F32),
                        pltpu.VMEM((T + CONV_PAD, CONV_CW), F32)],
        input_output_aliases={4: 0},
        compiler_params=_params(("parallel",)),
    )(proj, dact, conv_w, conv_b, dproj)


def _split3(a):
    hi = a.astype(BF16)
    r = a - hi.astype(F32)
    mid = r.astype(BF16)
    return hi, mid, (r - mid.astype(F32)).astype(BF16)


def _dg3(a, m, ca, cm, a_first):
    dims = (((ca,), (cm,)), ((), ())) if a_first else (((cm,), (ca,)), ((), ()))
    out = None
    for p in _split3(a):
        t = lax.dot_general(p, m, dims, preferred_element_type=F32) if a_first else \
            lax.dot_general(m, p, dims, preferred_element_type=F32)
        out = t if out is None else out + t
    return out


@jax.custom_vjp
def exact_right(a, m):
    return _dg3(a, m, 1, 0, True)


@jax.custom_vjp
def exact_left(m, a):
    return _dg3(a, m, 0, 1, False)


def _expand_bwd(m, g):
    hi = g.astype(BF16)
    lo = (g - hi.astype(F32)).astype(BF16)
    out = lax.dot_general(jnp.concatenate([hi, lo], axis=1), jnp.concatenate([m, m], axis=1),
                          (((1,), (1,)), ((), ())), preferred_element_type=F32)
    return out, jnp.zeros_like(m)


exact_right.defvjp(lambda a, m: (exact_right(a, m), m), _expand_bwd)
exact_left.defvjp(lambda m, a: (exact_left(m, a), m),
                  lambda m, g: (jnp.zeros_like(m), _dg3(g, m, 0, 0, False)))


def ssd_step(lane0, state, x, z, dtr, Bm, Cm, dtb, alog, dsk, nw):
    def iota(shape, dim):
        return lax.broadcasted_iota(jnp.int32, shape, dim)

    def one_hot(mask):
        return mask.astype(F32).astype(BF16)

    causal = iota((Q, Q), 0) >= iota((Q, Q), 1)
    eye = iota((Q, Q), 0) == iota((Q, Q), 1)
    lane = iota((1, 128), 1)
    colh = lax.shift_right_logical(iota((1, SSM_GW), 1), 6)
    to_cols = one_hot(iota((128, SSM_GW), 0) == lane0 + colh)

    dt_all = _softplus(dtr + dtb)
    a_all = dt_all * (-jnp.exp(alog))
    cum_all = exact_left(one_hot(causal), a_all)
    both = exact_right(jnp.concatenate([dt_all, cum_all], axis=0), to_cols)
    dt_f, cum_f = both[:Q], both[Q:]
    last_f = jnp.sum(jnp.where(iota((Q, 1), 0) == Q - 1, cum_f, 0.0), axis=0, keepdims=True)
    dsk_f = jnp.zeros((1, SSM_GW), F32)
    for h in range(SSM_HPG):
        dsk_f = jnp.where(colh == h, dsk[h], dsk_f)

    xdt = x * dt_f
    cb = dot_nt(Cm, Bm)
    ms = []
    for h in range(SSM_HPG):
        ch = jnp.sum(jnp.where(lane == lane0 + h, cum_all, 0.0), axis=1, keepdims=True)
        ch_t = jnp.sum(jnp.where(eye, ch, 0.0), axis=0, keepdims=True)
        ms.append(cb * jnp.exp(jnp.where(causal, ch - ch_t, -1e30)))
    first_half = lane < SSM_P
    blocks = []
    for b in range(SSM_HPG // 2):
        xb = xdt[:, b * 128:(b + 1) * 128]
        rhs = jnp.concatenate([jnp.where(first_half, xb, 0.0), jnp.where(first_half, 0.0, xb)], axis=0)
        blocks.append(dot_nn(jnp.concatenate(ms[2 * b:2 * b + 2], axis=1), rhs))
    y = jnp.concatenate(blocks, axis=1)
    y = y + dot_nn(Cm, state) * jnp.exp(cum_f) + x * dsk_f
    new_state = state * jnp.exp(last_f) + dot_tn(Bm, xdt * jnp.exp(last_f - cum_f))
    gated = y * (z * jax.nn.sigmoid(z))
    return new_state, _rms(gated) * nw


SSD_GPS = 4
_XW = SSD_GPS * SSM_GW
_BW = SSD_GPS * 128


def _ssd_in_specs(rev, nc):
    def n_of(n):
        return nc - 1 - n if rev else n
    return [
        pl.BlockSpec((Q, _XW), lambda g, n: (n_of(n), g)),
        pl.BlockSpec((Q, _BW), lambda g, n: (n_of(n), SSM_INNER // _BW + g)),
        pl.BlockSpec((Q, _BW), lambda g, n: (n_of(n), (SSM_INNER + SSM_GROUPS * 128) // _BW + g)),
        pl.BlockSpec((Q, _XW), lambda g, n: (n_of(n), P_Z // _XW + g)),
        pl.BlockSpec((Q, 128), lambda g, n: (n_of(n), 0)),
        pl.BlockSpec((1, 128), lambda g, n: (0, 0)),
        pl.BlockSpec((1, 128), lambda g, n: (0, 0)),
        pl.BlockSpec((SSD_GPS, SSM_HPG, 1, 1), lambda g, n: (g, 0, 0, 0)),
        pl.BlockSpec((1, _XW), lambda g, n: (0, g)),
    ]


def _ssd_group_inputs(gi, x_ref, b_ref, c_ref, z_ref, dt_ref, dtb_ref, al_ref, dk_ref, nw_ref):
    xs = slice(gi * SSM_GW, (gi + 1) * SSM_GW)
    bs = slice(gi * 128, (gi + 1) * 128)
    return (x_ref[:, xs].astype(F32), z_ref[:, xs].astype(F32), dt_ref[...],
            b_ref[:, bs], c_ref[:, bs],
            dtb_ref[...], al_ref[...], dk_ref[gi], nw_ref[:, xs])


def ssd_fwd(xact, proj, dtg, dtb, alog, dsk, nw, comm):
    T = xact.shape[0]
    nc = T // Q
    nx = len(comm)
    ng = SSM_GROUPS // SSD_GPS
    any_spec = pl.BlockSpec(memory_space=pl.ANY)

    def body(*refs):
        in_refs, src_refs = refs[:9], refs[9:9 + nx]
        yb_ref, st_ref = refs[9 + nx:11 + nx]
        out_refs, state, sems = refs[11 + nx:11 + 2 * nx], refs[11 + 2 * nx], refs[12 + 2 * nx:]
        g, n = pl.program_id(0), pl.program_id(1)
        exchange_start(comm, src_refs, out_refs, sems, (g == 0) & (n == 0))

        @pl.when(n == 0)
        def _():
            state[...] = jnp.zeros(state.shape, F32)

        for gi in range(SSD_GPS):
            lane0 = SSM_HPG * (SSD_GPS * pl.program_id(0) + gi)
            s = state[gi]
            st_ref[gi, 0] = s
            new_s, yb = ssd_step(lane0, s, *_ssd_group_inputs(gi, *in_refs))
            state[gi] = new_s
            yb_ref[:, gi * SSM_GW:(gi + 1) * SSM_GW] = yb.astype(yb_ref.dtype)

        exchange_finish(comm, src_refs, out_refs, sems, (g == ng - 1) & (n == nc - 1))

    return pl.pallas_call(
        body, name="ssd_fwd", grid=(ng, nc),
        out_shape=[jax.ShapeDtypeStruct((T, SSM_INNER), BF16),
                   jax.ShapeDtypeStruct((SSM_GROUPS, nc, 128, SSM_GW), F32)] + exchange_out_shapes(comm),
        in_specs=_ssd_in_specs(False, nc) + [any_spec] * nx,
        out_specs=[pl.BlockSpec((Q, _XW), lambda g, n: (n, g)),
                   pl.BlockSpec((SSD_GPS, 1, 128, SSM_GW), lambda g, n: (g, n, 0, 0))] + [any_spec] * nx,
        scratch_shapes=[pltpu.VMEM((SSD_GPS, 128, SSM_GW), F32)] + exchange_semaphores(comm),
        compiler_params=_params(("arbitrary", "arbitrary")),
    )(xact, xact, xact, proj, dtg, dtb, alog, dsk, nw, *[it[0] for it in comm])


def ssd_bwd(xact, proj, dtg, dtb, alog, dsk, nw, states, dyb, dproj, comm):
    T = xact.shape[0]
    nc = T // Q

    nx = len(comm)
    ng = SSM_GROUPS // SSD_GPS
    any_spec = pl.BlockSpec(memory_space=pl.ANY)

    def body(*refs):
        in_refs, (st_ref, dy_ref, _) = refs[:9], refs[9:12]
        src_refs, refs = refs[12:12 + nx], refs[12 + nx:]
        dx_ref, db_ref, dc_ref, dz_ref, ddt_ref, ddtb_ref, dal_ref, ddk_ref, dnw_ref = refs[:9]
        out_refs, dstate, sems = refs[9:9 + nx], refs[9 + nx], refs[10 + nx:]
        exchange_start(comm, src_refs, out_refs, sems, (pl.program_id(0) == 0) & (pl.program_id(1) == 0))

        @pl.when(pl.program_id(1) == 0)
        def _():
            dstate[...] = jnp.zeros(dstate.shape, F32)
            ddtb_ref[...] = jnp.zeros(ddtb_ref.shape, F32)
            dal_ref[...] = jnp.zeros(dal_ref.shape, F32)
            ddk_ref[...] = jnp.zeros(ddk_ref.shape, F32)
            dnw_ref[...] = jnp.zeros(dnw_ref.shape, F32)

        for gi in range(SSD_GPS):
            xs = slice(gi * SSM_GW, (gi + 1) * SSM_GW)
            bs = slice(gi * 128, (gi + 1) * 128)
            lane0 = SSM_HPG * (SSD_GPS * pl.program_id(0) + gi)
            ins = (st_ref[gi, 0],) + _ssd_group_inputs(gi, *in_refs)
            _, vjp = jax.vjp(functools.partial(ssd_step, lane0), *ins)
            ds, dx, dz, ddt, dbm, dcm, ddtb, dal, ddk, dnw = vjp((dstate[gi], dy_ref[:, xs].astype(F32)))
            dstate[gi] = ds
            dx_ref[:, xs] = dx.astype(dx_ref.dtype)
            db_ref[:, bs] = dbm.astype(db_ref.dtype)
            dc_ref[:, bs] = dcm.astype(dc_ref.dtype)
            dz_ref[:, xs] = dz.astype(dz_ref.dtype)
            ddt_ref[:, bs] = ddt
            ddtb_ref[gi] += ddtb
            dal_ref[gi] += dal
            ddk_ref[gi] += ddk
            dnw_ref[:, xs] += dnw

        exchange_finish(comm, src_refs, out_refs, sems,
                        (pl.program_id(0) == ng - 1) & (pl.program_id(1) == nc - 1))

    rev = lambda n: nc - 1 - n
    row_shape = jax.ShapeDtypeStruct((SSM_GROUPS, 1, 128), F32)
    row_spec = pl.BlockSpec((SSD_GPS, 1, 128), lambda g, n: (g, 0, 0))
    return pl.pallas_call(
        body, name="ssd_bwd", grid=(ng, nc),
        out_shape=[jax.ShapeDtypeStruct((T, SSM_INNER), BF16),
                   jax.ShapeDtypeStruct((T, SSM_GROUPS * 128), BF16),
                   jax.ShapeDtypeStruct((T, SSM_GROUPS * 128), BF16),
                   jax.ShapeDtypeStruct(dproj.shape, dproj.dtype),
                   jax.ShapeDtypeStruct((T, SSM_GROUPS * 128), F32),
                   row_shape, row_shape,
                   jax.ShapeDtypeStruct((SSM_GROUPS, SSM_HPG, 1, 1), F32),
                   jax.ShapeDtypeStruct((1, SSM_INNER), F32)] + exchange_out_shapes(comm),
        in_specs=_ssd_in_specs(True, nc) + [
            pl.BlockSpec((SSD_GPS, 1, 128, SSM_GW), lambda g, n: (g, rev(n), 0, 0)),
            pl.BlockSpec((Q, _XW), lambda g, n: (rev(n), g)),
            any_spec] + [any_spec] * nx,
        out_specs=[pl.BlockSpec((Q, _XW), lambda g, n: (rev(n), g)),
                   pl.BlockSpec((Q, _BW), lambda g, n: (rev(n), g)),
                   pl.BlockSpec((Q, _BW), lambda g, n: (rev(n), g)),
                   pl.BlockSpec((Q, _XW), lambda g, n: (rev(n), P_Z // _XW + g)),
                   pl.BlockSpec((Q, _BW), lambda g, n: (rev(n), g)),
                   row_spec, row_spec,
                   pl.BlockSpec((SSD_GPS, SSM_HPG, 1, 1), lambda g, n: (g, 0, 0, 0)),
                   pl.BlockSpec((1, _XW), lambda g, n: (0, g))] + [any_spec] * nx,
        scratch_shapes=[pltpu.VMEM((SSD_GPS, 128, SSM_GW), F32)] + exchange_semaphores(comm),
        input_output_aliases={11: 3},
        compiler_params=_params(("arbitrary", "arbitrary")),
    )(xact, xact, xact, proj, dtg, dtb, alog, dsk, nw, states, dyb, dproj, *[it[0] for it in comm])


ADAMW_WHOLE_ELEMS = 256 * 1024


def adamw(w, g, m, v, *, name):
    shape = w.shape
    parts = g.shape != shape
    nd = len(shape)
    if nd == 3 and shape[1] == 1 and w.size > ADAMW_WHOLE_ELEMS:
        assert not parts and shape[0] % 4 == 0
        grid = (4,)
        spec = g_spec = pl.BlockSpec((shape[0] // 4, 1, shape[2]), lambda i: (i, 0, 0))
    else:
        if w.size <= ADAMW_WHOLE_ELEMS:
            grid, tr = (1,), shape[-2]
        else:
            assert all(s == 1 for s in shape[:-2]) and shape[-2] % 256 == 0
            grid, tr = (shape[-2] // 256,), 256
        blk = tuple(shape[:-2]) + (tr, shape[-1])
        spec = pl.BlockSpec(blk, lambda i: (0,) * (nd - 2) + (i, 0))
        g_spec = pl.BlockSpec((N_DEV,) + blk[1:], lambda i: (0,) * (nd - 2) + (i, 0)) if parts else spec

    def body(w_ref, g_ref, m_ref, v_ref, go_ref, d_ref, nm_ref, nv_ref):
        if parts:
            g = g_ref[0:1].astype(F32)
            for j in range(1, N_DEV):
                g = g + g_ref[j:j + 1].astype(F32)
        else:
            g = g_ref[...]
        nm = ADAM_B1 * m_ref[...] + (1.0 - ADAM_B1) * g
        nv = ADAM_B2 * v_ref[...] + (1.0 - ADAM_B2) * jnp.square(g)
        m_hat = nm / (1.0 - ADAM_B1 ** ADAM_STEP)
        v_hat = nv / (1.0 - ADAM_B2 ** ADAM_STEP)
        go_ref[...] = g
        d_ref[...] = -ADAM_LR * (m_hat / (jnp.sqrt(v_hat) + ADAM_EPS) + ADAM_WD * w_ref[...])
        nm_ref[...] = nm
        nv_ref[...] = nv

    shp = jax.ShapeDtypeStruct(shape, F32)
    return pl.pallas_call(
        body, name=name, grid=grid,
        out_shape=[shp] * 4, in_specs=[spec, g_spec, spec, spec], out_specs=[spec] * 4,
        compiler_params=_params(("parallel",)),
    )(w, g, m, v)


def _pad_rows(a, rows):
    return jnp.pad(a, ((0, rows - a.shape[0]), (0, 0)))


WIN_W = 1408
N_IN = IN_WIDTH // N_DEV
_A6 = OFF_DT - 6 * N_IN
_C6 = 7 * N_IN - OFF_GA


_WIN_OFFSETS = (0, 4, 8, 12, 16, 20, None, 124)


def _w_in_window(shard, me):
    rows = shard.shape[0]
    z = lambda n: jnp.zeros((rows, n), shard.dtype)

    def plain(off):
        return lambda s: jnp.pad(s, ((0, 0), (off, WIN_W - N_IN - off)))

    def split(s):
        return jnp.concatenate([z(24), s[:, :_A6], s[:, _A6 + 32:], z(4), s[:, _A6:_A6 + 32], z(96)], axis=1)

    return lax.switch(me, [split if off is None else plain(off) for off in _WIN_OFFSETS], shard)


def _w_in_from_window(window, me):
    def plain(off):
        return lambda w: w[:, off:off + N_IN]

    def split(w):
        return jnp.concatenate([w[:, 24:24 + _A6], w[:, 1280:1312], w[:, 24 + _A6:24 + _A6 + _C6]], axis=1)

    return lax.switch(me, [split if off is None else plain(off) for off in _WIN_OFFSETS], window)


def _w_all_from_windows(g):
    def merge_first(p, t):
        return jnp.concatenate([p[:, :128] + t, p[:, 128:]], axis=1)

    parts = [g[0][:, :1280]]
    for j in range(1, 6):
        parts.append(merge_first(g[j][:, :1280], g[j - 1][:, 1280:]))
    p6 = merge_first(g[6][:, :1280], g[5][:, 1280:])
    parts.append(jnp.concatenate([p6[:, :1152], p6[:, 1152:] + g[7][:, :128]], axis=1))
    parts.append(g[7][:, 128:])
    parts.append(g[6][:, 1280:])
    return jnp.concatenate(parts, axis=1)


def _windows_of_w_all(gw):
    wins = [gw[:, 1280 * j:1280 * j + WIN_W] for j in range(6)]
    wins.append(jnp.concatenate([gw[:, 7680:8960], gw[:, PROJ_W:]], axis=1))
    wins.append(gw[:, 8832:PROJ_W])
    return jnp.stack(wins)


def kernel(x, c, w_mod, b_mod, w_in, gm_norm_w, gm_ws, gm_bs, conv_w, conv_b, dt_bias, a_log, d_skip, ssm_norm_w, w_branch_gm, w_branch_ssm, w_out, w_ff1, w_ff2, final_norm_w, loss_target, m_w_mod, m_b_mod, m_w_in, m_gm_norm_w, m_gm_ws, m_gm_bs, m_conv_w, m_conv_b, m_dt_bias, m_a_log, m_d_skip, m_ssm_norm_w, m_w_branch_gm, m_w_branch_ssm, m_w_out, m_w_ff1, m_w_ff2, m_final_norm_w, v_w_mod, v_b_mod, v_w_in, v_gm_norm_w, v_gm_ws, v_gm_bs, v_conv_w, v_conv_b, v_dt_bias, v_a_log, v_d_skip, v_ssm_norm_w, v_w_branch_gm, v_w_branch_ssm, v_w_out, v_w_ff1, v_w_ff2, v_final_norm_w):
    T = x.shape[1]
    me = 4 * lax.axis_index("x") + 2 * lax.axis_index("y") + lax.axis_index("c")
    x2 = x[0]
    tgt = loss_target[0]
    n_in = IN_WIDTH // N_DEV
    n_mod = N_MOD * D // N_DEV
    n_cv = CONV_DIM // N_DEV

    c_all, conv_w_full = exchange(
        [(c.reshape(8, 128), _whole, (N_DEV, 8, 128), _slot),
         (conv_w[0], _whole, (N_DEV, CONV_K, n_cv), _slot)], name="gather_c_convw")
    c_all = c_all.reshape(N_DEV, D)
    conv_w_full = conv_w_full.transpose(1, 0, 2).reshape(CONV_K, CONV_DIM)

    win = _w_in_window(w_in[0].astype(BF16), me)
    late_weights = [
        (w_branch_gm[0].astype(BF16), _whole, (D, D), _rows(D // N_DEV)),
        (w_branch_ssm[0].astype(BF16), _whole, (SSM_INNER, D), _rows(SSM_INNER // N_DEV)),
        (w_out[0].astype(BF16), _whole, (D, D), _rows(D // N_DEV)),
        (w_ff1[0].astype(BF16), _whole, (D, D_FF), _cols(D_FF // N_DEV)),
        (w_ff2[0].astype(BF16), _whole, (D_FF, D), _rows(D_FF // N_DEV))]

    c_pad = _pad_rows(c_all, 128)
    b_mine = lax.dynamic_slice(b_mod, (0, me * n_mod), (1, n_mod))

    def mod_fn(cp, w, b):
        ca = cp * jax.nn.sigmoid(cp)
        return (jnp.dot(ca, w, precision=HIGHEST, preferred_element_type=F32) + b,)

    (mod_part,) = whole_call(mod_fn, [c_pad, w_mod[0], b_mine], [((128, n_mod), F32)], name="mod_fwd")
    gmod = gather_blocks(mod_part[:N_DEV], name="gather_mod")
    mod = lax.dynamic_index_in_dim(gmod, me, axis=1, keepdims=False).reshape(N_MOD, D)
    sh1, sc1, gt1, sh2, sc2, gt2 = [mod[i:i + 1] for i in range(N_MOD)]

    h, gwin = modulate_with_gather(x2, sc1, sh1, win, tm=256, name="modulate1_gather_w_in")
    w_all = _w_all_from_windows(gwin)
    w_dt = w_all[:, PROJ_W:]
    proj = matmul(h, w_all, "nn", BF16, name="mm_proj", n=PROJ_W, tm=K1_TM)
    dtg = matmul(h, w_dt, "nn", F32, name="mm_dt")
    ws = gm_ws[0]
    bs3 = gm_bs[0].reshape(GM_GROUPS, Q, 1)
    sgu_rows = [(proj, D, P_U), (proj, D, P_V)]
    (ya,) = rowwise_call(fwd_body(fn_sgu), sgu_rows, [gm_norm_w, ws, bs3], [(D, BF16)], [],
                         tm=256, name="sgu_fwd")
    xact = conv_fwd(proj, conv_w_full, conv_b)
    dtb4 = jnp.pad(dt_bias, ((0, 0), (0, 96)))
    alog4 = jnp.pad(a_log, ((0, 0), (0, 96)))
    dsk4 = d_skip.reshape(SSM_GROUPS, SSM_HPG, 1, 1)
    yb, states, w_gm_f, w_ssm_f, w_out_f, w_ff1_f, w_ff2_f = ssd_fwd(
        xact, proj, dtg, dtb4, alog4, dsk4, ssm_norm_w, late_weights)
    pa = matmul(ya, w_gm_f, "nn", F32, name="mm_branch_gm", tm=K1_TM)
    gate_rows = [(proj, D, P_GA), (proj, D, P_GB)]
    mixed, pb = rowwise_call(
        lambda r, fl: (fn_mix(*r) + (r[3],), ()), gate_rows + [pa], [], [(D, BF16), (D, F32)], [],
        tm=FUSED_TM, name="branch_ssm_mix", mm=(yb, w_ssm_f, "nn", 3, None), tk=SSM_INNER)
    x1, h2, o = rowwise_call(
        lambda r, fl: (fn_res_modulate(*r, *fl) + (r[1],), ()), [x2], [gt1, sc2, sh2],
        [(D, F32), (D, BF16), (D, F32)], [], tm=FUSED_TM, name="out_res_modulate2",
        mm=(mixed, w_out_f, "nn", 1, None))
    f = matmul(h2, w_ff1_f, "nn", BF16, name="mm_ff1", tm=K1_TM)

    dx1, dgf, loss_v, dgt2, dfnw = rowwise_call(
        final_body, [x1, tgt], [gt2, final_norm_w.reshape(1, D)], [(D, F32), (D, BF16)],
        [(1, 128), (1, D), (1, D)], tm=FUSED_TM, name="ff2_loss_bwd", mm=(f, w_ff2_f, "nn", 1, relu2_tile),
        tk=D_FF)
    df = matmul(dgf, w_ff2_f, "nt", BF16, name="mm_ff2_dgrad", epi=relu2_grad_tile, epi_ins=(f,), tm=K1_TM)
    gw_ff2 = matmul(f, dgf, "tn", BF16, name="mm_ff2_wgrad", tk=WGRAD_TK, a_pro=relu2_tile)
    gw_ff1 = matmul(h2, df, "tn", BF16, name="mm_ff1_wgrad", tk=WGRAD_TK)

    def res_mod_bwd(r, fl):
        xv, ov, dx1v, dh2v = r
        _, vjp = jax.vjp(fn_res_modulate, xv, ov, *fl)
        dxv, dov, dg1, dsc, dsh = vjp((dx1v, dh2v))
        return (dxv, dov), (dg1, dsc, dsh)

    dxa, do, dgt1, dsc2, dsh2 = rowwise_call(
        res_mod_bwd, [x2, o, dx1], [gt1, sc2, sh2], [(D, F32), (D, BF16)],
        [(1, D), (1, D), (1, D)], tm=FUSED_TM, name="ff1_dgrad_res_modulate2_bwd",
        mm=(df, w_ff1_f, "nt", 3, None), tk=D_FF)
    gw_out = matmul(mixed, do, "tn", BF16, name="mm_out_wgrad", tk=WGRAD_TK)
    dproj = lax.empty((T, ALL_W), BF16)

    def mix_bwd(r, fl):
        dga, dgb, dpa, dpb = bwd_body(fn_mix, 4)(r, fl)[0]
        return (jnp.concatenate([dga, dgb], axis=1), dpa, dpb), ()

    dproj, dpa, dpb = rowwise_call(
        mix_bwd, gate_rows + [pa, pb], [], [(dproj, 2 * D, P_GA), (D, BF16), (D, BF16)], [],
        tm=FUSED_TM, name="out_dgrad_mix_bwd", mm=(do, w_out_f, "nt", 4, None))
    gw_gm = matmul(ya, dpa, "tn", BF16, name="mm_branch_gm_wgrad", tk=WGRAD_TK)
    dyb = matmul(dpb, w_ssm_f, "nt", BF16, name="mm_branch_ssm_dgrad", tm=K1_TM)
    gw_ssm = matmul(yb, dpb, "tn", BF16, name="mm_branch_ssm_wgrad", tk=WGRAD_TK)

    def sgu_bwd(r, fl):
        (du, dv), acc = bwd_body(fn_sgu, 2)(r, fl)
        return (jnp.concatenate([du, dv], axis=1),), acc

    dproj, dgnw, dws, dbs = rowwise_call(
        sgu_bwd, sgu_rows, [gm_norm_w, ws, bs3], [(dproj, 2 * D, P_U)],
        [(1, D), (GM_GROUPS, Q, Q), (GM_GROUPS, Q, 1)], tm=FUSED_TM, name="branch_gm_dgrad_sgu_bwd",
        mm=(dpa, w_gm_f, "nt", 2, None))
    early_grads = [
        (gw_gm, _rows(D // N_DEV), (N_DEV, D // N_DEV, D), _slot),
        (gw_ssm, _rows(SSM_INNER // N_DEV), (N_DEV, SSM_INNER // N_DEV, D), _slot),
        (gw_out, _rows(D // N_DEV), (N_DEV, D // N_DEV, D), _slot),
        (gw_ff1, _cols(D_FF // N_DEV), (N_DEV, D, D_FF // N_DEV), _slot),
        (gw_ff2, _rows(D_FF // N_DEV), (N_DEV, D_FF // N_DEV, D), _slot),
        (_pack_rows([dgnw, dws, dbs], EARLY_ROWS), _whole, (N_DEV, sum(EARLY_ROWS), 128), _slot)]
    (dxs, dbm, dcm, dproj, ddt8, ddtb, dalog, ddsk, dsnw,
     r_gm, r_ssm, r_out, r_ff1, r_ff2, early_all) = ssd_bwd(
        xact, proj, dtg, dtb4, alog4, dsk4, ssm_norm_w, states, dyb, dproj, early_grads)
    dconv_w, dconv_b = [], []
    for nm, dact_part, col0 in (("xs", dxs, 0), ("b", dbm, SSM_INNER), ("c", dcm, SSM_INNER + SSM_GROUPS * 128)):
        dproj, dcw, dcb = conv_bwd(proj, dact_part, col0, conv_w_full, conv_b, dproj, name="conv_bwd_" + nm)
        dconv_w.append(dcw)
        dconv_b.append(dcb)
    dconv_w = jnp.concatenate(dconv_w, axis=1)
    dconv_b = jnp.concatenate(dconv_b, axis=1)
    (dproj,) = rowwise_call(
        lambda r, fl: ((functools.reduce(jnp.add, r),), ()),
        [(ddt8, 128, 128 * g) for g in range(SSM_GROUPS)], [], [(dproj, 128, PROJ_W)], [],
        tm=1024, name="ddt_into_dproj")
    gw_all = matmul(h, dproj, "tn", BF16, name="mm_in_wgrad", tn=1152, tk=WGRAD_TK)
    mid_pack = _pack_rows([dconv_w, dconv_b, jnp.sum(ddtb, axis=0), jnp.sum(dalog, axis=0), ddsk, dsnw, dfnw,
                           jnp.concatenate([dgt1, dsh2, dsc2, dgt2], axis=0)], MID_ROWS)
    dh, r_in, mid_all = matmul(
        dproj, w_all, "nt", F32, name="mm_in_dgrad", tm=2048, tk=1152,
        comm=[(_windows_of_w_all(gw_all), _slot, (N_DEV, D, WIN_W), _slot),
              (mid_pack, _whole, (N_DEV, sum(MID_ROWS), 128), _slot)])
    grad_x, dsc1, dsh1 = rowwise_call(grad_x_body, [x2, dh, dxa], [sc1, sh1], [(D, F32)],
                                      [(1, D), (1, D)], tm=256, name="modulate1_bwd")

    g_w_in = _w_in_from_window(sum_devices(r_in, tr=256, name="sum_w_in_grads"), me).reshape(1, D, n_in)

    late_all = gather_blocks(_pack_rows([dsh1, dsc1, loss_v], LATE_ROWS), name="gather_dmod1_loss")
    s_early = _unpack_rows(sum_devices(early_all, tr=early_all.shape[1], name="sum_small_early"), EARLY_ROWS)
    s_mid = _unpack_rows(sum_devices(mid_all, tr=mid_all.shape[1], name="sum_small_mid"), MID_ROWS)
    s_late = _unpack_rows(sum_devices(late_all, tr=late_all.shape[1], name="sum_small_late"), LATE_ROWS)
    g_gm_norm_w = s_early[0][:D].reshape(1, D)
    g_gm_ws = s_early[1].reshape(GM_GROUPS * Q, Q)
    g_gm_bs = s_early[2][:GM_GROUPS * Q].reshape(GM_GROUPS, Q)
    g_conv_w_full = s_mid[0].reshape(CONV_K, CONV_DIM)
    g_conv_w = lax.dynamic_slice(g_conv_w_full, (0, me * n_cv), (CONV_K, n_cv))
    g_conv_b = s_mid[1].reshape(1, CONV_DIM)
    g_dt_bias = s_mid[2][:32].reshape(1, 32)
    g_a_log = s_mid[3][:32].reshape(1, 32)
    g_d_skip = s_mid[4][:32].reshape(1, 32)
    g_ssm_norm_w = s_mid[5].reshape(1, SSM_INNER)
    g_final_norm_w = s_mid[6][:D].reshape(1, D)
    g_b_mod = jnp.concatenate([s_late[0][:D], s_late[1][:D], s_mid[7]]).reshape(1, N_MOD * D)

    dmod_all = jnp.concatenate(
        [late_all.reshape(N_DEV, -1)[:, :2 * D],
         mid_all[:, sum(MID_ROWS[:7]):].reshape(N_DEV, 4 * D)], axis=1)
    dmod_mine = _pad_rows(lax.dynamic_slice(dmod_all, (0, me * n_mod), (N_DEV, n_mod)), 128)

    def wmod_grad_fn(cp, dm):
        ca = cp * jax.nn.sigmoid(cp)
        return (lax.dot_general(ca, dm, (((0,), (0,)), ((), ())), precision=HIGHEST,
                                preferred_element_type=F32),)

    (g_w_mod,) = whole_call(wmod_grad_fn, [c_pad, dmod_mine], [((D, n_mod), F32)], name="w_mod_grad")

    upd = {}

    def step(name, w, g, m, v, parts=False):
        upd[name] = adamw(w, g if parts else g.reshape(w.shape), m, v, name="adamw_" + name)

    step("w_mod", w_mod, g_w_mod, m_w_mod, v_w_mod)
    step("b_mod", b_mod, g_b_mod, m_b_mod, v_b_mod)
    col_major = lambda a: jnp.transpose(a, (2, 0, 1))
    upd["w_in"] = tuple(jnp.transpose(o, (1, 2, 0)) for o in adamw(
        col_major(w_in), col_major(g_w_in), col_major(m_w_in), col_major(v_w_in), name="adamw_w_in"))
    step("gm_norm_w", gm_norm_w, g_gm_norm_w, m_gm_norm_w, v_gm_norm_w)
    step("gm_ws", gm_ws, g_gm_ws, m_gm_ws, v_gm_ws)
    step("gm_bs", gm_bs, g_gm_bs, m_gm_bs, v_gm_bs)
    step("conv_w", conv_w, g_conv_w, m_conv_w, v_conv_w)
    step("conv_b", conv_b, g_conv_b, m_conv_b, v_conv_b)
    step("dt_bias", dt_bias, g_dt_bias, m_dt_bias, v_dt_bias)
    step("a_log", a_log, g_a_log, m_a_log, v_a_log)
    step("d_skip", d_skip, g_d_skip, m_d_skip, v_d_skip)
    step("ssm_norm_w", ssm_norm_w, g_ssm_norm_w, m_ssm_norm_w, v_ssm_norm_w)
    step("w_branch_gm", w_branch_gm, r_gm, m_w_branch_gm, v_w_branch_gm, parts=True)
    step("w_branch_ssm", w_branch_ssm, r_ssm, m_w_branch_ssm, v_w_branch_ssm, parts=True)
    step("w_out", w_out, r_out, m_w_out, v_w_out, parts=True)
    step("w_ff1", w_ff1, r_ff1, m_w_ff1, v_w_ff1, parts=True)
    step("w_ff2", w_ff2, r_ff2, m_w_ff2, v_w_ff2, parts=True)
    step("final_norm_w", final_norm_w.reshape(1, D), g_final_norm_w, m_final_norm_w.reshape(1, D),
         v_final_norm_w.reshape(1, D))
    upd["final_norm_w"] = tuple(a.reshape(D) for a in upd["final_norm_w"])

    loss = s_late[2][0]
    order = ["w_mod", "b_mod", "w_in", "gm_norm_w", "gm_ws", "gm_bs", "conv_w", "conv_b", "dt_bias", "a_log",
             "d_skip", "ssm_norm_w", "w_branch_gm", "w_branch_ssm", "w_out", "w_ff1", "w_ff2", "final_norm_w"]
    return (loss, grad_x.reshape(1, T, D),
            *[upd[n][0] for n in order], *[upd[n][1] for n in order],
            *[upd[n][2] for n in order], *[upd[n][3] for n in order])
```

```python
import functools

import jax
import jax.numpy as jnp
from jax import lax
from jax.experimental import pallas as pl
from jax.experimental.pallas import tpu as pltpu

F32 = jnp.float32
BF16 = jnp.bfloat16
MESH = pl.DeviceIdType.MESH
HIGHEST = lax.Precision.HIGHEST

N_DEV = 8
D = 1024
Q = 128
GM_GROUPS = 8
SSM_INNER = 2048
SSM_GROUPS = 8
SSM_HPG = 4
SSM_P = 64
SSM_GW = SSM_HPG * SSM_P
CONV_DIM = 4096
CONV_K = 4
D_FF = 4096
N_MOD = 6
EPS = 1e-6
IN_WIDTH = 10272
OFF_DT = 8192
OFF_GA = 8224
PROJ_W = 10240
ALL_W = 10368
P_U, P_V, P_Z, P_XBC, P_GA, P_GB = 0, 1024, 2048, 4096, 8192, 9216

ADAM_LR = 0.001
ADAM_B1 = 0.9
ADAM_B2 = 0.999
ADAM_EPS = 1e-08
ADAM_WD = 0.01
ADAM_STEP = 10

VMEM_LIMIT_BYTES = 48 * 1024 * 1024
K1_TM = 2048
FUSED_TM = 512
WGRAD_TK = 2048
EARLY_ROWS = (8, 1024, 8)
MID_ROWS = (128, 32, 8, 8, 8, 16, 8, 32)
LATE_ROWS = (8, 8, 8)


def _pack_rows(arrs, rows):
    def rows128(a, r):
        a = a.reshape(-1)
        return jnp.pad(a, (0, r * 128 - a.shape[0])).reshape(r, 128)
    return jnp.concatenate([rows128(a, r) for a, r in zip(arrs, rows)], axis=0)


def _unpack_rows(s, rows):
    out, o = [], 0
    for r in rows:
        out.append(s[o:o + r].reshape(-1))
        o += r
    return out


def _params(sem=None):
    return pltpu.CompilerParams(dimension_semantics=sem, vmem_limit_bytes=VMEM_LIMIT_BYTES)


def _dg(a, b, ca, cb):
    return lax.dot_general(a.astype(BF16), b.astype(BF16), (((ca,), (cb,)), ((), ())),
                           preferred_element_type=F32)


@jax.custom_vjp
def dot_nn(a, b):
    return _dg(a, b, 1, 0)


@jax.custom_vjp
def dot_nt(a, b):
    return _dg(a, b, 1, 1)


@jax.custom_vjp
def dot_tn(a, b):
    return _dg(a, b, 0, 0)


def _like(ct, primal):
    return ct.astype(primal.dtype)


dot_nn.defvjp(lambda a, b: (dot_nn(a, b), (a, b)),
              lambda r, g: (_like(dot_nt(g, r[1]), r[0]), _like(dot_tn(r[0], g), r[1])))
dot_nt.defvjp(lambda a, b: (dot_nt(a, b), (a, b)),
              lambda r, g: (_like(dot_nn(g, r[1]), r[0]), _like(dot_tn(g, r[0]), r[1])))
dot_tn.defvjp(lambda a, b: (dot_tn(a, b), (a, b)),
              lambda r, g: (_like(dot_nt(r[1], g), r[0]), _like(dot_nn(r[0], g), r[1])))


def _rms(x):
    return x * lax.rsqrt(jnp.mean(x * x, axis=-1, keepdims=True) + EPS)


def _softplus(x):
    return jnp.maximum(x, 0.0) + jnp.log1p(jnp.exp(-jnp.abs(x)))


def _rows(n):
    return lambda ref, j: ref.at[pl.ds(pl.multiple_of(j * n, n), n)]


def _cols(n):
    return lambda ref, j: ref.at[:, pl.ds(pl.multiple_of(j * n, n), n)]


def _slot(ref, j):
    return ref.at[j]


def _whole(ref, j):
    return ref


def exchange(items, *, name):
    n = len(items)

    def body(*refs):
        exchange_in_body(items, refs[:n], refs[n:2 * n], refs[2 * n:], True, True)

    return pl.pallas_call(
        body, name=name,
        out_shape=exchange_out_shapes(items),
        in_specs=[pl.BlockSpec(memory_space=pl.ANY)] * n,
        out_specs=[pl.BlockSpec(memory_space=pl.ANY)] * n,
        scratch_shapes=exchange_semaphores(items),
    )(*[it[0] for it in items])


def exchange_out_shapes(items):
    return [jax.ShapeDtypeStruct(tuple(shape), src.dtype) for (src, _, shape, _) in items]


def exchange_semaphores(items):
    n = len(items)
    return [pltpu.SemaphoreType.DMA((n, N_DEV - 1)), pltpu.SemaphoreType.DMA((n, N_DEV - 1)),
            pltpu.SemaphoreType.DMA((n,))]


def _exchange_copies(items, src_refs, out_refs, sems):
    send_sems, recv_sems, local_sems = sems
    x = lax.axis_index("x")
    y = lax.axis_index("y")
    c = lax.axis_index("c")
    me = 4 * x + 2 * y + c
    local = [pltpu.make_async_copy(src_win(src_refs[i], me), dst_win(out_refs[i], me), local_sems.at[i])
             for i, (_, src_win, _, dst_win) in enumerate(items)]
    remote = []
    for i, (_, src_win, _, dst_win) in enumerate(items):
        for k in range(1, N_DEV):
            px = lax.rem(x + ((k >> 2) & 1), 2)
            py = lax.rem(y + ((k >> 1) & 1), 2)
            pc = lax.rem(c + (k & 1), 2)
            peer = 4 * px + 2 * py + pc
            remote.append(pltpu.make_async_remote_copy(
                src_ref=src_win(src_refs[i], peer), dst_ref=dst_win(out_refs[i], me),
                send_sem=send_sems.at[i, k - 1], recv_sem=recv_sems.at[i, k - 1],
                device_id=(px, py, pc), device_id_type=MESH))
    return local, remote


def _when(cond, fn):
    if cond is True:
        fn()
    else:
        pl.when(cond)(fn)


def exchange_start(items, src_refs, out_refs, sems, cond):
    def start():
        local, remote = _exchange_copies(items, src_refs, out_refs, sems)
        for cp in local + remote:
            cp.start()
    _when(cond, start)


def exchange_finish(items, src_refs, out_refs, sems, cond):
    def finish():
        local, remote = _exchange_copies(items, src_refs, out_refs, sems)
        for cp in remote:
            cp.wait_send()
        for cp in remote:
            cp.wait_recv()
        for cp in local:
            cp.wait()
    _when(cond, finish)


def exchange_in_body(items, src_refs, out_refs, sems, first, last):
    exchange_start(items, src_refs, out_refs, sems, first)
    exchange_finish(items, src_refs, out_refs, sems, last)


def gather_blocks(src, *, name):
    return exchange([(src, _whole, (N_DEV,) + src.shape, _slot)], name=name)[0]


def _two_level_gather(src_ref, out_ref, send_sems, recv_sems, local_sem):
    x = lax.axis_index("x")
    y = lax.axis_index("y")
    c = lax.axis_index("c")
    me, sibling = (x, y, c), (x, y, 1 - c)
    chips = [(1 - x, y), (x, 1 - y), (1 - x, 1 - y)]

    def slot(px, py, pc):
        return out_ref.at[4 * px + 2 * py + pc]

    def copy(k, block, to, src=None):
        return pltpu.make_async_remote_copy(
            src_ref=slot(*block) if src is None else src, dst_ref=slot(*block),
            send_sem=send_sems.at[k], recv_sem=recv_sems.at[k], device_id=to, device_id_type=MESH)

    def own_copies():
        return ([copy(0, me, sibling, src=src_ref)]
                + [copy(1 + j, me, (*chip, c), src=src_ref) for j, chip in enumerate(chips)])

    def start():
        pltpu.make_async_copy(src_ref, slot(*me), local_sem).start()
        for cp in own_copies():
            cp.start()

    def finish():
        passed = [copy(4 + j, (*chip, c), sibling) for j, chip in enumerate(chips)]
        for j, chip in enumerate(chips):
            copy(1 + j, (*chip, c), me).wait_recv()
            passed[j].start()
        copy(0, sibling, me).wait_recv()
        for j, chip in enumerate(chips):
            copy(4 + j, (*chip, 1 - c), me).wait_recv()
        for cp in own_copies() + passed:
            cp.wait_send()
        pltpu.make_async_copy(src_ref, slot(*me), local_sem).wait()

    return start, finish


_TWO_LEVEL_SEMS = [pltpu.SemaphoreType.DMA((N_DEV - 1,)), pltpu.SemaphoreType.DMA((N_DEV - 1,)),
                   pltpu.SemaphoreType.DMA(())]


def modulate_with_gather(x2, sc, sh, src, *, tm, name):
    T, width = x2.shape
    n = T // tm

    def body(x_ref, sc_ref, sh_ref, src_ref, h_ref, out_ref, *sems):
        start, finish = _two_level_gather(src_ref, out_ref, *sems)
        pl.when(pl.program_id(0) == 0)(start)
        (h,) = fn_modulate(x_ref[...], sc_ref[...], sh_ref[...])
        h_ref[...] = h.astype(h_ref.dtype)
        pl.when(pl.program_id(0) == n - 1)(finish)

    row = pl.BlockSpec((tm, width), lambda i: (i, 0))
    full = pl.BlockSpec((1, width), lambda i: (0, 0))
    any_spec = pl.BlockSpec(memory_space=pl.ANY)
    return pl.pallas_call(
        body, name=name, grid=(n,),
        out_shape=[jax.ShapeDtypeStruct((T, width), BF16),
                   jax.ShapeDtypeStruct((N_DEV,) + src.shape, src.dtype)],
        in_specs=[row, full, full, any_spec], out_specs=[row, any_spec],
        scratch_shapes=_TWO_LEVEL_SEMS,
        compiler_params=_params(("arbitrary",)),
    )(x2, sc, sh, src)


def sum_devices(g, *, tr, name):
    _, R, C = g.shape

    def body(g_ref, o_ref):
        acc = g_ref[0].astype(F32)
        for j in range(1, N_DEV):
            acc = acc + g_ref[j].astype(F32)
        o_ref[...] = acc

    return pl.pallas_call(
        body, name=name, grid=(R // tr,),
        out_shape=jax.ShapeDtypeStruct((R, C), F32),
        in_specs=[pl.BlockSpec((N_DEV, tr, C), lambda i: (0, i, 0))],
        out_specs=pl.BlockSpec((tr, C), lambda i: (i, 0)),
        compiler_params=_params(("parallel",)),
    )(g)


def matmul(a, b, mode, out_dtype, *, name, tm=1024, tn=1024, tk=1024, n=None, comm=None,
           a_pro=None, epi=None, epi_ins=()):
    if mode == "nn":
        (M, K), (K2, N) = a.shape, b.shape
    elif mode == "nt":
        (M, K), (N, K2) = a.shape, b.shape
    else:
        (K, M), (K2, N) = a.shape, b.shape
    assert K == K2
    N = N if n is None else n
    tm, tn, tk = min(tm, M), min(tn, N), min(tk, K)
    assert M % tm == 0 and N % tn == 0 and K % tk == 0, (name, M, N, K, tm, tn, tk)
    nk = K // tk
    if mode == "tn":
        a_spec = pl.BlockSpec((tk, tm), lambda i, j, k: (k, i))
    else:
        a_spec = pl.BlockSpec((tm, tk), lambda i, j, k: (i, k))
    if mode == "nt":
        b_spec = pl.BlockSpec((tn, tk), lambda i, j, k: (j, k))
    else:
        b_spec = pl.BlockSpec((tk, tn), lambda i, j, k: (k, j))
    dims = {"nn": (1, 0), "nt": (1, 1), "tn": (0, 0)}[mode]
    items = list(comm) if comm else []
    nx = len(items)
    ne = len(epi_ins)
    gm, gn = M // tm, N // tn
    any_spec = pl.BlockSpec(memory_space=pl.ANY)
    o_spec = pl.BlockSpec((tm, tn), lambda i, j, k: (i, j))

    def body(*refs):
        a_ref, b_ref, e_refs = refs[0], refs[1], refs[2:2 + ne]
        refs = refs[2 + ne:]
        src_refs, o_ref, out_refs = refs[:nx], refs[nx], refs[1 + nx:1 + 2 * nx]
        acc_ref, sems = refs[1 + 2 * nx], refs[2 + 2 * nx:]
        i, j, k = pl.program_id(0), pl.program_id(1), pl.program_id(2)
        if items:
            exchange_start(items, src_refs, out_refs, sems, (i == 0) & (j == 0) & (k == 0))
        a_tile = a_ref[...] if a_pro is None else a_pro(a_ref[...])
        part = lax.dot_general(a_tile, b_ref[...], (((dims[0],), (dims[1],)), ((), ())),
                               preferred_element_type=F32)

        def finish(acc):
            if epi is not None:
                acc = epi(acc, *[e[...] for e in e_refs])
            o_ref[...] = acc.astype(o_ref.dtype)

        if nk == 1:
            finish(part)
        else:
            @pl.when(k == 0)
            def _():
                acc_ref[...] = part

            @pl.when((k > 0) & (k < nk - 1))
            def _():
                acc_ref[...] += part

            @pl.when(k == nk - 1)
            def _():
                finish(acc_ref[...] + part)

        if items:
            exchange_finish(items, src_refs, out_refs, sems, (i == gm - 1) & (j == gn - 1) & (k == nk - 1))

    res = pl.pallas_call(
        body, name=name, grid=(gm, gn, nk),
        out_shape=[jax.ShapeDtypeStruct((M, N), out_dtype)] + exchange_out_shapes(items),
        in_specs=[a_spec, b_spec] + [o_spec] * ne + [any_spec] * nx,
        out_specs=[o_spec] + [any_spec] * nx,
        scratch_shapes=[pltpu.VMEM((tm, tn) if nk > 1 else (8, 128), F32)]
        + (exchange_semaphores(items) if items else []),
        compiler_params=_params(("arbitrary",) * 3 if items else ("parallel", "parallel", "arbitrary")),
    )(a, b, *epi_ins, *[it[0] for it in items])
    return res if items else res[0]


def rowwise_call(body_fn, rows, fulls, row_outs, acc_outs, *, tm, name, mm=None, tk=1024):
    rows = [r if isinstance(r, tuple) else (r, r.shape[1], 0) for r in rows]
    T = rows[0][0].shape[0]
    tm = min(tm, T)
    assert T % tm == 0
    n_r, n_f, n_ro = len(rows), len(fulls), len(row_outs)
    into = [(k, ro) for k, ro in enumerate(row_outs) if len(ro) == 3]
    n_b = len(into)
    n_mm, nk = 0, 1
    if mm is not None:
        a, b, mode, pos, a_pro = mm
        n_mm = 2
        K = a.shape[1]
        N = b.shape[1] if mode == "nn" else b.shape[0]
        tk = min(tk, K)
        assert K % tk == 0 and a.shape[0] == T
        nk = K // tk
        b_contract = 0 if mode == "nn" else 1

    def row_body(refs, product):
        r_refs = refs[:n_r]
        f_refs = refs[n_r:n_r + n_f]
        refs = refs[n_r + n_f + n_b:]
        ro_refs = refs[:n_ro]
        ao_refs = refs[n_ro:n_ro + len(acc_outs)]
        r_vals = [r[...].astype(F32) for r in r_refs]
        if product is not None:
            r_vals.insert(pos, product)
        f_vals = [f[...].astype(F32) for f in f_refs]
        ro, ao = body_fn(r_vals, f_vals)
        for ref, v in zip(ro_refs, ro):
            ref[...] = v.astype(ref.dtype)
        if ao_refs:
            @pl.when(pl.program_id(0) == 0)
            def _():
                for ref in ao_refs:
                    ref[...] = jnp.zeros(ref.shape, F32)
            for ref, v in zip(ao_refs, ao):
                ref[...] += v.reshape(ref.shape)

    def body(*refs):
        if mm is None:
            return row_body(refs, None)
        a_ref, b_ref, rest, acc_ref = refs[0], refs[1], refs[2:-1], refs[-1]
        k = pl.program_id(1)
        a_tile = a_ref[...] if a_pro is None else a_pro(a_ref[...])
        part = lax.dot_general(a_tile, b_ref[...], (((1,), (b_contract,)), ((), ())),
                               preferred_element_type=F32)
        if nk == 1:
            return row_body(rest, part)

        @pl.when(k == 0)
        def _():
            acc_ref[...] = part

        @pl.when((k > 0) & (k < nk - 1))
        def _():
            acc_ref[...] += part

        @pl.when(k == nk - 1)
        def _():
            row_body(rest, acc_ref[...] + part)

    def full_spec(shape):
        nd = len(shape)
        return pl.BlockSpec(tuple(shape), lambda i, *_: (0,) * nd)

    def row_spec(w, off):
        return pl.BlockSpec((tm, w), functools.partial(lambda i, *_, o: (i, o), o=off // w))

    in_specs = []
    if mm is not None:
        in_specs.append(pl.BlockSpec((tm, tk), lambda i, k: (i, k)))
        in_specs.append(pl.BlockSpec((tk, N), lambda i, k: (k, 0)) if mode == "nn" else
                        pl.BlockSpec((N, tk), lambda i, k: (0, k)))
    in_specs += [row_spec(w, off) for (_, w, off) in rows]
    in_specs += [full_spec(f.shape) for f in fulls]
    in_specs += [pl.BlockSpec(memory_space=pl.ANY)] * n_b
    out_specs, out_shape = [], []
    for ro in row_outs:
        if len(ro) == 3:
            buf, w, off = ro
            out_specs.append(row_spec(w, off))
            out_shape.append(jax.ShapeDtypeStruct(buf.shape, buf.dtype))
        else:
            w, dt = ro
            out_specs.append(row_spec(w, 0))
            out_shape.append(jax.ShapeDtypeStruct((T, w), dt))
    out_specs += [full_spec(s) for s in acc_outs]
    out_shape += [jax.ShapeDtypeStruct(tuple(s), F32) for s in acc_outs]
    aliases = {n_mm + n_r + n_f + b: k for b, (k, _) in enumerate(into)}
    return pl.pallas_call(
        body, name=name, grid=(T // tm,) if mm is None else (T // tm, nk),
        out_shape=out_shape, in_specs=in_specs, out_specs=out_specs,
        scratch_shapes=[] if mm is None else [pltpu.VMEM((tm, N) if nk > 1 else (8, 128), F32)],
        input_output_aliases=aliases,
        compiler_params=_params(("arbitrary",) if mm is None else ("arbitrary", "arbitrary")),
    )(*([] if mm is None else [a, b]), *[r[0] for r in rows], *fulls, *[ro[0] for _, ro in into])


def fwd_body(fn):
    return lambda r, f: (fn(*r, *f), ())


def bwd_body(fn, n_rows):
    def body(r, f):
        ins, cots = r[:n_rows], r[n_rows:]
        _, vjp = jax.vjp(fn, *ins, *f)
        g = vjp(tuple(cots))
        return g[:n_rows], g[n_rows:]
    return body


def whole_call(fn, ins, outs, *, name):
    n_in = len(ins)

    def body(*refs):
        res = fn(*[r[...] for r in refs[:n_in]])
        for ref, v in zip(refs[n_in:], res):
            ref[...] = v.astype(ref.dtype)

    return pl.pallas_call(
        body, name=name,
        out_shape=[jax.ShapeDtypeStruct(tuple(s), dt) for (s, dt) in outs],
        compiler_params=_params(),
    )(*ins)


def fn_modulate(x, sc, sh):
    return (_rms(x) * (1.0 + sc) + sh,)


def fn_sgu(u, v, nw, ws, bs):
    ug = jax.nn.gelu(u)
    vn = _rms(jax.nn.gelu(v)) * nw
    ri = lax.broadcasted_iota(jnp.int32, (Q, Q), 0)
    ci = lax.broadcasted_iota(jnp.int32, (Q, Q), 1)
    causal = ri >= ci
    chunks = []
    for n in range(u.shape[0] // Q):
        vc = vn[n * Q:(n + 1) * Q]
        cols = [dot_nn(jnp.where(causal, ws[g], 0.0), vc[:, g * Q:(g + 1) * Q]) + bs[g]
                for g in range(GM_GROUPS)]
        chunks.append(jnp.concatenate(cols, axis=1))
    sv = chunks[0] if len(chunks) == 1 else jnp.concatenate(chunks, axis=0)
    return (ug * sv,)


def fn_mix(ga, gb, pa, pb):
    return (jax.nn.sigmoid(ga) * pa + jax.nn.sigmoid(gb) * pb,)


def fn_res_modulate(x, o, g1, sc2, sh2):
    x1 = x + g1 * o
    return x1, _rms(x1) * (1.0 + sc2) + sh2


def relu2_tile(f):
    return jnp.square(jnp.maximum(f.astype(F32), 0.0)).astype(BF16)


def relu2_grad_tile(dact, f):
    return dact * (2.0 * jnp.maximum(f.astype(F32), 0.0))


def final_body(r, f):
    x1, gf, tgt = r
    g2, fnw = f

    def loss_fn(x1, gf, g2, fnw):
        y = _rms(x1 + g2 * gf) * fnw
        row = 0.5 * jnp.mean(jnp.square(y - tgt), axis=-1, keepdims=True)
        return jnp.sum(row, axis=0, keepdims=True)

    l, vjp = jax.vjp(loss_fn, x1, gf, g2, fnw)
    dx1, dgf, dg2, dfnw = vjp(jnp.ones((1, 1), F32))
    return (dx1, dgf), (jnp.broadcast_to(l, (1, 128)), dg2, dfnw)


def grad_x_body(r, f):
    x, dh, dxa = r
    _, vjp = jax.vjp(fn_modulate, x, *f)
    dx, dsc, dsh = vjp((dh,))
    return (dx + dxa,), (dsc, dsh)


CONV_CW = 128
CONV_PAD = 8
CONV_ROWS = 128


def _conv_pre(xp, w_ref, b_ref, r0, R):
    acc = b_ref[...] + w_ref[0:1, :] * xp[r0 + CONV_PAD - 3:r0 + CONV_PAD - 3 + R, :]
    for k in range(1, CONV_K):
        s = r0 + CONV_PAD - 3 + k
        acc = acc + w_ref[k:k + 1, :] * xp[s:s + R, :]
    return acc


def conv_fwd(proj, conv_w, conv_b):
    T = proj.shape[0]
    R = min(CONV_ROWS, T)

    def body(x_ref, w_ref, b_ref, o_ref, xp):
        xp[0:CONV_PAD, :] = jnp.zeros((CONV_PAD, CONV_CW), F32)
        xp[CONV_PAD:CONV_PAD + T, :] = x_ref[...].astype(F32)
        for r0 in range(0, T, R):
            pre = _conv_pre(xp, w_ref, b_ref, r0, R)
            o_ref[r0:r0 + R, :] = (pre * jax.nn.sigmoid(pre)).astype(o_ref.dtype)

    return pl.pallas_call(
        body, name="conv_fwd", grid=(CONV_DIM // CONV_CW,),
        out_shape=jax.ShapeDtypeStruct((T, CONV_DIM), BF16),
        in_specs=[pl.BlockSpec((T, CONV_CW), lambda j: (0, P_XBC // CONV_CW + j)),
                  pl.BlockSpec((CONV_K, CONV_CW), lambda j: (0, j)),
                  pl.BlockSpec((1, CONV_CW), lambda j: (0, j))],
        out_specs=pl.BlockSpec((T, CONV_CW), lambda j: (0, j)),
        scratch_shapes=[pltpu.VMEM((T + CONV_PAD, CONV_CW), F32)],
        compiler_params=_params(("parallel",)),
    )(proj, conv_w, conv_b)


def conv_bwd(proj, dact, col0, conv_w, conv_b, dproj, *, name):
    T = proj.shape[0]
    R = min(CONV_ROWS, T)
    nb = dact.shape[1] // CONV_CW
    c0 = col0 // CONV_CW
    x0 = (P_XBC + col0) // CONV_CW

    def body(x_ref, d_ref, w_ref, b_ref, _, dx_ref, dw_ref, db_ref, xp, dp):
        xp[0:CONV_PAD, :] = jnp.zeros((CONV_PAD, CONV_CW), F32)
        xp[CONV_PAD:CONV_PAD + T, :] = x_ref[...].astype(F32)
        dp[T:T + CONV_PAD, :] = jnp.zeros((CONV_PAD, CONV_CW), F32)
        dws = [jnp.zeros((1, CONV_CW), F32) for _ in range(CONV_K)]
        db = jnp.zeros((1, CONV_CW), F32)
        for r0 in range(0, T, R):
            pre = _conv_pre(xp, w_ref, b_ref, r0, R)
            s = jax.nn.sigmoid(pre)
            dpre = d_ref[r0:r0 + R, :].astype(F32) * (s * (1.0 + pre * (1.0 - s)))
            dp[r0:r0 + R, :] = dpre
            db = db + jnp.sum(dpre, axis=0, keepdims=True)
            for k in range(CONV_K):
                st = r0 + CONV_PAD - 3 + k
                dws[k] = dws[k] + jnp.sum(dpre * xp[st:st + R, :], axis=0, keepdims=True)
        for r0 in range(0, T, R):
            acc = w_ref[0:1, :] * dp[r0 + 3:r0 + 3 + R, :]
            for k in range(1, CONV_K):
                acc = acc + w_ref[k:k + 1, :] * dp[r0 + 3 - k:r0 + 3 - k + R, :]
            dx_ref[r0:r0 + R, :] = acc.astype(dx_ref.dtype)
        for k in range(CONV_K):
            dw_ref[k:k + 1, :] = dws[k]
        db_ref[...] = db

    return pl.pallas_call(
        body, name=name, grid=(nb,),
        out_shape=[jax.ShapeDtypeStruct(dproj.shape, dproj.dtype),
                   jax.ShapeDtypeStruct((CONV_K, nb * CONV_CW), F32),
                   jax.ShapeDtypeStruct((1, nb * CONV_CW), F32)],
        in_specs=[pl.BlockSpec((T, CONV_CW), lambda j: (0, x0 + j)),
                  pl.BlockSpec((T, CONV_CW), lambda j: (0, j)),
                  pl.BlockSpec((CONV_K, CONV_CW), lambda j: (0, c0 + j)),
                  pl.BlockSpec((1, CONV_CW), lambda j: (0, c0 + j)),
                  pl.BlockSpec(memory_space=pl.ANY)],
        out_specs=[pl.BlockSpec((T, CONV_CW), lambda j: (0, x0 + j)),
                   pl.BlockSpec((CONV_K, CONV_CW), lambda j: (0, j)),
                   pl.BlockSpec((1, CONV_CW), lambda j: (0, j))],
        scratch_shapes=[pltpu.VMEM((T + CONV_PAD, CONV_CW), F32),
                        pltpu.VMEM((T + CONV_PAD, CONV_CW), F32)],
        input_output_aliases={4: 0},
        compiler_params=_params(("parallel",)),
    )(proj, dact, conv_w, conv_b, dproj)


def _split3(a):
    hi = a.astype(BF16)
    r = a - hi.astype(F32)
    mid = r.astype(BF16)
    return hi, mid, (r - mid.astype(F32)).astype(BF16)


def _dg3(a, m, ca, cm, a_first):
    dims = (((ca,), (cm,)), ((), ())) if a_first else (((cm,), (ca,)), ((), ()))
    out = None
    for p in _split3(a):
        t = lax.dot_general(p, m, dims, preferred_element_type=F32) if a_first else \
            lax.dot_general(m, p, dims, preferred_element_type=F32)
        out = t if out is None else out + t
    return out


@jax.custom_vjp
def exact_right(a, m):
    return _dg3(a, m, 1, 0, True)


@jax.custom_vjp
def exact_left(m, a):
    return _dg3(a, m, 0, 1, False)


def _expand_bwd(m, g):
    hi = g.astype(BF16)
    lo = (g - hi.astype(F32)).astype(BF16)
    out = lax.dot_general(jnp.concatenate([hi, lo], axis=1), jnp.concatenate([m, m], axis=1),
                          (((1,), (1,)), ((), ())), preferred_element_type=F32)
    return out, jnp.zeros_like(m)


exact_right.defvjp(lambda a, m: (exact_right(a, m), m), _expand_bwd)
exact_left.defvjp(lambda m, a: (exact_left(m, a), m),
                  lambda m, g: (jnp.zeros_like(m), _dg3(g, m, 0, 0, False)))


def ssd_step(lane0, state, x, z, dtr, Bm, Cm, dtb, alog, dsk, nw):
    def iota(shape, dim):
        return lax.broadcasted_iota(jnp.int32, shape, dim)

    def one_hot(mask):
        return mask.astype(F32).astype(BF16)

    causal = iota((Q, Q), 0) >= iota((Q, Q), 1)
    eye = iota((Q, Q), 0) == iota((Q, Q), 1)
    lane = iota((1, 128), 1)
    colh = lax.shift_right_logical(iota((1, SSM_GW), 1), 6)
    to_cols = one_hot(iota((128, SSM_GW), 0) == lane0 + colh)

    dt_all = _softplus(dtr + dtb)
    a_all = dt_all * (-jnp.exp(alog))
    cum_all = exact_left(one_hot(causal), a_all)
    both = exact_right(jnp.concatenate([dt_all, cum_all], axis=0), to_cols)
    dt_f, cum_f = both[:Q], both[Q:]
    last_f = jnp.sum(jnp.where(iota((Q, 1), 0) == Q - 1, cum_f, 0.0), axis=0, keepdims=True)
    dsk_f = jnp.zeros((1, SSM_GW), F32)
    for h in range(SSM_HPG):
        dsk_f = jnp.where(colh == h, dsk[h], dsk_f)

    xdt = x * dt_f
    cb = dot_nt(Cm, Bm)
    ms = []
    for h in range(SSM_HPG):
        ch = jnp.sum(jnp.where(lane == lane0 + h, cum_all, 0.0), axis=1, keepdims=True)
        ch_t = jnp.sum(jnp.where(eye, ch, 0.0), axis=0, keepdims=True)
        ms.append(cb * jnp.exp(jnp.where(causal, ch - ch_t, -1e30)))
    first_half = lane < SSM_P
    blocks = []
    for b in range(SSM_HPG // 2):
        xb = xdt[:, b * 128:(b + 1) * 128]
        rhs = jnp.concatenate([jnp.where(first_half, xb, 0.0), jnp.where(first_half, 0.0, xb)], axis=0)
        blocks.append(dot_nn(jnp.concatenate(ms[2 * b:2 * b + 2], axis=1), rhs))
    y = jnp.concatenate(blocks, axis=1)
    y = y + dot_nn(Cm, state) * jnp.exp(cum_f) + x * dsk_f
    new_state = state * jnp.exp(last_f) + dot_tn(Bm, xdt * jnp.exp(last_f - cum_f))
    gated = y * (z * jax.nn.sigmoid(z))
    return new_state, _rms(gated) * nw


SSD_GPS = 4
_XW = SSD_GPS * SSM_GW
_BW = SSD_GPS * 128


def _ssd_in_specs(rev, nc):
    def n_of(n):
        return nc - 1 - n if rev else n
    return [
        pl.BlockSpec((Q, _XW), lambda g, n: (n_of(n), g)),
        pl.BlockSpec((Q, _BW), lambda g, n: (n_of(n), SSM_INNER // _BW + g)),
        pl.BlockSpec((Q, _BW), lambda g, n: (n_of(n), (SSM_INNER + SSM_GROUPS * 128) // _BW + g)),
        pl.BlockSpec((Q, _XW), lambda g, n: (n_of(n), P_Z // _XW + g)),
        pl.BlockSpec((Q, 128), lambda g, n: (n_of(n), 0)),
        pl.BlockSpec((1, 128), lambda g, n: (0, 0)),
        pl.BlockSpec((1, 128), lambda g, n: (0, 0)),
        pl.BlockSpec((SSD_GPS, SSM_HPG, 1, 1), lambda g, n: (g, 0, 0, 0)),
        pl.BlockSpec((1, _XW), lambda g, n: (0, g)),
    ]


def _ssd_group_inputs(gi, x_ref, b_ref, c_ref, z_ref, dt_ref, dtb_ref, al_ref, dk_ref, nw_ref):
    xs = slice(gi * SSM_GW, (gi + 1) * SSM_GW)
    bs = slice(gi * 128, (gi + 1) * 128)
    return (x_ref[:, xs].astype(F32), z_ref[:, xs].astype(F32), dt_ref[...],
            b_ref[:, bs], c_ref[:, bs],
            dtb_ref[...], al_ref[...], dk_ref[gi], nw_ref[:, xs])


def ssd_fwd(xact, proj, dtg, dtb, alog, dsk, nw, comm):
    T = xact.shape[0]
    nc = T // Q
    nx = len(comm)
    ng = SSM_GROUPS // SSD_GPS
    any_spec = pl.BlockSpec(memory_space=pl.ANY)

    def body(*refs):
        in_refs, src_refs = refs[:9], refs[9:9 + nx]
        yb_ref, st_ref = refs[9 + nx:11 + nx]
        out_refs, state, sems = refs[11 + nx:11 + 2 * nx], refs[11 + 2 * nx], refs[12 + 2 * nx:]
        g, n = pl.program_id(0), pl.program_id(1)
        exchange_start(comm, src_refs, out_refs, sems, (g == 0) & (n == 0))

        @pl.when(n == 0)
        def _():
            state[...] = jnp.zeros(state.shape, F32)

        for gi in range(SSD_GPS):
            lane0 = SSM_HPG * (SSD_GPS * pl.program_id(0) + gi)
            s = state[gi]
            st_ref[gi, 0] = s
            new_s, yb = ssd_step(lane0, s, *_ssd_group_inputs(gi, *in_refs))
            state[gi] = new_s
            yb_ref[:, gi * SSM_GW:(gi + 1) * SSM_GW] = yb.astype(yb_ref.dtype)

        exchange_finish(comm, src_refs, out_refs, sems, (g == ng - 1) & (n == nc - 1))

    return pl.pallas_call(
        body, name="ssd_fwd", grid=(ng, nc),
        out_shape=[jax.ShapeDtypeStruct((T, SSM_INNER), BF16),
                   jax.ShapeDtypeStruct((SSM_GROUPS, nc, 128, SSM_GW), F32)] + exchange_out_shapes(comm),
        in_specs=_ssd_in_specs(False, nc) + [any_spec] * nx,
        out_specs=[pl.BlockSpec((Q, _XW), lambda g, n: (n, g)),
                   pl.BlockSpec((SSD_GPS, 1, 128, SSM_GW), lambda g, n: (g, n, 0, 0))] + [any_spec] * nx,
        scratch_shapes=[pltpu.VMEM((SSD_GPS, 128, SSM_GW), F32)] + exchange_semaphores(comm),
        compiler_params=_params(("arbitrary", "arbitrary")),
    )(xact, xact, xact, proj, dtg, dtb, alog, dsk, nw, *[it[0] for it in comm])


def ssd_bwd(xact, proj, dtg, dtb, alog, dsk, nw, states, dyb, dproj, comm):
    T = xact.shape[0]
    nc = T // Q

    nx = len(comm)
    ng = SSM_GROUPS // SSD_GPS
    any_spec = pl.BlockSpec(memory_space=pl.ANY)

    def body(*refs):
        in_refs, (st_ref, dy_ref, _) = refs[:9], refs[9:12]
        src_refs, refs = refs[12:12 + nx], refs[12 + nx:]
        dx_ref, db_ref, dc_ref, dz_ref, ddt_ref, ddtb_ref, dal_ref, ddk_ref, dnw_ref = refs[:9]
        out_refs, dstate, sems = refs[9:9 + nx], refs[9 + nx], refs[10 + nx:]
        exchange_start(comm, src_refs, out_refs, sems, (pl.program_id(0) == 0) & (pl.program_id(1) == 0))

        @pl.when(pl.program_id(1) == 0)
        def _():
            dstate[...] = jnp.zeros(dstate.shape, F32)
            ddtb_ref[...] = jnp.zeros(ddtb_ref.shape, F32)
            dal_ref[...] = jnp.zeros(dal_ref.shape, F32)
            ddk_ref[...] = jnp.zeros(ddk_ref.shape, F32)
            dnw_ref[...] = jnp.zeros(dnw_ref.shape, F32)

        for gi in range(SSD_GPS):
            xs = slice(gi * SSM_GW, (gi + 1) * SSM_GW)
            bs = slice(gi * 128, (gi + 1) * 128)
            lane0 = SSM_HPG * (SSD_GPS * pl.program_id(0) + gi)
            ins = (st_ref[gi, 0],) + _ssd_group_inputs(gi, *in_refs)
            _, vjp = jax.vjp(functools.partial(ssd_step, lane0), *ins)
            ds, dx, dz, ddt, dbm, dcm, ddtb, dal, ddk, dnw = vjp((dstate[gi], dy_ref[:, xs].astype(F32)))
            dstate[gi] = ds
            dx_ref[:, xs] = dx.astype(dx_ref.dtype)
            db_ref[:, bs] = dbm.astype(db_ref.dtype)
            dc_ref[:, bs] = dcm.astype(dc_ref.dtype)
            dz_ref[:, xs] = dz.astype(dz_ref.dtype)
            ddt_ref[:, bs] = ddt
            ddtb_ref[gi] += ddtb
            dal_ref[gi] += dal
            ddk_ref[gi] += ddk
            dnw_ref[:, xs] += dnw

        exchange_finish(comm, src_refs, out_refs, sems,
                        (pl.program_id(0) == ng - 1) & (pl.program_id(1) == nc - 1))

    rev = lambda n: nc - 1 - n
    row_shape = jax.ShapeDtypeStruct((SSM_GROUPS, 1, 128), F32)
    row_spec = pl.BlockSpec((SSD_GPS, 1, 128), lambda g, n: (g, 0, 0))
    return pl.pallas_call(
        body, name="ssd_bwd", grid=(ng, nc),
        out_shape=[jax.ShapeDtypeStruct((T, SSM_INNER), BF16),
                   jax.ShapeDtypeStruct((T, SSM_GROUPS * 128), BF16),
                   jax.ShapeDtypeStruct((T, SSM_GROUPS * 128), BF16),
                   jax.ShapeDtypeStruct(dproj.shape, dproj.dtype),
                   jax.ShapeDtypeStruct((T, SSM_GROUPS * 128), F32),
                   row_shape, row_shape,
                   jax.ShapeDtypeStruct((SSM_GROUPS, SSM_HPG, 1, 1), F32),
                   jax.ShapeDtypeStruct((1, SSM_INNER), F32)] + exchange_out_shapes(comm),
        in_specs=_ssd_in_specs(True, nc) + [
            pl.BlockSpec((SSD_GPS, 1, 128, SSM_GW), lambda g, n: (g, rev(n), 0, 0)),
            pl.BlockSpec((Q, _XW), lambda g, n: (rev(n), g)),
            any_spec] + [any_spec] * nx,
        out_specs=[pl.BlockSpec((Q, _XW), lambda g, n: (rev(n), g)),
                   pl.BlockSpec((Q, _BW), lambda g, n: (rev(n), g)),
                   pl.BlockSpec((Q, _BW), lambda g, n: (rev(n), g)),
                   pl.BlockSpec((Q, _XW), lambda g, n: (rev(n), P_Z // _XW + g)),
                   pl.BlockSpec((Q, _BW), lambda g, n: (rev(n), g)),
                   row_spec, row_spec,
                   pl.BlockSpec((SSD_GPS, SSM_HPG, 1, 1), lambda g, n: (g, 0, 0, 0)),
                   pl.BlockSpec((1, _XW), lambda g, n: (0, g))] + [any_spec] * nx,
        scratch_shapes=[pltpu.VMEM((SSD_GPS, 128, SSM_GW), F32)] + exchange_semaphores(comm),
        input_output_aliases={11: 3},
        compiler_params=_params(("arbitrary", "arbitrary")),
    )(xact, xact, xact, proj, dtg, dtb, alog, dsk, nw, states, dyb, dproj, *[it[0] for it in comm])


ADAMW_WHOLE_ELEMS = 256 * 1024


def adamw(w, g, m, v, *, name):
    shape = w.shape
    parts = g.shape != shape
    nd = len(shape)
    if nd == 3 and shape[1] == 1 and w.size > ADAMW_WHOLE_ELEMS:
        assert not parts and shape[0] % 4 == 0
        grid = (4,)
        spec = g_spec = pl.BlockSpec((shape[0] // 4, 1, shape[2]), lambda i: (i, 0, 0))
    else:
        if w.size <= ADAMW_WHOLE_ELEMS:
            grid, tr = (1,), shape[-2]
        else:
            assert all(s == 1 for s in shape[:-2]) and shape[-2] % 256 == 0
            grid, tr = (shape[-2] // 256,), 256
        blk = tuple(shape[:-2]) + (tr, shape[-1])
        spec = pl.BlockSpec(blk, lambda i: (0,) * (nd - 2) + (i, 0))
        g_spec = pl.BlockSpec((N_DEV,) + blk[1:], lambda i: (0,) * (nd - 2) + (i, 0)) if parts else spec

    def body(w_ref, g_ref, m_ref, v_ref, go_ref, d_ref, nm_ref, nv_ref):
        if parts:
            g = g_ref[0:1].astype(F32)
            for j in range(1, N_DEV):
                g = g + g_ref[j:j + 1].astype(F32)
        else:
            g = g_ref[...]
        nm = ADAM_B1 * m_ref[...] + (1.0 - ADAM_B1) * g
        nv = ADAM_B2 * v_ref[...] + (1.0 - ADAM_B2) * jnp.square(g)
        m_hat = nm / (1.0 - ADAM_B1 ** ADAM_STEP)
        v_hat = nv / (1.0 - ADAM_B2 ** ADAM_STEP)
        go_ref[...] = g
        d_ref[...] = -ADAM_LR * (m_hat / (jnp.sqrt(v_hat) + ADAM_EPS) + ADAM_WD * w_ref[...])
        nm_ref[...] = nm
        nv_ref[...] = nv

    shp = jax.ShapeDtypeStruct(shape, F32)
    return pl.pallas_call(
        body, name=name, grid=grid,
        out_shape=[shp] * 4, in_specs=[spec, g_spec, spec, spec], out_specs=[spec] * 4,
        compiler_params=_params(("parallel",)),
    )(w, g, m, v)


def _pad_rows(a, rows):
    return jnp.pad(a, ((0, rows - a.shape[0]), (0, 0)))


WIN_W = 1408
N_IN = IN_WIDTH // N_DEV
_A6 = OFF_DT - 6 * N_IN
_C6 = 7 * N_IN - OFF_GA


_WIN_OFFSETS = (0, 4, 8, 12, 16, 20, None, 124)


def _w_in_window(shard, me):
    rows = shard.shape[0]
    z = lambda n: jnp.zeros((rows, n), shard.dtype)

    def plain(off):
        return lambda s: jnp.pad(s, ((0, 0), (off, WIN_W - N_IN - off)))

    def split(s):
        return jnp.concatenate([z(24), s[:, :_A6], s[:, _A6 + 32:], z(4), s[:, _A6:_A6 + 32], z(96)], axis=1)

    return lax.switch(me, [split if off is None else plain(off) for off in _WIN_OFFSETS], shard)


def _w_in_from_window(window, me):
    def plain(off):
        return lambda w: w[:, off:off + N_IN]

    def split(w):
        return jnp.concatenate([w[:, 24:24 + _A6], w[:, 1280:1312], w[:, 24 + _A6:24 + _A6 + _C6]], axis=1)

    return lax.switch(me, [split if off is None else plain(off) for off in _WIN_OFFSETS], window)


def _w_all_from_windows(g):
    def merge_first(p, t):
        return jnp.concatenate([p[:, :128] + t, p[:, 128:]], axis=1)

    parts = [g[0][:, :1280]]
    for j in range(1, 6):
        parts.append(merge_first(g[j][:, :1280], g[j - 1][:, 1280:]))
    p6 = merge_first(g[6][:, :1280], g[5][:, 1280:])
    parts.append(jnp.concatenate([p6[:, :1152], p6[:, 1152:] + g[7][:, :128]], axis=1))
    parts.append(g[7][:, 128:])
    parts.append(g[6][:, 1280:])
    return jnp.concatenate(parts, axis=1)


def _windows_of_w_all(gw):
    wins = [gw[:, 1280 * j:1280 * j + WIN_W] for j in range(6)]
    wins.append(jnp.concatenate([gw[:, 7680:8960], gw[:, PROJ_W:]], axis=1))
    wins.append(gw[:, 8832:PROJ_W])
    return jnp.stack(wins)


def kernel(x, c, w_mod, b_mod, w_in, gm_norm_w, gm_ws, gm_bs, conv_w, conv_b, dt_bias, a_log, d_skip, ssm_norm_w, w_branch_gm, w_branch_ssm, w_out, w_ff1, w_ff2, final_norm_w, loss_target, m_w_mod, m_b_mod, m_w_in, m_gm_norm_w, m_gm_ws, m_gm_bs, m_conv_w, m_conv_b, m_dt_bias, m_a_log, m_d_skip, m_ssm_norm_w, m_w_branch_gm, m_w_branch_ssm, m_w_out, m_w_ff1, m_w_ff2, m_final_norm_w, v_w_mod, v_b_mod, v_w_in, v_gm_norm_w, v_gm_ws, v_gm_bs, v_conv_w, v_conv_b, v_dt_bias, v_a_log, v_d_skip, v_ssm_norm_w, v_w_branch_gm, v_w_branch_ssm, v_w_out, v_w_ff1, v_w_ff2, v_final_norm_w):
    T = x.shape[1]
    me = 4 * lax.axis_index("x") + 2 * lax.axis_index("y") + lax.axis_index("c")
    x2 = x[0]
    tgt = loss_target[0]
    n_in = IN_WIDTH // N_DEV
    n_mod = N_MOD * D // N_DEV
    n_cv = CONV_DIM // N_DEV

    c_all, conv_w_full = exchange(
        [(c.reshape(8, 128), _whole, (N_DEV, 8, 128), _slot),
         (conv_w[0], _whole, (N_DEV, CONV_K, n_cv), _slot)], name="gather_c_convw")
    c_all = c_all.reshape(N_DEV, D)
    conv_w_full = conv_w_full.transpose(1, 0, 2).reshape(CONV_K, CONV_DIM)

    win = _w_in_window(w_in[0].astype(BF16), me)
    late_weights = [
        (w_branch_gm[0].astype(BF16), _whole, (D, D), _rows(D // N_DEV)),
        (w_branch_ssm[0].astype(BF16), _whole, (SSM_INNER, D), _rows(SSM_INNER // N_DEV)),
        (w_out[0].astype(BF16), _whole, (D, D), _rows(D // N_DEV)),
        (w_ff1[0].astype(BF16), _whole, (D, D_FF), _cols(D_FF // N_DEV)),
        (w_ff2[0].astype(BF16), _whole, (D_FF, D), _rows(D_FF // N_DEV))]

    c_pad = _pad_rows(c_all, 128)
    b_mine = lax.dynamic_slice(b_mod, (0, me * n_mod), (1, n_mod))

    def mod_fn(cp, w, b):
        ca = cp * jax.nn.sigmoid(cp)
        return (jnp.dot(ca, w, precision=HIGHEST, preferred_element_type=F32) + b,)

    (mod_part,) = whole_call(mod_fn, [c_pad, w_mod[0], b_mine], [((128, n_mod), F32)], name="mod_fwd")
    gmod = gather_blocks(mod_part[:N_DEV], name="gather_mod")
    mod = lax.dynamic_index_in_dim(gmod, me, axis=1, keepdims=False).reshape(N_MOD, D)
    sh1, sc1, gt1, sh2, sc2, gt2 = [mod[i:i + 1] for i in range(N_MOD)]

    h, gwin = modulate_with_gather(x2, sc1, sh1, win, tm=256, name="modulate1_gather_w_in")
    w_all = _w_all_from_windows(gwin)
    w_dt = w_all[:, PROJ_W:]
    proj = matmul(h, w_all, "nn", BF16, name="mm_proj", n=PROJ_W, tm=K1_TM)
    dtg = matmul(h, w_dt, "nn", F32, name="mm_dt")
    ws = gm_ws[0]
    bs3 = gm_bs[0].reshape(GM_GROUPS, Q, 1)
    sgu_rows = [(proj, D, P_U), (proj, D, P_V)]
    (ya,) = rowwise_call(fwd_body(fn_sgu), sgu_rows, [gm_norm_w, ws, bs3], [(D, BF16)], [],
                         tm=256, name="sgu_fwd")
    xact = conv_fwd(proj, conv_w_full, conv_b)
    dtb4 = jnp.pad(dt_bias, ((0, 0), (0, 96)))
    alog4 = jnp.pad(a_log, ((0, 0), (0, 96)))
    dsk4 = d_skip.reshape(SSM_GROUPS, SSM_HPG, 1, 1)
    yb, states, w_gm_f, w_ssm_f, w_out_f, w_ff1_f, w_ff2_f = ssd_fwd(
        xact, proj, dtg, dtb4, alog4, dsk4, ssm_norm_w, late_weights)
    pa = matmul(ya, w_gm_f, "nn", BF16, name="mm_branch_gm", tm=K1_TM)
    gate_rows = [(proj, D, P_GA), (proj, D, P_GB)]
    mixed, pb = rowwise_call(
        lambda r, fl: (fn_mix(*r) + (r[3],), ()), gate_rows + [pa], [], [(D, BF16), (D, BF16)], [],
        tm=FUSED_TM, name="branch_ssm_mix", mm=(yb, w_ssm_f, "nn", 3, None), tk=SSM_INNER)
    x1, h2, o = rowwise_call(
        lambda r, fl: (fn_res_modulate(*r, *fl) + (r[1],), ()), [x2], [gt1, sc2, sh2],
        [(D, F32), (D, BF16), (D, F32)], [], tm=FUSED_TM, name="out_res_modulate2",
        mm=(mixed, w_out_f, "nn", 1, None))
    f = matmul(h2, w_ff1_f, "nn", BF16, name="mm_ff1", tm=K1_TM)

    dx1, dgf, loss_v, dgt2, dfnw = rowwise_call(
        final_body, [x1, tgt], [gt2, final_norm_w.reshape(1, D)], [(D, F32), (D, BF16)],
        [(1, 128), (1, D), (1, D)], tm=FUSED_TM, name="ff2_loss_bwd", mm=(f, w_ff2_f, "nn", 1, relu2_tile),
        tk=D_FF)
    df = matmul(dgf, w_ff2_f, "nt", BF16, name="mm_ff2_dgrad", epi=relu2_grad_tile, epi_ins=(f,), tm=K1_TM)
    gw_ff2 = matmul(f, dgf, "tn", BF16, name="mm_ff2_wgrad", tk=WGRAD_TK, a_pro=relu2_tile)
    gw_ff1 = matmul(h2, df, "tn", BF16, name="mm_ff1_wgrad", tk=WGRAD_TK)

    def res_mod_bwd(r, fl):
        xv, ov, dx1v, dh2v = r
        _, vjp = jax.vjp(fn_res_modulate, xv, ov, *fl)
        dxv, dov, dg1, dsc, dsh = vjp((dx1v, dh2v))
        return (dxv, dov), (dg1, dsc, dsh)

    dxa, do, dgt1, dsc2, dsh2 = rowwise_call(
        res_mod_bwd, [x2, o, dx1], [gt1, sc2, sh2], [(D, F32), (D, BF16)],
        [(1, D), (1, D), (1, D)], tm=FUSED_TM, name="ff1_dgrad_res_modulate2_bwd",
        mm=(df, w_ff1_f, "nt", 3, None), tk=D_FF)
    gw_out = matmul(mixed, do, "tn", BF16, name="mm_out_wgrad", tk=WGRAD_TK)
    dproj = lax.empty((T, ALL_W), BF16)

    def mix_bwd(r, fl):
        dga, dgb, dpa, dpb = bwd_body(fn_mix, 4)(r, fl)[0]
        return (jnp.concatenate([dga, dgb], axis=1), dpa, dpb), ()

    dproj, dpa, dpb = rowwise_call(
        mix_bwd, gate_rows + [pa, pb], [], [(dproj, 2 * D, P_GA), (D, BF16), (D, BF16)], [],
        tm=FUSED_TM, name="out_dgrad_mix_bwd", mm=(do, w_out_f, "nt", 4, None))
    gw_gm = matmul(ya, dpa, "tn", BF16, name="mm_branch_gm_wgrad", tk=WGRAD_TK)
    dyb = matmul(dpb, w_ssm_f, "nt", BF16, name="mm_branch_ssm_dgrad", tm=K1_TM)
    gw_ssm = matmul(yb, dpb, "tn", BF16, name="mm_branch_ssm_wgrad", tk=WGRAD_TK)

    def sgu_bwd(r, fl):
        (du, dv), acc = bwd_body(fn_sgu, 2)(r, fl)
        return (jnp.concatenate([du, dv], axis=1),), acc

    dproj, dgnw, dws, dbs = rowwise_call(
        sgu_bwd, sgu_rows, [gm_norm_w, ws, bs3], [(dproj, 2 * D, P_U)],
        [(1, D), (GM_GROUPS, Q, Q), (GM_GROUPS, Q, 1)], tm=FUSED_TM, name="branch_gm_dgrad_sgu_bwd",
        mm=(dpa, w_gm_f, "nt", 2, None))
    early_grads = [
        (gw_gm, _rows(D // N_DEV), (N_DEV, D // N_DEV, D), _slot),
        (gw_ssm, _rows(SSM_INNER // N_DEV), (N_DEV, SSM_INNER // N_DEV, D), _slot),
        (gw_out, _rows(D // N_DEV), (N_DEV, D // N_DEV, D), _slot),
        (gw_ff1, _cols(D_FF // N_DEV), (N_DEV, D, D_FF // N_DEV), _slot),
        (gw_ff2, _rows(D_FF // N_DEV), (N_DEV, D_FF // N_DEV, D), _slot),
        (_pack_rows([dgnw, dws, dbs], EARLY_ROWS), _whole, (N_DEV, sum(EARLY_ROWS), 128), _slot)]
    (dxs, dbm, dcm, dproj, ddt8, ddtb, dalog, ddsk, dsnw,
     r_gm, r_ssm, r_out, r_ff1, r_ff2, early_all) = ssd_bwd(
        xact, proj, dtg, dtb4, alog4, dsk4, ssm_norm_w, states, dyb, dproj, early_grads)
    dconv_w, dconv_b = [], []
    for nm, dact_part, col0 in (("xs", dxs, 0), ("b", dbm, SSM_INNER), ("c", dcm, SSM_INNER + SSM_GROUPS * 128)):
        dproj, dcw, dcb = conv_bwd(proj, dact_part, col0, conv_w_full, conv_b, dproj, name="conv_bwd_" + nm)
        dconv_w.append(dcw)
        dconv_b.append(dcb)
    dconv_w = jnp.concatenate(dconv_w, axis=1)
    dconv_b = jnp.concatenate(dconv_b, axis=1)
    (dproj,) = rowwise_call(
        lambda r, fl: ((functools.reduce(jnp.add, r),), ()),
        [(ddt8, 128, 128 * g) for g in range(SSM_GROUPS)], [], [(dproj, 128, PROJ_W)], [],
        tm=1024, name="ddt_into_dproj")
    gw_all = matmul(h, dproj, "tn", BF16, name="mm_in_wgrad", tn=1152, tk=WGRAD_TK)
    mid_pack = _pack_rows([dconv_w, dconv_b, jnp.sum(ddtb, axis=0), jnp.sum(dalog, axis=0), ddsk, dsnw, dfnw,
                           jnp.concatenate([dgt1, dsh2, dsc2, dgt2], axis=0)], MID_ROWS)
    dh, r_in, mid_all = matmul(
        dproj, w_all, "nt", BF16, name="mm_in_dgrad", tm=2048, tk=1152,
        comm=[(_windows_of_w_all(gw_all), _slot, (N_DEV, D, WIN_W), _slot),
              (mid_pack, _whole, (N_DEV, sum(MID_ROWS), 128), _slot)])
    grad_x, dsc1, dsh1 = rowwise_call(grad_x_body, [x2, dh, dxa], [sc1, sh1], [(D, F32)],
                                      [(1, D), (1, D)], tm=256, name="modulate1_bwd")

    g_w_in = _w_in_from_window(sum_devices(r_in, tr=256, name="sum_w_in_grads"), me).reshape(1, D, n_in)

    late_all = gather_blocks(_pack_rows([dsh1, dsc1, loss_v], LATE_ROWS), name="gather_dmod1_loss")
    s_early = _unpack_rows(sum_devices(early_all, tr=early_all.shape[1], name="sum_small_early"), EARLY_ROWS)
    s_mid = _unpack_rows(sum_devices(mid_all, tr=mid_all.shape[1], name="sum_small_mid"), MID_ROWS)
    s_late = _unpack_rows(sum_devices(late_all, tr=late_all.shape[1], name="sum_small_late"), LATE_ROWS)
    g_gm_norm_w = s_early[0][:D].reshape(1, D)
    g_gm_ws = s_early[1].reshape(GM_GROUPS * Q, Q)
    g_gm_bs = s_early[2][:GM_GROUPS * Q].reshape(GM_GROUPS, Q)
    g_conv_w_full = s_mid[0].reshape(CONV_K, CONV_DIM)
    g_conv_w = lax.dynamic_slice(g_conv_w_full, (0, me * n_cv), (CONV_K, n_cv))
    g_conv_b = s_mid[1].reshape(1, CONV_DIM)
    g_dt_bias = s_mid[2][:32].reshape(1, 32)
    g_a_log = s_mid[3][:32].reshape(1, 32)
    g_d_skip = s_mid[4][:32].reshape(1, 32)
    g_ssm_norm_w = s_mid[5].reshape(1, SSM_INNER)
    g_final_norm_w = s_mid[6][:D].reshape(1, D)
    g_b_mod = jnp.concatenate([s_late[0][:D], s_late[1][:D], s_mid[7]]).reshape(1, N_MOD * D)

    dmod_all = jnp.concatenate(
        [late_all.reshape(N_DEV, -1)[:, :2 * D],
         mid_all[:, sum(MID_ROWS[:7]):].reshape(N_DEV, 4 * D)], axis=1)
    dmod_mine = _pad_rows(lax.dynamic_slice(dmod_all, (0, me * n_mod), (N_DEV, n_mod)), 128)

    def wmod_grad_fn(cp, dm):
        ca = cp * jax.nn.sigmoid(cp)
        return (lax.dot_general(ca, dm, (((0,), (0,)), ((), ())), precision=HIGHEST,
                                preferred_element_type=F32),)

    (g_w_mod,) = whole_call(wmod_grad_fn, [c_pad, dmod_mine], [((D, n_mod), F32)], name="w_mod_grad")

    upd = {}

    def step(name, w, g, m, v, parts=False):
        upd[name] = adamw(w, g if parts else g.reshape(w.shape), m, v, name="adamw_" + name)

    step("w_mod", w_mod, g_w_mod, m_w_mod, v_w_mod)
    step("b_mod", b_mod, g_b_mod, m_b_mod, v_b_mod)
    col_major = lambda a: jnp.transpose(a, (2, 0, 1))
    upd["w_in"] = tuple(jnp.transpose(o, (1, 2, 0)) for o in adamw(
        col_major(w_in), col_major(g_w_in), col_major(m_w_in), col_major(v_w_in), name="adamw_w_in"))
    step("gm_norm_w", gm_norm_w, g_gm_norm_w, m_gm_norm_w, v_gm_norm_w)
    step("gm_ws", gm_ws, g_gm_ws, m_gm_ws, v_gm_ws)
    step("gm_bs", gm_bs, g_gm_bs, m_gm_bs, v_gm_bs)
    step("conv_w", conv_w, g_conv_w, m_conv_w, v_conv_w)
    step("conv_b", conv_b, g_conv_b, m_conv_b, v_conv_b)
    step("dt_bias", dt_bias, g_dt_bias, m_dt_bias, v_dt_bias)
    step("a_log", a_log, g_a_log, m_a_log, v_a_log)
    step("d_skip", d_skip, g_d_skip, m_d_skip, v_d_skip)
    step("ssm_norm_w", ssm_norm_w, g_ssm_norm_w, m_ssm_norm_w, v_ssm_norm_w)
    step("w_branch_gm", w_branch_gm, r_gm, m_w_branch_gm, v_w_branch_gm, parts=True)
    step("w_branch_ssm", w_branch_ssm, r_ssm, m_w_branch_ssm, v_w_branch_ssm, parts=True)
    step("w_out", w_out, r_out, m_w_out, v_w_out, parts=True)
    step("w_ff1", w_ff1, r_ff1, m_w_ff1, v_w_ff1, parts=True)
    step("w_ff2", w_ff2, r_ff2, m_w_ff2, v_w_ff2, parts=True)
    step("final_norm_w", final_norm_w.reshape(1, D), g_final_norm_w, m_final_norm_w.reshape(1, D),
         v_final_norm_w.reshape(1, D))
    upd["final_norm_w"] = tuple(a.reshape(D) for a in upd["final_norm_w"])

    loss = s_late[2][0]
    order = ["w_mod", "b_mod", "w_in", "gm_norm_w", "gm_ws", "gm_bs", "conv_w", "conv_b", "dt_bias", "a_log",
             "d_skip", "ssm_norm_w", "w_branch_gm", "w_branch_ssm", "w_out", "w_ff1", "w_ff2", "final_norm_w"]
    return (loss, grad_x.reshape(1, T, D),
            *[upd[n][0] for n in order], *[upd[n][1] for n in order],
            *[upd[n][2] for n in order], *[upd[n][3] for n in order])
```

```python
import functools

import jax
import jax.numpy as jnp
from jax import lax
from jax.experimental import pallas as pl
from jax.experimental.pallas import tpu as pltpu

F32 = jnp.float32
BF16 = jnp.bfloat16
MESH = pl.DeviceIdType.MESH
HIGHEST = lax.Precision.HIGHEST

N_DEV = 8
D = 1024
Q = 128
GM_GROUPS = 8
SSM_INNER = 2048
SSM_GROUPS = 8
SSM_HPG = 4
SSM_P = 64
SSM_GW = SSM_HPG * SSM_P
CONV_DIM = 4096
CONV_K = 4
D_FF = 4096
N_MOD = 6
EPS = 1e-6
IN_WIDTH = 10272
OFF_DT = 8192
OFF_GA = 8224
PROJ_W = 10240
ALL_W = 10368
P_U, P_V, P_Z, P_XBC, P_GA, P_GB = 0, 1024, 2048, 4096, 8192, 9216

ADAM_LR = 0.001
ADAM_B1 = 0.9
ADAM_B2 = 0.999
ADAM_EPS = 1e-08
ADAM_WD = 0.01
ADAM_STEP = 10

VMEM_LIMIT_BYTES = 48 * 1024 * 1024
K1_TM = 2048
FUSED_TM = 512
WGRAD_TK = 2048
EARLY_ROWS = (8, 1024, 8)
MID_ROWS = (128, 32, 8, 8, 8, 16, 8, 32)
LATE_ROWS = (8, 8, 8)


def _pack_rows(arrs, rows):
    def rows128(a, r):
        a = a.reshape(-1)
        return jnp.pad(a, (0, r * 128 - a.shape[0])).reshape(r, 128)
    return jnp.concatenate([rows128(a, r) for a, r in zip(arrs, rows)], axis=0)


def _unpack_rows(s, rows):
    out, o = [], 0
    for r in rows:
        out.append(s[o:o + r].reshape(-1))
        o += r
    return out


def _params(sem=None):
    return pltpu.CompilerParams(dimension_semantics=sem, vmem_limit_bytes=VMEM_LIMIT_BYTES)


def _dg(a, b, ca, cb):
    return lax.dot_general(a.astype(BF16), b.astype(BF16), (((ca,), (cb,)), ((), ())),
                           preferred_element_type=F32)


@jax.custom_vjp
def dot_nn(a, b):
    return _dg(a, b, 1, 0)


@jax.custom_vjp
def dot_nt(a, b):
    return _dg(a, b, 1, 1)


@jax.custom_vjp
def dot_tn(a, b):
    return _dg(a, b, 0, 0)


def _like(ct, primal):
    return ct.astype(primal.dtype)


dot_nn.defvjp(lambda a, b: (dot_nn(a, b), (a, b)),
              lambda r, g: (_like(dot_nt(g, r[1]), r[0]), _like(dot_tn(r[0], g), r[1])))
dot_nt.defvjp(lambda a, b: (dot_nt(a, b), (a, b)),
              lambda r, g: (_like(dot_nn(g, r[1]), r[0]), _like(dot_tn(g, r[0]), r[1])))
dot_tn.defvjp(lambda a, b: (dot_tn(a, b), (a, b)),
              lambda r, g: (_like(dot_nt(r[1], g), r[0]), _like(dot_nn(r[0], g), r[1])))


def _rms(x):
    return x * lax.rsqrt(jnp.mean(x * x, axis=-1, keepdims=True) + EPS)


def _softplus(x):
    return jnp.maximum(x, 0.0) + jnp.log1p(jnp.exp(-jnp.abs(x)))


def _rows(n):
    return lambda ref, j: ref.at[pl.ds(pl.multiple_of(j * n, n), n)]


def _cols(n):
    return lambda ref, j: ref.at[:, pl.ds(pl.multiple_of(j * n, n), n)]


def _slot(ref, j):
    return ref.at[j]


def _whole(ref, j):
    return ref


def exchange(items, *, name):
    n = len(items)

    def body(*refs):
        exchange_in_body(items, refs[:n], refs[n:2 * n], refs[2 * n:], True, True)

    return pl.pallas_call(
        body, name=name,
        out_shape=exchange_out_shapes(items),
        in_specs=[pl.BlockSpec(memory_space=pl.ANY)] * n,
        out_specs=[pl.BlockSpec(memory_space=pl.ANY)] * n,
        scratch_shapes=exchange_semaphores(items),
    )(*[it[0] for it in items])


def exchange_out_shapes(items):
    return [jax.ShapeDtypeStruct(tuple(shape), src.dtype) for (src, _, shape, _) in items]


def exchange_semaphores(items):
    n = len(items)
    return [pltpu.SemaphoreType.DMA((n, N_DEV - 1)), pltpu.SemaphoreType.DMA((n, N_DEV - 1)),
            pltpu.SemaphoreType.DMA((n,))]


def _exchange_copies(items, src_refs, out_refs, sems):
    send_sems, recv_sems, local_sems = sems
    x = lax.axis_index("x")
    y = lax.axis_index("y")
    c = lax.axis_index("c")
    me = 4 * x + 2 * y + c
    local = [pltpu.make_async_copy(src_win(src_refs[i], me), dst_win(out_refs[i], me), local_sems.at[i])
             for i, (_, src_win, _, dst_win) in enumerate(items)]
    remote = []
    for i, (_, src_win, _, dst_win) in enumerate(items):
        for k in range(1, N_DEV):
            px = lax.rem(x + ((k >> 2) & 1), 2)
            py = lax.rem(y + ((k >> 1) & 1), 2)
            pc = lax.rem(c + (k & 1), 2)
            peer = 4 * px + 2 * py + pc
            remote.append(pltpu.make_async_remote_copy(
                src_ref=src_win(src_refs[i], peer), dst_ref=dst_win(out_refs[i], me),
                send_sem=send_sems.at[i, k - 1], recv_sem=recv_sems.at[i, k - 1],
                device_id=(px, py, pc), device_id_type=MESH))
    return local, remote


def _when(cond, fn):
    if cond is True:
        fn()
    else:
        pl.when(cond)(fn)


def exchange_start(items, src_refs, out_refs, sems, cond):
    def start():
        local, remote = _exchange_copies(items, src_refs, out_refs, sems)
        for cp in local + remote:
            cp.start()
    _when(cond, start)


def exchange_finish(items, src_refs, out_refs, sems, cond):
    def finish():
        local, remote = _exchange_copies(items, src_refs, out_refs, sems)
        for cp in remote:
            cp.wait_send()
        for cp in remote:
            cp.wait_recv()
        for cp in local:
            cp.wait()
    _when(cond, finish)


def exchange_in_body(items, src_refs, out_refs, sems, first, last):
    exchange_start(items, src_refs, out_refs, sems, first)
    exchange_finish(items, src_refs, out_refs, sems, last)


def gather_blocks(src, *, name):
    return exchange([(src, _whole, (N_DEV,) + src.shape, _slot)], name=name)[0]


def _two_level_gather(src_ref, out_ref, send_sems, recv_sems, local_sem):
    x = lax.axis_index("x")
    y = lax.axis_index("y")
    c = lax.axis_index("c")
    me, sibling = (x, y, c), (x, y, 1 - c)
    chips = [(1 - x, y), (x, 1 - y), (1 - x, 1 - y)]

    def slot(px, py, pc):
        return out_ref.at[4 * px + 2 * py + pc]

    def copy(k, block, to, src=None):
        return pltpu.make_async_remote_copy(
            src_ref=slot(*block) if src is None else src, dst_ref=slot(*block),
            send_sem=send_sems.at[k], recv_sem=recv_sems.at[k], device_id=to, device_id_type=MESH)

    def own_copies():
        return ([copy(0, me, sibling, src=src_ref)]
                + [copy(1 + j, me, (*chip, c), src=src_ref) for j, chip in enumerate(chips)])

    def start():
        pltpu.make_async_copy(src_ref, slot(*me), local_sem).start()
        for cp in own_copies():
            cp.start()

    def finish():
        passed = [copy(4 + j, (*chip, c), sibling) for j, chip in enumerate(chips)]
        for j, chip in enumerate(chips):
            copy(1 + j, (*chip, c), me).wait_recv()
            passed[j].start()
        copy(0, sibling, me).wait_recv()
        for j, chip in enumerate(chips):
            copy(4 + j, (*chip, 1 - c), me).wait_recv()
        for cp in own_copies() + passed:
            cp.wait_send()
        pltpu.make_async_copy(src_ref, slot(*me), local_sem).wait()

    return start, finish


_TWO_LEVEL_SEMS = [pltpu.SemaphoreType.DMA((N_DEV - 1,)), pltpu.SemaphoreType.DMA((N_DEV - 1,)),
                   pltpu.SemaphoreType.DMA(())]


def modulate_with_gather(x2, sc, sh, src, *, tm, name):
    T, width = x2.shape
    n = T // tm

    def body(x_ref, sc_ref, sh_ref, src_ref, h_ref, out_ref, *sems):
        start, finish = _two_level_gather(src_ref, out_ref, *sems)
        pl.when(pl.program_id(0) == 0)(start)
        (h,) = fn_modulate(x_ref[...], sc_ref[...], sh_ref[...])
        h_ref[...] = h.astype(h_ref.dtype)
        pl.when(pl.program_id(0) == n - 1)(finish)

    row = pl.BlockSpec((tm, width), lambda i: (i, 0))
    full = pl.BlockSpec((1, width), lambda i: (0, 0))
    any_spec = pl.BlockSpec(memory_space=pl.ANY)
    return pl.pallas_call(
        body, name=name, grid=(n,),
        out_shape=[jax.ShapeDtypeStruct((T, width), BF16),
                   jax.ShapeDtypeStruct((N_DEV,) + src.shape, src.dtype)],
        in_specs=[row, full, full, any_spec], out_specs=[row, any_spec],
        scratch_shapes=_TWO_LEVEL_SEMS,
        compiler_params=_params(("arbitrary",)),
    )(x2, sc, sh, src)


def sum_devices(g, *, tr, name):
    _, R, C = g.shape

    def body(g_ref, o_ref):
        acc = g_ref[0].astype(F32)
        for j in range(1, N_DEV):
            acc = acc + g_ref[j].astype(F32)
        o_ref[...] = acc

    return pl.pallas_call(
        body, name=name, grid=(R // tr,),
        out_shape=jax.ShapeDtypeStruct((R, C), F32),
        in_specs=[pl.BlockSpec((N_DEV, tr, C), lambda i: (0, i, 0))],
        out_specs=pl.BlockSpec((tr, C), lambda i: (i, 0)),
        compiler_params=_params(("parallel",)),
    )(g)


def matmul(a, b, mode, out_dtype, *, name, tm=1024, tn=1024, tk=1024, n=None, comm=None,
           a_pro=None, epi=None, epi_ins=()):
    if mode == "nn":
        (M, K), (K2, N) = a.shape, b.shape
    elif mode == "nt":
        (M, K), (N, K2) = a.shape, b.shape
    else:
        (K, M), (K2, N) = a.shape, b.shape
    assert K == K2
    N = N if n is None else n
    tm, tn, tk = min(tm, M), min(tn, N), min(tk, K)
    assert M % tm == 0 and N % tn == 0 and K % tk == 0, (name, M, N, K, tm, tn, tk)
    nk = K // tk
    if mode == "tn":
        a_spec = pl.BlockSpec((tk, tm), lambda i, j, k: (k, i))
    else:
        a_spec = pl.BlockSpec((tm, tk), lambda i, j, k: (i, k))
    if mode == "nt":
        b_spec = pl.BlockSpec((tn, tk), lambda i, j, k: (j, k))
    else:
        b_spec = pl.BlockSpec((tk, tn), lambda i, j, k: (k, j))
    dims = {"nn": (1, 0), "nt": (1, 1), "tn": (0, 0)}[mode]
    items = list(comm) if comm else []
    nx = len(items)
    ne = len(epi_ins)
    gm, gn = M // tm, N // tn
    any_spec = pl.BlockSpec(memory_space=pl.ANY)
    o_spec = pl.BlockSpec((tm, tn), lambda i, j, k: (i, j))

    def body(*refs):
        a_ref, b_ref, e_refs = refs[0], refs[1], refs[2:2 + ne]
        refs = refs[2 + ne:]
        src_refs, o_ref, out_refs = refs[:nx], refs[nx], refs[1 + nx:1 + 2 * nx]
        acc_ref, sems = refs[1 + 2 * nx], refs[2 + 2 * nx:]
        i, j, k = pl.program_id(0), pl.program_id(1), pl.program_id(2)
        if items:
            exchange_start(items, src_refs, out_refs, sems, (i == 0) & (j == 0) & (k == 0))
        a_tile = a_ref[...] if a_pro is None else a_pro(a_ref[...])
        part = lax.dot_general(a_tile, b_ref[...], (((dims[0],), (dims[1],)), ((), ())),
                               preferred_element_type=F32)

        def finish(acc):
            if epi is not None:
                acc = epi(acc, *[e[...] for e in e_refs])
            o_ref[...] = acc.astype(o_ref.dtype)

        if nk == 1:
            finish(part)
        else:
            @pl.when(k == 0)
            def _():
                acc_ref[...] = part

            @pl.when((k > 0) & (k < nk - 1))
            def _():
                acc_ref[...] += part

            @pl.when(k == nk - 1)
            def _():
                finish(acc_ref[...] + part)

        if items:
            exchange_finish(items, src_refs, out_refs, sems, (i == gm - 1) & (j == gn - 1) & (k == nk - 1))

    res = pl.pallas_call(
        body, name=name, grid=(gm, gn, nk),
        out_shape=[jax.ShapeDtypeStruct((M, N), out_dtype)] + exchange_out_shapes(items),
        in_specs=[a_spec, b_spec] + [o_spec] * ne + [any_spec] * nx,
        out_specs=[o_spec] + [any_spec] * nx,
        scratch_shapes=[pltpu.VMEM((tm, tn) if nk > 1 else (8, 128), F32)]
        + (exchange_semaphores(items) if items else []),
        compiler_params=_params(("arbitrary",) * 3 if items else ("parallel", "parallel", "arbitrary")),
    )(a, b, *epi_ins, *[it[0] for it in items])
    return res if items else res[0]


def rowwise_call(body_fn, rows, fulls, row_outs, acc_outs, *, tm, name, mm=None, tk=1024):
    rows = [r if isinstance(r, tuple) else (r, r.shape[1], 0) for r in rows]
    T = rows[0][0].shape[0]
    tm = min(tm, T)
    assert T % tm == 0
    n_r, n_f, n_ro = len(rows), len(fulls), len(row_outs)
    into = [(k, ro) for k, ro in enumerate(row_outs) if len(ro) == 3]
    n_b = len(into)
    n_mm, nk = 0, 1
    if mm is not None:
        a, b, mode, pos, a_pro = mm
        n_mm = 2
        K = a.shape[1]
        N = b.shape[1] if mode == "nn" else b.shape[0]
        tk = min(tk, K)
        assert K % tk == 0 and a.shape[0] == T
        nk = K // tk
        b_contract = 0 if mode == "nn" else 1

    def row_body(refs, product):
        r_refs = refs[:n_r]
        f_refs = refs[n_r:n_r + n_f]
        refs = refs[n_r + n_f + n_b:]
        ro_refs = refs[:n_ro]
        ao_refs = refs[n_ro:n_ro + len(acc_outs)]
        r_vals = [r[...].astype(F32) for r in r_refs]
        if product is not None:
            r_vals.insert(pos, product)
        f_vals = [f[...].astype(F32) for f in f_refs]
        ro, ao = body_fn(r_vals, f_vals)
        for ref, v in zip(ro_refs, ro):
            ref[...] = v.astype(ref.dtype)
        if ao_refs:
            @pl.when(pl.program_id(0) == 0)
            def _():
                for ref in ao_refs:
                    ref[...] = jnp.zeros(ref.shape, F32)
            for ref, v in zip(ao_refs, ao):
                ref[...] += v.reshape(ref.shape)

    def body(*refs):
        if mm is None:
            return row_body(refs, None)
        a_ref, b_ref, rest, acc_ref = refs[0], refs[1], refs[2:-1], refs[-1]
        k = pl.program_id(1)
        a_tile = a_ref[...] if a_pro is None else a_pro(a_ref[...])
        part = lax.dot_general(a_tile, b_ref[...], (((1,), (b_contract,)), ((), ())),
                               preferred_element_type=F32)
        if nk == 1:
            return row_body(rest, part)

        @pl.when(k == 0)
        def _():
            acc_ref[...] = part

        @pl.when((k > 0) & (k < nk - 1))
        def _():
            acc_ref[...] += part

        @pl.when(k == nk - 1)
        def _():
            row_body(rest, acc_ref[...] + part)

    def full_spec(shape):
        nd = len(shape)
        return pl.BlockSpec(tuple(shape), lambda i, *_: (0,) * nd)

    def row_spec(w, off):
        return pl.BlockSpec((tm, w), functools.partial(lambda i, *_, o: (i, o), o=off // w))

    in_specs = []
    if mm is not None:
        in_specs.append(pl.BlockSpec((tm, tk), lambda i, k: (i, k)))
        in_specs.append(pl.BlockSpec((tk, N), lambda i, k: (k, 0)) if mode == "nn" else
                        pl.BlockSpec((N, tk), lambda i, k: (0, k)))
    in_specs += [row_spec(w, off) for (_, w, off) in rows]
    in_specs += [full_spec(f.shape) for f in fulls]
    in_specs += [pl.BlockSpec(memory_space=pl.ANY)] * n_b
    out_specs, out_shape = [], []
    for ro in row_outs:
        if len(ro) == 3:
            buf, w, off = ro
            out_specs.append(row_spec(w, off))
            out_shape.append(jax.ShapeDtypeStruct(buf.shape, buf.dtype))
        else:
            w, dt = ro
            out_specs.append(row_spec(w, 0))
            out_shape.append(jax.ShapeDtypeStruct((T, w), dt))
    out_specs += [full_spec(s) for s in acc_outs]
    out_shape += [jax.ShapeDtypeStruct(tuple(s), F32) for s in acc_outs]
    aliases = {n_mm + n_r + n_f + b: k for b, (k, _) in enumerate(into)}
    return pl.pallas_call(
        body, name=name, grid=(T // tm,) if mm is None else (T // tm, nk),
        out_shape=out_shape, in_specs=in_specs, out_specs=out_specs,
        scratch_shapes=[] if mm is None else [pltpu.VMEM((tm, N) if nk > 1 else (8, 128), F32)],
        input_output_aliases=aliases,
        compiler_params=_params(("arbitrary",) if mm is None else ("arbitrary", "arbitrary")),
    )(*([] if mm is None else [a, b]), *[r[0] for r in rows], *fulls, *[ro[0] for _, ro in into])


def fwd_body(fn):
    return lambda r, f: (fn(*r, *f), ())


def bwd_body(fn, n_rows):
    def body(r, f):
        ins, cots = r[:n_rows], r[n_rows:]
        _, vjp = jax.vjp(fn, *ins, *f)
        g = vjp(tuple(cots))
        return g[:n_rows], g[n_rows:]
    return body


def whole_call(fn, ins, outs, *, name):
    n_in = len(ins)

    def body(*refs):
        res = fn(*[r[...] for r in refs[:n_in]])
        for ref, v in zip(refs[n_in:], res):
            ref[...] = v.astype(ref.dtype)

    return pl.pallas_call(
        body, name=name,
        out_shape=[jax.ShapeDtypeStruct(tuple(s), dt) for (s, dt) in outs],
        compiler_params=_params(),
    )(*ins)


def fn_modulate(x, sc, sh):
    return (_rms(x) * (1.0 + sc) + sh,)


def fn_sgu(u, v, nw, ws, bs):
    ug = jax.nn.gelu(u)
    vn = _rms(jax.nn.gelu(v)) * nw
    ri = lax.broadcasted_iota(jnp.int32, (Q, Q), 0)
    ci = lax.broadcasted_iota(jnp.int32, (Q, Q), 1)
    causal = ri >= ci
    chunks = []
    for n in range(u.shape[0] // Q):
        vc = vn[n * Q:(n + 1) * Q]
        cols = [dot_nn(jnp.where(causal, ws[g], 0.0), vc[:, g * Q:(g + 1) * Q]) + bs[g]
                for g in range(GM_GROUPS)]
        chunks.append(jnp.concatenate(cols, axis=1))
    sv = chunks[0] if len(chunks) == 1 else jnp.concatenate(chunks, axis=0)
    return (ug * sv,)


def fn_mix(ga, gb, pa, pb):
    return (jax.nn.sigmoid(ga) * pa + jax.nn.sigmoid(gb) * pb,)


def fn_res_modulate(x, o, g1, sc2, sh2):
    x1 = x + g1 * o
    return x1, _rms(x1) * (1.0 + sc2) + sh2


def relu2_tile(f):
    return jnp.square(jnp.maximum(f.astype(F32), 0.0)).astype(BF16)


def relu2_grad_tile(dact, f):
    return dact * (2.0 * jnp.maximum(f.astype(F32), 0.0))


def final_body(r, f):
    x1, gf, tgt = r
    g2, fnw = f

    def loss_fn(x1, gf, g2, fnw):
        y = _rms(x1 + g2 * gf) * fnw
        row = 0.5 * jnp.mean(jnp.square(y - tgt), axis=-1, keepdims=True)
        return jnp.sum(row, axis=0, keepdims=True)

    l, vjp = jax.vjp(loss_fn, x1, gf, g2, fnw)
    dx1, dgf, dg2, dfnw = vjp(jnp.ones((1, 1), F32))
    return (dx1, dgf), (jnp.broadcast_to(l, (1, 128)), dg2, dfnw)


def grad_x_body(r, f):
    x, dh, dxa = r
    _, vjp = jax.vjp(fn_modulate, x, *f)
    dx, dsc, dsh = vjp((dh,))
    return (dx + dxa,), (dsc, dsh)


CONV_CW = 128
CONV_PAD = 8
CONV_ROWS = 128


def _conv_pre(xp, w_ref, b_ref, r0, R):
    acc = b_ref[...] + w_ref[0:1, :] * xp[r0 + CONV_PAD - 3:r0 + CONV_PAD - 3 + R, :]
    for k in range(1, CONV_K):
        s = r0 + CONV_PAD - 3 + k
        acc = acc + w_ref[k:k + 1, :] * xp[s:s + R, :]
    return acc


def conv_fwd(proj, conv_w, conv_b):
    T = proj.shape[0]
    R = min(CONV_ROWS, T)

    def body(x_ref, w_ref, b_ref, o_ref, xp):
        xp[0:CONV_PAD, :] = jnp.zeros((CONV_PAD, CONV_CW), F32)
        xp[CONV_PAD:CONV_PAD + T, :] = x_ref[...].astype(F32)
        for r0 in range(0, T, R):
            pre = _conv_pre(xp, w_ref, b_ref, r0, R)
            o_ref[r0:r0 + R, :] = (pre * jax.nn.sigmoid(pre)).astype(o_ref.dtype)

    return pl.pallas_call(
        body, name="conv_fwd", grid=(CONV_DIM // CONV_CW,),
        out_shape=jax.ShapeDtypeStruct((T, CONV_DIM), BF16),
        in_specs=[pl.BlockSpec((T, CONV_CW), lambda j: (0, P_XBC // CONV_CW + j)),
                  pl.BlockSpec((CONV_K, CONV_CW), lambda j: (0, j)),
                  pl.BlockSpec((1, CONV_CW), lambda j: (0, j))],
        out_specs=pl.BlockSpec((T, CONV_CW), lambda j: (0, j)),
        scratch_shapes=[pltpu.VMEM((T + CONV_PAD, CONV_CW), F32)],
        compiler_params=_params(("parallel",)),
    )(proj, conv_w, conv_b)


def conv_bwd(proj, dact, col0, conv_w, conv_b, dproj, *, name):
    T = proj.shape[0]
    R = min(CONV_ROWS, T)
    nb = dact.shape[1] // CONV_CW
    c0 = col0 // CONV_CW
    x0 = (P_XBC + col0) // CONV_CW

    def body(x_ref, d_ref, w_ref, b_ref, _, dx_ref, dw_ref, db_ref, xp, dp):
        xp[0:CONV_PAD, :] = jnp.zeros((CONV_PAD, CONV_CW), F32)
        xp[CONV_PAD:CONV_PAD + T, :] = x_ref[...].astype(F32)
        dp[T:T + CONV_PAD, :] = jnp.zeros((CONV_PAD, CONV_CW), F32)
        dws = [jnp.zeros((1, CONV_CW), F32) for _ in range(CONV_K)]
        db = jnp.zeros((1, CONV_CW), F32)
        for r0 in range(0, T, R):
            pre = _conv_pre(xp, w_ref, b_ref, r0, R)
            s = jax.nn.sigmoid(pre)
            dpre = d_ref[r0:r0 + R, :].astype(F32) * (s * (1.0 + pre * (1.0 - s)))
            dp[r0:r0 + R, :] = dpre
            db = db + jnp.sum(dpre, axis=0, keepdims=True)
            for k in range(CONV_K):
                st = r0 + CONV_PAD - 3 + k
                dws[k] = dws[k] + jnp.sum(dpre * xp[st:st + R, :], axis=0, keepdims=True)
        for r0 in range(0, T, R):
            acc = w_ref[0:1, :] * dp[r0 + 3:r0 + 3 + R, :]
            for k in range(1, CONV_K):
                acc = acc + w_ref[k:k + 1, :] * dp[r0 + 3 - k:r0 + 3 - k + R, :]
            dx_ref[r0:r0 + R, :] = acc.astype(dx_ref.dtype)
        for k in range(CONV_K):
            dw_ref[k:k + 1, :] = dws[k]
        db_ref[...] = db

    return pl.pallas_call(
        body, name=name, grid=(nb,),
        out_shape=[jax.ShapeDtypeStruct(dproj.shape, dproj.dtype),
                   jax.ShapeDtypeStruct((CONV_K, nb * CONV_CW), F32),
                   jax.ShapeDtypeStruct((1, nb * CONV_CW), F32)],
        in_specs=[pl.BlockSpec((T, CONV_CW), lambda j: (0, x0 + j)),
                  pl.BlockSpec((T, CONV_CW), lambda j: (0, j)),
                  pl.BlockSpec((CONV_K, CONV_CW), lambda j: (0, c0 + j)),
                  pl.BlockSpec((1, CONV_CW), lambda j: (0, c0 + j)),
                  pl.BlockSpec(memory_space=pl.ANY)],
        out_specs=[pl.BlockSpec((T, CONV_CW), lambda j: (0, x0 + j)),
                   pl.BlockSpec((CONV_K, CONV_CW), lambda j: (0, j)),
                   pl.BlockSpec((1, CONV_CW), lambda j: (0, j))],
        scratch_shapes=[pltpu.VMEM((T + CONV_PAD, CONV_CW), F32),
                        pltpu.VMEM((T + CONV_PAD, CONV_CW), F32)],
        input_output_aliases={4: 0},
        compiler_params=_params(("parallel",)),
    )(proj, dact, conv_w, conv_b, dproj)


def _split3(a):
    hi = a.astype(BF16)
    r = a - hi.astype(F32)
    mid = r.astype(BF16)
    return hi, mid, (r - mid.astype(F32)).astype(BF16)


def _dg3(a, m, ca, cm, a_first):
    dims = (((ca,), (cm,)), ((), ())) if a_first else (((cm,), (ca,)), ((), ()))
    out = None
    for p in _split3(a):
        t = lax.dot_general(p, m, dims, preferred_element_type=F32) if a_first else \
            lax.dot_general(m, p, dims, preferred_element_type=F32)
        out = t if out is None else out + t
    return out


@jax.custom_vjp
def exact_right(a, m):
    return _dg3(a, m, 1, 0, True)


@jax.custom_vjp
def exact_left(m, a):
    return _dg3(a, m, 0, 1, False)


def _expand_bwd(m, g):
    hi = g.astype(BF16)
    lo = (g - hi.astype(F32)).astype(BF16)
    out = lax.dot_general(jnp.concatenate([hi, lo], axis=1), jnp.concatenate([m, m], axis=1),
                          (((1,), (1,)), ((), ())), preferred_element_type=F32)
    return out, jnp.zeros_like(m)


exact_right.defvjp(lambda a, m: (exact_right(a, m), m), _expand_bwd)
exact_left.defvjp(lambda m, a: (exact_left(m, a), m),
                  lambda m, g: (jnp.zeros_like(m), _dg3(g, m, 0, 0, False)))


def ssd_step(lane0, state, x, z, dtr, Bm, Cm, dtb, alog, dsk, nw):
    def iota(shape, dim):
        return lax.broadcasted_iota(jnp.int32, shape, dim)

    def one_hot(mask):
        return mask.astype(F32).astype(BF16)

    causal = iota((Q, Q), 0) >= iota((Q, Q), 1)
    eye = iota((Q, Q), 0) == iota((Q, Q), 1)
    lane = iota((1, 128), 1)
    colh = lax.shift_right_logical(iota((1, SSM_GW), 1), 6)
    to_cols = one_hot(iota((128, SSM_GW), 0) == lane0 + colh)

    dt_all = _softplus(dtr + dtb)
    a_all = dt_all * (-jnp.exp(alog))
    cum_all = exact_left(one_hot(causal), a_all)
    both = exact_right(jnp.concatenate([dt_all, cum_all], axis=0), to_cols)
    dt_f, cum_f = both[:Q], both[Q:]
    last_f = jnp.sum(jnp.where(iota((Q, 1), 0) == Q - 1, cum_f, 0.0), axis=0, keepdims=True)
    dsk_f = jnp.zeros((1, SSM_GW), F32)
    for h in range(SSM_HPG):
        dsk_f = jnp.where(colh == h, dsk[h], dsk_f)

    xdt = x * dt_f
    cb = dot_nt(Cm, Bm)
    ms = []
    for h in range(SSM_HPG):
        ch = jnp.sum(jnp.where(lane == lane0 + h, cum_all, 0.0), axis=1, keepdims=True)
        ch_t = jnp.sum(jnp.where(eye, ch, 0.0), axis=0, keepdims=True)
        ms.append(cb * jnp.exp(jnp.where(causal, ch - ch_t, -1e30)))
    first_half = lane < SSM_P
    blocks = []
    for b in range(SSM_HPG // 2):
        xb = xdt[:, b * 128:(b + 1) * 128]
        rhs = jnp.concatenate([jnp.where(first_half, xb, 0.0), jnp.where(first_half, 0.0, xb)], axis=0)
        blocks.append(dot_nn(jnp.concatenate(ms[2 * b:2 * b + 2], axis=1), rhs))
    y = jnp.concatenate(blocks, axis=1)
    y = y + dot_nn(Cm, state) * jnp.exp(cum_f) + x * dsk_f
    new_state = state * jnp.exp(last_f) + dot_tn(Bm, xdt * jnp.exp(last_f - cum_f))
    gated = y * (z * jax.nn.sigmoid(z))
    return new_state, _rms(gated) * nw


SSD_GPS = 4
_XW = SSD_GPS * SSM_GW
_BW = SSD_GPS * 128


def _ssd_in_specs(rev, nc):
    def n_of(n):
        return nc - 1 - n if rev else n
    return [
        pl.BlockSpec((Q, _XW), lambda g, n: (n_of(n), g)),
        pl.BlockSpec((Q, _BW), lambda g, n: (n_of(n), SSM_INNER // _BW + g)),
        pl.BlockSpec((Q, _BW), lambda g, n: (n_of(n), (SSM_INNER + SSM_GROUPS * 128) // _BW + g)),
        pl.BlockSpec((Q, _XW), lambda g, n: (n_of(n), P_Z // _XW + g)),
        pl.BlockSpec((Q, 128), lambda g, n: (n_of(n), 0)),
        pl.BlockSpec((1, 128), lambda g, n: (0, 0)),
        pl.BlockSpec((1, 128), lambda g, n: (0, 0)),
        pl.BlockSpec((SSD_GPS, SSM_HPG, 1, 1), lambda g, n: (g, 0, 0, 0)),
        pl.BlockSpec((1, _XW), lambda g, n: (0, g)),
    ]


def _ssd_group_inputs(gi, x_ref, b_ref, c_ref, z_ref, dt_ref, dtb_ref, al_ref, dk_ref, nw_ref):
    xs = slice(gi * SSM_GW, (gi + 1) * SSM_GW)
    bs = slice(gi * 128, (gi + 1) * 128)
    return (x_ref[:, xs].astype(F32), z_ref[:, xs].astype(F32), dt_ref[...],
            b_ref[:, bs], c_ref[:, bs],
            dtb_ref[...], al_ref[...], dk_ref[gi], nw_ref[:, xs])


def ssd_fwd(xact, proj, dtg, dtb, alog, dsk, nw, comm):
    T = xact.shape[0]
    nc = T // Q
    nx = len(comm)
    ng = SSM_GROUPS // SSD_GPS
    any_spec = pl.BlockSpec(memory_space=pl.ANY)

    def body(*refs):
        in_refs, src_refs = refs[:9], refs[9:9 + nx]
        yb_ref, st_ref = refs[9 + nx:11 + nx]
        out_refs, state, sems = refs[11 + nx:11 + 2 * nx], refs[11 + 2 * nx], refs[12 + 2 * nx:]
        g, n = pl.program_id(0), pl.program_id(1)
        exchange_start(comm, src_refs, out_refs, sems, (g == 0) & (n == 0))

        @pl.when(n == 0)
        def _():
            state[...] = jnp.zeros(state.shape, F32)

        for gi in range(SSD_GPS):
            lane0 = SSM_HPG * (SSD_GPS * pl.program_id(0) + gi)
            s = state[gi]
            st_ref[gi, 0] = s
            new_s, yb = ssd_step(lane0, s, *_ssd_group_inputs(gi, *in_refs))
            state[gi] = new_s
            yb_ref[:, gi * SSM_GW:(gi + 1) * SSM_GW] = yb.astype(yb_ref.dtype)

        exchange_finish(comm, src_refs, out_refs, sems, (g == ng - 1) & (n == nc - 1))

    return pl.pallas_call(
        body, name="ssd_fwd", grid=(ng, nc),
        out_shape=[jax.ShapeDtypeStruct((T, SSM_INNER), BF16),
                   jax.ShapeDtypeStruct((SSM_GROUPS, nc, 128, SSM_GW), F32)] + exchange_out_shapes(comm),
        in_specs=_ssd_in_specs(False, nc) + [any_spec] * nx,
        out_specs=[pl.BlockSpec((Q, _XW), lambda g, n: (n, g)),
                   pl.BlockSpec((SSD_GPS, 1, 128, SSM_GW), lambda g, n: (g, n, 0, 0))] + [any_spec] * nx,
        scratch_shapes=[pltpu.VMEM((SSD_GPS, 128, SSM_GW), F32)] + exchange_semaphores(comm),
        compiler_params=_params(("arbitrary", "arbitrary")),
    )(xact, xact, xact, proj, dtg, dtb, alog, dsk, nw, *[it[0] for it in comm])


def ssd_bwd(xact, proj, dtg, dtb, alog, dsk, nw, states, dyb, dproj, comm):
    T = xact.shape[0]
    nc = T // Q

    nx = len(comm)
    ng = SSM_GROUPS // SSD_GPS
    any_spec = pl.BlockSpec(memory_space=pl.ANY)

    def body(*refs):
        in_refs, (st_ref, dy_ref, _) = refs[:9], refs[9:12]
        src_refs, refs = refs[12:12 + nx], refs[12 + nx:]
        dx_ref, db_ref, dc_ref, dz_ref, ddt_ref, ddtb_ref, dal_ref, ddk_ref, dnw_ref = refs[:9]
        out_refs, dstate, sems = refs[9:9 + nx], refs[9 + nx], refs[10 + nx:]
        exchange_start(comm, src_refs, out_refs, sems, (pl.program_id(0) == 0) & (pl.program_id(1) == 0))

        @pl.when(pl.program_id(1) == 0)
        def _():
            dstate[...] = jnp.zeros(dstate.shape, F32)
            ddtb_ref[...] = jnp.zeros(ddtb_ref.shape, F32)
            dal_ref[...] = jnp.zeros(dal_ref.shape, F32)
            ddk_ref[...] = jnp.zeros(ddk_ref.shape, F32)
            dnw_ref[...] = jnp.zeros(dnw_ref.shape, F32)

        for gi in range(SSD_GPS):
            xs = slice(gi * SSM_GW, (gi + 1) * SSM_GW)
            bs = slice(gi * 128, (gi + 1) * 128)
            lane0 = SSM_HPG * (SSD_GPS * pl.program_id(0) + gi)
            ins = (st_ref[gi, 0],) + _ssd_group_inputs(gi, *in_refs)
            _, vjp = jax.vjp(functools.partial(ssd_step, lane0), *ins)
            ds, dx, dz, ddt, dbm, dcm, ddtb, dal, ddk, dnw = vjp((dstate[gi], dy_ref[:, xs].astype(F32)))
            dstate[gi] = ds
            dx_ref[:, xs] = dx.astype(dx_ref.dtype)
            db_ref[:, bs] = dbm.astype(db_ref.dtype)
            dc_ref[:, bs] = dcm.astype(dc_ref.dtype)
            dz_ref[:, xs] = dz.astype(dz_ref.dtype)
            ddt_ref[:, bs] = ddt.astype(ddt_ref.dtype)
            ddtb_ref[gi] += ddtb
            dal_ref[gi] += dal
            ddk_ref[gi] += ddk
            dnw_ref[:, xs] += dnw

        exchange_finish(comm, src_refs, out_refs, sems,
                        (pl.program_id(0) == ng - 1) & (pl.program_id(1) == nc - 1))

    rev = lambda n: nc - 1 - n
    row_shape = jax.ShapeDtypeStruct((SSM_GROUPS, 1, 128), F32)
    row_spec = pl.BlockSpec((SSD_GPS, 1, 128), lambda g, n: (g, 0, 0))
    return pl.pallas_call(
        body, name="ssd_bwd", grid=(ng, nc),
        out_shape=[jax.ShapeDtypeStruct((T, SSM_INNER), BF16),
                   jax.ShapeDtypeStruct((T, SSM_GROUPS * 128), BF16),
                   jax.ShapeDtypeStruct((T, SSM_GROUPS * 128), BF16),
                   jax.ShapeDtypeStruct(dproj.shape, dproj.dtype),
                   jax.ShapeDtypeStruct((T, SSM_GROUPS * 128), BF16),
                   row_shape, row_shape,
                   jax.ShapeDtypeStruct((SSM_GROUPS, SSM_HPG, 1, 1), F32),
                   jax.ShapeDtypeStruct((1, SSM_INNER), F32)] + exchange_out_shapes(comm),
        in_specs=_ssd_in_specs(True, nc) + [
            pl.BlockSpec((SSD_GPS, 1, 128, SSM_GW), lambda g, n: (g, rev(n), 0, 0)),
            pl.BlockSpec((Q, _XW), lambda g, n: (rev(n), g)),
            any_spec] + [any_spec] * nx,
        out_specs=[pl.BlockSpec((Q, _XW), lambda g, n: (rev(n), g)),
                   pl.BlockSpec((Q, _BW), lambda g, n: (rev(n), g)),
                   pl.BlockSpec((Q, _BW), lambda g, n: (rev(n), g)),
                   pl.BlockSpec((Q, _XW), lambda g, n: (rev(n), P_Z // _XW + g)),
                   pl.BlockSpec((Q, _BW), lambda g, n: (rev(n), g)),
                   row_spec, row_spec,
                   pl.BlockSpec((SSD_GPS, SSM_HPG, 1, 1), lambda g, n: (g, 0, 0, 0)),
                   pl.BlockSpec((1, _XW), lambda g, n: (0, g))] + [any_spec] * nx,
        scratch_shapes=[pltpu.VMEM((SSD_GPS, 128, SSM_GW), F32)] + exchange_semaphores(comm),
        input_output_aliases={11: 3},
        compiler_params=_params(("arbitrary", "arbitrary")),
    )(xact, xact, xact, proj, dtg, dtb, alog, dsk, nw, states, dyb, dproj, *[it[0] for it in comm])


ADAMW_WHOLE_ELEMS = 256 * 1024


def adamw(w, g, m, v, *, name):
    shape = w.shape
    parts = g.shape != shape
    nd = len(shape)
    if nd == 3 and shape[1] == 1 and w.size > ADAMW_WHOLE_ELEMS:
        assert not parts and shape[0] % 4 == 0
        grid = (4,)
        spec = g_spec = pl.BlockSpec((shape[0] // 4, 1, shape[2]), lambda i: (i, 0, 0))
    else:
        if w.size <= ADAMW_WHOLE_ELEMS:
            grid, tr = (1,), shape[-2]
        else:
            assert all(s == 1 for s in shape[:-2]) and shape[-2] % 256 == 0
            grid, tr = (shape[-2] // 256,), 256
        blk = tuple(shape[:-2]) + (tr, shape[-1])
        spec = pl.BlockSpec(blk, lambda i: (0,) * (nd - 2) + (i, 0))
        g_spec = pl.BlockSpec((N_DEV,) + blk[1:], lambda i: (0,) * (nd - 2) + (i, 0)) if parts else spec

    def body(w_ref, g_ref, m_ref, v_ref, go_ref, d_ref, nm_ref, nv_ref):
        if parts:
            g = g_ref[0:1].astype(F32)
            for j in range(1, N_DEV):
                g = g + g_ref[j:j + 1].astype(F32)
        else:
            g = g_ref[...]
        nm = ADAM_B1 * m_ref[...] + (1.0 - ADAM_B1) * g
        nv = ADAM_B2 * v_ref[...] + (1.0 - ADAM_B2) * jnp.square(g)
        m_hat = nm / (1.0 - ADAM_B1 ** ADAM_STEP)
        v_hat = nv / (1.0 - ADAM_B2 ** ADAM_STEP)
        go_ref[...] = g
        d_ref[...] = -ADAM_LR * (m_hat / (jnp.sqrt(v_hat) + ADAM_EPS) + ADAM_WD * w_ref[...])
        nm_ref[...] = nm
        nv_ref[...] = nv

    shp = jax.ShapeDtypeStruct(shape, F32)
    return pl.pallas_call(
        body, name=name, grid=grid,
        out_shape=[shp] * 4, in_specs=[spec, g_spec, spec, spec], out_specs=[spec] * 4,
        compiler_params=_params(("parallel",)),
    )(w, g, m, v)


def _pad_rows(a, rows):
    return jnp.pad(a, ((0, rows - a.shape[0]), (0, 0)))


WIN_W = 1408
N_IN = IN_WIDTH // N_DEV
_A6 = OFF_DT - 6 * N_IN
_C6 = 7 * N_IN - OFF_GA


_WIN_OFFSETS = (0, 4, 8, 12, 16, 20, None, 124)


def _w_in_window(shard, me):
    rows = shard.shape[0]
    z = lambda n: jnp.zeros((rows, n), shard.dtype)

    def plain(off):
        return lambda s: jnp.pad(s, ((0, 0), (off, WIN_W - N_IN - off)))

    def split(s):
        return jnp.concatenate([z(24), s[:, :_A6], s[:, _A6 + 32:], z(4), s[:, _A6:_A6 + 32], z(96)], axis=1)

    return lax.switch(me, [split if off is None else plain(off) for off in _WIN_OFFSETS], shard)


def _w_in_from_window(window, me):
    def plain(off):
        return lambda w: w[:, off:off + N_IN]

    def split(w):
        return jnp.concatenate([w[:, 24:24 + _A6], w[:, 1280:1312], w[:, 24 + _A6:24 + _A6 + _C6]], axis=1)

    return lax.switch(me, [split if off is None else plain(off) for off in _WIN_OFFSETS], window)


def _w_all_from_windows(g):
    def merge_first(p, t):
        return jnp.concatenate([p[:, :128] + t, p[:, 128:]], axis=1)

    parts = [g[0][:, :1280]]
    for j in range(1, 6):
        parts.append(merge_first(g[j][:, :1280], g[j - 1][:, 1280:]))
    p6 = merge_first(g[6][:, :1280], g[5][:, 1280:])
    parts.append(jnp.concatenate([p6[:, :1152], p6[:, 1152:] + g[7][:, :128]], axis=1))
    parts.append(g[7][:, 128:])
    parts.append(g[6][:, 1280:])
    return jnp.concatenate(parts, axis=1)


def _windows_of_w_all(gw):
    wins = [gw[:, 1280 * j:1280 * j + WIN_W] for j in range(6)]
    wins.append(jnp.concatenate([gw[:, 7680:8960], gw[:, PROJ_W:]], axis=1))
    wins.append(gw[:, 8832:PROJ_W])
    return jnp.stack(wins)


def kernel(x, c, w_mod, b_mod, w_in, gm_norm_w, gm_ws, gm_bs, conv_w, conv_b, dt_bias, a_log, d_skip, ssm_norm_w, w_branch_gm, w_branch_ssm, w_out, w_ff1, w_ff2, final_norm_w, loss_target, m_w_mod, m_b_mod, m_w_in, m_gm_norm_w, m_gm_ws, m_gm_bs, m_conv_w, m_conv_b, m_dt_bias, m_a_log, m_d_skip, m_ssm_norm_w, m_w_branch_gm, m_w_branch_ssm, m_w_out, m_w_ff1, m_w_ff2, m_final_norm_w, v_w_mod, v_b_mod, v_w_in, v_gm_norm_w, v_gm_ws, v_gm_bs, v_conv_w, v_conv_b, v_dt_bias, v_a_log, v_d_skip, v_ssm_norm_w, v_w_branch_gm, v_w_branch_ssm, v_w_out, v_w_ff1, v_w_ff2, v_final_norm_w):
    T = x.shape[1]
    me = 4 * lax.axis_index("x") + 2 * lax.axis_index("y") + lax.axis_index("c")
    x2 = x[0]
    tgt = loss_target[0]
    n_in = IN_WIDTH // N_DEV
    n_mod = N_MOD * D // N_DEV
    n_cv = CONV_DIM // N_DEV

    c_all, conv_w_full = exchange(
        [(c.reshape(8, 128), _whole, (N_DEV, 8, 128), _slot),
         (conv_w[0], _whole, (N_DEV, CONV_K, n_cv), _slot)], name="gather_c_convw")
    c_all = c_all.reshape(N_DEV, D)
    conv_w_full = conv_w_full.transpose(1, 0, 2).reshape(CONV_K, CONV_DIM)

    win = _w_in_window(w_in[0].astype(BF16), me)
    late_weights = [
        (w_branch_gm[0].astype(BF16), _whole, (D, D), _rows(D // N_DEV)),
        (w_branch_ssm[0].astype(BF16), _whole, (SSM_INNER, D), _rows(SSM_INNER // N_DEV)),
        (w_out[0].astype(BF16), _whole, (D, D), _rows(D // N_DEV)),
        (w_ff1[0].astype(BF16), _whole, (D, D_FF), _cols(D_FF // N_DEV)),
        (w_ff2[0].astype(BF16), _whole, (D_FF, D), _rows(D_FF // N_DEV))]

    c_pad = _pad_rows(c_all, 128)
    b_mine = lax.dynamic_slice(b_mod, (0, me * n_mod), (1, n_mod))

    def mod_fn(cp, w, b):
        ca = cp * jax.nn.sigmoid(cp)
        return (jnp.dot(ca, w, precision=HIGHEST, preferred_element_type=F32) + b,)

    (mod_part,) = whole_call(mod_fn, [c_pad, w_mod[0], b_mine], [((128, n_mod), F32)], name="mod_fwd")
    gmod = gather_blocks(mod_part[:N_DEV], name="gather_mod")
    mod = lax.dynamic_index_in_dim(gmod, me, axis=1, keepdims=False).reshape(N_MOD, D)
    sh1, sc1, gt1, sh2, sc2, gt2 = [mod[i:i + 1] for i in range(N_MOD)]

    h, gwin = modulate_with_gather(x2, sc1, sh1, win, tm=256, name="modulate1_gather_w_in")
    w_all = _w_all_from_windows(gwin)
    w_dt = w_all[:, PROJ_W:]
    proj = matmul(h, w_all, "nn", BF16, name="mm_proj", n=PROJ_W, tm=K1_TM)
    dtg = matmul(h, w_dt, "nn", F32, name="mm_dt")
    ws = gm_ws[0]
    bs3 = gm_bs[0].reshape(GM_GROUPS, Q, 1)
    sgu_rows = [(proj, D, P_U), (proj, D, P_V)]
    (ya,) = rowwise_call(fwd_body(fn_sgu), sgu_rows, [gm_norm_w, ws, bs3], [(D, BF16)], [],
                         tm=256, name="sgu_fwd")
    xact = conv_fwd(proj, conv_w_full, conv_b)
    dtb4 = jnp.pad(dt_bias, ((0, 0), (0, 96)))
    alog4 = jnp.pad(a_log, ((0, 0), (0, 96)))
    dsk4 = d_skip.reshape(SSM_GROUPS, SSM_HPG, 1, 1)
    yb, states, w_gm_f, w_ssm_f, w_out_f, w_ff1_f, w_ff2_f = ssd_fwd(
        xact, proj, dtg, dtb4, alog4, dsk4, ssm_norm_w, late_weights)
    pa = matmul(ya, w_gm_f, "nn", BF16, name="mm_branch_gm", tm=K1_TM)
    gate_rows = [(proj, D, P_GA), (proj, D, P_GB)]
    mixed, pb = rowwise_call(
        lambda r, fl: (fn_mix(*r) + (r[3],), ()), gate_rows + [pa], [], [(D, BF16), (D, BF16)], [],
        tm=FUSED_TM, name="branch_ssm_mix", mm=(yb, w_ssm_f, "nn", 3, None), tk=SSM_INNER)
    x1, h2, o = rowwise_call(
        lambda r, fl: (fn_res_modulate(*r, *fl) + (r[1],), ()), [x2], [gt1, sc2, sh2],
        [(D, F32), (D, BF16), (D, F32)], [], tm=FUSED_TM, name="out_res_modulate2",
        mm=(mixed, w_out_f, "nn", 1, None))
    f = matmul(h2, w_ff1_f, "nn", BF16, name="mm_ff1", tm=K1_TM)

    dx1, dgf, loss_v, dgt2, dfnw = rowwise_call(
        final_body, [x1, tgt], [gt2, final_norm_w.reshape(1, D)], [(D, F32), (D, BF16)],
        [(1, 128), (1, D), (1, D)], tm=FUSED_TM, name="ff2_loss_bwd", mm=(f, w_ff2_f, "nn", 1, relu2_tile),
        tk=D_FF)
    df = matmul(dgf, w_ff2_f, "nt", BF16, name="mm_ff2_dgrad", epi=relu2_grad_tile, epi_ins=(f,), tm=K1_TM)
    gw_ff2 = matmul(f, dgf, "tn", BF16, name="mm_ff2_wgrad", tk=WGRAD_TK, a_pro=relu2_tile)
    gw_ff1 = matmul(h2, df, "tn", BF16, name="mm_ff1_wgrad", tk=WGRAD_TK)

    def res_mod_bwd(r, fl):
        xv, ov, dx1v, dh2v = r
        _, vjp = jax.vjp(fn_res_modulate, xv, ov, *fl)
        dxv, dov, dg1, dsc, dsh = vjp((dx1v, dh2v))
        return (dxv, dov), (dg1, dsc, dsh)

    dxa, do, dgt1, dsc2, dsh2 = rowwise_call(
        res_mod_bwd, [x2, o, dx1], [gt1, sc2, sh2], [(D, F32), (D, BF16)],
        [(1, D), (1, D), (1, D)], tm=FUSED_TM, name="ff1_dgrad_res_modulate2_bwd",
        mm=(df, w_ff1_f, "nt", 3, None), tk=D_FF)
    gw_out = matmul(mixed, do, "tn", BF16, name="mm_out_wgrad", tk=WGRAD_TK)
    dproj = lax.empty((T, ALL_W), BF16)

    def mix_bwd(r, fl):
        dga, dgb, dpa, dpb = bwd_body(fn_mix, 4)(r, fl)[0]
        return (jnp.concatenate([dga, dgb], axis=1), dpa, dpb), ()

    dproj, dpa, dpb = rowwise_call(
        mix_bwd, gate_rows + [pa, pb], [], [(dproj, 2 * D, P_GA), (D, BF16), (D, BF16)], [],
        tm=FUSED_TM, name="out_dgrad_mix_bwd", mm=(do, w_out_f, "nt", 4, None))
    gw_gm = matmul(ya, dpa, "tn", BF16, name="mm_branch_gm_wgrad", tk=WGRAD_TK)
    dyb = matmul(dpb, w_ssm_f, "nt", BF16, name="mm_branch_ssm_dgrad", tm=K1_TM)
    gw_ssm = matmul(yb, dpb, "tn", BF16, name="mm_branch_ssm_wgrad", tk=WGRAD_TK)

    def sgu_bwd(r, fl):
        (du, dv), acc = bwd_body(fn_sgu, 2)(r, fl)
        return (jnp.concatenate([du, dv], axis=1),), acc

    dproj, dgnw, dws, dbs = rowwise_call(
        sgu_bwd, sgu_rows, [gm_norm_w, ws, bs3], [(dproj, 2 * D, P_U)],
        [(1, D), (GM_GROUPS, Q, Q), (GM_GROUPS, Q, 1)], tm=FUSED_TM, name="branch_gm_dgrad_sgu_bwd",
        mm=(dpa, w_gm_f, "nt", 2, None))
    early_grads = [
        (gw_gm, _rows(D // N_DEV), (N_DEV, D // N_DEV, D), _slot),
        (gw_ssm, _rows(SSM_INNER // N_DEV), (N_DEV, SSM_INNER // N_DEV, D), _slot),
        (gw_out, _rows(D // N_DEV), (N_DEV, D // N_DEV, D), _slot),
        (gw_ff1, _cols(D_FF // N_DEV), (N_DEV, D, D_FF // N_DEV), _slot),
        (gw_ff2, _rows(D_FF // N_DEV), (N_DEV, D_FF // N_DEV, D), _slot),
        (_pack_rows([dgnw, dws, dbs], EARLY_ROWS), _whole, (N_DEV, sum(EARLY_ROWS), 128), _slot)]
    (dxs, dbm, dcm, dproj, ddt8, ddtb, dalog, ddsk, dsnw,
     r_gm, r_ssm, r_out, r_ff1, r_ff2, early_all) = ssd_bwd(
        xact, proj, dtg, dtb4, alog4, dsk4, ssm_norm_w, states, dyb, dproj, early_grads)
    dconv_w, dconv_b = [], []
    for nm, dact_part, col0 in (("xs", dxs, 0), ("b", dbm, SSM_INNER), ("c", dcm, SSM_INNER + SSM_GROUPS * 128)):
        dproj, dcw, dcb = conv_bwd(proj, dact_part, col0, conv_w_full, conv_b, dproj, name="conv_bwd_" + nm)
        dconv_w.append(dcw)
        dconv_b.append(dcb)
    dconv_w = jnp.concatenate(dconv_w, axis=1)
    dconv_b = jnp.concatenate(dconv_b, axis=1)
    (dproj,) = rowwise_call(
        lambda r, fl: ((functools.reduce(jnp.add, r),), ()),
        [(ddt8, 128, 128 * g) for g in range(SSM_GROUPS)], [], [(dproj, 128, PROJ_W)], [],
        tm=1024, name="ddt_into_dproj")
    gw_all = matmul(h, dproj, "tn", BF16, name="mm_in_wgrad", tn=1152, tk=WGRAD_TK)
    mid_pack = _pack_rows([dconv_w, dconv_b, jnp.sum(ddtb, axis=0), jnp.sum(dalog, axis=0), ddsk, dsnw, dfnw,
                           jnp.concatenate([dgt1, dsh2, dsc2, dgt2], axis=0)], MID_ROWS)
    dh, r_in, mid_all = matmul(
        dproj, w_all, "nt", BF16, name="mm_in_dgrad", tm=2048, tk=1152,
        comm=[(_windows_of_w_all(gw_all), _slot, (N_DEV, D, WIN_W), _slot),
              (mid_pack, _whole, (N_DEV, sum(MID_ROWS), 128), _slot)])
    grad_x, dsc1, dsh1 = rowwise_call(grad_x_body, [x2, dh, dxa], [sc1, sh1], [(D, F32)],
                                      [(1, D), (1, D)], tm=256, name="modulate1_bwd")

    g_w_in = _w_in_from_window(sum_devices(r_in, tr=256, name="sum_w_in_grads"), me).reshape(1, D, n_in)

    late_all = gather_blocks(_pack_rows([dsh1, dsc1, loss_v], LATE_ROWS), name="gather_dmod1_loss")
    s_early = _unpack_rows(sum_devices(early_all, tr=early_all.shape[1], name="sum_small_early"), EARLY_ROWS)
    s_mid = _unpack_rows(sum_devices(mid_all, tr=mid_all.shape[1], name="sum_small_mid"), MID_ROWS)
    s_late = _unpack_rows(sum_devices(late_all, tr=late_all.shape[1], name="sum_small_late"), LATE_ROWS)
    g_gm_norm_w = s_early[0][:D].reshape(1, D)
    g_gm_ws = s_early[1].reshape(GM_GROUPS * Q, Q)
    g_gm_bs = s_early[2][:GM_GROUPS * Q].reshape(GM_GROUPS, Q)
    g_conv_w_full = s_mid[0].reshape(CONV_K, CONV_DIM)
    g_conv_w = lax.dynamic_slice(g_conv_w_full, (0, me * n_cv), (CONV_K, n_cv))
    g_conv_b = s_mid[1].reshape(1, CONV_DIM)
    g_dt_bias = s_mid[2][:32].reshape(1, 32)
    g_a_log = s_mid[3][:32].reshape(1, 32)
    g_d_skip = s_mid[4][:32].reshape(1, 32)
    g_ssm_norm_w = s_mid[5].reshape(1, SSM_INNER)
    g_final_norm_w = s_mid[6][:D].reshape(1, D)
    g_b_mod = jnp.concatenate([s_late[0][:D], s_late[1][:D], s_mid[7]]).reshape(1, N_MOD * D)

    dmod_all = jnp.concatenate(
        [late_all.reshape(N_DEV, -1)[:, :2 * D],
         mid_all[:, sum(MID_ROWS[:7]):].reshape(N_DEV, 4 * D)], axis=1)
    dmod_mine = _pad_rows(lax.dynamic_slice(dmod_all, (0, me * n_mod), (N_DEV, n_mod)), 128)

    def wmod_grad_fn(cp, dm):
        ca = cp * jax.nn.sigmoid(cp)
        return (lax.dot_general(ca, dm, (((0,), (0,)), ((), ())), precision=HIGHEST,
                                preferred_element_type=F32),)

    (g_w_mod,) = whole_call(wmod_grad_fn, [c_pad, dmod_mine], [((D, n_mod), F32)], name="w_mod_grad")

    upd = {}

    def step(name, w, g, m, v, parts=False):
        upd[name] = adamw(w, g if parts else g.reshape(w.shape), m, v, name="adamw_" + name)

    step("w_mod", w_mod, g_w_mod, m_w_mod, v_w_mod)
    step("b_mod", b_mod, g_b_mod, m_b_mod, v_b_mod)
    col_major = lambda a: jnp.transpose(a, (2, 0, 1))
    upd["w_in"] = tuple(jnp.transpose(o, (1, 2, 0)) for o in adamw(
        col_major(w_in), col_major(g_w_in), col_major(m_w_in), col_major(v_w_in), name="adamw_w_in"))
    step("gm_norm_w", gm_norm_w, g_gm_norm_w, m_gm_norm_w, v_gm_norm_w)
    step("gm_ws", gm_ws, g_gm_ws, m_gm_ws, v_gm_ws)
    step("gm_bs", gm_bs, g_gm_bs, m_gm_bs, v_gm_bs)
    step("conv_w", conv_w, g_conv_w, m_conv_w, v_conv_w)
    step("conv_b", conv_b, g_conv_b, m_conv_b, v_conv_b)
    step("dt_bias", dt_bias, g_dt_bias, m_dt_bias, v_dt_bias)
    step("a_log", a_log, g_a_log, m_a_log, v_a_log)
    step("d_skip", d_skip, g_d_skip, m_d_skip, v_d_skip)
    step("ssm_norm_w", ssm_norm_w, g_ssm_norm_w, m_ssm_norm_w, v_ssm_norm_w)
    step("w_branch_gm", w_branch_gm, r_gm, m_w_branch_gm, v_w_branch_gm, parts=True)
    step("w_branch_ssm", w_branch_ssm, r_ssm, m_w_branch_ssm, v_w_branch_ssm, parts=True)
    step("w_out", w_out, r_out, m_w_out, v_w_out, parts=True)
    step("w_ff1", w_ff1, r_ff1, m_w_ff1, v_w_ff1, parts=True)
    step("w_ff2", w_ff2, r_ff2, m_w_ff2, v_w_ff2, parts=True)
    step("final_norm_w", final_norm_w.reshape(1, D), g_final_norm_w, m_final_norm_w.reshape(1, D),
         v_final_norm_w.reshape(1, D))
    upd["final_norm_w"] = tuple(a.reshape(D) for a in upd["final_norm_w"])

    loss = s_late[2][0]
    order = ["w_mod", "b_mod", "w_in", "gm_norm_w", "gm_ws", "gm_bs", "conv_w", "conv_b", "dt_bias", "a_log",
             "d_skip", "ssm_norm_w", "w_branch_gm", "w_branch_ssm", "w_out", "w_ff1", "w_ff2", "final_norm_w"]
    return (loss, grad_x.reshape(1, T, D),
            *[upd[n][0] for n in order], *[upd[n][1] for n in order],
            *[upd[n][2] for n in order], *[upd[n][3] for n in order])
```

```python
import functools

import jax
import jax.numpy as jnp
from jax import lax
from jax.experimental import pallas as pl
from jax.experimental.pallas import tpu as pltpu

F32 = jnp.float32
BF16 = jnp.bfloat16
MESH = pl.DeviceIdType.MESH
HIGHEST = lax.Precision.HIGHEST

N_DEV = 8
D = 1024
Q = 128
GM_GROUPS = 8
SSM_INNER = 2048
SSM_GROUPS = 8
SSM_HPG = 4
SSM_P = 64
SSM_GW = SSM_HPG * SSM_P
CONV_DIM = 4096
CONV_K = 4
D_FF = 4096
N_MOD = 6
EPS = 1e-6
IN_WIDTH = 10272
OFF_DT = 8192
OFF_GA = 8224
PROJ_W = 10240
ALL_W = 10368
P_U, P_V, P_Z, P_XBC, P_GA, P_GB = 0, 1024, 2048, 4096, 8192, 9216

ADAM_LR = 0.001
ADAM_B1 = 0.9
ADAM_B2 = 0.999
ADAM_EPS = 1e-08
ADAM_WD = 0.01
ADAM_STEP = 10

VMEM_LIMIT_BYTES = 48 * 1024 * 1024
K1_TM = 2048
FUSED_TM = 512
WGRAD_TK = 2048
EARLY_ROWS = (8, 1024, 8)
MID_ROWS = (128, 32, 8, 8, 8, 16, 8, 32)
LATE_ROWS = (8, 8, 8)


def _pack_rows(arrs, rows):
    def rows128(a, r):
        a = a.reshape(-1)
        return jnp.pad(a, (0, r * 128 - a.shape[0])).reshape(r, 128)
    return jnp.concatenate([rows128(a, r) for a, r in zip(arrs, rows)], axis=0)


def _unpack_rows(s, rows):
    out, o = [], 0
    for r in rows:
        out.append(s[o:o + r].reshape(-1))
        o += r
    return out


def _params(sem=None):
    return pltpu.CompilerParams(dimension_semantics=sem, vmem_limit_bytes=VMEM_LIMIT_BYTES)


def _dg(a, b, ca, cb):
    return lax.dot_general(a.astype(BF16), b.astype(BF16), (((ca,), (cb,)), ((), ())),
                           preferred_element_type=F32)


@jax.custom_vjp
def dot_nn(a, b):
    return _dg(a, b, 1, 0)


@jax.custom_vjp
def dot_nt(a, b):
    return _dg(a, b, 1, 1)


@jax.custom_vjp
def dot_tn(a, b):
    return _dg(a, b, 0, 0)


def _like(ct, primal):
    return ct.astype(primal.dtype)


dot_nn.defvjp(lambda a, b: (dot_nn(a, b), (a, b)),
              lambda r, g: (_like(dot_nt(g, r[1]), r[0]), _like(dot_tn(r[0], g), r[1])))
dot_nt.defvjp(lambda a, b: (dot_nt(a, b), (a, b)),
              lambda r, g: (_like(dot_nn(g, r[1]), r[0]), _like(dot_tn(g, r[0]), r[1])))
dot_tn.defvjp(lambda a, b: (dot_tn(a, b), (a, b)),
              lambda r, g: (_like(dot_nt(r[1], g), r[0]), _like(dot_nn(r[0], g), r[1])))


def _rms(x):
    return x * lax.rsqrt(jnp.mean(x * x, axis=-1, keepdims=True) + EPS)


def _softplus(x):
    return jnp.maximum(x, 0.0) + jnp.log1p(jnp.exp(-jnp.abs(x)))


def _rows(n):
    return lambda ref, j: ref.at[pl.ds(pl.multiple_of(j * n, n), n)]


def _cols(n):
    return lambda ref, j: ref.at[:, pl.ds(pl.multiple_of(j * n, n), n)]


def _slot(ref, j):
    return ref.at[j]


def _whole(ref, j):
    return ref


def exchange(items, *, name):
    n = len(items)

    def body(*refs):
        exchange_in_body(items, refs[:n], refs[n:2 * n], refs[2 * n:], True, True)

    return pl.pallas_call(
        body, name=name,
        out_shape=exchange_out_shapes(items),
        in_specs=[pl.BlockSpec(memory_space=pl.ANY)] * n,
        out_specs=[pl.BlockSpec(memory_space=pl.ANY)] * n,
        scratch_shapes=exchange_semaphores(items),
    )(*[it[0] for it in items])


def exchange_out_shapes(items):
    return [jax.ShapeDtypeStruct(tuple(shape), src.dtype) for (src, _, shape, _) in items]


def exchange_semaphores(items):
    n = len(items)
    return [pltpu.SemaphoreType.DMA((n, N_DEV - 1)), pltpu.SemaphoreType.DMA((n, N_DEV - 1)),
            pltpu.SemaphoreType.DMA((n,))]


def _exchange_copies(items, src_refs, out_refs, sems):
    send_sems, recv_sems, local_sems = sems
    x = lax.axis_index("x")
    y = lax.axis_index("y")
    c = lax.axis_index("c")
    me = 4 * x + 2 * y + c
    local = [pltpu.make_async_copy(src_win(src_refs[i], me), dst_win(out_refs[i], me), local_sems.at[i])
             for i, (_, src_win, _, dst_win) in enumerate(items)]
    remote = []
    for i, (_, src_win, _, dst_win) in enumerate(items):
        for k in range(1, N_DEV):
            px = lax.rem(x + ((k >> 2) & 1), 2)
            py = lax.rem(y + ((k >> 1) & 1), 2)
            pc = lax.rem(c + (k & 1), 2)
            peer = 4 * px + 2 * py + pc
            remote.append(pltpu.make_async_remote_copy(
                src_ref=src_win(src_refs[i], peer), dst_ref=dst_win(out_refs[i], me),
                send_sem=send_sems.at[i, k - 1], recv_sem=recv_sems.at[i, k - 1],
                device_id=(px, py, pc), device_id_type=MESH))
    return local, remote


def _when(cond, fn):
    if cond is True:
        fn()
    else:
        pl.when(cond)(fn)


def exchange_start(items, src_refs, out_refs, sems, cond):
    def start():
        local, remote = _exchange_copies(items, src_refs, out_refs, sems)
        for cp in local + remote:
            cp.start()
    _when(cond, start)


def exchange_finish(items, src_refs, out_refs, sems, cond):
    def finish():
        local, remote = _exchange_copies(items, src_refs, out_refs, sems)
        for cp in remote:
            cp.wait_send()
        for cp in remote:
            cp.wait_recv()
        for cp in local:
            cp.wait()
    _when(cond, finish)


def exchange_in_body(items, src_refs, out_refs, sems, first, last):
    exchange_start(items, src_refs, out_refs, sems, first)
    exchange_finish(items, src_refs, out_refs, sems, last)


def gather_blocks(src, *, name):
    return exchange([(src, _whole, (N_DEV,) + src.shape, _slot)], name=name)[0]


def _two_level_gather(src_ref, out_ref, send_sems, recv_sems, local_sem):
    x = lax.axis_index("x")
    y = lax.axis_index("y")
    c = lax.axis_index("c")
    me, sibling = (x, y, c), (x, y, 1 - c)
    chips = [(1 - x, y), (x, 1 - y), (1 - x, 1 - y)]

    def slot(px, py, pc):
        return out_ref.at[4 * px + 2 * py + pc]

    def copy(k, block, to, src=None):
        return pltpu.make_async_remote_copy(
            src_ref=slot(*block) if src is None else src, dst_ref=slot(*block),
            send_sem=send_sems.at[k], recv_sem=recv_sems.at[k], device_id=to, device_id_type=MESH)

    def own_copies():
        return ([copy(0, me, sibling, src=src_ref)]
                + [copy(1 + j, me, (*chip, c), src=src_ref) for j, chip in enumerate(chips)])

    def start():
        pltpu.make_async_copy(src_ref, slot(*me), local_sem).start()
        for cp in own_copies():
            cp.start()

    def finish():
        passed = [copy(4 + j, (*chip, c), sibling) for j, chip in enumerate(chips)]
        for j, chip in enumerate(chips):
            copy(1 + j, (*chip, c), me).wait_recv()
            passed[j].start()
        copy(0, sibling, me).wait_recv()
        for j, chip in enumerate(chips):
            copy(4 + j, (*chip, 1 - c), me).wait_recv()
        for cp in own_copies() + passed:
            cp.wait_send()
        pltpu.make_async_copy(src_ref, slot(*me), local_sem).wait()

    return start, finish


_TWO_LEVEL_SEMS = [pltpu.SemaphoreType.DMA((N_DEV - 1,)), pltpu.SemaphoreType.DMA((N_DEV - 1,)),
                   pltpu.SemaphoreType.DMA(())]


def modulate_with_gather(x2, sc, sh, src, *, tm, name):
    T, width = x2.shape
    n = T // tm

    def body(x_ref, sc_ref, sh_ref, src_ref, h_ref, out_ref, *sems):
        start, finish = _two_level_gather(src_ref, out_ref, *sems)
        pl.when(pl.program_id(0) == 0)(start)
        (h,) = fn_modulate(x_ref[...], sc_ref[...], sh_ref[...])
        h_ref[...] = h.astype(h_ref.dtype)
        pl.when(pl.program_id(0) == n - 1)(finish)

    row = pl.BlockSpec((tm, width), lambda i: (i, 0))
    full = pl.BlockSpec((1, width), lambda i: (0, 0))
    any_spec = pl.BlockSpec(memory_space=pl.ANY)
    return pl.pallas_call(
        body, name=name, grid=(n,),
        out_shape=[jax.ShapeDtypeStruct((T, width), BF16),
                   jax.ShapeDtypeStruct((N_DEV,) + src.shape, src.dtype)],
        in_specs=[row, full, full, any_spec], out_specs=[row, any_spec],
        scratch_shapes=_TWO_LEVEL_SEMS,
        compiler_params=_params(("arbitrary",)),
    )(x2, sc, sh, src)


def sum_devices(g, *, tr, name):
    _, R, C = g.shape

    def body(g_ref, o_ref):
        acc = g_ref[0].astype(F32)
        for j in range(1, N_DEV):
            acc = acc + g_ref[j].astype(F32)
        o_ref[...] = acc

    return pl.pallas_call(
        body, name=name, grid=(R // tr,),
        out_shape=jax.ShapeDtypeStruct((R, C), F32),
        in_specs=[pl.BlockSpec((N_DEV, tr, C), lambda i: (0, i, 0))],
        out_specs=pl.BlockSpec((tr, C), lambda i: (i, 0)),
        compiler_params=_params(("parallel",)),
    )(g)


def matmul(a, b, mode, out_dtype, *, name, tm=1024, tn=1024, tk=1024, n=None, comm=None,
           a_pro=None, epi=None, epi_ins=()):
    if mode == "nn":
        (M, K), (K2, N) = a.shape, b.shape
    elif mode == "nt":
        (M, K), (N, K2) = a.shape, b.shape
    else:
        (K, M), (K2, N) = a.shape, b.shape
    assert K == K2
    N = N if n is None else n
    tm, tn, tk = min(tm, M), min(tn, N), min(tk, K)
    assert M % tm == 0 and N % tn == 0 and K % tk == 0, (name, M, N, K, tm, tn, tk)
    nk = K // tk
    if mode == "tn":
        a_spec = pl.BlockSpec((tk, tm), lambda i, j, k: (k, i))
    else:
        a_spec = pl.BlockSpec((tm, tk), lambda i, j, k: (i, k))
    if mode == "nt":
        b_spec = pl.BlockSpec((tn, tk), lambda i, j, k: (j, k))
    else:
        b_spec = pl.BlockSpec((tk, tn), lambda i, j, k: (k, j))
    dims = {"nn": (1, 0), "nt": (1, 1), "tn": (0, 0)}[mode]
    items = list(comm) if comm else []
    nx = len(items)
    ne = len(epi_ins)
    gm, gn = M // tm, N // tn
    any_spec = pl.BlockSpec(memory_space=pl.ANY)
    o_spec = pl.BlockSpec((tm, tn), lambda i, j, k: (i, j))

    def body(*refs):
        a_ref, b_ref, e_refs = refs[0], refs[1], refs[2:2 + ne]
        refs = refs[2 + ne:]
        src_refs, o_ref, out_refs = refs[:nx], refs[nx], refs[1 + nx:1 + 2 * nx]
        acc_ref, sems = refs[1 + 2 * nx], refs[2 + 2 * nx:]
        i, j, k = pl.program_id(0), pl.program_id(1), pl.program_id(2)
        if items:
            exchange_start(items, src_refs, out_refs, sems, (i == 0) & (j == 0) & (k == 0))
        a_tile = a_ref[...] if a_pro is None else a_pro(a_ref[...])
        part = lax.dot_general(a_tile, b_ref[...], (((dims[0],), (dims[1],)), ((), ())),
                               preferred_element_type=F32)

        def finish(acc):
            if epi is not None:
                acc = epi(acc, *[e[...] for e in e_refs])
            o_ref[...] = acc.astype(o_ref.dtype)

        if nk == 1:
            finish(part)
        else:
            @pl.when(k == 0)
            def _():
                acc_ref[...] = part

            @pl.when((k > 0) & (k < nk - 1))
            def _():
                acc_ref[...] += part

            @pl.when(k == nk - 1)
            def _():
                finish(acc_ref[...] + part)

        if items:
            exchange_finish(items, src_refs, out_refs, sems, (i == gm - 1) & (j == gn - 1) & (k == nk - 1))

    res = pl.pallas_call(
        body, name=name, grid=(gm, gn, nk),
        out_shape=[jax.ShapeDtypeStruct((M, N), out_dtype)] + exchange_out_shapes(items),
        in_specs=[a_spec, b_spec] + [o_spec] * ne + [any_spec] * nx,
        out_specs=[o_spec] + [any_spec] * nx,
        scratch_shapes=[pltpu.VMEM((tm, tn) if nk > 1 else (8, 128), F32)]
        + (exchange_semaphores(items) if items else []),
        compiler_params=_params(("arbitrary",) * 3 if items else ("parallel", "parallel", "arbitrary")),
    )(a, b, *epi_ins, *[it[0] for it in items])
    return res if items else res[0]


def rowwise_call(body_fn, rows, fulls, row_outs, acc_outs, *, tm, name, mm=None, tk=1024):
    rows = [r if isinstance(r, tuple) else (r, r.shape[1], 0) for r in rows]
    T = rows[0][0].shape[0]
    tm = min(tm, T)
    assert T % tm == 0
    n_r, n_f, n_ro = len(rows), len(fulls), len(row_outs)
    into = [(k, ro) for k, ro in enumerate(row_outs) if len(ro) == 3]
    n_b = len(into)
    n_mm, nk = 0, 1
    if mm is not None:
        a, b, mode, pos, a_pro = mm
        n_mm = 2
        K = a.shape[1]
        N = b.shape[1] if mode == "nn" else b.shape[0]
        tk = min(tk, K)
        assert K % tk == 0 and a.shape[0] == T
        nk = K // tk
        b_contract = 0 if mode == "nn" else 1

    def row_body(refs, product):
        r_refs = refs[:n_r]
        f_refs = refs[n_r:n_r + n_f]
        refs = refs[n_r + n_f + n_b:]
        ro_refs = refs[:n_ro]
        ao_refs = refs[n_ro:n_ro + len(acc_outs)]
        r_vals = [r[...].astype(F32) for r in r_refs]
        if product is not None:
            r_vals.insert(pos, product)
        f_vals = [f[...].astype(F32) for f in f_refs]
        ro, ao = body_fn(r_vals, f_vals)
        for ref, v in zip(ro_refs, ro):
            ref[...] = v.astype(ref.dtype)
        if ao_refs:
            @pl.when(pl.program_id(0) == 0)
            def _():
                for ref in ao_refs:
                    ref[...] = jnp.zeros(ref.shape, F32)
            for ref, v in zip(ao_refs, ao):
                ref[...] += v.reshape(ref.shape)

    def body(*refs):
        if mm is None:
            return row_body(refs, None)
        a_ref, b_ref, rest, acc_ref = refs[0], refs[1], refs[2:-1], refs[-1]
        k = pl.program_id(1)
        a_tile = a_ref[...] if a_pro is None else a_pro(a_ref[...])
        part = lax.dot_general(a_tile, b_ref[...], (((1,), (b_contract,)), ((), ())),
                               preferred_element_type=F32)
        if nk == 1:
            return row_body(rest, part)

        @pl.when(k == 0)
        def _():
            acc_ref[...] = part

        @pl.when((k > 0) & (k < nk - 1))
        def _():
            acc_ref[...] += part

        @pl.when(k == nk - 1)
        def _():
            row_body(rest, acc_ref[...] + part)

    def full_spec(shape):
        nd = len(shape)
        return pl.BlockSpec(tuple(shape), lambda i, *_: (0,) * nd)

    def row_spec(w, off):
        return pl.BlockSpec((tm, w), functools.partial(lambda i, *_, o: (i, o), o=off // w))

    in_specs = []
    if mm is not None:
        in_specs.append(pl.BlockSpec((tm, tk), lambda i, k: (i, k)))
        in_specs.append(pl.BlockSpec((tk, N), lambda i, k: (k, 0)) if mode == "nn" else
                        pl.BlockSpec((N, tk), lambda i, k: (0, k)))
    in_specs += [row_spec(w, off) for (_, w, off) in rows]
    in_specs += [full_spec(f.shape) for f in fulls]
    in_specs += [pl.BlockSpec(memory_space=pl.ANY)] * n_b
    out_specs, out_shape = [], []
    for ro in row_outs:
        if len(ro) == 3:
            buf, w, off = ro
            out_specs.append(row_spec(w, off))
            out_shape.append(jax.ShapeDtypeStruct(buf.shape, buf.dtype))
        else:
            w, dt = ro
            out_specs.append(row_spec(w, 0))
            out_shape.append(jax.ShapeDtypeStruct((T, w), dt))
    out_specs += [full_spec(s) for s in acc_outs]
    out_shape += [jax.ShapeDtypeStruct(tuple(s), F32) for s in acc_outs]
    aliases = {n_mm + n_r + n_f + b: k for b, (k, _) in enumerate(into)}
    return pl.pallas_call(
        body, name=name, grid=(T // tm,) if mm is None else (T // tm, nk),
        out_shape=out_shape, in_specs=in_specs, out_specs=out_specs,
        scratch_shapes=[] if mm is None else [pltpu.VMEM((tm, N) if nk > 1 else (8, 128), F32)],
        input_output_aliases=aliases,
        compiler_params=_params(("arbitrary",) if mm is None else ("arbitrary", "arbitrary")),
    )(*([] if mm is None else [a, b]), *[r[0] for r in rows], *fulls, *[ro[0] for _, ro in into])


def fwd_body(fn):
    return lambda r, f: (fn(*r, *f), ())


def bwd_body(fn, n_rows):
    def body(r, f):
        ins, cots = r[:n_rows], r[n_rows:]
        _, vjp = jax.vjp(fn, *ins, *f)
        g = vjp(tuple(cots))
        return g[:n_rows], g[n_rows:]
    return body


def whole_call(fn, ins, outs, *, name):
    n_in = len(ins)

    def body(*refs):
        res = fn(*[r[...] for r in refs[:n_in]])
        for ref, v in zip(refs[n_in:], res):
            ref[...] = v.astype(ref.dtype)

    return pl.pallas_call(
        body, name=name,
        out_shape=[jax.ShapeDtypeStruct(tuple(s), dt) for (s, dt) in outs],
        compiler_params=_params(),
    )(*ins)


def fn_modulate(x, sc, sh):
    return (_rms(x) * (1.0 + sc) + sh,)


def fn_sgu(u, v, nw, ws, bs):
    ug = jax.nn.gelu(u)
    vn = _rms(jax.nn.gelu(v)) * nw
    ri = lax.broadcasted_iota(jnp.int32, (Q, Q), 0)
    ci = lax.broadcasted_iota(jnp.int32, (Q, Q), 1)
    causal = ri >= ci
    chunks = []
    for n in range(u.shape[0] // Q):
        vc = vn[n * Q:(n + 1) * Q]
        cols = [dot_nn(jnp.where(causal, ws[g], 0.0), vc[:, g * Q:(g + 1) * Q]) + bs[g]
                for g in range(GM_GROUPS)]
        chunks.append(jnp.concatenate(cols, axis=1))
    sv = chunks[0] if len(chunks) == 1 else jnp.concatenate(chunks, axis=0)
    return (ug * sv,)


def fn_mix(ga, gb, pa, pb):
    return (jax.nn.sigmoid(ga) * pa + jax.nn.sigmoid(gb) * pb,)


def fn_res_modulate(x, o, g1, sc2, sh2):
    x1 = x + g1 * o
    return x1, _rms(x1) * (1.0 + sc2) + sh2


def relu2_tile(f):
    return jnp.square(jnp.maximum(f.astype(F32), 0.0)).astype(BF16)


def relu2_grad_tile(dact, f):
    return dact * (2.0 * jnp.maximum(f.astype(F32), 0.0))


def final_body(r, f):
    x1, gf, tgt = r
    g2, fnw = f

    def loss_fn(x1, gf, g2, fnw):
        y = _rms(x1 + g2 * gf) * fnw
        row = 0.5 * jnp.mean(jnp.square(y - tgt), axis=-1, keepdims=True)
        return jnp.sum(row, axis=0, keepdims=True)

    l, vjp = jax.vjp(loss_fn, x1, gf, g2, fnw)
    dx1, dgf, dg2, dfnw = vjp(jnp.ones((1, 1), F32))
    return (dx1, dgf), (jnp.broadcast_to(l, (1, 128)), dg2, dfnw)


def grad_x_body(r, f):
    x, dh, dxa = r
    _, vjp = jax.vjp(fn_modulate, x, *f)
    dx, dsc, dsh = vjp((dh,))
    return (dx + dxa,), (dsc, dsh)


CONV_CW = 128
CONV_PAD = 8
CONV_ROWS = 128


def _conv_pre(xp, w_ref, b_ref, r0, R):
    acc = b_ref[...] + w_ref[0:1, :] * xp[r0 + CONV_PAD - 3:r0 + CONV_PAD - 3 + R, :]
    for k in range(1, CONV_K):
        s = r0 + CONV_PAD - 3 + k
        acc = acc + w_ref[k:k + 1, :] * xp[s:s + R, :]
    return acc


def conv_fwd(proj, conv_w, conv_b):
    T = proj.shape[0]
    R = min(CONV_ROWS, T)

    def body(x_ref, w_ref, b_ref, o_ref, xp):
        xp[0:CONV_PAD, :] = jnp.zeros((CONV_PAD, CONV_CW), F32)
        xp[CONV_PAD:CONV_PAD + T, :] = x_ref[...].astype(F32)
        for r0 in range(0, T, R):
            pre = _conv_pre(xp, w_ref, b_ref, r0, R)
            o_ref[r0:r0 + R, :] = (pre * jax.nn.sigmoid(pre)).astype(o_ref.dtype)

    return pl.pallas_call(
        body, name="conv_fwd", grid=(CONV_DIM // CONV_CW,),
        out_shape=jax.ShapeDtypeStruct((T, CONV_DIM), BF16),
        in_specs=[pl.BlockSpec((T, CONV_CW), lambda j: (0, P_XBC // CONV_CW + j)),
                  pl.BlockSpec((CONV_K, CONV_CW), lambda j: (0, j)),
                  pl.BlockSpec((1, CONV_CW), lambda j: (0, j))],
        out_specs=pl.BlockSpec((T, CONV_CW), lambda j: (0, j)),
        scratch_shapes=[pltpu.VMEM((T + CONV_PAD, CONV_CW), F32)],
        compiler_params=_params(("parallel",)),
    )(proj, conv_w, conv_b)


def conv_bwd(proj, dact, col0, conv_w, conv_b, dproj, *, name):
    T = proj.shape[0]
    R = min(CONV_ROWS, T)
    nb = dact.shape[1] // CONV_CW
    c0 = col0 // CONV_CW
    x0 = (P_XBC + col0) // CONV_CW

    def body(x_ref, d_ref, w_ref, b_ref, _, dx_ref, dw_ref, db_ref, xp, dp):
        xp[0:CONV_PAD, :] = jnp.zeros((CONV_PAD, CONV_CW), F32)
        xp[CONV_PAD:CONV_PAD + T, :] = x_ref[...].astype(F32)
        dp[T:T + CONV_PAD, :] = jnp.zeros((CONV_PAD, CONV_CW), F32)
        dws = [jnp.zeros((1, CONV_CW), F32) for _ in range(CONV_K)]
        db = jnp.zeros((1, CONV_CW), F32)
        for r0 in range(0, T, R):
            pre = _conv_pre(xp, w_ref, b_ref, r0, R)
            s = jax.nn.sigmoid(pre)
            dpre = d_ref[r0:r0 + R, :].astype(F32) * (s * (1.0 + pre * (1.0 - s)))
            dp[r0:r0 + R, :] = dpre
            db = db + jnp.sum(dpre, axis=0, keepdims=True)
            for k in range(CONV_K):
                st = r0 + CONV_PAD - 3 + k
                dws[k] = dws[k] + jnp.sum(dpre * xp[st:st + R, :], axis=0, keepdims=True)
        for r0 in range(0, T, R):
            acc = w_ref[0:1, :] * dp[r0 + 3:r0 + 3 + R, :]
            for k in range(1, CONV_K):
                acc = acc + w_ref[k:k + 1, :] * dp[r0 + 3 - k:r0 + 3 - k + R, :]
            dx_ref[r0:r0 + R, :] = acc.astype(dx_ref.dtype)
        for k in range(CONV_K):
            dw_ref[k:k + 1, :] = dws[k]
        db_ref[...] = db

    return pl.pallas_call(
        body, name=name, grid=(nb,),
        out_shape=[jax.ShapeDtypeStruct(dproj.shape, dproj.dtype),
                   jax.ShapeDtypeStruct((CONV_K, nb * CONV_CW), F32),
                   jax.ShapeDtypeStruct((1, nb * CONV_CW), F32)],
        in_specs=[pl.BlockSpec((T, CONV_CW), lambda j: (0, x0 + j)),
                  pl.BlockSpec((T, CONV_CW), lambda j: (0, j)),
                  pl.BlockSpec((CONV_K, CONV_CW), lambda j: (0, c0 + j)),
                  pl.BlockSpec((1, CONV_CW), lambda j: (0, c0 + j)),
                  pl.BlockSpec(memory_space=pl.ANY)],
        out_specs=[pl.BlockSpec((T, CONV_CW), lambda j: (0, x0 + j)),
                   pl.BlockSpec((CONV_K, CONV_CW), lambda j: (0, j)),
                   pl.BlockSpec((1, CONV_CW), lambda j: (0, j))],
        scratch_shapes=[pltpu.VMEM((T + CONV_PAD, CONV_CW), F32),
                        pltpu.VMEM((T + CONV_PAD, CONV_CW), F32)],
        input_output_aliases={4: 0},
        compiler_params=_params(("parallel",)),
    )(proj, dact, conv_w, conv_b, dproj)


def _split3(a):
    hi = a.astype(BF16)
    r = a - hi.astype(F32)
    mid = r.astype(BF16)
    return hi, mid, (r - mid.astype(F32)).astype(BF16)


def _dg3(a, m, ca, cm, a_first):
    dims = (((ca,), (cm,)), ((), ())) if a_first else (((cm,), (ca,)), ((), ()))
    out = None
    for p in _split3(a):
        t = lax.dot_general(p, m, dims, preferred_element_type=F32) if a_first else \
            lax.dot_general(m, p, dims, preferred_element_type=F32)
        out = t if out is None else out + t
    return out


@jax.custom_vjp
def exact_right(a, m):
    return _dg3(a, m, 1, 0, True)


@jax.custom_vjp
def exact_left(m, a):
    return _dg3(a, m, 0, 1, False)


def _expand_bwd(m, g):
    hi = g.astype(BF16)
    lo = (g - hi.astype(F32)).astype(BF16)
    out = lax.dot_general(jnp.concatenate([hi, lo], axis=1), jnp.concatenate([m, m], axis=1),
                          (((1,), (1,)), ((), ())), preferred_element_type=F32)
    return out, jnp.zeros_like(m)


exact_right.defvjp(lambda a, m: (exact_right(a, m), m), _expand_bwd)
exact_left.defvjp(lambda m, a: (exact_left(m, a), m),
                  lambda m, g: (jnp.zeros_like(m), _dg3(g, m, 0, 0, False)))


def ssd_step(lane0, per_block, state, x, z, dtr, Bm, Cm, dtb, alog, dsk, nw):
    def iota(shape, dim):
        return lax.broadcasted_iota(jnp.int32, shape, dim)

    def one_hot(mask):
        return mask.astype(F32).astype(BF16)

    causal = iota((Q, Q), 0) >= iota((Q, Q), 1)
    eye = iota((Q, Q), 0) == iota((Q, Q), 1)
    lane = iota((1, 128), 1)
    colh = lax.shift_right_logical(iota((1, SSM_GW), 1), 6)
    to_cols = one_hot(iota((128, SSM_GW), 0) == lane0 + colh)

    dt_all = _softplus(dtr + dtb)
    a_all = dt_all * (-jnp.exp(alog))
    cum_all = exact_left(one_hot(causal), a_all)
    both = exact_right(jnp.concatenate([dt_all, cum_all], axis=0), to_cols)
    dt_f, cum_f = both[:Q], both[Q:]
    last_f = jnp.sum(jnp.where(iota((Q, 1), 0) == Q - 1, cum_f, 0.0), axis=0, keepdims=True)
    dsk_f = jnp.zeros((1, SSM_GW), F32)
    for h in range(SSM_HPG):
        dsk_f = jnp.where(colh == h, dsk[h], dsk_f)

    xdt = x * dt_f
    cb = dot_nt(Cm, Bm)
    ms = []
    for h in range(SSM_HPG):
        ch = jnp.sum(jnp.where(lane == lane0 + h, cum_all, 0.0), axis=1, keepdims=True)
        ch_t = jnp.sum(jnp.where(eye, ch, 0.0), axis=0, keepdims=True)
        ms.append(cb * jnp.exp(jnp.where(causal, ch - ch_t, -1e30)))
    if per_block:
        first_half = lane < SSM_P
        blocks = []
        for b in range(SSM_HPG // 2):
            xb = xdt[:, b * 128:(b + 1) * 128]
            rhs = jnp.concatenate([jnp.where(first_half, xb, 0.0), jnp.where(first_half, 0.0, xb)], axis=0)
            blocks.append(dot_nn(jnp.concatenate(ms[2 * b:2 * b + 2], axis=1), rhs))
        y = jnp.concatenate(blocks, axis=1)
    else:
        rhs = jnp.concatenate([jnp.where(colh == h, xdt, 0.0) for h in range(SSM_HPG)], axis=0)
        y = dot_nn(jnp.concatenate(ms, axis=1), rhs)
    y = y + dot_nn(Cm, state) * jnp.exp(cum_f) + x * dsk_f
    new_state = state * jnp.exp(last_f) + dot_tn(Bm, xdt * jnp.exp(last_f - cum_f))
    gated = y * (z * jax.nn.sigmoid(z))
    return new_state, _rms(gated) * nw


SSD_GPS = 4
_XW = SSD_GPS * SSM_GW
_BW = SSD_GPS * 128


def _ssd_in_specs(rev, nc):
    def n_of(n):
        return nc - 1 - n if rev else n
    return [
        pl.BlockSpec((Q, _XW), lambda g, n: (n_of(n), g)),
        pl.BlockSpec((Q, _BW), lambda g, n: (n_of(n), SSM_INNER // _BW + g)),
        pl.BlockSpec((Q, _BW), lambda g, n: (n_of(n), (SSM_INNER + SSM_GROUPS * 128) // _BW + g)),
        pl.BlockSpec((Q, _XW), lambda g, n: (n_of(n), P_Z // _XW + g)),
        pl.BlockSpec((Q, 128), lambda g, n: (n_of(n), 0)),
        pl.BlockSpec((1, 128), lambda g, n: (0, 0)),
        pl.BlockSpec((1, 128), lambda g, n: (0, 0)),
        pl.BlockSpec((SSD_GPS, SSM_HPG, 1, 1), lambda g, n: (g, 0, 0, 0)),
        pl.BlockSpec((1, _XW), lambda g, n: (0, g)),
    ]


def _ssd_group_inputs(gi, x_ref, b_ref, c_ref, z_ref, dt_ref, dtb_ref, al_ref, dk_ref, nw_ref):
    xs = slice(gi * SSM_GW, (gi + 1) * SSM_GW)
    bs = slice(gi * 128, (gi + 1) * 128)
    return (x_ref[:, xs].astype(F32), z_ref[:, xs].astype(F32), dt_ref[...],
            b_ref[:, bs], c_ref[:, bs],
            dtb_ref[...], al_ref[...], dk_ref[gi], nw_ref[:, xs])


def ssd_fwd(xact, proj, dtg, dtb, alog, dsk, nw, comm):
    T = xact.shape[0]
    nc = T // Q
    nx = len(comm)
    ng = SSM_GROUPS // SSD_GPS
    any_spec = pl.BlockSpec(memory_space=pl.ANY)

    def body(*refs):
        in_refs, src_refs = refs[:9], refs[9:9 + nx]
        yb_ref, st_ref = refs[9 + nx:11 + nx]
        out_refs, state, sems = refs[11 + nx:11 + 2 * nx], refs[11 + 2 * nx], refs[12 + 2 * nx:]
        g, n = pl.program_id(0), pl.program_id(1)
        exchange_start(comm, src_refs, out_refs, sems, (g == 0) & (n == 0))

        @pl.when(n == 0)
        def _():
            state[...] = jnp.zeros(state.shape, F32)

        for gi in range(SSD_GPS):
            lane0 = SSM_HPG * (SSD_GPS * pl.program_id(0) + gi)
            s = state[gi]
            st_ref[gi, 0] = s
            new_s, yb = ssd_step(lane0, False, s, *_ssd_group_inputs(gi, *in_refs))
            state[gi] = new_s
            yb_ref[:, gi * SSM_GW:(gi + 1) * SSM_GW] = yb.astype(yb_ref.dtype)

        exchange_finish(comm, src_refs, out_refs, sems, (g == ng - 1) & (n == nc - 1))

    return pl.pallas_call(
        body, name="ssd_fwd", grid=(ng, nc),
        out_shape=[jax.ShapeDtypeStruct((T, SSM_INNER), BF16),
                   jax.ShapeDtypeStruct((SSM_GROUPS, nc, 128, SSM_GW), F32)] + exchange_out_shapes(comm),
        in_specs=_ssd_in_specs(False, nc) + [any_spec] * nx,
        out_specs=[pl.BlockSpec((Q, _XW), lambda g, n: (n, g)),
                   pl.BlockSpec((SSD_GPS, 1, 128, SSM_GW), lambda g, n: (g, n, 0, 0))] + [any_spec] * nx,
        scratch_shapes=[pltpu.VMEM((SSD_GPS, 128, SSM_GW), F32)] + exchange_semaphores(comm),
        compiler_params=_params(("arbitrary", "arbitrary")),
    )(xact, xact, xact, proj, dtg, dtb, alog, dsk, nw, *[it[0] for it in comm])


def ssd_bwd(xact, proj, dtg, dtb, alog, dsk, nw, states, dyb, dproj, comm):
    T = xact.shape[0]
    nc = T // Q

    nx = len(comm)
    ng = SSM_GROUPS // SSD_GPS
    any_spec = pl.BlockSpec(memory_space=pl.ANY)

    def body(*refs):
        in_refs, (st_ref, dy_ref, _) = refs[:9], refs[9:12]
        src_refs, refs = refs[12:12 + nx], refs[12 + nx:]
        dx_ref, db_ref, dc_ref, dz_ref, ddt_ref, ddtb_ref, dal_ref, ddk_ref, dnw_ref = refs[:9]
        out_refs, dstate, sems = refs[9:9 + nx], refs[9 + nx], refs[10 + nx:]
        exchange_start(comm, src_refs, out_refs, sems, (pl.program_id(0) == 0) & (pl.program_id(1) == 0))

        @pl.when(pl.program_id(1) == 0)
        def _():
            dstate[...] = jnp.zeros(dstate.shape, F32)
            ddtb_ref[...] = jnp.zeros(ddtb_ref.shape, F32)
            dal_ref[...] = jnp.zeros(dal_ref.shape, F32)
            ddk_ref[...] = jnp.zeros(ddk_ref.shape, F32)
            dnw_ref[...] = jnp.zeros(dnw_ref.shape, F32)

        for gi in range(SSD_GPS):
            xs = slice(gi * SSM_GW, (gi + 1) * SSM_GW)
            bs = slice(gi * 128, (gi + 1) * 128)
            lane0 = SSM_HPG * (SSD_GPS * pl.program_id(0) + gi)
            ins = (st_ref[gi, 0],) + _ssd_group_inputs(gi, *in_refs)
            _, vjp = jax.vjp(functools.partial(ssd_step, lane0, True), *ins)
            ds, dx, dz, ddt, dbm, dcm, ddtb, dal, ddk, dnw = vjp((dstate[gi], dy_ref[:, xs].astype(F32)))
            dstate[gi] = ds
            dx_ref[:, xs] = dx.astype(dx_ref.dtype)
            db_ref[:, bs] = dbm.astype(db_ref.dtype)
            dc_ref[:, bs] = dcm.astype(dc_ref.dtype)
            dz_ref[:, xs] = dz.astype(dz_ref.dtype)
            ddt_ref[:, bs] = ddt.astype(ddt_ref.dtype)
            ddtb_ref[gi] += ddtb
            dal_ref[gi] += dal
            ddk_ref[gi] += ddk
            dnw_ref[:, xs] += dnw

        exchange_finish(comm, src_refs, out_refs, sems,
                        (pl.program_id(0) == ng - 1) & (pl.program_id(1) == nc - 1))

    rev = lambda n: nc - 1 - n
    row_shape = jax.ShapeDtypeStruct((SSM_GROUPS, 1, 128), F32)
    row_spec = pl.BlockSpec((SSD_GPS, 1, 128), lambda g, n: (g, 0, 0))
    return pl.pallas_call(
        body, name="ssd_bwd", grid=(ng, nc),
        out_shape=[jax.ShapeDtypeStruct((T, SSM_INNER), BF16),
                   jax.ShapeDtypeStruct((T, SSM_GROUPS * 128), BF16),
                   jax.ShapeDtypeStruct((T, SSM_GROUPS * 128), BF16),
                   jax.ShapeDtypeStruct(dproj.shape, dproj.dtype),
                   jax.ShapeDtypeStruct((T, SSM_GROUPS * 128), BF16),
                   row_shape, row_shape,
                   jax.ShapeDtypeStruct((SSM_GROUPS, SSM_HPG, 1, 1), F32),
                   jax.ShapeDtypeStruct((1, SSM_INNER), F32)] + exchange_out_shapes(comm),
        in_specs=_ssd_in_specs(True, nc) + [
            pl.BlockSpec((SSD_GPS, 1, 128, SSM_GW), lambda g, n: (g, rev(n), 0, 0)),
            pl.BlockSpec((Q, _XW), lambda g, n: (rev(n), g)),
            any_spec] + [any_spec] * nx,
        out_specs=[pl.BlockSpec((Q, _XW), lambda g, n: (rev(n), g)),
                   pl.BlockSpec((Q, _BW), lambda g, n: (rev(n), g)),
                   pl.BlockSpec((Q, _BW), lambda g, n: (rev(n), g)),
                   pl.BlockSpec((Q, _XW), lambda g, n: (rev(n), P_Z // _XW + g)),
                   pl.BlockSpec((Q, _BW), lambda g, n: (rev(n), g)),
                   row_spec, row_spec,
                   pl.BlockSpec((SSD_GPS, SSM_HPG, 1, 1), lambda g, n: (g, 0, 0, 0)),
                   pl.BlockSpec((1, _XW), lambda g, n: (0, g))] + [any_spec] * nx,
        scratch_shapes=[pltpu.VMEM((SSD_GPS, 128, SSM_GW), F32)] + exchange_semaphores(comm),
        input_output_aliases={11: 3},
        compiler_params=_params(("arbitrary", "arbitrary")),
    )(xact, xact, xact, proj, dtg, dtb, alog, dsk, nw, states, dyb, dproj, *[it[0] for it in comm])


ADAMW_WHOLE_ELEMS = 256 * 1024


def adamw(w, g, m, v, *, name):
    shape = w.shape
    parts = g.shape != shape
    nd = len(shape)
    if nd == 3 and shape[1] == 1 and w.size > ADAMW_WHOLE_ELEMS:
        assert not parts and shape[0] % 4 == 0
        grid = (4,)
        spec = g_spec = pl.BlockSpec((shape[0] // 4, 1, shape[2]), lambda i: (i, 0, 0))
    else:
        if w.size <= ADAMW_WHOLE_ELEMS:
            grid, tr = (1,), shape[-2]
        else:
            assert all(s == 1 for s in shape[:-2]) and shape[-2] % 256 == 0
            grid, tr = (shape[-2] // 256,), 256
        blk = tuple(shape[:-2]) + (tr, shape[-1])
        spec = pl.BlockSpec(blk, lambda i: (0,) * (nd - 2) + (i, 0))
        g_spec = pl.BlockSpec((N_DEV,) + blk[1:], lambda i: (0,) * (nd - 2) + (i, 0)) if parts else spec

    def body(w_ref, g_ref, m_ref, v_ref, go_ref, d_ref, nm_ref, nv_ref):
        if parts:
            g = g_ref[0:1].astype(F32)
            for j in range(1, N_DEV):
                g = g + g_ref[j:j + 1].astype(F32)
        else:
            g = g_ref[...]
        nm = ADAM_B1 * m_ref[...] + (1.0 - ADAM_B1) * g
        nv = ADAM_B2 * v_ref[...] + (1.0 - ADAM_B2) * jnp.square(g)
        m_hat = nm / (1.0 - ADAM_B1 ** ADAM_STEP)
        v_hat = nv / (1.0 - ADAM_B2 ** ADAM_STEP)
        go_ref[...] = g
        d_ref[...] = -ADAM_LR * (m_hat / (jnp.sqrt(v_hat) + ADAM_EPS) + ADAM_WD * w_ref[...])
        nm_ref[...] = nm
        nv_ref[...] = nv

    shp = jax.ShapeDtypeStruct(shape, F32)
    return pl.pallas_call(
        body, name=name, grid=grid,
        out_shape=[shp] * 4, in_specs=[spec, g_spec, spec, spec], out_specs=[spec] * 4,
        compiler_params=_params(("parallel",)),
    )(w, g, m, v)


def _pad_rows(a, rows):
    return jnp.pad(a, ((0, rows - a.shape[0]), (0, 0)))


WIN_W = 1408
N_IN = IN_WIDTH // N_DEV
_A6 = OFF_DT - 6 * N_IN
_C6 = 7 * N_IN - OFF_GA


_WIN_OFFSETS = (0, 4, 8, 12, 16, 20, None, 124)


def _w_in_window(shard, me):
    rows = shard.shape[0]
    z = lambda n: jnp.zeros((rows, n), shard.dtype)

    def plain(off):
        return lambda s: jnp.pad(s, ((0, 0), (off, WIN_W - N_IN - off)))

    def split(s):
        return jnp.concatenate([z(24), s[:, :_A6], s[:, _A6 + 32:], z(4), s[:, _A6:_A6 + 32], z(96)], axis=1)

    return lax.switch(me, [split if off is None else plain(off) for off in _WIN_OFFSETS], shard)


def _w_in_from_window(window, me):
    def plain(off):
        return lambda w: w[:, off:off + N_IN]

    def split(w):
        return jnp.concatenate([w[:, 24:24 + _A6], w[:, 1280:1312], w[:, 24 + _A6:24 + _A6 + _C6]], axis=1)

    return lax.switch(me, [split if off is None else plain(off) for off in _WIN_OFFSETS], window)


def _w_all_from_windows(g):
    def merge_first(p, t):
        return jnp.concatenate([p[:, :128] + t, p[:, 128:]], axis=1)

    parts = [g[0][:, :1280]]
    for j in range(1, 6):
        parts.append(merge_first(g[j][:, :1280], g[j - 1][:, 1280:]))
    p6 = merge_first(g[6][:, :1280], g[5][:, 1280:])
    parts.append(jnp.concatenate([p6[:, :1152], p6[:, 1152:] + g[7][:, :128]], axis=1))
    parts.append(g[7][:, 128:])
    parts.append(g[6][:, 1280:])
    return jnp.concatenate(parts, axis=1)


def _windows_of_w_all(gw):
    wins = [gw[:, 1280 * j:1280 * j + WIN_W] for j in range(6)]
    wins.append(jnp.concatenate([gw[:, 7680:8960], gw[:, PROJ_W:]], axis=1))
    wins.append(gw[:, 8832:PROJ_W])
    return jnp.stack(wins)


def kernel(x, c, w_mod, b_mod, w_in, gm_norm_w, gm_ws, gm_bs, conv_w, conv_b, dt_bias, a_log, d_skip, ssm_norm_w, w_branch_gm, w_branch_ssm, w_out, w_ff1, w_ff2, final_norm_w, loss_target, m_w_mod, m_b_mod, m_w_in, m_gm_norm_w, m_gm_ws, m_gm_bs, m_conv_w, m_conv_b, m_dt_bias, m_a_log, m_d_skip, m_ssm_norm_w, m_w_branch_gm, m_w_branch_ssm, m_w_out, m_w_ff1, m_w_ff2, m_final_norm_w, v_w_mod, v_b_mod, v_w_in, v_gm_norm_w, v_gm_ws, v_gm_bs, v_conv_w, v_conv_b, v_dt_bias, v_a_log, v_d_skip, v_ssm_norm_w, v_w_branch_gm, v_w_branch_ssm, v_w_out, v_w_ff1, v_w_ff2, v_final_norm_w):
    T = x.shape[1]
    me = 4 * lax.axis_index("x") + 2 * lax.axis_index("y") + lax.axis_index("c")
    x2 = x[0]
    tgt = loss_target[0]
    n_in = IN_WIDTH // N_DEV
    n_mod = N_MOD * D // N_DEV
    n_cv = CONV_DIM // N_DEV

    c_all, conv_w_full = exchange(
        [(c.reshape(8, 128), _whole, (N_DEV, 8, 128), _slot),
         (conv_w[0], _whole, (N_DEV, CONV_K, n_cv), _slot)], name="gather_c_convw")
    c_all = c_all.reshape(N_DEV, D)
    conv_w_full = conv_w_full.transpose(1, 0, 2).reshape(CONV_K, CONV_DIM)

    win = _w_in_window(w_in[0].astype(BF16), me)
    late_weights = [
        (w_branch_gm[0].astype(BF16), _whole, (D, D), _rows(D // N_DEV)),
        (w_branch_ssm[0].astype(BF16), _whole, (SSM_INNER, D), _rows(SSM_INNER // N_DEV)),
        (w_out[0].astype(BF16), _whole, (D, D), _rows(D // N_DEV)),
        (w_ff1[0].astype(BF16), _whole, (D, D_FF), _cols(D_FF // N_DEV)),
        (w_ff2[0].astype(BF16), _whole, (D_FF, D), _rows(D_FF // N_DEV))]

    c_pad = _pad_rows(c_all, 128)
    b_mine = lax.dynamic_slice(b_mod, (0, me * n_mod), (1, n_mod))

    def mod_fn(cp, w, b):
        ca = cp * jax.nn.sigmoid(cp)
        return (jnp.dot(ca, w, precision=HIGHEST, preferred_element_type=F32) + b,)

    (mod_part,) = whole_call(mod_fn, [c_pad, w_mod[0], b_mine], [((128, n_mod), F32)], name="mod_fwd")
    gmod = gather_blocks(mod_part[:N_DEV], name="gather_mod")
    mod = lax.dynamic_index_in_dim(gmod, me, axis=1, keepdims=False).reshape(N_MOD, D)
    sh1, sc1, gt1, sh2, sc2, gt2 = [mod[i:i + 1] for i in range(N_MOD)]

    h, gwin = modulate_with_gather(x2, sc1, sh1, win, tm=256, name="modulate1_gather_w_in")
    w_all = _w_all_from_windows(gwin)
    w_dt = w_all[:, PROJ_W:]
    proj = matmul(h, w_all, "nn", BF16, name="mm_proj", n=PROJ_W, tm=K1_TM)
    dtg = matmul(h, w_dt, "nn", F32, name="mm_dt")
    ws = gm_ws[0]
    bs3 = gm_bs[0].reshape(GM_GROUPS, Q, 1)
    sgu_rows = [(proj, D, P_U), (proj, D, P_V)]
    (ya,) = rowwise_call(fwd_body(fn_sgu), sgu_rows, [gm_norm_w, ws, bs3], [(D, BF16)], [],
                         tm=256, name="sgu_fwd")
    xact = conv_fwd(proj, conv_w_full, conv_b)
    dtb4 = jnp.pad(dt_bias, ((0, 0), (0, 96)))
    alog4 = jnp.pad(a_log, ((0, 0), (0, 96)))
    dsk4 = d_skip.reshape(SSM_GROUPS, SSM_HPG, 1, 1)
    yb, states, w_gm_f, w_ssm_f, w_out_f, w_ff1_f, w_ff2_f = ssd_fwd(
        xact, proj, dtg, dtb4, alog4, dsk4, ssm_norm_w, late_weights)
    pa = matmul(ya, w_gm_f, "nn", BF16, name="mm_branch_gm", tm=K1_TM)
    gate_rows = [(proj, D, P_GA), (proj, D, P_GB)]
    mixed, pb = rowwise_call(
        lambda r, fl: (fn_mix(*r) + (r[3],), ()), gate_rows + [pa], [], [(D, BF16), (D, BF16)], [],
        tm=FUSED_TM, name="branch_ssm_mix", mm=(yb, w_ssm_f, "nn", 3, None), tk=SSM_INNER)
    x1, h2, o = rowwise_call(
        lambda r, fl: (fn_res_modulate(*r, *fl) + (r[1],), ()), [x2], [gt1, sc2, sh2],
        [(D, F32), (D, BF16), (D, F32)], [], tm=FUSED_TM, name="out_res_modulate2",
        mm=(mixed, w_out_f, "nn", 1, None))
    f = matmul(h2, w_ff1_f, "nn", BF16, name="mm_ff1", tm=K1_TM)

    dx1, dgf, loss_v, dgt2, dfnw = rowwise_call(
        final_body, [x1, tgt], [gt2, final_norm_w.reshape(1, D)], [(D, F32), (D, BF16)],
        [(1, 128), (1, D), (1, D)], tm=FUSED_TM, name="ff2_loss_bwd", mm=(f, w_ff2_f, "nn", 1, relu2_tile),
        tk=D_FF)
    df = matmul(dgf, w_ff2_f, "nt", BF16, name="mm_ff2_dgrad", epi=relu2_grad_tile, epi_ins=(f,), tm=K1_TM)
    gw_ff2 = matmul(f, dgf, "tn", BF16, name="mm_ff2_wgrad", tk=WGRAD_TK, a_pro=relu2_tile)
    gw_ff1 = matmul(h2, df, "tn", BF16, name="mm_ff1_wgrad", tk=WGRAD_TK)

    def res_mod_bwd(r, fl):
        xv, ov, dx1v, dh2v = r
        _, vjp = jax.vjp(fn_res_modulate, xv, ov, *fl)
        dxv, dov, dg1, dsc, dsh = vjp((dx1v, dh2v))
        return (dxv, dov), (dg1, dsc, dsh)

    dxa, do, dgt1, dsc2, dsh2 = rowwise_call(
        res_mod_bwd, [x2, o, dx1], [gt1, sc2, sh2], [(D, F32), (D, BF16)],
        [(1, D), (1, D), (1, D)], tm=FUSED_TM, name="ff1_dgrad_res_modulate2_bwd",
        mm=(df, w_ff1_f, "nt", 3, None), tk=D_FF)
    gw_out = matmul(mixed, do, "tn", BF16, name="mm_out_wgrad", tk=WGRAD_TK)
    dproj = lax.empty((T, ALL_W), BF16)

    def mix_bwd(r, fl):
        dga, dgb, dpa, dpb = bwd_body(fn_mix, 4)(r, fl)[0]
        return (jnp.concatenate([dga, dgb], axis=1), dpa, dpb), ()

    dproj, dpa, dpb = rowwise_call(
        mix_bwd, gate_rows + [pa, pb], [], [(dproj, 2 * D, P_GA), (D, BF16), (D, BF16)], [],
        tm=FUSED_TM, name="out_dgrad_mix_bwd", mm=(do, w_out_f, "nt", 4, None))
    gw_gm = matmul(ya, dpa, "tn", BF16, name="mm_branch_gm_wgrad", tk=WGRAD_TK)
    dyb = matmul(dpb, w_ssm_f, "nt", BF16, name="mm_branch_ssm_dgrad", tm=K1_TM)
    gw_ssm = matmul(yb, dpb, "tn", BF16, name="mm_branch_ssm_wgrad", tk=WGRAD_TK)

    def sgu_bwd(r, fl):
        (du, dv), acc = bwd_body(fn_sgu, 2)(r, fl)
        return (jnp.concatenate([du, dv], axis=1),), acc

    dproj, dgnw, dws, dbs = rowwise_call(
        sgu_bwd, sgu_rows, [gm_norm_w, ws, bs3], [(dproj, 2 * D, P_U)],
        [(1, D), (GM_GROUPS, Q, Q), (GM_GROUPS, Q, 1)], tm=FUSED_TM, name="branch_gm_dgrad_sgu_bwd",
        mm=(dpa, w_gm_f, "nt", 2, None))
    early_grads = [
        (gw_gm, _rows(D // N_DEV), (N_DEV, D // N_DEV, D), _slot),
        (gw_ssm, _rows(SSM_INNER // N_DEV), (N_DEV, SSM_INNER // N_DEV, D), _slot),
        (gw_out, _rows(D // N_DEV), (N_DEV, D // N_DEV, D), _slot),
        (gw_ff1, _cols(D_FF // N_DEV), (N_DEV, D, D_FF // N_DEV), _slot),
        (gw_ff2, _rows(D_FF // N_DEV), (N_DEV, D_FF // N_DEV, D), _slot),
        (_pack_rows([dgnw, dws, dbs], EARLY_ROWS), _whole, (N_DEV, sum(EARLY_ROWS), 128), _slot)]
    (dxs, dbm, dcm, dproj, ddt8, ddtb, dalog, ddsk, dsnw,
     r_gm, r_ssm, r_out, r_ff1, r_ff2, early_all) = ssd_bwd(
        xact, proj, dtg, dtb4, alog4, dsk4, ssm_norm_w, states, dyb, dproj, early_grads)
    dconv_w, dconv_b = [], []
    for nm, dact_part, col0 in (("xs", dxs, 0), ("b", dbm, SSM_INNER), ("c", dcm, SSM_INNER + SSM_GROUPS * 128)):
        dproj, dcw, dcb = conv_bwd(proj, dact_part, col0, conv_w_full, conv_b, dproj, name="conv_bwd_" + nm)
        dconv_w.append(dcw)
        dconv_b.append(dcb)
    dconv_w = jnp.concatenate(dconv_w, axis=1)
    dconv_b = jnp.concatenate(dconv_b, axis=1)
    (dproj,) = rowwise_call(
        lambda r, fl: ((functools.reduce(jnp.add, r),), ()),
        [(ddt8, 128, 128 * g) for g in range(SSM_GROUPS)], [], [(dproj, 128, PROJ_W)], [],
        tm=1024, name="ddt_into_dproj")
    gw_all = matmul(h, dproj, "tn", BF16, name="mm_in_wgrad", tn=1152, tk=WGRAD_TK)
    mid_pack = _pack_rows([dconv_w, dconv_b, jnp.sum(ddtb, axis=0), jnp.sum(dalog, axis=0), ddsk, dsnw, dfnw,
                           jnp.concatenate([dgt1, dsh2, dsc2, dgt2], axis=0)], MID_ROWS)
    dh, r_in, mid_all = matmul(
        dproj, w_all, "nt", BF16, name="mm_in_dgrad", tm=2048, tk=1152,
        comm=[(_windows_of_w_all(gw_all), _slot, (N_DEV, D, WIN_W), _slot),
              (mid_pack, _whole, (N_DEV, sum(MID_ROWS), 128), _slot)])
    grad_x, dsc1, dsh1 = rowwise_call(grad_x_body, [x2, dh, dxa], [sc1, sh1], [(D, F32)],
                                      [(1, D), (1, D)], tm=256, name="modulate1_bwd")

    g_w_in = _w_in_from_window(sum_devices(r_in, tr=256, name="sum_w_in_grads"), me).reshape(1, D, n_in)

    late_all = gather_blocks(_pack_rows([dsh1, dsc1, loss_v], LATE_ROWS), name="gather_dmod1_loss")
    s_early = _unpack_rows(sum_devices(early_all, tr=early_all.shape[1], name="sum_small_early"), EARLY_ROWS)
    s_mid = _unpack_rows(sum_devices(mid_all, tr=mid_all.shape[1], name="sum_small_mid"), MID_ROWS)
    s_late = _unpack_rows(sum_devices(late_all, tr=late_all.shape[1], name="sum_small_late"), LATE_ROWS)
    g_gm_norm_w = s_early[0][:D].reshape(1, D)
    g_gm_ws = s_early[1].reshape(GM_GROUPS * Q, Q)
    g_gm_bs = s_early[2][:GM_GROUPS * Q].reshape(GM_GROUPS, Q)
    g_conv_w_full = s_mid[0].reshape(CONV_K, CONV_DIM)
    g_conv_w = lax.dynamic_slice(g_conv_w_full, (0, me * n_cv), (CONV_K, n_cv))
    g_conv_b = s_mid[1].reshape(1, CONV_DIM)
    g_dt_bias = s_mid[2][:32].reshape(1, 32)
    g_a_log = s_mid[3][:32].reshape(1, 32)
    g_d_skip = s_mid[4][:32].reshape(1, 32)
    g_ssm_norm_w = s_mid[5].reshape(1, SSM_INNER)
    g_final_norm_w = s_mid[6][:D].reshape(1, D)
    g_b_mod = jnp.concatenate([s_late[0][:D], s_late[1][:D], s_mid[7]]).reshape(1, N_MOD * D)

    dmod_all = jnp.concatenate(
        [late_all.reshape(N_DEV, -1)[:, :2 * D],
         mid_all[:, sum(MID_ROWS[:7]):].reshape(N_DEV, 4 * D)], axis=1)
    dmod_mine = _pad_rows(lax.dynamic_slice(dmod_all, (0, me * n_mod), (N_DEV, n_mod)), 128)

    def wmod_grad_fn(cp, dm):
        ca = cp * jax.nn.sigmoid(cp)
        return (lax.dot_general(ca, dm, (((0,), (0,)), ((), ())), precision=HIGHEST,
                                preferred_element_type=F32),)

    (g_w_mod,) = whole_call(wmod_grad_fn, [c_pad, dmod_mine], [((D, n_mod), F32)], name="w_mod_grad")

    upd = {}

    def step(name, w, g, m, v, parts=False):
        upd[name] = adamw(w, g if parts else g.reshape(w.shape), m, v, name="adamw_" + name)

    step("w_mod", w_mod, g_w_mod, m_w_mod, v_w_mod)
    step("b_mod", b_mod, g_b_mod, m_b_mod, v_b_mod)
    col_major = lambda a: jnp.transpose(a, (2, 0, 1))
    upd["w_in"] = tuple(jnp.transpose(o, (1, 2, 0)) for o in adamw(
        col_major(w_in), col_major(g_w_in), col_major(m_w_in), col_major(v_w_in), name="adamw_w_in"))
    step("gm_norm_w", gm_norm_w, g_gm_norm_w, m_gm_norm_w, v_gm_norm_w)
    step("gm_ws", gm_ws, g_gm_ws, m_gm_ws, v_gm_ws)
    step("gm_bs", gm_bs, g_gm_bs, m_gm_bs, v_gm_bs)
    step("conv_w", conv_w, g_conv_w, m_conv_w, v_conv_w)
    step("conv_b", conv_b, g_conv_b, m_conv_b, v_conv_b)
    step("dt_bias", dt_bias, g_dt_bias, m_dt_bias, v_dt_bias)
    step("a_log", a_log, g_a_log, m_a_log, v_a_log)
    step("d_skip", d_skip, g_d_skip, m_d_skip, v_d_skip)
    step("ssm_norm_w", ssm_norm_w, g_ssm_norm_w, m_ssm_norm_w, v_ssm_norm_w)
    step("w_branch_gm", w_branch_gm, r_gm, m_w_branch_gm, v_w_branch_gm, parts=True)
    step("w_branch_ssm", w_branch_ssm, r_ssm, m_w_branch_ssm, v_w_branch_ssm, parts=True)
    step("w_out", w_out, r_out, m_w_out, v_w_out, parts=True)
    step("w_ff1", w_ff1, r_ff1, m_w_ff1, v_w_ff1, parts=True)
    step("w_ff2", w_ff2, r_ff2, m_w_ff2, v_w_ff2, parts=True)
    step("final_norm_w", final_norm_w.reshape(1, D), g_final_norm_w, m_final_norm_w.reshape(1, D),
         v_final_norm_w.reshape(1, D))
    upd["final_norm_w"] = tuple(a.reshape(D) for a in upd["final_norm_w"])

    loss = s_late[2][0]
    order = ["w_mod", "b_mod", "w_in", "gm_norm_w", "gm_ws", "gm_bs", "conv_w", "conv_b", "dt_bias", "a_log",
             "d_skip", "ssm_norm_w", "w_branch_gm", "w_branch_ssm", "w_out", "w_ff1", "w_ff2", "final_norm_w"]
    return (loss, grad_x.reshape(1, T, D),
            *[upd[n][0] for n in order], *[upd[n][1] for n in order],
            *[upd[n][2] for n in order], *[upd[n][3] for n in order])
```

```python
import functools

import jax
import jax.numpy as jnp
from jax import lax
from jax.experimental import pallas as pl
from jax.experimental.pallas import tpu as pltpu

F32 = jnp.float32
BF16 = jnp.bfloat16
MESH = pl.DeviceIdType.MESH
HIGHEST = lax.Precision.HIGHEST

N_DEV = 8
D = 1024
Q = 128
GM_GROUPS = 8
SSM_INNER = 2048
SSM_GROUPS = 8
SSM_HPG = 4
SSM_P = 64
SSM_GW = SSM_HPG * SSM_P
CONV_DIM = 4096
CONV_K = 4
D_FF = 4096
N_MOD = 6
EPS = 1e-6
IN_WIDTH = 10272
OFF_DT = 8192
OFF_GA = 8224
PROJ_W = 10240
ALL_W = 10368
P_U, P_V, P_Z, P_XBC, P_GA, P_GB = 0, 1024, 2048, 4096, 8192, 9216

ADAM_LR = 0.001
ADAM_B1 = 0.9
ADAM_B2 = 0.999
ADAM_EPS = 1e-08
ADAM_WD = 0.01
ADAM_STEP = 10

VMEM_LIMIT_BYTES = 48 * 1024 * 1024
K1_TM = 2048
FUSED_TM = 512
WGRAD_TK = 2048
EARLY_ROWS = (8, 1024, 8)
MID_ROWS = (128, 32, 8, 8, 8, 16, 8, 32)
LATE_ROWS = (8, 8, 8)


def _pack_rows(arrs, rows):
    def rows128(a, r):
        a = a.reshape(-1)
        return jnp.pad(a, (0, r * 128 - a.shape[0])).reshape(r, 128)
    return jnp.concatenate([rows128(a, r) for a, r in zip(arrs, rows)], axis=0)


def _unpack_rows(s, rows):
    out, o = [], 0
    for r in rows:
        out.append(s[o:o + r].reshape(-1))
        o += r
    return out


def _params(sem=None):
    return pltpu.CompilerParams(dimension_semantics=sem, vmem_limit_bytes=VMEM_LIMIT_BYTES)


def _dg(a, b, ca, cb):
    return lax.dot_general(a.astype(BF16), b.astype(BF16), (((ca,), (cb,)), ((), ())),
                           preferred_element_type=F32)


@jax.custom_vjp
def dot_nn(a, b):
    return _dg(a, b, 1, 0)


@jax.custom_vjp
def dot_nt(a, b):
    return _dg(a, b, 1, 1)


@jax.custom_vjp
def dot_tn(a, b):
    return _dg(a, b, 0, 0)


def _like(ct, primal):
    return ct.astype(primal.dtype)


dot_nn.defvjp(lambda a, b: (dot_nn(a, b), (a, b)),
              lambda r, g: (_like(dot_nt(g, r[1]), r[0]), _like(dot_tn(r[0], g), r[1])))
dot_nt.defvjp(lambda a, b: (dot_nt(a, b), (a, b)),
              lambda r, g: (_like(dot_nn(g, r[1]), r[0]), _like(dot_tn(g, r[0]), r[1])))
dot_tn.defvjp(lambda a, b: (dot_tn(a, b), (a, b)),
              lambda r, g: (_like(dot_nt(r[1], g), r[0]), _like(dot_nn(r[0], g), r[1])))


def _rms(x):
    return x * lax.rsqrt(jnp.mean(x * x, axis=-1, keepdims=True) + EPS)


def _softplus(x):
    return jnp.maximum(x, 0.0) + jnp.log1p(jnp.exp(-jnp.abs(x)))


def _rows(n):
    return lambda ref, j: ref.at[pl.ds(pl.multiple_of(j * n, n), n)]


def _cols(n):
    return lambda ref, j: ref.at[:, pl.ds(pl.multiple_of(j * n, n), n)]


def _slot(ref, j):
    return ref.at[j]


def _whole(ref, j):
    return ref


def exchange(items, *, name):
    n = len(items)

    def body(*refs):
        exchange_in_body(items, refs[:n], refs[n:2 * n], refs[2 * n:], True, True)

    return pl.pallas_call(
        body, name=name,
        out_shape=exchange_out_shapes(items),
        in_specs=[pl.BlockSpec(memory_space=pl.ANY)] * n,
        out_specs=[pl.BlockSpec(memory_space=pl.ANY)] * n,
        scratch_shapes=exchange_semaphores(items),
    )(*[it[0] for it in items])


def exchange_out_shapes(items):
    return [jax.ShapeDtypeStruct(tuple(shape), src.dtype) for (src, _, shape, _) in items]


def exchange_semaphores(items):
    n = len(items)
    return [pltpu.SemaphoreType.DMA((n, N_DEV - 1)), pltpu.SemaphoreType.DMA((n, N_DEV - 1)),
            pltpu.SemaphoreType.DMA((n,))]


def _exchange_copies(items, src_refs, out_refs, sems):
    send_sems, recv_sems, local_sems = sems
    x = lax.axis_index("x")
    y = lax.axis_index("y")
    c = lax.axis_index("c")
    me = 4 * x + 2 * y + c
    local = [pltpu.make_async_copy(src_win(src_refs[i], me), dst_win(out_refs[i], me), local_sems.at[i])
             for i, (_, src_win, _, dst_win) in enumerate(items)]
    remote = []
    for i, (_, src_win, _, dst_win) in enumerate(items):
        for k in range(1, N_DEV):
            px = lax.rem(x + ((k >> 2) & 1), 2)
            py = lax.rem(y + ((k >> 1) & 1), 2)
            pc = lax.rem(c + (k & 1), 2)
            peer = 4 * px + 2 * py + pc
            remote.append(pltpu.make_async_remote_copy(
                src_ref=src_win(src_refs[i], peer), dst_ref=dst_win(out_refs[i], me),
                send_sem=send_sems.at[i, k - 1], recv_sem=recv_sems.at[i, k - 1],
                device_id=(px, py, pc), device_id_type=MESH))
    return local, remote


def _when(cond, fn):
    if cond is True:
        fn()
    else:
        pl.when(cond)(fn)


def exchange_start(items, src_refs, out_refs, sems, cond):
    def start():
        local, remote = _exchange_copies(items, src_refs, out_refs, sems)
        for cp in local + remote:
            cp.start()
    _when(cond, start)


def exchange_finish(items, src_refs, out_refs, sems, cond):
    def finish():
        local, remote = _exchange_copies(items, src_refs, out_refs, sems)
        for cp in remote:
            cp.wait_send()
        for cp in remote:
            cp.wait_recv()
        for cp in local:
            cp.wait()
    _when(cond, finish)


def exchange_in_body(items, src_refs, out_refs, sems, first, last):
    exchange_start(items, src_refs, out_refs, sems, first)
    exchange_finish(items, src_refs, out_refs, sems, last)


def gather_blocks(src, *, name):
    return exchange([(src, _whole, (N_DEV,) + src.shape, _slot)], name=name)[0]


def _two_level_gather(src_ref, out_ref, send_sems, recv_sems, local_sem):
    x = lax.axis_index("x")
    y = lax.axis_index("y")
    c = lax.axis_index("c")
    me, sibling = (x, y, c), (x, y, 1 - c)
    chips = [(1 - x, y), (x, 1 - y), (1 - x, 1 - y)]

    def slot(px, py, pc):
        return out_ref.at[4 * px + 2 * py + pc]

    def copy(k, block, to, src=None):
        return pltpu.make_async_remote_copy(
            src_ref=slot(*block) if src is None else src, dst_ref=slot(*block),
            send_sem=send_sems.at[k], recv_sem=recv_sems.at[k], device_id=to, device_id_type=MESH)

    def own_copies():
        return ([copy(0, me, sibling, src=src_ref)]
                + [copy(1 + j, me, (*chip, c), src=src_ref) for j, chip in enumerate(chips)])

    def start():
        pltpu.make_async_copy(src_ref, slot(*me), local_sem).start()
        for cp in own_copies():
            cp.start()

    def finish():
        passed = [copy(4 + j, (*chip, c), sibling) for j, chip in enumerate(chips)]
        for j, chip in enumerate(chips):
            copy(1 + j, (*chip, c), me).wait_recv()
            passed[j].start()
        copy(0, sibling, me).wait_recv()
        for j, chip in enumerate(chips):
            copy(4 + j, (*chip, 1 - c), me).wait_recv()
        for cp in own_copies() + passed:
            cp.wait_send()
        pltpu.make_async_copy(src_ref, slot(*me), local_sem).wait()

    return start, finish


_TWO_LEVEL_SEMS = [pltpu.SemaphoreType.DMA((N_DEV - 1,)), pltpu.SemaphoreType.DMA((N_DEV - 1,)),
                   pltpu.SemaphoreType.DMA(())]


def modulate_with_gather(x2, sc, sh, src, *, tm, name):
    T, width = x2.shape
    n = T // tm

    def body(x_ref, sc_ref, sh_ref, src_ref, h_ref, out_ref, *sems):
        start, finish = _two_level_gather(src_ref, out_ref, *sems)
        pl.when(pl.program_id(0) == 0)(start)
        (h,) = fn_modulate(x_ref[...], sc_ref[...], sh_ref[...])
        h_ref[...] = h.astype(h_ref.dtype)
        pl.when(pl.program_id(0) == n - 1)(finish)

    row = pl.BlockSpec((tm, width), lambda i: (i, 0))
    full = pl.BlockSpec((1, width), lambda i: (0, 0))
    any_spec = pl.BlockSpec(memory_space=pl.ANY)
    return pl.pallas_call(
        body, name=name, grid=(n,),
        out_shape=[jax.ShapeDtypeStruct((T, width), BF16),
                   jax.ShapeDtypeStruct((N_DEV,) + src.shape, src.dtype)],
        in_specs=[row, full, full, any_spec], out_specs=[row, any_spec],
        scratch_shapes=_TWO_LEVEL_SEMS,
        compiler_params=_params(("arbitrary",)),
    )(x2, sc, sh, src)


def sum_devices(g, *, tr, name):
    _, R, C = g.shape

    def body(g_ref, o_ref):
        acc = g_ref[0].astype(F32)
        for j in range(1, N_DEV):
            acc = acc + g_ref[j].astype(F32)
        o_ref[...] = acc

    return pl.pallas_call(
        body, name=name, grid=(R // tr,),
        out_shape=jax.ShapeDtypeStruct((R, C), F32),
        in_specs=[pl.BlockSpec((N_DEV, tr, C), lambda i: (0, i, 0))],
        out_specs=pl.BlockSpec((tr, C), lambda i: (i, 0)),
        compiler_params=_params(("parallel",)),
    )(g)


def matmul(a, b, mode, out_dtype, *, name, tm=1024, tn=1024, tk=1024, n=None, comm=None,
           a_pro=None, epi=None, epi_ins=()):
    if mode == "nn":
        (M, K), (K2, N) = a.shape, b.shape
    elif mode == "nt":
        (M, K), (N, K2) = a.shape, b.shape
    else:
        (K, M), (K2, N) = a.shape, b.shape
    assert K == K2
    N = N if n is None else n
    tm, tn, tk = min(tm, M), min(tn, N), min(tk, K)
    assert M % tm == 0 and N % tn == 0 and K % tk == 0, (name, M, N, K, tm, tn, tk)
    nk = K // tk
    if mode == "tn":
        a_spec = pl.BlockSpec((tk, tm), lambda i, j, k: (k, i))
    else:
        a_spec = pl.BlockSpec((tm, tk), lambda i, j, k: (i, k))
    if mode == "nt":
        b_spec = pl.BlockSpec((tn, tk), lambda i, j, k: (j, k))
    else:
        b_spec = pl.BlockSpec((tk, tn), lambda i, j, k: (k, j))
    dims = {"nn": (1, 0), "nt": (1, 1), "tn": (0, 0)}[mode]
    items = list(comm) if comm else []
    nx = len(items)
    ne = len(epi_ins)
    gm, gn = M // tm, N // tn
    any_spec = pl.BlockSpec(memory_space=pl.ANY)
    o_spec = pl.BlockSpec((tm, tn), lambda i, j, k: (i, j))

    def body(*refs):
        a_ref, b_ref, e_refs = refs[0], refs[1], refs[2:2 + ne]
        refs = refs[2 + ne:]
        src_refs, o_ref, out_refs = refs[:nx], refs[nx], refs[1 + nx:1 + 2 * nx]
        acc_ref, sems = refs[1 + 2 * nx], refs[2 + 2 * nx:]
        i, j, k = pl.program_id(0), pl.program_id(1), pl.program_id(2)
        if items:
            exchange_start(items, src_refs, out_refs, sems, (i == 0) & (j == 0) & (k == 0))
        a_tile = a_ref[...] if a_pro is None else a_pro(a_ref[...])
        part = lax.dot_general(a_tile, b_ref[...], (((dims[0],), (dims[1],)), ((), ())),
                               preferred_element_type=F32)

        def finish(acc):
            if epi is not None:
                acc = epi(acc, *[e[...] for e in e_refs])
            o_ref[...] = acc.astype(o_ref.dtype)

        if nk == 1:
            finish(part)
        else:
            @pl.when(k == 0)
            def _():
                acc_ref[...] = part

            @pl.when((k > 0) & (k < nk - 1))
            def _():
                acc_ref[...] += part

            @pl.when(k == nk - 1)
            def _():
                finish(acc_ref[...] + part)

        if items:
            exchange_finish(items, src_refs, out_refs, sems, (i == gm - 1) & (j == gn - 1) & (k == nk - 1))

    res = pl.pallas_call(
        body, name=name, grid=(gm, gn, nk),
        out_shape=[jax.ShapeDtypeStruct((M, N), out_dtype)] + exchange_out_shapes(items),
        in_specs=[a_spec, b_spec] + [o_spec] * ne + [any_spec] * nx,
        out_specs=[o_spec] + [any_spec] * nx,
        scratch_shapes=[pltpu.VMEM((tm, tn) if nk > 1 else (8, 128), F32)]
        + (exchange_semaphores(items) if items else []),
        compiler_params=_params(("arbitrary",) * 3 if items else ("parallel", "parallel", "arbitrary")),
    )(a, b, *epi_ins, *[it[0] for it in items])
    return res if items else res[0]


def rowwise_call(body_fn, rows, fulls, row_outs, acc_outs, *, tm, name, mm=None, tk=1024):
    rows = [r if isinstance(r, tuple) else (r, r.shape[1], 0) for r in rows]
    T = rows[0][0].shape[0]
    tm = min(tm, T)
    assert T % tm == 0
    n_r, n_f, n_ro = len(rows), len(fulls), len(row_outs)
    into = [(k, ro) for k, ro in enumerate(row_outs) if len(ro) == 3]
    n_b = len(into)
    n_mm, nk = 0, 1
    if mm is not None:
        a, b, mode, pos, a_pro = mm
        n_mm = 2
        K = a.shape[1]
        N = b.shape[1] if mode == "nn" else b.shape[0]
        tk = min(tk, K)
        assert K % tk == 0 and a.shape[0] == T
        nk = K // tk
        b_contract = 0 if mode == "nn" else 1

    def row_body(refs, product):
        r_refs = refs[:n_r]
        f_refs = refs[n_r:n_r + n_f]
        refs = refs[n_r + n_f + n_b:]
        ro_refs = refs[:n_ro]
        ao_refs = refs[n_ro:n_ro + len(acc_outs)]
        r_vals = [r[...].astype(F32) for r in r_refs]
        if product is not None:
            r_vals.insert(pos, product)
        f_vals = [f[...].astype(F32) for f in f_refs]
        ro, ao = body_fn(r_vals, f_vals)
        for ref, v in zip(ro_refs, ro):
            ref[...] = v.astype(ref.dtype)
        if ao_refs:
            @pl.when(pl.program_id(0) == 0)
            def _():
                for ref in ao_refs:
                    ref[...] = jnp.zeros(ref.shape, F32)
            for ref, v in zip(ao_refs, ao):
                ref[...] += v.reshape(ref.shape)

    def body(*refs):
        if mm is None:
            return row_body(refs, None)
        a_ref, b_ref, rest, acc_ref = refs[0], refs[1], refs[2:-1], refs[-1]
        k = pl.program_id(1)
        a_tile = a_ref[...] if a_pro is None else a_pro(a_ref[...])
        part = lax.dot_general(a_tile, b_ref[...], (((1,), (b_contract,)), ((), ())),
                               preferred_element_type=F32)
        if nk == 1:
            return row_body(rest, part)

        @pl.when(k == 0)
        def _():
            acc_ref[...] = part

        @pl.when((k > 0) & (k < nk - 1))
        def _():
            acc_ref[...] += part

        @pl.when(k == nk - 1)
        def _():
            row_body(rest, acc_ref[...] + part)

    def full_spec(shape):
        nd = len(shape)
        return pl.BlockSpec(tuple(shape), lambda i, *_: (0,) * nd)

    def row_spec(w, off):
        return pl.BlockSpec((tm, w), functools.partial(lambda i, *_, o: (i, o), o=off // w))

    in_specs = []
    if mm is not None:
        in_specs.append(pl.BlockSpec((tm, tk), lambda i, k: (i, k)))
        in_specs.append(pl.BlockSpec((tk, N), lambda i, k: (k, 0)) if mode == "nn" else
                        pl.BlockSpec((N, tk), lambda i, k: (0, k)))
    in_specs += [row_spec(w, off) for (_, w, off) in rows]
    in_specs += [full_spec(f.shape) for f in fulls]
    in_specs += [pl.BlockSpec(memory_space=pl.ANY)] * n_b
    out_specs, out_shape = [], []
    for ro in row_outs:
        if len(ro) == 3:
            buf, w, off = ro
            out_specs.append(row_spec(w, off))
            out_shape.append(jax.ShapeDtypeStruct(buf.shape, buf.dtype))
        else:
            w, dt = ro
            out_specs.append(row_spec(w, 0))
            out_shape.append(jax.ShapeDtypeStruct((T, w), dt))
    out_specs += [full_spec(s) for s in acc_outs]
    out_shape += [jax.ShapeDtypeStruct(tuple(s), F32) for s in acc_outs]
    aliases = {n_mm + n_r + n_f + b: k for b, (k, _) in enumerate(into)}
    return pl.pallas_call(
        body, name=name, grid=(T // tm,) if mm is None else (T // tm, nk),
        out_shape=out_shape, in_specs=in_specs, out_specs=out_specs,
        scratch_shapes=[] if mm is None else [pltpu.VMEM((tm, N) if nk > 1 else (8, 128), F32)],
        input_output_aliases=aliases,
        compiler_params=_params(("arbitrary",) if mm is None else ("arbitrary", "arbitrary")),
    )(*([] if mm is None else [a, b]), *[r[0] for r in rows], *fulls, *[ro[0] for _, ro in into])


def fwd_body(fn):
    return lambda r, f: (fn(*r, *f), ())


def bwd_body(fn, n_rows):
    def body(r, f):
        ins, cots = r[:n_rows], r[n_rows:]
        _, vjp = jax.vjp(fn, *ins, *f)
        g = vjp(tuple(cots))
        return g[:n_rows], g[n_rows:]
    return body


def whole_call(fn, ins, outs, *, name):
    n_in = len(ins)

    def body(*refs):
        res = fn(*[r[...] for r in refs[:n_in]])
        for ref, v in zip(refs[n_in:], res):
            ref[...] = v.astype(ref.dtype)

    return pl.pallas_call(
        body, name=name,
        out_shape=[jax.ShapeDtypeStruct(tuple(s), dt) for (s, dt) in outs],
        compiler_params=_params(),
    )(*ins)


def fn_modulate(x, sc, sh):
    return (_rms(x) * (1.0 + sc) + sh,)


def fn_sgu(u, v, nw, ws, bs):
    ug = jax.nn.gelu(u)
    vn = _rms(jax.nn.gelu(v)) * nw
    ri = lax.broadcasted_iota(jnp.int32, (Q, Q), 0)
    ci = lax.broadcasted_iota(jnp.int32, (Q, Q), 1)
    causal = ri >= ci
    chunks = []
    for n in range(u.shape[0] // Q):
        vc = vn[n * Q:(n + 1) * Q]
        cols = [dot_nn(jnp.where(causal, ws[g], 0.0), vc[:, g * Q:(g + 1) * Q]) + bs[g]
                for g in range(GM_GROUPS)]
        chunks.append(jnp.concatenate(cols, axis=1))
    sv = chunks[0] if len(chunks) == 1 else jnp.concatenate(chunks, axis=0)
    return (ug * sv,)


def fn_mix(ga, gb, pa, pb):
    return (jax.nn.sigmoid(ga) * pa + jax.nn.sigmoid(gb) * pb,)


def fn_res_modulate(x, o, g1, sc2, sh2):
    x1 = x + g1 * o
    return x1, _rms(x1) * (1.0 + sc2) + sh2


def relu2_tile(f):
    return jnp.square(jnp.maximum(f.astype(F32), 0.0)).astype(BF16)


def relu2_grad_tile(dact, f):
    return dact * (2.0 * jnp.maximum(f.astype(F32), 0.0))


def final_body(r, f):
    x1, gf, tgt = r
    g2, fnw = f

    def loss_fn(x1, gf, g2, fnw):
        y = _rms(x1 + g2 * gf) * fnw
        row = 0.5 * jnp.mean(jnp.square(y - tgt), axis=-1, keepdims=True)
        return jnp.sum(row, axis=0, keepdims=True)

    l, vjp = jax.vjp(loss_fn, x1, gf, g2, fnw)
    dx1, dgf, dg2, dfnw = vjp(jnp.ones((1, 1), F32))
    return (dx1, dgf), (jnp.broadcast_to(l, (1, 128)), dg2, dfnw)


def grad_x_body(r, f):
    x, dh, dxa = r
    _, vjp = jax.vjp(fn_modulate, x, *f)
    dx, dsc, dsh = vjp((dh,))
    return (dx + dxa,), (dsc, dsh)


CONV_CW = 128
CONV_PAD = 8
CONV_ROWS = 128


def _conv_pre(xp, w_ref, b_ref, r0, R):
    acc = b_ref[...] + w_ref[0:1, :] * xp[r0 + CONV_PAD - 3:r0 + CONV_PAD - 3 + R, :]
    for k in range(1, CONV_K):
        s = r0 + CONV_PAD - 3 + k
        acc = acc + w_ref[k:k + 1, :] * xp[s:s + R, :]
    return acc


def conv_fwd(proj, conv_w, conv_b):
    T = proj.shape[0]
    R = min(CONV_ROWS, T)

    def body(x_ref, w_ref, b_ref, o_ref, xp):
        xp[0:CONV_PAD, :] = jnp.zeros((CONV_PAD, CONV_CW), F32)
        xp[CONV_PAD:CONV_PAD + T, :] = x_ref[...].astype(F32)
        for r0 in range(0, T, R):
            pre = _conv_pre(xp, w_ref, b_ref, r0, R)
            o_ref[r0:r0 + R, :] = (pre * jax.nn.sigmoid(pre)).astype(o_ref.dtype)

    return pl.pallas_call(
        body, name="conv_fwd", grid=(CONV_DIM // CONV_CW,),
        out_shape=jax.ShapeDtypeStruct((T, CONV_DIM), BF16),
        in_specs=[pl.BlockSpec((T, CONV_CW), lambda j: (0, P_XBC // CONV_CW + j)),
                  pl.BlockSpec((CONV_K, CONV_CW), lambda j: (0, j)),
                  pl.BlockSpec((1, CONV_CW), lambda j: (0, j))],
        out_specs=pl.BlockSpec((T, CONV_CW), lambda j: (0, j)),
        scratch_shapes=[pltpu.VMEM((T + CONV_PAD, CONV_CW), F32)],
        compiler_params=_params(("parallel",)),
    )(proj, conv_w, conv_b)


def conv_bwd(proj, dact, col0, conv_w, conv_b, dproj, *, name):
    T = proj.shape[0]
    R = min(CONV_ROWS, T)
    nb = dact.shape[1] // CONV_CW
    c0 = col0 // CONV_CW
    x0 = (P_XBC + col0) // CONV_CW

    def body(x_ref, d_ref, w_ref, b_ref, _, dx_ref, dw_ref, db_ref, xp, dp):
        xp[0:CONV_PAD, :] = jnp.zeros((CONV_PAD, CONV_CW), F32)
        xp[CONV_PAD:CONV_PAD + T, :] = x_ref[...].astype(F32)
        dp[T:T + CONV_PAD, :] = jnp.zeros((CONV_PAD, CONV_CW), F32)
        dws = [jnp.zeros((1, CONV_CW), F32) for _ in range(CONV_K)]
        db = jnp.zeros((1, CONV_CW), F32)
        for r0 in range(0, T, R):
            pre = _conv_pre(xp, w_ref, b_ref, r0, R)
            s = jax.nn.sigmoid(pre)
            dpre = d_ref[r0:r0 + R, :].astype(F32) * (s * (1.0 + pre * (1.0 - s)))
            dp[r0:r0 + R, :] = dpre
            db = db + jnp.sum(dpre, axis=0, keepdims=True)
            for k in range(CONV_K):
                st = r0 + CONV_PAD - 3 + k
                dws[k] = dws[k] + jnp.sum(dpre * xp[st:st + R, :], axis=0, keepdims=True)
        for r0 in range(0, T, R):
            acc = w_ref[0:1, :] * dp[r0 + 3:r0 + 3 + R, :]
            for k in range(1, CONV_K):
                acc = acc + w_ref[k:k + 1, :] * dp[r0 + 3 - k:r0 + 3 - k + R, :]
            dx_ref[r0:r0 + R, :] = acc.astype(dx_ref.dtype)
        for k in range(CONV_K):
            dw_ref[k:k + 1, :] = dws[k]
        db_ref[...] = db

    return pl.pallas_call(
        body, name=name, grid=(nb,),
        out_shape=[jax.ShapeDtypeStruct(dproj.shape, dproj.dtype),
                   jax.ShapeDtypeStruct((CONV_K, nb * CONV_CW), F32),
                   jax.ShapeDtypeStruct((1, nb * CONV_CW), F32)],
        in_specs=[pl.BlockSpec((T, CONV_CW), lambda j: (0, x0 + j)),
                  pl.BlockSpec((T, CONV_CW), lambda j: (0, j)),
                  pl.BlockSpec((CONV_K, CONV_CW), lambda j: (0, c0 + j)),
                  pl.BlockSpec((1, CONV_CW), lambda j: (0, c0 + j)),
                  pl.BlockSpec(memory_space=pl.ANY)],
        out_specs=[pl.BlockSpec((T, CONV_CW), lambda j: (0, x0 + j)),
                   pl.BlockSpec((CONV_K, CONV_CW), lambda j: (0, j)),
                   pl.BlockSpec((1, CONV_CW), lambda j: (0, j))],
        scratch_shapes=[pltpu.VMEM((T + CONV_PAD, CONV_CW), F32),
                        pltpu.VMEM((T + CONV_PAD, CONV_CW), F32)],
        input_output_aliases={4: 0},
        compiler_params=_params(("parallel",)),
    )(proj, dact, conv_w, conv_b, dproj)


def _split3(a):
    hi = a.astype(BF16)
    r = a - hi.astype(F32)
    mid = r.astype(BF16)
    return hi, mid, (r - mid.astype(F32)).astype(BF16)


def _dg3(a, m, ca, cm, a_first):
    dims = (((ca,), (cm,)), ((), ())) if a_first else (((cm,), (ca,)), ((), ()))
    out = None
    for p in _split3(a):
        t = lax.dot_general(p, m, dims, preferred_element_type=F32) if a_first else \
            lax.dot_general(m, p, dims, preferred_element_type=F32)
        out = t if out is None else out + t
    return out


@jax.custom_vjp
def exact_right(a, m):
    return _dg3(a, m, 1, 0, True)


@jax.custom_vjp
def exact_left(m, a):
    return _dg3(a, m, 0, 1, False)


def _expand_bwd(m, g):
    hi = g.astype(BF16)
    lo = (g - hi.astype(F32)).astype(BF16)
    out = lax.dot_general(jnp.concatenate([hi, lo], axis=1), jnp.concatenate([m, m], axis=1),
                          (((1,), (1,)), ((), ())), preferred_element_type=F32)
    return out, jnp.zeros_like(m)


exact_right.defvjp(lambda a, m: (exact_right(a, m), m), _expand_bwd)
exact_left.defvjp(lambda m, a: (exact_left(m, a), m),
                  lambda m, g: (jnp.zeros_like(m), _dg3(g, m, 0, 0, False)))


def ssd_step(lane0, per_block, state, x, z, dtr, Bm, Cm, dtb, alog, dsk, nw):
    def iota(shape, dim):
        return lax.broadcasted_iota(jnp.int32, shape, dim)

    def one_hot(mask):
        return mask.astype(F32).astype(BF16)

    causal = iota((Q, Q), 0) >= iota((Q, Q), 1)
    eye = iota((Q, Q), 0) == iota((Q, Q), 1)
    lane = iota((1, 128), 1)
    colh = lax.shift_right_logical(iota((1, SSM_GW), 1), 6)
    to_cols = one_hot(iota((128, SSM_GW), 0) == lane0 + colh)

    dt_all = _softplus(dtr + dtb)
    a_all = dt_all * (-jnp.exp(alog))
    cum_all = exact_left(one_hot(causal), a_all)
    both = exact_right(jnp.concatenate([dt_all, cum_all], axis=0), to_cols)
    dt_f, cum_f = both[:Q], both[Q:]
    last_f = jnp.sum(jnp.where(iota((Q, 1), 0) == Q - 1, cum_f, 0.0), axis=0, keepdims=True)
    dsk_f = jnp.zeros((1, SSM_GW), F32)
    for h in range(SSM_HPG):
        dsk_f = jnp.where(colh == h, dsk[h], dsk_f)

    xdt = x * dt_f
    cb = dot_nt(Cm, Bm)
    ms = []
    for h in range(SSM_HPG):
        ch = jnp.sum(jnp.where(lane == lane0 + h, cum_all, 0.0), axis=1, keepdims=True)
        ch_t = jnp.sum(jnp.where(eye, ch, 0.0), axis=0, keepdims=True)
        ms.append(cb * jnp.exp(jnp.where(causal, ch - ch_t, -1e30)))
    if per_block:
        first_half = lane < SSM_P
        blocks = []
        for b in range(SSM_HPG // 2):
            xb = xdt[:, b * 128:(b + 1) * 128]
            rhs = jnp.concatenate([jnp.where(first_half, xb, 0.0), jnp.where(first_half, 0.0, xb)], axis=0)
            blocks.append(dot_nn(jnp.concatenate(ms[2 * b:2 * b + 2], axis=1), rhs))
        y = jnp.concatenate(blocks, axis=1)
    else:
        rhs = jnp.concatenate([jnp.where(colh == h, xdt, 0.0) for h in range(SSM_HPG)], axis=0)
        y = dot_nn(jnp.concatenate(ms, axis=1), rhs)
    y = y + dot_nn(Cm, state) * jnp.exp(cum_f) + x * dsk_f
    new_state = state * jnp.exp(last_f) + dot_tn(Bm, xdt * jnp.exp(last_f - cum_f))
    gated = y * (z * jax.nn.sigmoid(z))
    return new_state, _rms(gated) * nw


SSD_GPS = 4
_XW = SSD_GPS * SSM_GW
_BW = SSD_GPS * 128


def _ssd_in_specs(rev, nc):
    def n_of(n):
        return nc - 1 - n if rev else n
    return [
        pl.BlockSpec((Q, _XW), lambda g, n: (n_of(n), g)),
        pl.BlockSpec((Q, _BW), lambda g, n: (n_of(n), SSM_INNER // _BW + g)),
        pl.BlockSpec((Q, _BW), lambda g, n: (n_of(n), (SSM_INNER + SSM_GROUPS * 128) // _BW + g)),
        pl.BlockSpec((Q, _XW), lambda g, n: (n_of(n), P_Z // _XW + g)),
        pl.BlockSpec((Q, 128), lambda g, n: (n_of(n), 0)),
        pl.BlockSpec((1, 128), lambda g, n: (0, 0)),
        pl.BlockSpec((1, 128), lambda g, n: (0, 0)),
        pl.BlockSpec((SSD_GPS, SSM_HPG, 1, 1), lambda g, n: (g, 0, 0, 0)),
        pl.BlockSpec((1, _XW), lambda g, n: (0, g)),
    ]


def _ssd_group_inputs(gi, x_ref, b_ref, c_ref, z_ref, dt_ref, dtb_ref, al_ref, dk_ref, nw_ref):
    xs = slice(gi * SSM_GW, (gi + 1) * SSM_GW)
    bs = slice(gi * 128, (gi + 1) * 128)
    return (x_ref[:, xs].astype(F32), z_ref[:, xs].astype(F32), dt_ref[...],
            b_ref[:, bs], c_ref[:, bs],
            dtb_ref[...], al_ref[...], dk_ref[gi], nw_ref[:, xs])


def ssd_fwd(xact, proj, dtg, dtb, alog, dsk, nw, comm):
    T = xact.shape[0]
    nc = T // Q
    nx = len(comm)
    ng = SSM_GROUPS // SSD_GPS
    any_spec = pl.BlockSpec(memory_space=pl.ANY)

    def body(*refs):
        in_refs, src_refs = refs[:9], refs[9:9 + nx]
        yb_ref, st_ref = refs[9 + nx:11 + nx]
        out_refs, state, sems = refs[11 + nx:11 + 2 * nx], refs[11 + 2 * nx], refs[12 + 2 * nx:]
        g, n = pl.program_id(0), pl.program_id(1)
        exchange_start(comm, src_refs, out_refs, sems, (g == 0) & (n == 0))

        @pl.when(n == 0)
        def _():
            state[...] = jnp.zeros(state.shape, F32)

        for gi in range(SSD_GPS):
            lane0 = SSM_HPG * (SSD_GPS * pl.program_id(0) + gi)
            s = state[gi]
            st_ref[gi, 0] = s
            new_s, yb = ssd_step(lane0, False, s, *_ssd_group_inputs(gi, *in_refs))
            state[gi] = new_s
            yb_ref[:, gi * SSM_GW:(gi + 1) * SSM_GW] = yb.astype(yb_ref.dtype)

        exchange_finish(comm, src_refs, out_refs, sems, (g == ng - 1) & (n == nc - 1))

    return pl.pallas_call(
        body, name="ssd_fwd", grid=(ng, nc),
        out_shape=[jax.ShapeDtypeStruct((T, SSM_INNER), BF16),
                   jax.ShapeDtypeStruct((SSM_GROUPS, nc, 128, SSM_GW), F32)] + exchange_out_shapes(comm),
        in_specs=_ssd_in_specs(False, nc) + [any_spec] * nx,
        out_specs=[pl.BlockSpec((Q, _XW), lambda g, n: (n, g)),
                   pl.BlockSpec((SSD_GPS, 1, 128, SSM_GW), lambda g, n: (g, n, 0, 0))] + [any_spec] * nx,
        scratch_shapes=[pltpu.VMEM((SSD_GPS, 128, SSM_GW), F32)] + exchange_semaphores(comm),
        compiler_params=_params(("arbitrary", "arbitrary")),
    )(xact, xact, xact, proj, dtg, dtb, alog, dsk, nw, *[it[0] for it in comm])


def ssd_bwd(xact, proj, dtg, dtb, alog, dsk, nw, states, dyb, dproj, comm):
    T = xact.shape[0]
    nc = T // Q

    nx = len(comm)
    ng = SSM_GROUPS // SSD_GPS
    any_spec = pl.BlockSpec(memory_space=pl.ANY)

    def body(*refs):
        in_refs, (st_ref, dy_ref, _) = refs[:9], refs[9:12]
        src_refs, refs = refs[12:12 + nx], refs[12 + nx:]
        dx_ref, db_ref, dc_ref, dz_ref, ddt_ref, ddtb_ref, dal_ref, ddk_ref, dnw_ref = refs[:9]
        out_refs, dstate, sems = refs[9:9 + nx], refs[9 + nx], refs[10 + nx:]
        exchange_start(comm, src_refs, out_refs, sems, (pl.program_id(0) == 0) & (pl.program_id(1) == 0))

        @pl.when(pl.program_id(1) == 0)
        def _():
            dstate[...] = jnp.zeros(dstate.shape, F32)
            ddtb_ref[...] = jnp.zeros(ddtb_ref.shape, F32)
            dal_ref[...] = jnp.zeros(dal_ref.shape, F32)
            ddk_ref[...] = jnp.zeros(ddk_ref.shape, F32)
            dnw_ref[...] = jnp.zeros(dnw_ref.shape, F32)

        for gi in range(SSD_GPS):
            xs = slice(gi * SSM_GW, (gi + 1) * SSM_GW)
            bs = slice(gi * 128, (gi + 1) * 128)
            lane0 = SSM_HPG * (SSD_GPS * pl.program_id(0) + gi)
            ins = (st_ref[gi, 0],) + _ssd_group_inputs(gi, *in_refs)
            _, vjp = jax.vjp(functools.partial(ssd_step, lane0, True), *ins)
            ds, dx, dz, ddt, dbm, dcm, ddtb, dal, ddk, dnw = vjp((dstate[gi], dy_ref[:, xs].astype(F32)))
            dstate[gi] = ds
            dx_ref[:, xs] = dx.astype(dx_ref.dtype)
            db_ref[:, bs] = dbm.astype(db_ref.dtype)
            dc_ref[:, bs] = dcm.astype(dc_ref.dtype)
            dz_ref[:, xs] = dz.astype(dz_ref.dtype)
            ddt_ref[:, bs] = ddt.astype(ddt_ref.dtype)
            ddtb_ref[gi] += ddtb
            dal_ref[gi] += dal
            ddk_ref[gi] += ddk
            dnw_ref[:, xs] += dnw

        exchange_finish(comm, src_refs, out_refs, sems,
                        (pl.program_id(0) == ng - 1) & (pl.program_id(1) == nc - 1))

    rev = lambda n: nc - 1 - n
    row_shape = jax.ShapeDtypeStruct((SSM_GROUPS, 1, 128), F32)
    row_spec = pl.BlockSpec((SSD_GPS, 1, 128), lambda g, n: (g, 0, 0))
    return pl.pallas_call(
        body, name="ssd_bwd", grid=(ng, nc),
        out_shape=[jax.ShapeDtypeStruct((T, SSM_INNER), BF16),
                   jax.ShapeDtypeStruct((T, SSM_GROUPS * 128), BF16),
                   jax.ShapeDtypeStruct((T, SSM_GROUPS * 128), BF16),
                   jax.ShapeDtypeStruct(dproj.shape, dproj.dtype),
                   jax.ShapeDtypeStruct((T, SSM_GROUPS * 128), BF16),
                   row_shape, row_shape,
                   jax.ShapeDtypeStruct((SSM_GROUPS, SSM_HPG, 1, 1), F32),
                   jax.ShapeDtypeStruct((1, SSM_INNER), F32)] + exchange_out_shapes(comm),
        in_specs=_ssd_in_specs(True, nc) + [
            pl.BlockSpec((SSD_GPS, 1, 128, SSM_GW), lambda g, n: (g, rev(n), 0, 0)),
            pl.BlockSpec((Q, _XW), lambda g, n: (rev(n), g)),
            any_spec] + [any_spec] * nx,
        out_specs=[pl.BlockSpec((Q, _XW), lambda g, n: (rev(n), g)),
                   pl.BlockSpec((Q, _BW), lambda g, n: (rev(n), g)),
                   pl.BlockSpec((Q, _BW), lambda g, n: (rev(n), g)),
                   pl.BlockSpec((Q, _XW), lambda g, n: (rev(n), P_Z // _XW + g)),
                   pl.BlockSpec((Q, _BW), lambda g, n: (rev(n), g)),
                   row_spec, row_spec,
                   pl.BlockSpec((SSD_GPS, SSM_HPG, 1, 1), lambda g, n: (g, 0, 0, 0)),
                   pl.BlockSpec((1, _XW), lambda g, n: (0, g))] + [any_spec] * nx,
        scratch_shapes=[pltpu.VMEM((SSD_GPS, 128, SSM_GW), F32)] + exchange_semaphores(comm),
        input_output_aliases={11: 3},
        compiler_params=_params(("arbitrary", "arbitrary")),
    )(xact, xact, xact, proj, dtg, dtb, alog, dsk, nw, states, dyb, dproj, *[it[0] for it in comm])


ADAMW_WHOLE_ELEMS = 256 * 1024


def adamw(w, g, m, v, *, name):
    shape = w.shape
    parts = g.shape != shape
    nd = len(shape)
    if nd == 3 and shape[1] == 1 and w.size > ADAMW_WHOLE_ELEMS:
        assert not parts and shape[0] % 4 == 0
        grid = (4,)
        spec = g_spec = pl.BlockSpec((shape[0] // 4, 1, shape[2]), lambda i: (i, 0, 0))
    else:
        if w.size <= ADAMW_WHOLE_ELEMS:
            grid, tr = (1,), shape[-2]
        else:
            assert all(s == 1 for s in shape[:-2]) and shape[-2] % 256 == 0
            grid, tr = (shape[-2] // 256,), 256
        blk = tuple(shape[:-2]) + (tr, shape[-1])
        spec = pl.BlockSpec(blk, lambda i: (0,) * (nd - 2) + (i, 0))
        g_spec = pl.BlockSpec((N_DEV,) + blk[1:], lambda i: (0,) * (nd - 2) + (i, 0)) if parts else spec

    def body(w_ref, g_ref, m_ref, v_ref, go_ref, d_ref, nm_ref, nv_ref):
        if parts:
            g = g_ref[0:1].astype(F32)
            for j in range(1, N_DEV):
                g = g + g_ref[j:j + 1].astype(F32)
        else:
            g = g_ref[...]
        nm = ADAM_B1 * m_ref[...] + (1.0 - ADAM_B1) * g
        nv = ADAM_B2 * v_ref[...] + (1.0 - ADAM_B2) * jnp.square(g)
        m_hat = nm / (1.0 - ADAM_B1 ** ADAM_STEP)
        v_hat = nv / (1.0 - ADAM_B2 ** ADAM_STEP)
        go_ref[...] = g
        d_ref[...] = -ADAM_LR * (m_hat / (jnp.sqrt(v_hat) + ADAM_EPS) + ADAM_WD * w_ref[...])
        nm_ref[...] = nm
        nv_ref[...] = nv

    shp = jax.ShapeDtypeStruct(shape, F32)
    return pl.pallas_call(
        body, name=name, grid=grid,
        out_shape=[shp] * 4, in_specs=[spec, g_spec, spec, spec], out_specs=[spec] * 4,
        compiler_params=_params(("parallel",)),
    )(w, g, m, v)


def _pad_rows(a, rows):
    return jnp.pad(a, ((0, rows - a.shape[0]), (0, 0)))


WIN_W = 1408
N_IN = IN_WIDTH // N_DEV
_A6 = OFF_DT - 6 * N_IN
_C6 = 7 * N_IN - OFF_GA


_WIN_OFFSETS = (0, 4, 8, 12, 16, 20, None, 124)


def _w_in_window(shard, me):
    rows = shard.shape[0]
    z = lambda n: jnp.zeros((rows, n), shard.dtype)

    def plain(off):
        return lambda s: jnp.pad(s, ((0, 0), (off, WIN_W - N_IN - off)))

    def split(s):
        return jnp.concatenate([z(24), s[:, :_A6], s[:, _A6 + 32:], z(4), s[:, _A6:_A6 + 32], z(96)], axis=1)

    return lax.switch(me, [split if off is None else plain(off) for off in _WIN_OFFSETS], shard)


def _w_in_from_window(window, me):
    def plain(off):
        return lambda w: w[:, off:off + N_IN]

    def split(w):
        return jnp.concatenate([w[:, 24:24 + _A6], w[:, 1280:1312], w[:, 24 + _A6:24 + _A6 + _C6]], axis=1)

    return lax.switch(me, [split if off is None else plain(off) for off in _WIN_OFFSETS], window)


def _w_all_from_windows(g):
    def merge_first(p, t):
        return jnp.concatenate([p[:, :128] + t, p[:, 128:]], axis=1)

    parts = [g[0][:, :1280]]
    for j in range(1, 6):
        parts.append(merge_first(g[j][:, :1280], g[j - 1][:, 1280:]))
    p6 = merge_first(g[6][:, :1280], g[5][:, 1280:])
    parts.append(jnp.concatenate([p6[:, :1152], p6[:, 1152:] + g[7][:, :128]], axis=1))
    parts.append(g[7][:, 128:])
    parts.append(g[6][:, 1280:])
    return jnp.concatenate(parts, axis=1)


def _windows_of_w_all(gw):
    wins = [gw[:, 1280 * j:1280 * j + WIN_W] for j in range(6)]
    wins.append(jnp.concatenate([gw[:, 7680:8960], gw[:, PROJ_W:]], axis=1))
    wins.append(gw[:, 8832:PROJ_W])
    return jnp.stack(wins)


def kernel(x, c, w_mod, b_mod, w_in, gm_norm_w, gm_ws, gm_bs, conv_w, conv_b, dt_bias, a_log, d_skip, ssm_norm_w, w_branch_gm, w_branch_ssm, w_out, w_ff1, w_ff2, final_norm_w, loss_target, m_w_mod, m_b_mod, m_w_in, m_gm_norm_w, m_gm_ws, m_gm_bs, m_conv_w, m_conv_b, m_dt_bias, m_a_log, m_d_skip, m_ssm_norm_w, m_w_branch_gm, m_w_branch_ssm, m_w_out, m_w_ff1, m_w_ff2, m_final_norm_w, v_w_mod, v_b_mod, v_w_in, v_gm_norm_w, v_gm_ws, v_gm_bs, v_conv_w, v_conv_b, v_dt_bias, v_a_log, v_d_skip, v_ssm_norm_w, v_w_branch_gm, v_w_branch_ssm, v_w_out, v_w_ff1, v_w_ff2, v_final_norm_w):
    T = x.shape[1]
    me = 4 * lax.axis_index("x") + 2 * lax.axis_index("y") + lax.axis_index("c")
    x2 = x[0]
    tgt = loss_target[0]
    n_in = IN_WIDTH // N_DEV
    n_mod = N_MOD * D // N_DEV
    n_cv = CONV_DIM // N_DEV

    c_all, conv_w_full = exchange(
        [(c.reshape(8, 128), _whole, (N_DEV, 8, 128), _slot),
         (conv_w[0], _whole, (N_DEV, CONV_K, n_cv), _slot)], name="gather_c_convw")
    c_all = c_all.reshape(N_DEV, D)
    conv_w_full = conv_w_full.transpose(1, 0, 2).reshape(CONV_K, CONV_DIM)

    win = _w_in_window(w_in[0].astype(BF16), me)
    late_weights = [
        (w_branch_gm[0].astype(BF16), _whole, (D, D), _rows(D // N_DEV)),
        (w_branch_ssm[0].astype(BF16), _whole, (SSM_INNER, D), _rows(SSM_INNER // N_DEV)),
        (w_out[0].astype(BF16), _whole, (D, D), _rows(D // N_DEV)),
        (w_ff1[0].astype(BF16), _whole, (D, D_FF), _cols(D_FF // N_DEV)),
        (w_ff2[0].astype(BF16), _whole, (D_FF, D), _rows(D_FF // N_DEV))]

    c_pad = _pad_rows(c_all, 128)
    b_mine = lax.dynamic_slice(b_mod, (0, me * n_mod), (1, n_mod))

    def mod_fn(cp, w, b):
        ca = cp * jax.nn.sigmoid(cp)
        return (jnp.dot(ca, w, precision=HIGHEST, preferred_element_type=F32) + b,)

    (mod_part,) = whole_call(mod_fn, [c_pad, w_mod[0], b_mine], [((128, n_mod), F32)], name="mod_fwd")
    gmod = gather_blocks(mod_part[:N_DEV], name="gather_mod")
    mod = lax.dynamic_index_in_dim(gmod, me, axis=1, keepdims=False).reshape(N_MOD, D)
    sh1, sc1, gt1, sh2, sc2, gt2 = [mod[i:i + 1] for i in range(N_MOD)]

    h, gwin = modulate_with_gather(x2, sc1, sh1, win, tm=256, name="modulate1_gather_w_in")
    w_all = _w_all_from_windows(gwin)
    w_dt = w_all[:, PROJ_W:]
    proj = matmul(h, w_all, "nn", BF16, name="mm_proj", n=PROJ_W, tm=K1_TM)
    dtg = matmul(h, w_dt, "nn", F32, name="mm_dt")
    ws = gm_ws[0]
    bs3 = gm_bs[0].reshape(GM_GROUPS, Q, 1)
    sgu_rows = [(proj, D, P_U), (proj, D, P_V)]
    (ya,) = rowwise_call(fwd_body(fn_sgu), sgu_rows, [gm_norm_w, ws, bs3], [(D, BF16)], [],
                         tm=FUSED_TM, name="sgu_fwd")
    xact = conv_fwd(proj, conv_w_full, conv_b)
    dtb4 = jnp.pad(dt_bias, ((0, 0), (0, 96)))
    alog4 = jnp.pad(a_log, ((0, 0), (0, 96)))
    dsk4 = d_skip.reshape(SSM_GROUPS, SSM_HPG, 1, 1)
    yb, states, w_gm_f, w_ssm_f, w_out_f, w_ff1_f, w_ff2_f = ssd_fwd(
        xact, proj, dtg, dtb4, alog4, dsk4, ssm_norm_w, late_weights)
    pa = matmul(ya, w_gm_f, "nn", BF16, name="mm_branch_gm", tm=K1_TM)
    gate_rows = [(proj, D, P_GA), (proj, D, P_GB)]
    mixed, pb = rowwise_call(
        lambda r, fl: (fn_mix(*r) + (r[3],), ()), gate_rows + [pa], [], [(D, BF16), (D, BF16)], [],
        tm=FUSED_TM, name="branch_ssm_mix", mm=(yb, w_ssm_f, "nn", 3, None), tk=SSM_INNER)
    x1, h2, o = rowwise_call(
        lambda r, fl: (fn_res_modulate(*r, *fl) + (r[1],), ()), [x2], [gt1, sc2, sh2],
        [(D, F32), (D, BF16), (D, F32)], [], tm=FUSED_TM, name="out_res_modulate2",
        mm=(mixed, w_out_f, "nn", 1, None))
    f = matmul(h2, w_ff1_f, "nn", BF16, name="mm_ff1", tm=K1_TM)

    dx1, dgf, loss_v, dgt2, dfnw = rowwise_call(
        final_body, [x1, tgt], [gt2, final_norm_w.reshape(1, D)], [(D, F32), (D, BF16)],
        [(1, 128), (1, D), (1, D)], tm=FUSED_TM, name="ff2_loss_bwd", mm=(f, w_ff2_f, "nn", 1, relu2_tile),
        tk=D_FF)
    df = matmul(dgf, w_ff2_f, "nt", BF16, name="mm_ff2_dgrad", epi=relu2_grad_tile, epi_ins=(f,), tm=K1_TM)
    gw_ff2 = matmul(f, dgf, "tn", BF16, name="mm_ff2_wgrad", tk=WGRAD_TK, a_pro=relu2_tile)
    gw_ff1 = matmul(h2, df, "tn", BF16, name="mm_ff1_wgrad", tk=WGRAD_TK)

    def res_mod_bwd(r, fl):
        xv, ov, dx1v, dh2v = r
        _, vjp = jax.vjp(fn_res_modulate, xv, ov, *fl)
        dxv, dov, dg1, dsc, dsh = vjp((dx1v, dh2v))
        return (dxv, dov), (dg1, dsc, dsh)

    dxa, do, dgt1, dsc2, dsh2 = rowwise_call(
        res_mod_bwd, [x2, o, dx1], [gt1, sc2, sh2], [(D, F32), (D, BF16)],
        [(1, D), (1, D), (1, D)], tm=FUSED_TM, name="ff1_dgrad_res_modulate2_bwd",
        mm=(df, w_ff1_f, "nt", 3, None), tk=D_FF)
    gw_out = matmul(mixed, do, "tn", BF16, name="mm_out_wgrad", tk=WGRAD_TK)
    dproj = lax.empty((T, ALL_W), BF16)

    def mix_bwd(r, fl):
        dga, dgb, dpa, dpb = bwd_body(fn_mix, 4)(r, fl)[0]
        return (jnp.concatenate([dga, dgb], axis=1), dpa, dpb), ()

    dproj, dpa, dpb = rowwise_call(
        mix_bwd, gate_rows + [pa, pb], [], [(dproj, 2 * D, P_GA), (D, BF16), (D, BF16)], [],
        tm=FUSED_TM, name="out_dgrad_mix_bwd", mm=(do, w_out_f, "nt", 4, None))
    gw_gm = matmul(ya, dpa, "tn", BF16, name="mm_branch_gm_wgrad", tk=WGRAD_TK)
    dyb = matmul(dpb, w_ssm_f, "nt", BF16, name="mm_branch_ssm_dgrad", tm=K1_TM)
    gw_ssm = matmul(yb, dpb, "tn", BF16, name="mm_branch_ssm_wgrad", tk=WGRAD_TK)

    def sgu_bwd(r, fl):
        (du, dv), acc = bwd_body(fn_sgu, 2)(r, fl)
        return (jnp.concatenate([du, dv], axis=1),), acc

    dproj, dgnw, dws, dbs = rowwise_call(
        sgu_bwd, sgu_rows, [gm_norm_w, ws, bs3], [(dproj, 2 * D, P_U)],
        [(1, D), (GM_GROUPS, Q, Q), (GM_GROUPS, Q, 1)], tm=FUSED_TM, name="branch_gm_dgrad_sgu_bwd",
        mm=(dpa, w_gm_f, "nt", 2, None))
    early_grads = [
        (gw_gm, _rows(D // N_DEV), (N_DEV, D // N_DEV, D), _slot),
        (gw_ssm, _rows(SSM_INNER // N_DEV), (N_DEV, SSM_INNER // N_DEV, D), _slot),
        (gw_out, _rows(D // N_DEV), (N_DEV, D // N_DEV, D), _slot),
        (gw_ff1, _cols(D_FF // N_DEV), (N_DEV, D, D_FF // N_DEV), _slot),
        (gw_ff2, _rows(D_FF // N_DEV), (N_DEV, D_FF // N_DEV, D), _slot),
        (_pack_rows([dgnw, dws, dbs], EARLY_ROWS), _whole, (N_DEV, sum(EARLY_ROWS), 128), _slot)]
    (dxs, dbm, dcm, dproj, ddt8, ddtb, dalog, ddsk, dsnw,
     r_gm, r_ssm, r_out, r_ff1, r_ff2, early_all) = ssd_bwd(
        xact, proj, dtg, dtb4, alog4, dsk4, ssm_norm_w, states, dyb, dproj, early_grads)
    dconv_w, dconv_b = [], []
    for nm, dact_part, col0 in (("xs", dxs, 0), ("b", dbm, SSM_INNER), ("c", dcm, SSM_INNER + SSM_GROUPS * 128)):
        dproj, dcw, dcb = conv_bwd(proj, dact_part, col0, conv_w_full, conv_b, dproj, name="conv_bwd_" + nm)
        dconv_w.append(dcw)
        dconv_b.append(dcb)
    dconv_w = jnp.concatenate(dconv_w, axis=1)
    dconv_b = jnp.concatenate(dconv_b, axis=1)
    (dproj,) = rowwise_call(
        lambda r, fl: ((functools.reduce(jnp.add, r),), ()),
        [(ddt8, 128, 128 * g) for g in range(SSM_GROUPS)], [], [(dproj, 128, PROJ_W)], [],
        tm=1024, name="ddt_into_dproj")
    gw_all = matmul(h, dproj, "tn", BF16, name="mm_in_wgrad", tn=1152, tk=WGRAD_TK)
    mid_pack = _pack_rows([dconv_w, dconv_b, jnp.sum(ddtb, axis=0), jnp.sum(dalog, axis=0), ddsk, dsnw, dfnw,
                           jnp.concatenate([dgt1, dsh2, dsc2, dgt2], axis=0)], MID_ROWS)
    dh, r_in, mid_all = matmul(
        dproj, w_all, "nt", BF16, name="mm_in_dgrad", tm=2048, tk=1152,
        comm=[(_windows_of_w_all(gw_all), _slot, (N_DEV, D, WIN_W), _slot),
              (mid_pack, _whole, (N_DEV, sum(MID_ROWS), 128), _slot)])
    grad_x, dsc1, dsh1 = rowwise_call(grad_x_body, [x2, dh, dxa], [sc1, sh1], [(D, F32)],
                                      [(1, D), (1, D)], tm=256, name="modulate1_bwd")

    g_w_in = _w_in_from_window(sum_devices(r_in, tr=256, name="sum_w_in_grads"), me).reshape(1, D, n_in)

    late_all = gather_blocks(_pack_rows([dsh1, dsc1, loss_v], LATE_ROWS), name="gather_dmod1_loss")
    s_early = _unpack_rows(sum_devices(early_all, tr=early_all.shape[1], name="sum_small_early"), EARLY_ROWS)
    s_mid = _unpack_rows(sum_devices(mid_all, tr=mid_all.shape[1], name="sum_small_mid"), MID_ROWS)
    s_late = _unpack_rows(sum_devices(late_all, tr=late_all.shape[1], name="sum_small_late"), LATE_ROWS)
    g_gm_norm_w = s_early[0][:D].reshape(1, D)
    g_gm_ws = s_early[1].reshape(GM_GROUPS * Q, Q)
    g_gm_bs = s_early[2][:GM_GROUPS * Q].reshape(GM_GROUPS, Q)
    g_conv_w_full = s_mid[0].reshape(CONV_K, CONV_DIM)
    g_conv_w = lax.dynamic_slice(g_conv_w_full, (0, me * n_cv), (CONV_K, n_cv))
    g_conv_b = s_mid[1].reshape(1, CONV_DIM)
    g_dt_bias = s_mid[2][:32].reshape(1, 32)
    g_a_log = s_mid[3][:32].reshape(1, 32)
    g_d_skip = s_mid[4][:32].reshape(1, 32)
    g_ssm_norm_w = s_mid[5].reshape(1, SSM_INNER)
    g_final_norm_w = s_mid[6][:D].reshape(1, D)
    g_b_mod = jnp.concatenate([s_late[0][:D], s_late[1][:D], s_mid[7]]).reshape(1, N_MOD * D)

    dmod_all = jnp.concatenate(
        [late_all.reshape(N_DEV, -1)[:, :2 * D],
         mid_all[:, sum(MID_ROWS[:7]):].reshape(N_DEV, 4 * D)], axis=1)
    dmod_mine = _pad_rows(lax.dynamic_slice(dmod_all, (0, me * n_mod), (N_DEV, n_mod)), 128)

    def wmod_grad_fn(cp, dm):
        ca = cp * jax.nn.sigmoid(cp)
        return (lax.dot_general(ca, dm, (((0,), (0,)), ((), ())), precision=HIGHEST,
                                preferred_element_type=F32),)

    (g_w_mod,) = whole_call(wmod_grad_fn, [c_pad, dmod_mine], [((D, n_mod), F32)], name="w_mod_grad")

    upd = {}

    def step(name, w, g, m, v, parts=False):
        upd[name] = adamw(w, g if parts else g.reshape(w.shape), m, v, name="adamw_" + name)

    step("w_mod", w_mod, g_w_mod, m_w_mod, v_w_mod)
    step("b_mod", b_mod, g_b_mod, m_b_mod, v_b_mod)
    col_major = lambda a: jnp.transpose(a, (2, 0, 1))
    upd["w_in"] = tuple(jnp.transpose(o, (1, 2, 0)) for o in adamw(
        col_major(w_in), col_major(g_w_in), col_major(m_w_in), col_major(v_w_in), name="adamw_w_in"))
    step("gm_norm_w", gm_norm_w, g_gm_norm_w, m_gm_norm_w, v_gm_norm_w)
    step("gm_ws", gm_ws, g_gm_ws, m_gm_ws, v_gm_ws)
    step("gm_bs", gm_bs, g_gm_bs, m_gm_bs, v_gm_bs)
    step("conv_w", conv_w, g_conv_w, m_conv_w, v_conv_w)
    step("conv_b", conv_b, g_conv_b, m_conv_b, v_conv_b)
    step("dt_bias", dt_bias, g_dt_bias, m_dt_bias, v_dt_bias)
    step("a_log", a_log, g_a_log, m_a_log, v_a_log)
    step("d_skip", d_skip, g_d_skip, m_d_skip, v_d_skip)
    step("ssm_norm_w", ssm_norm_w, g_ssm_norm_w, m_ssm_norm_w, v_ssm_norm_w)
    step("w_branch_gm", w_branch_gm, r_gm, m_w_branch_gm, v_w_branch_gm, parts=True)
    step("w_branch_ssm", w_branch_ssm, r_ssm, m_w_branch_ssm, v_w_branch_ssm, parts=True)
    step("w_out", w_out, r_out, m_w_out, v_w_out, parts=True)
    step("w_ff1", w_ff1, r_ff1, m_w_ff1, v_w_ff1, parts=True)
    step("w_ff2", w_ff2, r_ff2, m_w_ff2, v_w_ff2, parts=True)
    step("final_norm_w", final_norm_w.reshape(1, D), g_final_norm_w, m_final_norm_w.reshape(1, D),
         v_final_norm_w.reshape(1, D))
    upd["final_norm_w"] = tuple(a.reshape(D) for a in upd["final_norm_w"])

    loss = s_late[2][0]
    order = ["w_mod", "b_mod", "w_in", "gm_norm_w", "gm_ws", "gm_bs", "conv_w", "conv_b", "dt_bias", "a_log",
             "d_skip", "ssm_norm_w", "w_branch_gm", "w_branch_ssm", "w_out", "w_ff1", "w_ff2", "final_norm_w"]
    return (loss, grad_x.reshape(1, T, D),
            *[upd[n][0] for n in order], *[upd[n][1] for n in order],
            *[upd[n][2] for n in order], *[upd[n][3] for n in order])
```
